```python
import math
import jax
import jax.numpy as jnp
from jax import lax
import numpy as np

D_MODEL = 1024
BATCH = 2
SEQ = 8192
DEPTH = 1

MEM_LEN = 256
RMS_EPS = 1e-5

SSM_HEADS = 8
SSM_HEAD_DIM = 64
SSM_WIDTH = SSM_HEADS * SSM_HEAD_DIM
SSM_STATE = 128
SSM_GROUPS = 2
CONV_WIDTH = 4
CHUNK = 128
CONV_CH = SSM_WIDTH + 2 * SSM_GROUPS * SSM_STATE

ATTN_HEADS = 8
KV_HEADS = 2
Q_PER_KV = ATTN_HEADS // KV_HEADS
HEAD_DIM = 64
ATTN_WIDTH = ATTN_HEADS * HEAD_DIM
KV_WIDTH = KV_HEADS * HEAD_DIM
WINDOW = 128
BLOCK = WINDOW
QKV_COLS = ATTN_WIDTH + 2 * KV_WIDTH

MIX_WIDTH = SSM_WIDTH + ATTN_WIDTH
IN_COLS = SSM_WIDTH + CONV_CH + SSM_HEADS + QKV_COLS

REL_BUCKETS = 32
REL_MAX_DIST = 128

CROSS_HEADS = 4
CROSS_HEAD_DIM = 128
CROSS_WIDTH = CROSS_HEADS * CROSS_HEAD_DIM

N_EXPERTS = 32
TOP_K = 4
D_FF = D_MODEL
SWIGLU_ALPHA = 1.702
SWIGLU_LIMIT = 7.0
MOE_BLOCK = 128

kernel_name = "hybrid_ssd_swa_sink_t5_memxattn_moe"


def rmsnorm(x, g):
    xf = x.astype(jnp.float32)
    y = xf * lax.rsqrt(jnp.mean(xf * xf, axis=-1, keepdims=True) + RMS_EPS)
    return (y * g.astype(jnp.float32)).astype(x.dtype)


def causal_depthwise_conv(u, w, b):
    c = u.shape[-1]
    out = lax.conv_general_dilated(
        u, w[:, None, :].astype(u.dtype), window_strides=(1,),
        padding=[(CONV_WIDTH - 1, 0)], dimension_numbers=("NWC", "WIO", "NWC"),
        feature_group_count=c)
    return out + b.astype(u.dtype)


def segsum(a):
    t = a.shape[-1]
    cs = jnp.cumsum(a, axis=-1)
    diff = cs[..., :, None] - cs[..., None, :]
    mask = jnp.tril(jnp.ones((t, t), dtype=bool))
    return jnp.where(mask, diff, -jnp.inf)


def ssd_chunked(xh, dt, a_head, bm, cm):
    b, l, nh, p = xh.shape
    g, n = bm.shape[-2], bm.shape[-1]
    r = nh // g
    c = l // CHUNK
    xdt = (xh * dt[..., None]).reshape(b, c, CHUNK, g, r, p)
    a = (dt * a_head).reshape(b, c, CHUNK, g, r).transpose(0, 3, 4, 1, 2)
    bc = bm.reshape(b, c, CHUNK, g, n)
    cc = cm.reshape(b, c, CHUNK, g, n)
    a_cs = jnp.cumsum(a, axis=-1)
    decay = jnp.exp(segsum(a))
    cb = jnp.einsum("bclgn,bcsgn->bgcls", cc, bc)
    y_diag = jnp.einsum("bgrcls,bcsgrp->bclgrp", cb[:, :, None] * decay, xdt)
    to_end = jnp.exp(a_cs[..., -1:] - a_cs)
    states = jnp.einsum("bcsgn,bgrcs,bcsgrp->cbgrpn", bc, to_end, xdt)
    chunk_decay = jnp.exp(a_cs[..., -1]).transpose(3, 0, 1, 2)

    def step(h_state, inp):
        st, dec = inp
        return h_state * dec[..., None, None] + st, h_state

    h0 = jnp.zeros((b, g, r, p, n), jnp.float32)
    _, h_in = lax.scan(step, h0, (states, chunk_decay))
    y_off = jnp.einsum("bclgn,cbgrpn,bgrcl->bclgrp", cc, h_in, jnp.exp(a_cs))
    return (y_diag + y_off).reshape(b, l, nh, p)


def ssd_mixer(z, xbc, dt_raw, conv_w, conv_b, dt_bias, a_log, d_skip, g_ssm_out):
    b, l, _ = xbc.shape
    xbc = jax.nn.silu(causal_depthwise_conv(xbc, conv_w, conv_b))
    xs, bm, cm = jnp.split(xbc, [SSM_WIDTH, SSM_WIDTH + SSM_GROUPS * SSM_STATE], axis=-1)
    f32 = jnp.float32
    dt = jax.nn.softplus(dt_raw.astype(f32) + dt_bias.astype(f32))
    a_head = -jnp.exp(a_log.astype(f32))
    xh = xs.astype(f32).reshape(b, l, SSM_HEADS, SSM_HEAD_DIM)
    y = ssd_chunked(xh, dt, a_head,
                    bm.astype(f32).reshape(b, l, SSM_GROUPS, SSM_STATE),
                    cm.astype(f32).reshape(b, l, SSM_GROUPS, SSM_STATE))
    y = y + d_skip.astype(f32)[:, None] * xh
    y = y.reshape(b, l, SSM_WIDTH) * jax.nn.silu(z.astype(f32))
    yg = y.reshape(b, l, SSM_GROUPS, SSM_WIDTH // SSM_GROUPS)
    yg = yg * lax.rsqrt(jnp.mean(yg * yg, axis=-1, keepdims=True) + RMS_EPS)
    return (yg.reshape(b, l, SSM_WIDTH) * g_ssm_out.astype(f32)).astype(z.dtype)


def t5_bucket(dist):
    d = jnp.maximum(dist, 0)
    max_exact = REL_BUCKETS // 2
    ratio = jnp.maximum(d, max_exact).astype(jnp.float32) / max_exact
    large = max_exact + (jnp.log(ratio) / math.log(REL_MAX_DIST / max_exact)
                         * (REL_BUCKETS - max_exact)).astype(jnp.int32)
    large = jnp.minimum(large, REL_BUCKETS - 1)
    return jnp.where(d < max_exact, d, large)


def swa_mixer(qkv, sinks, rel_bias):
    b, l, _ = qkv.shape
    nb = l // BLOCK
    q, k, v = jnp.split(qkv, [ATTN_WIDTH, ATTN_WIDTH + KV_WIDTH], axis=-1)
    q = q.reshape(b, nb, BLOCK, KV_HEADS, Q_PER_KV, HEAD_DIM)
    k = k.reshape(b, nb, BLOCK, KV_HEADS, HEAD_DIM)
    v = v.reshape(b, nb, BLOCK, KV_HEADS, HEAD_DIM)

    def with_prev(t):
        prev = jnp.pad(t, ((0, 0), (1, 0), (0, 0), (0, 0), (0, 0)))[:, :-1]
        return jnp.concatenate([prev, t], axis=2)

    k2, v2 = with_prev(k), with_prev(v)
    s = jnp.einsum("bnqkgd,bnskd->bnkgqs", q, k2).astype(jnp.float32) * (HEAD_DIM ** -0.5)
    q_pos = jnp.arange(BLOCK)[:, None]
    k_pos = jnp.arange(2 * BLOCK)[None, :] - BLOCK
    dist = q_pos - k_pos
    bias = rel_bias.astype(jnp.float32)[t5_bucket(dist)]
    bias = bias.reshape(BLOCK, 2 * BLOCK, KV_HEADS, Q_PER_KV).transpose(2, 3, 0, 1)
    in_window = (dist >= 0) & (dist < WINDOW)
    k_abs = jnp.arange(nb)[:, None] * BLOCK + k_pos
    mask = in_window[None] & (k_abs >= 0)[:, None, :]
    s = jnp.where(mask[None, :, None, None], s + bias, -jnp.inf)
    sink = jnp.broadcast_to(
        sinks.astype(jnp.float32).reshape(KV_HEADS, Q_PER_KV)[None, None, :, :, None, None],
        s.shape[:-1] + (1,))
    p = jax.nn.softmax(jnp.concatenate([s, sink], axis=-1), axis=-1)[..., :-1]
    o = jnp.einsum("bnkgqs,bnskd->bnqkgd", p.astype(v.dtype), v2)
    return o.reshape(b, l, ATTN_WIDTH)


def hybrid_mixer(u, w_in, b_qkv, conv_w, conv_b, dt_bias, a_log, d_skip, g_ssm_out,
                 attn_sinks, rel_bias, w_out, b_out):
    proj = u @ w_in
    z, xbc, dt_raw, qkv = jnp.split(
        proj, [SSM_WIDTH, SSM_WIDTH + CONV_CH, SSM_WIDTH + CONV_CH + SSM_HEADS], axis=-1)
    y_ssm = ssd_mixer(z, xbc, dt_raw, conv_w, conv_b, dt_bias, a_log, d_skip, g_ssm_out)
    y_attn = swa_mixer(qkv + b_qkv.astype(qkv.dtype), attn_sinks, rel_bias).astype(y_ssm.dtype)
    return jnp.concatenate([y_ssm, y_attn], axis=-1) @ w_out + b_out


def cross_attention(u, m, w_q, w_kv, w_o):
    b, l, _ = u.shape
    n_mem = m.shape[1]
    q = (u @ w_q).reshape(b, l, CROSS_HEADS, CROSS_HEAD_DIM)
    k, v = jnp.split(m @ w_kv, 2, axis=-1)
    k = k.reshape(b, n_mem, CROSS_HEADS, CROSS_HEAD_DIM)
    v = v.reshape(b, n_mem, CROSS_HEADS, CROSS_HEAD_DIM)
    s = jnp.einsum("blhd,bmhd->bhlm", q, k).astype(jnp.float32) * (CROSS_HEAD_DIM ** -0.5)
    p = jax.nn.softmax(s, axis=-1).astype(v.dtype)
    o = jnp.einsum("bhlm,bmhd->blhd", p, v).reshape(b, l, CROSS_WIDTH)
    return o @ w_o


def clamped_swiglu(h):
    x_glu, x_lin = h[..., ::2], h[..., 1::2]
    x_glu = jnp.minimum(x_glu, SWIGLU_LIMIT)
    x_lin = jnp.clip(x_lin, -SWIGLU_LIMIT, SWIGLU_LIMIT)
    return x_glu * jax.nn.sigmoid(SWIGLU_ALPHA * x_glu) * (x_lin + 1.0)


def moe(u, w_router, b_router, w1, b1, w2, b2):
    b, l, d = u.shape
    n_tok = b * l
    n_assign = n_tok * TOP_K
    xt = u.reshape(n_tok, d)
    logits = xt @ w_router + b_router
    top_vals, top_idx = lax.top_k(logits, TOP_K)
    gates = jax.nn.softmax(top_vals.astype(jnp.float32), axis=-1).astype(xt.dtype)
    e_flat = top_idx.reshape(-1).astype(jnp.int32)
    tok_flat = jnp.arange(n_assign, dtype=jnp.int32) // TOP_K
    g_flat = gates.reshape(-1)
    counts = jnp.bincount(e_flat, length=N_EXPERTS).astype(jnp.int32)
    padded = (counts + MOE_BLOCK - 1) // MOE_BLOCK * MOE_BLOCK
    padded_end = jnp.cumsum(padded)
    padded_start = padded_end - padded
    count_start = jnp.cumsum(counts) - counts
    order = jnp.argsort(e_flat, stable=True)
    se = e_flat[order]
    rank = jnp.arange(n_assign, dtype=jnp.int32) - count_start[se]
    dest = padded_start[se] + rank
    n_blocks = (n_assign + MOE_BLOCK - 1) // MOE_BLOCK + N_EXPERTS
    n_rows = n_blocks * MOE_BLOCK
    row_tok = jnp.zeros((n_rows,), jnp.int32).at[dest].set(tok_flat[order])
    row_gate = jnp.zeros((n_rows,), xt.dtype).at[dest].set(g_flat[order])
    block_start = jnp.arange(n_blocks, dtype=jnp.int32) * MOE_BLOCK
    block_exp = jnp.minimum(jnp.searchsorted(padded_end, block_start, side="right"),
                            N_EXPERTS - 1).astype(jnp.int32)

    def expert_block(args):
        toks, e = args
        hdn = xt[toks] @ w1[e] + b1[e]
        return clamped_swiglu(hdn) @ w2[e] + b2[e]

    y_rows = lax.map(expert_block, (row_tok.reshape(n_blocks, MOE_BLOCK), block_exp))
    y_rows = y_rows.reshape(n_rows, d) * row_gate[:, None]
    out = jnp.zeros((n_tok, d), xt.dtype).at[row_tok].add(y_rows)
    return out.reshape(b, l, d)


def setup_inputs(seed: int = 0) -> dict:
    key = jax.random.key(seed)
    ks = iter(jax.random.split(key, 40))
    L = DEPTH
    f32 = jnp.float32

    def nrm(shape, scale):
        return jax.random.normal(next(ks), shape, f32) * scale

    def gain(shape):
        return 1.0 + nrm(shape, 0.05)

    dt0 = jnp.exp(jax.random.uniform(next(ks), (L, SSM_HEADS), f32)
                  * (math.log(0.1) - math.log(0.001)) + math.log(0.001))
    dt_bias = dt0 + jnp.log(-jnp.expm1(-dt0))
    a_log = jnp.log(jax.random.uniform(next(ks), (L, SSM_HEADS), f32, minval=1.0, maxval=16.0))
    return {
        "x": nrm((BATCH, SEQ, D_MODEL), 1.0),
        "mem": nrm((BATCH, MEM_LEN, D_MODEL), 1.0),
        "g_mix": gain((L, D_MODEL)),
        "w_in": nrm((L, D_MODEL, IN_COLS), D_MODEL ** -0.5),
        "b_qkv": nrm((L, QKV_COLS), 0.02),
        "conv_w": nrm((L, CONV_WIDTH, CONV_CH), CONV_WIDTH ** -0.5),
        "conv_b": nrm((L, CONV_CH), 0.02),
        "dt_bias": dt_bias,
        "a_log": a_log,
        "d_skip": gain((L, SSM_HEADS)),
        "g_ssm_out": gain((L, SSM_WIDTH)),
        "attn_sinks": nrm((L, ATTN_HEADS), 0.5),
        "rel_bias": nrm((REL_BUCKETS, ATTN_HEADS), 0.5),
        "w_out": nrm((L, MIX_WIDTH, D_MODEL), MIX_WIDTH ** -0.5),
        "b_out": nrm((L, D_MODEL), 0.02),
        "g_cross": gain((L, D_MODEL)),
        "g_mem": gain((L, D_MODEL)),
        "w_q_cross": nrm((L, D_MODEL, CROSS_WIDTH), D_MODEL ** -0.5),
        "w_kv_cross": nrm((L, D_MODEL, 2 * CROSS_WIDTH), D_MODEL ** -0.5),
        "w_o_cross": nrm((L, CROSS_WIDTH, D_MODEL), CROSS_WIDTH ** -0.5),
        "g_ffn": gain((L, D_MODEL)),
        "w_router": nrm((L, D_MODEL, N_EXPERTS), D_MODEL ** -0.5),
        "b_router": nrm((L, N_EXPERTS), 0.01),
        "w1": nrm((L, N_EXPERTS, D_MODEL, 2 * D_FF), D_MODEL ** -0.5),
        "b1": nrm((L, N_EXPERTS, 2 * D_FF), 0.01),
        "w2": nrm((L, N_EXPERTS, D_FF, D_MODEL), D_FF ** -0.5),
        "b2": nrm((L, N_EXPERTS, D_MODEL), 0.01),
        "g_final": gain((D_MODEL,)),
    }


def reference(x, mem, g_mix, w_in, b_qkv, conv_w, conv_b, dt_bias, a_log, d_skip, g_ssm_out,
              attn_sinks, rel_bias, w_out, b_out, g_cross, g_mem, w_q_cross, w_kv_cross,
              w_o_cross, g_ffn, w_router, b_router, w1, b1, w2, b2, g_final):
    h = x
    for i in range(DEPTH):
        h = h + hybrid_mixer(rmsnorm(h, g_mix[i]), w_in[i], b_qkv[i], conv_w[i], conv_b[i],
                             dt_bias[i], a_log[i], d_skip[i], g_ssm_out[i], attn_sinks[i],
                             rel_bias, w_out[i], b_out[i]).astype(h.dtype)
        h = h + cross_attention(rmsnorm(h, g_cross[i]), rmsnorm(mem, g_mem[i]),
                                w_q_cross[i], w_kv_cross[i], w_o_cross[i]).astype(h.dtype)
        h = h + moe(rmsnorm(h, g_ffn[i]), w_router[i], b_router[i], w1[i], b1[i],
                    w2[i], b2[i]).astype(h.dtype)
    return rmsnorm(h, g_final)
```

```python
import functools
import math

import numpy as np
import jax
import jax.numpy as jnp
from jax import lax
from jax.experimental import pallas as pl
from jax.experimental.pallas import tpu as pltpu

F32 = jnp.float32
BF16 = jnp.bfloat16
I32 = jnp.int32

D_MODEL = 1024
RMS_EPS = 1e-5

SSM_HEADS = 8
SSM_HEAD_DIM = 64
SSM_WIDTH = SSM_HEADS * SSM_HEAD_DIM
SSM_STATE = 128
SSM_GROUPS = 2
HEADS_PER_GROUP = SSM_HEADS // SSM_GROUPS
GROUP_WIDTH = SSM_WIDTH // SSM_GROUPS
CONV_WIDTH = 4
CHUNK = 128
CONV_CH = SSM_WIDTH + 2 * SSM_GROUPS * SSM_STATE

ATTN_HEADS = 8
KV_HEADS = 2
Q_PER_KV = ATTN_HEADS // KV_HEADS
HEAD_DIM = 64
ATTN_WIDTH = ATTN_HEADS * HEAD_DIM
KV_WIDTH = KV_HEADS * HEAD_DIM
WINDOW = 128
BLOCK = WINDOW
QKV_COLS = ATTN_WIDTH + 2 * KV_WIDTH

REL_BUCKETS = 32
REL_MAX_DIST = 128

CROSS_HEADS = 4
CROSS_HEAD_DIM = 128
CROSS_WIDTH = CROSS_HEADS * CROSS_HEAD_DIM

N_EXPERTS = 32
TOP_K = 4
D_FF = D_MODEL
SWIGLU_ALPHA = 1.702
SWIGLU_LIMIT = 7.0

LANES = 128
DT_PAD = LANES
PROJ_COLS = SSM_WIDTH + CONV_CH + DT_PAD + QKV_COLS
VMEM_LIMIT = 56 * 1024 * 1024

TM_PROJ = 512
TM_MID = 512
TM_SCATTER = 512
TM_FINAL = 256
MOE_ROWS = 256
DMA_UNROLL = 8


def _nt(a, b):
    return lax.dot_general(a, b, (((1,), (1,)), ((), ())), preferred_element_type=F32)


def _tn(a, b):
    return lax.dot_general(a, b, (((0,), (0,)), ((), ())), preferred_element_type=F32)


def _dot(a, b):
    return jnp.dot(a, b, preferred_element_type=F32)


def _dot_exact(a, b):
    return jnp.dot(a, b, preferred_element_type=F32, precision=lax.Precision.HIGHEST)


def _rms(x, g):
    return x * lax.rsqrt(jnp.mean(x * x, axis=-1, keepdims=True) + RMS_EPS) * g


def _sigmoid(x):
    return 1.0 / (1.0 + jnp.exp(-x))


def _params(sem=None):
    return pltpu.CompilerParams(dimension_semantics=sem, vmem_limit_bytes=VMEM_LIMIT)


def _inproj_kernel(x_ref, g_ref, w_ref, bq_ref, z_ref, xbc_ref, dt_ref, qkv_ref):
    u = _rms(x_ref[...], g_ref[...])
    p = _dot(u.astype(BF16), w_ref[...])
    z_ref[...] = p[:, :SSM_WIDTH]
    xbc_ref[...] = p[:, SSM_WIDTH:SSM_WIDTH + CONV_CH]
    dt_ref[...] = p[:, SSM_WIDTH + CONV_CH:SSM_WIDTH + CONV_CH + DT_PAD]
    qkv_ref[...] = p[:, SSM_WIDTH + CONV_CH + DT_PAD:] + bq_ref[...]


def _inproj(x2, g_mix, w_cat, b_qkv):
    n = x2.shape[0]
    row = lambda i: (i, 0)
    fixed = lambda i: (0, 0)
    return pl.pallas_call(
        _inproj_kernel,
        grid=(n // TM_PROJ,),
        in_specs=[
            pl.BlockSpec((TM_PROJ, D_MODEL), row),
            pl.BlockSpec((1, D_MODEL), fixed),
            pl.BlockSpec((D_MODEL, PROJ_COLS), fixed),
            pl.BlockSpec((1, QKV_COLS), fixed),
        ],
        out_specs=[
            pl.BlockSpec((TM_PROJ, SSM_WIDTH), row),
            pl.BlockSpec((TM_PROJ, CONV_CH), row),
            pl.BlockSpec((TM_PROJ, DT_PAD), row),
            pl.BlockSpec((TM_PROJ, QKV_COLS), row),
        ],
        out_shape=[
            jax.ShapeDtypeStruct((n, SSM_WIDTH), F32),
            jax.ShapeDtypeStruct((n, CONV_CH), F32),
            jax.ShapeDtypeStruct((n, DT_PAD), F32),
            jax.ShapeDtypeStruct((n, QKV_COLS), F32),
        ],
        compiler_params=_params(("arbitrary",)),
    )(x2, g_mix, w_cat, b_qkv)


def _ssd_kernel(xbc_ref, z_ref, dt_ref, cw_ref, cb_ref, dtb_ref, ah_ref, dsk_ref, gout_ref,
                e64_ref, e128_ref, tri_ref, y_ref, prev_ref, state_ref):
    c = pl.program_id(1)

    @pl.when(c == 0)
    def _():
        prev_ref[...] = jnp.zeros_like(prev_ref)
        state_ref[...] = jnp.zeros_like(state_ref)

    u = xbc_ref[...]
    pv = prev_ref[...]
    row = lax.broadcasted_iota(I32, (CHUNK, 1), 0)
    acc = u * cw_ref[CONV_WIDTH - 1:CONV_WIDTH, :] + cb_ref[...]
    for j in range(1, CONV_WIDTH):
        shifted = jnp.where(row >= j, pltpu.roll(u, j, 0), pltpu.roll(pv, j, 0))
        acc = acc + shifted * cw_ref[CONV_WIDTH - 1 - j:CONV_WIDTH - j, :]
    prev_ref[...] = u
    xbc = acc * _sigmoid(acc)

    xs = xbc[:, :SSM_WIDTH]
    bm = xbc[:, SSM_WIDTH:SSM_WIDTH + SSM_GROUPS * SSM_STATE]
    cm = xbc[:, SSM_WIDTH + SSM_GROUPS * SSM_STATE:]

    dtr = dt_ref[...] + dtb_ref[...]
    dt = jnp.maximum(dtr, 0.0) + jnp.log1p(jnp.exp(-jnp.abs(dtr)))
    a = dt * ah_ref[...]
    cs = _dot_exact(tri_ref[...], a)
    cs_row = cs.T
    dt_full = _dot_exact(dt, e64_ref[...])
    cs_full = _dot_exact(cs, e64_ref[...])
    cs_wide = _dot_exact(cs, e128_ref[...])
    cs_last = cs_full[CHUNK - 1:CHUNK, :]

    xdt = xs * dt_full
    xdt_b = xdt.astype(BF16)
    xw_b = (xdt * jnp.exp(cs_last - cs_full)).astype(BF16)
    in_decay = jnp.exp(cs_full)
    chunk_decay = jnp.exp(cs_last)

    li = lax.broadcasted_iota(I32, (CHUNK, CHUNK), 0)
    si = lax.broadcasted_iota(I32, (CHUNK, CHUNK), 1)
    causal = li >= si

    ys = []
    for g in range(SSM_GROUPS):
        bg = bm[:, g * SSM_STATE:(g + 1) * SSM_STATE].astype(BF16)
        cg = cm[:, g * SSM_STATE:(g + 1) * SSM_STATE].astype(BF16)
        cb = _nt(cg, bg)
        yd = []
        for r in range(HEADS_PER_GROUP):
            h = g * HEADS_PER_GROUP + r
            diff = cs_wide[:, h * CHUNK:(h + 1) * CHUNK] - cs_row[h:h + 1, :]
            m = cb * jnp.exp(jnp.where(causal, diff, -jnp.inf))
            yd.append(_dot(m.astype(BF16), xdt_b[:, h * SSM_HEAD_DIM:(h + 1) * SSM_HEAD_DIM]))
        y_diag = jnp.concatenate(yd, axis=1)
        gs = slice(g * GROUP_WIDTH, (g + 1) * GROUP_WIDTH)
        st = state_ref[g]
        y_off = _dot(cg, st.astype(BF16)) * in_decay[:, gs]
        state_ref[g] = st * chunk_decay[:, gs] + _tn(bg, xw_b[:, gs])
        ys.append(y_diag + y_off)
    y = jnp.concatenate(ys, axis=1) + dsk_ref[...] * xs

    zz = z_ref[...]
    y = y * (zz * _sigmoid(zz))
    outs = []
    for g in range(SSM_GROUPS):
        yg = y[:, g * GROUP_WIDTH:(g + 1) * GROUP_WIDTH]
        outs.append(yg * lax.rsqrt(jnp.mean(yg * yg, axis=-1, keepdims=True) + RMS_EPS))
    y_ref[...] = jnp.concatenate(outs, axis=1) * gout_ref[...]


def _ssd(xbc, z, dtp, conv_w, conv_b, dt_bias_p, a_head_p, d_full, g_out, batch, seq):
    n = xbc.shape[0]
    nc = seq // CHUNK
    row = lambda b, c: (b * nc + c, 0)
    fixed = lambda b, c: (0, 0)
    head = np.arange(SSM_WIDTH) // SSM_HEAD_DIM
    e64 = jnp.asarray((np.arange(DT_PAD)[:, None] == head[None, :]).astype(np.float32))
    head_w = np.arange(SSM_HEADS * CHUNK) // CHUNK
    e128 = jnp.asarray((np.arange(DT_PAD)[:, None] == head_w[None, :]).astype(np.float32))
    tri = jnp.asarray(np.tril(np.ones((CHUNK, CHUNK), np.float32)))
    return pl.pallas_call(
        _ssd_kernel,
        grid=(batch, nc),
        in_specs=[
            pl.BlockSpec((CHUNK, CONV_CH), row),
            pl.BlockSpec((CHUNK, SSM_WIDTH), row),
            pl.BlockSpec((CHUNK, DT_PAD), row),
            pl.BlockSpec((CONV_WIDTH, CONV_CH), fixed),
            pl.BlockSpec((1, CONV_CH), fixed),
            pl.BlockSpec((1, DT_PAD), fixed),
            pl.BlockSpec((1, DT_PAD), fixed),
            pl.BlockSpec((1, SSM_WIDTH), fixed),
            pl.BlockSpec((1, SSM_WIDTH), fixed),
            pl.BlockSpec((DT_PAD, SSM_WIDTH), fixed),
            pl.BlockSpec((DT_PAD, SSM_HEADS * CHUNK), fixed),
            pl.BlockSpec((CHUNK, CHUNK), fixed),
        ],
        out_specs=pl.BlockSpec((CHUNK, SSM_WIDTH), row),
        out_shape=jax.ShapeDtypeStruct((n, SSM_WIDTH), F32),
        scratch_shapes=[
            pltpu.VMEM((CHUNK, CONV_CH), F32),
            pltpu.VMEM((SSM_GROUPS, SSM_STATE, GROUP_WIDTH), F32),
        ],
        compiler_params=_params(("arbitrary", "arbitrary")),
    )(xbc, z, dtp, conv_w, conv_b, dt_bias_p, a_head_p, d_full, g_out, e64, e128, tri)


def _t5_bucket_table():
    q_pos = np.arange(BLOCK)[:, None]
    k_pos = np.arange(2 * BLOCK)[None, :] - BLOCK
    dist = q_pos - k_pos
    d = np.maximum(dist, 0)
    max_exact = REL_BUCKETS // 2
    ratio = np.maximum(d, max_exact).astype(np.float32) / np.float32(max_exact)
    large = max_exact + (np.log(ratio) / np.float32(math.log(REL_MAX_DIST / max_exact))
                         * np.float32(REL_BUCKETS - max_exact)).astype(np.int32)
    large = np.minimum(large, REL_BUCKETS - 1)
    bucket = np.where(d < max_exact, d, large)
    in_window = (dist >= 0) & (dist < WINDOW)
    return np.where(in_window, bucket, -1).astype(np.int32)


def _bias_kernel(rb_ref, bucket_ref, out_ref):
    bucket = bucket_ref[...]
    for h in range(ATTN_HEADS):
        acc = jnp.zeros(bucket.shape, F32)
        for b in range(REL_BUCKETS):
            acc = jnp.where(bucket == b, rb_ref[b, h], acc)
        out_ref[h] = jnp.where(bucket >= 0, acc, -jnp.inf)


def _bias_table(rel_bias):
    bucket = jnp.asarray(_t5_bucket_table())
    return pl.pallas_call(
        _bias_kernel,
        in_specs=[
            pl.BlockSpec(memory_space=pltpu.SMEM),
            pl.BlockSpec(memory_space=pltpu.VMEM),
        ],
        out_specs=pl.BlockSpec(memory_space=pltpu.VMEM),
        out_shape=jax.ShapeDtypeStruct((ATTN_HEADS, BLOCK, 2 * BLOCK), F32),
    )(rel_bias, bucket)


def _swa_kernel(sink_ref, q_ref, k_ref, v_ref, kp_ref, vp_ref, bias_ref, o_ref):
    i = pl.program_id(1)
    q = (q_ref[...] * (HEAD_DIM ** -0.5)).astype(BF16)
    col = lax.broadcasted_iota(I32, (BLOCK, 2 * BLOCK), 1)
    key_ok = jnp.logical_or(col >= BLOCK, i > 0)
    outs = []
    for g in range(KV_HEADS):
        gs = slice(g * HEAD_DIM, (g + 1) * HEAD_DIM)
        k2 = jnp.concatenate([kp_ref[:, gs], k_ref[:, gs]], axis=0).astype(BF16)
        v2 = jnp.concatenate([vp_ref[:, gs], v_ref[:, gs]], axis=0).astype(BF16)
        for r in range(Q_PER_KV):
            h = g * Q_PER_KV + r
            s = _nt(q[:, h * HEAD_DIM:(h + 1) * HEAD_DIM], k2) + bias_ref[h]
            s = jnp.where(key_ok, s, -jnp.inf)
            sink = sink_ref[h]
            m = jnp.maximum(jnp.max(s, axis=-1, keepdims=True), sink)
            p = jnp.exp(s - m)
            denom = jnp.sum(p, axis=-1, keepdims=True) + jnp.exp(sink - m)
            outs.append(_dot(p.astype(BF16), v2) / denom)
    o_ref[...] = jnp.concatenate(outs, axis=1)


def _swa(qkv, sinks, bias, batch, seq):
    n = qkv.shape[0]
    nb = seq // BLOCK
    kcol = ATTN_WIDTH // KV_WIDTH
    return pl.pallas_call(
        _swa_kernel,
        grid_spec=pltpu.PrefetchScalarGridSpec(
            num_scalar_prefetch=1,
            grid=(batch, nb),
            in_specs=[
                pl.BlockSpec((BLOCK, ATTN_WIDTH), lambda b, i, s: (b * nb + i, 0)),
                pl.BlockSpec((BLOCK, KV_WIDTH), lambda b, i, s: (b * nb + i, kcol)),
                pl.BlockSpec((BLOCK, KV_WIDTH), lambda b, i, s: (b * nb + i, kcol + 1)),
                pl.BlockSpec((BLOCK, KV_WIDTH), lambda b, i, s: (b * nb + jnp.maximum(i - 1, 0), kcol)),
                pl.BlockSpec((BLOCK, KV_WIDTH), lambda b, i, s: (b * nb + jnp.maximum(i - 1, 0), kcol + 1)),
                pl.BlockSpec((ATTN_HEADS, BLOCK, 2 * BLOCK), lambda b, i, s: (0, 0, 0)),
            ],
            out_specs=pl.BlockSpec((BLOCK, ATTN_WIDTH), lambda b, i, s: (b * nb + i, 0)),
        ),
        out_shape=jax.ShapeDtypeStruct((n, ATTN_WIDTH), F32),
        compiler_params=_params(("arbitrary", "arbitrary")),
    )(sinks, qkv, qkv, qkv, qkv, qkv, bias)


def _memkv_kernel(m_ref, g_ref, w_ref, k_ref, v_ref):
    u = _rms(m_ref[...], g_ref[...])
    kv = _dot(u.astype(BF16), w_ref[...])
    k_ref[...] = kv[:, :CROSS_WIDTH].astype(BF16)
    v_ref[...] = kv[:, CROSS_WIDTH:].astype(BF16)


def _memkv(mem2, g_mem, w_kv):
    n = mem2.shape[0]
    return pl.pallas_call(
        _memkv_kernel,
        out_shape=[jax.ShapeDtypeStruct((n, CROSS_WIDTH), BF16)] * 2,
        compiler_params=_params(),
    )(mem2, g_mem, w_kv)


def _mid_kernel(ys_ref, ya_ref, x_ref, wout_ref, bout_ref, gc_ref, wq_ref, k_ref, v_ref, wo_ref,
                gf_ref, wr_ref, br_ref, upper_ref,
                h2_ref, u3_ref, idx_ref, gate_ref, rank_ref, cnt_ref, run_ref):
    i = pl.program_id(0)

    @pl.when(i == 0)
    def _():
        run_ref[...] = jnp.zeros_like(run_ref)

    ycat = jnp.concatenate([ys_ref[...], ya_ref[...]], axis=1).astype(BF16)
    h1 = x_ref[...] + _dot(ycat, wout_ref[...]) + bout_ref[...]

    u2 = _rms(h1, gc_ref[...])
    q = _dot(u2.astype(BF16), wq_ref[...]).astype(BF16)
    outs = []
    for h in range(CROSS_HEADS):
        hs = slice(h * CROSS_HEAD_DIM, (h + 1) * CROSS_HEAD_DIM)
        s = _nt(q[:, hs], k_ref[:, hs]) * (CROSS_HEAD_DIM ** -0.5)
        m = jnp.max(s, axis=-1, keepdims=True)
        p = jnp.exp(s - m)
        outs.append(_dot(p.astype(BF16), v_ref[:, hs]) / jnp.sum(p, axis=-1, keepdims=True))
    o = jnp.concatenate(outs, axis=1).astype(BF16)
    h2 = h1 + _dot(o, wo_ref[...])
    h2_ref[...] = h2

    u3 = _rms(h2, gf_ref[...])
    u3_ref[...] = u3

    logits = _nt(wr_ref[...], u3.astype(BF16)) + br_ref[...]
    eid = lax.broadcasted_iota(I32, logits.shape, 0)
    vals, idxs, hots = [], [], []
    for _ in range(TOP_K):
        m = jnp.max(logits, axis=0, keepdims=True)
        ix = jnp.min(jnp.where(logits == m, eid, N_EXPERTS), axis=0, keepdims=True)
        hot = eid == ix
        logits = jnp.where(hot, -jnp.inf, logits)
        vals.append(m)
        idxs.append(ix)
        hots.append(hot)
    ex = [jnp.exp(v - vals[0]) for v in vals]
    tot = ex[0] + ex[1] + ex[2] + ex[3]
    gate_ref[...] = jnp.concatenate([e / tot for e in ex], axis=0)
    idx_ref[...] = jnp.concatenate(idxs, axis=0)

    base = run_ref[...]
    ranks = []
    for kk in range(TOP_K):
        hot_f = jnp.where(hots[kk], 1.0, 0.0)
        before = _dot(hot_f.astype(BF16), upper_ref[...])
        ranks.append(jnp.sum(jnp.where(hots[kk], before + base, 0.0), axis=0, keepdims=True))
        base = base + jnp.sum(hot_f, axis=1, keepdims=True)
    rank_ref[...] = jnp.concatenate(ranks, axis=0).astype(I32)
    run_ref[...] = base
    cnt_ref[...] = jnp.broadcast_to(base, cnt_ref.shape)


def _mid(y_ssm, y_attn, x2, w_out, b_out, g_cross, w_q, k_mem, v_mem, w_o, g_ffn, w_r_t, b_r, seq, mem_len):
    n = x2.shape[0]
    tiles_per_batch = seq // TM_MID
    row = lambda i: (i, 0)
    col = lambda i: (0, i)
    fixed = lambda i: (0, 0)
    memb = lambda i: (i // tiles_per_batch, 0)
    upper = jnp.asarray(np.triu(np.ones((TM_MID, TM_MID), np.float32), 1)).astype(BF16)
    return pl.pallas_call(
        _mid_kernel,
        grid=(n // TM_MID,),
        in_specs=[
            pl.BlockSpec((TM_MID, SSM_WIDTH), row),
            pl.BlockSpec((TM_MID, ATTN_WIDTH), row),
            pl.BlockSpec((TM_MID, D_MODEL), row),
            pl.BlockSpec((D_MODEL, D_MODEL), fixed),
            pl.BlockSpec((1, D_MODEL), fixed),
            pl.BlockSpec((1, D_MODEL), fixed),
            pl.BlockSpec((D_MODEL, CROSS_WIDTH), fixed),
            pl.BlockSpec((mem_len, CROSS_WIDTH), memb),
            pl.BlockSpec((mem_len, CROSS_WIDTH), memb),
            pl.BlockSpec((CROSS_WIDTH, D_MODEL), fixed),
            pl.BlockSpec((1, D_MODEL), fixed),
            pl.BlockSpec((N_EXPERTS, D_MODEL), fixed),
            pl.BlockSpec((N_EXPERTS, 1), fixed),
            pl.BlockSpec((TM_MID, TM_MID), fixed),
        ],
        out_specs=[
            pl.BlockSpec((TM_MID, D_MODEL), row),
            pl.BlockSpec((TM_MID, D_MODEL), row),
            pl.BlockSpec((TOP_K, TM_MID), col),
            pl.BlockSpec((TOP_K, TM_MID), col),
            pl.BlockSpec((TOP_K, TM_MID), col),
            pl.BlockSpec((N_EXPERTS, LANES), fixed),
        ],
        out_shape=[
            jax.ShapeDtypeStruct((n, D_MODEL), F32),
            jax.ShapeDtypeStruct((n, D_MODEL), F32),
            jax.ShapeDtypeStruct((TOP_K, n), I32),
            jax.ShapeDtypeStruct((TOP_K, n), F32),
            jax.ShapeDtypeStruct((TOP_K, n), I32),
            jax.ShapeDtypeStruct((N_EXPERTS, LANES), F32),
        ],
        scratch_shapes=[pltpu.VMEM((N_EXPERTS, 1), F32)],
        compiler_params=_params(("arbitrary",)),
    )(y_ssm, y_attn, x2, w_out, b_out, g_cross, w_q, k_mem, v_mem, w_o, g_ffn, w_r_t, b_r, upper)


def _pos_kernel(start_ref, idx_ref, rank_ref, pos_ref):
    idx = idx_ref[...]
    acc = rank_ref[...]
    for e in range(N_EXPERTS):
        acc = acc + jnp.where(idx == e, start_ref[e], 0)
    pos_ref[...] = acc


def _positions(start, idx, rank):
    return pl.pallas_call(
        _pos_kernel,
        in_specs=[
            pl.BlockSpec(memory_space=pltpu.SMEM),
            pl.BlockSpec(memory_space=pltpu.VMEM),
            pl.BlockSpec(memory_space=pltpu.VMEM),
        ],
        out_specs=pl.BlockSpec(memory_space=pltpu.VMEM),
        out_shape=jax.ShapeDtypeStruct(idx.shape, I32),
    )(start, idx, rank)


def _row_copy(src_ref, src_row, dst_ref, dst_row, sem):
    return pltpu.make_async_copy(src_ref.at[pl.ds(src_row, 1)], dst_ref.at[pl.ds(dst_row, 1)], sem)


def _scatter_kernel(n_tok, pos_ref, u_ref, xs_in_ref, xs_ref, sem):
    del xs_in_ref
    base = pl.program_id(0) * TM_SCATTER

    def issue(j, carry):
        for jj in range(DMA_UNROLL):
            t = j * DMA_UNROLL + jj
            for kk in range(TOP_K):
                _row_copy(u_ref, t, xs_ref, pos_ref[kk * n_tok + base + t], sem).start()
        return carry

    def drain(j, carry):
        for jj in range(DMA_UNROLL):
            for kk in range(TOP_K):
                _row_copy(u_ref, 0, xs_ref, 0, sem).wait()
        return carry

    lax.fori_loop(0, TM_SCATTER // DMA_UNROLL, issue, 0)
    lax.fori_loop(0, TM_SCATTER // DMA_UNROLL, drain, 0)


def _scatter_rows(pos_flat, u3, xs_init):
    n = u3.shape[0]
    return pl.pallas_call(
        functools.partial(_scatter_kernel, n),
        grid_spec=pltpu.PrefetchScalarGridSpec(
            num_scalar_prefetch=1,
            grid=(n // TM_SCATTER,),
            in_specs=[
                pl.BlockSpec((TM_SCATTER, D_MODEL), lambda i, p: (i, 0)),
                pl.BlockSpec(memory_space=pl.ANY),
            ],
            out_specs=pl.BlockSpec(memory_space=pl.ANY),
            scratch_shapes=[pltpu.SemaphoreType.DMA],
        ),
        out_shape=jax.ShapeDtypeStruct(xs_init.shape, xs_init.dtype),
        input_output_aliases={2: 0},
        compiler_params=_params(("arbitrary",)),
    )(pos_flat, u3, xs_init)


def _expert_kernel(be_ref, na_ref, x_ref, w1g_ref, w1l_ref, b1g_ref, b1l_ref, w2_ref, b2_ref, y_ref):
    b = pl.program_id(0)

    @pl.when(b < na_ref[0])
    def _():
        x = x_ref[...].astype(BF16)
        hg = _dot(x, w1g_ref[0]) + b1g_ref[0]
        hl = _dot(x, w1l_ref[0]) + b1l_ref[0]
        hg = jnp.minimum(hg, SWIGLU_LIMIT)
        hl = jnp.clip(hl, -SWIGLU_LIMIT, SWIGLU_LIMIT)
        act = hg * _sigmoid(SWIGLU_ALPHA * hg) * (hl + 1.0)
        y_ref[...] = _dot(act.astype(BF16), w2_ref[0]) + b2_ref[0]

    @pl.when(b >= na_ref[0])
    def _():
        y_ref[...] = jnp.zeros_like(y_ref)


def _experts(block_exp, n_active, xs, w1g, w1l, b1g, b1l, w2, b2):
    n_rows = xs.shape[0]
    n_blocks = n_rows // MOE_ROWS
    xmap = lambda b, be, na: (jnp.minimum(b, na[0] - 1), 0)
    emap = lambda b, be, na: (be[b], 0, 0)
    return pl.pallas_call(
        _expert_kernel,
        grid_spec=pltpu.PrefetchScalarGridSpec(
            num_scalar_prefetch=2,
            grid=(n_blocks,),
            in_specs=[
                pl.BlockSpec((MOE_ROWS, D_MODEL), xmap),
                pl.BlockSpec((1, D_MODEL, D_FF), emap),
                pl.BlockSpec((1, D_MODEL, D_FF), emap),
                pl.BlockSpec((1, 1, D_FF), emap),
                pl.BlockSpec((1, 1, D_FF), emap),
                pl.BlockSpec((1, D_FF, D_MODEL), emap),
                pl.BlockSpec((1, 1, D_MODEL), emap),
            ],
            out_specs=pl.BlockSpec((MOE_ROWS, D_MODEL), lambda b, be, na: (b, 0)),
        ),
        out_shape=jax.ShapeDtypeStruct((n_rows, D_MODEL), F32),
        compiler_params=_params(("arbitrary",)),
    )(block_exp, n_active, xs, w1g, w1l, b1g, b1l, w2, b2)


def _final_kernel(n_tok, pos_ref, h2_ref, gate_ref, g_ref, ys_ref, o_ref, buf_ref, sem):
    base = pl.program_id(0) * TM_FINAL

    def issue(j, carry):
        for jj in range(DMA_UNROLL):
            t = j * DMA_UNROLL + jj
            for kk in range(TOP_K):
                _row_copy(ys_ref, pos_ref[kk * n_tok + base + t], buf_ref.at[kk], t, sem).start()
        return carry

    def drain(j, carry):
        for jj in range(DMA_UNROLL):
            for kk in range(TOP_K):
                _row_copy(ys_ref, 0, buf_ref.at[kk], 0, sem).wait()
        return carry

    lax.fori_loop(0, TM_FINAL // DMA_UNROLL, issue, 0)
    lax.fori_loop(0, TM_FINAL // DMA_UNROLL, drain, 0)

    gates = gate_ref[...].T
    h = h2_ref[...]
    for kk in range(TOP_K):
        h = h + buf_ref[kk] * gates[:, kk:kk + 1]
    o_ref[...] = _rms(h, g_ref[...])


def _combine(pos_flat, h2, gates, g_final, ys):
    n = h2.shape[0]
    return pl.pallas_call(
        functools.partial(_final_kernel, n),
        grid_spec=pltpu.PrefetchScalarGridSpec(
            num_scalar_prefetch=1,
            grid=(n // TM_FINAL,),
            in_specs=[
                pl.BlockSpec((TM_FINAL, D_MODEL), lambda i, p: (i, 0)),
                pl.BlockSpec((TOP_K, TM_FINAL), lambda i, p: (0, i)),
                pl.BlockSpec((1, D_MODEL), lambda i, p: (0, 0)),
                pl.BlockSpec(memory_space=pl.ANY),
            ],
            out_specs=pl.BlockSpec((TM_FINAL, D_MODEL), lambda i, p: (i, 0)),
            scratch_shapes=[
                pltpu.VMEM((TOP_K, TM_FINAL, D_MODEL), F32),
                pltpu.SemaphoreType.DMA,
            ],
        ),
        out_shape=jax.ShapeDtypeStruct((n, D_MODEL), F32),
        compiler_params=_params(("arbitrary",)),
    )(pos_flat, h2, gates, g_final, ys)


def _layer(h, mem2, batch, seq, mem_len, g_mix, w_in, b_qkv, conv_w, conv_b, dt_bias, a_log, d_skip,
           g_ssm_out, attn_sinks, bias, w_out, b_out, g_cross, g_mem, w_q_cross, w_kv_cross, w_o_cross,
           g_ffn, w_router, b_router, w1, b1, w2, b2):
    n = h.shape[0]
    c0, c1, c2 = SSM_WIDTH, SSM_WIDTH + CONV_CH, SSM_WIDTH + CONV_CH + SSM_HEADS
    w_cat = jnp.concatenate(
        [w_in[:, :c1], jnp.pad(w_in[:, c1:c2], ((0, 0), (0, DT_PAD - SSM_HEADS))), w_in[:, c2:]],
        axis=1).astype(BF16)
    z, xbc, dtp, qkv = _inproj(h, g_mix[None, :], w_cat, b_qkv[None, :])

    pad_h = (0, DT_PAD - SSM_HEADS)
    dt_bias_p = jnp.pad(dt_bias, pad_h)[None, :]
    a_head_p = jnp.pad(-jnp.exp(a_log), pad_h)[None, :]
    d_full = jnp.repeat(d_skip, SSM_HEAD_DIM)[None, :]
    y_ssm = _ssd(xbc, z, dtp, conv_w, conv_b[None, :], dt_bias_p, a_head_p, d_full,
                 g_ssm_out[None, :], batch, seq)
    y_attn = _swa(qkv, attn_sinks, bias, batch, seq)

    k_mem, v_mem = _memkv(mem2, g_mem[None, :], w_kv_cross.astype(BF16))
    h2, u3, idx, gates, rank, cnt = _mid(
        y_ssm, y_attn, h, w_out.astype(BF16), b_out[None, :], g_cross[None, :],
        w_q_cross.astype(BF16), k_mem, v_mem, w_o_cross.astype(BF16), g_ffn[None, :],
        w_router.T.astype(BF16), b_router[:, None], seq, mem_len)

    n_blocks = (n * TOP_K) // MOE_ROWS + N_EXPERTS
    counts = cnt[:, 0].astype(I32)
    padded = (counts + MOE_ROWS - 1) // MOE_ROWS * MOE_ROWS
    padded_end = jnp.cumsum(padded)
    start = (padded_end - padded).astype(I32)
    n_active = (padded_end[-1] // MOE_ROWS).astype(I32)
    blk = jnp.arange(n_blocks, dtype=I32)
    blk_exp = jnp.minimum(jnp.searchsorted(padded_end, blk * MOE_ROWS, side="right"),
                          N_EXPERTS - 1).astype(I32)
    blk_exp = jnp.where(blk < n_active, blk_exp, blk_exp[n_active - 1])

    pos_flat = _positions(start, idx, rank).reshape(-1)
    xs = _scatter_rows(pos_flat, u3, jnp.zeros((n_blocks * MOE_ROWS, D_MODEL), F32))

    w1r = w1.reshape(N_EXPERTS, D_MODEL, D_FF, 2)
    b1r = b1.reshape(N_EXPERTS, 1, D_FF, 2)
    ys = _experts(blk_exp, n_active[None], xs, w1r[..., 0].astype(BF16), w1r[..., 1].astype(BF16),
                  b1r[..., 0], b1r[..., 1], w2.astype(BF16), b2[:, None, :])
    return pos_flat, h2, gates, ys


def kernel(x, mem, g_mix, w_in, b_qkv, conv_w, conv_b, dt_bias, a_log, d_skip, g_ssm_out, attn_sinks, rel_bias, w_out, b_out, g_cross, g_mem, w_q_cross, w_kv_cross, w_o_cross, g_ffn, w_router, b_router, w1, b1, w2, b2, g_final):
    batch, seq, d = x.shape
    mem_len = mem.shape[1]
    depth = g_mix.shape[0]
    assert depth == 1 and d == D_MODEL and seq % TM_MID == 0
    h = x.reshape(batch * seq, d)
    mem2 = mem.reshape(batch * mem_len, d)
    bias = _bias_table(rel_bias)
    pos_flat, h2, gates, ys = _layer(
        h, mem2, batch, seq, mem_len, g_mix[0], w_in[0], b_qkv[0], conv_w[0], conv_b[0], dt_bias[0],
        a_log[0], d_skip[0], g_ssm_out[0], attn_sinks[0], bias, w_out[0], b_out[0], g_cross[0],
        g_mem[0], w_q_cross[0], w_kv_cross[0], w_o_cross[0], g_ffn[0], w_router[0], b_router[0],
        w1[0], b1[0], w2[0], b2[0])
    out = _combine(pos_flat, h2, gates, g_final[None, :], ys)
    return out.reshape(batch, seq, d)
```

```python
import functools
import math

import numpy as np
import jax
import jax.numpy as jnp
from jax import lax
from jax.experimental import pallas as pl
from jax.experimental.pallas import tpu as pltpu

F32 = jnp.float32
BF16 = jnp.bfloat16
I32 = jnp.int32

D_MODEL = 1024
RMS_EPS = 1e-5

SSM_HEADS = 8
SSM_HEAD_DIM = 64
SSM_WIDTH = SSM_HEADS * SSM_HEAD_DIM
SSM_STATE = 128
SSM_GROUPS = 2
HEADS_PER_GROUP = SSM_HEADS // SSM_GROUPS
GROUP_WIDTH = SSM_WIDTH // SSM_GROUPS
CONV_WIDTH = 4
CHUNK = 128
CONV_CH = SSM_WIDTH + 2 * SSM_GROUPS * SSM_STATE

ATTN_HEADS = 8
KV_HEADS = 2
Q_PER_KV = ATTN_HEADS // KV_HEADS
HEAD_DIM = 64
ATTN_WIDTH = ATTN_HEADS * HEAD_DIM
KV_WIDTH = KV_HEADS * HEAD_DIM
WINDOW = 128
BLOCK = WINDOW
QKV_COLS = ATTN_WIDTH + 2 * KV_WIDTH

REL_BUCKETS = 32
REL_MAX_DIST = 128

CROSS_HEADS = 4
CROSS_HEAD_DIM = 128
CROSS_WIDTH = CROSS_HEADS * CROSS_HEAD_DIM

N_EXPERTS = 32
TOP_K = 4
D_FF = D_MODEL
SWIGLU_ALPHA = 1.702
SWIGLU_LIMIT = 7.0

LANES = 128
DT_PAD = LANES
PROJ_COLS = SSM_WIDTH + CONV_CH + DT_PAD + QKV_COLS
VMEM_LIMIT = 56 * 1024 * 1024

TM_PROJ = 512
TM_MID = 512
TM_SCATTER = 512
TM_FINAL = 256
MOE_ROWS = 256
DMA_UNROLL = 8


def _nt(a, b):
    return lax.dot_general(a, b, (((1,), (1,)), ((), ())), preferred_element_type=F32)


def _tn(a, b):
    return lax.dot_general(a, b, (((0,), (0,)), ((), ())), preferred_element_type=F32)


def _dot(a, b):
    return jnp.dot(a, b, preferred_element_type=F32)


def _dot_exact(a, b):
    return jnp.dot(a, b, preferred_element_type=F32, precision=lax.Precision.HIGHEST)


def _rms(x, g):
    return x * lax.rsqrt(jnp.mean(x * x, axis=-1, keepdims=True) + RMS_EPS) * g


def _sigmoid(x):
    return 1.0 / (1.0 + jnp.exp(-x))


def _params(sem=None):
    return pltpu.CompilerParams(dimension_semantics=sem, vmem_limit_bytes=VMEM_LIMIT)


def _inproj_kernel(x_ref, g_ref, w_ref, bq_ref, z_ref, xbc_ref, dt_ref, qkv_ref):
    u = _rms(x_ref[...], g_ref[...])
    p = _dot(u.astype(BF16), w_ref[...])
    z_ref[...] = p[:, :SSM_WIDTH]
    xbc_ref[...] = p[:, SSM_WIDTH:SSM_WIDTH + CONV_CH]
    dt_ref[...] = p[:, SSM_WIDTH + CONV_CH:SSM_WIDTH + CONV_CH + DT_PAD]
    qkv_ref[...] = p[:, SSM_WIDTH + CONV_CH + DT_PAD:] + bq_ref[...]


def _inproj(x2, g_mix, w_cat, b_qkv):
    n = x2.shape[0]
    row = lambda i: (i, 0)
    fixed = lambda i: (0, 0)
    return pl.pallas_call(
        _inproj_kernel,
        grid=(n // TM_PROJ,),
        in_specs=[
            pl.BlockSpec((TM_PROJ, D_MODEL), row),
            pl.BlockSpec((1, D_MODEL), fixed),
            pl.BlockSpec((D_MODEL, PROJ_COLS), fixed),
            pl.BlockSpec((1, QKV_COLS), fixed),
        ],
        out_specs=[
            pl.BlockSpec((TM_PROJ, SSM_WIDTH), row),
            pl.BlockSpec((TM_PROJ, CONV_CH), row),
            pl.BlockSpec((TM_PROJ, DT_PAD), row),
            pl.BlockSpec((TM_PROJ, QKV_COLS), row),
        ],
        out_shape=[
            jax.ShapeDtypeStruct((n, SSM_WIDTH), F32),
            jax.ShapeDtypeStruct((n, CONV_CH), F32),
            jax.ShapeDtypeStruct((n, DT_PAD), F32),
            jax.ShapeDtypeStruct((n, QKV_COLS), F32),
        ],
        compiler_params=_params(("arbitrary",)),
    )(x2, g_mix, w_cat, b_qkv)


def _ssd_kernel(xbc_ref, z_ref, dt_ref, cw_ref, cb_ref, dtb_ref, ah_ref, dsk_ref, gout_ref,
                e64_ref, e128_ref, tri_ref, y_ref, prev_ref, state_ref):
    c = pl.program_id(1)

    @pl.when(c == 0)
    def _():
        prev_ref[...] = jnp.zeros_like(prev_ref)
        state_ref[...] = jnp.zeros_like(state_ref)

    u = xbc_ref[...]
    pv = prev_ref[...]
    row = lax.broadcasted_iota(I32, (CHUNK, 1), 0)
    acc = u * cw_ref[CONV_WIDTH - 1:CONV_WIDTH, :] + cb_ref[...]
    for j in range(1, CONV_WIDTH):
        shifted = jnp.where(row >= j, pltpu.roll(u, j, 0), pltpu.roll(pv, j, 0))
        acc = acc + shifted * cw_ref[CONV_WIDTH - 1 - j:CONV_WIDTH - j, :]
    prev_ref[...] = u
    xbc = acc * _sigmoid(acc)

    xs = xbc[:, :SSM_WIDTH]
    bm = xbc[:, SSM_WIDTH:SSM_WIDTH + SSM_GROUPS * SSM_STATE]
    cm = xbc[:, SSM_WIDTH + SSM_GROUPS * SSM_STATE:]

    dtr = dt_ref[...] + dtb_ref[...]
    dt = jnp.maximum(dtr, 0.0) + jnp.log1p(jnp.exp(-jnp.abs(dtr)))
    a = dt * ah_ref[...]
    cs = _dot_exact(tri_ref[...], a)
    cs_row = cs.T
    dt_full = _dot_exact(dt, e64_ref[...])
    cs_full = _dot_exact(cs, e64_ref[...])
    cs_wide = _dot_exact(cs, e128_ref[...])
    cs_last = cs_full[CHUNK - 1:CHUNK, :]

    xdt = xs * dt_full
    xdt_b = xdt.astype(BF16)
    xw_b = (xdt * jnp.exp(cs_last - cs_full)).astype(BF16)
    in_decay = jnp.exp(cs_full)
    chunk_decay = jnp.exp(cs_last)

    li = lax.broadcasted_iota(I32, (CHUNK, CHUNK), 0)
    si = lax.broadcasted_iota(I32, (CHUNK, CHUNK), 1)
    causal = li >= si

    ys = []
    for g in range(SSM_GROUPS):
        bg = bm[:, g * SSM_STATE:(g + 1) * SSM_STATE].astype(BF16)
        cg = cm[:, g * SSM_STATE:(g + 1) * SSM_STATE].astype(BF16)
        cb = _nt(cg, bg)
        yd = []
        for r in range(HEADS_PER_GROUP):
            h = g * HEADS_PER_GROUP + r
            diff = cs_wide[:, h * CHUNK:(h + 1) * CHUNK] - cs_row[h:h + 1, :]
            m = cb * jnp.exp(jnp.where(causal, diff, -jnp.inf))
            yd.append(_dot(m.astype(BF16), xdt_b[:, h * SSM_HEAD_DIM:(h + 1) * SSM_HEAD_DIM]))
        y_diag = jnp.concatenate(yd, axis=1)
        gs = slice(g * GROUP_WIDTH, (g + 1) * GROUP_WIDTH)
        st = state_ref[g]
        y_off = _dot(cg, st.astype(BF16)) * in_decay[:, gs]
        state_ref[g] = st * chunk_decay[:, gs] + _tn(bg, xw_b[:, gs])
        ys.append(y_diag + y_off)
    y = jnp.concatenate(ys, axis=1) + dsk_ref[...] * xs

    zz = z_ref[...]
    y = y * (zz * _sigmoid(zz))
    outs = []
    for g in range(SSM_GROUPS):
        yg = y[:, g * GROUP_WIDTH:(g + 1) * GROUP_WIDTH]
        outs.append(yg * lax.rsqrt(jnp.mean(yg * yg, axis=-1, keepdims=True) + RMS_EPS))
    y_ref[...] = jnp.concatenate(outs, axis=1) * gout_ref[...]


def _ssd(xbc, z, dtp, conv_w, conv_b, dt_bias_p, a_head_p, d_full, g_out, batch, seq):
    n = xbc.shape[0]
    nc = seq // CHUNK
    row = lambda b, c: (b * nc + c, 0)
    fixed = lambda b, c: (0, 0)
    head = np.arange(SSM_WIDTH) // SSM_HEAD_DIM
    e64 = jnp.asarray((np.arange(DT_PAD)[:, None] == head[None, :]).astype(np.float32))
    head_w = np.arange(SSM_HEADS * CHUNK) // CHUNK
    e128 = jnp.asarray((np.arange(DT_PAD)[:, None] == head_w[None, :]).astype(np.float32))
    tri = jnp.asarray(np.tril(np.ones((CHUNK, CHUNK), np.float32)))
    return pl.pallas_call(
        _ssd_kernel,
        grid=(batch, nc),
        in_specs=[
            pl.BlockSpec((CHUNK, CONV_CH), row),
            pl.BlockSpec((CHUNK, SSM_WIDTH), row),
            pl.BlockSpec((CHUNK, DT_PAD), row),
            pl.BlockSpec((CONV_WIDTH, CONV_CH), fixed),
            pl.BlockSpec((1, CONV_CH), fixed),
            pl.BlockSpec((1, DT_PAD), fixed),
            pl.BlockSpec((1, DT_PAD), fixed),
            pl.BlockSpec((1, SSM_WIDTH), fixed),
            pl.BlockSpec((1, SSM_WIDTH), fixed),
            pl.BlockSpec((DT_PAD, SSM_WIDTH), fixed),
            pl.BlockSpec((DT_PAD, SSM_HEADS * CHUNK), fixed),
            pl.BlockSpec((CHUNK, CHUNK), fixed),
        ],
        out_specs=pl.BlockSpec((CHUNK, SSM_WIDTH), row),
        out_shape=jax.ShapeDtypeStruct((n, SSM_WIDTH), F32),
        scratch_shapes=[
            pltpu.VMEM((CHUNK, CONV_CH), F32),
            pltpu.VMEM((SSM_GROUPS, SSM_STATE, GROUP_WIDTH), F32),
        ],
        compiler_params=_params(("arbitrary", "arbitrary")),
    )(xbc, z, dtp, conv_w, conv_b, dt_bias_p, a_head_p, d_full, g_out, e64, e128, tri)


def _t5_bucket_table():
    q_pos = np.arange(BLOCK)[:, None]
    k_pos = np.arange(2 * BLOCK)[None, :] - BLOCK
    dist = q_pos - k_pos
    d = np.maximum(dist, 0)
    max_exact = REL_BUCKETS // 2
    ratio = np.maximum(d, max_exact).astype(np.float32) / np.float32(max_exact)
    large = max_exact + (np.log(ratio) / np.float32(math.log(REL_MAX_DIST / max_exact))
                         * np.float32(REL_BUCKETS - max_exact)).astype(np.int32)
    large = np.minimum(large, REL_BUCKETS - 1)
    bucket = np.where(d < max_exact, d, large)
    in_window = (dist >= 0) & (dist < WINDOW)
    return np.where(in_window, bucket, -1).astype(np.int32)


def _bias_kernel(rb_ref, bucket_ref, out_ref):
    bucket = bucket_ref[...]
    for h in range(ATTN_HEADS):
        acc = jnp.zeros(bucket.shape, F32)
        for b in range(REL_BUCKETS):
            acc = jnp.where(bucket == b, rb_ref[b, h], acc)
        out_ref[h] = jnp.where(bucket >= 0, acc, -jnp.inf)


def _bias_table(rel_bias):
    bucket = jnp.asarray(_t5_bucket_table())
    return pl.pallas_call(
        _bias_kernel,
        in_specs=[
            pl.BlockSpec(memory_space=pltpu.SMEM),
            pl.BlockSpec(memory_space=pltpu.VMEM),
        ],
        out_specs=pl.BlockSpec(memory_space=pltpu.VMEM),
        out_shape=jax.ShapeDtypeStruct((ATTN_HEADS, BLOCK, 2 * BLOCK), F32),
    )(rel_bias, bucket)


def _swa_kernel(sink_ref, q_ref, k_ref, v_ref, kp_ref, vp_ref, bias_ref, o_ref):
    i = pl.program_id(1)
    q = (q_ref[...] * (HEAD_DIM ** -0.5)).astype(BF16)
    col = lax.broadcasted_iota(I32, (BLOCK, 2 * BLOCK), 1)
    key_ok = jnp.logical_or(col >= BLOCK, i > 0)
    outs = []
    for g in range(KV_HEADS):
        gs = slice(g * HEAD_DIM, (g + 1) * HEAD_DIM)
        k2 = jnp.concatenate([kp_ref[:, gs], k_ref[:, gs]], axis=0).astype(BF16)
        v2 = jnp.concatenate([vp_ref[:, gs], v_ref[:, gs]], axis=0).astype(BF16)
        for r in range(Q_PER_KV):
            h = g * Q_PER_KV + r
            s = _nt(q[:, h * HEAD_DIM:(h + 1) * HEAD_DIM], k2) + bias_ref[h]
            s = jnp.where(key_ok, s, -jnp.inf)
            sink = sink_ref[h]
            m = jnp.maximum(jnp.max(s, axis=-1, keepdims=True), sink)
            p = jnp.exp(s - m)
            denom = jnp.sum(p, axis=-1, keepdims=True) + jnp.exp(sink - m)
            outs.append(_dot(p.astype(BF16), v2) / denom)
    o_ref[...] = jnp.concatenate(outs, axis=1)


def _swa(qkv, sinks, bias, batch, seq):
    n = qkv.shape[0]
    nb = seq // BLOCK
    kcol = ATTN_WIDTH // KV_WIDTH
    return pl.pallas_call(
        _swa_kernel,
        grid_spec=pltpu.PrefetchScalarGridSpec(
            num_scalar_prefetch=1,
            grid=(batch, nb),
            in_specs=[
                pl.BlockSpec((BLOCK, ATTN_WIDTH), lambda b, i, s: (b * nb + i, 0)),
                pl.BlockSpec((BLOCK, KV_WIDTH), lambda b, i, s: (b * nb + i, kcol)),
                pl.BlockSpec((BLOCK, KV_WIDTH), lambda b, i, s: (b * nb + i, kcol + 1)),
                pl.BlockSpec((BLOCK, KV_WIDTH), lambda b, i, s: (b * nb + jnp.maximum(i - 1, 0), kcol)),
                pl.BlockSpec((BLOCK, KV_WIDTH), lambda b, i, s: (b * nb + jnp.maximum(i - 1, 0), kcol + 1)),
                pl.BlockSpec((ATTN_HEADS, BLOCK, 2 * BLOCK), lambda b, i, s: (0, 0, 0)),
            ],
            out_specs=pl.BlockSpec((BLOCK, ATTN_WIDTH), lambda b, i, s: (b * nb + i, 0)),
        ),
        out_shape=jax.ShapeDtypeStruct((n, ATTN_WIDTH), F32),
        compiler_params=_params(("arbitrary", "arbitrary")),
    )(sinks, qkv, qkv, qkv, qkv, qkv, bias)


def _memkv_kernel(m_ref, g_ref, w_ref, k_ref, v_ref):
    u = _rms(m_ref[...], g_ref[...])
    kv = _dot(u.astype(BF16), w_ref[...])
    k_ref[...] = kv[:, :CROSS_WIDTH].astype(BF16)
    v_ref[...] = kv[:, CROSS_WIDTH:].astype(BF16)


def _memkv(mem2, g_mem, w_kv):
    n = mem2.shape[0]
    return pl.pallas_call(
        _memkv_kernel,
        out_shape=[jax.ShapeDtypeStruct((n, CROSS_WIDTH), BF16)] * 2,
        compiler_params=_params(),
    )(mem2, g_mem, w_kv)


def _mid_kernel(ys_ref, ya_ref, x_ref, wout_ref, bout_ref, gc_ref, wq_ref, k_ref, v_ref, wo_ref,
                gf_ref, wr_ref, br_ref, upper_ref,
                h2_ref, u3_ref, idx_ref, gate_ref, rank_ref, cnt_ref, run_ref):
    i = pl.program_id(0)

    @pl.when(i == 0)
    def _():
        run_ref[...] = jnp.zeros_like(run_ref)

    ycat = jnp.concatenate([ys_ref[...], ya_ref[...]], axis=1).astype(BF16)
    h1 = x_ref[...] + _dot(ycat, wout_ref[...]) + bout_ref[...]

    u2 = _rms(h1, gc_ref[...])
    q = _dot(u2.astype(BF16), wq_ref[...]).astype(BF16)
    outs = []
    for h in range(CROSS_HEADS):
        hs = slice(h * CROSS_HEAD_DIM, (h + 1) * CROSS_HEAD_DIM)
        s = _nt(q[:, hs], k_ref[:, hs]) * (CROSS_HEAD_DIM ** -0.5)
        m = jnp.max(s, axis=-1, keepdims=True)
        p = jnp.exp(s - m)
        outs.append(_dot(p.astype(BF16), v_ref[:, hs]) / jnp.sum(p, axis=-1, keepdims=True))
    o = jnp.concatenate(outs, axis=1).astype(BF16)
    h2 = h1 + _dot(o, wo_ref[...])
    h2_ref[...] = h2

    u3 = _rms(h2, gf_ref[...])
    u3_ref[...] = u3

    logits = _nt(wr_ref[...], u3.astype(BF16)) + br_ref[...]
    eid = lax.broadcasted_iota(I32, logits.shape, 0)
    vals, idxs, hots = [], [], []
    for _ in range(TOP_K):
        m = jnp.max(logits, axis=0, keepdims=True)
        ix = jnp.min(jnp.where(logits == m, eid, N_EXPERTS), axis=0, keepdims=True)
        hot = eid == ix
        logits = jnp.where(hot, -jnp.inf, logits)
        vals.append(m)
        idxs.append(ix)
        hots.append(hot)
    ex = [jnp.exp(v - vals[0]) for v in vals]
    tot = ex[0] + ex[1] + ex[2] + ex[3]
    gate_ref[...] = jnp.concatenate([e / tot for e in ex], axis=0)
    idx_ref[...] = jnp.concatenate(idxs, axis=0)

    base = run_ref[...]
    ranks = []
    for kk in range(TOP_K):
        hot_f = jnp.where(hots[kk], 1.0, 0.0)
        before = _dot(hot_f.astype(BF16), upper_ref[...])
        ranks.append(jnp.sum(jnp.where(hots[kk], before + base, 0.0), axis=0, keepdims=True))
        base = base + jnp.sum(hot_f, axis=1, keepdims=True)
    rank_ref[...] = jnp.concatenate(ranks, axis=0).astype(I32)
    run_ref[...] = base
    cnt_ref[...] = jnp.broadcast_to(base, cnt_ref.shape)


def _mid(y_ssm, y_attn, x2, w_out, b_out, g_cross, w_q, k_mem, v_mem, w_o, g_ffn, w_r_t, b_r, seq, mem_len):
    n = x2.shape[0]
    tiles_per_batch = seq // TM_MID
    row = lambda i: (i, 0)
    col = lambda i: (0, i)
    fixed = lambda i: (0, 0)
    memb = lambda i: (i // tiles_per_batch, 0)
    upper = jnp.asarray(np.triu(np.ones((TM_MID, TM_MID), np.float32), 1)).astype(BF16)
    return pl.pallas_call(
        _mid_kernel,
        grid=(n // TM_MID,),
        in_specs=[
            pl.BlockSpec((TM_MID, SSM_WIDTH), row),
            pl.BlockSpec((TM_MID, ATTN_WIDTH), row),
            pl.BlockSpec((TM_MID, D_MODEL), row),
            pl.BlockSpec((D_MODEL, D_MODEL), fixed),
            pl.BlockSpec((1, D_MODEL), fixed),
            pl.BlockSpec((1, D_MODEL), fixed),
            pl.BlockSpec((D_MODEL, CROSS_WIDTH), fixed),
            pl.BlockSpec((mem_len, CROSS_WIDTH), memb),
            pl.BlockSpec((mem_len, CROSS_WIDTH), memb),
            pl.BlockSpec((CROSS_WIDTH, D_MODEL), fixed),
            pl.BlockSpec((1, D_MODEL), fixed),
            pl.BlockSpec((N_EXPERTS, D_MODEL), fixed),
            pl.BlockSpec((N_EXPERTS, 1), fixed),
            pl.BlockSpec((TM_MID, TM_MID), fixed),
        ],
        out_specs=[
            pl.BlockSpec((TM_MID, D_MODEL), row),
            pl.BlockSpec((TM_MID, D_MODEL), row),
            pl.BlockSpec((TOP_K, TM_MID), col),
            pl.BlockSpec((TOP_K, TM_MID), col),
            pl.BlockSpec((TOP_K, TM_MID), col),
            pl.BlockSpec((N_EXPERTS, LANES), fixed),
        ],
        out_shape=[
            jax.ShapeDtypeStruct((n, D_MODEL), F32),
            jax.ShapeDtypeStruct((n, D_MODEL), F32),
            jax.ShapeDtypeStruct((TOP_K, n), I32),
            jax.ShapeDtypeStruct((TOP_K, n), F32),
            jax.ShapeDtypeStruct((TOP_K, n), I32),
            jax.ShapeDtypeStruct((N_EXPERTS, LANES), F32),
        ],
        scratch_shapes=[pltpu.VMEM((N_EXPERTS, 1), F32)],
        compiler_params=_params(("arbitrary",)),
    )(y_ssm, y_attn, x2, w_out, b_out, g_cross, w_q, k_mem, v_mem, w_o, g_ffn, w_r_t, b_r, upper)


def _pos_kernel(start_ref, idx_ref, rank_ref, pos_ref):
    idx = idx_ref[...]
    acc = rank_ref[...]
    for e in range(N_EXPERTS):
        acc = acc + jnp.where(idx == e, start_ref[e], 0)
    pos_ref[...] = acc


def _positions(start, idx, rank):
    return pl.pallas_call(
        _pos_kernel,
        in_specs=[
            pl.BlockSpec(memory_space=pltpu.SMEM),
            pl.BlockSpec(memory_space=pltpu.VMEM),
            pl.BlockSpec(memory_space=pltpu.VMEM),
        ],
        out_specs=pl.BlockSpec(memory_space=pltpu.VMEM),
        out_shape=jax.ShapeDtypeStruct(idx.shape, I32),
    )(start, idx, rank)


def _row_copy(src_ref, src_row, dst_ref, dst_row, sem):
    return pltpu.make_async_copy(src_ref.at[pl.ds(src_row, 1)], dst_ref.at[pl.ds(dst_row, 1)], sem)


def _scatter_kernel(n_tok, pos_ref, u_ref, xs_in_ref, xs_ref, sem):
    del xs_in_ref
    base = pl.program_id(0) * TM_SCATTER

    def issue(j, carry):
        for jj in range(DMA_UNROLL):
            t = j * DMA_UNROLL + jj
            for kk in range(TOP_K):
                _row_copy(u_ref, t, xs_ref, pos_ref[kk * n_tok + base + t], sem).start()
        return carry

    def drain(j, carry):
        for jj in range(DMA_UNROLL):
            for kk in range(TOP_K):
                _row_copy(u_ref, 0, xs_ref, 0, sem).wait()
        return carry

    lax.fori_loop(0, TM_SCATTER // DMA_UNROLL, issue, 0)
    lax.fori_loop(0, TM_SCATTER // DMA_UNROLL, drain, 0)


def _scatter_rows(pos_flat, u3, xs_init):
    n = u3.shape[0]
    return pl.pallas_call(
        functools.partial(_scatter_kernel, n),
        grid_spec=pltpu.PrefetchScalarGridSpec(
            num_scalar_prefetch=1,
            grid=(n // TM_SCATTER,),
            in_specs=[
                pl.BlockSpec((TM_SCATTER, D_MODEL), lambda i, p: (i, 0)),
                pl.BlockSpec(memory_space=pl.ANY),
            ],
            out_specs=pl.BlockSpec(memory_space=pl.ANY),
            scratch_shapes=[pltpu.SemaphoreType.DMA],
        ),
        out_shape=jax.ShapeDtypeStruct(xs_init.shape, xs_init.dtype),
        input_output_aliases={2: 0},
        compiler_params=_params(("arbitrary",)),
    )(pos_flat, u3, xs_init)


PAIR_COLS = 2 * LANES


def _expert_kernel(be_ref, na_ref, x_ref, w1_ref, b1_ref, w2_ref, b2_ref, perm_ref, y_ref, w1p_ref, w2b_ref):
    b = pl.program_id(0)
    active = b < na_ref[0]
    new_expert = jnp.logical_or(b == 0, be_ref[b] != be_ref[jnp.maximum(b - 1, 0)])

    @pl.when(jnp.logical_and(active, new_expert))
    def _():
        for c in range(2 * D_FF // PAIR_COLS):
            cs = slice(c * PAIR_COLS, (c + 1) * PAIR_COLS)
            w1p_ref[:, cs] = _dot(w1_ref[0, :, cs].astype(BF16), perm_ref[...]).astype(BF16)
        w2b_ref[...] = w2_ref[0].astype(BF16)

    @pl.when(active)
    def _():
        x = x_ref[...].astype(BF16)
        hdn = _dot(x, w1p_ref[...]) + b1_ref[0]
        acts = []
        for c in range(2 * D_FF // PAIR_COLS):
            hg = jnp.minimum(hdn[:, c * PAIR_COLS:c * PAIR_COLS + LANES], SWIGLU_LIMIT)
            hl = jnp.clip(hdn[:, c * PAIR_COLS + LANES:(c + 1) * PAIR_COLS], -SWIGLU_LIMIT, SWIGLU_LIMIT)
            acts.append((hg * _sigmoid(SWIGLU_ALPHA * hg) * (hl + 1.0)).astype(BF16))
        y_ref[...] = _dot(jnp.concatenate(acts, axis=1), w2b_ref[...]) + b2_ref[0]

    @pl.when(jnp.logical_not(active))
    def _():
        y_ref[...] = jnp.zeros_like(y_ref)


def _experts(block_exp, n_active, xs, w1, b1p, w2, b2):
    n_rows = xs.shape[0]
    n_blocks = n_rows // MOE_ROWS
    xmap = lambda b, be, na: (jnp.minimum(b, na[0] - 1), 0)
    emap = lambda b, be, na: (be[b], 0, 0)
    perm = np.zeros((PAIR_COLS, PAIR_COLS), np.float32)
    perm[2 * np.arange(LANES), np.arange(LANES)] = 1.0
    perm[2 * np.arange(LANES) + 1, LANES + np.arange(LANES)] = 1.0
    return pl.pallas_call(
        _expert_kernel,
        name="experts",
        grid_spec=pltpu.PrefetchScalarGridSpec(
            num_scalar_prefetch=2,
            grid=(n_blocks,),
            in_specs=[
                pl.BlockSpec((MOE_ROWS, D_MODEL), xmap),
                pl.BlockSpec((1, D_MODEL, 2 * D_FF), emap),
                pl.BlockSpec((1, 1, 2 * D_FF), emap),
                pl.BlockSpec((1, D_FF, D_MODEL), emap),
                pl.BlockSpec((1, 1, D_MODEL), emap),
                pl.BlockSpec((PAIR_COLS, PAIR_COLS), lambda b, be, na: (0, 0)),
            ],
            out_specs=pl.BlockSpec((MOE_ROWS, D_MODEL), lambda b, be, na: (b, 0)),
            scratch_shapes=[
                pltpu.VMEM((D_MODEL, 2 * D_FF), BF16),
                pltpu.VMEM((D_FF, D_MODEL), BF16),
            ],
        ),
        out_shape=jax.ShapeDtypeStruct((n_rows, D_MODEL), F32),
        compiler_params=_params(("arbitrary",)),
    )(block_exp, n_active, xs, w1, b1p, w2, b2, jnp.asarray(perm).astype(BF16))


def _final_kernel(n_tok, pos_ref, h2_ref, gate_ref, g_ref, ys_ref, o_ref, buf_ref, sem):
    base = pl.program_id(0) * TM_FINAL

    def issue(j, carry):
        for jj in range(DMA_UNROLL):
            t = j * DMA_UNROLL + jj
            for kk in range(TOP_K):
                _row_copy(ys_ref, pos_ref[kk * n_tok + base + t], buf_ref.at[kk], t, sem).start()
        return carry

    def drain(j, carry):
        for jj in range(DMA_UNROLL):
            for kk in range(TOP_K):
                _row_copy(ys_ref, 0, buf_ref.at[kk], 0, sem).wait()
        return carry

    lax.fori_loop(0, TM_FINAL // DMA_UNROLL, issue, 0)
    lax.fori_loop(0, TM_FINAL // DMA_UNROLL, drain, 0)

    gates = gate_ref[...].T
    h = h2_ref[...]
    for kk in range(TOP_K):
        h = h + buf_ref[kk] * gates[:, kk:kk + 1]
    o_ref[...] = _rms(h, g_ref[...])


def _combine(pos_flat, h2, gates, g_final, ys):
    n = h2.shape[0]
    return pl.pallas_call(
        functools.partial(_final_kernel, n),
        grid_spec=pltpu.PrefetchScalarGridSpec(
            num_scalar_prefetch=1,
            grid=(n // TM_FINAL,),
            in_specs=[
                pl.BlockSpec((TM_FINAL, D_MODEL), lambda i, p: (i, 0)),
                pl.BlockSpec((TOP_K, TM_FINAL), lambda i, p: (0, i)),
                pl.BlockSpec((1, D_MODEL), lambda i, p: (0, 0)),
                pl.BlockSpec(memory_space=pl.ANY),
            ],
            out_specs=pl.BlockSpec((TM_FINAL, D_MODEL), lambda i, p: (i, 0)),
            scratch_shapes=[
                pltpu.VMEM((TOP_K, TM_FINAL, D_MODEL), F32),
                pltpu.SemaphoreType.DMA,
            ],
        ),
        out_shape=jax.ShapeDtypeStruct((n, D_MODEL), F32),
        compiler_params=_params(("arbitrary",)),
    )(pos_flat, h2, gates, g_final, ys)


def _layer(h, mem2, batch, seq, mem_len, g_mix, w_in, b_qkv, conv_w, conv_b, dt_bias, a_log, d_skip,
           g_ssm_out, attn_sinks, bias, w_out, b_out, g_cross, g_mem, w_q_cross, w_kv_cross, w_o_cross,
           g_ffn, w_router, b_router, w1, b1, w2, b2):
    n = h.shape[0]
    c0, c1, c2 = SSM_WIDTH, SSM_WIDTH + CONV_CH, SSM_WIDTH + CONV_CH + SSM_HEADS
    w_cat = jnp.concatenate(
        [w_in[:, :c1], jnp.pad(w_in[:, c1:c2], ((0, 0), (0, DT_PAD - SSM_HEADS))), w_in[:, c2:]],
        axis=1).astype(BF16)
    z, xbc, dtp, qkv = _inproj(h, g_mix[None, :], w_cat, b_qkv[None, :])

    pad_h = (0, DT_PAD - SSM_HEADS)
    dt_bias_p = jnp.pad(dt_bias, pad_h)[None, :]
    a_head_p = jnp.pad(-jnp.exp(a_log), pad_h)[None, :]
    d_full = jnp.repeat(d_skip, SSM_HEAD_DIM)[None, :]
    y_ssm = _ssd(xbc, z, dtp, conv_w, conv_b[None, :], dt_bias_p, a_head_p, d_full,
                 g_ssm_out[None, :], batch, seq)
    y_attn = _swa(qkv, attn_sinks, bias, batch, seq)

    k_mem, v_mem = _memkv(mem2, g_mem[None, :], w_kv_cross.astype(BF16))
    h2, u3, idx, gates, rank, cnt = _mid(
        y_ssm, y_attn, h, w_out.astype(BF16), b_out[None, :], g_cross[None, :],
        w_q_cross.astype(BF16), k_mem, v_mem, w_o_cross.astype(BF16), g_ffn[None, :],
        w_router.T.astype(BF16), b_router[:, None], seq, mem_len)

    n_blocks = (n * TOP_K) // MOE_ROWS + N_EXPERTS
    counts = cnt[:, 0].astype(I32)
    padded = (counts + MOE_ROWS - 1) // MOE_ROWS * MOE_ROWS
    padded_end = jnp.cumsum(padded)
    start = (padded_end - padded).astype(I32)
    n_active = (padded_end[-1] // MOE_ROWS).astype(I32)
    blk = jnp.arange(n_blocks, dtype=I32)
    blk_exp = jnp.sum((blk * MOE_ROWS)[:, None] >= padded_end[None, :], axis=1)
    blk_exp = jnp.minimum(blk_exp, N_EXPERTS - 1).astype(I32)
    blk_exp = jnp.where(blk < n_active, blk_exp, blk_exp[n_active - 1])

    pos_flat = _positions(start, idx, rank).reshape(-1)
    xs = _scatter_rows(pos_flat, u3, jnp.zeros((n_blocks * MOE_ROWS, D_MODEL), F32))

    b1p = b1.reshape(N_EXPERTS, 2 * D_FF // PAIR_COLS, LANES, 2).transpose(0, 1, 3, 2)
    ys = _experts(blk_exp, n_active[None], xs, w1, b1p.reshape(N_EXPERTS, 1, 2 * D_FF), w2, b2[:, None, :])
    return pos_flat, h2, gates, ys


def kernel(x, mem, g_mix, w_in, b_qkv, conv_w, conv_b, dt_bias, a_log, d_skip, g_ssm_out, attn_sinks, rel_bias, w_out, b_out, g_cross, g_mem, w_q_cross, w_kv_cross, w_o_cross, g_ffn, w_router, b_router, w1, b1, w2, b2, g_final):
    batch, seq, d = x.shape
    mem_len = mem.shape[1]
    depth = g_mix.shape[0]
    assert depth == 1 and d == D_MODEL and seq % TM_MID == 0
    h = x.reshape(batch * seq, d)
    mem2 = mem.reshape(batch * mem_len, d)
    bias = _bias_table(rel_bias)
    pos_flat, h2, gates, ys = _layer(
        h, mem2, batch, seq, mem_len, g_mix[0], w_in[0], b_qkv[0], conv_w[0], conv_b[0], dt_bias[0],
        a_log[0], d_skip[0], g_ssm_out[0], attn_sinks[0], bias, w_out[0], b_out[0], g_cross[0],
        g_mem[0], w_q_cross[0], w_kv_cross[0], w_o_cross[0], g_ffn[0], w_router[0], b_router[0],
        w1[0], b1[0], w2[0], b2[0])
    out = _combine(pos_flat, h2, gates, g_final[None, :], ys)
    return out.reshape(batch, seq, d)
```

```python
import functools
import math

import numpy as np
import jax
import jax.numpy as jnp
from jax import lax
from jax.experimental import pallas as pl
from jax.experimental.pallas import tpu as pltpu

F32 = jnp.float32
BF16 = jnp.bfloat16
I32 = jnp.int32

D_MODEL = 1024
RMS_EPS = 1e-5

SSM_HEADS = 8
SSM_HEAD_DIM = 64
SSM_WIDTH = SSM_HEADS * SSM_HEAD_DIM
SSM_STATE = 128
SSM_GROUPS = 2
HEADS_PER_GROUP = SSM_HEADS // SSM_GROUPS
GROUP_WIDTH = SSM_WIDTH // SSM_GROUPS
CONV_WIDTH = 4
CHUNK = 128
CONV_CH = SSM_WIDTH + 2 * SSM_GROUPS * SSM_STATE

ATTN_HEADS = 8
KV_HEADS = 2
Q_PER_KV = ATTN_HEADS // KV_HEADS
HEAD_DIM = 64
ATTN_WIDTH = ATTN_HEADS * HEAD_DIM
KV_WIDTH = KV_HEADS * HEAD_DIM
WINDOW = 128
BLOCK = WINDOW
QKV_COLS = ATTN_WIDTH + 2 * KV_WIDTH

REL_BUCKETS = 32
REL_MAX_DIST = 128

CROSS_HEADS = 4
CROSS_HEAD_DIM = 128
CROSS_WIDTH = CROSS_HEADS * CROSS_HEAD_DIM

N_EXPERTS = 32
TOP_K = 4
D_FF = D_MODEL
SWIGLU_ALPHA = 1.702
SWIGLU_LIMIT = 7.0

LANES = 128
SUBLANES = 8
DT_PAD = LANES
PROJ_COLS = SSM_WIDTH + CONV_CH + DT_PAD + QKV_COLS
VMEM_LIMIT = 56 * 1024 * 1024

TM_PROJ = 512
TM_MID = 512
TM_SCATTER = 512
TM_FINAL = 256
MOE_ROWS = 256
DMA_UNROLL = 8


def _nt(a, b):
    return lax.dot_general(a, b, (((1,), (1,)), ((), ())), preferred_element_type=F32)


def _tn(a, b):
    return lax.dot_general(a, b, (((0,), (0,)), ((), ())), preferred_element_type=F32)


def _dot(a, b):
    return jnp.dot(a, b, preferred_element_type=F32)


def _dot_exact(a, b):
    return jnp.dot(a, b, preferred_element_type=F32, precision=lax.Precision.HIGHEST)


def _rms(x, g):
    return x * lax.rsqrt(jnp.mean(x * x, axis=-1, keepdims=True) + RMS_EPS) * g


def _sigmoid(x):
    return 1.0 / (1.0 + jnp.exp(-x))


def _params(sem=None):
    return pltpu.CompilerParams(dimension_semantics=sem, vmem_limit_bytes=VMEM_LIMIT)


def _inproj_kernel(x_ref, g_ref, w_ref, bq_ref, z_ref, xbc_ref, dt_ref, qkv_ref):
    u = _rms(x_ref[...], g_ref[...])
    p = _dot(u.astype(BF16), w_ref[...])
    z_ref[...] = p[:, :SSM_WIDTH]
    xbc_ref[...] = p[:, SSM_WIDTH:SSM_WIDTH + CONV_CH]
    dt_ref[...] = p[:, SSM_WIDTH + CONV_CH:SSM_WIDTH + CONV_CH + DT_PAD]
    qkv_ref[...] = p[:, SSM_WIDTH + CONV_CH + DT_PAD:] + bq_ref[...]


def _inproj(x2, g_mix, w_cat, b_qkv):
    n = x2.shape[0]
    row = lambda i: (i, 0)
    fixed = lambda i: (0, 0)
    return pl.pallas_call(
        _inproj_kernel,
        grid=(n // TM_PROJ,),
        in_specs=[
            pl.BlockSpec((TM_PROJ, D_MODEL), row),
            pl.BlockSpec((1, D_MODEL), fixed),
            pl.BlockSpec((D_MODEL, PROJ_COLS), fixed),
            pl.BlockSpec((1, QKV_COLS), fixed),
        ],
        out_specs=[
            pl.BlockSpec((TM_PROJ, SSM_WIDTH), row),
            pl.BlockSpec((TM_PROJ, CONV_CH), row),
            pl.BlockSpec((TM_PROJ, DT_PAD), row),
            pl.BlockSpec((TM_PROJ, QKV_COLS), row),
        ],
        out_shape=[
            jax.ShapeDtypeStruct((n, SSM_WIDTH), F32),
            jax.ShapeDtypeStruct((n, CONV_CH), F32),
            jax.ShapeDtypeStruct((n, DT_PAD), F32),
            jax.ShapeDtypeStruct((n, QKV_COLS), F32),
        ],
        compiler_params=_params(("arbitrary",)),
    )(x2, g_mix, w_cat, b_qkv)


def _ssd_kernel(xbc_ref, z_ref, dt_ref, cw_ref, cb_ref, dtb_ref, ah_ref, dsk_ref, gout_ref,
                e64_ref, e128_ref, tri_ref, y_ref, prev_ref, state_ref):
    c = pl.program_id(1)

    @pl.when(c == 0)
    def _():
        prev_ref[...] = jnp.zeros_like(prev_ref)
        state_ref[...] = jnp.zeros_like(state_ref)

    u = xbc_ref[...]
    pv = prev_ref[...]
    row = lax.broadcasted_iota(I32, (CHUNK, 1), 0)
    acc = u * cw_ref[CONV_WIDTH - 1:CONV_WIDTH, :] + cb_ref[...]
    for j in range(1, CONV_WIDTH):
        shifted = jnp.where(row >= j, pltpu.roll(u, j, 0), pltpu.roll(pv, j, 0))
        acc = acc + shifted * cw_ref[CONV_WIDTH - 1 - j:CONV_WIDTH - j, :]
    prev_ref[...] = u
    xbc = acc * _sigmoid(acc)

    xs = xbc[:, :SSM_WIDTH]
    bm = xbc[:, SSM_WIDTH:SSM_WIDTH + SSM_GROUPS * SSM_STATE]
    cm = xbc[:, SSM_WIDTH + SSM_GROUPS * SSM_STATE:]

    dtr = dt_ref[...] + dtb_ref[...]
    dt = jnp.maximum(dtr, 0.0) + jnp.log1p(jnp.exp(-jnp.abs(dtr)))
    a = dt * ah_ref[...]
    cs = _dot_exact(tri_ref[...], a)
    cs_row = cs.T
    dt_full = _dot_exact(dt, e64_ref[...])
    cs_full = _dot_exact(cs, e64_ref[...])
    cs_wide = _dot_exact(cs, e128_ref[...])
    cs_last = cs_full[CHUNK - 1:CHUNK, :]

    xdt = xs * dt_full
    xdt_b = xdt.astype(BF16)
    xw_b = (xdt * jnp.exp(cs_last - cs_full)).astype(BF16)
    in_decay = jnp.exp(cs_full)
    chunk_decay = jnp.exp(cs_last)

    li = lax.broadcasted_iota(I32, (CHUNK, CHUNK), 0)
    si = lax.broadcasted_iota(I32, (CHUNK, CHUNK), 1)
    causal = li >= si

    ys = []
    for g in range(SSM_GROUPS):
        bg = bm[:, g * SSM_STATE:(g + 1) * SSM_STATE].astype(BF16)
        cg = cm[:, g * SSM_STATE:(g + 1) * SSM_STATE].astype(BF16)
        cb = _nt(cg, bg)
        yd = []
        for r in range(HEADS_PER_GROUP):
            h = g * HEADS_PER_GROUP + r
            diff = cs_wide[:, h * CHUNK:(h + 1) * CHUNK] - cs_row[h:h + 1, :]
            m = cb * jnp.exp(jnp.where(causal, diff, -jnp.inf))
            yd.append(_dot(m.astype(BF16), xdt_b[:, h * SSM_HEAD_DIM:(h + 1) * SSM_HEAD_DIM]))
        y_diag = jnp.concatenate(yd, axis=1)
        gs = slice(g * GROUP_WIDTH, (g + 1) * GROUP_WIDTH)
        st = state_ref[g]
        y_off = _dot(cg, st.astype(BF16)) * in_decay[:, gs]
        state_ref[g] = st * chunk_decay[:, gs] + _tn(bg, xw_b[:, gs])
        ys.append(y_diag + y_off)
    y = jnp.concatenate(ys, axis=1) + dsk_ref[...] * xs

    zz = z_ref[...]
    y = y * (zz * _sigmoid(zz))
    outs = []
    for g in range(SSM_GROUPS):
        yg = y[:, g * GROUP_WIDTH:(g + 1) * GROUP_WIDTH]
        outs.append(yg * lax.rsqrt(jnp.mean(yg * yg, axis=-1, keepdims=True) + RMS_EPS))
    y_ref[...] = jnp.concatenate(outs, axis=1) * gout_ref[...]


def _ssd(xbc, z, dtp, conv_w, conv_b, dt_bias_p, a_head_p, d_full, g_out, batch, seq):
    n = xbc.shape[0]
    nc = seq // CHUNK
    row = lambda b, c: (b * nc + c, 0)
    fixed = lambda b, c: (0, 0)
    head = np.arange(SSM_WIDTH) // SSM_HEAD_DIM
    e64 = jnp.asarray((np.arange(DT_PAD)[:, None] == head[None, :]).astype(np.float32))
    head_w = np.arange(SSM_HEADS * CHUNK) // CHUNK
    e128 = jnp.asarray((np.arange(DT_PAD)[:, None] == head_w[None, :]).astype(np.float32))
    tri = jnp.asarray(np.tril(np.ones((CHUNK, CHUNK), np.float32)))
    return pl.pallas_call(
        _ssd_kernel,
        grid=(batch, nc),
        in_specs=[
            pl.BlockSpec((CHUNK, CONV_CH), row),
            pl.BlockSpec((CHUNK, SSM_WIDTH), row),
            pl.BlockSpec((CHUNK, DT_PAD), row),
            pl.BlockSpec((CONV_WIDTH, CONV_CH), fixed),
            pl.BlockSpec((1, CONV_CH), fixed),
            pl.BlockSpec((1, DT_PAD), fixed),
            pl.BlockSpec((1, DT_PAD), fixed),
            pl.BlockSpec((1, SSM_WIDTH), fixed),
            pl.BlockSpec((1, SSM_WIDTH), fixed),
            pl.BlockSpec((DT_PAD, SSM_WIDTH), fixed),
            pl.BlockSpec((DT_PAD, SSM_HEADS * CHUNK), fixed),
            pl.BlockSpec((CHUNK, CHUNK), fixed),
        ],
        out_specs=pl.BlockSpec((CHUNK, SSM_WIDTH), row),
        out_shape=jax.ShapeDtypeStruct((n, SSM_WIDTH), F32),
        scratch_shapes=[
            pltpu.VMEM((CHUNK, CONV_CH), F32),
            pltpu.VMEM((SSM_GROUPS, SSM_STATE, GROUP_WIDTH), F32),
        ],
        compiler_params=_params(("arbitrary", "arbitrary")),
    )(xbc, z, dtp, conv_w, conv_b, dt_bias_p, a_head_p, d_full, g_out, e64, e128, tri)


def _t5_bucket_table():
    q_pos = np.arange(BLOCK)[:, None]
    k_pos = np.arange(2 * BLOCK)[None, :] - BLOCK
    dist = q_pos - k_pos
    d = np.maximum(dist, 0)
    max_exact = REL_BUCKETS // 2
    ratio = np.maximum(d, max_exact).astype(np.float32) / np.float32(max_exact)
    large = max_exact + (np.log(ratio) / np.float32(math.log(REL_MAX_DIST / max_exact))
                         * np.float32(REL_BUCKETS - max_exact)).astype(np.int32)
    large = np.minimum(large, REL_BUCKETS - 1)
    bucket = np.where(d < max_exact, d, large)
    in_window = (dist >= 0) & (dist < WINDOW)
    return np.where(in_window, bucket, -1).astype(np.int32)


def _bias_kernel(rb_ref, bucket_ref, out_ref):
    bucket = bucket_ref[...]
    for h in range(ATTN_HEADS):
        acc = jnp.zeros(bucket.shape, F32)
        for b in range(REL_BUCKETS):
            acc = jnp.where(bucket == b, rb_ref[b, h], acc)
        out_ref[h] = jnp.where(bucket >= 0, acc, -jnp.inf)


def _bias_table(rel_bias):
    bucket = jnp.asarray(_t5_bucket_table())
    return pl.pallas_call(
        _bias_kernel,
        in_specs=[
            pl.BlockSpec(memory_space=pltpu.SMEM),
            pl.BlockSpec(memory_space=pltpu.VMEM),
        ],
        out_specs=pl.BlockSpec(memory_space=pltpu.VMEM),
        out_shape=jax.ShapeDtypeStruct((ATTN_HEADS, BLOCK, 2 * BLOCK), F32),
    )(rel_bias, bucket)


def _swa_kernel(sink_ref, q_ref, k_ref, v_ref, kp_ref, vp_ref, bias_ref, o_ref):
    i = pl.program_id(1)
    q = (q_ref[...] * (HEAD_DIM ** -0.5)).astype(BF16)
    col = lax.broadcasted_iota(I32, (BLOCK, 2 * BLOCK), 1)
    key_ok = jnp.logical_or(col >= BLOCK, i > 0)
    outs = []
    for g in range(KV_HEADS):
        gs = slice(g * HEAD_DIM, (g + 1) * HEAD_DIM)
        k2 = jnp.concatenate([kp_ref[:, gs], k_ref[:, gs]], axis=0).astype(BF16)
        v2 = jnp.concatenate([vp_ref[:, gs], v_ref[:, gs]], axis=0).astype(BF16)
        for r in range(Q_PER_KV):
            h = g * Q_PER_KV + r
            s = _nt(q[:, h * HEAD_DIM:(h + 1) * HEAD_DIM], k2) + bias_ref[h]
            s = jnp.where(key_ok, s, -jnp.inf)
            sink = sink_ref[h]
            m = jnp.maximum(jnp.max(s, axis=-1, keepdims=True), sink)
            p = jnp.exp(s - m)
            denom = jnp.sum(p, axis=-1, keepdims=True) + jnp.exp(sink - m)
            outs.append(_dot(p.astype(BF16), v2) / denom)
    o_ref[...] = jnp.concatenate(outs, axis=1)


def _swa(qkv, sinks, bias, batch, seq):
    n = qkv.shape[0]
    nb = seq // BLOCK
    kcol = ATTN_WIDTH // KV_WIDTH
    return pl.pallas_call(
        _swa_kernel,
        grid_spec=pltpu.PrefetchScalarGridSpec(
            num_scalar_prefetch=1,
            grid=(batch, nb),
            in_specs=[
                pl.BlockSpec((BLOCK, ATTN_WIDTH), lambda b, i, s: (b * nb + i, 0)),
                pl.BlockSpec((BLOCK, KV_WIDTH), lambda b, i, s: (b * nb + i, kcol)),
                pl.BlockSpec((BLOCK, KV_WIDTH), lambda b, i, s: (b * nb + i, kcol + 1)),
                pl.BlockSpec((BLOCK, KV_WIDTH), lambda b, i, s: (b * nb + jnp.maximum(i - 1, 0), kcol)),
                pl.BlockSpec((BLOCK, KV_WIDTH), lambda b, i, s: (b * nb + jnp.maximum(i - 1, 0), kcol + 1)),
                pl.BlockSpec((ATTN_HEADS, BLOCK, 2 * BLOCK), lambda b, i, s: (0, 0, 0)),
            ],
            out_specs=pl.BlockSpec((BLOCK, ATTN_WIDTH), lambda b, i, s: (b * nb + i, 0)),
        ),
        out_shape=jax.ShapeDtypeStruct((n, ATTN_WIDTH), F32),
        compiler_params=_params(("arbitrary", "arbitrary")),
    )(sinks, qkv, qkv, qkv, qkv, qkv, bias)


def _memkv_kernel(m_ref, g_ref, w_ref, k_ref, v_ref):
    u = _rms(m_ref[...], g_ref[...])
    kv = _dot(u.astype(BF16), w_ref[...])
    k_ref[...] = kv[:, :CROSS_WIDTH].astype(BF16)
    v_ref[...] = kv[:, CROSS_WIDTH:].astype(BF16)


def _memkv(mem2, g_mem, w_kv):
    n = mem2.shape[0]
    return pl.pallas_call(
        _memkv_kernel,
        out_shape=[jax.ShapeDtypeStruct((n, CROSS_WIDTH), BF16)] * 2,
        compiler_params=_params(),
    )(mem2, g_mem, w_kv)


def _mid_kernel(ys_ref, ya_ref, x_ref, wout_ref, bout_ref, gc_ref, wq_ref, k_ref, v_ref, wo_ref,
                gf_ref, wr_ref, br_ref, upper_ref, lower_ref,
                h2_ref, u3_ref, gate_ref, lp_ref, tcnt_ref, tbase_ref, run_ref):
    i = pl.program_id(0)

    @pl.when(i == 0)
    def _():
        run_ref[...] = jnp.zeros_like(run_ref)

    ycat = jnp.concatenate([ys_ref[...], ya_ref[...]], axis=1).astype(BF16)
    h1 = x_ref[...] + _dot(ycat, wout_ref[...]) + bout_ref[...]

    u2 = _rms(h1, gc_ref[...])
    q = _dot(u2.astype(BF16), wq_ref[...]).astype(BF16)
    outs = []
    for h in range(CROSS_HEADS):
        hs = slice(h * CROSS_HEAD_DIM, (h + 1) * CROSS_HEAD_DIM)
        s = _nt(q[:, hs], k_ref[:, hs]) * (CROSS_HEAD_DIM ** -0.5)
        m = jnp.max(s, axis=-1, keepdims=True)
        p = jnp.exp(s - m)
        outs.append(_dot(p.astype(BF16), v_ref[:, hs]) / jnp.sum(p, axis=-1, keepdims=True))
    o = jnp.concatenate(outs, axis=1).astype(BF16)
    h2 = h1 + _dot(o, wo_ref[...])
    h2_ref[...] = h2

    u3 = _rms(h2, gf_ref[...]).astype(BF16)
    u3_ref[...] = u3

    logits = _nt(wr_ref[...], u3) + br_ref[...]
    eid = lax.broadcasted_iota(I32, logits.shape, 0)
    vals, idxs, hots = [], [], []
    for _ in range(TOP_K):
        m = jnp.max(logits, axis=0, keepdims=True)
        ix = jnp.min(jnp.where(logits == m, eid, N_EXPERTS), axis=0, keepdims=True)
        hot = eid == ix
        logits = jnp.where(hot, -jnp.inf, logits)
        vals.append(m)
        idxs.append(ix)
        hots.append(hot)
    ex = [jnp.exp(v - vals[0]) for v in vals]
    tot = ex[0] + ex[1] + ex[2] + ex[3]
    gate_ref[...] = jnp.concatenate([e / tot for e in ex], axis=0)

    hot_f = [jnp.where(hot, 1.0, 0.0) for hot in hots]
    cnt_k = [jnp.sum(hf, axis=1, keepdims=True) for hf in hot_f]
    tile_cnt = cnt_k[0] + cnt_k[1] + cnt_k[2] + cnt_k[3]
    tile_cnt = jnp.floor((tile_cnt + (SUBLANES - 1)) * (1.0 / SUBLANES)) * SUBLANES
    offs = _dot_exact(lower_ref[...], jnp.broadcast_to(tile_cnt, (N_EXPERTS, LANES)))[:, :1]
    local = []
    for kk in range(TOP_K):
        before = _dot(hot_f[kk].astype(BF16), upper_ref[...])
        local.append(jnp.sum(jnp.where(hots[kk], before + offs, 0.0), axis=0, keepdims=True))
        offs = offs + cnt_k[kk]
    lp_ref[...] = jnp.concatenate(local, axis=0).astype(I32)
    tcnt_ref[0] = jnp.broadcast_to(tile_cnt, (N_EXPERTS, LANES)).astype(I32)
    tbase_ref[0] = jnp.broadcast_to(run_ref[...], (N_EXPERTS, LANES)).astype(I32)
    run_ref[...] = run_ref[...] + tile_cnt


def _mid(y_ssm, y_attn, x2, w_out, b_out, g_cross, w_q, k_mem, v_mem, w_o, g_ffn, w_r_t, b_r, seq, mem_len):
    n = x2.shape[0]
    tiles_per_batch = seq // TM_MID
    row = lambda i: (i, 0)
    col = lambda i: (0, i)
    fixed = lambda i: (0, 0)
    memb = lambda i: (i // tiles_per_batch, 0)
    tile3 = lambda i: (i, 0, 0)
    n_tiles = n // TM_MID
    upper = jnp.asarray(np.triu(np.ones((TM_MID, TM_MID), np.float32), 1)).astype(BF16)
    lower = jnp.asarray(np.tril(np.ones((N_EXPERTS, N_EXPERTS), np.float32), -1))
    return pl.pallas_call(
        _mid_kernel,
        name="mid",
        grid=(n_tiles,),
        in_specs=[
            pl.BlockSpec((TM_MID, SSM_WIDTH), row),
            pl.BlockSpec((TM_MID, ATTN_WIDTH), row),
            pl.BlockSpec((TM_MID, D_MODEL), row),
            pl.BlockSpec((D_MODEL, D_MODEL), fixed),
            pl.BlockSpec((1, D_MODEL), fixed),
            pl.BlockSpec((1, D_MODEL), fixed),
            pl.BlockSpec((D_MODEL, CROSS_WIDTH), fixed),
            pl.BlockSpec((mem_len, CROSS_WIDTH), memb),
            pl.BlockSpec((mem_len, CROSS_WIDTH), memb),
            pl.BlockSpec((CROSS_WIDTH, D_MODEL), fixed),
            pl.BlockSpec((1, D_MODEL), fixed),
            pl.BlockSpec((N_EXPERTS, D_MODEL), fixed),
            pl.BlockSpec((N_EXPERTS, 1), fixed),
            pl.BlockSpec((TM_MID, TM_MID), fixed),
            pl.BlockSpec((N_EXPERTS, N_EXPERTS), fixed),
        ],
        out_specs=[
            pl.BlockSpec((TM_MID, D_MODEL), row),
            pl.BlockSpec((TM_MID, D_MODEL), row),
            pl.BlockSpec((TOP_K, TM_MID), col),
            pl.BlockSpec((TOP_K, TM_MID), col),
            pl.BlockSpec((1, N_EXPERTS, LANES), tile3),
            pl.BlockSpec((1, N_EXPERTS, LANES), tile3),
        ],
        out_shape=[
            jax.ShapeDtypeStruct((n, D_MODEL), F32),
            jax.ShapeDtypeStruct((n, D_MODEL), BF16),
            jax.ShapeDtypeStruct((TOP_K, n), F32),
            jax.ShapeDtypeStruct((TOP_K, n), I32),
            jax.ShapeDtypeStruct((n_tiles, N_EXPERTS, LANES), I32),
            jax.ShapeDtypeStruct((n_tiles, N_EXPERTS, LANES), I32),
        ],
        scratch_shapes=[pltpu.VMEM((N_EXPERTS, 1), F32)],
        compiler_params=_params(("arbitrary",)),
    )(y_ssm, y_attn, x2, w_out, b_out, g_cross, w_q, k_mem, v_mem, w_o, g_ffn, w_r_t, b_r, upper, lower)


ROW_CHUNK = 256
LOCAL_ROWS = -(-(TM_MID * TOP_K + N_EXPERTS * (SUBLANES - 1)) // ROW_CHUNK) * ROW_CHUNK
RUN_BITS = tuple(1 << s for s in range(TM_MID.bit_length() - 1, SUBLANES.bit_length() - 2, -1))
PAD_BITS = tuple(1 << s for s in range((MOE_ROWS - 1).bit_length() - 1, SUBLANES.bit_length() - 2, -1))


def _run_copies(n, src_ref, src, dst_ref, dst, sem, bits, wait, advance_src=True):
    for bit in bits:
        step = n & bit

        @pl.when(step != 0)
        def _():
            cp = pltpu.make_async_copy(src_ref.at[pl.ds(pl.multiple_of(src, SUBLANES), bit)],
                                       dst_ref.at[pl.ds(pl.multiple_of(dst, SUBLANES), bit)], sem)
            if wait:
                cp.wait()
            else:
                cp.start()

        if advance_src:
            src = src + step
        dst = dst + step


def _dispatch_kernel(tcnt_ref, lo_ref, gs_ref, padn_ref, pads_ref, na_ref, u_ref, lp_ref, xs_ref,
                     xloc_ref, zero_ref, sem, zsem):
    i = pl.program_id(0)

    def zero_fill(wait):
        def body(e, carry):
            _run_copies(padn_ref[e], zero_ref, 0, xs_ref, pads_ref[e], zsem, PAD_BITS, wait, advance_src=False)
            return carry
        lax.fori_loop(0, N_EXPERTS, body, 0)

        def tail(b, carry):
            cp = pltpu.make_async_copy(zero_ref, xs_ref.at[pl.ds(pl.multiple_of(b * MOE_ROWS, MOE_ROWS), MOE_ROWS)], zsem)
            if wait:
                cp.wait()
            else:
                cp.start()
            return carry
        lax.fori_loop(na_ref[0], xs_ref.shape[0] // MOE_ROWS, tail, 0)

    @pl.when(i == 0)
    def _():
        zero_ref[...] = jnp.zeros_like(zero_ref)
        zero_fill(False)

    u = u_ref[...]
    lp = lp_ref[...]
    for c in range(LOCAL_ROWS // ROW_CHUNK):
        r = lax.broadcasted_iota(I32, (ROW_CHUNK, TM_MID), 0) + c * ROW_CHUNK
        p = jnp.zeros((ROW_CHUNK, TM_MID), F32)
        for kk in range(TOP_K):
            p = jnp.where(lp[kk:kk + 1, :] == r, 1.0, p)
        xloc_ref[c * ROW_CHUNK:(c + 1) * ROW_CHUNK, :] = _dot(p.astype(BF16), u)

    def runs(wait):
        def body(e, carry):
            j = i * N_EXPERTS + e
            _run_copies(tcnt_ref[j], xloc_ref, lo_ref[j], xs_ref, gs_ref[j], sem, RUN_BITS, wait)
            return carry
        lax.fori_loop(0, N_EXPERTS, body, 0)

    runs(False)
    runs(True)

    @pl.when(i == 0)
    def _():
        zero_fill(True)


def _dispatch(tcnt, lo, gstart, padn, pads, n_active, u3, lp, n_rows):
    n = u3.shape[0]
    return pl.pallas_call(
        _dispatch_kernel,
        name="dispatch",
        grid_spec=pltpu.PrefetchScalarGridSpec(
            num_scalar_prefetch=6,
            grid=(n // TM_MID,),
            in_specs=[
                pl.BlockSpec((TM_MID, D_MODEL), lambda i, *_: (i, 0)),
                pl.BlockSpec((TOP_K, TM_MID), lambda i, *_: (0, i)),
            ],
            out_specs=pl.BlockSpec(memory_space=pl.ANY),
            scratch_shapes=[
                pltpu.VMEM((LOCAL_ROWS, D_MODEL), F32),
                pltpu.VMEM((MOE_ROWS, D_MODEL), F32),
                pltpu.SemaphoreType.DMA,
                pltpu.SemaphoreType.DMA,
            ],
        ),
        out_shape=jax.ShapeDtypeStruct((n_rows, D_MODEL), F32),
        compiler_params=_params(("arbitrary",)),
    )(tcnt, lo, gstart, padn, pads, n_active, u3, lp)


PAIR_COLS = 2 * LANES


def _expert_kernel(be_ref, na_ref, x_ref, w1_ref, b1_ref, w2_ref, b2_ref, perm_ref, y_ref, w1p_ref, w2b_ref):
    b = pl.program_id(0)
    active = b < na_ref[0]
    new_expert = jnp.logical_or(b == 0, be_ref[b] != be_ref[jnp.maximum(b - 1, 0)])

    @pl.when(jnp.logical_and(active, new_expert))
    def _():
        for c in range(2 * D_FF // PAIR_COLS):
            cs = slice(c * PAIR_COLS, (c + 1) * PAIR_COLS)
            w1p_ref[:, cs] = _dot(w1_ref[0, :, cs].astype(BF16), perm_ref[...]).astype(BF16)
        w2b_ref[...] = w2_ref[0].astype(BF16)

    @pl.when(active)
    def _():
        x = x_ref[...].astype(BF16)
        hdn = _dot(x, w1p_ref[...]) + b1_ref[0]
        acts = []
        for c in range(2 * D_FF // PAIR_COLS):
            hg = jnp.minimum(hdn[:, c * PAIR_COLS:c * PAIR_COLS + LANES], SWIGLU_LIMIT)
            hl = jnp.clip(hdn[:, c * PAIR_COLS + LANES:(c + 1) * PAIR_COLS], -SWIGLU_LIMIT, SWIGLU_LIMIT)
            acts.append((hg * _sigmoid(SWIGLU_ALPHA * hg) * (hl + 1.0)).astype(BF16))
        y_ref[...] = _dot(jnp.concatenate(acts, axis=1), w2b_ref[...]) + b2_ref[0]

    @pl.when(jnp.logical_not(active))
    def _():
        y_ref[...] = jnp.zeros_like(y_ref)


def _experts(block_exp, n_active, xs, w1, b1p, w2, b2):
    n_rows = xs.shape[0]
    n_blocks = n_rows // MOE_ROWS
    xmap = lambda b, be, na: (jnp.minimum(b, na[0] - 1), 0)
    emap = lambda b, be, na: (be[b], 0, 0)
    perm = np.zeros((PAIR_COLS, PAIR_COLS), np.float32)
    perm[2 * np.arange(LANES), np.arange(LANES)] = 1.0
    perm[2 * np.arange(LANES) + 1, LANES + np.arange(LANES)] = 1.0
    return pl.pallas_call(
        _expert_kernel,
        name="experts",
        grid_spec=pltpu.PrefetchScalarGridSpec(
            num_scalar_prefetch=2,
            grid=(n_blocks,),
            in_specs=[
                pl.BlockSpec((MOE_ROWS, D_MODEL), xmap),
                pl.BlockSpec((1, D_MODEL, 2 * D_FF), emap),
                pl.BlockSpec((1, 1, 2 * D_FF), emap),
                pl.BlockSpec((1, D_FF, D_MODEL), emap),
                pl.BlockSpec((1, 1, D_MODEL), emap),
                pl.BlockSpec((PAIR_COLS, PAIR_COLS), lambda b, be, na: (0, 0)),
            ],
            out_specs=pl.BlockSpec((MOE_ROWS, D_MODEL), lambda b, be, na: (b, 0)),
            scratch_shapes=[
                pltpu.VMEM((D_MODEL, 2 * D_FF), BF16),
                pltpu.VMEM((D_FF, D_MODEL), BF16),
            ],
        ),
        out_shape=jax.ShapeDtypeStruct((n_rows, D_MODEL), F32),
        compiler_params=_params(("arbitrary",)),
    )(block_exp, n_active, xs, w1, b1p, w2, b2, jnp.asarray(perm).astype(BF16))


def _combine_kernel(tcnt_ref, lo_ref, gs_ref, h2_ref, gate_ref, lp_ref, g_ref, ys_ref, o_ref, yloc_ref, sem):
    i = pl.program_id(0)

    @pl.when(i == 0)
    def _():
        yloc_ref[...] = jnp.zeros_like(yloc_ref)

    def runs(wait):
        def body(e, carry):
            j = i * N_EXPERTS + e
            _run_copies(tcnt_ref[j], ys_ref, gs_ref[j], yloc_ref, lo_ref[j], sem, RUN_BITS, wait)
            return carry
        lax.fori_loop(0, N_EXPERTS, body, 0)

    runs(False)
    gates = gate_ref[...].T
    lp = lp_ref[...].astype(F32).T
    runs(True)

    h = h2_ref[...]
    for c in range(LOCAL_ROWS // ROW_CHUNK):
        lane = (lax.broadcasted_iota(I32, (TM_MID, ROW_CHUNK), 1) + c * ROW_CHUNK).astype(F32)
        a = jnp.zeros((TM_MID, ROW_CHUNK), F32)
        for kk in range(TOP_K):
            a = jnp.where(lp[:, kk:kk + 1] == lane, gates[:, kk:kk + 1], a)
        h = h + _dot(a.astype(BF16), yloc_ref[c * ROW_CHUNK:(c + 1) * ROW_CHUNK, :].astype(BF16))
    o_ref[...] = _rms(h, g_ref[...])


def _combine(tcnt, lo, gstart, h2, gates, lp, g_final, ys):
    n = h2.shape[0]
    return pl.pallas_call(
        _combine_kernel,
        name="combine",
        grid_spec=pltpu.PrefetchScalarGridSpec(
            num_scalar_prefetch=3,
            grid=(n // TM_MID,),
            in_specs=[
                pl.BlockSpec((TM_MID, D_MODEL), lambda i, *_: (i, 0)),
                pl.BlockSpec((TOP_K, TM_MID), lambda i, *_: (0, i)),
                pl.BlockSpec((TOP_K, TM_MID), lambda i, *_: (0, i)),
                pl.BlockSpec((1, D_MODEL), lambda i, *_: (0, 0)),
                pl.BlockSpec(memory_space=pl.ANY),
            ],
            out_specs=pl.BlockSpec((TM_MID, D_MODEL), lambda i, *_: (i, 0)),
            scratch_shapes=[
                pltpu.VMEM((LOCAL_ROWS, D_MODEL), F32),
                pltpu.SemaphoreType.DMA,
            ],
        ),
        out_shape=jax.ShapeDtypeStruct((n, D_MODEL), F32),
        compiler_params=_params(("arbitrary",)),
    )(tcnt, lo, gstart, h2, gates, lp, g_final, ys)


def _layer(h, mem2, batch, seq, mem_len, g_mix, w_in, b_qkv, conv_w, conv_b, dt_bias, a_log, d_skip,
           g_ssm_out, attn_sinks, bias, w_out, b_out, g_cross, g_mem, w_q_cross, w_kv_cross, w_o_cross,
           g_ffn, w_router, b_router, w1, b1, w2, b2, g_final):
    n = h.shape[0]
    c0, c1, c2 = SSM_WIDTH, SSM_WIDTH + CONV_CH, SSM_WIDTH + CONV_CH + SSM_HEADS
    w_cat = jnp.concatenate(
        [w_in[:, :c1], jnp.pad(w_in[:, c1:c2], ((0, 0), (0, DT_PAD - SSM_HEADS))), w_in[:, c2:]],
        axis=1).astype(BF16)
    z, xbc, dtp, qkv = _inproj(h, g_mix[None, :], w_cat, b_qkv[None, :])

    pad_h = (0, DT_PAD - SSM_HEADS)
    dt_bias_p = jnp.pad(dt_bias, pad_h)[None, :]
    a_head_p = jnp.pad(-jnp.exp(a_log), pad_h)[None, :]
    d_full = jnp.repeat(d_skip, SSM_HEAD_DIM)[None, :]
    y_ssm = _ssd(xbc, z, dtp, conv_w, conv_b[None, :], dt_bias_p, a_head_p, d_full,
                 g_ssm_out[None, :], batch, seq)
    y_attn = _swa(qkv, attn_sinks, bias, batch, seq)

    k_mem, v_mem = _memkv(mem2, g_mem[None, :], w_kv_cross.astype(BF16))
    h2, u3, gates, lp, tcnt, tbase = _mid(
        y_ssm, y_attn, h, w_out.astype(BF16), b_out[None, :], g_cross[None, :],
        w_q_cross.astype(BF16), k_mem, v_mem, w_o_cross.astype(BF16), g_ffn[None, :],
        w_router.T.astype(BF16), b_router[:, None], seq, mem_len)

    max_rows = n * TOP_K + (n // TM_MID) * N_EXPERTS * (SUBLANES - 1)
    n_blocks = -(-max_rows // MOE_ROWS) + N_EXPERTS
    tcnt = tcnt[:, :, 0]
    tbase = tbase[:, :, 0]
    counts = tbase[-1] + tcnt[-1]
    padded = (counts + MOE_ROWS - 1) // MOE_ROWS * MOE_ROWS
    padded_end = jnp.cumsum(padded)
    start = (padded_end - padded).astype(I32)
    n_active = (padded_end[-1] // MOE_ROWS).astype(I32)
    blk = jnp.arange(n_blocks, dtype=I32)
    blk_exp = jnp.sum((blk * MOE_ROWS)[:, None] >= padded_end[None, :], axis=1)
    blk_exp = jnp.minimum(blk_exp, N_EXPERTS - 1).astype(I32)
    blk_exp = jnp.where(blk < n_active, blk_exp, blk_exp[n_active - 1])

    run_n = tcnt.reshape(-1)
    run_local = (jnp.cumsum(tcnt, axis=1) - tcnt).reshape(-1)
    run_global = (start[None, :] + tbase).reshape(-1)
    xs = _dispatch(run_n, run_local, run_global, padded - counts, start + counts, n_active[None], u3, lp,
                   n_blocks * MOE_ROWS)

    b1p = b1.reshape(N_EXPERTS, 2 * D_FF // PAIR_COLS, LANES, 2).transpose(0, 1, 3, 2)
    ys = _experts(blk_exp, n_active[None], xs, w1, b1p.reshape(N_EXPERTS, 1, 2 * D_FF), w2, b2[:, None, :])
    return _combine(run_n, run_local, run_global, h2, gates, lp, g_final, ys)


def kernel(x, mem, g_mix, w_in, b_qkv, conv_w, conv_b, dt_bias, a_log, d_skip, g_ssm_out, attn_sinks, rel_bias, w_out, b_out, g_cross, g_mem, w_q_cross, w_kv_cross, w_o_cross, g_ffn, w_router, b_router, w1, b1, w2, b2, g_final):
    batch, seq, d = x.shape
    mem_len = mem.shape[1]
    depth = g_mix.shape[0]
    assert depth == 1 and d == D_MODEL and seq % TM_MID == 0
    h = x.reshape(batch * seq, d)
    mem2 = mem.reshape(batch * mem_len, d)
    bias = _bias_table(rel_bias)
    out = _layer(
        h, mem2, batch, seq, mem_len, g_mix[0], w_in[0], b_qkv[0], conv_w[0], conv_b[0], dt_bias[0],
        a_log[0], d_skip[0], g_ssm_out[0], attn_sinks[0], bias, w_out[0], b_out[0], g_cross[0],
        g_mem[0], w_q_cross[0], w_kv_cross[0], w_o_cross[0], g_ffn[0], w_router[0], b_router[0],
        w1[0], b1[0], w2[0], b2[0], g_final[None, :])
    return out.reshape(batch, seq, d)
```

```python
import functools
import math

import numpy as np
import jax
import jax.numpy as jnp
from jax import lax
from jax.experimental import pallas as pl
from jax.experimental.pallas import tpu as pltpu

F32 = jnp.float32
BF16 = jnp.bfloat16
I32 = jnp.int32

D_MODEL = 1024
RMS_EPS = 1e-5

SSM_HEADS = 8
SSM_HEAD_DIM = 64
SSM_WIDTH = SSM_HEADS * SSM_HEAD_DIM
SSM_STATE = 128
SSM_GROUPS = 2
HEADS_PER_GROUP = SSM_HEADS // SSM_GROUPS
GROUP_WIDTH = SSM_WIDTH // SSM_GROUPS
CONV_WIDTH = 4
CHUNK = 128
CONV_CH = SSM_WIDTH + 2 * SSM_GROUPS * SSM_STATE

ATTN_HEADS = 8
KV_HEADS = 2
Q_PER_KV = ATTN_HEADS // KV_HEADS
HEAD_DIM = 64
ATTN_WIDTH = ATTN_HEADS * HEAD_DIM
KV_WIDTH = KV_HEADS * HEAD_DIM
WINDOW = 128
BLOCK = WINDOW
QKV_COLS = ATTN_WIDTH + 2 * KV_WIDTH

REL_BUCKETS = 32
REL_MAX_DIST = 128

CROSS_HEADS = 4
CROSS_HEAD_DIM = 128
CROSS_WIDTH = CROSS_HEADS * CROSS_HEAD_DIM

N_EXPERTS = 32
TOP_K = 4
D_FF = D_MODEL
SWIGLU_ALPHA = 1.702
SWIGLU_LIMIT = 7.0

LANES = 128
SUBLANES = 8
DT_PAD = LANES
PROJ_COLS = SSM_WIDTH + CONV_CH + DT_PAD + QKV_COLS
VMEM_LIMIT = 56 * 1024 * 1024

TM_PROJ = 512
TM_MID = 512
TM_SCATTER = 512
TM_FINAL = 256
MOE_ROWS = 256
DMA_UNROLL = 8


def _nt(a, b):
    return lax.dot_general(a, b, (((1,), (1,)), ((), ())), preferred_element_type=F32)


def _tn(a, b):
    return lax.dot_general(a, b, (((0,), (0,)), ((), ())), preferred_element_type=F32)


def _dot(a, b):
    return jnp.dot(a, b, preferred_element_type=F32)


def _dot_exact(a, b):
    return jnp.dot(a, b, preferred_element_type=F32, precision=lax.Precision.HIGHEST)


def _rms(x, g):
    return x * lax.rsqrt(jnp.mean(x * x, axis=-1, keepdims=True) + RMS_EPS) * g


def _sigmoid(x):
    return 1.0 / (1.0 + jnp.exp(-x))


def _params(sem=None):
    return pltpu.CompilerParams(dimension_semantics=sem, vmem_limit_bytes=VMEM_LIMIT)


def _inproj_kernel(x_ref, g_ref, w_ref, bq_ref, z_ref, xbc_ref, dt_ref, qkv_ref):
    u = _rms(x_ref[...], g_ref[...])
    p = _dot(u.astype(BF16), w_ref[...])
    z_ref[...] = p[:, :SSM_WIDTH]
    xbc_ref[...] = p[:, SSM_WIDTH:SSM_WIDTH + CONV_CH]
    dt_ref[...] = p[:, SSM_WIDTH + CONV_CH:SSM_WIDTH + CONV_CH + DT_PAD]
    qkv_ref[...] = p[:, SSM_WIDTH + CONV_CH + DT_PAD:] + bq_ref[...]


def _inproj(x2, g_mix, w_cat, b_qkv):
    n = x2.shape[0]
    row = lambda i: (i, 0)
    fixed = lambda i: (0, 0)
    return pl.pallas_call(
        _inproj_kernel,
        grid=(n // TM_PROJ,),
        in_specs=[
            pl.BlockSpec((TM_PROJ, D_MODEL), row),
            pl.BlockSpec((1, D_MODEL), fixed),
            pl.BlockSpec((D_MODEL, PROJ_COLS), fixed),
            pl.BlockSpec((1, QKV_COLS), fixed),
        ],
        out_specs=[
            pl.BlockSpec((TM_PROJ, SSM_WIDTH), row),
            pl.BlockSpec((TM_PROJ, CONV_CH), row),
            pl.BlockSpec((TM_PROJ, DT_PAD), row),
            pl.BlockSpec((TM_PROJ, QKV_COLS), row),
        ],
        out_shape=[
            jax.ShapeDtypeStruct((n, SSM_WIDTH), F32),
            jax.ShapeDtypeStruct((n, CONV_CH), F32),
            jax.ShapeDtypeStruct((n, DT_PAD), F32),
            jax.ShapeDtypeStruct((n, QKV_COLS), F32),
        ],
        compiler_params=_params(("arbitrary",)),
    )(x2, g_mix, w_cat, b_qkv)


def _ssd_kernel(xbc_ref, z_ref, dt_ref, cw_ref, cb_ref, dtb_ref, ah_ref, dsk_ref, gout_ref,
                e64_ref, e128_ref, tri_ref, y_ref, prev_ref, state_ref):
    c = pl.program_id(1)

    @pl.when(c == 0)
    def _():
        prev_ref[...] = jnp.zeros_like(prev_ref)
        state_ref[...] = jnp.zeros_like(state_ref)

    u = xbc_ref[...]
    pv = prev_ref[...]
    row = lax.broadcasted_iota(I32, (CHUNK, 1), 0)
    acc = u * cw_ref[CONV_WIDTH - 1:CONV_WIDTH, :] + cb_ref[...]
    for j in range(1, CONV_WIDTH):
        shifted = jnp.where(row >= j, pltpu.roll(u, j, 0), pltpu.roll(pv, j, 0))
        acc = acc + shifted * cw_ref[CONV_WIDTH - 1 - j:CONV_WIDTH - j, :]
    prev_ref[...] = u
    xbc = acc * _sigmoid(acc)

    xs = xbc[:, :SSM_WIDTH]
    bm = xbc[:, SSM_WIDTH:SSM_WIDTH + SSM_GROUPS * SSM_STATE]
    cm = xbc[:, SSM_WIDTH + SSM_GROUPS * SSM_STATE:]

    dtr = dt_ref[...] + dtb_ref[...]
    dt = jnp.maximum(dtr, 0.0) + jnp.log1p(jnp.exp(-jnp.abs(dtr)))
    a = dt * ah_ref[...]
    cs = _dot_exact(tri_ref[...], a)
    cs_row = cs.T
    dt_full = _dot_exact(dt, e64_ref[...])
    cs_full = _dot_exact(cs, e64_ref[...])
    cs_wide = _dot_exact(cs, e128_ref[...])
    cs_last = cs_full[CHUNK - 1:CHUNK, :]

    xdt = xs * dt_full
    xdt_b = xdt.astype(BF16)
    xw_b = (xdt * jnp.exp(cs_last - cs_full)).astype(BF16)
    in_decay = jnp.exp(cs_full)
    chunk_decay = jnp.exp(cs_last)

    li = lax.broadcasted_iota(I32, (CHUNK, CHUNK), 0)
    si = lax.broadcasted_iota(I32, (CHUNK, CHUNK), 1)
    causal = li >= si

    ys = []
    for g in range(SSM_GROUPS):
        bg = bm[:, g * SSM_STATE:(g + 1) * SSM_STATE].astype(BF16)
        cg = cm[:, g * SSM_STATE:(g + 1) * SSM_STATE].astype(BF16)
        cb = _nt(cg, bg)
        yd = []
        for r in range(HEADS_PER_GROUP):
            h = g * HEADS_PER_GROUP + r
            diff = cs_wide[:, h * CHUNK:(h + 1) * CHUNK] - cs_row[h:h + 1, :]
            m = cb * jnp.exp(jnp.where(causal, diff, -jnp.inf))
            yd.append(_dot(m.astype(BF16), xdt_b[:, h * SSM_HEAD_DIM:(h + 1) * SSM_HEAD_DIM]))
        y_diag = jnp.concatenate(yd, axis=1)
        gs = slice(g * GROUP_WIDTH, (g + 1) * GROUP_WIDTH)
        st = state_ref[g]
        y_off = _dot(cg, st.astype(BF16)) * in_decay[:, gs]
        state_ref[g] = st * chunk_decay[:, gs] + _tn(bg, xw_b[:, gs])
        ys.append(y_diag + y_off)
    y = jnp.concatenate(ys, axis=1) + dsk_ref[...] * xs

    zz = z_ref[...]
    y = y * (zz * _sigmoid(zz))
    outs = []
    for g in range(SSM_GROUPS):
        yg = y[:, g * GROUP_WIDTH:(g + 1) * GROUP_WIDTH]
        outs.append(yg * lax.rsqrt(jnp.mean(yg * yg, axis=-1, keepdims=True) + RMS_EPS))
    y_ref[...] = jnp.concatenate(outs, axis=1) * gout_ref[...]


def _ssd(xbc, z, dtp, conv_w, conv_b, dt_bias_p, a_head_p, d_full, g_out, batch, seq):
    n = xbc.shape[0]
    nc = seq // CHUNK
    row = lambda b, c: (b * nc + c, 0)
    fixed = lambda b, c: (0, 0)
    head = np.arange(SSM_WIDTH) // SSM_HEAD_DIM
    e64 = jnp.asarray((np.arange(DT_PAD)[:, None] == head[None, :]).astype(np.float32))
    head_w = np.arange(SSM_HEADS * CHUNK) // CHUNK
    e128 = jnp.asarray((np.arange(DT_PAD)[:, None] == head_w[None, :]).astype(np.float32))
    tri = jnp.asarray(np.tril(np.ones((CHUNK, CHUNK), np.float32)))
    return pl.pallas_call(
        _ssd_kernel,
        grid=(batch, nc),
        in_specs=[
            pl.BlockSpec((CHUNK, CONV_CH), row),
            pl.BlockSpec((CHUNK, SSM_WIDTH), row),
            pl.BlockSpec((CHUNK, DT_PAD), row),
            pl.BlockSpec((CONV_WIDTH, CONV_CH), fixed),
            pl.BlockSpec((1, CONV_CH), fixed),
            pl.BlockSpec((1, DT_PAD), fixed),
            pl.BlockSpec((1, DT_PAD), fixed),
            pl.BlockSpec((1, SSM_WIDTH), fixed),
            pl.BlockSpec((1, SSM_WIDTH), fixed),
            pl.BlockSpec((DT_PAD, SSM_WIDTH), fixed),
            pl.BlockSpec((DT_PAD, SSM_HEADS * CHUNK), fixed),
            pl.BlockSpec((CHUNK, CHUNK), fixed),
        ],
        out_specs=pl.BlockSpec((CHUNK, SSM_WIDTH), row),
        out_shape=jax.ShapeDtypeStruct((n, SSM_WIDTH), F32),
        scratch_shapes=[
            pltpu.VMEM((CHUNK, CONV_CH), F32),
            pltpu.VMEM((SSM_GROUPS, SSM_STATE, GROUP_WIDTH), F32),
        ],
        compiler_params=_params(("arbitrary", "arbitrary")),
    )(xbc, z, dtp, conv_w, conv_b, dt_bias_p, a_head_p, d_full, g_out, e64, e128, tri)


def _t5_bucket_table():
    q_pos = np.arange(BLOCK)[:, None]
    k_pos = np.arange(2 * BLOCK)[None, :] - BLOCK
    dist = q_pos - k_pos
    d = np.maximum(dist, 0)
    max_exact = REL_BUCKETS // 2
    ratio = np.maximum(d, max_exact).astype(np.float32) / np.float32(max_exact)
    large = max_exact + (np.log(ratio) / np.float32(math.log(REL_MAX_DIST / max_exact))
                         * np.float32(REL_BUCKETS - max_exact)).astype(np.int32)
    large = np.minimum(large, REL_BUCKETS - 1)
    bucket = np.where(d < max_exact, d, large)
    in_window = (dist >= 0) & (dist < WINDOW)
    return np.where(in_window, bucket, -1).astype(np.int32)


def _bias_kernel(rb_ref, bucket_ref, out_ref):
    bucket = bucket_ref[...]
    for h in range(ATTN_HEADS):
        acc = jnp.zeros(bucket.shape, F32)
        for b in range(REL_BUCKETS):
            acc = jnp.where(bucket == b, rb_ref[b, h], acc)
        out_ref[h] = jnp.where(bucket >= 0, acc, -jnp.inf)


def _bias_table(rel_bias):
    bucket = jnp.asarray(_t5_bucket_table())
    return pl.pallas_call(
        _bias_kernel,
        in_specs=[
            pl.BlockSpec(memory_space=pltpu.SMEM),
            pl.BlockSpec(memory_space=pltpu.VMEM),
        ],
        out_specs=pl.BlockSpec(memory_space=pltpu.VMEM),
        out_shape=jax.ShapeDtypeStruct((ATTN_HEADS, BLOCK, 2 * BLOCK), F32),
    )(rel_bias, bucket)


def _swa_kernel(sink_ref, q_ref, k_ref, v_ref, kp_ref, vp_ref, bias_ref, o_ref):
    i = pl.program_id(1)
    q = (q_ref[...] * (HEAD_DIM ** -0.5)).astype(BF16)
    col = lax.broadcasted_iota(I32, (BLOCK, 2 * BLOCK), 1)
    key_ok = jnp.logical_or(col >= BLOCK, i > 0)
    outs = []
    for g in range(KV_HEADS):
        gs = slice(g * HEAD_DIM, (g + 1) * HEAD_DIM)
        k2 = jnp.concatenate([kp_ref[:, gs], k_ref[:, gs]], axis=0).astype(BF16)
        v2 = jnp.concatenate([vp_ref[:, gs], v_ref[:, gs]], axis=0).astype(BF16)
        for r in range(Q_PER_KV):
            h = g * Q_PER_KV + r
            s = _nt(q[:, h * HEAD_DIM:(h + 1) * HEAD_DIM], k2) + bias_ref[h]
            s = jnp.where(key_ok, s, -jnp.inf)
            sink = sink_ref[h]
            m = jnp.maximum(jnp.max(s, axis=-1, keepdims=True), sink)
            p = jnp.exp(s - m)
            denom = jnp.sum(p, axis=-1, keepdims=True) + jnp.exp(sink - m)
            outs.append(_dot(p.astype(BF16), v2) / denom)
    o_ref[...] = jnp.concatenate(outs, axis=1)


def _swa(qkv, sinks, bias, batch, seq):
    n = qkv.shape[0]
    nb = seq // BLOCK
    kcol = ATTN_WIDTH // KV_WIDTH
    return pl.pallas_call(
        _swa_kernel,
        grid_spec=pltpu.PrefetchScalarGridSpec(
            num_scalar_prefetch=1,
            grid=(batch, nb),
            in_specs=[
                pl.BlockSpec((BLOCK, ATTN_WIDTH), lambda b, i, s: (b * nb + i, 0)),
                pl.BlockSpec((BLOCK, KV_WIDTH), lambda b, i, s: (b * nb + i, kcol)),
                pl.BlockSpec((BLOCK, KV_WIDTH), lambda b, i, s: (b * nb + i, kcol + 1)),
                pl.BlockSpec((BLOCK, KV_WIDTH), lambda b, i, s: (b * nb + jnp.maximum(i - 1, 0), kcol)),
                pl.BlockSpec((BLOCK, KV_WIDTH), lambda b, i, s: (b * nb + jnp.maximum(i - 1, 0), kcol + 1)),
                pl.BlockSpec((ATTN_HEADS, BLOCK, 2 * BLOCK), lambda b, i, s: (0, 0, 0)),
            ],
            out_specs=pl.BlockSpec((BLOCK, ATTN_WIDTH), lambda b, i, s: (b * nb + i, 0)),
        ),
        out_shape=jax.ShapeDtypeStruct((n, ATTN_WIDTH), F32),
        compiler_params=_params(("arbitrary", "arbitrary")),
    )(sinks, qkv, qkv, qkv, qkv, qkv, bias)


def _memkv_kernel(m_ref, g_ref, w_ref, k_ref, v_ref):
    u = _rms(m_ref[...], g_ref[...])
    kv = _dot(u.astype(BF16), w_ref[...])
    k_ref[...] = kv[:, :CROSS_WIDTH].astype(BF16)
    v_ref[...] = kv[:, CROSS_WIDTH:].astype(BF16)


def _memkv(mem2, g_mem, w_kv):
    n = mem2.shape[0]
    return pl.pallas_call(
        _memkv_kernel,
        out_shape=[jax.ShapeDtypeStruct((n, CROSS_WIDTH), BF16)] * 2,
        compiler_params=_params(),
    )(mem2, g_mem, w_kv)


def _mid_kernel(ys_ref, ya_ref, x_ref, wout_ref, bout_ref, gc_ref, wq_ref, k_ref, v_ref, wo_ref,
                gf_ref, wr_ref, br_ref, upper_ref, lower_ref,
                h2_ref, u3_ref, gate_ref, lp_ref, tcnt_ref, tbase_ref, run_ref):
    i = pl.program_id(0)

    @pl.when(i == 0)
    def _():
        run_ref[...] = jnp.zeros_like(run_ref)

    ycat = jnp.concatenate([ys_ref[...], ya_ref[...]], axis=1).astype(BF16)
    h1 = x_ref[...] + _dot(ycat, wout_ref[...]) + bout_ref[...]

    u2 = _rms(h1, gc_ref[...])
    q = _dot(u2.astype(BF16), wq_ref[...]).astype(BF16)
    outs = []
    for h in range(CROSS_HEADS):
        hs = slice(h * CROSS_HEAD_DIM, (h + 1) * CROSS_HEAD_DIM)
        s = _nt(q[:, hs], k_ref[:, hs]) * (CROSS_HEAD_DIM ** -0.5)
        m = jnp.max(s, axis=-1, keepdims=True)
        p = jnp.exp(s - m)
        outs.append(_dot(p.astype(BF16), v_ref[:, hs]) / jnp.sum(p, axis=-1, keepdims=True))
    o = jnp.concatenate(outs, axis=1).astype(BF16)
    h2 = h1 + _dot(o, wo_ref[...])
    h2_ref[...] = h2

    u3 = _rms(h2, gf_ref[...]).astype(BF16)
    u3_ref[...] = u3

    logits = _nt(wr_ref[...], u3) + br_ref[...]
    eid = lax.broadcasted_iota(I32, logits.shape, 0)
    vals, idxs, hots = [], [], []
    for _ in range(TOP_K):
        m = jnp.max(logits, axis=0, keepdims=True)
        ix = jnp.min(jnp.where(logits == m, eid, N_EXPERTS), axis=0, keepdims=True)
        hot = eid == ix
        logits = jnp.where(hot, -jnp.inf, logits)
        vals.append(m)
        idxs.append(ix)
        hots.append(hot)
    ex = [jnp.exp(v - vals[0]) for v in vals]
    tot = ex[0] + ex[1] + ex[2] + ex[3]
    gate_ref[...] = jnp.concatenate([e / tot for e in ex], axis=0)

    hot_f = [jnp.where(hot, 1.0, 0.0) for hot in hots]
    cnt_k = [jnp.sum(hf, axis=1, keepdims=True) for hf in hot_f]
    tile_cnt = cnt_k[0] + cnt_k[1] + cnt_k[2] + cnt_k[3]
    tile_cnt = jnp.floor((tile_cnt + (SUBLANES - 1)) * (1.0 / SUBLANES)) * SUBLANES
    offs = _dot_exact(lower_ref[...], jnp.broadcast_to(tile_cnt, (N_EXPERTS, LANES)))[:, :1]
    local = []
    for kk in range(TOP_K):
        before = _dot(hot_f[kk].astype(BF16), upper_ref[...])
        local.append(jnp.sum(jnp.where(hots[kk], before + offs, 0.0), axis=0, keepdims=True))
        offs = offs + cnt_k[kk]
    lp_ref[...] = jnp.concatenate(local, axis=0).astype(I32)
    tcnt_ref[0] = jnp.broadcast_to(tile_cnt, (N_EXPERTS, LANES)).astype(I32)
    tbase_ref[0] = jnp.broadcast_to(run_ref[...], (N_EXPERTS, LANES)).astype(I32)
    run_ref[...] = run_ref[...] + tile_cnt


def _mid(y_ssm, y_attn, x2, w_out, b_out, g_cross, w_q, k_mem, v_mem, w_o, g_ffn, w_r_t, b_r, seq, mem_len):
    n = x2.shape[0]
    tiles_per_batch = seq // TM_MID
    row = lambda i: (i, 0)
    col = lambda i: (0, i)
    fixed = lambda i: (0, 0)
    memb = lambda i: (i // tiles_per_batch, 0)
    tile3 = lambda i: (i, 0, 0)
    n_tiles = n // TM_MID
    upper = jnp.asarray(np.triu(np.ones((TM_MID, TM_MID), np.float32), 1)).astype(BF16)
    lower = jnp.asarray(np.tril(np.ones((N_EXPERTS, N_EXPERTS), np.float32), -1))
    return pl.pallas_call(
        _mid_kernel,
        name="mid",
        grid=(n_tiles,),
        in_specs=[
            pl.BlockSpec((TM_MID, SSM_WIDTH), row),
            pl.BlockSpec((TM_MID, ATTN_WIDTH), row),
            pl.BlockSpec((TM_MID, D_MODEL), row),
            pl.BlockSpec((D_MODEL, D_MODEL), fixed),
            pl.BlockSpec((1, D_MODEL), fixed),
            pl.BlockSpec((1, D_MODEL), fixed),
            pl.BlockSpec((D_MODEL, CROSS_WIDTH), fixed),
            pl.BlockSpec((mem_len, CROSS_WIDTH), memb),
            pl.BlockSpec((mem_len, CROSS_WIDTH), memb),
            pl.BlockSpec((CROSS_WIDTH, D_MODEL), fixed),
            pl.BlockSpec((1, D_MODEL), fixed),
            pl.BlockSpec((N_EXPERTS, D_MODEL), fixed),
            pl.BlockSpec((N_EXPERTS, 1), fixed),
            pl.BlockSpec((TM_MID, TM_MID), fixed),
            pl.BlockSpec((N_EXPERTS, N_EXPERTS), fixed),
        ],
        out_specs=[
            pl.BlockSpec((TM_MID, D_MODEL), row),
            pl.BlockSpec((TM_MID, D_MODEL), row),
            pl.BlockSpec((TOP_K, TM_MID), col),
            pl.BlockSpec((TOP_K, TM_MID), col),
            pl.BlockSpec((1, N_EXPERTS, LANES), tile3),
            pl.BlockSpec((1, N_EXPERTS, LANES), tile3),
        ],
        out_shape=[
            jax.ShapeDtypeStruct((n, D_MODEL), F32),
            jax.ShapeDtypeStruct((n, D_MODEL), BF16),
            jax.ShapeDtypeStruct((TOP_K, n), F32),
            jax.ShapeDtypeStruct((TOP_K, n), I32),
            jax.ShapeDtypeStruct((n_tiles, N_EXPERTS, LANES), I32),
            jax.ShapeDtypeStruct((n_tiles, N_EXPERTS, LANES), I32),
        ],
        scratch_shapes=[pltpu.VMEM((N_EXPERTS, 1), F32)],
        compiler_params=_params(("arbitrary",)),
    )(y_ssm, y_attn, x2, w_out, b_out, g_cross, w_q, k_mem, v_mem, w_o, g_ffn, w_r_t, b_r, upper, lower)


ROW_CHUNK = 256
LOCAL_ROWS = -(-(TM_MID * TOP_K + N_EXPERTS * (SUBLANES - 1)) // ROW_CHUNK) * ROW_CHUNK
RUN_BITS = tuple(1 << s for s in range(TM_MID.bit_length() - 1, SUBLANES.bit_length() - 2, -1))
PAD_BITS = tuple(1 << s for s in range((MOE_ROWS - 1).bit_length() - 1, SUBLANES.bit_length() - 2, -1))


def _run_copies(n, src_ref, src, dst_ref, dst, sem, bits, wait, advance_src=True):
    for bit in bits:
        step = n & bit

        @pl.when(step != 0)
        def _():
            cp = pltpu.make_async_copy(src_ref.at[pl.ds(pl.multiple_of(src, SUBLANES), bit)],
                                       dst_ref.at[pl.ds(pl.multiple_of(dst, SUBLANES), bit)], sem)
            if wait:
                cp.wait()
            else:
                cp.start()

        if advance_src:
            src = src + step
        dst = dst + step


def _dispatch_kernel(tcnt_ref, lo_ref, gs_ref, padn_ref, pads_ref, na_ref, u_ref, lp_ref, xs_ref,
                     xloc_ref, zero_ref, sem, zsem):
    i = pl.program_id(0)

    def zero_fill(wait):
        def body(e, carry):
            _run_copies(padn_ref[e], zero_ref, 0, xs_ref, pads_ref[e], zsem, PAD_BITS, wait, advance_src=False)
            return carry
        lax.fori_loop(0, N_EXPERTS, body, 0)

        def tail(b, carry):
            cp = pltpu.make_async_copy(zero_ref, xs_ref.at[pl.ds(pl.multiple_of(b * MOE_ROWS, MOE_ROWS), MOE_ROWS)], zsem)
            if wait:
                cp.wait()
            else:
                cp.start()
            return carry
        lax.fori_loop(na_ref[0], xs_ref.shape[0] // MOE_ROWS, tail, 0)

    @pl.when(i == 0)
    def _():
        zero_ref[...] = jnp.zeros_like(zero_ref)
        zero_fill(False)

    slot = i % 2
    u = u_ref[...]
    lp = lp_ref[...]
    for c in range(LOCAL_ROWS // ROW_CHUNK):
        r = lax.broadcasted_iota(I32, (ROW_CHUNK, TM_MID), 0) + c * ROW_CHUNK
        p = jnp.zeros((ROW_CHUNK, TM_MID), F32)
        for kk in range(TOP_K):
            p = jnp.where(lp[kk:kk + 1, :] == r, 1.0, p)
        xloc_ref[slot, c * ROW_CHUNK:(c + 1) * ROW_CHUNK, :] = _dot(p.astype(BF16), u)

    def runs(tile, wait):
        s = tile % 2

        def body(e, carry):
            j = tile * N_EXPERTS + e
            _run_copies(tcnt_ref[j], xloc_ref.at[s], lo_ref[j], xs_ref, gs_ref[j], sem.at[s], RUN_BITS, wait)
            return carry
        lax.fori_loop(0, N_EXPERTS, body, 0)

    runs(i, False)

    @pl.when(i > 0)
    def _():
        runs(i - 1, True)

    @pl.when(i == pl.num_programs(0) - 1)
    def _():
        runs(i, True)

    @pl.when(i == 0)
    def _():
        zero_fill(True)


def _dispatch(tcnt, lo, gstart, padn, pads, n_active, u3, lp, n_rows):
    n = u3.shape[0]
    return pl.pallas_call(
        _dispatch_kernel,
        name="dispatch",
        grid_spec=pltpu.PrefetchScalarGridSpec(
            num_scalar_prefetch=6,
            grid=(n // TM_MID,),
            in_specs=[
                pl.BlockSpec((TM_MID, D_MODEL), lambda i, *_: (i, 0)),
                pl.BlockSpec((TOP_K, TM_MID), lambda i, *_: (0, i)),
            ],
            out_specs=pl.BlockSpec(memory_space=pl.ANY),
            scratch_shapes=[
                pltpu.VMEM((2, LOCAL_ROWS, D_MODEL), F32),
                pltpu.VMEM((MOE_ROWS, D_MODEL), F32),
                pltpu.SemaphoreType.DMA((2,)),
                pltpu.SemaphoreType.DMA,
            ],
        ),
        out_shape=jax.ShapeDtypeStruct((n_rows, D_MODEL), F32),
        compiler_params=_params(("arbitrary",)),
    )(tcnt, lo, gstart, padn, pads, n_active, u3, lp)


PAIR_COLS = 2 * LANES


def _expert_kernel(be_ref, nx_ref, na_ref, x_ref, b1_ref, b2_ref, perm_ref, w1_hbm, w2_hbm, y_ref,
                   w1s_ref, w2s_ref, w1p_ref, w2b_ref, sem):
    b = pl.program_id(0)
    active = b < na_ref[0]
    expert = be_ref[b]
    new_expert = jnp.logical_or(b == 0, expert != be_ref[jnp.maximum(b - 1, 0)])

    def fetch(e, wait):
        for src, dst, s in ((w1_hbm, w1s_ref, 0), (w2_hbm, w2s_ref, 1)):
            cp = pltpu.make_async_copy(src.at[e], dst, sem.at[s])
            if wait:
                cp.wait()
            else:
                cp.start()

    @pl.when(b == 0)
    def _():
        fetch(expert, False)

    @pl.when(jnp.logical_and(active, new_expert))
    def _():
        fetch(expert, True)
        for c in range(2 * D_FF // PAIR_COLS):
            cs = slice(c * PAIR_COLS, (c + 1) * PAIR_COLS)
            w1p_ref[:, cs] = _dot(w1s_ref[:, cs].astype(BF16), perm_ref[...]).astype(BF16)
        w2b_ref[...] = w2s_ref[...].astype(BF16)

        @pl.when(nx_ref[b] >= 0)
        def _():
            fetch(nx_ref[b], False)

    @pl.when(active)
    def _():
        x = x_ref[...].astype(BF16)
        hdn = _dot(x, w1p_ref[...]) + b1_ref[0]
        acts = []
        for c in range(2 * D_FF // PAIR_COLS):
            hg = jnp.minimum(hdn[:, c * PAIR_COLS:c * PAIR_COLS + LANES], SWIGLU_LIMIT)
            hl = jnp.clip(hdn[:, c * PAIR_COLS + LANES:(c + 1) * PAIR_COLS], -SWIGLU_LIMIT, SWIGLU_LIMIT)
            acts.append((hg * _sigmoid(SWIGLU_ALPHA * hg) * (hl + 1.0)).astype(BF16))
        y_ref[...] = _dot(jnp.concatenate(acts, axis=1), w2b_ref[...]) + b2_ref[0]

    @pl.when(jnp.logical_not(active))
    def _():
        y_ref[...] = jnp.zeros_like(y_ref)


def _experts(block_exp, next_exp, n_active, xs, w1, b1p, w2, b2):
    n_rows = xs.shape[0]
    n_blocks = n_rows // MOE_ROWS
    xmap = lambda b, be, nx, na: (jnp.maximum(jnp.minimum(b, na[0] - 1), 0), 0)
    emap = lambda b, be, nx, na: (be[b], 0, 0)
    perm = np.zeros((PAIR_COLS, PAIR_COLS), np.float32)
    perm[2 * np.arange(LANES), np.arange(LANES)] = 1.0
    perm[2 * np.arange(LANES) + 1, LANES + np.arange(LANES)] = 1.0
    return pl.pallas_call(
        _expert_kernel,
        name="experts",
        grid_spec=pltpu.PrefetchScalarGridSpec(
            num_scalar_prefetch=3,
            grid=(n_blocks,),
            in_specs=[
                pl.BlockSpec((MOE_ROWS, D_MODEL), xmap),
                pl.BlockSpec((1, 1, 2 * D_FF), emap),
                pl.BlockSpec((1, 1, D_MODEL), emap),
                pl.BlockSpec((PAIR_COLS, PAIR_COLS), lambda b, be, nx, na: (0, 0)),
                pl.BlockSpec(memory_space=pl.ANY),
                pl.BlockSpec(memory_space=pl.ANY),
            ],
            out_specs=pl.BlockSpec((MOE_ROWS, D_MODEL), lambda b, be, nx, na: (b, 0)),
            scratch_shapes=[
                pltpu.VMEM((D_MODEL, 2 * D_FF), F32),
                pltpu.VMEM((D_FF, D_MODEL), F32),
                pltpu.VMEM((D_MODEL, 2 * D_FF), BF16),
                pltpu.VMEM((D_FF, D_MODEL), BF16),
                pltpu.SemaphoreType.DMA((2,)),
            ],
        ),
        out_shape=jax.ShapeDtypeStruct((n_rows, D_MODEL), F32),
        compiler_params=_params(("arbitrary",)),
    )(block_exp, next_exp, n_active, xs, b1p, b2, jnp.asarray(perm).astype(BF16), w1, w2)


def _combine_kernel(tcnt_ref, lo_ref, gs_ref, h2_ref, gate_ref, lp_ref, g_ref, ys_ref, o_ref, yloc_ref, sem):
    i = pl.program_id(0)

    slot = i % 2

    def runs(tile, wait):
        s = tile % 2

        def body(e, carry):
            j = tile * N_EXPERTS + e
            _run_copies(tcnt_ref[j], ys_ref, gs_ref[j], yloc_ref.at[s], lo_ref[j], sem.at[s], RUN_BITS, wait)
            return carry
        lax.fori_loop(0, N_EXPERTS, body, 0)

    @pl.when(i == 0)
    def _():
        yloc_ref[...] = jnp.zeros_like(yloc_ref)
        runs(i, False)

    @pl.when(i + 1 < pl.num_programs(0))
    def _():
        runs(i + 1, False)

    gates = gate_ref[...].T
    lp = lp_ref[...].astype(F32).T
    runs(i, True)

    h = h2_ref[...]
    for c in range(LOCAL_ROWS // ROW_CHUNK):
        lane = (lax.broadcasted_iota(I32, (TM_MID, ROW_CHUNK), 1) + c * ROW_CHUNK).astype(F32)
        a = jnp.zeros((TM_MID, ROW_CHUNK), F32)
        for kk in range(TOP_K):
            a = jnp.where(lp[:, kk:kk + 1] == lane, gates[:, kk:kk + 1], a)
        h = h + _dot(a.astype(BF16), yloc_ref[slot, c * ROW_CHUNK:(c + 1) * ROW_CHUNK, :].astype(BF16))
    o_ref[...] = _rms(h, g_ref[...])


def _combine(tcnt, lo, gstart, h2, gates, lp, g_final, ys):
    n = h2.shape[0]
    return pl.pallas_call(
        _combine_kernel,
        name="combine",
        grid_spec=pltpu.PrefetchScalarGridSpec(
            num_scalar_prefetch=3,
            grid=(n // TM_MID,),
            in_specs=[
                pl.BlockSpec((TM_MID, D_MODEL), lambda i, *_: (i, 0)),
                pl.BlockSpec((TOP_K, TM_MID), lambda i, *_: (0, i)),
                pl.BlockSpec((TOP_K, TM_MID), lambda i, *_: (0, i)),
                pl.BlockSpec((1, D_MODEL), lambda i, *_: (0, 0)),
                pl.BlockSpec(memory_space=pl.ANY),
            ],
            out_specs=pl.BlockSpec((TM_MID, D_MODEL), lambda i, *_: (i, 0)),
            scratch_shapes=[
                pltpu.VMEM((2, LOCAL_ROWS, D_MODEL), F32),
                pltpu.SemaphoreType.DMA((2,)),
            ],
        ),
        out_shape=jax.ShapeDtypeStruct((n, D_MODEL), F32),
        compiler_params=_params(("arbitrary",)),
    )(tcnt, lo, gstart, h2, gates, lp, g_final, ys)


def _layer(h, mem2, batch, seq, mem_len, g_mix, w_in, b_qkv, conv_w, conv_b, dt_bias, a_log, d_skip,
           g_ssm_out, attn_sinks, bias, w_out, b_out, g_cross, g_mem, w_q_cross, w_kv_cross, w_o_cross,
           g_ffn, w_router, b_router, w1, b1, w2, b2, g_final):
    n = h.shape[0]
    c0, c1, c2 = SSM_WIDTH, SSM_WIDTH + CONV_CH, SSM_WIDTH + CONV_CH + SSM_HEADS
    w_cat = jnp.concatenate(
        [w_in[:, :c1], jnp.pad(w_in[:, c1:c2], ((0, 0), (0, DT_PAD - SSM_HEADS))), w_in[:, c2:]],
        axis=1).astype(BF16)
    z, xbc, dtp, qkv = _inproj(h, g_mix[None, :], w_cat, b_qkv[None, :])

    pad_h = (0, DT_PAD - SSM_HEADS)
    dt_bias_p = jnp.pad(dt_bias, pad_h)[None, :]
    a_head_p = jnp.pad(-jnp.exp(a_log), pad_h)[None, :]
    d_full = jnp.repeat(d_skip, SSM_HEAD_DIM)[None, :]
    y_ssm = _ssd(xbc, z, dtp, conv_w, conv_b[None, :], dt_bias_p, a_head_p, d_full,
                 g_ssm_out[None, :], batch, seq)
    y_attn = _swa(qkv, attn_sinks, bias, batch, seq)

    k_mem, v_mem = _memkv(mem2, g_mem[None, :], w_kv_cross.astype(BF16))
    h2, u3, gates, lp, tcnt, tbase = _mid(
        y_ssm, y_attn, h, w_out.astype(BF16), b_out[None, :], g_cross[None, :],
        w_q_cross.astype(BF16), k_mem, v_mem, w_o_cross.astype(BF16), g_ffn[None, :],
        w_router.T.astype(BF16), b_router[:, None], seq, mem_len)

    max_rows = n * TOP_K + (n // TM_MID) * N_EXPERTS * (SUBLANES - 1)
    n_blocks = -(-max_rows // MOE_ROWS) + N_EXPERTS
    tcnt = tcnt[:, :, 0]
    tbase = tbase[:, :, 0]
    counts = tbase[-1] + tcnt[-1]
    padded = (counts + MOE_ROWS - 1) // MOE_ROWS * MOE_ROWS
    padded_end = jnp.cumsum(padded)
    start = (padded_end - padded).astype(I32)
    n_active = (padded_end[-1] // MOE_ROWS).astype(I32)
    blk = jnp.arange(n_blocks, dtype=I32)
    blk_exp = jnp.sum((blk * MOE_ROWS)[:, None] >= padded_end[None, :], axis=1)
    blk_exp = jnp.minimum(blk_exp, N_EXPERTS - 1).astype(I32)
    blk_exp = jnp.where(blk < n_active, blk_exp, blk_exp[n_active - 1])
    eid = jnp.arange(N_EXPERTS, dtype=I32)
    later = jnp.logical_and(padded[None, :] > 0, eid[None, :] > eid[:, None])
    next_of = jnp.min(jnp.where(later, eid[None, :], N_EXPERTS), axis=1)
    next_of = jnp.where(next_of < N_EXPERTS, next_of, -1).astype(I32)
    blk_next = next_of[blk_exp]

    run_n = tcnt.reshape(-1)
    run_local = (jnp.cumsum(tcnt, axis=1) - tcnt).reshape(-1)
    run_global = (start[None, :] + tbase).reshape(-1)
    xs = _dispatch(run_n, run_local, run_global, padded - counts, start + counts, n_active[None], u3, lp,
                   n_blocks * MOE_ROWS)

    b1p = b1.reshape(N_EXPERTS, 2 * D_FF // PAIR_COLS, LANES, 2).transpose(0, 1, 3, 2)
    ys = _experts(blk_exp, blk_next, n_active[None], xs, w1, b1p.reshape(N_EXPERTS, 1, 2 * D_FF), w2,
                  b2[:, None, :])
    return _combine(run_n, run_local, run_global, h2, gates, lp, g_final, ys)


def kernel(x, mem, g_mix, w_in, b_qkv, conv_w, conv_b, dt_bias, a_log, d_skip, g_ssm_out, attn_sinks, rel_bias, w_out, b_out, g_cross, g_mem, w_q_cross, w_kv_cross, w_o_cross, g_ffn, w_router, b_router, w1, b1, w2, b2, g_final):
    batch, seq, d = x.shape
    mem_len = mem.shape[1]
    depth = g_mix.shape[0]
    assert depth == 1 and d == D_MODEL and seq % TM_MID == 0
    h = x.reshape(batch * seq, d)
    mem2 = mem.reshape(batch * mem_len, d)
    bias = _bias_table(rel_bias)
    out = _layer(
        h, mem2, batch, seq, mem_len, g_mix[0], w_in[0], b_qkv[0], conv_w[0], conv_b[0], dt_bias[0],
        a_log[0], d_skip[0], g_ssm_out[0], attn_sinks[0], bias, w_out[0], b_out[0], g_cross[0],
        g_mem[0], w_q_cross[0], w_kv_cross[0], w_o_cross[0], g_ffn[0], w_router[0], b_router[0],
        w1[0], b1[0], w2[0], b2[0], g_final[None, :])
    return out.reshape(batch, seq, d)
```

```python
import functools
import math

import numpy as np
import jax
import jax.numpy as jnp
from jax import lax
from jax.experimental import pallas as pl
from jax.experimental.pallas import tpu as pltpu

F32 = jnp.float32
BF16 = jnp.bfloat16
I32 = jnp.int32

D_MODEL = 1024
RMS_EPS = 1e-5

SSM_HEADS = 8
SSM_HEAD_DIM = 64
SSM_WIDTH = SSM_HEADS * SSM_HEAD_DIM
SSM_STATE = 128
SSM_GROUPS = 2
HEADS_PER_GROUP = SSM_HEADS // SSM_GROUPS
GROUP_WIDTH = SSM_WIDTH // SSM_GROUPS
CONV_WIDTH = 4
CHUNK = 128
CONV_CH = SSM_WIDTH + 2 * SSM_GROUPS * SSM_STATE

ATTN_HEADS = 8
KV_HEADS = 2
Q_PER_KV = ATTN_HEADS // KV_HEADS
HEAD_DIM = 64
ATTN_WIDTH = ATTN_HEADS * HEAD_DIM
KV_WIDTH = KV_HEADS * HEAD_DIM
WINDOW = 128
BLOCK = WINDOW
QKV_COLS = ATTN_WIDTH + 2 * KV_WIDTH

REL_BUCKETS = 32
REL_MAX_DIST = 128

CROSS_HEADS = 4
CROSS_HEAD_DIM = 128
CROSS_WIDTH = CROSS_HEADS * CROSS_HEAD_DIM

N_EXPERTS = 32
TOP_K = 4
D_FF = D_MODEL
SWIGLU_ALPHA = 1.702
SWIGLU_LIMIT = 7.0

LANES = 128
SUBLANES = 8
DT_PAD = LANES
PROJ_COLS = SSM_WIDTH + CONV_CH + DT_PAD + QKV_COLS
VMEM_LIMIT = 56 * 1024 * 1024

TM_PROJ = 512
TM_MID = 512
TM_SCATTER = 512
TM_FINAL = 256
MOE_ROWS = 512
MOE_PART = 256
DMA_UNROLL = 8


def _nt(a, b):
    return lax.dot_general(a, b, (((1,), (1,)), ((), ())), preferred_element_type=F32)


def _tn(a, b):
    return lax.dot_general(a, b, (((0,), (0,)), ((), ())), preferred_element_type=F32)


def _dot(a, b):
    return jnp.dot(a, b, preferred_element_type=F32)


def _dot_exact(a, b):
    return jnp.dot(a, b, preferred_element_type=F32, precision=lax.Precision.HIGHEST)


def _rms(x, g):
    return x * lax.rsqrt(jnp.mean(x * x, axis=-1, keepdims=True) + RMS_EPS) * g


def _sigmoid(x):
    return 1.0 / (1.0 + jnp.exp(-x))


def _params(sem=None):
    return pltpu.CompilerParams(dimension_semantics=sem, vmem_limit_bytes=VMEM_LIMIT)


def _inproj_kernel(x_ref, g_ref, w_ref, bq_ref, z_ref, xbc_ref, dt_ref, qkv_ref):
    u = _rms(x_ref[...], g_ref[...])
    p = _dot(u.astype(BF16), w_ref[...])
    z_ref[...] = p[:, :SSM_WIDTH]
    xbc_ref[...] = p[:, SSM_WIDTH:SSM_WIDTH + CONV_CH]
    dt_ref[...] = p[:, SSM_WIDTH + CONV_CH:SSM_WIDTH + CONV_CH + DT_PAD]
    qkv_ref[...] = p[:, SSM_WIDTH + CONV_CH + DT_PAD:] + bq_ref[...]


def _inproj(x2, g_mix, w_cat, b_qkv):
    n = x2.shape[0]
    row = lambda i: (i, 0)
    fixed = lambda i: (0, 0)
    return pl.pallas_call(
        _inproj_kernel,
        grid=(n // TM_PROJ,),
        in_specs=[
            pl.BlockSpec((TM_PROJ, D_MODEL), row),
            pl.BlockSpec((1, D_MODEL), fixed),
            pl.BlockSpec((D_MODEL, PROJ_COLS), fixed),
            pl.BlockSpec((1, QKV_COLS), fixed),
        ],
        out_specs=[
            pl.BlockSpec((TM_PROJ, SSM_WIDTH), row),
            pl.BlockSpec((TM_PROJ, CONV_CH), row),
            pl.BlockSpec((TM_PROJ, DT_PAD), row),
            pl.BlockSpec((TM_PROJ, QKV_COLS), row),
        ],
        out_shape=[
            jax.ShapeDtypeStruct((n, SSM_WIDTH), F32),
            jax.ShapeDtypeStruct((n, CONV_CH), F32),
            jax.ShapeDtypeStruct((n, DT_PAD), F32),
            jax.ShapeDtypeStruct((n, QKV_COLS), F32),
        ],
        compiler_params=_params(("arbitrary",)),
    )(x2, g_mix, w_cat, b_qkv)


def _split3(x):
    hi = x.astype(BF16)
    rest = x - hi.astype(F32)
    mid = rest.astype(BF16)
    lo = (rest - mid.astype(F32)).astype(BF16)
    return jnp.concatenate([hi, mid, lo], axis=1)


def _ssd_kernel(xbc_ref, z_ref, dt_ref, cw_ref, cb_ref, dtb_ref, ah_ref, dsk_ref, gout_ref,
                e64_ref, ecat_ref, tri_ref, y_ref, conv_ref, state_ref):
    c = pl.program_id(0)

    @pl.when(c == 0)
    def _():
        conv_ref[:, :SUBLANES, :] = jnp.zeros((conv_ref.shape[0], SUBLANES, CONV_CH), F32)
        state_ref[...] = jnp.zeros_like(state_ref)

    for b in range(xbc_ref.shape[0]):
        _ssd_chunk(b, xbc_ref, z_ref, dt_ref, cw_ref, cb_ref, dtb_ref, ah_ref, dsk_ref, gout_ref,
                   e64_ref, ecat_ref, tri_ref, y_ref, conv_ref, state_ref)


def _ssd_chunk(b, xbc_ref, z_ref, dt_ref, cw_ref, cb_ref, dtb_ref, ah_ref, dsk_ref, gout_ref,
               e64_ref, ecat_ref, tri_ref, y_ref, conv_ref, state_ref):
    u = xbc_ref[b]
    conv_ref[b, SUBLANES:, :] = u
    acc = u * cw_ref[CONV_WIDTH - 1:CONV_WIDTH, :] + cb_ref[...]
    for j in range(1, CONV_WIDTH):
        shifted = conv_ref[b, SUBLANES - j:SUBLANES - j + CHUNK, :]
        acc = acc + shifted * cw_ref[CONV_WIDTH - 1 - j:CONV_WIDTH - j, :]
    conv_ref[b, :SUBLANES, :] = u[CHUNK - SUBLANES:, :]
    xbc = acc * _sigmoid(acc)

    xs = xbc[:, :SSM_WIDTH]
    bm = xbc[:, SSM_WIDTH:SSM_WIDTH + SSM_GROUPS * SSM_STATE]
    cm = xbc[:, SSM_WIDTH + SSM_GROUPS * SSM_STATE:]

    dtr = dt_ref[b] + dtb_ref[...]
    dt = jnp.maximum(dtr, 0.0) + jnp.log1p(jnp.exp(-jnp.abs(dtr)))
    a = dt * ah_ref[...]
    a3 = _split3(a)
    a_rows = jnp.concatenate([a3[:, :DT_PAD], a3[:, DT_PAD:2 * DT_PAD], a3[:, 2 * DT_PAD:]], axis=0)
    cs = _dot(tri_ref[...], a_rows)
    cs_row = cs.T
    dt_full = _dot(_split3(dt), e64_ref[...])
    cs_cat = _dot(_split3(cs), ecat_ref[...])
    cs_wide = cs_cat[:, :SSM_HEADS * CHUNK]
    cs_full = cs_cat[:, SSM_HEADS * CHUNK:]
    cs_last = cs_full[CHUNK - 1:CHUNK, :]

    xdt = xs * dt_full
    xdt_b = xdt.astype(BF16)
    xw_b = (xdt * jnp.exp(cs_last - cs_full)).astype(BF16)
    in_decay = jnp.exp(cs_full)
    chunk_decay = jnp.exp(cs_last)

    li = lax.broadcasted_iota(I32, (CHUNK, CHUNK), 0)
    si = lax.broadcasted_iota(I32, (CHUNK, CHUNK), 1)
    causal = li >= si

    ys = []
    for g in range(SSM_GROUPS):
        bg = bm[:, g * SSM_STATE:(g + 1) * SSM_STATE].astype(BF16)
        cg = cm[:, g * SSM_STATE:(g + 1) * SSM_STATE].astype(BF16)
        cb = _nt(cg, bg)
        yd = []
        for r in range(HEADS_PER_GROUP):
            h = g * HEADS_PER_GROUP + r
            diff = cs_wide[:, h * CHUNK:(h + 1) * CHUNK] - cs_row[h:h + 1, :]
            m = cb * jnp.exp(jnp.where(causal, diff, -jnp.inf))
            yd.append(_dot(m.astype(BF16), xdt_b[:, h * SSM_HEAD_DIM:(h + 1) * SSM_HEAD_DIM]))
        y_diag = jnp.concatenate(yd, axis=1)
        gs = slice(g * GROUP_WIDTH, (g + 1) * GROUP_WIDTH)
        st = state_ref[b, g]
        y_off = _dot(cg, st.astype(BF16)) * in_decay[:, gs]
        state_ref[b, g] = st * chunk_decay[:, gs] + _tn(bg, xw_b[:, gs])
        ys.append(y_diag + y_off)
    y = jnp.concatenate(ys, axis=1) + dsk_ref[...] * xs

    zz = z_ref[b]
    y = y * (zz * _sigmoid(zz))
    outs = []
    for g in range(SSM_GROUPS):
        yg = y[:, g * GROUP_WIDTH:(g + 1) * GROUP_WIDTH]
        outs.append(yg * lax.rsqrt(jnp.mean(yg * yg, axis=-1, keepdims=True) + RMS_EPS))
    y_ref[b] = jnp.concatenate(outs, axis=1) * gout_ref[...]


def _ssd(xbc, z, dtp, conv_w, conv_b, dt_bias_p, a_head_p, d_full, g_out):
    batch, seq, _ = xbc.shape
    chunk = lambda c: (0, c, 0)
    fixed = lambda c: (0, 0)
    head = np.arange(SSM_WIDTH) // SSM_HEAD_DIM
    e64 = (np.arange(DT_PAD)[:, None] == head[None, :]).astype(np.float32)
    head_w = np.arange(SSM_HEADS * CHUNK) // CHUNK
    e128 = (np.arange(DT_PAD)[:, None] == head_w[None, :]).astype(np.float32)
    e64_3 = jnp.asarray(np.tile(e64, (3, 1))).astype(BF16)
    ecat_3 = jnp.asarray(np.tile(np.concatenate([e128, e64], axis=1), (3, 1))).astype(BF16)
    tri_3 = jnp.asarray(np.tile(np.tril(np.ones((CHUNK, CHUNK), np.float32)), (1, 3))).astype(BF16)
    return pl.pallas_call(
        _ssd_kernel,
        name="ssd",
        grid=(seq // CHUNK,),
        in_specs=[
            pl.BlockSpec((batch, CHUNK, CONV_CH), chunk),
            pl.BlockSpec((batch, CHUNK, SSM_WIDTH), chunk),
            pl.BlockSpec((batch, CHUNK, DT_PAD), chunk),
            pl.BlockSpec((CONV_WIDTH, CONV_CH), fixed),
            pl.BlockSpec((1, CONV_CH), fixed),
            pl.BlockSpec((1, DT_PAD), fixed),
            pl.BlockSpec((1, DT_PAD), fixed),
            pl.BlockSpec((1, SSM_WIDTH), fixed),
            pl.BlockSpec((1, SSM_WIDTH), fixed),
            pl.BlockSpec((3 * DT_PAD, SSM_WIDTH), fixed),
            pl.BlockSpec((3 * DT_PAD, SSM_HEADS * CHUNK + SSM_WIDTH), fixed),
            pl.BlockSpec((CHUNK, 3 * CHUNK), fixed),
        ],
        out_specs=pl.BlockSpec((batch, CHUNK, SSM_WIDTH), chunk),
        out_shape=jax.ShapeDtypeStruct((batch, seq, SSM_WIDTH), F32),
        scratch_shapes=[
            pltpu.VMEM((batch, SUBLANES + CHUNK, CONV_CH), F32),
            pltpu.VMEM((batch, SSM_GROUPS, SSM_STATE, GROUP_WIDTH), F32),
        ],
        compiler_params=_params(("arbitrary",)),
    )(xbc, z, dtp, conv_w, conv_b, dt_bias_p, a_head_p, d_full, g_out, e64_3, ecat_3, tri_3)


def _t5_bucket_table():
    q_pos = np.arange(BLOCK)[:, None]
    k_pos = np.arange(2 * BLOCK)[None, :] - BLOCK
    dist = q_pos - k_pos
    d = np.maximum(dist, 0)
    max_exact = REL_BUCKETS // 2
    ratio = np.maximum(d, max_exact).astype(np.float32) / np.float32(max_exact)
    large = max_exact + (np.log(ratio) / np.float32(math.log(REL_MAX_DIST / max_exact))
                         * np.float32(REL_BUCKETS - max_exact)).astype(np.int32)
    large = np.minimum(large, REL_BUCKETS - 1)
    bucket = np.where(d < max_exact, d, large)
    in_window = (dist >= 0) & (dist < WINDOW)
    return np.where(in_window, bucket, -1).astype(np.int32)


def _bias_kernel(rb_ref, bucket_ref, out_ref):
    bucket = bucket_ref[...]
    for h in range(ATTN_HEADS):
        acc = jnp.zeros(bucket.shape, F32)
        for b in range(REL_BUCKETS):
            acc = jnp.where(bucket == b, rb_ref[b, h], acc)
        out_ref[h] = jnp.where(bucket >= 0, acc, -jnp.inf)


def _bias_table(rel_bias):
    bucket = jnp.asarray(_t5_bucket_table())
    return pl.pallas_call(
        _bias_kernel,
        in_specs=[
            pl.BlockSpec(memory_space=pltpu.SMEM),
            pl.BlockSpec(memory_space=pltpu.VMEM),
        ],
        out_specs=pl.BlockSpec(memory_space=pltpu.VMEM),
        out_shape=jax.ShapeDtypeStruct((ATTN_HEADS, BLOCK, 2 * BLOCK), F32),
    )(rel_bias, bucket)


def _swa_kernel(sink_ref, q_ref, k_ref, v_ref, kp_ref, vp_ref, bias_ref, o_ref):
    i = pl.program_id(0)
    col = lax.broadcasted_iota(I32, (BLOCK, 2 * BLOCK), 1)
    key_ok = jnp.logical_or(col >= BLOCK, i > 0)
    for b in range(q_ref.shape[0]):
        q = (q_ref[b] * (HEAD_DIM ** -0.5)).astype(BF16)
        outs = []
        for g in range(KV_HEADS):
            gs = slice(g * HEAD_DIM, (g + 1) * HEAD_DIM)
            k2 = jnp.concatenate([kp_ref[b, :, gs], k_ref[b, :, gs]], axis=0).astype(BF16)
            v2 = jnp.concatenate([vp_ref[b, :, gs], v_ref[b, :, gs]], axis=0).astype(BF16)
            for r in range(Q_PER_KV):
                h = g * Q_PER_KV + r
                s = _nt(q[:, h * HEAD_DIM:(h + 1) * HEAD_DIM], k2) + bias_ref[h]
                s = jnp.where(key_ok, s, -jnp.inf)
                sink = sink_ref[h]
                m = jnp.maximum(jnp.max(s, axis=-1, keepdims=True), sink)
                p = jnp.exp(s - m)
                denom = jnp.sum(p, axis=-1, keepdims=True) + jnp.exp(sink - m)
                outs.append(_dot(p.astype(BF16), v2) / denom)
        o_ref[b] = jnp.concatenate(outs, axis=1)


def _swa(qkv, sinks, bias):
    batch, seq, _ = qkv.shape
    kcol = ATTN_WIDTH // KV_WIDTH
    prev = lambda i: jnp.maximum(i - 1, 0)
    return pl.pallas_call(
        _swa_kernel,
        name="swa",
        grid_spec=pltpu.PrefetchScalarGridSpec(
            num_scalar_prefetch=1,
            grid=(seq // BLOCK,),
            in_specs=[
                pl.BlockSpec((batch, BLOCK, ATTN_WIDTH), lambda i, s: (0, i, 0)),
                pl.BlockSpec((batch, BLOCK, KV_WIDTH), lambda i, s: (0, i, kcol)),
                pl.BlockSpec((batch, BLOCK, KV_WIDTH), lambda i, s: (0, i, kcol + 1)),
                pl.BlockSpec((batch, BLOCK, KV_WIDTH), lambda i, s: (0, prev(i), kcol)),
                pl.BlockSpec((batch, BLOCK, KV_WIDTH), lambda i, s: (0, prev(i), kcol + 1)),
                pl.BlockSpec((ATTN_HEADS, BLOCK, 2 * BLOCK), lambda i, s: (0, 0, 0)),
            ],
            out_specs=pl.BlockSpec((batch, BLOCK, ATTN_WIDTH), lambda i, s: (0, i, 0)),
        ),
        out_shape=jax.ShapeDtypeStruct((batch, seq, ATTN_WIDTH), F32),
        compiler_params=_params(("arbitrary",)),
    )(sinks, qkv, qkv, qkv, qkv, qkv, bias)


def _memkv_kernel(m_ref, g_ref, w_ref, k_ref, v_ref):
    u = _rms(m_ref[...], g_ref[...])
    kv = _dot(u.astype(BF16), w_ref[...])
    k_ref[...] = kv[:, :CROSS_WIDTH].astype(BF16)
    v_ref[...] = kv[:, CROSS_WIDTH:].astype(BF16)


def _memkv(mem2, g_mem, w_kv):
    n = mem2.shape[0]
    return pl.pallas_call(
        _memkv_kernel,
        out_shape=[jax.ShapeDtypeStruct((n, CROSS_WIDTH), BF16)] * 2,
        compiler_params=_params(),
    )(mem2, g_mem, w_kv)


def _mid_kernel(ys_ref, ya_ref, x_ref, wout_ref, bout_ref, gc_ref, wq_ref, k_ref, v_ref, wo_ref,
                gf_ref, wr_ref, br_ref, upper_ref, lower_ref,
                h2_ref, u3_ref, gate_ref, lp_ref, tcnt_ref, tbase_ref, run_ref):
    i = pl.program_id(0)

    @pl.when(i == 0)
    def _():
        run_ref[...] = jnp.zeros_like(run_ref)

    ycat = jnp.concatenate([ys_ref[...], ya_ref[...]], axis=1).astype(BF16)
    h1 = x_ref[...] + _dot(ycat, wout_ref[...]) + bout_ref[...]

    u2 = _rms(h1, gc_ref[...])
    q = _dot(u2.astype(BF16), wq_ref[...]).astype(BF16)
    outs = []
    for h in range(CROSS_HEADS):
        hs = slice(h * CROSS_HEAD_DIM, (h + 1) * CROSS_HEAD_DIM)
        s = _nt(q[:, hs], k_ref[:, hs]) * (CROSS_HEAD_DIM ** -0.5)
        m = jnp.max(s, axis=-1, keepdims=True)
        p = jnp.exp(s - m)
        outs.append(_dot(p.astype(BF16), v_ref[:, hs]) / jnp.sum(p, axis=-1, keepdims=True))
    o = jnp.concatenate(outs, axis=1).astype(BF16)
    h2 = h1 + _dot(o, wo_ref[...])
    h2_ref[...] = h2

    u3 = _rms(h2, gf_ref[...]).astype(BF16)
    u3_ref[...] = u3

    logits = _nt(wr_ref[...], u3) + br_ref[...]
    eid = lax.broadcasted_iota(I32, logits.shape, 0)
    vals, idxs, hots = [], [], []
    for _ in range(TOP_K):
        m = jnp.max(logits, axis=0, keepdims=True)
        ix = jnp.min(jnp.where(logits == m, eid, N_EXPERTS), axis=0, keepdims=True)
        hot = eid == ix
        logits = jnp.where(hot, -jnp.inf, logits)
        vals.append(m)
        idxs.append(ix)
        hots.append(hot)
    ex = [jnp.exp(v - vals[0]) for v in vals]
    tot = ex[0] + ex[1] + ex[2] + ex[3]
    gate_ref[...] = jnp.concatenate([e / tot for e in ex], axis=0)

    hot_f = [jnp.where(hot, 1.0, 0.0) for hot in hots]
    cnt_k = [jnp.sum(hf, axis=1, keepdims=True) for hf in hot_f]
    tile_cnt = cnt_k[0] + cnt_k[1] + cnt_k[2] + cnt_k[3]
    tile_cnt = jnp.floor((tile_cnt + (SUBLANES - 1)) * (1.0 / SUBLANES)) * SUBLANES
    offs = _dot_exact(lower_ref[...], jnp.broadcast_to(tile_cnt, (N_EXPERTS, LANES)))[:, :1]
    local = []
    for kk in range(TOP_K):
        before = _dot(hot_f[kk].astype(BF16), upper_ref[...])
        local.append(jnp.sum(jnp.where(hots[kk], before + offs, 0.0), axis=0, keepdims=True))
        offs = offs + cnt_k[kk]
    lp_ref[...] = jnp.concatenate(local, axis=0).astype(I32)
    tcnt_ref[0] = jnp.broadcast_to(tile_cnt, (N_EXPERTS, LANES)).astype(I32)
    tbase_ref[0] = jnp.broadcast_to(run_ref[...], (N_EXPERTS, LANES)).astype(I32)
    run_ref[...] = run_ref[...] + tile_cnt


def _mid(y_ssm, y_attn, x2, w_out, b_out, g_cross, w_q, k_mem, v_mem, w_o, g_ffn, w_r_t, b_r, seq, mem_len):
    n = x2.shape[0]
    tiles_per_batch = seq // TM_MID
    row = lambda i: (i, 0)
    col = lambda i: (0, i)
    fixed = lambda i: (0, 0)
    memb = lambda i: (i // tiles_per_batch, 0)
    tile3 = lambda i: (i, 0, 0)
    n_tiles = n // TM_MID
    upper = jnp.asarray(np.triu(np.ones((TM_MID, TM_MID), np.float32), 1)).astype(BF16)
    lower = jnp.asarray(np.tril(np.ones((N_EXPERTS, N_EXPERTS), np.float32), -1))
    return pl.pallas_call(
        _mid_kernel,
        name="mid",
        grid=(n_tiles,),
        in_specs=[
            pl.BlockSpec((TM_MID, SSM_WIDTH), row),
            pl.BlockSpec((TM_MID, ATTN_WIDTH), row),
            pl.BlockSpec((TM_MID, D_MODEL), row),
            pl.BlockSpec((D_MODEL, D_MODEL), fixed),
            pl.BlockSpec((1, D_MODEL), fixed),
            pl.BlockSpec((1, D_MODEL), fixed),
            pl.BlockSpec((D_MODEL, CROSS_WIDTH), fixed),
            pl.BlockSpec((mem_len, CROSS_WIDTH), memb),
            pl.BlockSpec((mem_len, CROSS_WIDTH), memb),
            pl.BlockSpec((CROSS_WIDTH, D_MODEL), fixed),
            pl.BlockSpec((1, D_MODEL), fixed),
            pl.BlockSpec((N_EXPERTS, D_MODEL), fixed),
            pl.BlockSpec((N_EXPERTS, 1), fixed),
            pl.BlockSpec((TM_MID, TM_MID), fixed),
            pl.BlockSpec((N_EXPERTS, N_EXPERTS), fixed),
        ],
        out_specs=[
            pl.BlockSpec((TM_MID, D_MODEL), row),
            pl.BlockSpec((TM_MID, D_MODEL), row),
            pl.BlockSpec((TOP_K, TM_MID), col),
            pl.BlockSpec((TOP_K, TM_MID), col),
            pl.BlockSpec((1, N_EXPERTS, LANES), tile3),
            pl.BlockSpec((1, N_EXPERTS, LANES), tile3),
        ],
        out_shape=[
            jax.ShapeDtypeStruct((n, D_MODEL), F32),
            jax.ShapeDtypeStruct((n, D_MODEL), BF16),
            jax.ShapeDtypeStruct((TOP_K, n), F32),
            jax.ShapeDtypeStruct((TOP_K, n), I32),
            jax.ShapeDtypeStruct((n_tiles, N_EXPERTS, LANES), I32),
            jax.ShapeDtypeStruct((n_tiles, N_EXPERTS, LANES), I32),
        ],
        scratch_shapes=[pltpu.VMEM((N_EXPERTS, 1), F32)],
        compiler_params=_params(("arbitrary",)),
    )(y_ssm, y_attn, x2, w_out, b_out, g_cross, w_q, k_mem, v_mem, w_o, g_ffn, w_r_t, b_r, upper, lower)


ROW_CHUNK = 256
LOCAL_ROWS = -(-(TM_MID * TOP_K + N_EXPERTS * (SUBLANES - 1)) // ROW_CHUNK) * ROW_CHUNK
RUN_BITS = tuple(1 << s for s in range(TM_MID.bit_length() - 1, SUBLANES.bit_length() - 2, -1))
PAD_BITS = tuple(1 << s for s in range((MOE_ROWS - 1).bit_length() - 1, SUBLANES.bit_length() - 2, -1))


def _run_copies(n, src_ref, src, dst_ref, dst, sem, bits, wait, advance_src=True):
    for bit in bits:
        step = n & bit

        @pl.when(step != 0)
        def _():
            cp = pltpu.make_async_copy(src_ref.at[pl.ds(pl.multiple_of(src, SUBLANES), bit)],
                                       dst_ref.at[pl.ds(pl.multiple_of(dst, SUBLANES), bit)], sem)
            if wait:
                cp.wait()
            else:
                cp.start()

        if advance_src:
            src = src + step
        dst = dst + step


def _dispatch_kernel(tcnt_ref, lo_ref, gs_ref, padn_ref, pads_ref, na_ref, u_ref, lp_ref, xs_ref,
                     xloc_ref, zero_ref, sem, zsem):
    i = pl.program_id(0)

    def zero_fill(wait):
        def body(e, carry):
            _run_copies(padn_ref[e], zero_ref, 0, xs_ref, pads_ref[e], zsem, PAD_BITS, wait, advance_src=False)
            return carry
        lax.fori_loop(0, N_EXPERTS, body, 0)

        def tail(b, carry):
            cp = pltpu.make_async_copy(zero_ref, xs_ref.at[pl.ds(pl.multiple_of(b * MOE_ROWS, MOE_ROWS), MOE_ROWS)], zsem)
            if wait:
                cp.wait()
            else:
                cp.start()
            return carry
        lax.fori_loop(na_ref[0], xs_ref.shape[0] // MOE_ROWS, tail, 0)

    @pl.when(i == 0)
    def _():
        zero_ref[...] = jnp.zeros_like(zero_ref)
        zero_fill(False)

    slot = i % 2
    u = u_ref[...]
    lp = lp_ref[...]
    for c in range(LOCAL_ROWS // ROW_CHUNK):
        r = lax.broadcasted_iota(I32, (ROW_CHUNK, TM_MID), 0) + c * ROW_CHUNK
        p = jnp.zeros((ROW_CHUNK, TM_MID), F32)
        for kk in range(TOP_K):
            p = jnp.where(lp[kk:kk + 1, :] == r, 1.0, p)
        xloc_ref[slot, c * ROW_CHUNK:(c + 1) * ROW_CHUNK, :] = _dot(p.astype(BF16), u)

    def runs(tile, wait):
        s = tile % 2

        def body(e, carry):
            j = tile * N_EXPERTS + e
            _run_copies(tcnt_ref[j], xloc_ref.at[s], lo_ref[j], xs_ref, gs_ref[j], sem.at[s], RUN_BITS, wait)
            return carry
        lax.fori_loop(0, N_EXPERTS, body, 0)

    runs(i, False)

    @pl.when(i > 0)
    def _():
        runs(i - 1, True)

    @pl.when(i == pl.num_programs(0) - 1)
    def _():
        runs(i, True)

    @pl.when(i == 0)
    def _():
        zero_fill(True)


def _dispatch(tcnt, lo, gstart, padn, pads, n_active, u3, lp, n_rows):
    n = u3.shape[0]
    return pl.pallas_call(
        _dispatch_kernel,
        name="dispatch",
        grid_spec=pltpu.PrefetchScalarGridSpec(
            num_scalar_prefetch=6,
            grid=(n // TM_MID,),
            in_specs=[
                pl.BlockSpec((TM_MID, D_MODEL), lambda i, *_: (i, 0)),
                pl.BlockSpec((TOP_K, TM_MID), lambda i, *_: (0, i)),
            ],
            out_specs=pl.BlockSpec(memory_space=pl.ANY),
            scratch_shapes=[
                pltpu.VMEM((2, LOCAL_ROWS, D_MODEL), F32),
                pltpu.VMEM((MOE_ROWS, D_MODEL), F32),
                pltpu.SemaphoreType.DMA((2,)),
                pltpu.SemaphoreType.DMA,
            ],
        ),
        out_shape=jax.ShapeDtypeStruct((n_rows, D_MODEL), F32),
        compiler_params=_params(("arbitrary",)),
    )(tcnt, lo, gstart, padn, pads, n_active, u3, lp)


PAIR_COLS = 2 * LANES


def _expert_kernel(be_ref, nx_ref, nv_ref, na_ref, x_ref, b1_ref, b2_ref, perm_ref, w1_hbm, w2_hbm, y_ref,
                   w1s_ref, w2s_ref, w1p_ref, w2b_ref, sem):
    b = pl.program_id(0)
    valid_rows = nv_ref[b]
    active = valid_rows > 0
    expert = be_ref[b]
    new_expert = jnp.logical_or(b == 0, expert != be_ref[jnp.maximum(b - 1, 0)])

    def fetch(e, wait):
        for src, dst, s in ((w1_hbm, w1s_ref, 0), (w2_hbm, w2s_ref, 1)):
            cp = pltpu.make_async_copy(src.at[e], dst, sem.at[s])
            if wait:
                cp.wait()
            else:
                cp.start()

    @pl.when(b == 0)
    def _():
        fetch(expert, False)

    @pl.when(jnp.logical_and(active, new_expert))
    def _():
        fetch(expert, True)
        for c in range(2 * D_FF // PAIR_COLS):
            cs = slice(c * PAIR_COLS, (c + 1) * PAIR_COLS)
            w1p_ref[:, cs] = _dot(w1s_ref[:, cs].astype(BF16), perm_ref[...]).astype(BF16)
        w2b_ref[...] = w2s_ref[...].astype(BF16)

        @pl.when(nx_ref[b] >= 0)
        def _():
            fetch(nx_ref[b], False)

    for part in range(MOE_ROWS // MOE_PART):
        rows = slice(part * MOE_PART, (part + 1) * MOE_PART)
        live = valid_rows > part * MOE_PART

        @pl.when(live)
        def _():
            x = x_ref[rows, :].astype(BF16)
            hdn = _dot(x, w1p_ref[...]) + b1_ref[0]
            acts = []
            for c in range(2 * D_FF // PAIR_COLS):
                hg = jnp.minimum(hdn[:, c * PAIR_COLS:c * PAIR_COLS + LANES], SWIGLU_LIMIT)
                hl = jnp.clip(hdn[:, c * PAIR_COLS + LANES:(c + 1) * PAIR_COLS], -SWIGLU_LIMIT, SWIGLU_LIMIT)
                acts.append((hg * _sigmoid(SWIGLU_ALPHA * hg) * (hl + 1.0)).astype(BF16))
            y_ref[rows, :] = _dot(jnp.concatenate(acts, axis=1), w2b_ref[...]) + b2_ref[0]

        @pl.when(jnp.logical_not(live))
        def _():
            y_ref[rows, :] = jnp.zeros((MOE_PART, D_MODEL), F32)


def _experts(block_exp, next_exp, valid_rows, n_active, xs, w1, b1p, w2, b2):
    n_rows = xs.shape[0]
    n_blocks = n_rows // MOE_ROWS
    xmap = lambda b, be, nx, nv, na: (jnp.maximum(jnp.minimum(b, na[0] - 1), 0), 0)
    emap = lambda b, be, nx, nv, na: (be[b], 0, 0)
    perm = np.zeros((PAIR_COLS, PAIR_COLS), np.float32)
    perm[2 * np.arange(LANES), np.arange(LANES)] = 1.0
    perm[2 * np.arange(LANES) + 1, LANES + np.arange(LANES)] = 1.0
    return pl.pallas_call(
        _expert_kernel,
        name="experts",
        grid_spec=pltpu.PrefetchScalarGridSpec(
            num_scalar_prefetch=4,
            grid=(n_blocks,),
            in_specs=[
                pl.BlockSpec((MOE_ROWS, D_MODEL), xmap),
                pl.BlockSpec((1, 1, 2 * D_FF), emap),
                pl.BlockSpec((1, 1, D_MODEL), emap),
                pl.BlockSpec((PAIR_COLS, PAIR_COLS), lambda b, *_: (0, 0)),
                pl.BlockSpec(memory_space=pl.ANY),
                pl.BlockSpec(memory_space=pl.ANY),
            ],
            out_specs=pl.BlockSpec((MOE_ROWS, D_MODEL), lambda b, *_: (b, 0)),
            scratch_shapes=[
                pltpu.VMEM((D_MODEL, 2 * D_FF), F32),
                pltpu.VMEM((D_FF, D_MODEL), F32),
                pltpu.VMEM((D_MODEL, 2 * D_FF), BF16),
                pltpu.VMEM((D_FF, D_MODEL), BF16),
                pltpu.SemaphoreType.DMA((2,)),
            ],
        ),
        out_shape=jax.ShapeDtypeStruct((n_rows, D_MODEL), F32),
        compiler_params=_params(("arbitrary",)),
    )(block_exp, next_exp, valid_rows, n_active, xs, b1p, b2, jnp.asarray(perm).astype(BF16), w1, w2)


def _combine_kernel(tcnt_ref, lo_ref, gs_ref, h2_ref, gate_ref, lp_ref, g_ref, ys_ref, o_ref, yloc_ref, sem):
    i = pl.program_id(0)

    slot = i % 2

    def runs(tile, wait):
        s = tile % 2

        def body(e, carry):
            j = tile * N_EXPERTS + e
            _run_copies(tcnt_ref[j], ys_ref, gs_ref[j], yloc_ref.at[s], lo_ref[j], sem.at[s], RUN_BITS, wait)
            return carry
        lax.fori_loop(0, N_EXPERTS, body, 0)

    @pl.when(i == 0)
    def _():
        yloc_ref[...] = jnp.zeros_like(yloc_ref)
        runs(i, False)

    @pl.when(i + 1 < pl.num_programs(0))
    def _():
        runs(i + 1, False)

    gates = gate_ref[...].T
    lp = lp_ref[...].astype(F32).T
    runs(i, True)

    h = h2_ref[...]
    for c in range(LOCAL_ROWS // ROW_CHUNK):
        lane = (lax.broadcasted_iota(I32, (TM_MID, ROW_CHUNK), 1) + c * ROW_CHUNK).astype(F32)
        a = jnp.zeros((TM_MID, ROW_CHUNK), F32)
        for kk in range(TOP_K):
            a = jnp.where(lp[:, kk:kk + 1] == lane, gates[:, kk:kk + 1], a)
        h = h + _dot(a.astype(BF16), yloc_ref[slot, c * ROW_CHUNK:(c + 1) * ROW_CHUNK, :].astype(BF16))
    o_ref[...] = _rms(h, g_ref[...])


def _combine(tcnt, lo, gstart, h2, gates, lp, g_final, ys):
    n = h2.shape[0]
    return pl.pallas_call(
        _combine_kernel,
        name="combine",
        grid_spec=pltpu.PrefetchScalarGridSpec(
            num_scalar_prefetch=3,
            grid=(n // TM_MID,),
            in_specs=[
                pl.BlockSpec((TM_MID, D_MODEL), lambda i, *_: (i, 0)),
                pl.BlockSpec((TOP_K, TM_MID), lambda i, *_: (0, i)),
                pl.BlockSpec((TOP_K, TM_MID), lambda i, *_: (0, i)),
                pl.BlockSpec((1, D_MODEL), lambda i, *_: (0, 0)),
                pl.BlockSpec(memory_space=pl.ANY),
            ],
            out_specs=pl.BlockSpec((TM_MID, D_MODEL), lambda i, *_: (i, 0)),
            scratch_shapes=[
                pltpu.VMEM((2, LOCAL_ROWS, D_MODEL), F32),
                pltpu.SemaphoreType.DMA((2,)),
            ],
        ),
        out_shape=jax.ShapeDtypeStruct((n, D_MODEL), F32),
        compiler_params=_params(("arbitrary",)),
    )(tcnt, lo, gstart, h2, gates, lp, g_final, ys)


def _layer(h, mem2, batch, seq, mem_len, g_mix, w_in, b_qkv, conv_w, conv_b, dt_bias, a_log, d_skip,
           g_ssm_out, attn_sinks, bias, w_out, b_out, g_cross, g_mem, w_q_cross, w_kv_cross, w_o_cross,
           g_ffn, w_router, b_router, w1, b1, w2, b2, g_final):
    n = h.shape[0]
    c0, c1, c2 = SSM_WIDTH, SSM_WIDTH + CONV_CH, SSM_WIDTH + CONV_CH + SSM_HEADS
    w_cat = jnp.concatenate(
        [w_in[:, :c1], jnp.pad(w_in[:, c1:c2], ((0, 0), (0, DT_PAD - SSM_HEADS))), w_in[:, c2:]],
        axis=1).astype(BF16)
    z, xbc, dtp, qkv = _inproj(h, g_mix[None, :], w_cat, b_qkv[None, :])

    pad_h = (0, DT_PAD - SSM_HEADS)
    dt_bias_p = jnp.pad(dt_bias, pad_h)[None, :]
    a_head_p = jnp.pad(-jnp.exp(a_log), pad_h)[None, :]
    d_full = jnp.repeat(d_skip, SSM_HEAD_DIM)[None, :]
    per_seq = lambda t: t.reshape(batch, seq, t.shape[-1])
    y_ssm = _ssd(per_seq(xbc), per_seq(z), per_seq(dtp), conv_w, conv_b[None, :], dt_bias_p, a_head_p,
                 d_full, g_ssm_out[None, :]).reshape(n, SSM_WIDTH)
    y_attn = _swa(per_seq(qkv), attn_sinks, bias).reshape(n, ATTN_WIDTH)

    k_mem, v_mem = _memkv(mem2, g_mem[None, :], w_kv_cross.astype(BF16))
    h2, u3, gates, lp, tcnt, tbase = _mid(
        y_ssm, y_attn, h, w_out.astype(BF16), b_out[None, :], g_cross[None, :],
        w_q_cross.astype(BF16), k_mem, v_mem, w_o_cross.astype(BF16), g_ffn[None, :],
        w_router.T.astype(BF16), b_router[:, None], seq, mem_len)

    max_rows = n * TOP_K + (n // TM_MID) * N_EXPERTS * (SUBLANES - 1)
    n_blocks = -(-max_rows // MOE_ROWS) + N_EXPERTS
    tcnt = tcnt[:, :, 0]
    tbase = tbase[:, :, 0]
    counts = tbase[-1] + tcnt[-1]
    padded = (counts + MOE_ROWS - 1) // MOE_ROWS * MOE_ROWS
    padded_end = jnp.cumsum(padded)
    start = (padded_end - padded).astype(I32)
    n_active = (padded_end[-1] // MOE_ROWS).astype(I32)
    blk = jnp.arange(n_blocks, dtype=I32)
    blk_exp = jnp.sum((blk * MOE_ROWS)[:, None] >= padded_end[None, :], axis=1)
    blk_exp = jnp.minimum(blk_exp, N_EXPERTS - 1).astype(I32)
    blk_exp = jnp.where(blk < n_active, blk_exp, blk_exp[n_active - 1])
    eid = jnp.arange(N_EXPERTS, dtype=I32)
    later = jnp.logical_and(padded[None, :] > 0, eid[None, :] > eid[:, None])
    next_of = jnp.min(jnp.where(later, eid[None, :], N_EXPERTS), axis=1)
    next_of = jnp.where(next_of < N_EXPERTS, next_of, -1).astype(I32)
    blk_next = next_of[blk_exp]
    blk_valid = jnp.clip((start + counts)[blk_exp] - blk * MOE_ROWS, 0, MOE_ROWS)
    blk_valid = jnp.where(blk < n_active, blk_valid, 0).astype(I32)

    run_n = tcnt.reshape(-1)
    run_local = (jnp.cumsum(tcnt, axis=1) - tcnt).reshape(-1)
    run_global = (start[None, :] + tbase).reshape(-1)
    xs = _dispatch(run_n, run_local, run_global, padded - counts, start + counts, n_active[None], u3, lp,
                   n_blocks * MOE_ROWS)

    b1p = b1.reshape(N_EXPERTS, 2 * D_FF // PAIR_COLS, LANES, 2).transpose(0, 1, 3, 2)
    ys = _experts(blk_exp, blk_next, blk_valid, n_active[None], xs, w1,
                  b1p.reshape(N_EXPERTS, 1, 2 * D_FF), w2, b2[:, None, :])
    return _combine(run_n, run_local, run_global, h2, gates, lp, g_final, ys)


def kernel(x, mem, g_mix, w_in, b_qkv, conv_w, conv_b, dt_bias, a_log, d_skip, g_ssm_out, attn_sinks, rel_bias, w_out, b_out, g_cross, g_mem, w_q_cross, w_kv_cross, w_o_cross, g_ffn, w_router, b_router, w1, b1, w2, b2, g_final):
    batch, seq, d = x.shape
    mem_len = mem.shape[1]
    depth = g_mix.shape[0]
    assert depth == 1 and d == D_MODEL and seq % TM_MID == 0
    h = x.reshape(batch * seq, d)
    mem2 = mem.reshape(batch * mem_len, d)
    bias = _bias_table(rel_bias)
    out = _layer(
        h, mem2, batch, seq, mem_len, g_mix[0], w_in[0], b_qkv[0], conv_w[0], conv_b[0], dt_bias[0],
        a_log[0], d_skip[0], g_ssm_out[0], attn_sinks[0], bias, w_out[0], b_out[0], g_cross[0],
        g_mem[0], w_q_cross[0], w_kv_cross[0], w_o_cross[0], g_ffn[0], w_router[0], b_router[0],
        w1[0], b1[0], w2[0], b2[0], g_final[None, :])
    return out.reshape(batch, seq, d)
```

```python
import functools
import math

import numpy as np
import jax
import jax.numpy as jnp
from jax import lax
from jax.experimental import pallas as pl
from jax.experimental.pallas import tpu as pltpu

F32 = jnp.float32
BF16 = jnp.bfloat16
I32 = jnp.int32

D_MODEL = 1024
RMS_EPS = 1e-5

SSM_HEADS = 8
SSM_HEAD_DIM = 64
SSM_WIDTH = SSM_HEADS * SSM_HEAD_DIM
SSM_STATE = 128
SSM_GROUPS = 2
HEADS_PER_GROUP = SSM_HEADS // SSM_GROUPS
GROUP_WIDTH = SSM_WIDTH // SSM_GROUPS
CONV_WIDTH = 4
CHUNK = 128
CONV_CH = SSM_WIDTH + 2 * SSM_GROUPS * SSM_STATE

ATTN_HEADS = 8
KV_HEADS = 2
Q_PER_KV = ATTN_HEADS // KV_HEADS
HEAD_DIM = 64
ATTN_WIDTH = ATTN_HEADS * HEAD_DIM
KV_WIDTH = KV_HEADS * HEAD_DIM
WINDOW = 128
BLOCK = WINDOW
QKV_COLS = ATTN_WIDTH + 2 * KV_WIDTH

REL_BUCKETS = 32
REL_MAX_DIST = 128

CROSS_HEADS = 4
CROSS_HEAD_DIM = 128
CROSS_WIDTH = CROSS_HEADS * CROSS_HEAD_DIM

N_EXPERTS = 32
TOP_K = 4
D_FF = D_MODEL
SWIGLU_ALPHA = 1.702
SWIGLU_LIMIT = 7.0

LANES = 128
SUBLANES = 8
DT_PAD = LANES
PROJ_COLS = SSM_WIDTH + CONV_CH + DT_PAD + QKV_COLS
VMEM_LIMIT = 56 * 1024 * 1024

TM_PROJ = 512
TM_MID = 512
TM_SCATTER = 512
TM_FINAL = 256
MOE_ROWS = 512
MOE_PART = 512
DMA_UNROLL = 8


def _nt(a, b):
    return lax.dot_general(a, b, (((1,), (1,)), ((), ())), preferred_element_type=F32)


def _tn(a, b):
    return lax.dot_general(a, b, (((0,), (0,)), ((), ())), preferred_element_type=F32)


def _dot(a, b):
    return jnp.dot(a, b, preferred_element_type=F32)


def _dot_exact(a, b):
    return jnp.dot(a, b, preferred_element_type=F32, precision=lax.Precision.HIGHEST)


def _rms(x, g):
    return x * lax.rsqrt(jnp.mean(x * x, axis=-1, keepdims=True) + RMS_EPS) * g


def _sigmoid(x):
    return 1.0 / (1.0 + jnp.exp(-x))


def _params(sem=None):
    return pltpu.CompilerParams(dimension_semantics=sem, vmem_limit_bytes=VMEM_LIMIT)


def _inproj_kernel(x_ref, g_ref, w_ref, bq_ref, z_ref, xbc_ref, dt_ref, qkv_ref):
    u = _rms(x_ref[...], g_ref[...])
    p = _dot(u.astype(BF16), w_ref[...])
    z_ref[...] = p[:, :SSM_WIDTH]
    xbc_ref[...] = p[:, SSM_WIDTH:SSM_WIDTH + CONV_CH]
    dt_ref[...] = p[:, SSM_WIDTH + CONV_CH:SSM_WIDTH + CONV_CH + DT_PAD]
    qkv_ref[...] = p[:, SSM_WIDTH + CONV_CH + DT_PAD:] + bq_ref[...]


def _inproj(x2, g_mix, w_cat, b_qkv):
    n = x2.shape[0]
    row = lambda i: (i, 0)
    fixed = lambda i: (0, 0)
    return pl.pallas_call(
        _inproj_kernel,
        grid=(n // TM_PROJ,),
        in_specs=[
            pl.BlockSpec((TM_PROJ, D_MODEL), row),
            pl.BlockSpec((1, D_MODEL), fixed),
            pl.BlockSpec((D_MODEL, PROJ_COLS), fixed),
            pl.BlockSpec((1, QKV_COLS), fixed),
        ],
        out_specs=[
            pl.BlockSpec((TM_PROJ, SSM_WIDTH), row),
            pl.BlockSpec((TM_PROJ, CONV_CH), row),
            pl.BlockSpec((TM_PROJ, DT_PAD), row),
            pl.BlockSpec((TM_PROJ, QKV_COLS), row),
        ],
        out_shape=[
            jax.ShapeDtypeStruct((n, SSM_WIDTH), F32),
            jax.ShapeDtypeStruct((n, CONV_CH), F32),
            jax.ShapeDtypeStruct((n, DT_PAD), F32),
            jax.ShapeDtypeStruct((n, QKV_COLS), F32),
        ],
        compiler_params=_params(("arbitrary",)),
    )(x2, g_mix, w_cat, b_qkv)


def _split3(x):
    hi = x.astype(BF16)
    rest = x - hi.astype(F32)
    mid = rest.astype(BF16)
    lo = (rest - mid.astype(F32)).astype(BF16)
    return jnp.concatenate([hi, mid, lo], axis=1)


def _ssd_kernel(xbc_ref, z_ref, dt_ref, cw_ref, cb_ref, dtb_ref, ah_ref, dsk_ref, gout_ref,
                e64_ref, ecat_ref, tri_ref, y_ref, conv_ref, state_ref):
    c = pl.program_id(0)

    @pl.when(c == 0)
    def _():
        conv_ref[:, :SUBLANES, :] = jnp.zeros((conv_ref.shape[0], SUBLANES, CONV_CH), F32)
        state_ref[...] = jnp.zeros_like(state_ref)

    for b in range(xbc_ref.shape[0]):
        _ssd_chunk(b, xbc_ref, z_ref, dt_ref, cw_ref, cb_ref, dtb_ref, ah_ref, dsk_ref, gout_ref,
                   e64_ref, ecat_ref, tri_ref, y_ref, conv_ref, state_ref)


def _ssd_chunk(b, xbc_ref, z_ref, dt_ref, cw_ref, cb_ref, dtb_ref, ah_ref, dsk_ref, gout_ref,
               e64_ref, ecat_ref, tri_ref, y_ref, conv_ref, state_ref):
    u = xbc_ref[b]
    conv_ref[b, SUBLANES:, :] = u
    acc = u * cw_ref[CONV_WIDTH - 1:CONV_WIDTH, :] + cb_ref[...]
    for j in range(1, CONV_WIDTH):
        shifted = conv_ref[b, SUBLANES - j:SUBLANES - j + CHUNK, :]
        acc = acc + shifted * cw_ref[CONV_WIDTH - 1 - j:CONV_WIDTH - j, :]
    conv_ref[b, :SUBLANES, :] = u[CHUNK - SUBLANES:, :]
    xbc = acc * _sigmoid(acc)

    xs = xbc[:, :SSM_WIDTH]
    bm = xbc[:, SSM_WIDTH:SSM_WIDTH + SSM_GROUPS * SSM_STATE]
    cm = xbc[:, SSM_WIDTH + SSM_GROUPS * SSM_STATE:]

    dtr = dt_ref[b] + dtb_ref[...]
    dt = jnp.maximum(dtr, 0.0) + jnp.log1p(jnp.exp(-jnp.abs(dtr)))
    a = dt * ah_ref[...]
    a3 = _split3(a)
    a_rows = jnp.concatenate([a3[:, :DT_PAD], a3[:, DT_PAD:2 * DT_PAD], a3[:, 2 * DT_PAD:]], axis=0)
    cs = _dot(tri_ref[...], a_rows)
    cs_row = cs.T
    dt_full = _dot(_split3(dt), e64_ref[...])
    cs_cat = _dot(_split3(cs), ecat_ref[...])
    cs_wide = cs_cat[:, :SSM_HEADS * CHUNK]
    cs_full = cs_cat[:, SSM_HEADS * CHUNK:]
    cs_last = cs_full[CHUNK - 1:CHUNK, :]

    xdt = xs * dt_full
    xdt_b = xdt.astype(BF16)
    xw_b = (xdt * jnp.exp(cs_last - cs_full)).astype(BF16)
    in_decay = jnp.exp(cs_full)
    chunk_decay = jnp.exp(cs_last)

    li = lax.broadcasted_iota(I32, (CHUNK, CHUNK), 0)
    si = lax.broadcasted_iota(I32, (CHUNK, CHUNK), 1)
    causal = li >= si

    ys = []
    for g in range(SSM_GROUPS):
        bg = bm[:, g * SSM_STATE:(g + 1) * SSM_STATE].astype(BF16)
        cg = cm[:, g * SSM_STATE:(g + 1) * SSM_STATE].astype(BF16)
        cb = _nt(cg, bg)
        yd = []
        for r in range(HEADS_PER_GROUP):
            h = g * HEADS_PER_GROUP + r
            diff = cs_wide[:, h * CHUNK:(h + 1) * CHUNK] - cs_row[h:h + 1, :]
            m = cb * jnp.exp(jnp.where(causal, diff, -jnp.inf))
            yd.append(_dot(m.astype(BF16), xdt_b[:, h * SSM_HEAD_DIM:(h + 1) * SSM_HEAD_DIM]))
        y_diag = jnp.concatenate(yd, axis=1)
        gs = slice(g * GROUP_WIDTH, (g + 1) * GROUP_WIDTH)
        st = state_ref[b, g]
        y_off = _dot(cg, st.astype(BF16)) * in_decay[:, gs]
        state_ref[b, g] = st * chunk_decay[:, gs] + _tn(bg, xw_b[:, gs])
        ys.append(y_diag + y_off)
    y = jnp.concatenate(ys, axis=1) + dsk_ref[...] * xs

    zz = z_ref[b]
    y = y * (zz * _sigmoid(zz))
    outs = []
    for g in range(SSM_GROUPS):
        yg = y[:, g * GROUP_WIDTH:(g + 1) * GROUP_WIDTH]
        outs.append(yg * lax.rsqrt(jnp.mean(yg * yg, axis=-1, keepdims=True) + RMS_EPS))
    y_ref[b] = jnp.concatenate(outs, axis=1) * gout_ref[...]


def _ssd(xbc, z, dtp, conv_w, conv_b, dt_bias_p, a_head_p, d_full, g_out):
    batch, seq, _ = xbc.shape
    chunk = lambda c: (0, c, 0)
    fixed = lambda c: (0, 0)
    head = np.arange(SSM_WIDTH) // SSM_HEAD_DIM
    e64 = (np.arange(DT_PAD)[:, None] == head[None, :]).astype(np.float32)
    head_w = np.arange(SSM_HEADS * CHUNK) // CHUNK
    e128 = (np.arange(DT_PAD)[:, None] == head_w[None, :]).astype(np.float32)
    e64_3 = jnp.asarray(np.tile(e64, (3, 1))).astype(BF16)
    ecat_3 = jnp.asarray(np.tile(np.concatenate([e128, e64], axis=1), (3, 1))).astype(BF16)
    tri_3 = jnp.asarray(np.tile(np.tril(np.ones((CHUNK, CHUNK), np.float32)), (1, 3))).astype(BF16)
    return pl.pallas_call(
        _ssd_kernel,
        name="ssd",
        grid=(seq // CHUNK,),
        in_specs=[
            pl.BlockSpec((batch, CHUNK, CONV_CH), chunk),
            pl.BlockSpec((batch, CHUNK, SSM_WIDTH), chunk),
            pl.BlockSpec((batch, CHUNK, DT_PAD), chunk),
            pl.BlockSpec((CONV_WIDTH, CONV_CH), fixed),
            pl.BlockSpec((1, CONV_CH), fixed),
            pl.BlockSpec((1, DT_PAD), fixed),
            pl.BlockSpec((1, DT_PAD), fixed),
            pl.BlockSpec((1, SSM_WIDTH), fixed),
            pl.BlockSpec((1, SSM_WIDTH), fixed),
            pl.BlockSpec((3 * DT_PAD, SSM_WIDTH), fixed),
            pl.BlockSpec((3 * DT_PAD, SSM_HEADS * CHUNK + SSM_WIDTH), fixed),
            pl.BlockSpec((CHUNK, 3 * CHUNK), fixed),
        ],
        out_specs=pl.BlockSpec((batch, CHUNK, SSM_WIDTH), chunk),
        out_shape=jax.ShapeDtypeStruct((batch, seq, SSM_WIDTH), F32),
        scratch_shapes=[
            pltpu.VMEM((batch, SUBLANES + CHUNK, CONV_CH), F32),
            pltpu.VMEM((batch, SSM_GROUPS, SSM_STATE, GROUP_WIDTH), F32),
        ],
        compiler_params=_params(("arbitrary",)),
    )(xbc, z, dtp, conv_w, conv_b, dt_bias_p, a_head_p, d_full, g_out, e64_3, ecat_3, tri_3)


def _t5_bucket_table():
    q_pos = np.arange(BLOCK)[:, None]
    k_pos = np.arange(2 * BLOCK)[None, :] - BLOCK
    dist = q_pos - k_pos
    d = np.maximum(dist, 0)
    max_exact = REL_BUCKETS // 2
    ratio = np.maximum(d, max_exact).astype(np.float32) / np.float32(max_exact)
    large = max_exact + (np.log(ratio) / np.float32(math.log(REL_MAX_DIST / max_exact))
                         * np.float32(REL_BUCKETS - max_exact)).astype(np.int32)
    large = np.minimum(large, REL_BUCKETS - 1)
    bucket = np.where(d < max_exact, d, large)
    in_window = (dist >= 0) & (dist < WINDOW)
    return np.where(in_window, bucket, -1).astype(np.int32)


def _bias_kernel(rb_ref, bucket_ref, out_ref):
    bucket = bucket_ref[...]
    for h in range(ATTN_HEADS):
        acc = jnp.zeros(bucket.shape, F32)
        for b in range(REL_BUCKETS):
            acc = jnp.where(bucket == b, rb_ref[b, h], acc)
        out_ref[h] = jnp.where(bucket >= 0, acc, -jnp.inf)


def _bias_table(rel_bias):
    bucket = jnp.asarray(_t5_bucket_table())
    return pl.pallas_call(
        _bias_kernel,
        in_specs=[
            pl.BlockSpec(memory_space=pltpu.SMEM),
            pl.BlockSpec(memory_space=pltpu.VMEM),
        ],
        out_specs=pl.BlockSpec(memory_space=pltpu.VMEM),
        out_shape=jax.ShapeDtypeStruct((ATTN_HEADS, BLOCK, 2 * BLOCK), F32),
    )(rel_bias, bucket)


def _swa_kernel(sink_ref, q_ref, k_ref, v_ref, kp_ref, vp_ref, bias_ref, o_ref):
    i = pl.program_id(0)
    col = lax.broadcasted_iota(I32, (BLOCK, 2 * BLOCK), 1)
    key_ok = jnp.logical_or(col >= BLOCK, i > 0)
    for b in range(q_ref.shape[0]):
        q = (q_ref[b] * (HEAD_DIM ** -0.5)).astype(BF16)
        outs = []
        for g in range(KV_HEADS):
            gs = slice(g * HEAD_DIM, (g + 1) * HEAD_DIM)
            k2 = jnp.concatenate([kp_ref[b, :, gs], k_ref[b, :, gs]], axis=0).astype(BF16)
            v2 = jnp.concatenate([vp_ref[b, :, gs], v_ref[b, :, gs]], axis=0).astype(BF16)
            for r in range(Q_PER_KV):
                h = g * Q_PER_KV + r
                s = _nt(q[:, h * HEAD_DIM:(h + 1) * HEAD_DIM], k2) + bias_ref[h]
                s = jnp.where(key_ok, s, -jnp.inf)
                sink = sink_ref[h]
                m = jnp.maximum(jnp.max(s, axis=-1, keepdims=True), sink)
                p = jnp.exp(s - m)
                denom = jnp.sum(p, axis=-1, keepdims=True) + jnp.exp(sink - m)
                outs.append(_dot(p.astype(BF16), v2) / denom)
        o_ref[b] = jnp.concatenate(outs, axis=1)


def _swa(qkv, sinks, bias):
    batch, seq, _ = qkv.shape
    kcol = ATTN_WIDTH // KV_WIDTH
    prev = lambda i: jnp.maximum(i - 1, 0)
    return pl.pallas_call(
        _swa_kernel,
        name="swa",
        grid_spec=pltpu.PrefetchScalarGridSpec(
            num_scalar_prefetch=1,
            grid=(seq // BLOCK,),
            in_specs=[
                pl.BlockSpec((batch, BLOCK, ATTN_WIDTH), lambda i, s: (0, i, 0)),
                pl.BlockSpec((batch, BLOCK, KV_WIDTH), lambda i, s: (0, i, kcol)),
                pl.BlockSpec((batch, BLOCK, KV_WIDTH), lambda i, s: (0, i, kcol + 1)),
                pl.BlockSpec((batch, BLOCK, KV_WIDTH), lambda i, s: (0, prev(i), kcol)),
                pl.BlockSpec((batch, BLOCK, KV_WIDTH), lambda i, s: (0, prev(i), kcol + 1)),
                pl.BlockSpec((ATTN_HEADS, BLOCK, 2 * BLOCK), lambda i, s: (0, 0, 0)),
            ],
            out_specs=pl.BlockSpec((batch, BLOCK, ATTN_WIDTH), lambda i, s: (0, i, 0)),
        ),
        out_shape=jax.ShapeDtypeStruct((batch, seq, ATTN_WIDTH), F32),
        compiler_params=_params(("arbitrary",)),
    )(sinks, qkv, qkv, qkv, qkv, qkv, bias)


def _memkv_kernel(m_ref, g_ref, w_ref, k_ref, v_ref):
    u = _rms(m_ref[...], g_ref[...])
    kv = _dot(u.astype(BF16), w_ref[...])
    k_ref[...] = kv[:, :CROSS_WIDTH].astype(BF16)
    v_ref[...] = kv[:, CROSS_WIDTH:].astype(BF16)


def _memkv(mem2, g_mem, w_kv):
    n = mem2.shape[0]
    return pl.pallas_call(
        _memkv_kernel,
        out_shape=[jax.ShapeDtypeStruct((n, CROSS_WIDTH), BF16)] * 2,
        compiler_params=_params(),
    )(mem2, g_mem, w_kv)


def _mid_kernel(ys_ref, ya_ref, x_ref, wout_ref, bout_ref, gc_ref, wq_ref, k_ref, v_ref, wo_ref,
                gf_ref, wr_ref, br_ref, upper_ref, lower_ref,
                h2_ref, u3_ref, gate_ref, lp_ref, tcnt_ref, tbase_ref, run_ref):
    i = pl.program_id(0)

    @pl.when(i == 0)
    def _():
        run_ref[...] = jnp.zeros_like(run_ref)

    ycat = jnp.concatenate([ys_ref[...], ya_ref[...]], axis=1).astype(BF16)
    h1 = x_ref[...] + _dot(ycat, wout_ref[...]) + bout_ref[...]

    u2 = _rms(h1, gc_ref[...])
    q = _dot(u2.astype(BF16), wq_ref[...]).astype(BF16)
    outs = []
    for h in range(CROSS_HEADS):
        hs = slice(h * CROSS_HEAD_DIM, (h + 1) * CROSS_HEAD_DIM)
        s = _nt(q[:, hs], k_ref[:, hs]) * (CROSS_HEAD_DIM ** -0.5)
        m = jnp.max(s, axis=-1, keepdims=True)
        p = jnp.exp(s - m)
        outs.append(_dot(p.astype(BF16), v_ref[:, hs]) / jnp.sum(p, axis=-1, keepdims=True))
    o = jnp.concatenate(outs, axis=1).astype(BF16)
    h2 = h1 + _dot(o, wo_ref[...])
    h2_ref[...] = h2

    u3 = _rms(h2, gf_ref[...]).astype(BF16)
    u3_ref[...] = u3

    logits = _nt(wr_ref[...], u3) + br_ref[...]
    eid = lax.broadcasted_iota(I32, logits.shape, 0)
    vals, idxs, hots = [], [], []
    for _ in range(TOP_K):
        m = jnp.max(logits, axis=0, keepdims=True)
        ix = jnp.min(jnp.where(logits == m, eid, N_EXPERTS), axis=0, keepdims=True)
        hot = eid == ix
        logits = jnp.where(hot, -jnp.inf, logits)
        vals.append(m)
        idxs.append(ix)
        hots.append(hot)
    ex = [jnp.exp(v - vals[0]) for v in vals]
    tot = ex[0] + ex[1] + ex[2] + ex[3]
    gate_ref[...] = jnp.concatenate([e / tot for e in ex], axis=0)

    hot_f = [jnp.where(hot, 1.0, 0.0) for hot in hots]
    cnt_k = [jnp.sum(hf, axis=1, keepdims=True) for hf in hot_f]
    tile_cnt = cnt_k[0] + cnt_k[1] + cnt_k[2] + cnt_k[3]
    tile_cnt = jnp.floor((tile_cnt + (SUBLANES - 1)) * (1.0 / SUBLANES)) * SUBLANES
    offs = _dot_exact(lower_ref[...], jnp.broadcast_to(tile_cnt, (N_EXPERTS, LANES)))[:, :1]
    local = []
    for kk in range(TOP_K):
        before = _dot(hot_f[kk].astype(BF16), upper_ref[...])
        local.append(jnp.sum(jnp.where(hots[kk], before + offs, 0.0), axis=0, keepdims=True))
        offs = offs + cnt_k[kk]
    lp_ref[...] = jnp.concatenate(local, axis=0).astype(I32)
    tcnt_ref[0] = jnp.broadcast_to(tile_cnt, (N_EXPERTS, LANES)).astype(I32)
    tbase_ref[0] = jnp.broadcast_to(run_ref[...], (N_EXPERTS, LANES)).astype(I32)
    run_ref[...] = run_ref[...] + tile_cnt


def _mid(y_ssm, y_attn, x2, w_out, b_out, g_cross, w_q, k_mem, v_mem, w_o, g_ffn, w_r_t, b_r, seq, mem_len):
    n = x2.shape[0]
    tiles_per_batch = seq // TM_MID
    row = lambda i: (i, 0)
    col = lambda i: (0, i)
    fixed = lambda i: (0, 0)
    memb = lambda i: (i // tiles_per_batch, 0)
    tile3 = lambda i: (i, 0, 0)
    n_tiles = n // TM_MID
    upper = jnp.asarray(np.triu(np.ones((TM_MID, TM_MID), np.float32), 1)).astype(BF16)
    lower = jnp.asarray(np.tril(np.ones((N_EXPERTS, N_EXPERTS), np.float32), -1))
    return pl.pallas_call(
        _mid_kernel,
        name="mid",
        grid=(n_tiles,),
        in_specs=[
            pl.BlockSpec((TM_MID, SSM_WIDTH), row),
            pl.BlockSpec((TM_MID, ATTN_WIDTH), row),
            pl.BlockSpec((TM_MID, D_MODEL), row),
            pl.BlockSpec((D_MODEL, D_MODEL), fixed),
            pl.BlockSpec((1, D_MODEL), fixed),
            pl.BlockSpec((1, D_MODEL), fixed),
            pl.BlockSpec((D_MODEL, CROSS_WIDTH), fixed),
            pl.BlockSpec((mem_len, CROSS_WIDTH), memb),
            pl.BlockSpec((mem_len, CROSS_WIDTH), memb),
            pl.BlockSpec((CROSS_WIDTH, D_MODEL), fixed),
            pl.BlockSpec((1, D_MODEL), fixed),
            pl.BlockSpec((N_EXPERTS, D_MODEL), fixed),
            pl.BlockSpec((N_EXPERTS, 1), fixed),
            pl.BlockSpec((TM_MID, TM_MID), fixed),
            pl.BlockSpec((N_EXPERTS, N_EXPERTS), fixed),
        ],
        out_specs=[
            pl.BlockSpec((TM_MID, D_MODEL), row),
            pl.BlockSpec((TM_MID, D_MODEL), row),
            pl.BlockSpec((TOP_K, TM_MID), col),
            pl.BlockSpec((TOP_K, TM_MID), col),
            pl.BlockSpec((1, N_EXPERTS, LANES), tile3),
            pl.BlockSpec((1, N_EXPERTS, LANES), tile3),
        ],
        out_shape=[
            jax.ShapeDtypeStruct((n, D_MODEL), F32),
            jax.ShapeDtypeStruct((n, D_MODEL), BF16),
            jax.ShapeDtypeStruct((TOP_K, n), F32),
            jax.ShapeDtypeStruct((TOP_K, n), I32),
            jax.ShapeDtypeStruct((n_tiles, N_EXPERTS, LANES), I32),
            jax.ShapeDtypeStruct((n_tiles, N_EXPERTS, LANES), I32),
        ],
        scratch_shapes=[pltpu.VMEM((N_EXPERTS, 1), F32)],
        compiler_params=_params(("arbitrary",)),
    )(y_ssm, y_attn, x2, w_out, b_out, g_cross, w_q, k_mem, v_mem, w_o, g_ffn, w_r_t, b_r, upper, lower)


ROW_CHUNK = 256
LOCAL_ROWS = -(-(TM_MID * TOP_K + N_EXPERTS * (SUBLANES - 1)) // ROW_CHUNK) * ROW_CHUNK
RUN_BITS = tuple(1 << s for s in range(TM_MID.bit_length() - 1, SUBLANES.bit_length() - 2, -1))
PAD_BITS = tuple(1 << s for s in range((MOE_ROWS - 1).bit_length() - 1, SUBLANES.bit_length() - 2, -1))


def _run_copies(n, src_ref, src, dst_ref, dst, sem, bits, wait, advance_src=True):
    for bit in bits:
        step = n & bit

        @pl.when(step != 0)
        def _():
            cp = pltpu.make_async_copy(src_ref.at[pl.ds(pl.multiple_of(src, SUBLANES), bit)],
                                       dst_ref.at[pl.ds(pl.multiple_of(dst, SUBLANES), bit)], sem)
            if wait:
                cp.wait()
            else:
                cp.start()

        if advance_src:
            src = src + step
        dst = dst + step


def _dispatch_kernel(tcnt_ref, lo_ref, gs_ref, padn_ref, pads_ref, na_ref, u_ref, lp_ref, xs_ref,
                     xloc_ref, zero_ref, sem, zsem):
    i = pl.program_id(0)

    def zero_fill(wait):
        def body(e, carry):
            _run_copies(padn_ref[e], zero_ref, 0, xs_ref, pads_ref[e], zsem, PAD_BITS, wait, advance_src=False)
            return carry
        lax.fori_loop(0, N_EXPERTS, body, 0)

        def tail(b, carry):
            cp = pltpu.make_async_copy(zero_ref, xs_ref.at[pl.ds(pl.multiple_of(b * MOE_ROWS, MOE_ROWS), MOE_ROWS)], zsem)
            if wait:
                cp.wait()
            else:
                cp.start()
            return carry
        lax.fori_loop(na_ref[0], xs_ref.shape[0] // MOE_ROWS, tail, 0)

    @pl.when(i == 0)
    def _():
        zero_ref[...] = jnp.zeros_like(zero_ref)
        zero_fill(False)

    slot = i % 2
    u = u_ref[...]
    lp = lp_ref[...]
    for c in range(LOCAL_ROWS // ROW_CHUNK):
        r = lax.broadcasted_iota(I32, (ROW_CHUNK, TM_MID), 0) + c * ROW_CHUNK
        p = jnp.zeros((ROW_CHUNK, TM_MID), F32)
        for kk in range(TOP_K):
            p = jnp.where(lp[kk:kk + 1, :] == r, 1.0, p)
        xloc_ref[slot, c * ROW_CHUNK:(c + 1) * ROW_CHUNK, :] = _dot(p.astype(BF16), u)

    def runs(tile, wait):
        s = tile % 2

        def body(e, carry):
            j = tile * N_EXPERTS + e
            _run_copies(tcnt_ref[j], xloc_ref.at[s], lo_ref[j], xs_ref, gs_ref[j], sem.at[s], RUN_BITS, wait)
            return carry
        lax.fori_loop(0, N_EXPERTS, body, 0)

    runs(i, False)

    @pl.when(i > 0)
    def _():
        runs(i - 1, True)

    @pl.when(i == pl.num_programs(0) - 1)
    def _():
        runs(i, True)

    @pl.when(i == 0)
    def _():
        zero_fill(True)


def _dispatch(tcnt, lo, gstart, padn, pads, n_active, u3, lp, n_rows):
    n = u3.shape[0]
    return pl.pallas_call(
        _dispatch_kernel,
        name="dispatch",
        grid_spec=pltpu.PrefetchScalarGridSpec(
            num_scalar_prefetch=6,
            grid=(n // TM_MID,),
            in_specs=[
                pl.BlockSpec((TM_MID, D_MODEL), lambda i, *_: (i, 0)),
                pl.BlockSpec((TOP_K, TM_MID), lambda i, *_: (0, i)),
            ],
            out_specs=pl.BlockSpec(memory_space=pl.ANY),
            scratch_shapes=[
                pltpu.VMEM((2, LOCAL_ROWS, D_MODEL), F32),
                pltpu.VMEM((MOE_ROWS, D_MODEL), F32),
                pltpu.SemaphoreType.DMA((2,)),
                pltpu.SemaphoreType.DMA,
            ],
        ),
        out_shape=jax.ShapeDtypeStruct((n_rows, D_MODEL), F32),
        compiler_params=_params(("arbitrary",)),
    )(tcnt, lo, gstart, padn, pads, n_active, u3, lp)


PAIR_COLS = 2 * LANES


def _expert_kernel(be_ref, nx_ref, nv_ref, na_ref, x_ref, b1_ref, b2_ref, perm_ref, w1_hbm, w2_hbm, y_ref,
                   w1s_ref, w2s_ref, w1p_ref, w2b_ref, sem):
    b = pl.program_id(0)
    valid_rows = nv_ref[b]
    active = valid_rows > 0
    expert = be_ref[b]
    new_expert = jnp.logical_or(b == 0, expert != be_ref[jnp.maximum(b - 1, 0)])

    def fetch(e, wait):
        for src, dst, s in ((w1_hbm, w1s_ref, 0), (w2_hbm, w2s_ref, 1)):
            cp = pltpu.make_async_copy(src.at[e], dst, sem.at[s])
            if wait:
                cp.wait()
            else:
                cp.start(priority=1)

    @pl.when(b == 0)
    def _():
        fetch(expert, False)

    @pl.when(jnp.logical_and(active, new_expert))
    def _():
        fetch(expert, True)
        for c in range(2 * D_FF // PAIR_COLS):
            cs = slice(c * PAIR_COLS, (c + 1) * PAIR_COLS)
            w1p_ref[:, cs] = _dot(w1s_ref[:, cs].astype(BF16), perm_ref[...]).astype(BF16)
        w2b_ref[...] = w2s_ref[...].astype(BF16)

        @pl.when(nx_ref[b] >= 0)
        def _():
            fetch(nx_ref[b], False)

    for part in range(MOE_ROWS // MOE_PART):
        rows = slice(part * MOE_PART, (part + 1) * MOE_PART)
        live = valid_rows > part * MOE_PART

        @pl.when(live)
        def _():
            x = x_ref[rows, :].astype(BF16)
            hdn = _dot(x, w1p_ref[...]) + b1_ref[0]
            acts = []
            for c in range(2 * D_FF // PAIR_COLS):
                hg = jnp.minimum(hdn[:, c * PAIR_COLS:c * PAIR_COLS + LANES], SWIGLU_LIMIT)
                hl = jnp.clip(hdn[:, c * PAIR_COLS + LANES:(c + 1) * PAIR_COLS], -SWIGLU_LIMIT, SWIGLU_LIMIT)
                acts.append((hg * _sigmoid(SWIGLU_ALPHA * hg) * (hl + 1.0)).astype(BF16))
            y_ref[rows, :] = _dot(jnp.concatenate(acts, axis=1), w2b_ref[...]) + b2_ref[0]

        @pl.when(jnp.logical_not(live))
        def _():
            y_ref[rows, :] = jnp.zeros((MOE_PART, D_MODEL), F32)


def _experts(block_exp, next_exp, valid_rows, n_active, xs, w1, b1p, w2, b2):
    n_rows = xs.shape[0]
    n_blocks = n_rows // MOE_ROWS
    xmap = lambda b, be, nx, nv, na: (jnp.maximum(jnp.minimum(b, na[0] - 1), 0), 0)
    emap = lambda b, be, nx, nv, na: (be[b], 0, 0)
    perm = np.zeros((PAIR_COLS, PAIR_COLS), np.float32)
    perm[2 * np.arange(LANES), np.arange(LANES)] = 1.0
    perm[2 * np.arange(LANES) + 1, LANES + np.arange(LANES)] = 1.0
    return pl.pallas_call(
        _expert_kernel,
        name="experts",
        grid_spec=pltpu.PrefetchScalarGridSpec(
            num_scalar_prefetch=4,
            grid=(n_blocks,),
            in_specs=[
                pl.BlockSpec((MOE_ROWS, D_MODEL), xmap),
                pl.BlockSpec((1, 1, 2 * D_FF), emap),
                pl.BlockSpec((1, 1, D_MODEL), emap),
                pl.BlockSpec((PAIR_COLS, PAIR_COLS), lambda b, *_: (0, 0)),
                pl.BlockSpec(memory_space=pl.ANY),
                pl.BlockSpec(memory_space=pl.ANY),
            ],
            out_specs=pl.BlockSpec((MOE_ROWS, D_MODEL), lambda b, *_: (b, 0)),
            scratch_shapes=[
                pltpu.VMEM((D_MODEL, 2 * D_FF), F32),
                pltpu.VMEM((D_FF, D_MODEL), F32),
                pltpu.VMEM((D_MODEL, 2 * D_FF), BF16),
                pltpu.VMEM((D_FF, D_MODEL), BF16),
                pltpu.SemaphoreType.DMA((2,)),
            ],
        ),
        out_shape=jax.ShapeDtypeStruct((n_rows, D_MODEL), F32),
        compiler_params=_params(("arbitrary",)),
    )(block_exp, next_exp, valid_rows, n_active, xs, b1p, b2, jnp.asarray(perm).astype(BF16), w1, w2)


def _combine_kernel(tcnt_ref, lo_ref, gs_ref, h2_ref, gate_ref, lp_ref, g_ref, ys_ref, o_ref, yloc_ref, sem):
    i = pl.program_id(0)

    slot = i % 2

    def runs(tile, wait):
        s = tile % 2

        def body(e, carry):
            j = tile * N_EXPERTS + e
            _run_copies(tcnt_ref[j], ys_ref, gs_ref[j], yloc_ref.at[s], lo_ref[j], sem.at[s], RUN_BITS, wait)
            return carry
        lax.fori_loop(0, N_EXPERTS, body, 0)

    @pl.when(i == 0)
    def _():
        yloc_ref[...] = jnp.zeros_like(yloc_ref)
        runs(i, False)

    @pl.when(i + 1 < pl.num_programs(0))
    def _():
        runs(i + 1, False)

    gates = gate_ref[...].T
    lp = lp_ref[...].astype(F32).T
    runs(i, True)

    h = h2_ref[...]
    for c in range(LOCAL_ROWS // ROW_CHUNK):
        lane = (lax.broadcasted_iota(I32, (TM_MID, ROW_CHUNK), 1) + c * ROW_CHUNK).astype(F32)
        a = jnp.zeros((TM_MID, ROW_CHUNK), F32)
        for kk in range(TOP_K):
            a = jnp.where(lp[:, kk:kk + 1] == lane, gates[:, kk:kk + 1], a)
        h = h + _dot(a.astype(BF16), yloc_ref[slot, c * ROW_CHUNK:(c + 1) * ROW_CHUNK, :].astype(BF16))
    o_ref[...] = _rms(h, g_ref[...])


def _combine(tcnt, lo, gstart, h2, gates, lp, g_final, ys):
    n = h2.shape[0]
    return pl.pallas_call(
        _combine_kernel,
        name="combine",
        grid_spec=pltpu.PrefetchScalarGridSpec(
            num_scalar_prefetch=3,
            grid=(n // TM_MID,),
            in_specs=[
                pl.BlockSpec((TM_MID, D_MODEL), lambda i, *_: (i, 0)),
                pl.BlockSpec((TOP_K, TM_MID), lambda i, *_: (0, i)),
                pl.BlockSpec((TOP_K, TM_MID), lambda i, *_: (0, i)),
                pl.BlockSpec((1, D_MODEL), lambda i, *_: (0, 0)),
                pl.BlockSpec(memory_space=pl.ANY),
            ],
            out_specs=pl.BlockSpec((TM_MID, D_MODEL), lambda i, *_: (i, 0)),
            scratch_shapes=[
                pltpu.VMEM((2, LOCAL_ROWS, D_MODEL), F32),
                pltpu.SemaphoreType.DMA((2,)),
            ],
        ),
        out_shape=jax.ShapeDtypeStruct((n, D_MODEL), F32),
        compiler_params=_params(("arbitrary",)),
    )(tcnt, lo, gstart, h2, gates, lp, g_final, ys)


def _layer(h, mem2, batch, seq, mem_len, g_mix, w_in, b_qkv, conv_w, conv_b, dt_bias, a_log, d_skip,
           g_ssm_out, attn_sinks, bias, w_out, b_out, g_cross, g_mem, w_q_cross, w_kv_cross, w_o_cross,
           g_ffn, w_router, b_router, w1, b1, w2, b2, g_final):
    n = h.shape[0]
    c0, c1, c2 = SSM_WIDTH, SSM_WIDTH + CONV_CH, SSM_WIDTH + CONV_CH + SSM_HEADS
    w_cat = jnp.concatenate(
        [w_in[:, :c1], jnp.pad(w_in[:, c1:c2], ((0, 0), (0, DT_PAD - SSM_HEADS))), w_in[:, c2:]],
        axis=1).astype(BF16)
    z, xbc, dtp, qkv = _inproj(h, g_mix[None, :], w_cat, b_qkv[None, :])

    pad_h = (0, DT_PAD - SSM_HEADS)
    dt_bias_p = jnp.pad(dt_bias, pad_h)[None, :]
    a_head_p = jnp.pad(-jnp.exp(a_log), pad_h)[None, :]
    d_full = jnp.repeat(d_skip, SSM_HEAD_DIM)[None, :]
    per_seq = lambda t: t.reshape(batch, seq, t.shape[-1])
    y_ssm = _ssd(per_seq(xbc), per_seq(z), per_seq(dtp), conv_w, conv_b[None, :], dt_bias_p, a_head_p,
                 d_full, g_ssm_out[None, :]).reshape(n, SSM_WIDTH)
    y_attn = _swa(per_seq(qkv), attn_sinks, bias).reshape(n, ATTN_WIDTH)

    k_mem, v_mem = _memkv(mem2, g_mem[None, :], w_kv_cross.astype(BF16))
    h2, u3, gates, lp, tcnt, tbase = _mid(
        y_ssm, y_attn, h, w_out.astype(BF16), b_out[None, :], g_cross[None, :],
        w_q_cross.astype(BF16), k_mem, v_mem, w_o_cross.astype(BF16), g_ffn[None, :],
        w_router.T.astype(BF16), b_router[:, None], seq, mem_len)

    max_rows = n * TOP_K + (n // TM_MID) * N_EXPERTS * (SUBLANES - 1)
    n_blocks = -(-max_rows // MOE_ROWS) + N_EXPERTS
    tcnt = tcnt[:, :, 0]
    tbase = tbase[:, :, 0]
    counts = tbase[-1] + tcnt[-1]
    padded = (counts + MOE_ROWS - 1) // MOE_ROWS * MOE_ROWS
    padded_end = jnp.cumsum(padded)
    start = (padded_end - padded).astype(I32)
    n_active = (padded_end[-1] // MOE_ROWS).astype(I32)
    blk = jnp.arange(n_blocks, dtype=I32)
    blk_exp = jnp.sum((blk * MOE_ROWS)[:, None] >= padded_end[None, :], axis=1)
    blk_exp = jnp.minimum(blk_exp, N_EXPERTS - 1).astype(I32)
    blk_exp = jnp.where(blk < n_active, blk_exp, blk_exp[n_active - 1])
    eid = jnp.arange(N_EXPERTS, dtype=I32)
    later = jnp.logical_and(padded[None, :] > 0, eid[None, :] > eid[:, None])
    next_of = jnp.min(jnp.where(later, eid[None, :], N_EXPERTS), axis=1)
    next_of = jnp.where(next_of < N_EXPERTS, next_of, -1).astype(I32)
    blk_next = next_of[blk_exp]
    blk_valid = jnp.clip((start + counts)[blk_exp] - blk * MOE_ROWS, 0, MOE_ROWS)
    blk_valid = jnp.where(blk < n_active, blk_valid, 0).astype(I32)

    run_n = tcnt.reshape(-1)
    run_local = (jnp.cumsum(tcnt, axis=1) - tcnt).reshape(-1)
    run_global = (start[None, :] + tbase).reshape(-1)
    xs = _dispatch(run_n, run_local, run_global, padded - counts, start + counts, n_active[None], u3, lp,
                   n_blocks * MOE_ROWS)

    b1p = b1.reshape(N_EXPERTS, 2 * D_FF // PAIR_COLS, LANES, 2).transpose(0, 1, 3, 2)
    ys = _experts(blk_exp, blk_next, blk_valid, n_active[None], xs, w1,
                  b1p.reshape(N_EXPERTS, 1, 2 * D_FF), w2, b2[:, None, :])
    return _combine(run_n, run_local, run_global, h2, gates, lp, g_final, ys)


def kernel(x, mem, g_mix, w_in, b_qkv, conv_w, conv_b, dt_bias, a_log, d_skip, g_ssm_out, attn_sinks, rel_bias, w_out, b_out, g_cross, g_mem, w_q_cross, w_kv_cross, w_o_cross, g_ffn, w_router, b_router, w1, b1, w2, b2, g_final):
    batch, seq, d = x.shape
    mem_len = mem.shape[1]
    depth = g_mix.shape[0]
    assert depth == 1 and d == D_MODEL and seq % TM_MID == 0
    h = x.reshape(batch * seq, d)
    mem2 = mem.reshape(batch * mem_len, d)
    bias = _bias_table(rel_bias)
    out = _layer(
        h, mem2, batch, seq, mem_len, g_mix[0], w_in[0], b_qkv[0], conv_w[0], conv_b[0], dt_bias[0],
        a_log[0], d_skip[0], g_ssm_out[0], attn_sinks[0], bias, w_out[0], b_out[0], g_cross[0],
        g_mem[0], w_q_cross[0], w_kv_cross[0], w_o_cross[0], g_ffn[0], w_router[0], b_router[0],
        w1[0], b1[0], w2[0], b2[0], g_final[None, :])
    return out.reshape(batch, seq, d)
```

```python
import math

import numpy as np
import jax
import jax.numpy as jnp
from jax import lax
from jax.experimental import pallas as pl
from jax.experimental.pallas import tpu as pltpu

F32 = jnp.float32
BF16 = jnp.bfloat16
I32 = jnp.int32

D_MODEL = 1024
RMS_EPS = 1e-5

SSM_HEADS = 8
SSM_HEAD_DIM = 64
SSM_WIDTH = SSM_HEADS * SSM_HEAD_DIM
SSM_STATE = 128
SSM_GROUPS = 2
HEADS_PER_GROUP = SSM_HEADS // SSM_GROUPS
GROUP_WIDTH = SSM_WIDTH // SSM_GROUPS
CONV_WIDTH = 4
CHUNK = 128
CONV_CH = SSM_WIDTH + 2 * SSM_GROUPS * SSM_STATE

ATTN_HEADS = 8
KV_HEADS = 2
Q_PER_KV = ATTN_HEADS // KV_HEADS
HEAD_DIM = 64
ATTN_WIDTH = ATTN_HEADS * HEAD_DIM
KV_WIDTH = KV_HEADS * HEAD_DIM
WINDOW = 128
BLOCK = WINDOW
QKV_COLS = ATTN_WIDTH + 2 * KV_WIDTH

REL_BUCKETS = 32
REL_MAX_DIST = 128

CROSS_HEADS = 4
CROSS_HEAD_DIM = 128
CROSS_WIDTH = CROSS_HEADS * CROSS_HEAD_DIM

N_EXPERTS = 32
TOP_K = 4
D_FF = D_MODEL
SWIGLU_ALPHA = 1.702
SWIGLU_LIMIT = 7.0

LANES = 128
SUBLANES = 8
DT_PAD = LANES
PROJ_COLS = SSM_WIDTH + CONV_CH + DT_PAD + QKV_COLS
VMEM_LIMIT = 56 * 1024 * 1024

TM_PROJ = 512
TM_MID = 512
MID_PARTS = 1
MOE_ROWS = 512
MOE_PART = 512
WEIGHT_DMA_ROWS = 256


def _nt(a, b):
    return lax.dot_general(a, b, (((1,), (1,)), ((), ())), preferred_element_type=F32)


def _tn(a, b):
    return lax.dot_general(a, b, (((0,), (0,)), ((), ())), preferred_element_type=F32)


def _dot(a, b):
    return jnp.dot(a, b, preferred_element_type=F32)


def _dot_exact(a, b):
    return jnp.dot(a, b, preferred_element_type=F32, precision=lax.Precision.HIGHEST)


def _rms(x, g):
    return x * lax.rsqrt(jnp.mean(x * x, axis=-1, keepdims=True) + RMS_EPS) * g


def _sigmoid(x):
    return 1.0 / (1.0 + jnp.exp(-x))


def _params(sem=None):
    return pltpu.CompilerParams(dimension_semantics=sem, vmem_limit_bytes=VMEM_LIMIT)


def _inproj_kernel(x_ref, g_ref, w_ref, bq_ref, z_ref, xbc_ref, dt_ref, qkv_ref):
    u = _rms(x_ref[...], g_ref[...])
    p = _dot(u.astype(BF16), w_ref[...])
    z_ref[...] = p[:, :SSM_WIDTH]
    xbc_ref[...] = p[:, SSM_WIDTH:SSM_WIDTH + CONV_CH]
    dt_ref[...] = p[:, SSM_WIDTH + CONV_CH:SSM_WIDTH + CONV_CH + DT_PAD]
    qkv_ref[...] = p[:, SSM_WIDTH + CONV_CH + DT_PAD:] + bq_ref[...]


def _inproj(x2, g_mix, w_cat, b_qkv):
    n = x2.shape[0]
    row = lambda i: (i, 0)
    fixed = lambda i: (0, 0)
    return pl.pallas_call(
        _inproj_kernel,
        grid=(n // TM_PROJ,),
        in_specs=[
            pl.BlockSpec((TM_PROJ, D_MODEL), row),
            pl.BlockSpec((1, D_MODEL), fixed),
            pl.BlockSpec((D_MODEL, PROJ_COLS), fixed),
            pl.BlockSpec((1, QKV_COLS), fixed),
        ],
        out_specs=[
            pl.BlockSpec((TM_PROJ, SSM_WIDTH), row),
            pl.BlockSpec((TM_PROJ, CONV_CH), row),
            pl.BlockSpec((TM_PROJ, DT_PAD), row),
            pl.BlockSpec((TM_PROJ, QKV_COLS), row),
        ],
        out_shape=[
            jax.ShapeDtypeStruct((n, SSM_WIDTH), F32),
            jax.ShapeDtypeStruct((n, CONV_CH), F32),
            jax.ShapeDtypeStruct((n, DT_PAD), F32),
            jax.ShapeDtypeStruct((n, QKV_COLS), F32),
        ],
        compiler_params=_params(("arbitrary",)),
    )(x2, g_mix, w_cat, b_qkv)


def _split3(x):
    hi = x.astype(BF16)
    rest = x - hi.astype(F32)
    mid = rest.astype(BF16)
    lo = (rest - mid.astype(F32)).astype(BF16)
    return jnp.concatenate([hi, mid, lo], axis=1)


def _ssd_kernel(xbc_ref, z_ref, dt_ref, cw_ref, cb_ref, dtb_ref, ah_ref, dsk_ref, gout_ref,
                e64_ref, ecat_ref, tri_ref, y_ref, conv_ref, state_ref):
    c = pl.program_id(0)

    @pl.when(c == 0)
    def _():
        conv_ref[:, :SUBLANES, :] = jnp.zeros((conv_ref.shape[0], SUBLANES, CONV_CH), F32)
        state_ref[...] = jnp.zeros_like(state_ref)

    for b in range(xbc_ref.shape[0]):
        _ssd_chunk(b, xbc_ref, z_ref, dt_ref, cw_ref, cb_ref, dtb_ref, ah_ref, dsk_ref, gout_ref,
                   e64_ref, ecat_ref, tri_ref, y_ref, conv_ref, state_ref)


def _ssd_chunk(b, xbc_ref, z_ref, dt_ref, cw_ref, cb_ref, dtb_ref, ah_ref, dsk_ref, gout_ref,
               e64_ref, ecat_ref, tri_ref, y_ref, conv_ref, state_ref):
    u = xbc_ref[b]
    conv_ref[b, SUBLANES:, :] = u
    acc = u * cw_ref[CONV_WIDTH - 1:CONV_WIDTH, :] + cb_ref[...]
    for j in range(1, CONV_WIDTH):
        shifted = conv_ref[b, SUBLANES - j:SUBLANES - j + CHUNK, :]
        acc = acc + shifted * cw_ref[CONV_WIDTH - 1 - j:CONV_WIDTH - j, :]
    conv_ref[b, :SUBLANES, :] = u[CHUNK - SUBLANES:, :]
    xbc = acc * _sigmoid(acc)

    xs = xbc[:, :SSM_WIDTH]
    bm = xbc[:, SSM_WIDTH:SSM_WIDTH + SSM_GROUPS * SSM_STATE]
    cm = xbc[:, SSM_WIDTH + SSM_GROUPS * SSM_STATE:]

    dtr = dt_ref[b] + dtb_ref[...]
    dt = jnp.maximum(dtr, 0.0) + jnp.log1p(jnp.exp(-jnp.abs(dtr)))
    a = dt * ah_ref[...]
    a3 = _split3(a)
    a_rows = jnp.concatenate([a3[:, :DT_PAD], a3[:, DT_PAD:2 * DT_PAD], a3[:, 2 * DT_PAD:]], axis=0)
    cs = _dot(tri_ref[...], a_rows)
    cs_row = cs.T
    dt_full = _dot(_split3(dt), e64_ref[...])
    cs_cat = _dot(_split3(cs), ecat_ref[...])
    cs_wide = cs_cat[:, :SSM_HEADS * CHUNK]
    cs_full = cs_cat[:, SSM_HEADS * CHUNK:]
    cs_last = cs_full[CHUNK - 1:CHUNK, :]

    xdt = xs * dt_full
    xdt_b = xdt.astype(BF16)
    xw_b = (xdt * jnp.exp(cs_last - cs_full)).astype(BF16)
    in_decay = jnp.exp(cs_full)
    chunk_decay = jnp.exp(cs_last)

    li = lax.broadcasted_iota(I32, (CHUNK, CHUNK), 0)
    si = lax.broadcasted_iota(I32, (CHUNK, CHUNK), 1)
    causal = li >= si

    ys = []
    for g in range(SSM_GROUPS):
        bg = bm[:, g * SSM_STATE:(g + 1) * SSM_STATE].astype(BF16)
        cg = cm[:, g * SSM_STATE:(g + 1) * SSM_STATE].astype(BF16)
        cb = _nt(cg, bg)
        yd = []
        for r in range(HEADS_PER_GROUP):
            h = g * HEADS_PER_GROUP + r
            diff = cs_wide[:, h * CHUNK:(h + 1) * CHUNK] - cs_row[h:h + 1, :]
            m = cb * jnp.exp(jnp.where(causal, diff, -jnp.inf))
            yd.append(_dot(m.astype(BF16), xdt_b[:, h * SSM_HEAD_DIM:(h + 1) * SSM_HEAD_DIM]))
        y_diag = jnp.concatenate(yd, axis=1)
        gs = slice(g * GROUP_WIDTH, (g + 1) * GROUP_WIDTH)
        st = state_ref[b, g]
        y_off = _dot(cg, st.astype(BF16)) * in_decay[:, gs]
        state_ref[b, g] = st * chunk_decay[:, gs] + _tn(bg, xw_b[:, gs])
        ys.append(y_diag + y_off)
    y = jnp.concatenate(ys, axis=1) + dsk_ref[...] * xs

    zz = z_ref[b]
    y = y * (zz * _sigmoid(zz))
    outs = []
    for g in range(SSM_GROUPS):
        yg = y[:, g * GROUP_WIDTH:(g + 1) * GROUP_WIDTH]
        outs.append(yg * lax.rsqrt(jnp.mean(yg * yg, axis=-1, keepdims=True) + RMS_EPS))
    y_ref[b] = jnp.concatenate(outs, axis=1) * gout_ref[...]


def _ssd(xbc, z, dtp, conv_w, conv_b, dt_bias_p, a_head_p, d_full, g_out):
    batch, seq, _ = xbc.shape
    chunk = lambda c: (0, c, 0)
    fixed = lambda c: (0, 0)
    head = np.arange(SSM_WIDTH) // SSM_HEAD_DIM
    e64 = (np.arange(DT_PAD)[:, None] == head[None, :]).astype(np.float32)
    head_w = np.arange(SSM_HEADS * CHUNK) // CHUNK
    e128 = (np.arange(DT_PAD)[:, None] == head_w[None, :]).astype(np.float32)
    e64_3 = jnp.asarray(np.tile(e64, (3, 1))).astype(BF16)
    ecat_3 = jnp.asarray(np.tile(np.concatenate([e128, e64], axis=1), (3, 1))).astype(BF16)
    tri_3 = jnp.asarray(np.tile(np.tril(np.ones((CHUNK, CHUNK), np.float32)), (1, 3))).astype(BF16)
    return pl.pallas_call(
        _ssd_kernel,
        name="ssd",
        grid=(seq // CHUNK,),
        in_specs=[
            pl.BlockSpec((batch, CHUNK, CONV_CH), chunk),
            pl.BlockSpec((batch, CHUNK, SSM_WIDTH), chunk),
            pl.BlockSpec((batch, CHUNK, DT_PAD), chunk),
            pl.BlockSpec((CONV_WIDTH, CONV_CH), fixed),
            pl.BlockSpec((1, CONV_CH), fixed),
            pl.BlockSpec((1, DT_PAD), fixed),
            pl.BlockSpec((1, DT_PAD), fixed),
            pl.BlockSpec((1, SSM_WIDTH), fixed),
            pl.BlockSpec((1, SSM_WIDTH), fixed),
            pl.BlockSpec((3 * DT_PAD, SSM_WIDTH), fixed),
            pl.BlockSpec((3 * DT_PAD, SSM_HEADS * CHUNK + SSM_WIDTH), fixed),
            pl.BlockSpec((CHUNK, 3 * CHUNK), fixed),
        ],
        out_specs=pl.BlockSpec((batch, CHUNK, SSM_WIDTH), chunk),
        out_shape=jax.ShapeDtypeStruct((batch, seq, SSM_WIDTH), F32),
        scratch_shapes=[
            pltpu.VMEM((batch, SUBLANES + CHUNK, CONV_CH), F32),
            pltpu.VMEM((batch, SSM_GROUPS, SSM_STATE, GROUP_WIDTH), F32),
        ],
        compiler_params=_params(("arbitrary",)),
    )(xbc, z, dtp, conv_w, conv_b, dt_bias_p, a_head_p, d_full, g_out, e64_3, ecat_3, tri_3)


def _t5_bucket_table():
    q_pos = np.arange(BLOCK)[:, None]
    k_pos = np.arange(2 * BLOCK)[None, :] - BLOCK
    dist = q_pos - k_pos
    d = np.maximum(dist, 0)
    max_exact = REL_BUCKETS // 2
    ratio = np.maximum(d, max_exact).astype(np.float32) / np.float32(max_exact)
    large = max_exact + (np.log(ratio) / np.float32(math.log(REL_MAX_DIST / max_exact))
                         * np.float32(REL_BUCKETS - max_exact)).astype(np.int32)
    large = np.minimum(large, REL_BUCKETS - 1)
    bucket = np.where(d < max_exact, d, large)
    in_window = (dist >= 0) & (dist < WINDOW)
    return np.where(in_window, bucket, -1).astype(np.int32)


def _bias_kernel(rb_ref, bucket_ref, out_ref):
    bucket = bucket_ref[...]
    for h in range(ATTN_HEADS):
        acc = jnp.zeros(bucket.shape, F32)
        for b in range(REL_BUCKETS):
            acc = jnp.where(bucket == b, rb_ref[b, h], acc)
        out_ref[h] = jnp.where(bucket >= 0, acc, -jnp.inf)


def _bias_table(rel_bias):
    bucket = jnp.asarray(_t5_bucket_table())
    return pl.pallas_call(
        _bias_kernel,
        in_specs=[
            pl.BlockSpec(memory_space=pltpu.SMEM),
            pl.BlockSpec(memory_space=pltpu.VMEM),
        ],
        out_specs=pl.BlockSpec(memory_space=pltpu.VMEM),
        out_shape=jax.ShapeDtypeStruct((ATTN_HEADS, BLOCK, 2 * BLOCK), F32),
    )(rel_bias, bucket)


def _swa_kernel(sink_ref, q_ref, k_ref, v_ref, kp_ref, vp_ref, bias_ref, o_ref):
    i = pl.program_id(0)
    col = lax.broadcasted_iota(I32, (BLOCK, 2 * BLOCK), 1)
    key_ok = jnp.logical_or(col >= BLOCK, i > 0)
    for b in range(q_ref.shape[0]):
        q = (q_ref[b] * (HEAD_DIM ** -0.5)).astype(BF16)
        outs = []
        for g in range(KV_HEADS):
            gs = slice(g * HEAD_DIM, (g + 1) * HEAD_DIM)
            k2 = jnp.concatenate([kp_ref[b, :, gs], k_ref[b, :, gs]], axis=0).astype(BF16)
            v2 = jnp.concatenate([vp_ref[b, :, gs], v_ref[b, :, gs]], axis=0).astype(BF16)
            for r in range(Q_PER_KV):
                h = g * Q_PER_KV + r
                s = _nt(q[:, h * HEAD_DIM:(h + 1) * HEAD_DIM], k2) + bias_ref[h]
                s = jnp.where(key_ok, s, -jnp.inf)
                sink = sink_ref[h]
                m = jnp.maximum(jnp.max(s, axis=-1, keepdims=True), sink)
                p = jnp.exp(s - m)
                denom = jnp.sum(p, axis=-1, keepdims=True) + jnp.exp(sink - m)
                outs.append(_dot(p.astype(BF16), v2) / denom)
        o_ref[b] = jnp.concatenate(outs, axis=1)


def _swa(qkv, sinks, bias):
    batch, seq, _ = qkv.shape
    kcol = ATTN_WIDTH // KV_WIDTH
    prev = lambda i: jnp.maximum(i - 1, 0)
    return pl.pallas_call(
        _swa_kernel,
        name="swa",
        grid_spec=pltpu.PrefetchScalarGridSpec(
            num_scalar_prefetch=1,
            grid=(seq // BLOCK,),
            in_specs=[
                pl.BlockSpec((batch, BLOCK, ATTN_WIDTH), lambda i, s: (0, i, 0)),
                pl.BlockSpec((batch, BLOCK, KV_WIDTH), lambda i, s: (0, i, kcol)),
                pl.BlockSpec((batch, BLOCK, KV_WIDTH), lambda i, s: (0, i, kcol + 1)),
                pl.BlockSpec((batch, BLOCK, KV_WIDTH), lambda i, s: (0, prev(i), kcol)),
                pl.BlockSpec((batch, BLOCK, KV_WIDTH), lambda i, s: (0, prev(i), kcol + 1)),
                pl.BlockSpec((ATTN_HEADS, BLOCK, 2 * BLOCK), lambda i, s: (0, 0, 0)),
            ],
            out_specs=pl.BlockSpec((batch, BLOCK, ATTN_WIDTH), lambda i, s: (0, i, 0)),
        ),
        out_shape=jax.ShapeDtypeStruct((batch, seq, ATTN_WIDTH), F32),
        compiler_params=_params(("arbitrary",)),
    )(sinks, qkv, qkv, qkv, qkv, qkv, bias)


def _memkv_kernel(m_ref, g_ref, w_ref, k_ref, v_ref):
    u = _rms(m_ref[...], g_ref[...])
    kv = _dot(u.astype(BF16), w_ref[...])
    k_ref[...] = kv[:, :CROSS_WIDTH].astype(BF16)
    v_ref[...] = kv[:, CROSS_WIDTH:].astype(BF16)


def _memkv(mem2, g_mem, w_kv):
    n = mem2.shape[0]
    return pl.pallas_call(
        _memkv_kernel,
        out_shape=[jax.ShapeDtypeStruct((n, CROSS_WIDTH), BF16)] * 2,
        compiler_params=_params(),
    )(mem2, g_mem, w_kv)


def _mid_kernel(ys_ref, ya_ref, x_ref, wout_ref, bout_ref, gc_ref, wq_ref, k_ref, v_ref, wo_ref,
                gf_ref, wr_ref, br_ref, upper_ref, lower_ref,
                h2_ref, u3_ref, gate_ref, lp_ref, tcnt_ref, tbase_ref, run_ref):
    i = pl.program_id(0)

    @pl.when(i == 0)
    def _():
        run_ref[...] = jnp.zeros_like(run_ref)

    u3_parts = []
    for part in range(MID_PARTS):
        rows = slice(part * (TM_MID // MID_PARTS), (part + 1) * (TM_MID // MID_PARTS))
        ycat = jnp.concatenate([ys_ref[rows, :], ya_ref[rows, :]], axis=1).astype(BF16)
        h1 = x_ref[rows, :] + _dot(ycat, wout_ref[...]) + bout_ref[...]

        u2 = _rms(h1, gc_ref[...])
        q = _dot(u2.astype(BF16), wq_ref[...]).astype(BF16)
        outs = []
        for h in range(CROSS_HEADS):
            hs = slice(h * CROSS_HEAD_DIM, (h + 1) * CROSS_HEAD_DIM)
            s = _nt(q[:, hs], k_ref[:, hs]) * (CROSS_HEAD_DIM ** -0.5)
            m = jnp.max(s, axis=-1, keepdims=True)
            p = jnp.exp(s - m)
            outs.append(_dot(p.astype(BF16), v_ref[:, hs]) / jnp.sum(p, axis=-1, keepdims=True))
        o = jnp.concatenate(outs, axis=1).astype(BF16)
        h2 = h1 + _dot(o, wo_ref[...])
        h2_ref[rows, :] = h2
        u3_parts.append(_rms(h2, gf_ref[...]).astype(BF16))
    u3 = jnp.concatenate(u3_parts, axis=0)
    u3_ref[...] = u3

    logits = _nt(wr_ref[...], u3) + br_ref[...]
    eid = lax.broadcasted_iota(I32, logits.shape, 0)
    vals, idxs, hots = [], [], []
    for _ in range(TOP_K):
        m = jnp.max(logits, axis=0, keepdims=True)
        ix = jnp.min(jnp.where(logits == m, eid, N_EXPERTS), axis=0, keepdims=True)
        hot = eid == ix
        logits = jnp.where(hot, -jnp.inf, logits)
        vals.append(m)
        idxs.append(ix)
        hots.append(hot)
    ex = [jnp.exp(v - vals[0]) for v in vals]
    tot = ex[0] + ex[1] + ex[2] + ex[3]
    gate_ref[...] = jnp.concatenate([e / tot for e in ex], axis=0)

    hot_f = [jnp.where(hot, 1.0, 0.0) for hot in hots]
    cnt_k = [jnp.sum(hf, axis=1, keepdims=True) for hf in hot_f]
    tile_cnt = cnt_k[0] + cnt_k[1] + cnt_k[2] + cnt_k[3]
    tile_cnt = jnp.floor((tile_cnt + (SUBLANES - 1)) * (1.0 / SUBLANES)) * SUBLANES
    offs = _dot_exact(lower_ref[...], jnp.broadcast_to(tile_cnt, (N_EXPERTS, LANES)))[:, :1]
    local = []
    for kk in range(TOP_K):
        before = _dot(hot_f[kk].astype(BF16), upper_ref[...])
        local.append(jnp.sum(jnp.where(hots[kk], before + offs, 0.0), axis=0, keepdims=True))
        offs = offs + cnt_k[kk]
    lp_ref[...] = jnp.concatenate(local, axis=0).astype(I32)
    tcnt_ref[0] = jnp.broadcast_to(tile_cnt, (N_EXPERTS, LANES)).astype(I32)
    tbase_ref[0] = jnp.broadcast_to(run_ref[...], (N_EXPERTS, LANES)).astype(I32)
    run_ref[...] = run_ref[...] + tile_cnt


def _mid(y_ssm, y_attn, x2, w_out, b_out, g_cross, w_q, k_mem, v_mem, w_o, g_ffn, w_r_t, b_r, seq, mem_len):
    n = x2.shape[0]
    tiles_per_batch = seq // TM_MID
    row = lambda i: (i, 0)
    col = lambda i: (0, i)
    fixed = lambda i: (0, 0)
    memb = lambda i: (i // tiles_per_batch, 0)
    tile3 = lambda i: (i, 0, 0)
    n_tiles = n // TM_MID
    upper = jnp.asarray(np.triu(np.ones((TM_MID, TM_MID), np.float32), 1)).astype(BF16)
    lower = jnp.asarray(np.tril(np.ones((N_EXPERTS, N_EXPERTS), np.float32), -1))
    return pl.pallas_call(
        _mid_kernel,
        name="mid",
        grid=(n_tiles,),
        in_specs=[
            pl.BlockSpec((TM_MID, SSM_WIDTH), row),
            pl.BlockSpec((TM_MID, ATTN_WIDTH), row),
            pl.BlockSpec((TM_MID, D_MODEL), row),
            pl.BlockSpec((D_MODEL, D_MODEL), fixed),
            pl.BlockSpec((1, D_MODEL), fixed),
            pl.BlockSpec((1, D_MODEL), fixed),
            pl.BlockSpec((D_MODEL, CROSS_WIDTH), fixed),
            pl.BlockSpec((mem_len, CROSS_WIDTH), memb),
            pl.BlockSpec((mem_len, CROSS_WIDTH), memb),
            pl.BlockSpec((CROSS_WIDTH, D_MODEL), fixed),
            pl.BlockSpec((1, D_MODEL), fixed),
            pl.BlockSpec((N_EXPERTS, D_MODEL), fixed),
            pl.BlockSpec((N_EXPERTS, 1), fixed),
            pl.BlockSpec((TM_MID, TM_MID), fixed),
            pl.BlockSpec((N_EXPERTS, N_EXPERTS), fixed),
        ],
        out_specs=[
            pl.BlockSpec((TM_MID, D_MODEL), row),
            pl.BlockSpec((TM_MID, D_MODEL), row),
            pl.BlockSpec((TOP_K, TM_MID), col),
            pl.BlockSpec((TOP_K, TM_MID), col),
            pl.BlockSpec((1, N_EXPERTS, LANES), tile3),
            pl.BlockSpec((1, N_EXPERTS, LANES), tile3),
        ],
        out_shape=[
            jax.ShapeDtypeStruct((n, D_MODEL), F32),
            jax.ShapeDtypeStruct((n, D_MODEL), BF16),
            jax.ShapeDtypeStruct((TOP_K, n), F32),
            jax.ShapeDtypeStruct((TOP_K, n), I32),
            jax.ShapeDtypeStruct((n_tiles, N_EXPERTS, LANES), I32),
            jax.ShapeDtypeStruct((n_tiles, N_EXPERTS, LANES), I32),
        ],
        scratch_shapes=[pltpu.VMEM((N_EXPERTS, 1), F32)],
        compiler_params=_params(("arbitrary",)),
    )(y_ssm, y_attn, x2, w_out, b_out, g_cross, w_q, k_mem, v_mem, w_o, g_ffn, w_r_t, b_r, upper, lower)


ROW_CHUNK = 256
LOCAL_ROWS = -(-(TM_MID * TOP_K + N_EXPERTS * (SUBLANES - 1)) // ROW_CHUNK) * ROW_CHUNK
RUN_BITS = tuple(1 << s for s in range(TM_MID.bit_length() - 1, SUBLANES.bit_length() - 2, -1))
PAD_BITS = tuple(1 << s for s in range((MOE_ROWS - 1).bit_length() - 1, SUBLANES.bit_length() - 2, -1))


def _run_copies(n, src_ref, src, dst_ref, dst, sem, bits, wait, advance_src=True):
    for bit in bits:
        step = n & bit

        @pl.when(step != 0)
        def _():
            cp = pltpu.make_async_copy(src_ref.at[pl.ds(pl.multiple_of(src, SUBLANES), bit)],
                                       dst_ref.at[pl.ds(pl.multiple_of(dst, SUBLANES), bit)], sem)
            if wait:
                cp.wait()
            else:
                cp.start()

        if advance_src:
            src = src + step
        dst = dst + step


def _dispatch_kernel(tcnt_ref, lo_ref, gs_ref, padn_ref, pads_ref, na_ref, u_ref, lp_ref, xs_ref,
                     xloc_ref, zero_ref, sem, zsem):
    i = pl.program_id(0)

    def zero_fill(wait):
        def body(e, carry):
            _run_copies(padn_ref[e], zero_ref, 0, xs_ref, pads_ref[e], zsem, PAD_BITS, wait, advance_src=False)
            return carry
        lax.fori_loop(0, N_EXPERTS, body, 0)

        def tail(b, carry):
            cp = pltpu.make_async_copy(zero_ref, xs_ref.at[pl.ds(pl.multiple_of(b * MOE_ROWS, MOE_ROWS), MOE_ROWS)], zsem)
            if wait:
                cp.wait()
            else:
                cp.start()
            return carry
        lax.fori_loop(na_ref[0], xs_ref.shape[0] // MOE_ROWS, tail, 0)

    @pl.when(i == 0)
    def _():
        zero_ref[...] = jnp.zeros_like(zero_ref)
        zero_fill(False)

    slot = i % 2
    u = u_ref[...]
    lp = lp_ref[...]
    for c in range(LOCAL_ROWS // ROW_CHUNK):
        r = lax.broadcasted_iota(I32, (ROW_CHUNK, TM_MID), 0) + c * ROW_CHUNK
        p = jnp.zeros((ROW_CHUNK, TM_MID), F32)
        for kk in range(TOP_K):
            p = jnp.where(lp[kk:kk + 1, :] == r, 1.0, p)
        xloc_ref[slot, c * ROW_CHUNK:(c + 1) * ROW_CHUNK, :] = _dot(p.astype(BF16), u)

    def runs(tile, wait):
        s = tile % 2

        def body(e, carry):
            j = tile * N_EXPERTS + e
            _run_copies(tcnt_ref[j], xloc_ref.at[s], lo_ref[j], xs_ref, gs_ref[j], sem.at[s], RUN_BITS, wait)
            return carry
        lax.fori_loop(0, N_EXPERTS, body, 0)

    runs(i, False)

    @pl.when(i > 0)
    def _():
        runs(i - 1, True)

    @pl.when(i == pl.num_programs(0) - 1)
    def _():
        runs(i, True)

    @pl.when(i == 0)
    def _():
        zero_fill(True)


def _dispatch(tcnt, lo, gstart, padn, pads, n_active, u3, lp, n_rows):
    n = u3.shape[0]
    return pl.pallas_call(
        _dispatch_kernel,
        name="dispatch",
        grid_spec=pltpu.PrefetchScalarGridSpec(
            num_scalar_prefetch=6,
            grid=(n // TM_MID,),
            in_specs=[
                pl.BlockSpec((TM_MID, D_MODEL), lambda i, *_: (i, 0)),
                pl.BlockSpec((TOP_K, TM_MID), lambda i, *_: (0, i)),
            ],
            out_specs=pl.BlockSpec(memory_space=pl.ANY),
            scratch_shapes=[
                pltpu.VMEM((2, LOCAL_ROWS, D_MODEL), F32),
                pltpu.VMEM((MOE_ROWS, D_MODEL), F32),
                pltpu.SemaphoreType.DMA((2,)),
                pltpu.SemaphoreType.DMA,
            ],
        ),
        out_shape=jax.ShapeDtypeStruct((n_rows, D_MODEL), F32),
        compiler_params=_params(("arbitrary",)),
    )(tcnt, lo, gstart, padn, pads, n_active, u3, lp)


PAIR_COLS = 2 * LANES


def _expert_kernel(be_ref, nx_ref, nv_ref, na_ref, x_ref, b1_ref, b2_ref, perm_ref, w1_hbm, w2_hbm, y_ref,
                   w1s_ref, w2s_ref, w1p_ref, w2b_ref, sem):
    b = pl.program_id(0)
    valid_rows = nv_ref[b]
    active = valid_rows > 0
    expert = be_ref[b]
    new_expert = jnp.logical_or(b == 0, expert != be_ref[jnp.maximum(b - 1, 0)])

    def fetch(e, wait):
        for src, dst, s in ((w1_hbm, w1s_ref, 0), (w2_hbm, w2s_ref, 1)):
            for r in range(0, src.shape[1], WEIGHT_DMA_ROWS):
                rs = pl.ds(r, WEIGHT_DMA_ROWS)
                cp = pltpu.make_async_copy(src.at[e, rs], dst.at[rs], sem.at[s])
                if wait:
                    cp.wait()
                else:
                    cp.start(priority=1)

    @pl.when(b == 0)
    def _():
        fetch(expert, False)

    @pl.when(jnp.logical_and(active, new_expert))
    def _():
        fetch(expert, True)
        for c in range(2 * D_FF // PAIR_COLS):
            cs = slice(c * PAIR_COLS, (c + 1) * PAIR_COLS)
            w1p_ref[:, cs] = _dot(w1s_ref[:, cs].astype(BF16), perm_ref[...]).astype(BF16)
        w2b_ref[...] = w2s_ref[...].astype(BF16)

        @pl.when(nx_ref[b] >= 0)
        def _():
            fetch(nx_ref[b], False)

    for part in range(MOE_ROWS // MOE_PART):
        rows = slice(part * MOE_PART, (part + 1) * MOE_PART)
        live = valid_rows > part * MOE_PART

        @pl.when(live)
        def _():
            x = x_ref[rows, :].astype(BF16)
            hdn = _dot(x, w1p_ref[...]) + b1_ref[0]
            acts = []
            for c in range(2 * D_FF // PAIR_COLS):
                hg = jnp.minimum(hdn[:, c * PAIR_COLS:c * PAIR_COLS + LANES], SWIGLU_LIMIT)
                hl = jnp.clip(hdn[:, c * PAIR_COLS + LANES:(c + 1) * PAIR_COLS], -SWIGLU_LIMIT, SWIGLU_LIMIT)
                acts.append((hg * _sigmoid(SWIGLU_ALPHA * hg) * (hl + 1.0)).astype(BF16))
            y_ref[rows, :] = _dot(jnp.concatenate(acts, axis=1), w2b_ref[...]) + b2_ref[0]

        @pl.when(jnp.logical_not(live))
        def _():
            y_ref[rows, :] = jnp.zeros((MOE_PART, D_MODEL), F32)


def _experts(block_exp, next_exp, valid_rows, n_active, xs, w1, b1p, w2, b2):
    n_rows = xs.shape[0]
    n_blocks = n_rows // MOE_ROWS
    xmap = lambda b, be, nx, nv, na: (jnp.maximum(jnp.minimum(b, na[0] - 1), 0), 0)
    emap = lambda b, be, nx, nv, na: (be[b], 0, 0)
    perm = np.zeros((PAIR_COLS, PAIR_COLS), np.float32)
    perm[2 * np.arange(LANES), np.arange(LANES)] = 1.0
    perm[2 * np.arange(LANES) + 1, LANES + np.arange(LANES)] = 1.0
    return pl.pallas_call(
        _expert_kernel,
        name="experts",
        grid_spec=pltpu.PrefetchScalarGridSpec(
            num_scalar_prefetch=4,
            grid=(n_blocks,),
            in_specs=[
                pl.BlockSpec((MOE_ROWS, D_MODEL), xmap),
                pl.BlockSpec((1, 1, 2 * D_FF), emap),
                pl.BlockSpec((1, 1, D_MODEL), emap),
                pl.BlockSpec((PAIR_COLS, PAIR_COLS), lambda b, *_: (0, 0)),
                pl.BlockSpec(memory_space=pl.ANY),
                pl.BlockSpec(memory_space=pl.ANY),
            ],
            out_specs=pl.BlockSpec((MOE_ROWS, D_MODEL), lambda b, *_: (b, 0)),
            scratch_shapes=[
                pltpu.VMEM((D_MODEL, 2 * D_FF), F32),
                pltpu.VMEM((D_FF, D_MODEL), F32),
                pltpu.VMEM((D_MODEL, 2 * D_FF), BF16),
                pltpu.VMEM((D_FF, D_MODEL), BF16),
                pltpu.SemaphoreType.DMA((2,)),
            ],
        ),
        out_shape=jax.ShapeDtypeStruct((n_rows, D_MODEL), F32),
        compiler_params=_params(("arbitrary",)),
    )(block_exp, next_exp, valid_rows, n_active, xs, b1p, b2, jnp.asarray(perm).astype(BF16), w1, w2)


def _combine_kernel(tcnt_ref, lo_ref, gs_ref, h2_ref, gate_ref, lp_ref, g_ref, ys_ref, o_ref, yloc_ref, sem):
    i = pl.program_id(0)

    slot = i % 2

    def runs(tile, wait):
        s = tile % 2

        def body(e, carry):
            j = tile * N_EXPERTS + e
            _run_copies(tcnt_ref[j], ys_ref, gs_ref[j], yloc_ref.at[s], lo_ref[j], sem.at[s], RUN_BITS, wait)
            return carry
        lax.fori_loop(0, N_EXPERTS, body, 0)

    @pl.when(i == 0)
    def _():
        yloc_ref[...] = jnp.zeros_like(yloc_ref)
        runs(i, False)

    @pl.when(i + 1 < pl.num_programs(0))
    def _():
        runs(i + 1, False)

    gates = gate_ref[...].T
    lp = lp_ref[...].astype(F32).T
    runs(i, True)

    h = h2_ref[...]
    for c in range(LOCAL_ROWS // ROW_CHUNK):
        lane = (lax.broadcasted_iota(I32, (TM_MID, ROW_CHUNK), 1) + c * ROW_CHUNK).astype(F32)
        a = jnp.zeros((TM_MID, ROW_CHUNK), F32)
        for kk in range(TOP_K):
            a = jnp.where(lp[:, kk:kk + 1] == lane, gates[:, kk:kk + 1], a)
        h = h + _dot(a.astype(BF16), yloc_ref[slot, c * ROW_CHUNK:(c + 1) * ROW_CHUNK, :].astype(BF16))
    o_ref[...] = _rms(h, g_ref[...])


def _combine(tcnt, lo, gstart, h2, gates, lp, g_final, ys):
    n = h2.shape[0]
    return pl.pallas_call(
        _combine_kernel,
        name="combine",
        grid_spec=pltpu.PrefetchScalarGridSpec(
            num_scalar_prefetch=3,
            grid=(n // TM_MID,),
            in_specs=[
                pl.BlockSpec((TM_MID, D_MODEL), lambda i, *_: (i, 0)),
                pl.BlockSpec((TOP_K, TM_MID), lambda i, *_: (0, i)),
                pl.BlockSpec((TOP_K, TM_MID), lambda i, *_: (0, i)),
                pl.BlockSpec((1, D_MODEL), lambda i, *_: (0, 0)),
                pl.BlockSpec(memory_space=pl.ANY),
            ],
            out_specs=pl.BlockSpec((TM_MID, D_MODEL), lambda i, *_: (i, 0)),
            scratch_shapes=[
                pltpu.VMEM((2, LOCAL_ROWS, D_MODEL), F32),
                pltpu.SemaphoreType.DMA((2,)),
            ],
        ),
        out_shape=jax.ShapeDtypeStruct((n, D_MODEL), F32),
        compiler_params=_params(("arbitrary",)),
    )(tcnt, lo, gstart, h2, gates, lp, g_final, ys)


def _layer(h, mem2, batch, seq, mem_len, g_mix, w_in, b_qkv, conv_w, conv_b, dt_bias, a_log, d_skip,
           g_ssm_out, attn_sinks, bias, w_out, b_out, g_cross, g_mem, w_q_cross, w_kv_cross, w_o_cross,
           g_ffn, w_router, b_router, w1, b1, w2, b2, g_final):
    n = h.shape[0]
    c1, c2 = SSM_WIDTH + CONV_CH, SSM_WIDTH + CONV_CH + SSM_HEADS
    w_cat = jnp.concatenate(
        [w_in[:, :c1], jnp.pad(w_in[:, c1:c2], ((0, 0), (0, DT_PAD - SSM_HEADS))), w_in[:, c2:]],
        axis=1).astype(BF16)
    z, xbc, dtp, qkv = _inproj(h, g_mix[None, :], w_cat, b_qkv[None, :])

    pad_h = (0, DT_PAD - SSM_HEADS)
    dt_bias_p = jnp.pad(dt_bias, pad_h)[None, :]
    a_head_p = jnp.pad(-jnp.exp(a_log), pad_h)[None, :]
    d_full = jnp.repeat(d_skip, SSM_HEAD_DIM)[None, :]
    per_seq = lambda t: t.reshape(batch, seq, t.shape[-1])
    y_ssm = _ssd(per_seq(xbc), per_seq(z), per_seq(dtp), conv_w, conv_b[None, :], dt_bias_p, a_head_p,
                 d_full, g_ssm_out[None, :]).reshape(n, SSM_WIDTH)
    y_attn = _swa(per_seq(qkv), attn_sinks, bias).reshape(n, ATTN_WIDTH)

    k_mem, v_mem = _memkv(mem2, g_mem[None, :], w_kv_cross.astype(BF16))
    h2, u3, gates, lp, tcnt, tbase = _mid(
        y_ssm, y_attn, h, w_out.astype(BF16), b_out[None, :], g_cross[None, :],
        w_q_cross.astype(BF16), k_mem, v_mem, w_o_cross.astype(BF16), g_ffn[None, :],
        w_router.T.astype(BF16), b_router[:, None], seq, mem_len)

    max_rows = n * TOP_K + (n // TM_MID) * N_EXPERTS * (SUBLANES - 1)
    n_blocks = -(-max_rows // MOE_ROWS) + N_EXPERTS
    tcnt = tcnt[:, :, 0]
    tbase = tbase[:, :, 0]
    counts = tbase[-1] + tcnt[-1]
    padded = (counts + MOE_ROWS - 1) // MOE_ROWS * MOE_ROWS
    padded_end = jnp.cumsum(padded)
    start = (padded_end - padded).astype(I32)
    n_active = (padded_end[-1] // MOE_ROWS).astype(I32)
    blk = jnp.arange(n_blocks, dtype=I32)
    blk_exp = jnp.sum((blk * MOE_ROWS)[:, None] >= padded_end[None, :], axis=1)
    eid = jnp.arange(N_EXPERTS, dtype=I32)
    last_exp = jnp.max(jnp.where(padded > 0, eid, 0))
    blk_exp = jnp.where(blk < n_active, jnp.minimum(blk_exp, N_EXPERTS - 1), last_exp).astype(I32)
    later = jnp.logical_and(padded[None, :] > 0, eid[None, :] > eid[:, None])
    next_of = jnp.min(jnp.where(later, eid[None, :], N_EXPERTS), axis=1)
    next_of = jnp.where(next_of < N_EXPERTS, next_of, -1).astype(I32)
    of_blk = blk_exp[:, None] == eid[None, :]
    blk_next = jnp.sum(jnp.where(of_blk, next_of[None, :], 0), axis=1).astype(I32)
    blk_end = jnp.sum(jnp.where(of_blk, (start + counts)[None, :], 0), axis=1)
    blk_valid = jnp.clip(blk_end - blk * MOE_ROWS, 0, MOE_ROWS)
    blk_valid = jnp.where(blk < n_active, blk_valid, 0).astype(I32)

    run_n = tcnt.reshape(-1)
    run_local = (jnp.cumsum(tcnt, axis=1) - tcnt).reshape(-1)
    run_global = (start[None, :] + tbase).reshape(-1)
    xs = _dispatch(run_n, run_local, run_global, padded - counts, start + counts, n_active[None], u3, lp,
                   n_blocks * MOE_ROWS)

    b1p = b1.reshape(N_EXPERTS, 2 * D_FF // PAIR_COLS, LANES, 2).transpose(0, 1, 3, 2)
    ys = _experts(blk_exp, blk_next, blk_valid, n_active[None], xs, w1,
                  b1p.reshape(N_EXPERTS, 1, 2 * D_FF), w2, b2[:, None, :])
    return _combine(run_n, run_local, run_global, h2, gates, lp, g_final, ys)


def kernel(x, mem, g_mix, w_in, b_qkv, conv_w, conv_b, dt_bias, a_log, d_skip, g_ssm_out, attn_sinks, rel_bias, w_out, b_out, g_cross, g_mem, w_q_cross, w_kv_cross, w_o_cross, g_ffn, w_router, b_router, w1, b1, w2, b2, g_final):
    batch, seq, d = x.shape
    mem_len = mem.shape[1]
    depth = g_mix.shape[0]
    assert depth == 1 and d == D_MODEL and seq % TM_MID == 0
    h = x.reshape(batch * seq, d)
    mem2 = mem.reshape(batch * mem_len, d)
    bias = _bias_table(rel_bias)
    out = _layer(
        h, mem2, batch, seq, mem_len, g_mix[0], w_in[0], b_qkv[0], conv_w[0], conv_b[0], dt_bias[0],
        a_log[0], d_skip[0], g_ssm_out[0], attn_sinks[0], bias, w_out[0], b_out[0], g_cross[0],
        g_mem[0], w_q_cross[0], w_kv_cross[0], w_o_cross[0], g_ffn[0], w_router[0], b_router[0],
        w1[0], b1[0], w2[0], b2[0], g_final[None, :])
    return out.reshape(batch, seq, d)
```

```python
import math

import numpy as np
import jax
import jax.numpy as jnp
from jax import lax
from jax.experimental import pallas as pl
from jax.experimental.pallas import tpu as pltpu

F32 = jnp.float32
BF16 = jnp.bfloat16
I32 = jnp.int32

D_MODEL = 1024
RMS_EPS = 1e-5

SSM_HEADS = 8
SSM_HEAD_DIM = 64
SSM_WIDTH = SSM_HEADS * SSM_HEAD_DIM
SSM_STATE = 128
SSM_GROUPS = 2
HEADS_PER_GROUP = SSM_HEADS // SSM_GROUPS
GROUP_WIDTH = SSM_WIDTH // SSM_GROUPS
CONV_WIDTH = 4
CHUNK = 128
CONV_CH = SSM_WIDTH + 2 * SSM_GROUPS * SSM_STATE

ATTN_HEADS = 8
KV_HEADS = 2
Q_PER_KV = ATTN_HEADS // KV_HEADS
HEAD_DIM = 64
ATTN_WIDTH = ATTN_HEADS * HEAD_DIM
KV_WIDTH = KV_HEADS * HEAD_DIM
WINDOW = 128
BLOCK = WINDOW
QKV_COLS = ATTN_WIDTH + 2 * KV_WIDTH

REL_BUCKETS = 32
REL_MAX_DIST = 128

CROSS_HEADS = 4
CROSS_HEAD_DIM = 128
CROSS_WIDTH = CROSS_HEADS * CROSS_HEAD_DIM

N_EXPERTS = 32
TOP_K = 4
D_FF = D_MODEL
SWIGLU_ALPHA = 1.702
SWIGLU_LIMIT = 7.0

LANES = 128
SUBLANES = 8
DT_PAD = LANES
PROJ_COLS = SSM_WIDTH + CONV_CH + DT_PAD + QKV_COLS
VMEM_LIMIT = 56 * 1024 * 1024

TM_PROJ = 512
TM_MID = 512
MID_PARTS = 1
MOE_ROWS = 512
MOE_PART = 512
WEIGHT_DMA_ROWS = 256


def _nt(a, b):
    return lax.dot_general(a, b, (((1,), (1,)), ((), ())), preferred_element_type=F32)


def _tn(a, b):
    return lax.dot_general(a, b, (((0,), (0,)), ((), ())), preferred_element_type=F32)


def _dot(a, b):
    return jnp.dot(a, b, preferred_element_type=F32)


def _dot_exact(a, b):
    return jnp.dot(a, b, preferred_element_type=F32, precision=lax.Precision.HIGHEST)


def _rms(x, g):
    return x * lax.rsqrt(jnp.mean(x * x, axis=-1, keepdims=True) + RMS_EPS) * g


def _sigmoid(x):
    return 1.0 / (1.0 + jnp.exp(-x))


def _params(sem=None):
    return pltpu.CompilerParams(dimension_semantics=sem, vmem_limit_bytes=VMEM_LIMIT)


def _inproj_kernel(x_ref, g_ref, w_ref, bq_ref, z_ref, xbc_ref, dt_ref, qkv_ref):
    u = _rms(x_ref[...], g_ref[...])
    p = _dot(u.astype(BF16), w_ref[...])
    z_ref[...] = p[:, :SSM_WIDTH]
    xbc_ref[...] = p[:, SSM_WIDTH:SSM_WIDTH + CONV_CH]
    dt_ref[...] = p[:, SSM_WIDTH + CONV_CH:SSM_WIDTH + CONV_CH + DT_PAD]
    qkv_ref[...] = p[:, SSM_WIDTH + CONV_CH + DT_PAD:] + bq_ref[...]


def _inproj(x2, g_mix, w_cat, b_qkv):
    n = x2.shape[0]
    row = lambda i: (i, 0)
    fixed = lambda i: (0, 0)
    return pl.pallas_call(
        _inproj_kernel,
        grid=(n // TM_PROJ,),
        in_specs=[
            pl.BlockSpec((TM_PROJ, D_MODEL), row),
            pl.BlockSpec((1, D_MODEL), fixed),
            pl.BlockSpec((D_MODEL, PROJ_COLS), fixed),
            pl.BlockSpec((1, QKV_COLS), fixed),
        ],
        out_specs=[
            pl.BlockSpec((TM_PROJ, SSM_WIDTH), row),
            pl.BlockSpec((TM_PROJ, CONV_CH), row),
            pl.BlockSpec((TM_PROJ, DT_PAD), row),
            pl.BlockSpec((TM_PROJ, QKV_COLS), row),
        ],
        out_shape=[
            jax.ShapeDtypeStruct((n, SSM_WIDTH), F32),
            jax.ShapeDtypeStruct((n, CONV_CH), F32),
            jax.ShapeDtypeStruct((n, DT_PAD), F32),
            jax.ShapeDtypeStruct((n, QKV_COLS), F32),
        ],
        compiler_params=_params(("arbitrary",)),
    )(x2, g_mix, w_cat, b_qkv)


def _split3(x):
    hi = x.astype(BF16)
    rest = x - hi.astype(F32)
    mid = rest.astype(BF16)
    lo = (rest - mid.astype(F32)).astype(BF16)
    return jnp.concatenate([hi, mid, lo], axis=1)


def _ssd_kernel(xbc_ref, z_ref, dt_ref, cw_ref, cb_ref, dtb_ref, ah_ref, dsk_ref, gout_ref,
                e64_ref, ecat_ref, tri_ref, y_ref, conv_ref, state_ref):
    c = pl.program_id(0)

    @pl.when(c == 0)
    def _():
        conv_ref[:, :SUBLANES, :] = jnp.zeros((conv_ref.shape[0], SUBLANES, CONV_CH), F32)
        state_ref[...] = jnp.zeros_like(state_ref)

    for b in range(xbc_ref.shape[0]):
        _ssd_chunk(b, xbc_ref, z_ref, dt_ref, cw_ref, cb_ref, dtb_ref, ah_ref, dsk_ref, gout_ref,
                   e64_ref, ecat_ref, tri_ref, y_ref, conv_ref, state_ref)


def _ssd_chunk(b, xbc_ref, z_ref, dt_ref, cw_ref, cb_ref, dtb_ref, ah_ref, dsk_ref, gout_ref,
               e64_ref, ecat_ref, tri_ref, y_ref, conv_ref, state_ref):
    u = xbc_ref[b]
    conv_ref[b, SUBLANES:, :] = u
    acc = u * cw_ref[CONV_WIDTH - 1:CONV_WIDTH, :] + cb_ref[...]
    for j in range(1, CONV_WIDTH):
        shifted = conv_ref[b, SUBLANES - j:SUBLANES - j + CHUNK, :]
        acc = acc + shifted * cw_ref[CONV_WIDTH - 1 - j:CONV_WIDTH - j, :]
    conv_ref[b, :SUBLANES, :] = u[CHUNK - SUBLANES:, :]
    xbc = acc * _sigmoid(acc)

    xs = xbc[:, :SSM_WIDTH]
    bm = xbc[:, SSM_WIDTH:SSM_WIDTH + SSM_GROUPS * SSM_STATE]
    cm = xbc[:, SSM_WIDTH + SSM_GROUPS * SSM_STATE:]

    dtr = dt_ref[b] + dtb_ref[...]
    dt = jnp.maximum(dtr, 0.0) + jnp.log1p(jnp.exp(-jnp.abs(dtr)))
    a = dt * ah_ref[...]
    a3 = _split3(a)
    a_rows = jnp.concatenate([a3[:, :DT_PAD], a3[:, DT_PAD:2 * DT_PAD], a3[:, 2 * DT_PAD:]], axis=0)
    cs = _dot(tri_ref[...], a_rows)
    cs_row = cs.T
    dt_full = _dot(_split3(dt), e64_ref[...])
    cs_cat = _dot(_split3(cs), ecat_ref[...])
    cs_wide = cs_cat[:, :SSM_HEADS * CHUNK]
    cs_full = cs_cat[:, SSM_HEADS * CHUNK:]
    cs_last = cs_full[CHUNK - 1:CHUNK, :]

    xdt = xs * dt_full
    xdt_b = xdt.astype(BF16)
    xw_b = (xdt * jnp.exp(cs_last - cs_full)).astype(BF16)
    in_decay = jnp.exp(cs_full)
    chunk_decay = jnp.exp(cs_last)

    li = lax.broadcasted_iota(I32, (CHUNK, CHUNK), 0)
    si = lax.broadcasted_iota(I32, (CHUNK, CHUNK), 1)
    causal = li >= si

    ys = []
    for g in range(SSM_GROUPS):
        bg = bm[:, g * SSM_STATE:(g + 1) * SSM_STATE].astype(BF16)
        cg = cm[:, g * SSM_STATE:(g + 1) * SSM_STATE].astype(BF16)
        cb = _nt(cg, bg)
        yd = []
        for r in range(HEADS_PER_GROUP):
            h = g * HEADS_PER_GROUP + r
            diff = cs_wide[:, h * CHUNK:(h + 1) * CHUNK] - cs_row[h:h + 1, :]
            m = cb * jnp.exp(jnp.where(causal, diff, -jnp.inf))
            yd.append(_dot(m.astype(BF16), xdt_b[:, h * SSM_HEAD_DIM:(h + 1) * SSM_HEAD_DIM]))
        y_diag = jnp.concatenate(yd, axis=1)
        gs = slice(g * GROUP_WIDTH, (g + 1) * GROUP_WIDTH)
        st = state_ref[b, g]
        y_off = _dot(cg, st.astype(BF16)) * in_decay[:, gs]
        state_ref[b, g] = st * chunk_decay[:, gs] + _tn(bg, xw_b[:, gs])
        ys.append(y_diag + y_off)
    y = jnp.concatenate(ys, axis=1) + dsk_ref[...] * xs

    zz = z_ref[b]
    y = y * (zz * _sigmoid(zz))
    outs = []
    for g in range(SSM_GROUPS):
        yg = y[:, g * GROUP_WIDTH:(g + 1) * GROUP_WIDTH]
        outs.append(yg * lax.rsqrt(jnp.mean(yg * yg, axis=-1, keepdims=True) + RMS_EPS))
    y_ref[b] = jnp.concatenate(outs, axis=1) * gout_ref[...]


def _ssd(xbc, z, dtp, conv_w, conv_b, dt_bias_p, a_head_p, d_full, g_out):
    batch, seq, _ = xbc.shape
    chunk = lambda c: (0, c, 0)
    fixed = lambda c: (0, 0)
    head = np.arange(SSM_WIDTH) // SSM_HEAD_DIM
    e64 = (np.arange(DT_PAD)[:, None] == head[None, :]).astype(np.float32)
    head_w = np.arange(SSM_HEADS * CHUNK) // CHUNK
    e128 = (np.arange(DT_PAD)[:, None] == head_w[None, :]).astype(np.float32)
    e64_3 = jnp.asarray(np.tile(e64, (3, 1))).astype(BF16)
    ecat_3 = jnp.asarray(np.tile(np.concatenate([e128, e64], axis=1), (3, 1))).astype(BF16)
    tri_3 = jnp.asarray(np.tile(np.tril(np.ones((CHUNK, CHUNK), np.float32)), (1, 3))).astype(BF16)
    return pl.pallas_call(
        _ssd_kernel,
        name="ssd",
        grid=(seq // CHUNK,),
        in_specs=[
            pl.BlockSpec((batch, CHUNK, CONV_CH), chunk),
            pl.BlockSpec((batch, CHUNK, SSM_WIDTH), chunk),
            pl.BlockSpec((batch, CHUNK, DT_PAD), chunk),
            pl.BlockSpec((CONV_WIDTH, CONV_CH), fixed),
            pl.BlockSpec((1, CONV_CH), fixed),
            pl.BlockSpec((1, DT_PAD), fixed),
            pl.BlockSpec((1, DT_PAD), fixed),
            pl.BlockSpec((1, SSM_WIDTH), fixed),
            pl.BlockSpec((1, SSM_WIDTH), fixed),
            pl.BlockSpec((3 * DT_PAD, SSM_WIDTH), fixed),
            pl.BlockSpec((3 * DT_PAD, SSM_HEADS * CHUNK + SSM_WIDTH), fixed),
            pl.BlockSpec((CHUNK, 3 * CHUNK), fixed),
        ],
        out_specs=pl.BlockSpec((batch, CHUNK, SSM_WIDTH), chunk),
        out_shape=jax.ShapeDtypeStruct((batch, seq, SSM_WIDTH), F32),
        scratch_shapes=[
            pltpu.VMEM((batch, SUBLANES + CHUNK, CONV_CH), F32),
            pltpu.VMEM((batch, SSM_GROUPS, SSM_STATE, GROUP_WIDTH), F32),
        ],
        compiler_params=_params(("arbitrary",)),
    )(xbc, z, dtp, conv_w, conv_b, dt_bias_p, a_head_p, d_full, g_out, e64_3, ecat_3, tri_3)


def _t5_bucket_table():
    q_pos = np.arange(BLOCK)[:, None]
    k_pos = np.arange(2 * BLOCK)[None, :] - BLOCK
    dist = q_pos - k_pos
    d = np.maximum(dist, 0)
    max_exact = REL_BUCKETS // 2
    ratio = np.maximum(d, max_exact).astype(np.float32) / np.float32(max_exact)
    large = max_exact + (np.log(ratio) / np.float32(math.log(REL_MAX_DIST / max_exact))
                         * np.float32(REL_BUCKETS - max_exact)).astype(np.int32)
    large = np.minimum(large, REL_BUCKETS - 1)
    bucket = np.where(d < max_exact, d, large)
    in_window = (dist >= 0) & (dist < WINDOW)
    return np.where(in_window, bucket, -1).astype(np.int32)


def _bias_kernel(rb_ref, bucket_ref, out_ref):
    bucket = bucket_ref[...]
    for h in range(ATTN_HEADS):
        acc = jnp.zeros(bucket.shape, F32)
        for b in range(REL_BUCKETS):
            acc = jnp.where(bucket == b, rb_ref[b, h], acc)
        out_ref[h] = jnp.where(bucket >= 0, acc, -jnp.inf)


def _bias_table(rel_bias):
    bucket = jnp.asarray(_t5_bucket_table())
    return pl.pallas_call(
        _bias_kernel,
        in_specs=[
            pl.BlockSpec(memory_space=pltpu.SMEM),
            pl.BlockSpec(memory_space=pltpu.VMEM),
        ],
        out_specs=pl.BlockSpec(memory_space=pltpu.VMEM),
        out_shape=jax.ShapeDtypeStruct((ATTN_HEADS, BLOCK, 2 * BLOCK), F32),
    )(rel_bias, bucket)


def _swa_kernel(sink_ref, q_ref, k_ref, v_ref, kp_ref, vp_ref, bias_ref, o_ref):
    i = pl.program_id(0)
    col = lax.broadcasted_iota(I32, (BLOCK, 2 * BLOCK), 1)
    key_ok = jnp.logical_or(col >= BLOCK, i > 0)
    for b in range(q_ref.shape[0]):
        q = (q_ref[b] * (HEAD_DIM ** -0.5)).astype(BF16)
        outs = []
        for g in range(KV_HEADS):
            gs = slice(g * HEAD_DIM, (g + 1) * HEAD_DIM)
            k2 = jnp.concatenate([kp_ref[b, :, gs], k_ref[b, :, gs]], axis=0).astype(BF16)
            v2 = jnp.concatenate([vp_ref[b, :, gs], v_ref[b, :, gs]], axis=0).astype(BF16)
            for r in range(Q_PER_KV):
                h = g * Q_PER_KV + r
                s = _nt(q[:, h * HEAD_DIM:(h + 1) * HEAD_DIM], k2) + bias_ref[h]
                s = jnp.where(key_ok, s, -jnp.inf)
                sink = sink_ref[h]
                m = jnp.maximum(jnp.max(s, axis=-1, keepdims=True), sink)
                p = jnp.exp(s - m)
                denom = jnp.sum(p, axis=-1, keepdims=True) + jnp.exp(sink - m)
                outs.append(_dot(p.astype(BF16), v2) / denom)
        o_ref[b] = jnp.concatenate(outs, axis=1)


def _swa(qkv, sinks, bias):
    batch, seq, _ = qkv.shape
    kcol = ATTN_WIDTH // KV_WIDTH
    prev = lambda i: jnp.maximum(i - 1, 0)
    return pl.pallas_call(
        _swa_kernel,
        name="swa",
        grid_spec=pltpu.PrefetchScalarGridSpec(
            num_scalar_prefetch=1,
            grid=(seq // BLOCK,),
            in_specs=[
                pl.BlockSpec((batch, BLOCK, ATTN_WIDTH), lambda i, s: (0, i, 0)),
                pl.BlockSpec((batch, BLOCK, KV_WIDTH), lambda i, s: (0, i, kcol)),
                pl.BlockSpec((batch, BLOCK, KV_WIDTH), lambda i, s: (0, i, kcol + 1)),
                pl.BlockSpec((batch, BLOCK, KV_WIDTH), lambda i, s: (0, prev(i), kcol)),
                pl.BlockSpec((batch, BLOCK, KV_WIDTH), lambda i, s: (0, prev(i), kcol + 1)),
                pl.BlockSpec((ATTN_HEADS, BLOCK, 2 * BLOCK), lambda i, s: (0, 0, 0)),
            ],
            out_specs=pl.BlockSpec((batch, BLOCK, ATTN_WIDTH), lambda i, s: (0, i, 0)),
        ),
        out_shape=jax.ShapeDtypeStruct((batch, seq, ATTN_WIDTH), F32),
        compiler_params=_params(("arbitrary",)),
    )(sinks, qkv, qkv, qkv, qkv, qkv, bias)


def _memkv_kernel(m_ref, g_ref, w_ref, k_ref, v_ref):
    u = _rms(m_ref[...], g_ref[...])
    kv = _dot(u.astype(BF16), w_ref[...])
    k_ref[...] = kv[:, :CROSS_WIDTH].astype(BF16)
    v_ref[...] = kv[:, CROSS_WIDTH:].astype(BF16)


def _memkv(mem2, g_mem, w_kv):
    n = mem2.shape[0]
    return pl.pallas_call(
        _memkv_kernel,
        out_shape=[jax.ShapeDtypeStruct((n, CROSS_WIDTH), BF16)] * 2,
        compiler_params=_params(),
    )(mem2, g_mem, w_kv)


def _mid_kernel(ys_ref, ya_ref, x_ref, wout_ref, bout_ref, gc_ref, wq_ref, k_ref, v_ref, wo_ref,
                gf_ref, wr_ref, br_ref, upper_ref, lower_ref,
                h2_ref, u3_ref, gate_ref, lp_ref, tcnt_ref, tbase_ref, run_ref):
    i = pl.program_id(0)

    @pl.when(i == 0)
    def _():
        run_ref[...] = jnp.zeros_like(run_ref)

    u3_parts = []
    for part in range(MID_PARTS):
        rows = slice(part * (TM_MID // MID_PARTS), (part + 1) * (TM_MID // MID_PARTS))
        ycat = jnp.concatenate([ys_ref[rows, :], ya_ref[rows, :]], axis=1).astype(BF16)
        h1 = x_ref[rows, :] + _dot(ycat, wout_ref[...]) + bout_ref[...]

        u2 = _rms(h1, gc_ref[...])
        q = _dot(u2.astype(BF16), wq_ref[...]).astype(BF16)
        outs = []
        for h in range(CROSS_HEADS):
            hs = slice(h * CROSS_HEAD_DIM, (h + 1) * CROSS_HEAD_DIM)
            s = _nt(q[:, hs], k_ref[:, hs]) * (CROSS_HEAD_DIM ** -0.5)
            m = jnp.max(s, axis=-1, keepdims=True)
            p = jnp.exp(s - m)
            outs.append(_dot(p.astype(BF16), v_ref[:, hs]) / jnp.sum(p, axis=-1, keepdims=True))
        o = jnp.concatenate(outs, axis=1).astype(BF16)
        h2 = h1 + _dot(o, wo_ref[...])
        h2_ref[rows, :] = h2
        u3_parts.append(_rms(h2, gf_ref[...]).astype(BF16))
    u3 = jnp.concatenate(u3_parts, axis=0)
    u3_ref[...] = u3

    logits = _nt(wr_ref[...], u3) + br_ref[...]
    eid = lax.broadcasted_iota(I32, logits.shape, 0)
    vals, idxs, hots = [], [], []
    for _ in range(TOP_K):
        m = jnp.max(logits, axis=0, keepdims=True)
        ix = jnp.min(jnp.where(logits == m, eid, N_EXPERTS), axis=0, keepdims=True)
        hot = eid == ix
        logits = jnp.where(hot, -jnp.inf, logits)
        vals.append(m)
        idxs.append(ix)
        hots.append(hot)
    ex = [jnp.exp(v - vals[0]) for v in vals]
    tot = ex[0] + ex[1] + ex[2] + ex[3]
    gate_ref[...] = jnp.concatenate([e / tot for e in ex], axis=0)

    hot_f = [jnp.where(hot, 1.0, 0.0) for hot in hots]
    cnt_k = [jnp.sum(hf, axis=1, keepdims=True) for hf in hot_f]
    tile_cnt = cnt_k[0] + cnt_k[1] + cnt_k[2] + cnt_k[3]
    tile_cnt = jnp.floor((tile_cnt + (SUBLANES - 1)) * (1.0 / SUBLANES)) * SUBLANES
    offs = _dot_exact(lower_ref[...], jnp.broadcast_to(tile_cnt, (N_EXPERTS, LANES)))[:, :1]
    local = []
    for kk in range(TOP_K):
        before = _dot(hot_f[kk].astype(BF16), upper_ref[...])
        local.append(jnp.sum(jnp.where(hots[kk], before + offs, 0.0), axis=0, keepdims=True))
        offs = offs + cnt_k[kk]
    lp_ref[...] = jnp.concatenate(local, axis=0).astype(I32)
    tcnt_ref[0] = jnp.broadcast_to(tile_cnt, (N_EXPERTS, LANES)).astype(I32)
    tbase_ref[0] = jnp.broadcast_to(run_ref[...], (N_EXPERTS, LANES)).astype(I32)
    run_ref[...] = run_ref[...] + tile_cnt


def _mid(y_ssm, y_attn, x2, w_out, b_out, g_cross, w_q, k_mem, v_mem, w_o, g_ffn, w_r_t, b_r, seq, mem_len):
    n = x2.shape[0]
    tiles_per_batch = seq // TM_MID
    row = lambda i: (i, 0)
    col = lambda i: (0, i)
    fixed = lambda i: (0, 0)
    memb = lambda i: (i // tiles_per_batch, 0)
    tile3 = lambda i: (i, 0, 0)
    n_tiles = n // TM_MID
    upper = jnp.asarray(np.triu(np.ones((TM_MID, TM_MID), np.float32), 1)).astype(BF16)
    lower = jnp.asarray(np.tril(np.ones((N_EXPERTS, N_EXPERTS), np.float32), -1))
    return pl.pallas_call(
        _mid_kernel,
        name="mid",
        grid=(n_tiles,),
        in_specs=[
            pl.BlockSpec((TM_MID, SSM_WIDTH), row),
            pl.BlockSpec((TM_MID, ATTN_WIDTH), row),
            pl.BlockSpec((TM_MID, D_MODEL), row),
            pl.BlockSpec((D_MODEL, D_MODEL), fixed),
            pl.BlockSpec((1, D_MODEL), fixed),
            pl.BlockSpec((1, D_MODEL), fixed),
            pl.BlockSpec((D_MODEL, CROSS_WIDTH), fixed),
            pl.BlockSpec((mem_len, CROSS_WIDTH), memb),
            pl.BlockSpec((mem_len, CROSS_WIDTH), memb),
            pl.BlockSpec((CROSS_WIDTH, D_MODEL), fixed),
            pl.BlockSpec((1, D_MODEL), fixed),
            pl.BlockSpec((N_EXPERTS, D_MODEL), fixed),
            pl.BlockSpec((N_EXPERTS, 1), fixed),
            pl.BlockSpec((TM_MID, TM_MID), fixed),
            pl.BlockSpec((N_EXPERTS, N_EXPERTS), fixed),
        ],
        out_specs=[
            pl.BlockSpec((TM_MID, D_MODEL), row),
            pl.BlockSpec((TM_MID, D_MODEL), row),
            pl.BlockSpec((TOP_K, TM_MID), col),
            pl.BlockSpec((TOP_K, TM_MID), col),
            pl.BlockSpec((1, N_EXPERTS, LANES), tile3),
            pl.BlockSpec((1, N_EXPERTS, LANES), tile3),
        ],
        out_shape=[
            jax.ShapeDtypeStruct((n, D_MODEL), F32),
            jax.ShapeDtypeStruct((n, D_MODEL), BF16),
            jax.ShapeDtypeStruct((TOP_K, n), F32),
            jax.ShapeDtypeStruct((TOP_K, n), I32),
            jax.ShapeDtypeStruct((n_tiles, N_EXPERTS, LANES), I32),
            jax.ShapeDtypeStruct((n_tiles, N_EXPERTS, LANES), I32),
        ],
        scratch_shapes=[pltpu.VMEM((N_EXPERTS, 1), F32)],
        compiler_params=_params(("arbitrary",)),
    )(y_ssm, y_attn, x2, w_out, b_out, g_cross, w_q, k_mem, v_mem, w_o, g_ffn, w_r_t, b_r, upper, lower)


ROW_CHUNK = 256
LOCAL_ROWS = -(-(TM_MID * TOP_K + N_EXPERTS * (SUBLANES - 1)) // ROW_CHUNK) * ROW_CHUNK
RUN_BITS = tuple(1 << s for s in range(TM_MID.bit_length() - 1, SUBLANES.bit_length() - 2, -1))
PAD_BITS = tuple(1 << s for s in range((MOE_ROWS - 1).bit_length() - 1, SUBLANES.bit_length() - 2, -1))


U32 = jnp.uint32
PACKED = D_MODEL // 2


def _pack_pairs(v):
    bits = lax.bitcast_convert_type(v, U32)
    return bits[:, PACKED:] | (bits[:, :PACKED] >> 16)


def _unpack_pairs(w):
    lo = lax.bitcast_convert_type(w << 16, F32)
    hi = lax.bitcast_convert_type(w & jnp.uint32(0xFFFF0000), F32)
    return jnp.concatenate([lo, hi], axis=1).astype(BF16)


def _run_copies(n, src_ref, src, dst_ref, dst, sem, bits, wait, advance_src=True):
    for bit in bits:
        step = n & bit

        @pl.when(step != 0)
        def _():
            cp = pltpu.make_async_copy(src_ref.at[pl.ds(pl.multiple_of(src, SUBLANES), bit)],
                                       dst_ref.at[pl.ds(pl.multiple_of(dst, SUBLANES), bit)], sem)
            if wait:
                cp.wait()
            else:
                cp.start()

        if advance_src:
            src = src + step
        dst = dst + step


def _dispatch_kernel(tcnt_ref, lo_ref, gs_ref, padn_ref, pads_ref, na_ref, u_ref, lp_ref, xs_ref,
                     xloc_ref, zero_ref, sem, zsem):
    i = pl.program_id(0)

    def zero_fill(wait):
        def body(e, carry):
            _run_copies(padn_ref[e], zero_ref, 0, xs_ref, pads_ref[e], zsem, PAD_BITS, wait, advance_src=False)
            return carry
        lax.fori_loop(0, N_EXPERTS, body, 0)

        def tail(b, carry):
            cp = pltpu.make_async_copy(zero_ref, xs_ref.at[pl.ds(pl.multiple_of(b * MOE_ROWS, MOE_ROWS), MOE_ROWS)], zsem)
            if wait:
                cp.wait()
            else:
                cp.start()
            return carry
        lax.fori_loop(na_ref[0], xs_ref.shape[0] // MOE_ROWS, tail, 0)

    @pl.when(i == 0)
    def _():
        zero_ref[...] = jnp.zeros_like(zero_ref)
        zero_fill(False)

    slot = i % 2
    u = u_ref[...]
    lp = lp_ref[...]
    for c in range(LOCAL_ROWS // ROW_CHUNK):
        r = lax.broadcasted_iota(I32, (ROW_CHUNK, TM_MID), 0) + c * ROW_CHUNK
        p = jnp.zeros((ROW_CHUNK, TM_MID), F32)
        for kk in range(TOP_K):
            p = jnp.where(lp[kk:kk + 1, :] == r, 1.0, p)
        xloc_ref[slot, c * ROW_CHUNK:(c + 1) * ROW_CHUNK, :] = _pack_pairs(_dot(p.astype(BF16), u))

    def runs(tile, wait):
        s = tile % 2

        def body(e, carry):
            j = tile * N_EXPERTS + e
            _run_copies(tcnt_ref[j], xloc_ref.at[s], lo_ref[j], xs_ref, gs_ref[j], sem.at[s], RUN_BITS, wait)
            return carry
        lax.fori_loop(0, N_EXPERTS, body, 0)

    runs(i, False)

    @pl.when(i > 0)
    def _():
        runs(i - 1, True)

    @pl.when(i == pl.num_programs(0) - 1)
    def _():
        runs(i, True)

    @pl.when(i == 0)
    def _():
        zero_fill(True)


def _dispatch(tcnt, lo, gstart, padn, pads, n_active, u3, lp, n_rows):
    n = u3.shape[0]
    return pl.pallas_call(
        _dispatch_kernel,
        name="dispatch",
        grid_spec=pltpu.PrefetchScalarGridSpec(
            num_scalar_prefetch=6,
            grid=(n // TM_MID,),
            in_specs=[
                pl.BlockSpec((TM_MID, D_MODEL), lambda i, *_: (i, 0)),
                pl.BlockSpec((TOP_K, TM_MID), lambda i, *_: (0, i)),
            ],
            out_specs=pl.BlockSpec(memory_space=pl.ANY),
            scratch_shapes=[
                pltpu.VMEM((2, LOCAL_ROWS, PACKED), U32),
                pltpu.VMEM((MOE_ROWS, PACKED), U32),
                pltpu.SemaphoreType.DMA((2,)),
                pltpu.SemaphoreType.DMA,
            ],
        ),
        out_shape=jax.ShapeDtypeStruct((n_rows, PACKED), U32),
        compiler_params=_params(("arbitrary",)),
    )(tcnt, lo, gstart, padn, pads, n_active, u3, lp)


PAIR_COLS = 2 * LANES


def _expert_kernel(be_ref, nx_ref, nv_ref, na_ref, x_ref, b1_ref, b2_ref, perm_ref, w1_hbm, w2_hbm, y_ref,
                   w1s_ref, w2s_ref, w1p_ref, w2b_ref, sem):
    b = pl.program_id(0)
    valid_rows = nv_ref[b]
    active = valid_rows > 0
    expert = be_ref[b]
    new_expert = jnp.logical_or(b == 0, expert != be_ref[jnp.maximum(b - 1, 0)])

    def fetch(e, wait):
        for src, dst, s in ((w1_hbm, w1s_ref, 0), (w2_hbm, w2s_ref, 1)):
            for r in range(0, src.shape[1], WEIGHT_DMA_ROWS):
                rs = pl.ds(r, WEIGHT_DMA_ROWS)
                cp = pltpu.make_async_copy(src.at[e, rs], dst.at[rs], sem.at[s])
                if wait:
                    cp.wait()
                else:
                    cp.start(priority=1)

    @pl.when(b == 0)
    def _():
        fetch(expert, False)

    @pl.when(jnp.logical_and(active, new_expert))
    def _():
        fetch(expert, True)
        for c in range(2 * D_FF // PAIR_COLS):
            cs = slice(c * PAIR_COLS, (c + 1) * PAIR_COLS)
            w1p_ref[:, cs] = _dot(w1s_ref[:, cs].astype(BF16), perm_ref[...]).astype(BF16)
        w2b_ref[...] = w2s_ref[...].astype(BF16)

        @pl.when(nx_ref[b] >= 0)
        def _():
            fetch(nx_ref[b], False)

    for part in range(MOE_ROWS // MOE_PART):
        rows = slice(part * MOE_PART, (part + 1) * MOE_PART)
        live = valid_rows > part * MOE_PART

        @pl.when(live)
        def _():
            x = _unpack_pairs(x_ref[rows, :])
            hdn = _dot(x, w1p_ref[...]) + b1_ref[0]
            acts = []
            for c in range(2 * D_FF // PAIR_COLS):
                hg = jnp.minimum(hdn[:, c * PAIR_COLS:c * PAIR_COLS + LANES], SWIGLU_LIMIT)
                hl = jnp.clip(hdn[:, c * PAIR_COLS + LANES:(c + 1) * PAIR_COLS], -SWIGLU_LIMIT, SWIGLU_LIMIT)
                acts.append((hg * _sigmoid(SWIGLU_ALPHA * hg) * (hl + 1.0)).astype(BF16))
            y = _dot(jnp.concatenate(acts, axis=1), w2b_ref[...]) + b2_ref[0]
            y_ref[rows, :] = _pack_pairs(y.astype(BF16).astype(F32))

        @pl.when(jnp.logical_not(live))
        def _():
            y_ref[rows, :] = jnp.zeros((MOE_PART, PACKED), U32)


def _experts(block_exp, next_exp, valid_rows, n_active, xs, w1, b1p, w2, b2):
    n_rows = xs.shape[0]
    n_blocks = n_rows // MOE_ROWS
    xmap = lambda b, be, nx, nv, na: (jnp.maximum(jnp.minimum(b, na[0] - 1), 0), 0)
    emap = lambda b, be, nx, nv, na: (be[b], 0, 0)
    perm = np.zeros((PAIR_COLS, PAIR_COLS), np.float32)
    perm[2 * np.arange(LANES), np.arange(LANES)] = 1.0
    perm[2 * np.arange(LANES) + 1, LANES + np.arange(LANES)] = 1.0
    return pl.pallas_call(
        _expert_kernel,
        name="experts",
        grid_spec=pltpu.PrefetchScalarGridSpec(
            num_scalar_prefetch=4,
            grid=(n_blocks,),
            in_specs=[
                pl.BlockSpec((MOE_ROWS, PACKED), xmap),
                pl.BlockSpec((1, 1, 2 * D_FF), emap),
                pl.BlockSpec((1, 1, D_MODEL), emap),
                pl.BlockSpec((PAIR_COLS, PAIR_COLS), lambda b, *_: (0, 0)),
                pl.BlockSpec(memory_space=pl.ANY),
                pl.BlockSpec(memory_space=pl.ANY),
            ],
            out_specs=pl.BlockSpec((MOE_ROWS, PACKED), lambda b, *_: (b, 0)),
            scratch_shapes=[
                pltpu.VMEM((D_MODEL, 2 * D_FF), F32),
                pltpu.VMEM((D_FF, D_MODEL), F32),
                pltpu.VMEM((D_MODEL, 2 * D_FF), BF16),
                pltpu.VMEM((D_FF, D_MODEL), BF16),
                pltpu.SemaphoreType.DMA((2,)),
            ],
        ),
        out_shape=jax.ShapeDtypeStruct((n_rows, PACKED), U32),
        compiler_params=_params(("arbitrary",)),
    )(block_exp, next_exp, valid_rows, n_active, xs, b1p, b2, jnp.asarray(perm).astype(BF16), w1, w2)


def _combine_kernel(tcnt_ref, lo_ref, gs_ref, h2_ref, gate_ref, lp_ref, g_ref, ys_ref, o_ref, yloc_ref, sem):
    i = pl.program_id(0)

    slot = i % 2

    def runs(tile, wait):
        s = tile % 2

        def body(e, carry):
            j = tile * N_EXPERTS + e
            _run_copies(tcnt_ref[j], ys_ref, gs_ref[j], yloc_ref.at[s], lo_ref[j], sem.at[s], RUN_BITS, wait)
            return carry
        lax.fori_loop(0, N_EXPERTS, body, 0)

    @pl.when(i == 0)
    def _():
        yloc_ref[...] = jnp.zeros_like(yloc_ref)
        runs(i, False)

    @pl.when(i + 1 < pl.num_programs(0))
    def _():
        runs(i + 1, False)

    gates = gate_ref[...].T
    lp = lp_ref[...].astype(F32).T
    runs(i, True)

    h = h2_ref[...]
    for c in range(LOCAL_ROWS // ROW_CHUNK):
        lane = (lax.broadcasted_iota(I32, (TM_MID, ROW_CHUNK), 1) + c * ROW_CHUNK).astype(F32)
        a = jnp.zeros((TM_MID, ROW_CHUNK), F32)
        for kk in range(TOP_K):
            a = jnp.where(lp[:, kk:kk + 1] == lane, gates[:, kk:kk + 1], a)
        h = h + _dot(a.astype(BF16), _unpack_pairs(yloc_ref[slot, c * ROW_CHUNK:(c + 1) * ROW_CHUNK, :]))
    o_ref[...] = _rms(h, g_ref[...])


def _combine(tcnt, lo, gstart, h2, gates, lp, g_final, ys):
    n = h2.shape[0]
    return pl.pallas_call(
        _combine_kernel,
        name="combine",
        grid_spec=pltpu.PrefetchScalarGridSpec(
            num_scalar_prefetch=3,
            grid=(n // TM_MID,),
            in_specs=[
                pl.BlockSpec((TM_MID, D_MODEL), lambda i, *_: (i, 0)),
                pl.BlockSpec((TOP_K, TM_MID), lambda i, *_: (0, i)),
                pl.BlockSpec((TOP_K, TM_MID), lambda i, *_: (0, i)),
                pl.BlockSpec((1, D_MODEL), lambda i, *_: (0, 0)),
                pl.BlockSpec(memory_space=pl.ANY),
            ],
            out_specs=pl.BlockSpec((TM_MID, D_MODEL), lambda i, *_: (i, 0)),
            scratch_shapes=[
                pltpu.VMEM((2, LOCAL_ROWS, PACKED), U32),
                pltpu.SemaphoreType.DMA((2,)),
            ],
        ),
        out_shape=jax.ShapeDtypeStruct((n, D_MODEL), F32),
        compiler_params=_params(("arbitrary",)),
    )(tcnt, lo, gstart, h2, gates, lp, g_final, ys)


def _layer(h, mem2, batch, seq, mem_len, g_mix, w_in, b_qkv, conv_w, conv_b, dt_bias, a_log, d_skip,
           g_ssm_out, attn_sinks, bias, w_out, b_out, g_cross, g_mem, w_q_cross, w_kv_cross, w_o_cross,
           g_ffn, w_router, b_router, w1, b1, w2, b2, g_final):
    n = h.shape[0]
    c1, c2 = SSM_WIDTH + CONV_CH, SSM_WIDTH + CONV_CH + SSM_HEADS
    w_cat = jnp.concatenate(
        [w_in[:, :c1], jnp.pad(w_in[:, c1:c2], ((0, 0), (0, DT_PAD - SSM_HEADS))), w_in[:, c2:]],
        axis=1).astype(BF16)
    z, xbc, dtp, qkv = _inproj(h, g_mix[None, :], w_cat, b_qkv[None, :])

    pad_h = (0, DT_PAD - SSM_HEADS)
    dt_bias_p = jnp.pad(dt_bias, pad_h)[None, :]
    a_head_p = jnp.pad(-jnp.exp(a_log), pad_h)[None, :]
    d_full = jnp.repeat(d_skip, SSM_HEAD_DIM)[None, :]
    per_seq = lambda t: t.reshape(batch, seq, t.shape[-1])
    y_ssm = _ssd(per_seq(xbc), per_seq(z), per_seq(dtp), conv_w, conv_b[None, :], dt_bias_p, a_head_p,
                 d_full, g_ssm_out[None, :]).reshape(n, SSM_WIDTH)
    y_attn = _swa(per_seq(qkv), attn_sinks, bias).reshape(n, ATTN_WIDTH)

    k_mem, v_mem = _memkv(mem2, g_mem[None, :], w_kv_cross.astype(BF16))
    h2, u3, gates, lp, tcnt, tbase = _mid(
        y_ssm, y_attn, h, w_out.astype(BF16), b_out[None, :], g_cross[None, :],
        w_q_cross.astype(BF16), k_mem, v_mem, w_o_cross.astype(BF16), g_ffn[None, :],
        w_router.T.astype(BF16), b_router[:, None], seq, mem_len)

    max_rows = n * TOP_K + (n // TM_MID) * N_EXPERTS * (SUBLANES - 1)
    n_blocks = -(-max_rows // MOE_ROWS) + N_EXPERTS
    tcnt = tcnt[:, :, 0]
    tbase = tbase[:, :, 0]
    counts = tbase[-1] + tcnt[-1]
    padded = (counts + MOE_ROWS - 1) // MOE_ROWS * MOE_ROWS
    padded_end = jnp.cumsum(padded)
    start = (padded_end - padded).astype(I32)
    n_active = (padded_end[-1] // MOE_ROWS).astype(I32)
    blk = jnp.arange(n_blocks, dtype=I32)
    blk_exp = jnp.sum((blk * MOE_ROWS)[:, None] >= padded_end[None, :], axis=1)
    eid = jnp.arange(N_EXPERTS, dtype=I32)
    last_exp = jnp.max(jnp.where(padded > 0, eid, 0))
    blk_exp = jnp.where(blk < n_active, jnp.minimum(blk_exp, N_EXPERTS - 1), last_exp).astype(I32)
    later = jnp.logical_and(padded[None, :] > 0, eid[None, :] > eid[:, None])
    next_of = jnp.min(jnp.where(later, eid[None, :], N_EXPERTS), axis=1)
    next_of = jnp.where(next_of < N_EXPERTS, next_of, -1).astype(I32)
    of_blk = blk_exp[:, None] == eid[None, :]
    blk_next = jnp.sum(jnp.where(of_blk, next_of[None, :], 0), axis=1).astype(I32)
    blk_end = jnp.sum(jnp.where(of_blk, (start + counts)[None, :], 0), axis=1)
    blk_valid = jnp.clip(blk_end - blk * MOE_ROWS, 0, MOE_ROWS)
    blk_valid = jnp.where(blk < n_active, blk_valid, 0).astype(I32)

    run_n = tcnt.reshape(-1)
    run_local = (jnp.cumsum(tcnt, axis=1) - tcnt).reshape(-1)
    run_global = (start[None, :] + tbase).reshape(-1)
    xs = _dispatch(run_n, run_local, run_global, padded - counts, start + counts, n_active[None], u3, lp,
                   n_blocks * MOE_ROWS)

    b1p = b1.reshape(N_EXPERTS, 2 * D_FF // PAIR_COLS, LANES, 2).transpose(0, 1, 3, 2)
    ys = _experts(blk_exp, blk_next, blk_valid, n_active[None], xs, w1,
                  b1p.reshape(N_EXPERTS, 1, 2 * D_FF), w2, b2[:, None, :])
    return _combine(run_n, run_local, run_global, h2, gates, lp, g_final, ys)


def kernel(x, mem, g_mix, w_in, b_qkv, conv_w, conv_b, dt_bias, a_log, d_skip, g_ssm_out, attn_sinks, rel_bias, w_out, b_out, g_cross, g_mem, w_q_cross, w_kv_cross, w_o_cross, g_ffn, w_router, b_router, w1, b1, w2, b2, g_final):
    batch, seq, d = x.shape
    mem_len = mem.shape[1]
    depth = g_mix.shape[0]
    assert depth == 1 and d == D_MODEL and seq % TM_MID == 0
    h = x.reshape(batch * seq, d)
    mem2 = mem.reshape(batch * mem_len, d)
    bias = _bias_table(rel_bias)
    out = _layer(
        h, mem2, batch, seq, mem_len, g_mix[0], w_in[0], b_qkv[0], conv_w[0], conv_b[0], dt_bias[0],
        a_log[0], d_skip[0], g_ssm_out[0], attn_sinks[0], bias, w_out[0], b_out[0], g_cross[0],
        g_mem[0], w_q_cross[0], w_kv_cross[0], w_o_cross[0], g_ffn[0], w_router[0], b_router[0],
        w1[0], b1[0], w2[0], b2[0], g_final[None, :])
    return out.reshape(batch, seq, d)
```

```python
import math

import numpy as np
import jax
import jax.numpy as jnp
from jax import lax
from jax.experimental import pallas as pl
from jax.experimental.pallas import tpu as pltpu

F32 = jnp.float32
BF16 = jnp.bfloat16
I32 = jnp.int32

D_MODEL = 1024
RMS_EPS = 1e-5

SSM_HEADS = 8
SSM_HEAD_DIM = 64
SSM_WIDTH = SSM_HEADS * SSM_HEAD_DIM
SSM_STATE = 128
SSM_GROUPS = 2
HEADS_PER_GROUP = SSM_HEADS // SSM_GROUPS
GROUP_WIDTH = SSM_WIDTH // SSM_GROUPS
CONV_WIDTH = 4
CHUNK = 128
CONV_CH = SSM_WIDTH + 2 * SSM_GROUPS * SSM_STATE

ATTN_HEADS = 8
KV_HEADS = 2
Q_PER_KV = ATTN_HEADS // KV_HEADS
HEAD_DIM = 64
ATTN_WIDTH = ATTN_HEADS * HEAD_DIM
KV_WIDTH = KV_HEADS * HEAD_DIM
WINDOW = 128
BLOCK = WINDOW
QKV_COLS = ATTN_WIDTH + 2 * KV_WIDTH

REL_BUCKETS = 32
REL_MAX_DIST = 128

CROSS_HEADS = 4
CROSS_HEAD_DIM = 128
CROSS_WIDTH = CROSS_HEADS * CROSS_HEAD_DIM

N_EXPERTS = 32
TOP_K = 4
D_FF = D_MODEL
SWIGLU_ALPHA = 1.702
SWIGLU_LIMIT = 7.0

LANES = 128
SUBLANES = 8
DT_PAD = LANES
PROJ_COLS = SSM_WIDTH + CONV_CH + DT_PAD + QKV_COLS
VMEM_LIMIT = 56 * 1024 * 1024

TM_PROJ = 512
TM_MID = 512
MID_PARTS = 1
MOE_ROWS = 512
MOE_PART = 512
WEIGHT_DMA_ROWS = 256


def _nt(a, b):
    return lax.dot_general(a, b, (((1,), (1,)), ((), ())), preferred_element_type=F32)


def _tn(a, b):
    return lax.dot_general(a, b, (((0,), (0,)), ((), ())), preferred_element_type=F32)


def _dot(a, b):
    return jnp.dot(a, b, preferred_element_type=F32)


def _dot_exact(a, b):
    return jnp.dot(a, b, preferred_element_type=F32, precision=lax.Precision.HIGHEST)


def _rms(x, g):
    return x * lax.rsqrt(jnp.mean(x * x, axis=-1, keepdims=True) + RMS_EPS) * g


def _sigmoid(x):
    return 1.0 / (1.0 + jnp.exp(-x))


def _params(sem=None):
    return pltpu.CompilerParams(dimension_semantics=sem, vmem_limit_bytes=VMEM_LIMIT)


def _inproj_kernel(x_ref, g_ref, w_ref, bq_ref, z_ref, xbc_ref, dt_ref, qkv_ref):
    u = _rms(x_ref[...], g_ref[...])
    p = _dot(u.astype(BF16), w_ref[...])
    z_ref[...] = p[:, :SSM_WIDTH]
    xbc_ref[...] = p[:, SSM_WIDTH:SSM_WIDTH + CONV_CH]
    dt_ref[...] = p[:, SSM_WIDTH + CONV_CH:SSM_WIDTH + CONV_CH + DT_PAD]
    qkv_ref[...] = p[:, SSM_WIDTH + CONV_CH + DT_PAD:] + bq_ref[...]


def _inproj(x2, g_mix, w_cat, b_qkv):
    n = x2.shape[0]
    row = lambda i: (i, 0)
    fixed = lambda i: (0, 0)
    return pl.pallas_call(
        _inproj_kernel,
        grid=(n // TM_PROJ,),
        in_specs=[
            pl.BlockSpec((TM_PROJ, D_MODEL), row),
            pl.BlockSpec((1, D_MODEL), fixed),
            pl.BlockSpec((D_MODEL, PROJ_COLS), fixed),
            pl.BlockSpec((1, QKV_COLS), fixed),
        ],
        out_specs=[
            pl.BlockSpec((TM_PROJ, SSM_WIDTH), row),
            pl.BlockSpec((TM_PROJ, CONV_CH), row),
            pl.BlockSpec((TM_PROJ, DT_PAD), row),
            pl.BlockSpec((TM_PROJ, QKV_COLS), row),
        ],
        out_shape=[
            jax.ShapeDtypeStruct((n, SSM_WIDTH), F32),
            jax.ShapeDtypeStruct((n, CONV_CH), F32),
            jax.ShapeDtypeStruct((n, DT_PAD), F32),
            jax.ShapeDtypeStruct((n, QKV_COLS), F32),
        ],
        compiler_params=_params(("arbitrary",)),
    )(x2, g_mix, w_cat, b_qkv)


def _split3(x):
    hi = x.astype(BF16)
    rest = x - hi.astype(F32)
    mid = rest.astype(BF16)
    lo = (rest - mid.astype(F32)).astype(BF16)
    return jnp.concatenate([hi, mid, lo], axis=1)


def _ssd_kernel(xbc_ref, z_ref, dt_ref, cw_ref, cb_ref, dtb_ref, ah_ref, dsk_ref, gout_ref,
                e64_ref, ecat_ref, tri_ref, y_ref, conv_ref, state_ref):
    c = pl.program_id(0)

    @pl.when(c == 0)
    def _():
        conv_ref[:, :SUBLANES, :] = jnp.zeros((conv_ref.shape[0], SUBLANES, CONV_CH), F32)
        state_ref[...] = jnp.zeros_like(state_ref)

    for b in range(xbc_ref.shape[0]):
        _ssd_chunk(b, xbc_ref, z_ref, dt_ref, cw_ref, cb_ref, dtb_ref, ah_ref, dsk_ref, gout_ref,
                   e64_ref, ecat_ref, tri_ref, y_ref, conv_ref, state_ref)


def _ssd_chunk(b, xbc_ref, z_ref, dt_ref, cw_ref, cb_ref, dtb_ref, ah_ref, dsk_ref, gout_ref,
               e64_ref, ecat_ref, tri_ref, y_ref, conv_ref, state_ref):
    u = xbc_ref[b]
    conv_ref[b, SUBLANES:, :] = u
    acc = u * cw_ref[CONV_WIDTH - 1:CONV_WIDTH, :] + cb_ref[...]
    for j in range(1, CONV_WIDTH):
        shifted = conv_ref[b, SUBLANES - j:SUBLANES - j + CHUNK, :]
        acc = acc + shifted * cw_ref[CONV_WIDTH - 1 - j:CONV_WIDTH - j, :]
    conv_ref[b, :SUBLANES, :] = u[CHUNK - SUBLANES:, :]
    xbc = acc * _sigmoid(acc)

    xs = xbc[:, :SSM_WIDTH]
    bm = xbc[:, SSM_WIDTH:SSM_WIDTH + SSM_GROUPS * SSM_STATE]
    cm = xbc[:, SSM_WIDTH + SSM_GROUPS * SSM_STATE:]

    dtr = dt_ref[b] + dtb_ref[...]
    dt = jnp.maximum(dtr, 0.0) + jnp.log1p(jnp.exp(-jnp.abs(dtr)))
    a = dt * ah_ref[...]
    a3 = _split3(a)
    a_rows = jnp.concatenate([a3[:, :DT_PAD], a3[:, DT_PAD:2 * DT_PAD], a3[:, 2 * DT_PAD:]], axis=0)
    cs = _dot(tri_ref[...], a_rows)
    cs_row = cs.T
    dt_full = _dot(_split3(dt), e64_ref[...])
    cs_cat = _dot(_split3(cs), ecat_ref[...])
    cs_wide = cs_cat[:, :SSM_HEADS * CHUNK]
    cs_full = cs_cat[:, SSM_HEADS * CHUNK:]
    cs_last = cs_full[CHUNK - 1:CHUNK, :]

    xdt = xs * dt_full
    xdt_b = xdt.astype(BF16)
    xw_b = (xdt * jnp.exp(cs_last - cs_full)).astype(BF16)
    in_decay = jnp.exp(cs_full)
    chunk_decay = jnp.exp(cs_last)

    li = lax.broadcasted_iota(I32, (CHUNK, CHUNK), 0)
    si = lax.broadcasted_iota(I32, (CHUNK, CHUNK), 1)
    causal = li >= si

    ys = []
    for g in range(SSM_GROUPS):
        bg = bm[:, g * SSM_STATE:(g + 1) * SSM_STATE].astype(BF16)
        cg = cm[:, g * SSM_STATE:(g + 1) * SSM_STATE].astype(BF16)
        cb = _nt(cg, bg)
        yd = []
        for r in range(HEADS_PER_GROUP):
            h = g * HEADS_PER_GROUP + r
            diff = cs_wide[:, h * CHUNK:(h + 1) * CHUNK] - cs_row[h:h + 1, :]
            m = cb * jnp.exp(jnp.where(causal, diff, -jnp.inf))
            yd.append(_dot(m.astype(BF16), xdt_b[:, h * SSM_HEAD_DIM:(h + 1) * SSM_HEAD_DIM]))
        y_diag = jnp.concatenate(yd, axis=1)
        gs = slice(g * GROUP_WIDTH, (g + 1) * GROUP_WIDTH)
        st = state_ref[b, g]
        y_off = _dot(cg, st.astype(BF16)) * in_decay[:, gs]
        state_ref[b, g] = st * chunk_decay[:, gs] + _tn(bg, xw_b[:, gs])
        ys.append(y_diag + y_off)
    y = jnp.concatenate(ys, axis=1) + dsk_ref[...] * xs

    zz = z_ref[b]
    y = y * (zz * _sigmoid(zz))
    outs = []
    for g in range(SSM_GROUPS):
        yg = y[:, g * GROUP_WIDTH:(g + 1) * GROUP_WIDTH]
        outs.append(yg * lax.rsqrt(jnp.mean(yg * yg, axis=-1, keepdims=True) + RMS_EPS))
    y_ref[b] = jnp.concatenate(outs, axis=1) * gout_ref[...]


def _ssd(xbc, z, dtp, conv_w, conv_b, dt_bias_p, a_head_p, d_full, g_out):
    batch, seq, _ = xbc.shape
    chunk = lambda c: (0, c, 0)
    fixed = lambda c: (0, 0)
    head = np.arange(SSM_WIDTH) // SSM_HEAD_DIM
    e64 = (np.arange(DT_PAD)[:, None] == head[None, :]).astype(np.float32)
    head_w = np.arange(SSM_HEADS * CHUNK) // CHUNK
    e128 = (np.arange(DT_PAD)[:, None] == head_w[None, :]).astype(np.float32)
    e64_3 = jnp.asarray(np.tile(e64, (3, 1))).astype(BF16)
    ecat_3 = jnp.asarray(np.tile(np.concatenate([e128, e64], axis=1), (3, 1))).astype(BF16)
    tri_3 = jnp.asarray(np.tile(np.tril(np.ones((CHUNK, CHUNK), np.float32)), (1, 3))).astype(BF16)
    return pl.pallas_call(
        _ssd_kernel,
        name="ssd",
        grid=(seq // CHUNK,),
        in_specs=[
            pl.BlockSpec((batch, CHUNK, CONV_CH), chunk),
            pl.BlockSpec((batch, CHUNK, SSM_WIDTH), chunk),
            pl.BlockSpec((batch, CHUNK, DT_PAD), chunk),
            pl.BlockSpec((CONV_WIDTH, CONV_CH), fixed),
            pl.BlockSpec((1, CONV_CH), fixed),
            pl.BlockSpec((1, DT_PAD), fixed),
            pl.BlockSpec((1, DT_PAD), fixed),
            pl.BlockSpec((1, SSM_WIDTH), fixed),
            pl.BlockSpec((1, SSM_WIDTH), fixed),
            pl.BlockSpec((3 * DT_PAD, SSM_WIDTH), fixed),
            pl.BlockSpec((3 * DT_PAD, SSM_HEADS * CHUNK + SSM_WIDTH), fixed),
            pl.BlockSpec((CHUNK, 3 * CHUNK), fixed),
        ],
        out_specs=pl.BlockSpec((batch, CHUNK, SSM_WIDTH), chunk),
        out_shape=jax.ShapeDtypeStruct((batch, seq, SSM_WIDTH), F32),
        scratch_shapes=[
            pltpu.VMEM((batch, SUBLANES + CHUNK, CONV_CH), F32),
            pltpu.VMEM((batch, SSM_GROUPS, SSM_STATE, GROUP_WIDTH), F32),
        ],
        compiler_params=_params(("arbitrary",)),
    )(xbc, z, dtp, conv_w, conv_b, dt_bias_p, a_head_p, d_full, g_out, e64_3, ecat_3, tri_3)


def _t5_bucket_table():
    q_pos = np.arange(BLOCK)[:, None]
    k_pos = np.arange(2 * BLOCK)[None, :] - BLOCK
    dist = q_pos - k_pos
    d = np.maximum(dist, 0)
    max_exact = REL_BUCKETS // 2
    ratio = np.maximum(d, max_exact).astype(np.float32) / np.float32(max_exact)
    large = max_exact + (np.log(ratio) / np.float32(math.log(REL_MAX_DIST / max_exact))
                         * np.float32(REL_BUCKETS - max_exact)).astype(np.int32)
    large = np.minimum(large, REL_BUCKETS - 1)
    bucket = np.where(d < max_exact, d, large)
    in_window = (dist >= 0) & (dist < WINDOW)
    return np.where(in_window, bucket, -1).astype(np.int32)


def _bias_kernel(rb_ref, bucket_ref, out_ref):
    bucket = bucket_ref[...]
    for h in range(ATTN_HEADS):
        acc = jnp.zeros(bucket.shape, F32)
        for b in range(REL_BUCKETS):
            acc = jnp.where(bucket == b, rb_ref[b, h], acc)
        out_ref[h] = jnp.where(bucket >= 0, acc, -jnp.inf)


def _bias_table(rel_bias):
    bucket = jnp.asarray(_t5_bucket_table())
    return pl.pallas_call(
        _bias_kernel,
        in_specs=[
            pl.BlockSpec(memory_space=pltpu.SMEM),
            pl.BlockSpec(memory_space=pltpu.VMEM),
        ],
        out_specs=pl.BlockSpec(memory_space=pltpu.VMEM),
        out_shape=jax.ShapeDtypeStruct((ATTN_HEADS, BLOCK, 2 * BLOCK), F32),
    )(rel_bias, bucket)


def _swa_kernel(sink_ref, q_ref, k_ref, v_ref, kp_ref, vp_ref, bias_ref, o_ref):
    i = pl.program_id(0)
    col = lax.broadcasted_iota(I32, (BLOCK, 2 * BLOCK), 1)
    key_ok = jnp.logical_or(col >= BLOCK, i > 0)
    for b in range(q_ref.shape[0]):
        q = (q_ref[b] * (HEAD_DIM ** -0.5)).astype(BF16)
        outs = []
        for g in range(KV_HEADS):
            gs = slice(g * HEAD_DIM, (g + 1) * HEAD_DIM)
            k2 = jnp.concatenate([kp_ref[b, :, gs], k_ref[b, :, gs]], axis=0).astype(BF16)
            v2 = jnp.concatenate([vp_ref[b, :, gs], v_ref[b, :, gs]], axis=0).astype(BF16)
            for r in range(Q_PER_KV):
                h = g * Q_PER_KV + r
                s = _nt(q[:, h * HEAD_DIM:(h + 1) * HEAD_DIM], k2) + bias_ref[h]
                s = jnp.where(key_ok, s, -jnp.inf)
                sink = sink_ref[h]
                m = jnp.maximum(jnp.max(s, axis=-1, keepdims=True), sink)
                p = jnp.exp(s - m)
                denom = jnp.sum(p, axis=-1, keepdims=True) + jnp.exp(sink - m)
                outs.append(_dot(p.astype(BF16), v2) / denom)
        o_ref[b] = jnp.concatenate(outs, axis=1)


def _swa(qkv, sinks, bias):
    batch, seq, _ = qkv.shape
    kcol = ATTN_WIDTH // KV_WIDTH
    prev = lambda i: jnp.maximum(i - 1, 0)
    return pl.pallas_call(
        _swa_kernel,
        name="swa",
        grid_spec=pltpu.PrefetchScalarGridSpec(
            num_scalar_prefetch=1,
            grid=(seq // BLOCK,),
            in_specs=[
                pl.BlockSpec((batch, BLOCK, ATTN_WIDTH), lambda i, s: (0, i, 0)),
                pl.BlockSpec((batch, BLOCK, KV_WIDTH), lambda i, s: (0, i, kcol)),
                pl.BlockSpec((batch, BLOCK, KV_WIDTH), lambda i, s: (0, i, kcol + 1)),
                pl.BlockSpec((batch, BLOCK, KV_WIDTH), lambda i, s: (0, prev(i), kcol)),
                pl.BlockSpec((batch, BLOCK, KV_WIDTH), lambda i, s: (0, prev(i), kcol + 1)),
                pl.BlockSpec((ATTN_HEADS, BLOCK, 2 * BLOCK), lambda i, s: (0, 0, 0)),
            ],
            out_specs=pl.BlockSpec((batch, BLOCK, ATTN_WIDTH), lambda i, s: (0, i, 0)),
        ),
        out_shape=jax.ShapeDtypeStruct((batch, seq, ATTN_WIDTH), F32),
        compiler_params=_params(("arbitrary",)),
    )(sinks, qkv, qkv, qkv, qkv, qkv, bias)


def _memkv_kernel(m_ref, g_ref, w_ref, k_ref, v_ref):
    u = _rms(m_ref[...], g_ref[...])
    kv = _dot(u.astype(BF16), w_ref[...])
    k_ref[...] = kv[:, :CROSS_WIDTH].astype(BF16)
    v_ref[...] = kv[:, CROSS_WIDTH:].astype(BF16)


def _memkv(mem2, g_mem, w_kv):
    n = mem2.shape[0]
    return pl.pallas_call(
        _memkv_kernel,
        out_shape=[jax.ShapeDtypeStruct((n, CROSS_WIDTH), BF16)] * 2,
        compiler_params=_params(),
    )(mem2, g_mem, w_kv)


def _mid_kernel(ys_ref, ya_ref, x_ref, wout_ref, bout_ref, gc_ref, wq_ref, k_ref, v_ref, wo_ref,
                gf_ref, wr_ref, br_ref, upper_ref, lower_ref,
                h2_ref, u3_ref, gate_ref, lp_ref, tcnt_ref, tbase_ref, run_ref):
    i = pl.program_id(0)

    @pl.when(i == 0)
    def _():
        run_ref[...] = jnp.zeros_like(run_ref)

    u3_parts = []
    for part in range(MID_PARTS):
        rows = slice(part * (TM_MID // MID_PARTS), (part + 1) * (TM_MID // MID_PARTS))
        ycat = jnp.concatenate([ys_ref[rows, :], ya_ref[rows, :]], axis=1).astype(BF16)
        h1 = x_ref[rows, :] + _dot(ycat, wout_ref[...]) + bout_ref[...]

        u2 = _rms(h1, gc_ref[...])
        q = _dot(u2.astype(BF16), wq_ref[...]).astype(BF16)
        outs = []
        for h in range(CROSS_HEADS):
            hs = slice(h * CROSS_HEAD_DIM, (h + 1) * CROSS_HEAD_DIM)
            s = _nt(q[:, hs], k_ref[:, hs]) * (CROSS_HEAD_DIM ** -0.5)
            m = jnp.max(s, axis=-1, keepdims=True)
            p = jnp.exp(s - m)
            outs.append(_dot(p.astype(BF16), v_ref[:, hs]) / jnp.sum(p, axis=-1, keepdims=True))
        o = jnp.concatenate(outs, axis=1).astype(BF16)
        h2 = h1 + _dot(o, wo_ref[...])
        h2_ref[rows, :] = h2
        u3_parts.append(_rms(h2, gf_ref[...]).astype(BF16))
    u3 = jnp.concatenate(u3_parts, axis=0)
    u3_ref[...] = u3

    logits = _nt(wr_ref[...], u3) + br_ref[...]
    eid = lax.broadcasted_iota(I32, logits.shape, 0)
    vals, idxs, hots = [], [], []
    for _ in range(TOP_K):
        m = jnp.max(logits, axis=0, keepdims=True)
        ix = jnp.min(jnp.where(logits == m, eid, N_EXPERTS), axis=0, keepdims=True)
        hot = eid == ix
        logits = jnp.where(hot, -jnp.inf, logits)
        vals.append(m)
        idxs.append(ix)
        hots.append(hot)
    ex = [jnp.exp(v - vals[0]) for v in vals]
    tot = ex[0] + ex[1] + ex[2] + ex[3]
    gate_ref[...] = jnp.concatenate([e / tot for e in ex], axis=0)

    hot_f = [jnp.where(hot, 1.0, 0.0) for hot in hots]
    cnt_k = [jnp.sum(hf, axis=1, keepdims=True) for hf in hot_f]
    tile_cnt = cnt_k[0] + cnt_k[1] + cnt_k[2] + cnt_k[3]
    tile_cnt = jnp.floor((tile_cnt + (SUBLANES - 1)) * (1.0 / SUBLANES)) * SUBLANES
    offs = _dot_exact(lower_ref[...], jnp.broadcast_to(tile_cnt, (N_EXPERTS, LANES)))[:, :1]
    local = []
    for kk in range(TOP_K):
        before = _dot(hot_f[kk].astype(BF16), upper_ref[...])
        local.append(jnp.sum(jnp.where(hots[kk], before + offs, 0.0), axis=0, keepdims=True))
        offs = offs + cnt_k[kk]
    lp_ref[...] = jnp.concatenate(local, axis=0).astype(I32)
    tcnt_ref[0] = jnp.broadcast_to(tile_cnt, (N_EXPERTS, LANES)).astype(I32)
    tbase_ref[0] = jnp.broadcast_to(run_ref[...], (N_EXPERTS, LANES)).astype(I32)
    run_ref[...] = run_ref[...] + tile_cnt


def _mid(y_ssm, y_attn, x2, w_out, b_out, g_cross, w_q, k_mem, v_mem, w_o, g_ffn, w_r_t, b_r, seq, mem_len):
    n = x2.shape[0]
    tiles_per_batch = seq // TM_MID
    row = lambda i: (i, 0)
    col = lambda i: (0, i)
    fixed = lambda i: (0, 0)
    memb = lambda i: (i // tiles_per_batch, 0)
    tile3 = lambda i: (i, 0, 0)
    n_tiles = n // TM_MID
    upper = jnp.asarray(np.triu(np.ones((TM_MID, TM_MID), np.float32), 1)).astype(BF16)
    lower = jnp.asarray(np.tril(np.ones((N_EXPERTS, N_EXPERTS), np.float32), -1))
    return pl.pallas_call(
        _mid_kernel,
        name="mid",
        grid=(n_tiles,),
        in_specs=[
            pl.BlockSpec((TM_MID, SSM_WIDTH), row),
            pl.BlockSpec((TM_MID, ATTN_WIDTH), row),
            pl.BlockSpec((TM_MID, D_MODEL), row),
            pl.BlockSpec((D_MODEL, D_MODEL), fixed),
            pl.BlockSpec((1, D_MODEL), fixed),
            pl.BlockSpec((1, D_MODEL), fixed),
            pl.BlockSpec((D_MODEL, CROSS_WIDTH), fixed),
            pl.BlockSpec((mem_len, CROSS_WIDTH), memb),
            pl.BlockSpec((mem_len, CROSS_WIDTH), memb),
            pl.BlockSpec((CROSS_WIDTH, D_MODEL), fixed),
            pl.BlockSpec((1, D_MODEL), fixed),
            pl.BlockSpec((N_EXPERTS, D_MODEL), fixed),
            pl.BlockSpec((N_EXPERTS, 1), fixed),
            pl.BlockSpec((TM_MID, TM_MID), fixed),
            pl.BlockSpec((N_EXPERTS, N_EXPERTS), fixed),
        ],
        out_specs=[
            pl.BlockSpec((TM_MID, D_MODEL), row),
            pl.BlockSpec((TM_MID, D_MODEL), row),
            pl.BlockSpec((TOP_K, TM_MID), col),
            pl.BlockSpec((TOP_K, TM_MID), col),
            pl.BlockSpec((1, N_EXPERTS, LANES), tile3),
            pl.BlockSpec((1, N_EXPERTS, LANES), tile3),
        ],
        out_shape=[
            jax.ShapeDtypeStruct((n, D_MODEL), F32),
            jax.ShapeDtypeStruct((n, D_MODEL), BF16),
            jax.ShapeDtypeStruct((TOP_K, n), F32),
            jax.ShapeDtypeStruct((TOP_K, n), I32),
            jax.ShapeDtypeStruct((n_tiles, N_EXPERTS, LANES), I32),
            jax.ShapeDtypeStruct((n_tiles, N_EXPERTS, LANES), I32),
        ],
        scratch_shapes=[pltpu.VMEM((N_EXPERTS, 1), F32)],
        compiler_params=_params(("arbitrary",)),
    )(y_ssm, y_attn, x2, w_out, b_out, g_cross, w_q, k_mem, v_mem, w_o, g_ffn, w_r_t, b_r, upper, lower)


ROW_CHUNK = 256
LOCAL_ROWS = -(-(TM_MID * TOP_K + N_EXPERTS * (SUBLANES - 1)) // ROW_CHUNK) * ROW_CHUNK
RUN_BITS = tuple(1 << s for s in range(TM_MID.bit_length() - 1, SUBLANES.bit_length() - 2, -1))
PAD_BITS = tuple(1 << s for s in range((MOE_ROWS - 1).bit_length() - 1, SUBLANES.bit_length() - 2, -1))


U32 = jnp.uint32
PACKED = D_MODEL // 2


def _pack_pairs(v):
    bits = lax.bitcast_convert_type(v, U32)
    return bits[:, PACKED:] | (bits[:, :PACKED] >> 16)


def _unpack_pairs(w):
    lo = lax.bitcast_convert_type(w << 16, F32)
    hi = lax.bitcast_convert_type(w & jnp.uint32(0xFFFF0000), F32)
    return jnp.concatenate([lo, hi], axis=1).astype(BF16)


def _run_copies(n, src_ref, src, dst_ref, dst, sem, bits, wait, advance_src=True):
    for bit in bits:
        step = n & bit

        @pl.when(step != 0)
        def _():
            cp = pltpu.make_async_copy(src_ref.at[pl.ds(pl.multiple_of(src, SUBLANES), bit)],
                                       dst_ref.at[pl.ds(pl.multiple_of(dst, SUBLANES), bit)], sem)
            if wait:
                cp.wait()
            else:
                cp.start()

        if advance_src:
            src = src + step
        dst = dst + step


def _wait_rows(n, src_ref, dst_ref, sem):
    rows = pl.ds(0, pl.multiple_of(n, SUBLANES))
    pltpu.make_async_copy(src_ref.at[rows], dst_ref.at[rows], sem).wait()


def _dispatch_kernel(tcnt_ref, lo_ref, gs_ref, padn_ref, pads_ref, na_ref, u_ref, lp_ref, xs_ref,
                     xloc_ref, zero_ref, sem, zsem):
    i = pl.program_id(0)

    def zero_fill(wait):
        def body(e, carry):
            _run_copies(padn_ref[e], zero_ref, 0, xs_ref, pads_ref[e], zsem, PAD_BITS, wait, advance_src=False)
            return carry
        lax.fori_loop(0, N_EXPERTS, body, 0)

        def tail(b, carry):
            cp = pltpu.make_async_copy(zero_ref, xs_ref.at[pl.ds(pl.multiple_of(b * MOE_ROWS, MOE_ROWS), MOE_ROWS)], zsem)
            if wait:
                cp.wait()
            else:
                cp.start()
            return carry
        lax.fori_loop(na_ref[0], xs_ref.shape[0] // MOE_ROWS, tail, 0)

    @pl.when(i == 0)
    def _():
        zero_ref[...] = jnp.zeros_like(zero_ref)
        zero_fill(False)

    slot = i % 2
    u = u_ref[...]
    lp = lp_ref[...]
    for c in range(LOCAL_ROWS // ROW_CHUNK):
        r = lax.broadcasted_iota(I32, (ROW_CHUNK, TM_MID), 0) + c * ROW_CHUNK
        p = jnp.zeros((ROW_CHUNK, TM_MID), F32)
        for kk in range(TOP_K):
            p = jnp.where(lp[kk:kk + 1, :] == r, 1.0, p)
        xloc_ref[slot, c * ROW_CHUNK:(c + 1) * ROW_CHUNK, :] = _pack_pairs(_dot(p.astype(BF16), u))

    def start_runs(tile):
        s = tile % 2

        def body(e, carry):
            j = tile * N_EXPERTS + e
            _run_copies(tcnt_ref[j], xloc_ref.at[s], lo_ref[j], xs_ref, gs_ref[j], sem.at[s], RUN_BITS, False)
            return carry
        lax.fori_loop(0, N_EXPERTS, body, 0)

    def wait_runs(tile):
        s = tile % 2
        last = tile * N_EXPERTS + N_EXPERTS - 1
        _wait_rows(lo_ref[last] + tcnt_ref[last], xloc_ref.at[s], xs_ref, sem.at[s])

    start_runs(i)

    @pl.when(i > 0)
    def _():
        wait_runs(i - 1)

    @pl.when(i == pl.num_programs(0) - 1)
    def _():
        wait_runs(i)

    @pl.when(i == 0)
    def _():
        zero_fill(True)


def _dispatch(tcnt, lo, gstart, padn, pads, n_active, u3, lp, n_rows):
    n = u3.shape[0]
    return pl.pallas_call(
        _dispatch_kernel,
        name="dispatch",
        grid_spec=pltpu.PrefetchScalarGridSpec(
            num_scalar_prefetch=6,
            grid=(n // TM_MID,),
            in_specs=[
                pl.BlockSpec((TM_MID, D_MODEL), lambda i, *_: (i, 0)),
                pl.BlockSpec((TOP_K, TM_MID), lambda i, *_: (0, i)),
            ],
            out_specs=pl.BlockSpec(memory_space=pl.ANY),
            scratch_shapes=[
                pltpu.VMEM((2, LOCAL_ROWS, PACKED), U32),
                pltpu.VMEM((MOE_ROWS, PACKED), U32),
                pltpu.SemaphoreType.DMA((2,)),
                pltpu.SemaphoreType.DMA,
            ],
        ),
        out_shape=jax.ShapeDtypeStruct((n_rows, PACKED), U32),
        compiler_params=_params(("arbitrary",)),
    )(tcnt, lo, gstart, padn, pads, n_active, u3, lp)


PAIR_COLS = 2 * LANES


def _expert_kernel(be_ref, nx_ref, nv_ref, na_ref, x_ref, b1_ref, b2_ref, perm_ref, w1_hbm, w2_hbm, y_ref,
                   w1s_ref, w2s_ref, w1p_ref, w2b_ref, sem):
    b = pl.program_id(0)
    valid_rows = nv_ref[b]
    active = valid_rows > 0
    expert = be_ref[b]
    new_expert = jnp.logical_or(b == 0, expert != be_ref[jnp.maximum(b - 1, 0)])

    def fetch(e, wait):
        for src, dst, s in ((w1_hbm, w1s_ref, 0), (w2_hbm, w2s_ref, 1)):
            for r in range(0, src.shape[1], WEIGHT_DMA_ROWS):
                rs = pl.ds(r, WEIGHT_DMA_ROWS)
                cp = pltpu.make_async_copy(src.at[e, rs], dst.at[rs], sem.at[s])
                if wait:
                    cp.wait()
                else:
                    cp.start(priority=1)

    @pl.when(b == 0)
    def _():
        fetch(expert, False)

    @pl.when(jnp.logical_and(active, new_expert))
    def _():
        fetch(expert, True)
        for c in range(2 * D_FF // PAIR_COLS):
            cs = slice(c * PAIR_COLS, (c + 1) * PAIR_COLS)
            w1p_ref[:, cs] = _dot(w1s_ref[:, cs].astype(BF16), perm_ref[...]).astype(BF16)
        w2b_ref[...] = w2s_ref[...].astype(BF16)

        @pl.when(nx_ref[b] >= 0)
        def _():
            fetch(nx_ref[b], False)

    for part in range(MOE_ROWS // MOE_PART):
        rows = slice(part * MOE_PART, (part + 1) * MOE_PART)
        live = valid_rows > part * MOE_PART

        @pl.when(live)
        def _():
            x = _unpack_pairs(x_ref[rows, :])
            hdn = _dot(x, w1p_ref[...]) + b1_ref[0]
            acts = []
            for c in range(2 * D_FF // PAIR_COLS):
                hg = jnp.minimum(hdn[:, c * PAIR_COLS:c * PAIR_COLS + LANES], SWIGLU_LIMIT)
                hl = jnp.clip(hdn[:, c * PAIR_COLS + LANES:(c + 1) * PAIR_COLS], -SWIGLU_LIMIT, SWIGLU_LIMIT)
                acts.append((hg * _sigmoid(SWIGLU_ALPHA * hg) * (hl + 1.0)).astype(BF16))
            y = _dot(jnp.concatenate(acts, axis=1), w2b_ref[...]) + b2_ref[0]
            y_ref[rows, :] = _pack_pairs(y.astype(BF16).astype(F32))

        @pl.when(jnp.logical_not(live))
        def _():
            y_ref[rows, :] = jnp.zeros((MOE_PART, PACKED), U32)


def _experts(block_exp, next_exp, valid_rows, n_active, xs, w1, b1p, w2, b2):
    n_rows = xs.shape[0]
    n_blocks = n_rows // MOE_ROWS
    xmap = lambda b, be, nx, nv, na: (jnp.maximum(jnp.minimum(b, na[0] - 1), 0), 0)
    emap = lambda b, be, nx, nv, na: (be[b], 0, 0)
    perm = np.zeros((PAIR_COLS, PAIR_COLS), np.float32)
    perm[2 * np.arange(LANES), np.arange(LANES)] = 1.0
    perm[2 * np.arange(LANES) + 1, LANES + np.arange(LANES)] = 1.0
    return pl.pallas_call(
        _expert_kernel,
        name="experts",
        grid_spec=pltpu.PrefetchScalarGridSpec(
            num_scalar_prefetch=4,
            grid=(n_blocks,),
            in_specs=[
                pl.BlockSpec((MOE_ROWS, PACKED), xmap),
                pl.BlockSpec((1, 1, 2 * D_FF), emap),
                pl.BlockSpec((1, 1, D_MODEL), emap),
                pl.BlockSpec((PAIR_COLS, PAIR_COLS), lambda b, *_: (0, 0)),
                pl.BlockSpec(memory_space=pl.ANY),
                pl.BlockSpec(memory_space=pl.ANY),
            ],
            out_specs=pl.BlockSpec((MOE_ROWS, PACKED), lambda b, *_: (b, 0)),
            scratch_shapes=[
                pltpu.VMEM((D_MODEL, 2 * D_FF), F32),
                pltpu.VMEM((D_FF, D_MODEL), F32),
                pltpu.VMEM((D_MODEL, 2 * D_FF), BF16),
                pltpu.VMEM((D_FF, D_MODEL), BF16),
                pltpu.SemaphoreType.DMA((2,)),
            ],
        ),
        out_shape=jax.ShapeDtypeStruct((n_rows, PACKED), U32),
        compiler_params=_params(("arbitrary",)),
    )(block_exp, next_exp, valid_rows, n_active, xs, b1p, b2, jnp.asarray(perm).astype(BF16), w1, w2)


def _combine_kernel(tcnt_ref, lo_ref, gs_ref, h2_ref, gate_ref, lp_ref, g_ref, ys_ref, o_ref, yloc_ref, sem):
    i = pl.program_id(0)

    slot = i % 2

    def start_runs(tile):
        s = tile % 2

        def body(e, carry):
            j = tile * N_EXPERTS + e
            _run_copies(tcnt_ref[j], ys_ref, gs_ref[j], yloc_ref.at[s], lo_ref[j], sem.at[s], RUN_BITS, False)
            return carry
        lax.fori_loop(0, N_EXPERTS, body, 0)

    @pl.when(i == 0)
    def _():
        yloc_ref[...] = jnp.zeros_like(yloc_ref)
        start_runs(i)

    @pl.when(i + 1 < pl.num_programs(0))
    def _():
        start_runs(i + 1)

    gates = gate_ref[...].T
    lp = lp_ref[...].astype(F32).T
    last = i * N_EXPERTS + N_EXPERTS - 1
    _wait_rows(lo_ref[last] + tcnt_ref[last], ys_ref, yloc_ref.at[slot], sem.at[slot])

    h = h2_ref[...]
    for c in range(LOCAL_ROWS // ROW_CHUNK):
        lane = (lax.broadcasted_iota(I32, (TM_MID, ROW_CHUNK), 1) + c * ROW_CHUNK).astype(F32)
        a = jnp.zeros((TM_MID, ROW_CHUNK), F32)
        for kk in range(TOP_K):
            a = jnp.where(lp[:, kk:kk + 1] == lane, gates[:, kk:kk + 1], a)
        h = h + _dot(a.astype(BF16), _unpack_pairs(yloc_ref[slot, c * ROW_CHUNK:(c + 1) * ROW_CHUNK, :]))
    o_ref[...] = _rms(h, g_ref[...])


def _combine(tcnt, lo, gstart, h2, gates, lp, g_final, ys):
    n = h2.shape[0]
    return pl.pallas_call(
        _combine_kernel,
        name="combine",
        grid_spec=pltpu.PrefetchScalarGridSpec(
            num_scalar_prefetch=3,
            grid=(n // TM_MID,),
            in_specs=[
                pl.BlockSpec((TM_MID, D_MODEL), lambda i, *_: (i, 0)),
                pl.BlockSpec((TOP_K, TM_MID), lambda i, *_: (0, i)),
                pl.BlockSpec((TOP_K, TM_MID), lambda i, *_: (0, i)),
                pl.BlockSpec((1, D_MODEL), lambda i, *_: (0, 0)),
                pl.BlockSpec(memory_space=pl.ANY),
            ],
            out_specs=pl.BlockSpec((TM_MID, D_MODEL), lambda i, *_: (i, 0)),
            scratch_shapes=[
                pltpu.VMEM((2, LOCAL_ROWS, PACKED), U32),
                pltpu.SemaphoreType.DMA((2,)),
            ],
        ),
        out_shape=jax.ShapeDtypeStruct((n, D_MODEL), F32),
        compiler_params=_params(("arbitrary",)),
    )(tcnt, lo, gstart, h2, gates, lp, g_final, ys)


def _layer(h, mem2, batch, seq, mem_len, g_mix, w_in, b_qkv, conv_w, conv_b, dt_bias, a_log, d_skip,
           g_ssm_out, attn_sinks, bias, w_out, b_out, g_cross, g_mem, w_q_cross, w_kv_cross, w_o_cross,
           g_ffn, w_router, b_router, w1, b1, w2, b2, g_final):
    n = h.shape[0]
    c1, c2 = SSM_WIDTH + CONV_CH, SSM_WIDTH + CONV_CH + SSM_HEADS
    w_cat = jnp.concatenate(
        [w_in[:, :c1], jnp.pad(w_in[:, c1:c2], ((0, 0), (0, DT_PAD - SSM_HEADS))), w_in[:, c2:]],
        axis=1).astype(BF16)
    z, xbc, dtp, qkv = _inproj(h, g_mix[None, :], w_cat, b_qkv[None, :])

    pad_h = (0, DT_PAD - SSM_HEADS)
    dt_bias_p = jnp.pad(dt_bias, pad_h)[None, :]
    a_head_p = jnp.pad(-jnp.exp(a_log), pad_h)[None, :]
    d_full = jnp.repeat(d_skip, SSM_HEAD_DIM)[None, :]
    per_seq = lambda t: t.reshape(batch, seq, t.shape[-1])
    y_ssm = _ssd(per_seq(xbc), per_seq(z), per_seq(dtp), conv_w, conv_b[None, :], dt_bias_p, a_head_p,
                 d_full, g_ssm_out[None, :]).reshape(n, SSM_WIDTH)
    y_attn = _swa(per_seq(qkv), attn_sinks, bias).reshape(n, ATTN_WIDTH)

    k_mem, v_mem = _memkv(mem2, g_mem[None, :], w_kv_cross.astype(BF16))
    h2, u3, gates, lp, tcnt, tbase = _mid(
        y_ssm, y_attn, h, w_out.astype(BF16), b_out[None, :], g_cross[None, :],
        w_q_cross.astype(BF16), k_mem, v_mem, w_o_cross.astype(BF16), g_ffn[None, :],
        w_router.T.astype(BF16), b_router[:, None], seq, mem_len)

    max_rows = n * TOP_K + (n // TM_MID) * N_EXPERTS * (SUBLANES - 1)
    n_blocks = -(-max_rows // MOE_ROWS) + N_EXPERTS
    tcnt = tcnt[:, :, 0]
    tbase = tbase[:, :, 0]
    counts = tbase[-1] + tcnt[-1]
    padded = (counts + MOE_ROWS - 1) // MOE_ROWS * MOE_ROWS
    padded_end = jnp.cumsum(padded)
    start = (padded_end - padded).astype(I32)
    n_active = (padded_end[-1] // MOE_ROWS).astype(I32)
    blk = jnp.arange(n_blocks, dtype=I32)
    blk_exp = jnp.sum((blk * MOE_ROWS)[:, None] >= padded_end[None, :], axis=1)
    eid = jnp.arange(N_EXPERTS, dtype=I32)
    last_exp = jnp.max(jnp.where(padded > 0, eid, 0))
    blk_exp = jnp.where(blk < n_active, jnp.minimum(blk_exp, N_EXPERTS - 1), last_exp).astype(I32)
    later = jnp.logical_and(padded[None, :] > 0, eid[None, :] > eid[:, None])
    next_of = jnp.min(jnp.where(later, eid[None, :], N_EXPERTS), axis=1)
    next_of = jnp.where(next_of < N_EXPERTS, next_of, -1).astype(I32)
    of_blk = blk_exp[:, None] == eid[None, :]
    blk_next = jnp.sum(jnp.where(of_blk, next_of[None, :], 0), axis=1).astype(I32)
    blk_end = jnp.sum(jnp.where(of_blk, (start + counts)[None, :], 0), axis=1)
    blk_valid = jnp.clip(blk_end - blk * MOE_ROWS, 0, MOE_ROWS)
    blk_valid = jnp.where(blk < n_active, blk_valid, 0).astype(I32)

    run_n = tcnt.reshape(-1)
    run_local = (jnp.cumsum(tcnt, axis=1) - tcnt).reshape(-1)
    run_global = (start[None, :] + tbase).reshape(-1)
    xs = _dispatch(run_n, run_local, run_global, padded - counts, start + counts, n_active[None], u3, lp,
                   n_blocks * MOE_ROWS)

    b1p = b1.reshape(N_EXPERTS, 2 * D_FF // PAIR_COLS, LANES, 2).transpose(0, 1, 3, 2)
    ys = _experts(blk_exp, blk_next, blk_valid, n_active[None], xs, w1,
                  b1p.reshape(N_EXPERTS, 1, 2 * D_FF), w2, b2[:, None, :])
    return _combine(run_n, run_local, run_global, h2, gates, lp, g_final, ys)


def kernel(x, mem, g_mix, w_in, b_qkv, conv_w, conv_b, dt_bias, a_log, d_skip, g_ssm_out, attn_sinks, rel_bias, w_out, b_out, g_cross, g_mem, w_q_cross, w_kv_cross, w_o_cross, g_ffn, w_router, b_router, w1, b1, w2, b2, g_final):
    batch, seq, d = x.shape
    mem_len = mem.shape[1]
    depth = g_mix.shape[0]
    assert depth == 1 and d == D_MODEL and seq % TM_MID == 0
    h = x.reshape(batch * seq, d)
    mem2 = mem.reshape(batch * mem_len, d)
    bias = _bias_table(rel_bias)
    out = _layer(
        h, mem2, batch, seq, mem_len, g_mix[0], w_in[0], b_qkv[0], conv_w[0], conv_b[0], dt_bias[0],
        a_log[0], d_skip[0], g_ssm_out[0], attn_sinks[0], bias, w_out[0], b_out[0], g_cross[0],
        g_mem[0], w_q_cross[0], w_kv_cross[0], w_o_cross[0], g_ffn[0], w_router[0], b_router[0],
        w1[0], b1[0], w2[0], b2[0], g_final[None, :])
    return out.reshape(batch, seq, d)
```

```python
import math

import numpy as np
import jax
import jax.numpy as jnp
from jax import lax
from jax.experimental import pallas as pl
from jax.experimental.pallas import tpu as pltpu

F32 = jnp.float32
BF16 = jnp.bfloat16
I32 = jnp.int32

D_MODEL = 1024
RMS_EPS = 1e-5

SSM_HEADS = 8
SSM_HEAD_DIM = 64
SSM_WIDTH = SSM_HEADS * SSM_HEAD_DIM
SSM_STATE = 128
SSM_GROUPS = 2
HEADS_PER_GROUP = SSM_HEADS // SSM_GROUPS
GROUP_WIDTH = SSM_WIDTH // SSM_GROUPS
CONV_WIDTH = 4
CHUNK = 128
CONV_CH = SSM_WIDTH + 2 * SSM_GROUPS * SSM_STATE

ATTN_HEADS = 8
KV_HEADS = 2
Q_PER_KV = ATTN_HEADS // KV_HEADS
HEAD_DIM = 64
ATTN_WIDTH = ATTN_HEADS * HEAD_DIM
KV_WIDTH = KV_HEADS * HEAD_DIM
WINDOW = 128
BLOCK = WINDOW
QKV_COLS = ATTN_WIDTH + 2 * KV_WIDTH

REL_BUCKETS = 32
REL_MAX_DIST = 128

CROSS_HEADS = 4
CROSS_HEAD_DIM = 128
CROSS_WIDTH = CROSS_HEADS * CROSS_HEAD_DIM

N_EXPERTS = 32
TOP_K = 4
D_FF = D_MODEL
SWIGLU_ALPHA = 1.702
SWIGLU_LIMIT = 7.0

LANES = 128
SUBLANES = 8
DT_PAD = LANES
PROJ_COLS = SSM_WIDTH + CONV_CH + DT_PAD + QKV_COLS
VMEM_LIMIT = 56 * 1024 * 1024

TM_PROJ = 512
TM_MID = 512
MID_PARTS = 1
MOE_ROWS = 512
MOE_PART = 128
WEIGHT_DMA_ROWS = 256


def _nt(a, b):
    return lax.dot_general(a, b, (((1,), (1,)), ((), ())), preferred_element_type=F32)


def _tn(a, b):
    return lax.dot_general(a, b, (((0,), (0,)), ((), ())), preferred_element_type=F32)


def _dot(a, b):
    return jnp.dot(a, b, preferred_element_type=F32)


def _dot_exact(a, b):
    return jnp.dot(a, b, preferred_element_type=F32, precision=lax.Precision.HIGHEST)


def _rms(x, g):
    return x * lax.rsqrt(jnp.mean(x * x, axis=-1, keepdims=True) + RMS_EPS) * g


def _sigmoid(x):
    return 1.0 / (1.0 + jnp.exp(-x))


def _params(sem=None):
    return pltpu.CompilerParams(dimension_semantics=sem, vmem_limit_bytes=VMEM_LIMIT)


def _inproj_kernel(x_ref, g_ref, w_ref, bq_ref, z_ref, xbc_ref, dt_ref, qkv_ref):
    u = _rms(x_ref[...], g_ref[...])
    p = _dot(u.astype(BF16), w_ref[...])
    z_ref[...] = p[:, :SSM_WIDTH]
    xbc_ref[...] = p[:, SSM_WIDTH:SSM_WIDTH + CONV_CH]
    dt_ref[...] = p[:, SSM_WIDTH + CONV_CH:SSM_WIDTH + CONV_CH + DT_PAD]
    qkv_ref[...] = p[:, SSM_WIDTH + CONV_CH + DT_PAD:] + bq_ref[...]


def _inproj(x2, g_mix, w_cat, b_qkv):
    n = x2.shape[0]
    row = lambda i: (i, 0)
    fixed = lambda i: (0, 0)
    return pl.pallas_call(
        _inproj_kernel,
        grid=(n // TM_PROJ,),
        in_specs=[
            pl.BlockSpec((TM_PROJ, D_MODEL), row),
            pl.BlockSpec((1, D_MODEL), fixed),
            pl.BlockSpec((D_MODEL, PROJ_COLS), fixed),
            pl.BlockSpec((1, QKV_COLS), fixed),
        ],
        out_specs=[
            pl.BlockSpec((TM_PROJ, SSM_WIDTH), row),
            pl.BlockSpec((TM_PROJ, CONV_CH), row),
            pl.BlockSpec((TM_PROJ, DT_PAD), row),
            pl.BlockSpec((TM_PROJ, QKV_COLS), row),
        ],
        out_shape=[
            jax.ShapeDtypeStruct((n, SSM_WIDTH), F32),
            jax.ShapeDtypeStruct((n, CONV_CH), F32),
            jax.ShapeDtypeStruct((n, DT_PAD), F32),
            jax.ShapeDtypeStruct((n, QKV_COLS), F32),
        ],
        compiler_params=_params(("arbitrary",)),
    )(x2, g_mix, w_cat, b_qkv)


def _split3(x):
    hi = x.astype(BF16)
    rest = x - hi.astype(F32)
    mid = rest.astype(BF16)
    lo = (rest - mid.astype(F32)).astype(BF16)
    return jnp.concatenate([hi, mid, lo], axis=1)


def _ssd_kernel(xbc_ref, z_ref, dt_ref, cw_ref, cb_ref, dtb_ref, ah_ref, dsk_ref, gout_ref,
                e64_ref, ecat_ref, tri_ref, y_ref, conv_ref, state_ref):
    c = pl.program_id(0)

    @pl.when(c == 0)
    def _():
        conv_ref[:, :SUBLANES, :] = jnp.zeros((conv_ref.shape[0], SUBLANES, CONV_CH), F32)
        state_ref[...] = jnp.zeros_like(state_ref)

    for b in range(xbc_ref.shape[0]):
        _ssd_chunk(b, xbc_ref, z_ref, dt_ref, cw_ref, cb_ref, dtb_ref, ah_ref, dsk_ref, gout_ref,
                   e64_ref, ecat_ref, tri_ref, y_ref, conv_ref, state_ref)


def _ssd_chunk(b, xbc_ref, z_ref, dt_ref, cw_ref, cb_ref, dtb_ref, ah_ref, dsk_ref, gout_ref,
               e64_ref, ecat_ref, tri_ref, y_ref, conv_ref, state_ref):
    u = xbc_ref[b]
    conv_ref[b, SUBLANES:, :] = u
    acc = u * cw_ref[CONV_WIDTH - 1:CONV_WIDTH, :] + cb_ref[...]
    for j in range(1, CONV_WIDTH):
        shifted = conv_ref[b, SUBLANES - j:SUBLANES - j + CHUNK, :]
        acc = acc + shifted * cw_ref[CONV_WIDTH - 1 - j:CONV_WIDTH - j, :]
    conv_ref[b, :SUBLANES, :] = u[CHUNK - SUBLANES:, :]
    xbc = acc * _sigmoid(acc)

    xs = xbc[:, :SSM_WIDTH]
    bm = xbc[:, SSM_WIDTH:SSM_WIDTH + SSM_GROUPS * SSM_STATE]
    cm = xbc[:, SSM_WIDTH + SSM_GROUPS * SSM_STATE:]

    dtr = dt_ref[b] + dtb_ref[...]
    dt = jnp.maximum(dtr, 0.0) + jnp.log1p(jnp.exp(-jnp.abs(dtr)))
    a = dt * ah_ref[...]
    a3 = _split3(a)
    a_rows = jnp.concatenate([a3[:, :DT_PAD], a3[:, DT_PAD:2 * DT_PAD], a3[:, 2 * DT_PAD:]], axis=0)
    cs = _dot(tri_ref[...], a_rows)
    cs_row = cs.T
    dt_full = _dot(_split3(dt), e64_ref[...])
    cs_cat = _dot(_split3(cs), ecat_ref[...])
    cs_wide = cs_cat[:, :SSM_HEADS * CHUNK]
    cs_full = cs_cat[:, SSM_HEADS * CHUNK:]
    cs_last = cs_full[CHUNK - 1:CHUNK, :]

    xdt = xs * dt_full
    xdt_b = xdt.astype(BF16)
    xw_b = (xdt * jnp.exp(cs_last - cs_full)).astype(BF16)
    in_decay = jnp.exp(cs_full)
    chunk_decay = jnp.exp(cs_last)

    li = lax.broadcasted_iota(I32, (CHUNK, CHUNK), 0)
    si = lax.broadcasted_iota(I32, (CHUNK, CHUNK), 1)
    causal = li >= si

    ys = []
    for g in range(SSM_GROUPS):
        bg = bm[:, g * SSM_STATE:(g + 1) * SSM_STATE].astype(BF16)
        cg = cm[:, g * SSM_STATE:(g + 1) * SSM_STATE].astype(BF16)
        cb = _nt(cg, bg)
        yd = []
        for r in range(HEADS_PER_GROUP):
            h = g * HEADS_PER_GROUP + r
            diff = cs_wide[:, h * CHUNK:(h + 1) * CHUNK] - cs_row[h:h + 1, :]
            m = cb * jnp.exp(jnp.where(causal, diff, -jnp.inf))
            yd.append(_dot(m.astype(BF16), xdt_b[:, h * SSM_HEAD_DIM:(h + 1) * SSM_HEAD_DIM]))
        y_diag = jnp.concatenate(yd, axis=1)
        gs = slice(g * GROUP_WIDTH, (g + 1) * GROUP_WIDTH)
        st = state_ref[b, g]
        y_off = _dot(cg, st.astype(BF16)) * in_decay[:, gs]
        state_ref[b, g] = st * chunk_decay[:, gs] + _tn(bg, xw_b[:, gs])
        ys.append(y_diag + y_off)
    y = jnp.concatenate(ys, axis=1) + dsk_ref[...] * xs

    zz = z_ref[b]
    y = y * (zz * _sigmoid(zz))
    outs = []
    for g in range(SSM_GROUPS):
        yg = y[:, g * GROUP_WIDTH:(g + 1) * GROUP_WIDTH]
        outs.append(yg * lax.rsqrt(jnp.mean(yg * yg, axis=-1, keepdims=True) + RMS_EPS))
    y_ref[b] = jnp.concatenate(outs, axis=1) * gout_ref[...]


def _ssd(xbc, z, dtp, conv_w, conv_b, dt_bias_p, a_head_p, d_full, g_out):
    batch, seq, _ = xbc.shape
    chunk = lambda c: (0, c, 0)
    fixed = lambda c: (0, 0)
    head = np.arange(SSM_WIDTH) // SSM_HEAD_DIM
    e64 = (np.arange(DT_PAD)[:, None] == head[None, :]).astype(np.float32)
    head_w = np.arange(SSM_HEADS * CHUNK) // CHUNK
    e128 = (np.arange(DT_PAD)[:, None] == head_w[None, :]).astype(np.float32)
    e64_3 = jnp.asarray(np.tile(e64, (3, 1))).astype(BF16)
    ecat_3 = jnp.asarray(np.tile(np.concatenate([e128, e64], axis=1), (3, 1))).astype(BF16)
    tri_3 = jnp.asarray(np.tile(np.tril(np.ones((CHUNK, CHUNK), np.float32)), (1, 3))).astype(BF16)
    return pl.pallas_call(
        _ssd_kernel,
        name="ssd",
        grid=(seq // CHUNK,),
        in_specs=[
            pl.BlockSpec((batch, CHUNK, CONV_CH), chunk),
            pl.BlockSpec((batch, CHUNK, SSM_WIDTH), chunk),
            pl.BlockSpec((batch, CHUNK, DT_PAD), chunk),
            pl.BlockSpec((CONV_WIDTH, CONV_CH), fixed),
            pl.BlockSpec((1, CONV_CH), fixed),
            pl.BlockSpec((1, DT_PAD), fixed),
            pl.BlockSpec((1, DT_PAD), fixed),
            pl.BlockSpec((1, SSM_WIDTH), fixed),
            pl.BlockSpec((1, SSM_WIDTH), fixed),
            pl.BlockSpec((3 * DT_PAD, SSM_WIDTH), fixed),
            pl.BlockSpec((3 * DT_PAD, SSM_HEADS * CHUNK + SSM_WIDTH), fixed),
            pl.BlockSpec((CHUNK, 3 * CHUNK), fixed),
        ],
        out_specs=pl.BlockSpec((batch, CHUNK, SSM_WIDTH), chunk),
        out_shape=jax.ShapeDtypeStruct((batch, seq, SSM_WIDTH), F32),
        scratch_shapes=[
            pltpu.VMEM((batch, SUBLANES + CHUNK, CONV_CH), F32),
            pltpu.VMEM((batch, SSM_GROUPS, SSM_STATE, GROUP_WIDTH), F32),
        ],
        compiler_params=_params(("arbitrary",)),
    )(xbc, z, dtp, conv_w, conv_b, dt_bias_p, a_head_p, d_full, g_out, e64_3, ecat_3, tri_3)


def _t5_bucket_table():
    q_pos = np.arange(BLOCK)[:, None]
    k_pos = np.arange(2 * BLOCK)[None, :] - BLOCK
    dist = q_pos - k_pos
    d = np.maximum(dist, 0)
    max_exact = REL_BUCKETS // 2
    ratio = np.maximum(d, max_exact).astype(np.float32) / np.float32(max_exact)
    large = max_exact + (np.log(ratio) / np.float32(math.log(REL_MAX_DIST / max_exact))
                         * np.float32(REL_BUCKETS - max_exact)).astype(np.int32)
    large = np.minimum(large, REL_BUCKETS - 1)
    bucket = np.where(d < max_exact, d, large)
    in_window = (dist >= 0) & (dist < WINDOW)
    return np.where(in_window, bucket, -1).astype(np.int32)


def _bias_kernel(rb_ref, bucket_ref, out_ref):
    bucket = bucket_ref[...]
    for h in range(ATTN_HEADS):
        acc = jnp.zeros(bucket.shape, F32)
        for b in range(REL_BUCKETS):
            acc = jnp.where(bucket == b, rb_ref[b, h], acc)
        out_ref[h] = jnp.where(bucket >= 0, acc, -jnp.inf)


def _bias_table(rel_bias):
    bucket = jnp.asarray(_t5_bucket_table())
    return pl.pallas_call(
        _bias_kernel,
        in_specs=[
            pl.BlockSpec(memory_space=pltpu.SMEM),
            pl.BlockSpec(memory_space=pltpu.VMEM),
        ],
        out_specs=pl.BlockSpec(memory_space=pltpu.VMEM),
        out_shape=jax.ShapeDtypeStruct((ATTN_HEADS, BLOCK, 2 * BLOCK), F32),
    )(rel_bias, bucket)


def _swa_kernel(sink_ref, q_ref, k_ref, v_ref, kp_ref, vp_ref, bias_ref, o_ref):
    i = pl.program_id(0)
    col = lax.broadcasted_iota(I32, (BLOCK, 2 * BLOCK), 1)
    key_ok = jnp.logical_or(col >= BLOCK, i > 0)
    for b in range(q_ref.shape[0]):
        q = (q_ref[b] * (HEAD_DIM ** -0.5)).astype(BF16)
        outs = []
        for g in range(KV_HEADS):
            gs = slice(g * HEAD_DIM, (g + 1) * HEAD_DIM)
            k2 = jnp.concatenate([kp_ref[b, :, gs], k_ref[b, :, gs]], axis=0).astype(BF16)
            v2 = jnp.concatenate([vp_ref[b, :, gs], v_ref[b, :, gs]], axis=0).astype(BF16)
            for r in range(Q_PER_KV):
                h = g * Q_PER_KV + r
                s = _nt(q[:, h * HEAD_DIM:(h + 1) * HEAD_DIM], k2) + bias_ref[h]
                s = jnp.where(key_ok, s, -jnp.inf)
                sink = sink_ref[h]
                m = jnp.maximum(jnp.max(s, axis=-1, keepdims=True), sink)
                p = jnp.exp(s - m)
                denom = jnp.sum(p, axis=-1, keepdims=True) + jnp.exp(sink - m)
                outs.append(_dot(p.astype(BF16), v2) / denom)
        o_ref[b] = jnp.concatenate(outs, axis=1)


def _swa(qkv, sinks, bias):
    batch, seq, _ = qkv.shape
    kcol = ATTN_WIDTH // KV_WIDTH
    prev = lambda i: jnp.maximum(i - 1, 0)
    return pl.pallas_call(
        _swa_kernel,
        name="swa",
        grid_spec=pltpu.PrefetchScalarGridSpec(
            num_scalar_prefetch=1,
            grid=(seq // BLOCK,),
            in_specs=[
                pl.BlockSpec((batch, BLOCK, ATTN_WIDTH), lambda i, s: (0, i, 0)),
                pl.BlockSpec((batch, BLOCK, KV_WIDTH), lambda i, s: (0, i, kcol)),
                pl.BlockSpec((batch, BLOCK, KV_WIDTH), lambda i, s: (0, i, kcol + 1)),
                pl.BlockSpec((batch, BLOCK, KV_WIDTH), lambda i, s: (0, prev(i), kcol)),
                pl.BlockSpec((batch, BLOCK, KV_WIDTH), lambda i, s: (0, prev(i), kcol + 1)),
                pl.BlockSpec((ATTN_HEADS, BLOCK, 2 * BLOCK), lambda i, s: (0, 0, 0)),
            ],
            out_specs=pl.BlockSpec((batch, BLOCK, ATTN_WIDTH), lambda i, s: (0, i, 0)),
        ),
        out_shape=jax.ShapeDtypeStruct((batch, seq, ATTN_WIDTH), F32),
        compiler_params=_params(("arbitrary",)),
    )(sinks, qkv, qkv, qkv, qkv, qkv, bias)


def _memkv_kernel(m_ref, g_ref, w_ref, k_ref, v_ref):
    u = _rms(m_ref[...], g_ref[...])
    kv = _dot(u.astype(BF16), w_ref[...])
    k_ref[...] = kv[:, :CROSS_WIDTH].astype(BF16)
    v_ref[...] = kv[:, CROSS_WIDTH:].astype(BF16)


def _memkv(mem2, g_mem, w_kv):
    n = mem2.shape[0]
    return pl.pallas_call(
        _memkv_kernel,
        out_shape=[jax.ShapeDtypeStruct((n, CROSS_WIDTH), BF16)] * 2,
        compiler_params=_params(),
    )(mem2, g_mem, w_kv)


def _mid_kernel(ys_ref, ya_ref, x_ref, wout_ref, bout_ref, gc_ref, wq_ref, k_ref, v_ref, wo_ref,
                gf_ref, wr_ref, br_ref, upper_ref, lower_ref,
                h2_ref, u3_ref, gate_ref, lp_ref, tcnt_ref, tbase_ref, run_ref):
    i = pl.program_id(0)

    @pl.when(i == 0)
    def _():
        run_ref[...] = jnp.zeros_like(run_ref)

    u3_parts = []
    for part in range(MID_PARTS):
        rows = slice(part * (TM_MID // MID_PARTS), (part + 1) * (TM_MID // MID_PARTS))
        ycat = jnp.concatenate([ys_ref[rows, :], ya_ref[rows, :]], axis=1).astype(BF16)
        h1 = x_ref[rows, :] + _dot(ycat, wout_ref[...]) + bout_ref[...]

        u2 = _rms(h1, gc_ref[...])
        q = _dot(u2.astype(BF16), wq_ref[...]).astype(BF16)
        outs = []
        for h in range(CROSS_HEADS):
            hs = slice(h * CROSS_HEAD_DIM, (h + 1) * CROSS_HEAD_DIM)
            s = _nt(q[:, hs], k_ref[:, hs]) * (CROSS_HEAD_DIM ** -0.5)
            m = jnp.max(s, axis=-1, keepdims=True)
            p = jnp.exp(s - m)
            outs.append(_dot(p.astype(BF16), v_ref[:, hs]) / jnp.sum(p, axis=-1, keepdims=True))
        o = jnp.concatenate(outs, axis=1).astype(BF16)
        h2 = h1 + _dot(o, wo_ref[...])
        h2_ref[rows, :] = h2
        u3_parts.append(_rms(h2, gf_ref[...]).astype(BF16))
    u3 = jnp.concatenate(u3_parts, axis=0)
    u3_ref[...] = u3

    logits = _nt(wr_ref[...], u3) + br_ref[...]
    eid = lax.broadcasted_iota(I32, logits.shape, 0)
    vals, idxs, hots = [], [], []
    for _ in range(TOP_K):
        m = jnp.max(logits, axis=0, keepdims=True)
        ix = jnp.min(jnp.where(logits == m, eid, N_EXPERTS), axis=0, keepdims=True)
        hot = eid == ix
        logits = jnp.where(hot, -jnp.inf, logits)
        vals.append(m)
        idxs.append(ix)
        hots.append(hot)
    ex = [jnp.exp(v - vals[0]) for v in vals]
    tot = ex[0] + ex[1] + ex[2] + ex[3]
    gate_ref[...] = jnp.concatenate([e / tot for e in ex], axis=0)

    hot_f = [jnp.where(hot, 1.0, 0.0) for hot in hots]
    cnt_k = [jnp.sum(hf, axis=1, keepdims=True) for hf in hot_f]
    tile_cnt = cnt_k[0] + cnt_k[1] + cnt_k[2] + cnt_k[3]
    tile_cnt = jnp.floor((tile_cnt + (SUBLANES - 1)) * (1.0 / SUBLANES)) * SUBLANES
    offs = _dot_exact(lower_ref[...], jnp.broadcast_to(tile_cnt, (N_EXPERTS, LANES)))[:, :1]
    local = []
    for kk in range(TOP_K):
        before = _dot(hot_f[kk].astype(BF16), upper_ref[...])
        local.append(jnp.sum(jnp.where(hots[kk], before + offs, 0.0), axis=0, keepdims=True))
        offs = offs + cnt_k[kk]
    lp_ref[...] = jnp.concatenate(local, axis=0).astype(I32)
    tcnt_ref[0] = jnp.broadcast_to(tile_cnt, (N_EXPERTS, LANES)).astype(I32)
    tbase_ref[0] = jnp.broadcast_to(run_ref[...], (N_EXPERTS, LANES)).astype(I32)
    run_ref[...] = run_ref[...] + tile_cnt


def _mid(y_ssm, y_attn, x2, w_out, b_out, g_cross, w_q, k_mem, v_mem, w_o, g_ffn, w_r_t, b_r, seq, mem_len):
    n = x2.shape[0]
    tiles_per_batch = seq // TM_MID
    row = lambda i: (i, 0)
    col = lambda i: (0, i)
    fixed = lambda i: (0, 0)
    memb = lambda i: (i // tiles_per_batch, 0)
    tile3 = lambda i: (i, 0, 0)
    n_tiles = n // TM_MID
    upper = jnp.asarray(np.triu(np.ones((TM_MID, TM_MID), np.float32), 1)).astype(BF16)
    lower = jnp.asarray(np.tril(np.ones((N_EXPERTS, N_EXPERTS), np.float32), -1))
    return pl.pallas_call(
        _mid_kernel,
        name="mid",
        grid=(n_tiles,),
        in_specs=[
            pl.BlockSpec((TM_MID, SSM_WIDTH), row),
            pl.BlockSpec((TM_MID, ATTN_WIDTH), row),
            pl.BlockSpec((TM_MID, D_MODEL), row),
            pl.BlockSpec((D_MODEL, D_MODEL), fixed),
            pl.BlockSpec((1, D_MODEL), fixed),
            pl.BlockSpec((1, D_MODEL), fixed),
            pl.BlockSpec((D_MODEL, CROSS_WIDTH), fixed),
            pl.BlockSpec((mem_len, CROSS_WIDTH), memb),
            pl.BlockSpec((mem_len, CROSS_WIDTH), memb),
            pl.BlockSpec((CROSS_WIDTH, D_MODEL), fixed),
            pl.BlockSpec((1, D_MODEL), fixed),
            pl.BlockSpec((N_EXPERTS, D_MODEL), fixed),
            pl.BlockSpec((N_EXPERTS, 1), fixed),
            pl.BlockSpec((TM_MID, TM_MID), fixed),
            pl.BlockSpec((N_EXPERTS, N_EXPERTS), fixed),
        ],
        out_specs=[
            pl.BlockSpec((TM_MID, D_MODEL), row),
            pl.BlockSpec((TM_MID, D_MODEL), row),
            pl.BlockSpec((TOP_K, TM_MID), col),
            pl.BlockSpec((TOP_K, TM_MID), col),
            pl.BlockSpec((1, N_EXPERTS, LANES), tile3),
            pl.BlockSpec((1, N_EXPERTS, LANES), tile3),
        ],
        out_shape=[
            jax.ShapeDtypeStruct((n, D_MODEL), F32),
            jax.ShapeDtypeStruct((n, D_MODEL), BF16),
            jax.ShapeDtypeStruct((TOP_K, n), F32),
            jax.ShapeDtypeStruct((TOP_K, n), I32),
            jax.ShapeDtypeStruct((n_tiles, N_EXPERTS, LANES), I32),
            jax.ShapeDtypeStruct((n_tiles, N_EXPERTS, LANES), I32),
        ],
        scratch_shapes=[pltpu.VMEM((N_EXPERTS, 1), F32)],
        compiler_params=_params(("arbitrary",)),
    )(y_ssm, y_attn, x2, w_out, b_out, g_cross, w_q, k_mem, v_mem, w_o, g_ffn, w_r_t, b_r, upper, lower)


ROW_CHUNK = 256
LOCAL_ROWS = -(-(TM_MID * TOP_K + N_EXPERTS * (SUBLANES - 1)) // ROW_CHUNK) * ROW_CHUNK
RUN_BITS = tuple(1 << s for s in range(TM_MID.bit_length() - 1, SUBLANES.bit_length() - 2, -1))
PAD_BITS = tuple(1 << s for s in range((MOE_ROWS - 1).bit_length() - 1, SUBLANES.bit_length() - 2, -1))


U32 = jnp.uint32
PACKED = D_MODEL // 2


def _pack_pairs(v):
    bits = lax.bitcast_convert_type(v, U32)
    return bits[:, PACKED:] | (bits[:, :PACKED] >> 16)


def _unpack_pairs(w):
    lo = lax.bitcast_convert_type(w << 16, F32)
    hi = lax.bitcast_convert_type(w & jnp.uint32(0xFFFF0000), F32)
    return jnp.concatenate([lo, hi], axis=1).astype(BF16)


def _run_copies(n, src_ref, src, dst_ref, dst, sem, bits, wait, advance_src=True):
    for bit in bits:
        step = n & bit

        @pl.when(step != 0)
        def _():
            cp = pltpu.make_async_copy(src_ref.at[pl.ds(pl.multiple_of(src, SUBLANES), bit)],
                                       dst_ref.at[pl.ds(pl.multiple_of(dst, SUBLANES), bit)], sem)
            if wait:
                cp.wait()
            else:
                cp.start()

        if advance_src:
            src = src + step
        dst = dst + step


def _wait_rows(n, src_ref, dst_ref, sem):
    rows = pl.ds(0, pl.multiple_of(n, SUBLANES))
    pltpu.make_async_copy(src_ref.at[rows], dst_ref.at[rows], sem).wait()


def _dispatch_kernel(tcnt_ref, lo_ref, gs_ref, padn_ref, pads_ref, na_ref, u_ref, lp_ref, xs_ref,
                     xloc_ref, zero_ref, sem, zsem):
    i = pl.program_id(0)

    def zero_fill(wait):
        def body(e, carry):
            _run_copies(padn_ref[e], zero_ref, 0, xs_ref, pads_ref[e], zsem, PAD_BITS, wait, advance_src=False)
            return carry
        lax.fori_loop(0, N_EXPERTS, body, 0)

        def tail(b, carry):
            cp = pltpu.make_async_copy(zero_ref, xs_ref.at[pl.ds(pl.multiple_of(b * MOE_ROWS, MOE_ROWS), MOE_ROWS)], zsem)
            if wait:
                cp.wait()
            else:
                cp.start()
            return carry
        lax.fori_loop(na_ref[0], xs_ref.shape[0] // MOE_ROWS, tail, 0)

    @pl.when(i == 0)
    def _():
        zero_ref[...] = jnp.zeros_like(zero_ref)
        zero_fill(False)

    slot = i % 2
    u = u_ref[...]
    lp = lp_ref[...]
    for c in range(LOCAL_ROWS // ROW_CHUNK):
        r = lax.broadcasted_iota(I32, (ROW_CHUNK, TM_MID), 0) + c * ROW_CHUNK
        p = jnp.zeros((ROW_CHUNK, TM_MID), F32)
        for kk in range(TOP_K):
            p = jnp.where(lp[kk:kk + 1, :] == r, 1.0, p)
        xloc_ref[slot, c * ROW_CHUNK:(c + 1) * ROW_CHUNK, :] = _pack_pairs(_dot(p.astype(BF16), u))

    def start_runs(tile):
        s = tile % 2

        def body(e, carry):
            j = tile * N_EXPERTS + e
            _run_copies(tcnt_ref[j], xloc_ref.at[s], lo_ref[j], xs_ref, gs_ref[j], sem.at[s], RUN_BITS, False)
            return carry
        lax.fori_loop(0, N_EXPERTS, body, 0)

    def wait_runs(tile):
        s = tile % 2
        last = tile * N_EXPERTS + N_EXPERTS - 1
        _wait_rows(lo_ref[last] + tcnt_ref[last], xloc_ref.at[s], xs_ref, sem.at[s])

    start_runs(i)

    @pl.when(i > 0)
    def _():
        wait_runs(i - 1)

    @pl.when(i == pl.num_programs(0) - 1)
    def _():
        wait_runs(i)

    @pl.when(i == 0)
    def _():
        zero_fill(True)


def _dispatch(tcnt, lo, gstart, padn, pads, n_active, u3, lp, n_rows):
    n = u3.shape[0]
    return pl.pallas_call(
        _dispatch_kernel,
        name="dispatch",
        grid_spec=pltpu.PrefetchScalarGridSpec(
            num_scalar_prefetch=6,
            grid=(n // TM_MID,),
            in_specs=[
                pl.BlockSpec((TM_MID, D_MODEL), lambda i, *_: (i, 0)),
                pl.BlockSpec((TOP_K, TM_MID), lambda i, *_: (0, i)),
            ],
            out_specs=pl.BlockSpec(memory_space=pl.ANY),
            scratch_shapes=[
                pltpu.VMEM((2, LOCAL_ROWS, PACKED), U32),
                pltpu.VMEM((MOE_ROWS, PACKED), U32),
                pltpu.SemaphoreType.DMA((2,)),
                pltpu.SemaphoreType.DMA,
            ],
        ),
        out_shape=jax.ShapeDtypeStruct((n_rows, PACKED), U32),
        compiler_params=_params(("arbitrary",)),
    )(tcnt, lo, gstart, padn, pads, n_active, u3, lp)


PAIR_COLS = 2 * LANES


def _expert_kernel(be_ref, nx_ref, nv_ref, na_ref, x_ref, b1_ref, b2_ref, perm_ref, w1_hbm, w2_hbm, y_ref,
                   w1s_ref, w2s_ref, w1p_ref, w2b_ref, sem):
    b = pl.program_id(0)
    valid_rows = nv_ref[b]
    active = valid_rows > 0
    expert = be_ref[b]
    new_expert = jnp.logical_or(b == 0, expert != be_ref[jnp.maximum(b - 1, 0)])

    def fetch(e, wait):
        for src, dst, s in ((w1_hbm, w1s_ref, 0), (w2_hbm, w2s_ref, 1)):
            for r in range(0, src.shape[1], WEIGHT_DMA_ROWS):
                rs = pl.ds(r, WEIGHT_DMA_ROWS)
                cp = pltpu.make_async_copy(src.at[e, rs], dst.at[rs], sem.at[s])
                if wait:
                    cp.wait()
                else:
                    cp.start(priority=1)

    @pl.when(b == 0)
    def _():
        fetch(expert, False)

    @pl.when(jnp.logical_and(active, new_expert))
    def _():
        fetch(expert, True)
        for c in range(2 * D_FF // PAIR_COLS):
            cs = slice(c * PAIR_COLS, (c + 1) * PAIR_COLS)
            w1p_ref[:, cs] = _dot(w1s_ref[:, cs].astype(BF16), perm_ref[...]).astype(BF16)
        w2b_ref[...] = w2s_ref[...].astype(BF16)

        @pl.when(nx_ref[b] >= 0)
        def _():
            fetch(nx_ref[b], False)

    for live in range(MOE_PART, MOE_ROWS + 1, MOE_PART):
        @pl.when(jnp.logical_and(valid_rows > live - MOE_PART, valid_rows <= live))
        def _():
            x = _unpack_pairs(x_ref[:live, :])
            hdn = _dot(x, w1p_ref[...]) + b1_ref[0]
            acts = []
            for c in range(2 * D_FF // PAIR_COLS):
                hg = jnp.minimum(hdn[:, c * PAIR_COLS:c * PAIR_COLS + LANES], SWIGLU_LIMIT)
                hl = jnp.clip(hdn[:, c * PAIR_COLS + LANES:(c + 1) * PAIR_COLS], -SWIGLU_LIMIT, SWIGLU_LIMIT)
                acts.append((hg * _sigmoid(SWIGLU_ALPHA * hg) * (hl + 1.0)).astype(BF16))
            y = _dot(jnp.concatenate(acts, axis=1), w2b_ref[...]) + b2_ref[0]
            y_ref[:live, :] = _pack_pairs(y.astype(BF16).astype(F32))
            if live < MOE_ROWS:
                y_ref[live:, :] = jnp.zeros((MOE_ROWS - live, PACKED), U32)

    @pl.when(jnp.logical_not(active))
    def _():
        y_ref[...] = jnp.zeros_like(y_ref)


def _experts(block_exp, next_exp, valid_rows, n_active, xs, w1, b1p, w2, b2):
    n_rows = xs.shape[0]
    n_blocks = n_rows // MOE_ROWS
    xmap = lambda b, be, nx, nv, na: (jnp.maximum(jnp.minimum(b, na[0] - 1), 0), 0)
    emap = lambda b, be, nx, nv, na: (be[b], 0, 0)
    perm = np.zeros((PAIR_COLS, PAIR_COLS), np.float32)
    perm[2 * np.arange(LANES), np.arange(LANES)] = 1.0
    perm[2 * np.arange(LANES) + 1, LANES + np.arange(LANES)] = 1.0
    return pl.pallas_call(
        _expert_kernel,
        name="experts",
        grid_spec=pltpu.PrefetchScalarGridSpec(
            num_scalar_prefetch=4,
            grid=(n_blocks,),
            in_specs=[
                pl.BlockSpec((MOE_ROWS, PACKED), xmap),
                pl.BlockSpec((1, 1, 2 * D_FF), emap),
                pl.BlockSpec((1, 1, D_MODEL), emap),
                pl.BlockSpec((PAIR_COLS, PAIR_COLS), lambda b, *_: (0, 0)),
                pl.BlockSpec(memory_space=pl.ANY),
                pl.BlockSpec(memory_space=pl.ANY),
            ],
            out_specs=pl.BlockSpec((MOE_ROWS, PACKED), lambda b, *_: (b, 0)),
            scratch_shapes=[
                pltpu.VMEM((D_MODEL, 2 * D_FF), F32),
                pltpu.VMEM((D_FF, D_MODEL), F32),
                pltpu.VMEM((D_MODEL, 2 * D_FF), BF16),
                pltpu.VMEM((D_FF, D_MODEL), BF16),
                pltpu.SemaphoreType.DMA((2,)),
            ],
        ),
        out_shape=jax.ShapeDtypeStruct((n_rows, PACKED), U32),
        compiler_params=_params(("arbitrary",)),
    )(block_exp, next_exp, valid_rows, n_active, xs, b1p, b2, jnp.asarray(perm).astype(BF16), w1, w2)


def _combine_kernel(tcnt_ref, lo_ref, gs_ref, h2_ref, gate_ref, lp_ref, g_ref, ys_ref, o_ref, yloc_ref, sem):
    i = pl.program_id(0)

    slot = i % 2

    def start_runs(tile):
        s = tile % 2

        def body(e, carry):
            j = tile * N_EXPERTS + e
            _run_copies(tcnt_ref[j], ys_ref, gs_ref[j], yloc_ref.at[s], lo_ref[j], sem.at[s], RUN_BITS, False)
            return carry
        lax.fori_loop(0, N_EXPERTS, body, 0)

    @pl.when(i == 0)
    def _():
        yloc_ref[...] = jnp.zeros_like(yloc_ref)
        start_runs(i)

    @pl.when(i + 1 < pl.num_programs(0))
    def _():
        start_runs(i + 1)

    gates = gate_ref[...].T
    lp = lp_ref[...].astype(F32).T
    last = i * N_EXPERTS + N_EXPERTS - 1
    _wait_rows(lo_ref[last] + tcnt_ref[last], ys_ref, yloc_ref.at[slot], sem.at[slot])

    h = h2_ref[...]
    for c in range(LOCAL_ROWS // ROW_CHUNK):
        lane = (lax.broadcasted_iota(I32, (TM_MID, ROW_CHUNK), 1) + c * ROW_CHUNK).astype(F32)
        a = jnp.zeros((TM_MID, ROW_CHUNK), F32)
        for kk in range(TOP_K):
            a = jnp.where(lp[:, kk:kk + 1] == lane, gates[:, kk:kk + 1], a)
        h = h + _dot(a.astype(BF16), _unpack_pairs(yloc_ref[slot, c * ROW_CHUNK:(c + 1) * ROW_CHUNK, :]))
    o_ref[...] = _rms(h, g_ref[...])


def _combine(tcnt, lo, gstart, h2, gates, lp, g_final, ys):
    n = h2.shape[0]
    return pl.pallas_call(
        _combine_kernel,
        name="combine",
        grid_spec=pltpu.PrefetchScalarGridSpec(
            num_scalar_prefetch=3,
            grid=(n // TM_MID,),
            in_specs=[
                pl.BlockSpec((TM_MID, D_MODEL), lambda i, *_: (i, 0)),
                pl.BlockSpec((TOP_K, TM_MID), lambda i, *_: (0, i)),
                pl.BlockSpec((TOP_K, TM_MID), lambda i, *_: (0, i)),
                pl.BlockSpec((1, D_MODEL), lambda i, *_: (0, 0)),
                pl.BlockSpec(memory_space=pl.ANY),
            ],
            out_specs=pl.BlockSpec((TM_MID, D_MODEL), lambda i, *_: (i, 0)),
            scratch_shapes=[
                pltpu.VMEM((2, LOCAL_ROWS, PACKED), U32),
                pltpu.SemaphoreType.DMA((2,)),
            ],
        ),
        out_shape=jax.ShapeDtypeStruct((n, D_MODEL), F32),
        compiler_params=_params(("arbitrary",)),
    )(tcnt, lo, gstart, h2, gates, lp, g_final, ys)


def _layer(h, mem2, batch, seq, mem_len, g_mix, w_in, b_qkv, conv_w, conv_b, dt_bias, a_log, d_skip,
           g_ssm_out, attn_sinks, bias, w_out, b_out, g_cross, g_mem, w_q_cross, w_kv_cross, w_o_cross,
           g_ffn, w_router, b_router, w1, b1, w2, b2, g_final):
    n = h.shape[0]
    c1, c2 = SSM_WIDTH + CONV_CH, SSM_WIDTH + CONV_CH + SSM_HEADS
    w_cat = jnp.concatenate(
        [w_in[:, :c1], jnp.pad(w_in[:, c1:c2], ((0, 0), (0, DT_PAD - SSM_HEADS))), w_in[:, c2:]],
        axis=1).astype(BF16)
    z, xbc, dtp, qkv = _inproj(h, g_mix[None, :], w_cat, b_qkv[None, :])

    pad_h = (0, DT_PAD - SSM_HEADS)
    dt_bias_p = jnp.pad(dt_bias, pad_h)[None, :]
    a_head_p = jnp.pad(-jnp.exp(a_log), pad_h)[None, :]
    d_full = jnp.repeat(d_skip, SSM_HEAD_DIM)[None, :]
    per_seq = lambda t: t.reshape(batch, seq, t.shape[-1])
    y_ssm = _ssd(per_seq(xbc), per_seq(z), per_seq(dtp), conv_w, conv_b[None, :], dt_bias_p, a_head_p,
                 d_full, g_ssm_out[None, :]).reshape(n, SSM_WIDTH)
    y_attn = _swa(per_seq(qkv), attn_sinks, bias).reshape(n, ATTN_WIDTH)

    k_mem, v_mem = _memkv(mem2, g_mem[None, :], w_kv_cross.astype(BF16))
    h2, u3, gates, lp, tcnt, tbase = _mid(
        y_ssm, y_attn, h, w_out.astype(BF16), b_out[None, :], g_cross[None, :],
        w_q_cross.astype(BF16), k_mem, v_mem, w_o_cross.astype(BF16), g_ffn[None, :],
        w_router.T.astype(BF16), b_router[:, None], seq, mem_len)

    max_rows = n * TOP_K + (n // TM_MID) * N_EXPERTS * (SUBLANES - 1)
    n_blocks = -(-max_rows // MOE_ROWS) + N_EXPERTS
    tcnt = tcnt[:, :, 0]
    tbase = tbase[:, :, 0]
    counts = tbase[-1] + tcnt[-1]
    padded = (counts + MOE_ROWS - 1) // MOE_ROWS * MOE_ROWS
    padded_end = jnp.cumsum(padded)
    start = (padded_end - padded).astype(I32)
    n_active = (padded_end[-1] // MOE_ROWS).astype(I32)
    blk = jnp.arange(n_blocks, dtype=I32)
    blk_exp = jnp.sum((blk * MOE_ROWS)[:, None] >= padded_end[None, :], axis=1)
    eid = jnp.arange(N_EXPERTS, dtype=I32)
    last_exp = jnp.max(jnp.where(padded > 0, eid, 0))
    blk_exp = jnp.where(blk < n_active, jnp.minimum(blk_exp, N_EXPERTS - 1), last_exp).astype(I32)
    later = jnp.logical_and(padded[None, :] > 0, eid[None, :] > eid[:, None])
    next_of = jnp.min(jnp.where(later, eid[None, :], N_EXPERTS), axis=1)
    next_of = jnp.where(next_of < N_EXPERTS, next_of, -1).astype(I32)
    of_blk = blk_exp[:, None] == eid[None, :]
    blk_next = jnp.sum(jnp.where(of_blk, next_of[None, :], 0), axis=1).astype(I32)
    blk_end = jnp.sum(jnp.where(of_blk, (start + counts)[None, :], 0), axis=1)
    blk_valid = jnp.clip(blk_end - blk * MOE_ROWS, 0, MOE_ROWS)
    blk_valid = jnp.where(blk < n_active, blk_valid, 0).astype(I32)

    run_n = tcnt.reshape(-1)
    run_local = (jnp.cumsum(tcnt, axis=1) - tcnt).reshape(-1)
    run_global = (start[None, :] + tbase).reshape(-1)
    xs = _dispatch(run_n, run_local, run_global, padded - counts, start + counts, n_active[None], u3, lp,
                   n_blocks * MOE_ROWS)

    b1p = b1.reshape(N_EXPERTS, 2 * D_FF // PAIR_COLS, LANES, 2).transpose(0, 1, 3, 2)
    ys = _experts(blk_exp, blk_next, blk_valid, n_active[None], xs, w1,
                  b1p.reshape(N_EXPERTS, 1, 2 * D_FF), w2, b2[:, None, :])
    return _combine(run_n, run_local, run_global, h2, gates, lp, g_final, ys)


def kernel(x, mem, g_mix, w_in, b_qkv, conv_w, conv_b, dt_bias, a_log, d_skip, g_ssm_out, attn_sinks, rel_bias, w_out, b_out, g_cross, g_mem, w_q_cross, w_kv_cross, w_o_cross, g_ffn, w_router, b_router, w1, b1, w2, b2, g_final):
    batch, seq, d = x.shape
    mem_len = mem.shape[1]
    depth = g_mix.shape[0]
    assert depth == 1 and d == D_MODEL and seq % TM_MID == 0
    h = x.reshape(batch * seq, d)
    mem2 = mem.reshape(batch * mem_len, d)
    bias = _bias_table(rel_bias)
    out = _layer(
        h, mem2, batch, seq, mem_len, g_mix[0], w_in[0], b_qkv[0], conv_w[0], conv_b[0], dt_bias[0],
        a_log[0], d_skip[0], g_ssm_out[0], attn_sinks[0], bias, w_out[0], b_out[0], g_cross[0],
        g_mem[0], w_q_cross[0], w_kv_cross[0], w_o_cross[0], g_ffn[0], w_router[0], b_router[0],
        w1[0], b1[0], w2[0], b2[0], g_final[None, :])
    return out.reshape(batch, seq, d)
```

```python
import math

import numpy as np
import jax
import jax.numpy as jnp
from jax import lax
from jax.experimental import pallas as pl
from jax.experimental.pallas import tpu as pltpu

F32 = jnp.float32
BF16 = jnp.bfloat16
I32 = jnp.int32

D_MODEL = 1024
RMS_EPS = 1e-5

SSM_HEADS = 8
SSM_HEAD_DIM = 64
SSM_WIDTH = SSM_HEADS * SSM_HEAD_DIM
SSM_STATE = 128
SSM_GROUPS = 2
HEADS_PER_GROUP = SSM_HEADS // SSM_GROUPS
GROUP_WIDTH = SSM_WIDTH // SSM_GROUPS
CONV_WIDTH = 4
CHUNK = 128
CONV_CH = SSM_WIDTH + 2 * SSM_GROUPS * SSM_STATE

ATTN_HEADS = 8
KV_HEADS = 2
Q_PER_KV = ATTN_HEADS // KV_HEADS
HEAD_DIM = 64
ATTN_WIDTH = ATTN_HEADS * HEAD_DIM
KV_WIDTH = KV_HEADS * HEAD_DIM
WINDOW = 128
BLOCK = WINDOW
QKV_COLS = ATTN_WIDTH + 2 * KV_WIDTH

REL_BUCKETS = 32
REL_MAX_DIST = 128

CROSS_HEADS = 4
CROSS_HEAD_DIM = 128
CROSS_WIDTH = CROSS_HEADS * CROSS_HEAD_DIM

N_EXPERTS = 32
TOP_K = 4
D_FF = D_MODEL
SWIGLU_ALPHA = 1.702
SWIGLU_LIMIT = 7.0

LANES = 128
SUBLANES = 8
DT_PAD = LANES
PROJ_COLS = SSM_WIDTH + CONV_CH + DT_PAD + QKV_COLS
VMEM_LIMIT = 56 * 1024 * 1024

TM_PROJ = 512
TM_MID = 512
MID_PARTS = 1
MOE_ROWS = 512
MOE_PART = 128
WEIGHT_DMA_ROWS = 256


def _nt(a, b):
    return lax.dot_general(a, b, (((1,), (1,)), ((), ())), preferred_element_type=F32)


def _tn(a, b):
    return lax.dot_general(a, b, (((0,), (0,)), ((), ())), preferred_element_type=F32)


def _dot(a, b):
    return jnp.dot(a, b, preferred_element_type=F32)


def _dot_exact(a, b):
    return jnp.dot(a, b, preferred_element_type=F32, precision=lax.Precision.HIGHEST)


def _rms(x, g):
    return x * lax.rsqrt(jnp.mean(x * x, axis=-1, keepdims=True) + RMS_EPS) * g


def _sigmoid(x):
    return 1.0 / (1.0 + jnp.exp(-x))


def _params(sem=None):
    return pltpu.CompilerParams(dimension_semantics=sem, vmem_limit_bytes=VMEM_LIMIT)


def _inproj_kernel(x_ref, g_ref, w_ref, bq_ref, z_ref, xbc_ref, dt_ref, qkv_ref):
    u = _rms(x_ref[...], g_ref[...])
    p = _dot(u.astype(BF16), w_ref[...])
    z_ref[...] = p[:, :SSM_WIDTH]
    xbc_ref[...] = p[:, SSM_WIDTH:SSM_WIDTH + CONV_CH]
    dt_ref[...] = p[:, SSM_WIDTH + CONV_CH:SSM_WIDTH + CONV_CH + DT_PAD]
    qkv_ref[...] = p[:, SSM_WIDTH + CONV_CH + DT_PAD:] + bq_ref[...]


def _inproj(x2, g_mix, w_cat, b_qkv):
    n = x2.shape[0]
    row = lambda i: (i, 0)
    fixed = lambda i: (0, 0)
    return pl.pallas_call(
        _inproj_kernel,
        grid=(n // TM_PROJ,),
        in_specs=[
            pl.BlockSpec((TM_PROJ, D_MODEL), row),
            pl.BlockSpec((1, D_MODEL), fixed),
            pl.BlockSpec((D_MODEL, PROJ_COLS), fixed),
            pl.BlockSpec((1, QKV_COLS), fixed),
        ],
        out_specs=[
            pl.BlockSpec((TM_PROJ, SSM_WIDTH), row),
            pl.BlockSpec((TM_PROJ, CONV_CH), row),
            pl.BlockSpec((TM_PROJ, DT_PAD), row),
            pl.BlockSpec((TM_PROJ, QKV_COLS), row),
        ],
        out_shape=[
            jax.ShapeDtypeStruct((n, SSM_WIDTH), F32),
            jax.ShapeDtypeStruct((n, CONV_CH), F32),
            jax.ShapeDtypeStruct((n, DT_PAD), F32),
            jax.ShapeDtypeStruct((n, QKV_COLS), F32),
        ],
        compiler_params=_params(("arbitrary",)),
    )(x2, g_mix, w_cat, b_qkv)


def _split3(x):
    hi = x.astype(BF16)
    rest = x - hi.astype(F32)
    mid = rest.astype(BF16)
    lo = (rest - mid.astype(F32)).astype(BF16)
    return jnp.concatenate([hi, mid, lo], axis=1)


def _ssd_kernel(xbc_ref, z_ref, dt_ref, cw_ref, cb_ref, dtb_ref, ah_ref, dsk_ref, gout_ref,
                e64_ref, ecat_ref, tri_ref, y_ref, conv_ref, state_ref):
    c = pl.program_id(0)

    @pl.when(c == 0)
    def _():
        conv_ref[:, :SUBLANES, :] = jnp.zeros((conv_ref.shape[0], SUBLANES, CONV_CH), F32)
        state_ref[...] = jnp.zeros_like(state_ref)

    for b in range(xbc_ref.shape[0]):
        _ssd_chunk(b, xbc_ref, z_ref, dt_ref, cw_ref, cb_ref, dtb_ref, ah_ref, dsk_ref, gout_ref,
                   e64_ref, ecat_ref, tri_ref, y_ref, conv_ref, state_ref)


def _ssd_chunk(b, xbc_ref, z_ref, dt_ref, cw_ref, cb_ref, dtb_ref, ah_ref, dsk_ref, gout_ref,
               e64_ref, ecat_ref, tri_ref, y_ref, conv_ref, state_ref):
    u = xbc_ref[b]
    conv_ref[b, SUBLANES:, :] = u
    acc = u * cw_ref[CONV_WIDTH - 1:CONV_WIDTH, :] + cb_ref[...]
    for j in range(1, CONV_WIDTH):
        shifted = conv_ref[b, SUBLANES - j:SUBLANES - j + CHUNK, :]
        acc = acc + shifted * cw_ref[CONV_WIDTH - 1 - j:CONV_WIDTH - j, :]
    conv_ref[b, :SUBLANES, :] = u[CHUNK - SUBLANES:, :]
    xbc = acc * _sigmoid(acc)

    xs = xbc[:, :SSM_WIDTH]
    bm = xbc[:, SSM_WIDTH:SSM_WIDTH + SSM_GROUPS * SSM_STATE]
    cm = xbc[:, SSM_WIDTH + SSM_GROUPS * SSM_STATE:]

    dtr = dt_ref[b] + dtb_ref[...]
    dt = jnp.maximum(dtr, 0.0) + jnp.log1p(jnp.exp(-jnp.abs(dtr)))
    a = dt * ah_ref[...]
    a3 = _split3(a)
    a_rows = jnp.concatenate([a3[:, :DT_PAD], a3[:, DT_PAD:2 * DT_PAD], a3[:, 2 * DT_PAD:]], axis=0)
    cs = _dot(tri_ref[...], a_rows)
    cs_row = cs.T
    dt_full = _dot(_split3(dt), e64_ref[...])
    cs_cat = _dot(_split3(cs), ecat_ref[...])
    cs_wide = cs_cat[:, :SSM_HEADS * CHUNK]
    cs_full = cs_cat[:, SSM_HEADS * CHUNK:]
    cs_last = cs_full[CHUNK - 1:CHUNK, :]

    xdt = xs * dt_full
    xdt_b = xdt.astype(BF16)
    xw_b = (xdt * jnp.exp(cs_last - cs_full)).astype(BF16)
    in_decay = jnp.exp(cs_full)
    chunk_decay = jnp.exp(cs_last)

    li = lax.broadcasted_iota(I32, (CHUNK, CHUNK), 0)
    si = lax.broadcasted_iota(I32, (CHUNK, CHUNK), 1)
    causal = li >= si

    ys = []
    for g in range(SSM_GROUPS):
        bg = bm[:, g * SSM_STATE:(g + 1) * SSM_STATE].astype(BF16)
        cg = cm[:, g * SSM_STATE:(g + 1) * SSM_STATE].astype(BF16)
        cb = _nt(cg, bg)
        yd = []
        for r in range(HEADS_PER_GROUP):
            h = g * HEADS_PER_GROUP + r
            diff = cs_wide[:, h * CHUNK:(h + 1) * CHUNK] - cs_row[h:h + 1, :]
            m = cb * jnp.exp(jnp.where(causal, diff, -jnp.inf))
            yd.append(_dot(m.astype(BF16), xdt_b[:, h * SSM_HEAD_DIM:(h + 1) * SSM_HEAD_DIM]))
        y_diag = jnp.concatenate(yd, axis=1)
        gs = slice(g * GROUP_WIDTH, (g + 1) * GROUP_WIDTH)
        st = state_ref[b, g]
        y_off = _dot(cg, st.astype(BF16)) * in_decay[:, gs]
        state_ref[b, g] = st * chunk_decay[:, gs] + _tn(bg, xw_b[:, gs])
        ys.append(y_diag + y_off)
    y = jnp.concatenate(ys, axis=1) + dsk_ref[...] * xs

    zz = z_ref[b]
    y = y * (zz * _sigmoid(zz))
    outs = []
    for g in range(SSM_GROUPS):
        yg = y[:, g * GROUP_WIDTH:(g + 1) * GROUP_WIDTH]
        outs.append(yg * lax.rsqrt(jnp.mean(yg * yg, axis=-1, keepdims=True) + RMS_EPS))
    y_ref[b] = jnp.concatenate(outs, axis=1) * gout_ref[...]


def _ssd(xbc, z, dtp, conv_w, conv_b, dt_bias_p, a_head_p, d_full, g_out):
    batch, seq, _ = xbc.shape
    chunk = lambda c: (0, c, 0)
    fixed = lambda c: (0, 0)
    head = np.arange(SSM_WIDTH) // SSM_HEAD_DIM
    e64 = (np.arange(DT_PAD)[:, None] == head[None, :]).astype(np.float32)
    head_w = np.arange(SSM_HEADS * CHUNK) // CHUNK
    e128 = (np.arange(DT_PAD)[:, None] == head_w[None, :]).astype(np.float32)
    e64_3 = jnp.asarray(np.tile(e64, (3, 1))).astype(BF16)
    ecat_3 = jnp.asarray(np.tile(np.concatenate([e128, e64], axis=1), (3, 1))).astype(BF16)
    tri_3 = jnp.asarray(np.tile(np.tril(np.ones((CHUNK, CHUNK), np.float32)), (1, 3))).astype(BF16)
    return pl.pallas_call(
        _ssd_kernel,
        name="ssd",
        grid=(seq // CHUNK,),
        in_specs=[
            pl.BlockSpec((batch, CHUNK, CONV_CH), chunk),
            pl.BlockSpec((batch, CHUNK, SSM_WIDTH), chunk),
            pl.BlockSpec((batch, CHUNK, DT_PAD), chunk),
            pl.BlockSpec((CONV_WIDTH, CONV_CH), fixed),
            pl.BlockSpec((1, CONV_CH), fixed),
            pl.BlockSpec((1, DT_PAD), fixed),
            pl.BlockSpec((1, DT_PAD), fixed),
            pl.BlockSpec((1, SSM_WIDTH), fixed),
            pl.BlockSpec((1, SSM_WIDTH), fixed),
            pl.BlockSpec((3 * DT_PAD, SSM_WIDTH), fixed),
            pl.BlockSpec((3 * DT_PAD, SSM_HEADS * CHUNK + SSM_WIDTH), fixed),
            pl.BlockSpec((CHUNK, 3 * CHUNK), fixed),
        ],
        out_specs=pl.BlockSpec((batch, CHUNK, SSM_WIDTH), chunk),
        out_shape=jax.ShapeDtypeStruct((batch, seq, SSM_WIDTH), F32),
        scratch_shapes=[
            pltpu.VMEM((batch, SUBLANES + CHUNK, CONV_CH), F32),
            pltpu.VMEM((batch, SSM_GROUPS, SSM_STATE, GROUP_WIDTH), F32),
        ],
        compiler_params=_params(("arbitrary",)),
    )(xbc, z, dtp, conv_w, conv_b, dt_bias_p, a_head_p, d_full, g_out, e64_3, ecat_3, tri_3)


def _t5_bucket_table():
    q_pos = np.arange(BLOCK)[:, None]
    k_pos = np.arange(2 * BLOCK)[None, :] - BLOCK
    dist = q_pos - k_pos
    d = np.maximum(dist, 0)
    max_exact = REL_BUCKETS // 2
    ratio = np.maximum(d, max_exact).astype(np.float32) / np.float32(max_exact)
    large = max_exact + (np.log(ratio) / np.float32(math.log(REL_MAX_DIST / max_exact))
                         * np.float32(REL_BUCKETS - max_exact)).astype(np.int32)
    large = np.minimum(large, REL_BUCKETS - 1)
    bucket = np.where(d < max_exact, d, large)
    in_window = (dist >= 0) & (dist < WINDOW)
    return np.where(in_window, bucket, -1).astype(np.int32)


def _bias_kernel(rb_ref, bucket_ref, out_ref):
    bucket = bucket_ref[...]
    for h in range(ATTN_HEADS):
        acc = jnp.zeros(bucket.shape, F32)
        for b in range(REL_BUCKETS):
            acc = jnp.where(bucket == b, rb_ref[b, h], acc)
        out_ref[h] = jnp.where(bucket >= 0, acc, -jnp.inf)


def _bias_table(rel_bias):
    bucket = jnp.asarray(_t5_bucket_table())
    return pl.pallas_call(
        _bias_kernel,
        in_specs=[
            pl.BlockSpec(memory_space=pltpu.SMEM),
            pl.BlockSpec(memory_space=pltpu.VMEM),
        ],
        out_specs=pl.BlockSpec(memory_space=pltpu.VMEM),
        out_shape=jax.ShapeDtypeStruct((ATTN_HEADS, BLOCK, 2 * BLOCK), F32),
    )(rel_bias, bucket)


def _swa_kernel(sink_ref, q_ref, k_ref, v_ref, kp_ref, vp_ref, bias_ref, o_ref):
    i = pl.program_id(0)
    col = lax.broadcasted_iota(I32, (BLOCK, 2 * BLOCK), 1)
    key_ok = jnp.logical_or(col >= BLOCK, i > 0)
    for b in range(q_ref.shape[0]):
        q = (q_ref[b] * (HEAD_DIM ** -0.5)).astype(BF16)
        outs = []
        for g in range(KV_HEADS):
            gs = slice(g * HEAD_DIM, (g + 1) * HEAD_DIM)
            k2 = jnp.concatenate([kp_ref[b, :, gs], k_ref[b, :, gs]], axis=0).astype(BF16)
            v2 = jnp.concatenate([vp_ref[b, :, gs], v_ref[b, :, gs]], axis=0).astype(BF16)
            for r in range(Q_PER_KV):
                h = g * Q_PER_KV + r
                s = _nt(q[:, h * HEAD_DIM:(h + 1) * HEAD_DIM], k2) + bias_ref[h]
                s = jnp.where(key_ok, s, -jnp.inf)
                sink = sink_ref[h]
                m = jnp.maximum(jnp.max(s, axis=-1, keepdims=True), sink)
                p = jnp.exp(s - m)
                denom = jnp.sum(p, axis=-1, keepdims=True) + jnp.exp(sink - m)
                outs.append(_dot(p.astype(BF16), v2) / denom)
        o_ref[b] = jnp.concatenate(outs, axis=1)


def _swa(qkv, sinks, bias):
    batch, seq, _ = qkv.shape
    kcol = ATTN_WIDTH // KV_WIDTH
    prev = lambda i: jnp.maximum(i - 1, 0)
    return pl.pallas_call(
        _swa_kernel,
        name="swa",
        grid_spec=pltpu.PrefetchScalarGridSpec(
            num_scalar_prefetch=1,
            grid=(seq // BLOCK,),
            in_specs=[
                pl.BlockSpec((batch, BLOCK, ATTN_WIDTH), lambda i, s: (0, i, 0)),
                pl.BlockSpec((batch, BLOCK, KV_WIDTH), lambda i, s: (0, i, kcol)),
                pl.BlockSpec((batch, BLOCK, KV_WIDTH), lambda i, s: (0, i, kcol + 1)),
                pl.BlockSpec((batch, BLOCK, KV_WIDTH), lambda i, s: (0, prev(i), kcol)),
                pl.BlockSpec((batch, BLOCK, KV_WIDTH), lambda i, s: (0, prev(i), kcol + 1)),
                pl.BlockSpec((ATTN_HEADS, BLOCK, 2 * BLOCK), lambda i, s: (0, 0, 0)),
            ],
            out_specs=pl.BlockSpec((batch, BLOCK, ATTN_WIDTH), lambda i, s: (0, i, 0)),
        ),
        out_shape=jax.ShapeDtypeStruct((batch, seq, ATTN_WIDTH), F32),
        compiler_params=_params(("arbitrary",)),
    )(sinks, qkv, qkv, qkv, qkv, qkv, bias)


def _memkv_kernel(m_ref, g_ref, w_ref, k_ref, v_ref):
    u = _rms(m_ref[...], g_ref[...])
    kv = _dot(u.astype(BF16), w_ref[...])
    k_ref[...] = kv[:, :CROSS_WIDTH].astype(BF16)
    v_ref[...] = kv[:, CROSS_WIDTH:].astype(BF16)


def _memkv(mem2, g_mem, w_kv):
    n = mem2.shape[0]
    return pl.pallas_call(
        _memkv_kernel,
        out_shape=[jax.ShapeDtypeStruct((n, CROSS_WIDTH), BF16)] * 2,
        compiler_params=_params(),
    )(mem2, g_mem, w_kv)


def _mid_kernel(ys_ref, ya_ref, x_ref, wout_ref, bout_ref, gc_ref, wq_ref, k_ref, v_ref, wo_ref,
                gf_ref, wr_ref, br_ref, upper_ref, lower_ref,
                h2_ref, u3_ref, gate_ref, lp_ref, tcnt_ref, tbase_ref, run_ref):
    i = pl.program_id(0)

    @pl.when(i == 0)
    def _():
        run_ref[...] = jnp.zeros_like(run_ref)

    u3_parts = []
    for part in range(MID_PARTS):
        rows = slice(part * (TM_MID // MID_PARTS), (part + 1) * (TM_MID // MID_PARTS))
        ycat = jnp.concatenate([ys_ref[rows, :], ya_ref[rows, :]], axis=1).astype(BF16)
        h1 = x_ref[rows, :] + _dot(ycat, wout_ref[...]) + bout_ref[...]

        u2 = _rms(h1, gc_ref[...])
        q = _dot(u2.astype(BF16), wq_ref[...]).astype(BF16)
        outs = []
        for h in range(CROSS_HEADS):
            hs = slice(h * CROSS_HEAD_DIM, (h + 1) * CROSS_HEAD_DIM)
            s = _nt(q[:, hs], k_ref[:, hs]) * (CROSS_HEAD_DIM ** -0.5)
            m = jnp.max(s, axis=-1, keepdims=True)
            p = jnp.exp(s - m)
            outs.append(_dot(p.astype(BF16), v_ref[:, hs]) / jnp.sum(p, axis=-1, keepdims=True))
        o = jnp.concatenate(outs, axis=1).astype(BF16)
        h2 = h1 + _dot(o, wo_ref[...])
        h2_ref[rows, :] = h2
        u3_parts.append(_rms(h2, gf_ref[...]).astype(BF16))
    u3 = jnp.concatenate(u3_parts, axis=0)
    u3_ref[...] = u3

    logits = _nt(wr_ref[...], u3) + br_ref[...]
    eid = lax.broadcasted_iota(I32, logits.shape, 0)
    vals, idxs, hots = [], [], []
    for _ in range(TOP_K):
        m = jnp.max(logits, axis=0, keepdims=True)
        ix = jnp.min(jnp.where(logits == m, eid, N_EXPERTS), axis=0, keepdims=True)
        hot = eid == ix
        logits = jnp.where(hot, -jnp.inf, logits)
        vals.append(m)
        idxs.append(ix)
        hots.append(hot)
    ex = [jnp.exp(v - vals[0]) for v in vals]
    tot = ex[0] + ex[1] + ex[2] + ex[3]
    gate_ref[...] = jnp.concatenate([e / tot for e in ex], axis=0)

    hot_f = [jnp.where(hot, 1.0, 0.0) for hot in hots]
    cnt_k = [jnp.sum(hf, axis=1, keepdims=True) for hf in hot_f]
    tile_cnt = cnt_k[0] + cnt_k[1] + cnt_k[2] + cnt_k[3]
    tile_cnt = jnp.floor((tile_cnt + (SUBLANES - 1)) * (1.0 / SUBLANES)) * SUBLANES
    offs = _dot_exact(lower_ref[...], jnp.broadcast_to(tile_cnt, (N_EXPERTS, LANES)))[:, :1]
    local = []
    for kk in range(TOP_K):
        before = _dot(hot_f[kk].astype(BF16), upper_ref[...])
        local.append(jnp.sum(jnp.where(hots[kk], before + offs, 0.0), axis=0, keepdims=True))
        offs = offs + cnt_k[kk]
    lp_ref[...] = jnp.concatenate(local, axis=0).astype(I32)
    tcnt_ref[0] = jnp.broadcast_to(tile_cnt, (N_EXPERTS, LANES)).astype(I32)
    tbase_ref[0] = jnp.broadcast_to(run_ref[...], (N_EXPERTS, LANES)).astype(I32)
    run_ref[...] = run_ref[...] + tile_cnt


def _mid(y_ssm, y_attn, x2, w_out, b_out, g_cross, w_q, k_mem, v_mem, w_o, g_ffn, w_r_t, b_r, seq, mem_len):
    n = x2.shape[0]
    tiles_per_batch = seq // TM_MID
    row = lambda i: (i, 0)
    col = lambda i: (0, i)
    fixed = lambda i: (0, 0)
    memb = lambda i: (i // tiles_per_batch, 0)
    tile3 = lambda i: (i, 0, 0)
    n_tiles = n // TM_MID
    upper = jnp.asarray(np.triu(np.ones((TM_MID, TM_MID), np.float32), 1)).astype(BF16)
    lower = jnp.asarray(np.tril(np.ones((N_EXPERTS, N_EXPERTS), np.float32), -1))
    return pl.pallas_call(
        _mid_kernel,
        name="mid",
        grid=(n_tiles,),
        in_specs=[
            pl.BlockSpec((TM_MID, SSM_WIDTH), row),
            pl.BlockSpec((TM_MID, ATTN_WIDTH), row),
            pl.BlockSpec((TM_MID, D_MODEL), row),
            pl.BlockSpec((D_MODEL, D_MODEL), fixed),
            pl.BlockSpec((1, D_MODEL), fixed),
            pl.BlockSpec((1, D_MODEL), fixed),
            pl.BlockSpec((D_MODEL, CROSS_WIDTH), fixed),
            pl.BlockSpec((mem_len, CROSS_WIDTH), memb),
            pl.BlockSpec((mem_len, CROSS_WIDTH), memb),
            pl.BlockSpec((CROSS_WIDTH, D_MODEL), fixed),
            pl.BlockSpec((1, D_MODEL), fixed),
            pl.BlockSpec((N_EXPERTS, D_MODEL), fixed),
            pl.BlockSpec((N_EXPERTS, 1), fixed),
            pl.BlockSpec((TM_MID, TM_MID), fixed),
            pl.BlockSpec((N_EXPERTS, N_EXPERTS), fixed),
        ],
        out_specs=[
            pl.BlockSpec((TM_MID, D_MODEL), row),
            pl.BlockSpec((TM_MID, D_MODEL), row),
            pl.BlockSpec((TOP_K, TM_MID), col),
            pl.BlockSpec((TOP_K, TM_MID), col),
            pl.BlockSpec((1, N_EXPERTS, LANES), tile3),
            pl.BlockSpec((1, N_EXPERTS, LANES), tile3),
        ],
        out_shape=[
            jax.ShapeDtypeStruct((n, D_MODEL), F32),
            jax.ShapeDtypeStruct((n, D_MODEL), BF16),
            jax.ShapeDtypeStruct((TOP_K, n), F32),
            jax.ShapeDtypeStruct((TOP_K, n), I32),
            jax.ShapeDtypeStruct((n_tiles, N_EXPERTS, LANES), I32),
            jax.ShapeDtypeStruct((n_tiles, N_EXPERTS, LANES), I32),
        ],
        scratch_shapes=[pltpu.VMEM((N_EXPERTS, 1), F32)],
        compiler_params=_params(("arbitrary",)),
    )(y_ssm, y_attn, x2, w_out, b_out, g_cross, w_q, k_mem, v_mem, w_o, g_ffn, w_r_t, b_r, upper, lower)


ROW_CHUNK = 256
LOCAL_ROWS = -(-(TM_MID * TOP_K + N_EXPERTS * (SUBLANES - 1)) // ROW_CHUNK) * ROW_CHUNK
RUN_BITS = tuple(1 << s for s in range(TM_MID.bit_length() - 1, SUBLANES.bit_length() - 2, -1))
PAD_BITS = tuple(1 << s for s in range((MOE_ROWS - 1).bit_length() - 1, SUBLANES.bit_length() - 2, -1))


U32 = jnp.uint32
PACKED = D_MODEL // 2


def _pack_pairs(v):
    bits = lax.bitcast_convert_type(v, U32)
    return bits[:, PACKED:] | (bits[:, :PACKED] >> 16)


def _unpack_pairs(w):
    lo = lax.bitcast_convert_type(w << 16, F32)
    hi = lax.bitcast_convert_type(w & jnp.uint32(0xFFFF0000), F32)
    return jnp.concatenate([lo, hi], axis=1).astype(BF16)


def _run_copies(n, src_ref, src, dst_ref, dst, sem, bits, wait, advance_src=True):
    for bit in bits:
        step = n & bit

        @pl.when(step != 0)
        def _():
            cp = pltpu.make_async_copy(src_ref.at[pl.ds(pl.multiple_of(src, SUBLANES), bit)],
                                       dst_ref.at[pl.ds(pl.multiple_of(dst, SUBLANES), bit)], sem)
            if wait:
                cp.wait()
            else:
                cp.start()

        if advance_src:
            src = src + step
        dst = dst + step


def _wait_rows(n, src_ref, dst_ref, sem):
    rows = pl.ds(0, pl.multiple_of(n, SUBLANES))
    pltpu.make_async_copy(src_ref.at[rows], dst_ref.at[rows], sem).wait()


def _dispatch_kernel(tcnt_ref, lo_ref, gs_ref, padn_ref, pads_ref, na_ref, u_ref, lp_ref, xs_ref,
                     xloc_ref, zero_ref, sem, zsem):
    i = pl.program_id(0)

    def zero_fill(wait):
        def body(e, carry):
            _run_copies(padn_ref[e], zero_ref, 0, xs_ref, pads_ref[e], zsem, PAD_BITS, wait, advance_src=False)
            return carry
        lax.fori_loop(0, N_EXPERTS, body, 0)

        def tail(b, carry):
            cp = pltpu.make_async_copy(zero_ref, xs_ref.at[pl.ds(pl.multiple_of(b * MOE_ROWS, MOE_ROWS), MOE_ROWS)], zsem)
            if wait:
                cp.wait()
            else:
                cp.start()
            return carry
        lax.fori_loop(na_ref[0], xs_ref.shape[0] // MOE_ROWS, tail, 0)

    @pl.when(i == 0)
    def _():
        zero_ref[...] = jnp.zeros_like(zero_ref)
        zero_fill(False)

    slot = i % 2
    u = u_ref[...]
    lp = lp_ref[...]
    for c in range(LOCAL_ROWS // ROW_CHUNK):
        r = lax.broadcasted_iota(I32, (ROW_CHUNK, TM_MID), 0) + c * ROW_CHUNK
        p = jnp.zeros((ROW_CHUNK, TM_MID), F32)
        for kk in range(TOP_K):
            p = jnp.where(lp[kk:kk + 1, :] == r, 1.0, p)
        xloc_ref[slot, c * ROW_CHUNK:(c + 1) * ROW_CHUNK, :] = _pack_pairs(_dot(p.astype(BF16), u))

    def start_runs(tile):
        s = tile % 2

        def body(e, carry):
            j = tile * N_EXPERTS + e
            _run_copies(tcnt_ref[j], xloc_ref.at[s], lo_ref[j], xs_ref, gs_ref[j], sem.at[s], RUN_BITS, False)
            return carry
        lax.fori_loop(0, N_EXPERTS, body, 0)

    def wait_runs(tile):
        s = tile % 2
        last = tile * N_EXPERTS + N_EXPERTS - 1
        _wait_rows(lo_ref[last] + tcnt_ref[last], xloc_ref.at[s], xs_ref, sem.at[s])

    start_runs(i)

    @pl.when(i > 0)
    def _():
        wait_runs(i - 1)

    @pl.when(i == pl.num_programs(0) - 1)
    def _():
        wait_runs(i)

    @pl.when(i == 0)
    def _():
        zero_fill(True)


def _dispatch(tcnt, lo, gstart, padn, pads, n_active, u3, lp, n_rows):
    n = u3.shape[0]
    return pl.pallas_call(
        _dispatch_kernel,
        name="dispatch",
        grid_spec=pltpu.PrefetchScalarGridSpec(
            num_scalar_prefetch=6,
            grid=(n // TM_MID,),
            in_specs=[
                pl.BlockSpec((TM_MID, D_MODEL), lambda i, *_: (i, 0)),
                pl.BlockSpec((TOP_K, TM_MID), lambda i, *_: (0, i)),
            ],
            out_specs=pl.BlockSpec(memory_space=pl.ANY),
            scratch_shapes=[
                pltpu.VMEM((2, LOCAL_ROWS, PACKED), U32),
                pltpu.VMEM((MOE_ROWS, PACKED), U32),
                pltpu.SemaphoreType.DMA((2,)),
                pltpu.SemaphoreType.DMA,
            ],
        ),
        out_shape=jax.ShapeDtypeStruct((n_rows, PACKED), U32),
        compiler_params=_params(("arbitrary",)),
    )(tcnt, lo, gstart, padn, pads, n_active, u3, lp)


PAIR_COLS = 2 * LANES


def _expert_kernel(be_ref, nx_ref, nv_ref, na_ref, x_ref, b1_ref, b2_ref, perm_ref, w1_hbm, w2_hbm, y_ref,
                   w1s_ref, w2s_ref, w1p_ref, w2b_ref, sem):
    b = pl.program_id(0)
    valid_rows = nv_ref[b]
    active = valid_rows > 0
    expert = be_ref[b]
    new_expert = jnp.logical_or(b == 0, expert != be_ref[jnp.maximum(b - 1, 0)])

    def fetch(e, wait):
        for src, dst, s in ((w1_hbm, w1s_ref, 0), (w2_hbm, w2s_ref, 1)):
            for r in range(0, src.shape[1], WEIGHT_DMA_ROWS):
                rs = pl.ds(r, WEIGHT_DMA_ROWS)
                cp = pltpu.make_async_copy(src.at[e, rs], dst.at[rs], sem.at[s])
                if wait:
                    cp.wait()
                else:
                    cp.start(priority=1)

    @pl.when(b == 0)
    def _():
        fetch(expert, False)

    @pl.when(jnp.logical_and(active, new_expert))
    def _():
        fetch(expert, True)
        for c in range(2 * D_FF // PAIR_COLS):
            cs = slice(c * PAIR_COLS, (c + 1) * PAIR_COLS)
            w1p_ref[:, cs] = _dot(w1s_ref[:, cs].astype(BF16), perm_ref[...]).astype(BF16)
        w2b_ref[...] = w2s_ref[...].astype(BF16)

        @pl.when(nx_ref[b] >= 0)
        def _():
            fetch(nx_ref[b], False)

    for live in range(MOE_PART, MOE_ROWS + 1, MOE_PART):
        @pl.when(jnp.logical_and(valid_rows > live - MOE_PART, valid_rows <= live))
        def _():
            x = _unpack_pairs(x_ref[:live, :])
            hdn = _dot(x, w1p_ref[...]) + b1_ref[0]
            acts = []
            for c in range(2 * D_FF // PAIR_COLS):
                hg = jnp.minimum(hdn[:, c * PAIR_COLS:c * PAIR_COLS + LANES], SWIGLU_LIMIT)
                hl = jnp.clip(hdn[:, c * PAIR_COLS + LANES:(c + 1) * PAIR_COLS], -SWIGLU_LIMIT, SWIGLU_LIMIT)
                acts.append((hg * _sigmoid(SWIGLU_ALPHA * hg) * (hl + 1.0)).astype(BF16))
            y = _dot(jnp.concatenate(acts, axis=1), w2b_ref[...]) + b2_ref[0]
            y_ref[:live, :] = _pack_pairs(y.astype(BF16).astype(F32))
            if live < MOE_ROWS:
                y_ref[live:, :] = jnp.zeros((MOE_ROWS - live, PACKED), U32)

    @pl.when(jnp.logical_not(active))
    def _():
        y_ref[...] = jnp.zeros_like(y_ref)


def _experts(block_exp, next_exp, valid_rows, n_active, xs, w1, b1p, w2, b2):
    n_rows = xs.shape[0]
    n_blocks = n_rows // MOE_ROWS
    xmap = lambda b, be, nx, nv, na: (jnp.maximum(jnp.minimum(b, na[0] - 1), 0), 0)
    emap = lambda b, be, nx, nv, na: (be[b], 0, 0)
    perm = np.zeros((PAIR_COLS, PAIR_COLS), np.float32)
    perm[2 * np.arange(LANES), np.arange(LANES)] = 1.0
    perm[2 * np.arange(LANES) + 1, LANES + np.arange(LANES)] = 1.0
    return pl.pallas_call(
        _expert_kernel,
        name="experts",
        grid_spec=pltpu.PrefetchScalarGridSpec(
            num_scalar_prefetch=4,
            grid=(n_blocks,),
            in_specs=[
                pl.BlockSpec((MOE_ROWS, PACKED), xmap),
                pl.BlockSpec((1, 1, 2 * D_FF), emap),
                pl.BlockSpec((1, 1, D_MODEL), emap),
                pl.BlockSpec((PAIR_COLS, PAIR_COLS), lambda b, *_: (0, 0)),
                pl.BlockSpec(memory_space=pl.ANY),
                pl.BlockSpec(memory_space=pl.ANY),
            ],
            out_specs=pl.BlockSpec((MOE_ROWS, PACKED), lambda b, *_: (b, 0)),
            scratch_shapes=[
                pltpu.VMEM((D_MODEL, 2 * D_FF), F32),
                pltpu.VMEM((D_FF, D_MODEL), F32),
                pltpu.VMEM((D_MODEL, 2 * D_FF), BF16),
                pltpu.VMEM((D_FF, D_MODEL), BF16),
                pltpu.SemaphoreType.DMA((2,)),
            ],
        ),
        out_shape=jax.ShapeDtypeStruct((n_rows, PACKED), U32),
        compiler_params=_params(("arbitrary",)),
    )(block_exp, next_exp, valid_rows, n_active, xs, b1p, b2, jnp.asarray(perm).astype(BF16), w1, w2)


def _combine_kernel(tcnt_ref, lo_ref, gs_ref, h2_ref, gate_ref, lp_ref, g_ref, ys_ref, o_ref, yloc_ref, sem):
    i = pl.program_id(0)

    slot = i % 2

    def start_runs(tile):
        s = tile % 2

        def body(e, carry):
            j = tile * N_EXPERTS + e
            _run_copies(tcnt_ref[j], ys_ref, gs_ref[j], yloc_ref.at[s], lo_ref[j], sem.at[s], RUN_BITS, False)
            return carry
        lax.fori_loop(0, N_EXPERTS, body, 0)

    @pl.when(i == 0)
    def _():
        yloc_ref[...] = jnp.zeros_like(yloc_ref)
        start_runs(i)

    @pl.when(i + 1 < pl.num_programs(0))
    def _():
        start_runs(i + 1)

    gates = gate_ref[...]
    lp = lp_ref[...]
    last = i * N_EXPERTS + N_EXPERTS - 1
    _wait_rows(lo_ref[last] + tcnt_ref[last], ys_ref, yloc_ref.at[slot], sem.at[slot])

    h = h2_ref[...]
    for c in range(LOCAL_ROWS // ROW_CHUNK):
        r = lax.broadcasted_iota(I32, (ROW_CHUNK, TM_MID), 0) + c * ROW_CHUNK
        a = jnp.zeros((ROW_CHUNK, TM_MID), F32)
        for kk in range(TOP_K):
            a = jnp.where(lp[kk:kk + 1, :] == r, gates[kk:kk + 1, :], a)
        h = h + _tn(a.astype(BF16), _unpack_pairs(yloc_ref[slot, c * ROW_CHUNK:(c + 1) * ROW_CHUNK, :]))
    o_ref[...] = _rms(h, g_ref[...])


def _combine(tcnt, lo, gstart, h2, gates, lp, g_final, ys):
    n = h2.shape[0]
    return pl.pallas_call(
        _combine_kernel,
        name="combine",
        grid_spec=pltpu.PrefetchScalarGridSpec(
            num_scalar_prefetch=3,
            grid=(n // TM_MID,),
            in_specs=[
                pl.BlockSpec((TM_MID, D_MODEL), lambda i, *_: (i, 0)),
                pl.BlockSpec((TOP_K, TM_MID), lambda i, *_: (0, i)),
                pl.BlockSpec((TOP_K, TM_MID), lambda i, *_: (0, i)),
                pl.BlockSpec((1, D_MODEL), lambda i, *_: (0, 0)),
                pl.BlockSpec(memory_space=pl.ANY),
            ],
            out_specs=pl.BlockSpec((TM_MID, D_MODEL), lambda i, *_: (i, 0)),
            scratch_shapes=[
                pltpu.VMEM((2, LOCAL_ROWS, PACKED), U32),
                pltpu.SemaphoreType.DMA((2,)),
            ],
        ),
        out_shape=jax.ShapeDtypeStruct((n, D_MODEL), F32),
        compiler_params=_params(("arbitrary",)),
    )(tcnt, lo, gstart, h2, gates, lp, g_final, ys)


def _layer(h, mem2, batch, seq, mem_len, g_mix, w_in, b_qkv, conv_w, conv_b, dt_bias, a_log, d_skip,
           g_ssm_out, attn_sinks, bias, w_out, b_out, g_cross, g_mem, w_q_cross, w_kv_cross, w_o_cross,
           g_ffn, w_router, b_router, w1, b1, w2, b2, g_final):
    n = h.shape[0]
    c1, c2 = SSM_WIDTH + CONV_CH, SSM_WIDTH + CONV_CH + SSM_HEADS
    w_cat = jnp.concatenate(
        [w_in[:, :c1], jnp.pad(w_in[:, c1:c2], ((0, 0), (0, DT_PAD - SSM_HEADS))), w_in[:, c2:]],
        axis=1).astype(BF16)
    z, xbc, dtp, qkv = _inproj(h, g_mix[None, :], w_cat, b_qkv[None, :])

    pad_h = (0, DT_PAD - SSM_HEADS)
    dt_bias_p = jnp.pad(dt_bias, pad_h)[None, :]
    a_head_p = jnp.pad(-jnp.exp(a_log), pad_h)[None, :]
    d_full = jnp.repeat(d_skip, SSM_HEAD_DIM)[None, :]
    per_seq = lambda t: t.reshape(batch, seq, t.shape[-1])
    y_ssm = _ssd(per_seq(xbc), per_seq(z), per_seq(dtp), conv_w, conv_b[None, :], dt_bias_p, a_head_p,
                 d_full, g_ssm_out[None, :]).reshape(n, SSM_WIDTH)
    y_attn = _swa(per_seq(qkv), attn_sinks, bias).reshape(n, ATTN_WIDTH)

    k_mem, v_mem = _memkv(mem2, g_mem[None, :], w_kv_cross.astype(BF16))
    h2, u3, gates, lp, tcnt, tbase = _mid(
        y_ssm, y_attn, h, w_out.astype(BF16), b_out[None, :], g_cross[None, :],
        w_q_cross.astype(BF16), k_mem, v_mem, w_o_cross.astype(BF16), g_ffn[None, :],
        w_router.T.astype(BF16), b_router[:, None], seq, mem_len)

    max_rows = n * TOP_K + (n // TM_MID) * N_EXPERTS * (SUBLANES - 1)
    n_blocks = -(-max_rows // MOE_ROWS) + N_EXPERTS
    tcnt = tcnt[:, :, 0]
    tbase = tbase[:, :, 0]
    counts = tbase[-1] + tcnt[-1]
    padded = (counts + MOE_ROWS - 1) // MOE_ROWS * MOE_ROWS
    padded_end = jnp.cumsum(padded)
    start = (padded_end - padded).astype(I32)
    n_active = (padded_end[-1] // MOE_ROWS).astype(I32)
    blk = jnp.arange(n_blocks, dtype=I32)
    blk_exp = jnp.sum((blk * MOE_ROWS)[:, None] >= padded_end[None, :], axis=1)
    eid = jnp.arange(N_EXPERTS, dtype=I32)
    last_exp = jnp.max(jnp.where(padded > 0, eid, 0))
    blk_exp = jnp.where(blk < n_active, jnp.minimum(blk_exp, N_EXPERTS - 1), last_exp).astype(I32)
    later = jnp.logical_and(padded[None, :] > 0, eid[None, :] > eid[:, None])
    next_of = jnp.min(jnp.where(later, eid[None, :], N_EXPERTS), axis=1)
    next_of = jnp.where(next_of < N_EXPERTS, next_of, -1).astype(I32)
    of_blk = blk_exp[:, None] == eid[None, :]
    blk_next = jnp.sum(jnp.where(of_blk, next_of[None, :], 0), axis=1).astype(I32)
    blk_end = jnp.sum(jnp.where(of_blk, (start + counts)[None, :], 0), axis=1)
    blk_valid = jnp.clip(blk_end - blk * MOE_ROWS, 0, MOE_ROWS)
    blk_valid = jnp.where(blk < n_active, blk_valid, 0).astype(I32)

    run_n = tcnt.reshape(-1)
    run_local = (jnp.cumsum(tcnt, axis=1) - tcnt).reshape(-1)
    run_global = (start[None, :] + tbase).reshape(-1)
    xs = _dispatch(run_n, run_local, run_global, padded - counts, start + counts, n_active[None], u3, lp,
                   n_blocks * MOE_ROWS)

    b1p = b1.reshape(N_EXPERTS, 2 * D_FF // PAIR_COLS, LANES, 2).transpose(0, 1, 3, 2)
    ys = _experts(blk_exp, blk_next, blk_valid, n_active[None], xs, w1,
                  b1p.reshape(N_EXPERTS, 1, 2 * D_FF), w2, b2[:, None, :])
    return _combine(run_n, run_local, run_global, h2, gates, lp, g_final, ys)


def kernel(x, mem, g_mix, w_in, b_qkv, conv_w, conv_b, dt_bias, a_log, d_skip, g_ssm_out, attn_sinks, rel_bias, w_out, b_out, g_cross, g_mem, w_q_cross, w_kv_cross, w_o_cross, g_ffn, w_router, b_router, w1, b1, w2, b2, g_final):
    batch, seq, d = x.shape
    mem_len = mem.shape[1]
    depth = g_mix.shape[0]
    assert depth == 1 and d == D_MODEL and seq % TM_MID == 0
    h = x.reshape(batch * seq, d)
    mem2 = mem.reshape(batch * mem_len, d)
    bias = _bias_table(rel_bias)
    out = _layer(
        h, mem2, batch, seq, mem_len, g_mix[0], w_in[0], b_qkv[0], conv_w[0], conv_b[0], dt_bias[0],
        a_log[0], d_skip[0], g_ssm_out[0], attn_sinks[0], bias, w_out[0], b_out[0], g_cross[0],
        g_mem[0], w_q_cross[0], w_kv_cross[0], w_o_cross[0], g_ffn[0], w_router[0], b_router[0],
        w1[0], b1[0], w2[0], b2[0], g_final[None, :])
    return out.reshape(batch, seq, d)
```

```python
import math

import numpy as np
import jax
import jax.numpy as jnp
from jax import lax
from jax.experimental import pallas as pl
from jax.experimental.pallas import tpu as pltpu

F32 = jnp.float32
BF16 = jnp.bfloat16
I32 = jnp.int32

D_MODEL = 1024
RMS_EPS = 1e-5

SSM_HEADS = 8
SSM_HEAD_DIM = 64
SSM_WIDTH = SSM_HEADS * SSM_HEAD_DIM
SSM_STATE = 128
SSM_GROUPS = 2
HEADS_PER_GROUP = SSM_HEADS // SSM_GROUPS
GROUP_WIDTH = SSM_WIDTH // SSM_GROUPS
CONV_WIDTH = 4
CHUNK = 128
CONV_CH = SSM_WIDTH + 2 * SSM_GROUPS * SSM_STATE

ATTN_HEADS = 8
KV_HEADS = 2
Q_PER_KV = ATTN_HEADS // KV_HEADS
HEAD_DIM = 64
ATTN_WIDTH = ATTN_HEADS * HEAD_DIM
KV_WIDTH = KV_HEADS * HEAD_DIM
WINDOW = 128
BLOCK = WINDOW
QKV_COLS = ATTN_WIDTH + 2 * KV_WIDTH

REL_BUCKETS = 32
REL_MAX_DIST = 128

CROSS_HEADS = 4
CROSS_HEAD_DIM = 128
CROSS_WIDTH = CROSS_HEADS * CROSS_HEAD_DIM

N_EXPERTS = 32
TOP_K = 4
D_FF = D_MODEL
SWIGLU_ALPHA = 1.702
SWIGLU_LIMIT = 7.0

LANES = 128
SUBLANES = 8
DT_PAD = LANES
PROJ_COLS = SSM_WIDTH + CONV_CH + DT_PAD + QKV_COLS
VMEM_LIMIT = 56 * 1024 * 1024

TM_PROJ = 512
TM_MID = 512
MID_PARTS = 1
MOE_ROWS = 512
MOE_PART = 128
WEIGHT_DMA_ROWS = 256


def _nt(a, b):
    return lax.dot_general(a, b, (((1,), (1,)), ((), ())), preferred_element_type=F32)


def _tn(a, b):
    return lax.dot_general(a, b, (((0,), (0,)), ((), ())), preferred_element_type=F32)


def _dot(a, b):
    return jnp.dot(a, b, preferred_element_type=F32)


def _dot_exact(a, b):
    return jnp.dot(a, b, preferred_element_type=F32, precision=lax.Precision.HIGHEST)


def _rms(x, g):
    return x * lax.rsqrt(jnp.mean(x * x, axis=-1, keepdims=True) + RMS_EPS) * g


def _sigmoid(x):
    return 1.0 / (1.0 + jnp.exp(-x))


def _params(sem=None, **kwargs):
    return pltpu.CompilerParams(dimension_semantics=sem, vmem_limit_bytes=VMEM_LIMIT, **kwargs)


def _inproj_kernel(x_ref, g_ref, w_ref, bq_ref, z_ref, xbc_ref, dt_ref, qkv_ref):
    u = _rms(x_ref[...], g_ref[...])
    p = _dot(u.astype(BF16), w_ref[...])
    z_ref[...] = p[:, :SSM_WIDTH]
    xbc_ref[...] = p[:, SSM_WIDTH:SSM_WIDTH + CONV_CH]
    dt_ref[...] = p[:, SSM_WIDTH + CONV_CH:SSM_WIDTH + CONV_CH + DT_PAD]
    qkv_ref[...] = p[:, SSM_WIDTH + CONV_CH + DT_PAD:] + bq_ref[...]


def _inproj(x2, g_mix, w_cat, b_qkv):
    n = x2.shape[0]
    row = lambda i: (i, 0)
    fixed = lambda i: (0, 0)
    return pl.pallas_call(
        _inproj_kernel,
        name="inproj",
        grid=(n // TM_PROJ,),
        in_specs=[
            pl.BlockSpec((TM_PROJ, D_MODEL), row),
            pl.BlockSpec((1, D_MODEL), fixed),
            pl.BlockSpec((D_MODEL, PROJ_COLS), fixed),
            pl.BlockSpec((1, QKV_COLS), fixed),
        ],
        out_specs=[
            pl.BlockSpec((TM_PROJ, SSM_WIDTH), row),
            pl.BlockSpec((TM_PROJ, CONV_CH), row),
            pl.BlockSpec((TM_PROJ, DT_PAD), row),
            pl.BlockSpec((TM_PROJ, QKV_COLS), row),
        ],
        out_shape=[
            jax.ShapeDtypeStruct((n, SSM_WIDTH), F32),
            jax.ShapeDtypeStruct((n, CONV_CH), F32),
            jax.ShapeDtypeStruct((n, DT_PAD), F32),
            jax.ShapeDtypeStruct((n, QKV_COLS), F32),
        ],
        compiler_params=_params(("arbitrary",)),
    )(x2, g_mix, w_cat, b_qkv)


def _split3(x):
    hi = x.astype(BF16)
    rest = x - hi.astype(F32)
    mid = rest.astype(BF16)
    lo = (rest - mid.astype(F32)).astype(BF16)
    return jnp.concatenate([hi, mid, lo], axis=1)


def _ssd_kernel(xbc_ref, z_ref, dt_ref, cw_ref, cb_ref, dtb_ref, ah_ref, dsk_ref, gout_ref,
                e64_ref, ecat_ref, tri_ref, y_ref, conv_ref, state_ref):
    c = pl.program_id(0)

    @pl.when(c == 0)
    def _():
        conv_ref[:, :SUBLANES, :] = jnp.zeros((conv_ref.shape[0], SUBLANES, CONV_CH), F32)
        state_ref[...] = jnp.zeros_like(state_ref)

    for b in range(xbc_ref.shape[0]):
        _ssd_chunk(b, xbc_ref, z_ref, dt_ref, cw_ref, cb_ref, dtb_ref, ah_ref, dsk_ref, gout_ref,
                   e64_ref, ecat_ref, tri_ref, y_ref, conv_ref, state_ref)


def _ssd_chunk(b, xbc_ref, z_ref, dt_ref, cw_ref, cb_ref, dtb_ref, ah_ref, dsk_ref, gout_ref,
               e64_ref, ecat_ref, tri_ref, y_ref, conv_ref, state_ref):
    u = xbc_ref[b]
    conv_ref[b, SUBLANES:, :] = u
    acc = u * cw_ref[CONV_WIDTH - 1:CONV_WIDTH, :] + cb_ref[...]
    for j in range(1, CONV_WIDTH):
        shifted = conv_ref[b, SUBLANES - j:SUBLANES - j + CHUNK, :]
        acc = acc + shifted * cw_ref[CONV_WIDTH - 1 - j:CONV_WIDTH - j, :]
    conv_ref[b, :SUBLANES, :] = u[CHUNK - SUBLANES:, :]
    xbc = acc * _sigmoid(acc)

    xs = xbc[:, :SSM_WIDTH]
    bm = xbc[:, SSM_WIDTH:SSM_WIDTH + SSM_GROUPS * SSM_STATE]
    cm = xbc[:, SSM_WIDTH + SSM_GROUPS * SSM_STATE:]

    dtr = dt_ref[b] + dtb_ref[...]
    dt = jnp.maximum(dtr, 0.0) + jnp.log1p(jnp.exp(-jnp.abs(dtr)))
    a = dt * ah_ref[...]
    a3 = _split3(a)
    a_rows = jnp.concatenate([a3[:, :DT_PAD], a3[:, DT_PAD:2 * DT_PAD], a3[:, 2 * DT_PAD:]], axis=0)
    cs = _dot(tri_ref[...], a_rows)
    cs_row = cs.T
    dt_full = _dot(_split3(dt), e64_ref[...])
    cs_cat = _dot(_split3(cs), ecat_ref[...])
    cs_wide = cs_cat[:, :SSM_HEADS * CHUNK]
    cs_full = cs_cat[:, SSM_HEADS * CHUNK:]
    cs_last = cs_full[CHUNK - 1:CHUNK, :]

    xdt = xs * dt_full
    xdt_b = xdt.astype(BF16)
    xw_b = (xdt * jnp.exp(cs_last - cs_full)).astype(BF16)
    in_decay = jnp.exp(cs_full)
    chunk_decay = jnp.exp(cs_last)

    li = lax.broadcasted_iota(I32, (CHUNK, CHUNK), 0)
    si = lax.broadcasted_iota(I32, (CHUNK, CHUNK), 1)
    causal = li >= si

    ys = []
    for g in range(SSM_GROUPS):
        bg = bm[:, g * SSM_STATE:(g + 1) * SSM_STATE].astype(BF16)
        cg = cm[:, g * SSM_STATE:(g + 1) * SSM_STATE].astype(BF16)
        cb = _nt(cg, bg)
        yd = []
        for r in range(HEADS_PER_GROUP):
            h = g * HEADS_PER_GROUP + r
            diff = cs_wide[:, h * CHUNK:(h + 1) * CHUNK] - cs_row[h:h + 1, :]
            m = cb * jnp.exp(jnp.where(causal, diff, -jnp.inf))
            yd.append(_dot(m.astype(BF16), xdt_b[:, h * SSM_HEAD_DIM:(h + 1) * SSM_HEAD_DIM]))
        y_diag = jnp.concatenate(yd, axis=1)
        gs = slice(g * GROUP_WIDTH, (g + 1) * GROUP_WIDTH)
        st = state_ref[b, g]
        y_off = _dot(cg, st.astype(BF16)) * in_decay[:, gs]
        state_ref[b, g] = st * chunk_decay[:, gs] + _tn(bg, xw_b[:, gs])
        ys.append(y_diag + y_off)
    y = jnp.concatenate(ys, axis=1) + dsk_ref[...] * xs

    zz = z_ref[b]
    y = y * (zz * _sigmoid(zz))
    outs = []
    for g in range(SSM_GROUPS):
        yg = y[:, g * GROUP_WIDTH:(g + 1) * GROUP_WIDTH]
        outs.append(yg * lax.rsqrt(jnp.mean(yg * yg, axis=-1, keepdims=True) + RMS_EPS))
    y_ref[b] = jnp.concatenate(outs, axis=1) * gout_ref[...]


def _ssd(xbc, z, dtp, conv_w, conv_b, dt_bias_p, a_head_p, d_full, g_out):
    batch, seq, _ = xbc.shape
    chunk = lambda c: (0, c, 0)
    fixed = lambda c: (0, 0)
    head = np.arange(SSM_WIDTH) // SSM_HEAD_DIM
    e64 = (np.arange(DT_PAD)[:, None] == head[None, :]).astype(np.float32)
    head_w = np.arange(SSM_HEADS * CHUNK) // CHUNK
    e128 = (np.arange(DT_PAD)[:, None] == head_w[None, :]).astype(np.float32)
    e64_3 = jnp.asarray(np.tile(e64, (3, 1))).astype(BF16)
    ecat_3 = jnp.asarray(np.tile(np.concatenate([e128, e64], axis=1), (3, 1))).astype(BF16)
    tri_3 = jnp.asarray(np.tile(np.tril(np.ones((CHUNK, CHUNK), np.float32)), (1, 3))).astype(BF16)
    return pl.pallas_call(
        _ssd_kernel,
        name="ssd",
        grid=(seq // CHUNK,),
        in_specs=[
            pl.BlockSpec((batch, CHUNK, CONV_CH), chunk),
            pl.BlockSpec((batch, CHUNK, SSM_WIDTH), chunk),
            pl.BlockSpec((batch, CHUNK, DT_PAD), chunk),
            pl.BlockSpec((CONV_WIDTH, CONV_CH), fixed),
            pl.BlockSpec((1, CONV_CH), fixed),
            pl.BlockSpec((1, DT_PAD), fixed),
            pl.BlockSpec((1, DT_PAD), fixed),
            pl.BlockSpec((1, SSM_WIDTH), fixed),
            pl.BlockSpec((1, SSM_WIDTH), fixed),
            pl.BlockSpec((3 * DT_PAD, SSM_WIDTH), fixed),
            pl.BlockSpec((3 * DT_PAD, SSM_HEADS * CHUNK + SSM_WIDTH), fixed),
            pl.BlockSpec((CHUNK, 3 * CHUNK), fixed),
        ],
        out_specs=pl.BlockSpec((batch, CHUNK, SSM_WIDTH), chunk),
        out_shape=jax.ShapeDtypeStruct((batch, seq, SSM_WIDTH), F32),
        scratch_shapes=[
            pltpu.VMEM((batch, SUBLANES + CHUNK, CONV_CH), F32),
            pltpu.VMEM((batch, SSM_GROUPS, SSM_STATE, GROUP_WIDTH), F32),
        ],
        compiler_params=_params(("arbitrary",)),
    )(xbc, z, dtp, conv_w, conv_b, dt_bias_p, a_head_p, d_full, g_out, e64_3, ecat_3, tri_3)


def _t5_bucket_table():
    q_pos = np.arange(BLOCK)[:, None]
    k_pos = np.arange(2 * BLOCK)[None, :] - BLOCK
    dist = q_pos - k_pos
    d = np.maximum(dist, 0)
    max_exact = REL_BUCKETS // 2
    ratio = np.maximum(d, max_exact).astype(np.float32) / np.float32(max_exact)
    large = max_exact + (np.log(ratio) / np.float32(math.log(REL_MAX_DIST / max_exact))
                         * np.float32(REL_BUCKETS - max_exact)).astype(np.int32)
    large = np.minimum(large, REL_BUCKETS - 1)
    bucket = np.where(d < max_exact, d, large)
    in_window = (dist >= 0) & (dist < WINDOW)
    return np.where(in_window, bucket, -1).astype(np.int32)


def _bias_kernel(rb_ref, bucket_ref, out_ref):
    bucket = bucket_ref[...]
    for h in range(ATTN_HEADS):
        acc = jnp.zeros(bucket.shape, F32)
        for b in range(REL_BUCKETS):
            acc = jnp.where(bucket == b, rb_ref[b, h], acc)
        out_ref[h] = jnp.where(bucket >= 0, acc, -jnp.inf)


def _bias_table(rel_bias):
    bucket = jnp.asarray(_t5_bucket_table())
    return pl.pallas_call(
        _bias_kernel,
        in_specs=[
            pl.BlockSpec(memory_space=pltpu.SMEM),
            pl.BlockSpec(memory_space=pltpu.VMEM),
        ],
        out_specs=pl.BlockSpec(memory_space=pltpu.VMEM),
        out_shape=jax.ShapeDtypeStruct((ATTN_HEADS, BLOCK, 2 * BLOCK), F32),
    )(rel_bias, bucket)


def _swa_kernel(sink_ref, q_ref, k_ref, v_ref, kp_ref, vp_ref, bias_ref, o_ref):
    i = pl.program_id(0)
    col = lax.broadcasted_iota(I32, (BLOCK, 2 * BLOCK), 1)
    key_ok = jnp.logical_or(col >= BLOCK, i > 0)
    for b in range(q_ref.shape[0]):
        q = (q_ref[b] * (HEAD_DIM ** -0.5)).astype(BF16)
        outs = []
        for g in range(KV_HEADS):
            gs = slice(g * HEAD_DIM, (g + 1) * HEAD_DIM)
            k2 = jnp.concatenate([kp_ref[b, :, gs], k_ref[b, :, gs]], axis=0).astype(BF16)
            v2 = jnp.concatenate([vp_ref[b, :, gs], v_ref[b, :, gs]], axis=0).astype(BF16)
            for r in range(Q_PER_KV):
                h = g * Q_PER_KV + r
                s = _nt(q[:, h * HEAD_DIM:(h + 1) * HEAD_DIM], k2) + bias_ref[h]
                s = jnp.where(key_ok, s, -jnp.inf)
                sink = sink_ref[h]
                m = jnp.maximum(jnp.max(s, axis=-1, keepdims=True), sink)
                p = jnp.exp(s - m)
                denom = jnp.sum(p, axis=-1, keepdims=True) + jnp.exp(sink - m)
                outs.append(_dot(p.astype(BF16), v2) / denom)
        o_ref[b] = jnp.concatenate(outs, axis=1)


def _swa(qkv, sinks, bias):
    batch, seq, _ = qkv.shape
    kcol = ATTN_WIDTH // KV_WIDTH
    prev = lambda i: jnp.maximum(i - 1, 0)
    return pl.pallas_call(
        _swa_kernel,
        name="swa",
        grid_spec=pltpu.PrefetchScalarGridSpec(
            num_scalar_prefetch=1,
            grid=(seq // BLOCK,),
            in_specs=[
                pl.BlockSpec((batch, BLOCK, ATTN_WIDTH), lambda i, s: (0, i, 0)),
                pl.BlockSpec((batch, BLOCK, KV_WIDTH), lambda i, s: (0, i, kcol)),
                pl.BlockSpec((batch, BLOCK, KV_WIDTH), lambda i, s: (0, i, kcol + 1)),
                pl.BlockSpec((batch, BLOCK, KV_WIDTH), lambda i, s: (0, prev(i), kcol)),
                pl.BlockSpec((batch, BLOCK, KV_WIDTH), lambda i, s: (0, prev(i), kcol + 1)),
                pl.BlockSpec((ATTN_HEADS, BLOCK, 2 * BLOCK), lambda i, s: (0, 0, 0)),
            ],
            out_specs=pl.BlockSpec((batch, BLOCK, ATTN_WIDTH), lambda i, s: (0, i, 0)),
        ),
        out_shape=jax.ShapeDtypeStruct((batch, seq, ATTN_WIDTH), F32),
        compiler_params=_params(("arbitrary",)),
    )(sinks, qkv, qkv, qkv, qkv, qkv, bias)


def _memkv_kernel(m_ref, g_ref, w_ref, k_ref, v_ref):
    u = _rms(m_ref[...], g_ref[...])
    kv = _dot(u.astype(BF16), w_ref[...])
    k_ref[...] = kv[:, :CROSS_WIDTH].astype(BF16)
    v_ref[...] = kv[:, CROSS_WIDTH:].astype(BF16)


def _memkv(mem2, g_mem, w_kv):
    n = mem2.shape[0]
    return pl.pallas_call(
        _memkv_kernel,
        out_shape=[jax.ShapeDtypeStruct((n, CROSS_WIDTH), BF16)] * 2,
        compiler_params=_params(),
    )(mem2, g_mem, w_kv)


def _mid_kernel(ys_ref, ya_ref, x_ref, wout_ref, bout_ref, gc_ref, wq_ref, k_ref, v_ref, wo_ref,
                gf_ref, wr_ref, br_ref, upper_ref, lower_ref,
                h2_ref, u3_ref, gate_ref, lp_ref, tcnt_ref, tbase_ref, run_ref):
    i = pl.program_id(0)

    @pl.when(i == 0)
    def _():
        run_ref[...] = jnp.zeros_like(run_ref)

    u3_parts = []
    for part in range(MID_PARTS):
        rows = slice(part * (TM_MID // MID_PARTS), (part + 1) * (TM_MID // MID_PARTS))
        ycat = jnp.concatenate([ys_ref[rows, :], ya_ref[rows, :]], axis=1).astype(BF16)
        h1 = x_ref[rows, :] + _dot(ycat, wout_ref[...]) + bout_ref[...]

        u2 = _rms(h1, gc_ref[...])
        q = _dot(u2.astype(BF16), wq_ref[...]).astype(BF16)
        outs = []
        for h in range(CROSS_HEADS):
            hs = slice(h * CROSS_HEAD_DIM, (h + 1) * CROSS_HEAD_DIM)
            s = _nt(q[:, hs], k_ref[:, hs]) * (CROSS_HEAD_DIM ** -0.5)
            m = jnp.max(s, axis=-1, keepdims=True)
            p = jnp.exp(s - m)
            outs.append(_dot(p.astype(BF16), v_ref[:, hs]) / jnp.sum(p, axis=-1, keepdims=True))
        o = jnp.concatenate(outs, axis=1).astype(BF16)
        h2 = h1 + _dot(o, wo_ref[...])
        h2_ref[rows, :] = h2
        u3_parts.append(_rms(h2, gf_ref[...]).astype(BF16))
    u3 = jnp.concatenate(u3_parts, axis=0)
    u3_ref[...] = u3

    logits = _nt(wr_ref[...], u3) + br_ref[...]
    eid = lax.broadcasted_iota(I32, logits.shape, 0)
    vals, idxs, hots = [], [], []
    for _ in range(TOP_K):
        m = jnp.max(logits, axis=0, keepdims=True)
        ix = jnp.min(jnp.where(logits == m, eid, N_EXPERTS), axis=0, keepdims=True)
        hot = eid == ix
        logits = jnp.where(hot, -jnp.inf, logits)
        vals.append(m)
        idxs.append(ix)
        hots.append(hot)
    ex = [jnp.exp(v - vals[0]) for v in vals]
    tot = ex[0] + ex[1] + ex[2] + ex[3]
    gate_ref[...] = jnp.concatenate([e / tot for e in ex], axis=0)

    hot_f = [jnp.where(hot, 1.0, 0.0) for hot in hots]
    cnt_k = [jnp.sum(hf, axis=1, keepdims=True) for hf in hot_f]
    tile_cnt = cnt_k[0] + cnt_k[1] + cnt_k[2] + cnt_k[3]
    tile_cnt = jnp.floor((tile_cnt + (SUBLANES - 1)) * (1.0 / SUBLANES)) * SUBLANES
    offs = _dot_exact(lower_ref[...], jnp.broadcast_to(tile_cnt, (N_EXPERTS, LANES)))[:, :1]
    local = []
    before_all = _dot(jnp.concatenate(hot_f, axis=0).astype(BF16), upper_ref[...])
    for kk in range(TOP_K):
        before = before_all[kk * N_EXPERTS:(kk + 1) * N_EXPERTS, :]
        local.append(jnp.sum(jnp.where(hots[kk], before + offs, 0.0), axis=0, keepdims=True))
        offs = offs + cnt_k[kk]
    lp_ref[...] = jnp.concatenate(local, axis=0).astype(I32)
    tcnt_ref[0] = jnp.broadcast_to(tile_cnt, (N_EXPERTS, LANES)).astype(I32)
    tbase_ref[0] = jnp.broadcast_to(run_ref[...], (N_EXPERTS, LANES)).astype(I32)
    run_ref[...] = run_ref[...] + tile_cnt


def _mid(y_ssm, y_attn, x2, w_out, b_out, g_cross, w_q, k_mem, v_mem, w_o, g_ffn, w_r_t, b_r, seq, mem_len):
    n = x2.shape[0]
    tiles_per_batch = seq // TM_MID
    row = lambda i: (i, 0)
    col = lambda i: (0, i)
    fixed = lambda i: (0, 0)
    memb = lambda i: (i // tiles_per_batch, 0)
    tile3 = lambda i: (i, 0, 0)
    n_tiles = n // TM_MID
    upper = jnp.asarray(np.triu(np.ones((TM_MID, TM_MID), np.float32), 1)).astype(BF16)
    lower = jnp.asarray(np.tril(np.ones((N_EXPERTS, N_EXPERTS), np.float32), -1))
    return pl.pallas_call(
        _mid_kernel,
        name="mid",
        grid=(n_tiles,),
        in_specs=[
            pl.BlockSpec((TM_MID, SSM_WIDTH), row),
            pl.BlockSpec((TM_MID, ATTN_WIDTH), row),
            pl.BlockSpec((TM_MID, D_MODEL), row),
            pl.BlockSpec((D_MODEL, D_MODEL), fixed),
            pl.BlockSpec((1, D_MODEL), fixed),
            pl.BlockSpec((1, D_MODEL), fixed),
            pl.BlockSpec((D_MODEL, CROSS_WIDTH), fixed),
            pl.BlockSpec((mem_len, CROSS_WIDTH), memb),
            pl.BlockSpec((mem_len, CROSS_WIDTH), memb),
            pl.BlockSpec((CROSS_WIDTH, D_MODEL), fixed),
            pl.BlockSpec((1, D_MODEL), fixed),
            pl.BlockSpec((N_EXPERTS, D_MODEL), fixed),
            pl.BlockSpec((N_EXPERTS, 1), fixed),
            pl.BlockSpec((TM_MID, TM_MID), fixed),
            pl.BlockSpec((N_EXPERTS, N_EXPERTS), fixed),
        ],
        out_specs=[
            pl.BlockSpec((TM_MID, D_MODEL), row),
            pl.BlockSpec((TM_MID, D_MODEL), row),
            pl.BlockSpec((TOP_K, TM_MID), col),
            pl.BlockSpec((TOP_K, TM_MID), col),
            pl.BlockSpec((1, N_EXPERTS, LANES), tile3),
            pl.BlockSpec((1, N_EXPERTS, LANES), tile3),
        ],
        out_shape=[
            jax.ShapeDtypeStruct((n, D_MODEL), F32),
            jax.ShapeDtypeStruct((n, D_MODEL), BF16),
            jax.ShapeDtypeStruct((TOP_K, n), F32),
            jax.ShapeDtypeStruct((TOP_K, n), I32),
            jax.ShapeDtypeStruct((n_tiles, N_EXPERTS, LANES), I32),
            jax.ShapeDtypeStruct((n_tiles, N_EXPERTS, LANES), I32),
        ],
        scratch_shapes=[pltpu.VMEM((N_EXPERTS, 1), F32)],
        compiler_params=_params(("arbitrary",)),
    )(y_ssm, y_attn, x2, w_out, b_out, g_cross, w_q, k_mem, v_mem, w_o, g_ffn, w_r_t, b_r, upper, lower)


ROW_CHUNK = 256
LOCAL_ROWS = -(-(TM_MID * TOP_K + N_EXPERTS * (SUBLANES - 1)) // ROW_CHUNK) * ROW_CHUNK
RUN_BITS = tuple(1 << s for s in range(TM_MID.bit_length() - 1, SUBLANES.bit_length() - 2, -1))
PAD_BITS = tuple(1 << s for s in range((MOE_ROWS - 1).bit_length() - 1, SUBLANES.bit_length() - 2, -1))


U32 = jnp.uint32
PACKED = D_MODEL // 2


def _pack_pairs(v):
    bits = lax.bitcast_convert_type(v, U32)
    return bits[:, PACKED:] | (bits[:, :PACKED] >> 16)


def _unpack_pairs(w):
    lo = lax.bitcast_convert_type(w << 16, F32)
    hi = lax.bitcast_convert_type(w & jnp.uint32(0xFFFF0000), F32)
    return jnp.concatenate([lo, hi], axis=1).astype(BF16)


def _run_copies(n, src_ref, src, dst_ref, dst, sem, bits, wait, advance_src=True):
    for bit in bits:
        step = n & bit

        @pl.when(step != 0)
        def _():
            cp = pltpu.make_async_copy(src_ref.at[pl.ds(pl.multiple_of(src, SUBLANES), bit)],
                                       dst_ref.at[pl.ds(pl.multiple_of(dst, SUBLANES), bit)], sem)
            if wait:
                cp.wait()
            else:
                cp.start()

        if advance_src:
            src = src + step
        dst = dst + step


def _wait_rows(n, src_ref, dst_ref, sem):
    rows = pl.ds(0, pl.multiple_of(n, SUBLANES))
    pltpu.make_async_copy(src_ref.at[rows], dst_ref.at[rows], sem).wait()


def _dispatch_kernel(tcnt_ref, lo_ref, gs_ref, padn_ref, pads_ref, na_ref, u_ref, lp_ref, xs_ref,
                     xloc_ref, zero_ref, sem, zsem):
    i = pl.program_id(0)

    def zero_fill(wait):
        def body(e, carry):
            _run_copies(padn_ref[e], zero_ref, 0, xs_ref, pads_ref[e], zsem, PAD_BITS, wait, advance_src=False)
            return carry
        lax.fori_loop(0, N_EXPERTS, body, 0)

        def tail(b, carry):
            cp = pltpu.make_async_copy(zero_ref, xs_ref.at[pl.ds(pl.multiple_of(b * MOE_ROWS, MOE_ROWS), MOE_ROWS)], zsem)
            if wait:
                cp.wait()
            else:
                cp.start()
            return carry
        lax.fori_loop(na_ref[0], xs_ref.shape[0] // MOE_ROWS, tail, 0)

    @pl.when(i == 0)
    def _():
        zero_ref[...] = jnp.zeros_like(zero_ref)
        zero_fill(False)

    slot = i % 2
    u = u_ref[...]
    lp = lp_ref[...]
    for c in range(LOCAL_ROWS // ROW_CHUNK):
        r = lax.broadcasted_iota(I32, (ROW_CHUNK, TM_MID), 0) + c * ROW_CHUNK
        p = jnp.zeros((ROW_CHUNK, TM_MID), F32)
        for kk in range(TOP_K):
            p = jnp.where(lp[kk:kk + 1, :] == r, 1.0, p)
        xloc_ref[slot, c * ROW_CHUNK:(c + 1) * ROW_CHUNK, :] = _pack_pairs(_dot(p.astype(BF16), u))

    def start_runs(tile):
        s = tile % 2

        def body(e, carry):
            j = tile * N_EXPERTS + e
            _run_copies(tcnt_ref[j], xloc_ref.at[s], lo_ref[j], xs_ref, gs_ref[j], sem.at[s], RUN_BITS, False)
            return carry
        lax.fori_loop(0, N_EXPERTS, body, 0)

    def wait_runs(tile):
        s = tile % 2
        last = tile * N_EXPERTS + N_EXPERTS - 1
        _wait_rows(lo_ref[last] + tcnt_ref[last], xloc_ref.at[s], xs_ref, sem.at[s])

    start_runs(i)

    @pl.when(i > 0)
    def _():
        wait_runs(i - 1)

    @pl.when(i == pl.num_programs(0) - 1)
    def _():
        wait_runs(i)

    @pl.when(i == 0)
    def _():
        zero_fill(True)


def _dispatch(tcnt, lo, gstart, padn, pads, n_active, u3, lp, n_rows):
    n = u3.shape[0]
    return pl.pallas_call(
        _dispatch_kernel,
        name="dispatch",
        grid_spec=pltpu.PrefetchScalarGridSpec(
            num_scalar_prefetch=6,
            grid=(n // TM_MID,),
            in_specs=[
                pl.BlockSpec((TM_MID, D_MODEL), lambda i, *_: (i, 0)),
                pl.BlockSpec((TOP_K, TM_MID), lambda i, *_: (0, i)),
            ],
            out_specs=pl.BlockSpec(memory_space=pl.ANY),
            scratch_shapes=[
                pltpu.VMEM((2, LOCAL_ROWS, PACKED), U32),
                pltpu.VMEM((MOE_ROWS, PACKED), U32),
                pltpu.SemaphoreType.DMA((2,)),
                pltpu.SemaphoreType.DMA,
            ],
        ),
        out_shape=jax.ShapeDtypeStruct((n_rows, PACKED), U32),
        compiler_params=_params(("arbitrary",)),
    )(tcnt, lo, gstart, padn, pads, n_active, u3, lp)


PAIR_COLS = 2 * LANES


def _expert_kernel(be_ref, nx_ref, nv_ref, na_ref, x_ref, b1_ref, b2_ref, perm_ref, w1_hbm, w2_hbm, y_ref,
                   w1s_ref, w2s_ref, w1p_ref, w2b_ref, sem):
    b = pl.program_id(0)
    valid_rows = nv_ref[b]
    active = valid_rows > 0
    expert = be_ref[b]
    new_expert = jnp.logical_or(b == 0, expert != be_ref[jnp.maximum(b - 1, 0)])

    def fetch(e, wait):
        for src, dst, s in ((w1_hbm, w1s_ref, 0), (w2_hbm, w2s_ref, 1)):
            for r in range(0, src.shape[1], WEIGHT_DMA_ROWS):
                rs = pl.ds(r, WEIGHT_DMA_ROWS)
                cp = pltpu.make_async_copy(src.at[e, rs], dst.at[rs], sem.at[s])
                if wait:
                    cp.wait()
                else:
                    cp.start(priority=1)

    @pl.when(jnp.logical_and(b == 0, active))
    def _():
        fetch(expert, False)

    @pl.when(jnp.logical_and(active, new_expert))
    def _():
        fetch(expert, True)
        for c in range(2 * D_FF // PAIR_COLS):
            cs = slice(c * PAIR_COLS, (c + 1) * PAIR_COLS)
            w1p_ref[:, cs] = _dot(w1s_ref[:, cs].astype(BF16), perm_ref[...]).astype(BF16)
        w2b_ref[...] = w2s_ref[...].astype(BF16)

        @pl.when(nx_ref[b] >= 0)
        def _():
            fetch(nx_ref[b], False)

    for live in range(MOE_PART, MOE_ROWS + 1, MOE_PART):
        @pl.when(jnp.logical_and(valid_rows > live - MOE_PART, valid_rows <= live))
        def _():
            x = _unpack_pairs(x_ref[:live, :])
            hdn = _dot(x, w1p_ref[...]) + b1_ref[0]
            acts = []
            for c in range(2 * D_FF // PAIR_COLS):
                hg = jnp.minimum(hdn[:, c * PAIR_COLS:c * PAIR_COLS + LANES], SWIGLU_LIMIT)
                hl = jnp.clip(hdn[:, c * PAIR_COLS + LANES:(c + 1) * PAIR_COLS], -SWIGLU_LIMIT, SWIGLU_LIMIT)
                acts.append((hg * _sigmoid(SWIGLU_ALPHA * hg) * (hl + 1.0)).astype(BF16))
            y = _dot(jnp.concatenate(acts, axis=1), w2b_ref[...]) + b2_ref[0]
            y_ref[:live, :] = _pack_pairs(y.astype(BF16).astype(F32))
            if live < MOE_ROWS:
                y_ref[live:, :] = jnp.zeros((MOE_ROWS - live, PACKED), U32)

    @pl.when(jnp.logical_not(active))
    def _():
        y_ref[...] = jnp.zeros_like(y_ref)


def _experts(block_exp, next_exp, valid_rows, n_active, xs, w1, b1p, w2, b2):
    n_rows = xs.shape[0]
    n_blocks = n_rows // MOE_ROWS
    xmap = lambda b, be, nx, nv, na: (jnp.maximum(jnp.minimum(b, na[0] - 1), 0), 0)
    emap = lambda b, be, nx, nv, na: (be[b], 0, 0)
    perm = np.zeros((PAIR_COLS, PAIR_COLS), np.float32)
    perm[2 * np.arange(LANES), np.arange(LANES)] = 1.0
    perm[2 * np.arange(LANES) + 1, LANES + np.arange(LANES)] = 1.0
    return pl.pallas_call(
        _expert_kernel,
        name="experts",
        grid_spec=pltpu.PrefetchScalarGridSpec(
            num_scalar_prefetch=4,
            grid=(n_blocks,),
            in_specs=[
                pl.BlockSpec((MOE_ROWS, PACKED), xmap),
                pl.BlockSpec((1, 1, 2 * D_FF), emap),
                pl.BlockSpec((1, 1, D_MODEL), emap),
                pl.BlockSpec((PAIR_COLS, PAIR_COLS), lambda b, *_: (0, 0)),
                pl.BlockSpec(memory_space=pl.ANY),
                pl.BlockSpec(memory_space=pl.ANY),
            ],
            out_specs=pl.BlockSpec((MOE_ROWS, PACKED), lambda b, *_: (b, 0)),
            scratch_shapes=[
                pltpu.VMEM((D_MODEL, 2 * D_FF), F32),
                pltpu.VMEM((D_FF, D_MODEL), F32),
                pltpu.VMEM((D_MODEL, 2 * D_FF), BF16),
                pltpu.VMEM((D_FF, D_MODEL), BF16),
                pltpu.SemaphoreType.DMA((2,)),
            ],
        ),
        out_shape=jax.ShapeDtypeStruct((n_rows, PACKED), U32),
        compiler_params=_params(("arbitrary",)),
    )(block_exp, next_exp, valid_rows, n_active, xs, b1p, b2, jnp.asarray(perm).astype(BF16), w1, w2)


def _combine_kernel(tcnt_ref, lo_ref, gs_ref, h2_ref, gate_ref, lp_ref, g_ref, ys_ref, o_ref, yloc_ref, sem):
    i = pl.program_id(0)

    slot = i % 2

    def start_runs(tile):
        s = tile % 2

        def body(e, carry):
            j = tile * N_EXPERTS + e
            _run_copies(tcnt_ref[j], ys_ref, gs_ref[j], yloc_ref.at[s], lo_ref[j], sem.at[s], RUN_BITS, False)
            return carry
        lax.fori_loop(0, N_EXPERTS, body, 0)

    @pl.when(i == 0)
    def _():
        yloc_ref[...] = jnp.zeros_like(yloc_ref)
        start_runs(i)

    @pl.when(i + 1 < pl.num_programs(0))
    def _():
        start_runs(i + 1)

    gates = gate_ref[...]
    lp = lp_ref[...]
    last = i * N_EXPERTS + N_EXPERTS - 1
    _wait_rows(lo_ref[last] + tcnt_ref[last], ys_ref, yloc_ref.at[slot], sem.at[slot])

    h = h2_ref[...]
    for c in range(LOCAL_ROWS // ROW_CHUNK):
        r = lax.broadcasted_iota(I32, (ROW_CHUNK, TM_MID), 0) + c * ROW_CHUNK
        a = jnp.zeros((ROW_CHUNK, TM_MID), F32)
        for kk in range(TOP_K):
            a = jnp.where(lp[kk:kk + 1, :] == r, gates[kk:kk + 1, :], a)
        h = h + _tn(a.astype(BF16), _unpack_pairs(yloc_ref[slot, c * ROW_CHUNK:(c + 1) * ROW_CHUNK, :]))
    o_ref[...] = _rms(h, g_ref[...])


def _combine(tcnt, lo, gstart, h2, gates, lp, g_final, ys):
    n = h2.shape[0]
    return pl.pallas_call(
        _combine_kernel,
        name="combine",
        grid_spec=pltpu.PrefetchScalarGridSpec(
            num_scalar_prefetch=3,
            grid=(n // TM_MID,),
            in_specs=[
                pl.BlockSpec((TM_MID, D_MODEL), lambda i, *_: (i, 0)),
                pl.BlockSpec((TOP_K, TM_MID), lambda i, *_: (0, i)),
                pl.BlockSpec((TOP_K, TM_MID), lambda i, *_: (0, i)),
                pl.BlockSpec((1, D_MODEL), lambda i, *_: (0, 0)),
                pl.BlockSpec(memory_space=pl.ANY),
            ],
            out_specs=pl.BlockSpec((TM_MID, D_MODEL), lambda i, *_: (i, 0)),
            scratch_shapes=[
                pltpu.VMEM((2, LOCAL_ROWS, PACKED), U32),
                pltpu.SemaphoreType.DMA((2,)),
            ],
        ),
        out_shape=jax.ShapeDtypeStruct((n, D_MODEL), F32),
        compiler_params=_params(("arbitrary",)),
    )(tcnt, lo, gstart, h2, gates, lp, g_final, ys)


def _layer(h, mem2, batch, seq, mem_len, g_mix, w_in, b_qkv, conv_w, conv_b, dt_bias, a_log, d_skip,
           g_ssm_out, attn_sinks, bias, w_out, b_out, g_cross, g_mem, w_q_cross, w_kv_cross, w_o_cross,
           g_ffn, w_router, b_router, w1, b1, w2, b2, g_final):
    n = h.shape[0]
    c1, c2 = SSM_WIDTH + CONV_CH, SSM_WIDTH + CONV_CH + SSM_HEADS
    w_cat = jnp.concatenate(
        [w_in[:, :c1], jnp.pad(w_in[:, c1:c2], ((0, 0), (0, DT_PAD - SSM_HEADS))), w_in[:, c2:]],
        axis=1).astype(BF16)
    z, xbc, dtp, qkv = _inproj(h, g_mix[None, :], w_cat, b_qkv[None, :])

    pad_h = (0, DT_PAD - SSM_HEADS)
    dt_bias_p = jnp.pad(dt_bias, pad_h)[None, :]
    a_head_p = jnp.pad(-jnp.exp(a_log), pad_h)[None, :]
    d_full = jnp.repeat(d_skip, SSM_HEAD_DIM)[None, :]
    per_seq = lambda t: t.reshape(batch, seq, t.shape[-1])
    y_ssm = _ssd(per_seq(xbc), per_seq(z), per_seq(dtp), conv_w, conv_b[None, :], dt_bias_p, a_head_p,
                 d_full, g_ssm_out[None, :]).reshape(n, SSM_WIDTH)
    y_attn = _swa(per_seq(qkv), attn_sinks, bias).reshape(n, ATTN_WIDTH)

    k_mem, v_mem = _memkv(mem2, g_mem[None, :], w_kv_cross.astype(BF16))
    h2, u3, gates, lp, tcnt, tbase = _mid(
        y_ssm, y_attn, h, w_out.astype(BF16), b_out[None, :], g_cross[None, :],
        w_q_cross.astype(BF16), k_mem, v_mem, w_o_cross.astype(BF16), g_ffn[None, :],
        w_router.T.astype(BF16), b_router[:, None], seq, mem_len)

    max_rows = n * TOP_K + (n // TM_MID) * N_EXPERTS * (SUBLANES - 1)
    n_blocks = -(-max_rows // MOE_ROWS) + N_EXPERTS
    tcnt = tcnt[:, :, 0]
    tbase = tbase[:, :, 0]
    counts = tbase[-1] + tcnt[-1]
    padded = (counts + MOE_ROWS - 1) // MOE_ROWS * MOE_ROWS
    padded_end = jnp.cumsum(padded)
    start = (padded_end - padded).astype(I32)
    n_active = (padded_end[-1] // MOE_ROWS).astype(I32)
    blk = jnp.arange(n_blocks, dtype=I32)
    blk_exp = jnp.sum((blk * MOE_ROWS)[:, None] >= padded_end[None, :], axis=1)
    eid = jnp.arange(N_EXPERTS, dtype=I32)
    last_exp = jnp.max(jnp.where(padded > 0, eid, 0))
    blk_exp = jnp.where(blk < n_active, jnp.minimum(blk_exp, N_EXPERTS - 1), last_exp).astype(I32)
    later = jnp.logical_and(padded[None, :] > 0, eid[None, :] > eid[:, None])
    next_of = jnp.min(jnp.where(later, eid[None, :], N_EXPERTS), axis=1)
    next_of = jnp.where(next_of < N_EXPERTS, next_of, -1).astype(I32)
    of_blk = blk_exp[:, None] == eid[None, :]
    blk_next = jnp.sum(jnp.where(of_blk, next_of[None, :], 0), axis=1).astype(I32)
    blk_end = jnp.sum(jnp.where(of_blk, (start + counts)[None, :], 0), axis=1)
    blk_valid = jnp.clip(blk_end - blk * MOE_ROWS, 0, MOE_ROWS)
    blk_valid = jnp.where(blk < n_active, blk_valid, 0).astype(I32)

    run_n = tcnt.reshape(-1)
    run_local = (jnp.cumsum(tcnt, axis=1) - tcnt).reshape(-1)
    run_global = (start[None, :] + tbase).reshape(-1)
    xs = _dispatch(run_n, run_local, run_global, padded - counts, start + counts, n_active[None], u3, lp,
                   n_blocks * MOE_ROWS)

    b1p = b1.reshape(N_EXPERTS, 2 * D_FF // PAIR_COLS, LANES, 2).transpose(0, 1, 3, 2)
    ys = _experts(blk_exp, blk_next, blk_valid, n_active[None], xs, w1,
                  b1p.reshape(N_EXPERTS, 1, 2 * D_FF), w2, b2[:, None, :])
    return _combine(run_n, run_local, run_global, h2, gates, lp, g_final, ys)


def kernel(x, mem, g_mix, w_in, b_qkv, conv_w, conv_b, dt_bias, a_log, d_skip, g_ssm_out, attn_sinks, rel_bias, w_out, b_out, g_cross, g_mem, w_q_cross, w_kv_cross, w_o_cross, g_ffn, w_router, b_router, w1, b1, w2, b2, g_final):
    batch, seq, d = x.shape
    mem_len = mem.shape[1]
    depth = g_mix.shape[0]
    assert depth == 1 and d == D_MODEL and seq % TM_MID == 0
    h = x.reshape(batch * seq, d)
    mem2 = mem.reshape(batch * mem_len, d)
    bias = _bias_table(rel_bias)
    out = _layer(
        h, mem2, batch, seq, mem_len, g_mix[0], w_in[0], b_qkv[0], conv_w[0], conv_b[0], dt_bias[0],
        a_log[0], d_skip[0], g_ssm_out[0], attn_sinks[0], bias, w_out[0], b_out[0], g_cross[0],
        g_mem[0], w_q_cross[0], w_kv_cross[0], w_o_cross[0], g_ffn[0], w_router[0], b_router[0],
        w1[0], b1[0], w2[0], b2[0], g_final[None, :])
    return out.reshape(batch, seq, d)
```

```python
import math

import numpy as np
import jax
import jax.numpy as jnp
from jax import lax
from jax.experimental import pallas as pl
from jax.experimental.pallas import tpu as pltpu

F32 = jnp.float32
BF16 = jnp.bfloat16
I32 = jnp.int32

D_MODEL = 1024
RMS_EPS = 1e-5

SSM_HEADS = 8
SSM_HEAD_DIM = 64
SSM_WIDTH = SSM_HEADS * SSM_HEAD_DIM
SSM_STATE = 128
SSM_GROUPS = 2
HEADS_PER_GROUP = SSM_HEADS // SSM_GROUPS
GROUP_WIDTH = SSM_WIDTH // SSM_GROUPS
CONV_WIDTH = 4
CHUNK = 128
CONV_CH = SSM_WIDTH + 2 * SSM_GROUPS * SSM_STATE

ATTN_HEADS = 8
KV_HEADS = 2
Q_PER_KV = ATTN_HEADS // KV_HEADS
HEAD_DIM = 64
ATTN_WIDTH = ATTN_HEADS * HEAD_DIM
KV_WIDTH = KV_HEADS * HEAD_DIM
WINDOW = 128
BLOCK = WINDOW
QKV_COLS = ATTN_WIDTH + 2 * KV_WIDTH

REL_BUCKETS = 32
REL_MAX_DIST = 128

CROSS_HEADS = 4
CROSS_HEAD_DIM = 128
CROSS_WIDTH = CROSS_HEADS * CROSS_HEAD_DIM

N_EXPERTS = 32
TOP_K = 4
D_FF = D_MODEL
SWIGLU_ALPHA = 1.702
SWIGLU_LIMIT = 7.0

LANES = 128
SUBLANES = 8
DT_PAD = LANES
PROJ_COLS = SSM_WIDTH + CONV_CH + DT_PAD + QKV_COLS
VMEM_LIMIT = 56 * 1024 * 1024

TM_PROJ = 512
TM_MID = 512
MID_PARTS = 1
MOE_ROWS = 512
MOE_PART = 128
WEIGHT_DMA_ROWS = 256


def _nt(a, b):
    return lax.dot_general(a, b, (((1,), (1,)), ((), ())), preferred_element_type=F32)


def _tn(a, b):
    return lax.dot_general(a, b, (((0,), (0,)), ((), ())), preferred_element_type=F32)


def _dot(a, b):
    return jnp.dot(a, b, preferred_element_type=F32)


def _dot_exact(a, b):
    return jnp.dot(a, b, preferred_element_type=F32, precision=lax.Precision.HIGHEST)


def _rms(x, g):
    return x * lax.rsqrt(jnp.mean(x * x, axis=-1, keepdims=True) + RMS_EPS) * g


def _sigmoid(x):
    return 1.0 / (1.0 + jnp.exp(-x))


def _params(sem=None, **kwargs):
    return pltpu.CompilerParams(dimension_semantics=sem, vmem_limit_bytes=VMEM_LIMIT, **kwargs)


def _inproj_kernel(x_ref, g_ref, w_f32, bq_ref, z_ref, xbc_ref, dt_ref, qkv_ref, w_ref):
    @pl.when(pl.program_id(0) == 0)
    def _():
        c1 = SSM_WIDTH + CONV_CH
        w_ref[:, :c1] = w_f32[:, :c1].astype(BF16)
        lane = lax.broadcasted_iota(I32, (D_MODEL, DT_PAD), 1)
        w_ref[:, c1:c1 + DT_PAD] = jnp.where(lane < SSM_HEADS, w_f32[:, c1:c1 + DT_PAD], 0.0).astype(BF16)
        w_ref[:, c1 + DT_PAD:] = w_f32[:, c1 + SSM_HEADS:].astype(BF16)

    u = _rms(x_ref[...], g_ref[...])
    p = _dot(u.astype(BF16), w_ref[...])
    z_ref[...] = p[:, :SSM_WIDTH]
    xbc_ref[...] = p[:, SSM_WIDTH:SSM_WIDTH + CONV_CH]
    dt_ref[...] = p[:, SSM_WIDTH + CONV_CH:SSM_WIDTH + CONV_CH + DT_PAD]
    qkv_ref[...] = p[:, SSM_WIDTH + CONV_CH + DT_PAD:] + bq_ref[...]


def _inproj(x2, g_mix, w_in, b_qkv):
    n = x2.shape[0]
    row = lambda i: (i, 0)
    fixed = lambda i: (0, 0)
    return pl.pallas_call(
        _inproj_kernel,
        name="inproj",
        grid=(n // TM_PROJ,),
        in_specs=[
            pl.BlockSpec((TM_PROJ, D_MODEL), row),
            pl.BlockSpec((1, D_MODEL), fixed),
            pl.BlockSpec(w_in.shape, fixed),
            pl.BlockSpec((1, QKV_COLS), fixed),
        ],
        out_specs=[
            pl.BlockSpec((TM_PROJ, SSM_WIDTH), row),
            pl.BlockSpec((TM_PROJ, CONV_CH), row),
            pl.BlockSpec((TM_PROJ, DT_PAD), row),
            pl.BlockSpec((TM_PROJ, QKV_COLS), row),
        ],
        out_shape=[
            jax.ShapeDtypeStruct((n, SSM_WIDTH), F32),
            jax.ShapeDtypeStruct((n, CONV_CH), F32),
            jax.ShapeDtypeStruct((n, DT_PAD), F32),
            jax.ShapeDtypeStruct((n, QKV_COLS), F32),
        ],
        scratch_shapes=[pltpu.VMEM((D_MODEL, PROJ_COLS), BF16)],
        compiler_params=_params(("arbitrary",)),
    )(x2, g_mix, w_in, b_qkv)


def _split3(x):
    hi = x.astype(BF16)
    rest = x - hi.astype(F32)
    mid = rest.astype(BF16)
    lo = (rest - mid.astype(F32)).astype(BF16)
    return jnp.concatenate([hi, mid, lo], axis=1)


def _ssd_kernel(xbc_ref, z_ref, dt_ref, cw_ref, cb_ref, dtb_ref, ah_ref, dsk_ref, gout_ref,
                e64_ref, ecat_ref, tri_ref, y_ref, conv_ref, state_ref):
    c = pl.program_id(0)

    @pl.when(c == 0)
    def _():
        conv_ref[:, :SUBLANES, :] = jnp.zeros((conv_ref.shape[0], SUBLANES, CONV_CH), F32)
        state_ref[...] = jnp.zeros_like(state_ref)

    for b in range(xbc_ref.shape[0]):
        _ssd_chunk(b, xbc_ref, z_ref, dt_ref, cw_ref, cb_ref, dtb_ref, ah_ref, dsk_ref, gout_ref,
                   e64_ref, ecat_ref, tri_ref, y_ref, conv_ref, state_ref)


def _ssd_chunk(b, xbc_ref, z_ref, dt_ref, cw_ref, cb_ref, dtb_ref, ah_ref, dsk_ref, gout_ref,
               e64_ref, ecat_ref, tri_ref, y_ref, conv_ref, state_ref):
    u = xbc_ref[b]
    conv_ref[b, SUBLANES:, :] = u
    acc = u * cw_ref[CONV_WIDTH - 1:CONV_WIDTH, :] + cb_ref[...]
    for j in range(1, CONV_WIDTH):
        shifted = conv_ref[b, SUBLANES - j:SUBLANES - j + CHUNK, :]
        acc = acc + shifted * cw_ref[CONV_WIDTH - 1 - j:CONV_WIDTH - j, :]
    conv_ref[b, :SUBLANES, :] = u[CHUNK - SUBLANES:, :]
    xbc = acc * _sigmoid(acc)

    xs = xbc[:, :SSM_WIDTH]
    bm = xbc[:, SSM_WIDTH:SSM_WIDTH + SSM_GROUPS * SSM_STATE]
    cm = xbc[:, SSM_WIDTH + SSM_GROUPS * SSM_STATE:]

    dtr = dt_ref[b] + dtb_ref[...]
    dt = jnp.maximum(dtr, 0.0) + jnp.log1p(jnp.exp(-jnp.abs(dtr)))
    a = dt * ah_ref[...]
    a3 = _split3(a)
    a_rows = jnp.concatenate([a3[:, :DT_PAD], a3[:, DT_PAD:2 * DT_PAD], a3[:, 2 * DT_PAD:]], axis=0)
    cs = _dot(tri_ref[...], a_rows)
    cs_row = cs.T
    dt_full = _dot(_split3(dt), e64_ref[...])
    cs_cat = _dot(_split3(cs), ecat_ref[...])
    cs_wide = cs_cat[:, :SSM_HEADS * CHUNK]
    cs_full = cs_cat[:, SSM_HEADS * CHUNK:]
    cs_last = cs_full[CHUNK - 1:CHUNK, :]

    xdt = xs * dt_full
    xdt_b = xdt.astype(BF16)
    xw_b = (xdt * jnp.exp(cs_last - cs_full)).astype(BF16)
    in_decay = jnp.exp(cs_full)
    chunk_decay = jnp.exp(cs_last)

    li = lax.broadcasted_iota(I32, (CHUNK, CHUNK), 0)
    si = lax.broadcasted_iota(I32, (CHUNK, CHUNK), 1)
    causal = li >= si

    ys = []
    for g in range(SSM_GROUPS):
        bg = bm[:, g * SSM_STATE:(g + 1) * SSM_STATE].astype(BF16)
        cg = cm[:, g * SSM_STATE:(g + 1) * SSM_STATE].astype(BF16)
        cb = _nt(cg, bg)
        yd = []
        for r in range(HEADS_PER_GROUP):
            h = g * HEADS_PER_GROUP + r
            diff = cs_wide[:, h * CHUNK:(h + 1) * CHUNK] - cs_row[h:h + 1, :]
            m = cb * jnp.exp(jnp.where(causal, diff, -jnp.inf))
            yd.append(_dot(m.astype(BF16), xdt_b[:, h * SSM_HEAD_DIM:(h + 1) * SSM_HEAD_DIM]))
        y_diag = jnp.concatenate(yd, axis=1)
        gs = slice(g * GROUP_WIDTH, (g + 1) * GROUP_WIDTH)
        st = state_ref[b, g]
        y_off = _dot(cg, st.astype(BF16)) * in_decay[:, gs]
        state_ref[b, g] = st * chunk_decay[:, gs] + _tn(bg, xw_b[:, gs])
        ys.append(y_diag + y_off)
    y = jnp.concatenate(ys, axis=1) + dsk_ref[...] * xs

    zz = z_ref[b]
    y = y * (zz * _sigmoid(zz))
    outs = []
    for g in range(SSM_GROUPS):
        yg = y[:, g * GROUP_WIDTH:(g + 1) * GROUP_WIDTH]
        outs.append(yg * lax.rsqrt(jnp.mean(yg * yg, axis=-1, keepdims=True) + RMS_EPS))
    y_ref[b] = jnp.concatenate(outs, axis=1) * gout_ref[...]


def _ssd(xbc, z, dtp, conv_w, conv_b, dt_bias_p, a_head_p, d_full, g_out):
    batch, seq, _ = xbc.shape
    chunk = lambda c: (0, c, 0)
    fixed = lambda c: (0, 0)
    head = np.arange(SSM_WIDTH) // SSM_HEAD_DIM
    e64 = (np.arange(DT_PAD)[:, None] == head[None, :]).astype(np.float32)
    head_w = np.arange(SSM_HEADS * CHUNK) // CHUNK
    e128 = (np.arange(DT_PAD)[:, None] == head_w[None, :]).astype(np.float32)
    e64_3 = jnp.asarray(np.tile(e64, (3, 1))).astype(BF16)
    ecat_3 = jnp.asarray(np.tile(np.concatenate([e128, e64], axis=1), (3, 1))).astype(BF16)
    tri_3 = jnp.asarray(np.tile(np.tril(np.ones((CHUNK, CHUNK), np.float32)), (1, 3))).astype(BF16)
    return pl.pallas_call(
        _ssd_kernel,
        name="ssd",
        grid=(seq // CHUNK,),
        in_specs=[
            pl.BlockSpec((batch, CHUNK, CONV_CH), chunk),
            pl.BlockSpec((batch, CHUNK, SSM_WIDTH), chunk),
            pl.BlockSpec((batch, CHUNK, DT_PAD), chunk),
            pl.BlockSpec((CONV_WIDTH, CONV_CH), fixed),
            pl.BlockSpec((1, CONV_CH), fixed),
            pl.BlockSpec((1, DT_PAD), fixed),
            pl.BlockSpec((1, DT_PAD), fixed),
            pl.BlockSpec((1, SSM_WIDTH), fixed),
            pl.BlockSpec((1, SSM_WIDTH), fixed),
            pl.BlockSpec((3 * DT_PAD, SSM_WIDTH), fixed),
            pl.BlockSpec((3 * DT_PAD, SSM_HEADS * CHUNK + SSM_WIDTH), fixed),
            pl.BlockSpec((CHUNK, 3 * CHUNK), fixed),
        ],
        out_specs=pl.BlockSpec((batch, CHUNK, SSM_WIDTH), chunk),
        out_shape=jax.ShapeDtypeStruct((batch, seq, SSM_WIDTH), F32),
        scratch_shapes=[
            pltpu.VMEM((batch, SUBLANES + CHUNK, CONV_CH), F32),
            pltpu.VMEM((batch, SSM_GROUPS, SSM_STATE, GROUP_WIDTH), F32),
        ],
        compiler_params=_params(("arbitrary",)),
    )(xbc, z, dtp, conv_w, conv_b, dt_bias_p, a_head_p, d_full, g_out, e64_3, ecat_3, tri_3)


def _t5_bucket_table():
    q_pos = np.arange(BLOCK)[:, None]
    k_pos = np.arange(2 * BLOCK)[None, :] - BLOCK
    dist = q_pos - k_pos
    d = np.maximum(dist, 0)
    max_exact = REL_BUCKETS // 2
    ratio = np.maximum(d, max_exact).astype(np.float32) / np.float32(max_exact)
    large = max_exact + (np.log(ratio) / np.float32(math.log(REL_MAX_DIST / max_exact))
                         * np.float32(REL_BUCKETS - max_exact)).astype(np.int32)
    large = np.minimum(large, REL_BUCKETS - 1)
    bucket = np.where(d < max_exact, d, large)
    in_window = (dist >= 0) & (dist < WINDOW)
    return np.where(in_window, bucket, -1).astype(np.int32)


def _bias_kernel(rb_ref, bucket_ref, out_ref):
    bucket = bucket_ref[...]
    for h in range(ATTN_HEADS):
        acc = jnp.zeros(bucket.shape, F32)
        for b in range(REL_BUCKETS):
            acc = jnp.where(bucket == b, rb_ref[b, h], acc)
        out_ref[h] = jnp.where(bucket >= 0, acc, -jnp.inf)


def _bias_table(rel_bias):
    bucket = jnp.asarray(_t5_bucket_table())
    return pl.pallas_call(
        _bias_kernel,
        in_specs=[
            pl.BlockSpec(memory_space=pltpu.SMEM),
            pl.BlockSpec(memory_space=pltpu.VMEM),
        ],
        out_specs=pl.BlockSpec(memory_space=pltpu.VMEM),
        out_shape=jax.ShapeDtypeStruct((ATTN_HEADS, BLOCK, 2 * BLOCK), F32),
    )(rel_bias, bucket)


def _swa_kernel(sink_ref, q_ref, k_ref, v_ref, kp_ref, vp_ref, bias_ref, o_ref):
    i = pl.program_id(0)
    col = lax.broadcasted_iota(I32, (BLOCK, 2 * BLOCK), 1)
    key_ok = jnp.logical_or(col >= BLOCK, i > 0)
    for b in range(q_ref.shape[0]):
        q = (q_ref[b] * (HEAD_DIM ** -0.5)).astype(BF16)
        outs = []
        for g in range(KV_HEADS):
            gs = slice(g * HEAD_DIM, (g + 1) * HEAD_DIM)
            k2 = jnp.concatenate([kp_ref[b, :, gs], k_ref[b, :, gs]], axis=0).astype(BF16)
            v2 = jnp.concatenate([vp_ref[b, :, gs], v_ref[b, :, gs]], axis=0).astype(BF16)
            for r in range(Q_PER_KV):
                h = g * Q_PER_KV + r
                s = _nt(q[:, h * HEAD_DIM:(h + 1) * HEAD_DIM], k2) + bias_ref[h]
                s = jnp.where(key_ok, s, -jnp.inf)
                sink = sink_ref[h]
                m = jnp.maximum(jnp.max(s, axis=-1, keepdims=True), sink)
                p = jnp.exp(s - m)
                denom = jnp.sum(p, axis=-1, keepdims=True) + jnp.exp(sink - m)
                outs.append(_dot(p.astype(BF16), v2) / denom)
        o_ref[b] = jnp.concatenate(outs, axis=1)


def _swa(qkv, sinks, bias):
    batch, seq, _ = qkv.shape
    kcol = ATTN_WIDTH // KV_WIDTH
    prev = lambda i: jnp.maximum(i - 1, 0)
    return pl.pallas_call(
        _swa_kernel,
        name="swa",
        grid_spec=pltpu.PrefetchScalarGridSpec(
            num_scalar_prefetch=1,
            grid=(seq // BLOCK,),
            in_specs=[
                pl.BlockSpec((batch, BLOCK, ATTN_WIDTH), lambda i, s: (0, i, 0)),
                pl.BlockSpec((batch, BLOCK, KV_WIDTH), lambda i, s: (0, i, kcol)),
                pl.BlockSpec((batch, BLOCK, KV_WIDTH), lambda i, s: (0, i, kcol + 1)),
                pl.BlockSpec((batch, BLOCK, KV_WIDTH), lambda i, s: (0, prev(i), kcol)),
                pl.BlockSpec((batch, BLOCK, KV_WIDTH), lambda i, s: (0, prev(i), kcol + 1)),
                pl.BlockSpec((ATTN_HEADS, BLOCK, 2 * BLOCK), lambda i, s: (0, 0, 0)),
            ],
            out_specs=pl.BlockSpec((batch, BLOCK, ATTN_WIDTH), lambda i, s: (0, i, 0)),
        ),
        out_shape=jax.ShapeDtypeStruct((batch, seq, ATTN_WIDTH), F32),
        compiler_params=_params(("arbitrary",)),
    )(sinks, qkv, qkv, qkv, qkv, qkv, bias)


def _memkv_kernel(m_ref, g_ref, w_ref, k_ref, v_ref):
    u = _rms(m_ref[...], g_ref[...])
    kv = _dot(u.astype(BF16), w_ref[...].astype(BF16))
    k_ref[...] = kv[:, :CROSS_WIDTH].astype(BF16)
    v_ref[...] = kv[:, CROSS_WIDTH:].astype(BF16)


def _memkv(mem2, g_mem, w_kv):
    n = mem2.shape[0]
    return pl.pallas_call(
        _memkv_kernel,
        out_shape=[jax.ShapeDtypeStruct((n, CROSS_WIDTH), BF16)] * 2,
        compiler_params=_params(),
    )(mem2, g_mem, w_kv)


def _mid_kernel(ys_ref, ya_ref, x_ref, wout_f32, bout_ref, gc_ref, wq_f32, k_ref, v_ref, wo_f32,
                gf_ref, wr_ref, br_ref, upper_ref, lower_ref,
                h2_ref, u3_ref, gate_ref, lp_ref, tcnt_ref, tbase_ref, run_ref, wout_ref, wq_ref, wo_ref):
    i = pl.program_id(0)

    @pl.when(i == 0)
    def _():
        run_ref[...] = jnp.zeros_like(run_ref)
        wout_ref[...] = wout_f32[...].astype(BF16)
        wq_ref[...] = wq_f32[...].astype(BF16)
        wo_ref[...] = wo_f32[...].astype(BF16)

    u3_parts = []
    for part in range(MID_PARTS):
        rows = slice(part * (TM_MID // MID_PARTS), (part + 1) * (TM_MID // MID_PARTS))
        ycat = jnp.concatenate([ys_ref[rows, :], ya_ref[rows, :]], axis=1).astype(BF16)
        h1 = x_ref[rows, :] + _dot(ycat, wout_ref[...]) + bout_ref[...]

        u2 = _rms(h1, gc_ref[...])
        q = _dot(u2.astype(BF16), wq_ref[...]).astype(BF16)
        outs = []
        for h in range(CROSS_HEADS):
            hs = slice(h * CROSS_HEAD_DIM, (h + 1) * CROSS_HEAD_DIM)
            s = _nt(q[:, hs], k_ref[:, hs]) * (CROSS_HEAD_DIM ** -0.5)
            m = jnp.max(s, axis=-1, keepdims=True)
            p = jnp.exp(s - m)
            outs.append(_dot(p.astype(BF16), v_ref[:, hs]) / jnp.sum(p, axis=-1, keepdims=True))
        o = jnp.concatenate(outs, axis=1).astype(BF16)
        h2 = h1 + _dot(o, wo_ref[...])
        h2_ref[rows, :] = h2
        u3_parts.append(_rms(h2, gf_ref[...]).astype(BF16))
    u3 = jnp.concatenate(u3_parts, axis=0)
    u3_ref[...] = u3

    logits = _nt(wr_ref[...], u3) + br_ref[...]
    eid = lax.broadcasted_iota(I32, logits.shape, 0)
    vals, idxs, hots = [], [], []
    for _ in range(TOP_K):
        m = jnp.max(logits, axis=0, keepdims=True)
        ix = jnp.min(jnp.where(logits == m, eid, N_EXPERTS), axis=0, keepdims=True)
        hot = eid == ix
        logits = jnp.where(hot, -jnp.inf, logits)
        vals.append(m)
        idxs.append(ix)
        hots.append(hot)
    ex = [jnp.exp(v - vals[0]) for v in vals]
    tot = ex[0] + ex[1] + ex[2] + ex[3]
    gate_ref[...] = jnp.concatenate([e / tot for e in ex], axis=0)

    hot_f = [jnp.where(hot, 1.0, 0.0) for hot in hots]
    cnt_k = [jnp.sum(hf, axis=1, keepdims=True) for hf in hot_f]
    tile_cnt = cnt_k[0] + cnt_k[1] + cnt_k[2] + cnt_k[3]
    tile_cnt = jnp.floor((tile_cnt + (SUBLANES - 1)) * (1.0 / SUBLANES)) * SUBLANES
    offs = _dot_exact(lower_ref[...], jnp.broadcast_to(tile_cnt, (N_EXPERTS, LANES)))[:, :1]
    local = []
    before_all = _dot(jnp.concatenate(hot_f, axis=0).astype(BF16), upper_ref[...])
    for kk in range(TOP_K):
        before = before_all[kk * N_EXPERTS:(kk + 1) * N_EXPERTS, :]
        local.append(jnp.sum(jnp.where(hots[kk], before + offs, 0.0), axis=0, keepdims=True))
        offs = offs + cnt_k[kk]
    lp_ref[...] = jnp.concatenate(local, axis=0).astype(I32)
    tcnt_ref[0] = jnp.broadcast_to(tile_cnt, (N_EXPERTS, LANES)).astype(I32)
    tbase_ref[0] = jnp.broadcast_to(run_ref[...], (N_EXPERTS, LANES)).astype(I32)
    run_ref[...] = run_ref[...] + tile_cnt


def _mid(y_ssm, y_attn, x2, w_out, b_out, g_cross, w_q, k_mem, v_mem, w_o, g_ffn, w_r_t, b_r, seq, mem_len):
    n = x2.shape[0]
    tiles_per_batch = seq // TM_MID
    row = lambda i: (i, 0)
    col = lambda i: (0, i)
    fixed = lambda i: (0, 0)
    memb = lambda i: (i // tiles_per_batch, 0)
    tile3 = lambda i: (i, 0, 0)
    n_tiles = n // TM_MID
    upper = jnp.asarray(np.triu(np.ones((TM_MID, TM_MID), np.float32), 1)).astype(BF16)
    lower = jnp.asarray(np.tril(np.ones((N_EXPERTS, N_EXPERTS), np.float32), -1))
    return pl.pallas_call(
        _mid_kernel,
        name="mid",
        grid=(n_tiles,),
        in_specs=[
            pl.BlockSpec((TM_MID, SSM_WIDTH), row),
            pl.BlockSpec((TM_MID, ATTN_WIDTH), row),
            pl.BlockSpec((TM_MID, D_MODEL), row),
            pl.BlockSpec((D_MODEL, D_MODEL), fixed),
            pl.BlockSpec((1, D_MODEL), fixed),
            pl.BlockSpec((1, D_MODEL), fixed),
            pl.BlockSpec((D_MODEL, CROSS_WIDTH), fixed),
            pl.BlockSpec((mem_len, CROSS_WIDTH), memb),
            pl.BlockSpec((mem_len, CROSS_WIDTH), memb),
            pl.BlockSpec((CROSS_WIDTH, D_MODEL), fixed),
            pl.BlockSpec((1, D_MODEL), fixed),
            pl.BlockSpec((N_EXPERTS, D_MODEL), fixed),
            pl.BlockSpec((N_EXPERTS, 1), fixed),
            pl.BlockSpec((TM_MID, TM_MID), fixed),
            pl.BlockSpec((N_EXPERTS, N_EXPERTS), fixed),
        ],
        out_specs=[
            pl.BlockSpec((TM_MID, D_MODEL), row),
            pl.BlockSpec((TM_MID, D_MODEL), row),
            pl.BlockSpec((TOP_K, TM_MID), col),
            pl.BlockSpec((TOP_K, TM_MID), col),
            pl.BlockSpec((1, N_EXPERTS, LANES), tile3),
            pl.BlockSpec((1, N_EXPERTS, LANES), tile3),
        ],
        out_shape=[
            jax.ShapeDtypeStruct((n, D_MODEL), F32),
            jax.ShapeDtypeStruct((n, D_MODEL), BF16),
            jax.ShapeDtypeStruct((TOP_K, n), F32),
            jax.ShapeDtypeStruct((TOP_K, n), I32),
            jax.ShapeDtypeStruct((n_tiles, N_EXPERTS, LANES), I32),
            jax.ShapeDtypeStruct((n_tiles, N_EXPERTS, LANES), I32),
        ],
        scratch_shapes=[
            pltpu.VMEM((N_EXPERTS, 1), F32),
            pltpu.VMEM((D_MODEL, D_MODEL), BF16),
            pltpu.VMEM((D_MODEL, CROSS_WIDTH), BF16),
            pltpu.VMEM((CROSS_WIDTH, D_MODEL), BF16),
        ],
        compiler_params=_params(("arbitrary",)),
    )(y_ssm, y_attn, x2, w_out, b_out, g_cross, w_q, k_mem, v_mem, w_o, g_ffn, w_r_t, b_r, upper, lower)


ROW_CHUNK = 256
LOCAL_ROWS = -(-(TM_MID * TOP_K + N_EXPERTS * (SUBLANES - 1)) // ROW_CHUNK) * ROW_CHUNK
RUN_BITS = tuple(1 << s for s in range(TM_MID.bit_length() - 1, SUBLANES.bit_length() - 2, -1))
PAD_BITS = tuple(1 << s for s in range((MOE_ROWS - 1).bit_length() - 1, SUBLANES.bit_length() - 2, -1))


U32 = jnp.uint32
PACKED = D_MODEL // 2


def _pack_pairs(v):
    bits = lax.bitcast_convert_type(v, U32)
    return bits[:, PACKED:] | (bits[:, :PACKED] >> 16)


def _unpack_pairs(w):
    lo = lax.bitcast_convert_type(w << 16, F32)
    hi = lax.bitcast_convert_type(w & jnp.uint32(0xFFFF0000), F32)
    return jnp.concatenate([lo, hi], axis=1).astype(BF16)


def _run_copies(n, src_ref, src, dst_ref, dst, sem, bits, wait, advance_src=True):
    for bit in bits:
        step = n & bit

        @pl.when(step != 0)
        def _():
            cp = pltpu.make_async_copy(src_ref.at[pl.ds(pl.multiple_of(src, SUBLANES), bit)],
                                       dst_ref.at[pl.ds(pl.multiple_of(dst, SUBLANES), bit)], sem)
            if wait:
                cp.wait()
            else:
                cp.start()

        if advance_src:
            src = src + step
        dst = dst + step


def _wait_rows(n, src_ref, dst_ref, sem):
    rows = pl.ds(0, pl.multiple_of(n, SUBLANES))
    pltpu.make_async_copy(src_ref.at[rows], dst_ref.at[rows], sem).wait()


def _dispatch_kernel(tcnt_ref, lo_ref, gs_ref, padn_ref, pads_ref, na_ref, u_ref, lp_ref, xs_ref,
                     xloc_ref, zero_ref, sem, zsem):
    i = pl.program_id(0)

    def zero_fill(wait):
        def body(e, carry):
            _run_copies(padn_ref[e], zero_ref, 0, xs_ref, pads_ref[e], zsem, PAD_BITS, wait, advance_src=False)
            return carry
        lax.fori_loop(0, N_EXPERTS, body, 0)

        def tail(b, carry):
            cp = pltpu.make_async_copy(zero_ref, xs_ref.at[pl.ds(pl.multiple_of(b * MOE_ROWS, MOE_ROWS), MOE_ROWS)], zsem)
            if wait:
                cp.wait()
            else:
                cp.start()
            return carry
        lax.fori_loop(na_ref[0], xs_ref.shape[0] // MOE_ROWS, tail, 0)

    @pl.when(i == 0)
    def _():
        zero_ref[...] = jnp.zeros_like(zero_ref)
        zero_fill(False)

    slot = i % 2
    u = u_ref[...]
    lp = lp_ref[...]
    for c in range(LOCAL_ROWS // ROW_CHUNK):
        r = lax.broadcasted_iota(I32, (ROW_CHUNK, TM_MID), 0) + c * ROW_CHUNK
        p = jnp.zeros((ROW_CHUNK, TM_MID), F32)
        for kk in range(TOP_K):
            p = jnp.where(lp[kk:kk + 1, :] == r, 1.0, p)
        xloc_ref[slot, c * ROW_CHUNK:(c + 1) * ROW_CHUNK, :] = _pack_pairs(_dot(p.astype(BF16), u))

    def start_runs(tile):
        s = tile % 2

        def body(e, carry):
            j = tile * N_EXPERTS + e
            _run_copies(tcnt_ref[j], xloc_ref.at[s], lo_ref[j], xs_ref, gs_ref[j], sem.at[s], RUN_BITS, False)
            return carry
        lax.fori_loop(0, N_EXPERTS, body, 0)

    def wait_runs(tile):
        s = tile % 2
        last = tile * N_EXPERTS + N_EXPERTS - 1
        _wait_rows(lo_ref[last] + tcnt_ref[last], xloc_ref.at[s], xs_ref, sem.at[s])

    start_runs(i)

    @pl.when(i > 0)
    def _():
        wait_runs(i - 1)

    @pl.when(i == pl.num_programs(0) - 1)
    def _():
        wait_runs(i)

    @pl.when(i == 0)
    def _():
        zero_fill(True)


def _dispatch(tcnt, lo, gstart, padn, pads, n_active, u3, lp, n_rows):
    n = u3.shape[0]
    return pl.pallas_call(
        _dispatch_kernel,
        name="dispatch",
        grid_spec=pltpu.PrefetchScalarGridSpec(
            num_scalar_prefetch=6,
            grid=(n // TM_MID,),
            in_specs=[
                pl.BlockSpec((TM_MID, D_MODEL), lambda i, *_: (i, 0)),
                pl.BlockSpec((TOP_K, TM_MID), lambda i, *_: (0, i)),
            ],
            out_specs=pl.BlockSpec(memory_space=pl.ANY),
            scratch_shapes=[
                pltpu.VMEM((2, LOCAL_ROWS, PACKED), U32),
                pltpu.VMEM((MOE_ROWS, PACKED), U32),
                pltpu.SemaphoreType.DMA((2,)),
                pltpu.SemaphoreType.DMA,
            ],
        ),
        out_shape=jax.ShapeDtypeStruct((n_rows, PACKED), U32),
        compiler_params=_params(("arbitrary",)),
    )(tcnt, lo, gstart, padn, pads, n_active, u3, lp)


PAIR_COLS = 2 * LANES


def _expert_kernel(be_ref, nx_ref, nv_ref, na_ref, x_ref, b1_ref, b2_ref, perm_ref, w1_hbm, w2_hbm, y_ref,
                   w1s_ref, w2s_ref, w1p_ref, w2b_ref, sem):
    b = pl.program_id(0)
    valid_rows = nv_ref[b]
    active = valid_rows > 0
    expert = be_ref[b]
    new_expert = jnp.logical_or(b == 0, expert != be_ref[jnp.maximum(b - 1, 0)])

    def fetch(e, wait):
        for src, dst, s in ((w1_hbm, w1s_ref, 0), (w2_hbm, w2s_ref, 1)):
            for r in range(0, src.shape[1], WEIGHT_DMA_ROWS):
                rs = pl.ds(r, WEIGHT_DMA_ROWS)
                cp = pltpu.make_async_copy(src.at[e, rs], dst.at[rs], sem.at[s])
                if wait:
                    cp.wait()
                else:
                    cp.start(priority=1)

    @pl.when(jnp.logical_and(b == 0, active))
    def _():
        fetch(expert, False)

    @pl.when(jnp.logical_and(active, new_expert))
    def _():
        fetch(expert, True)
        for c in range(2 * D_FF // PAIR_COLS):
            cs = slice(c * PAIR_COLS, (c + 1) * PAIR_COLS)
            w1p_ref[:, cs] = _dot(w1s_ref[:, cs].astype(BF16), perm_ref[...]).astype(BF16)
        w2b_ref[...] = w2s_ref[...].astype(BF16)

        @pl.when(nx_ref[b] >= 0)
        def _():
            fetch(nx_ref[b], False)

    for live in range(MOE_PART, MOE_ROWS + 1, MOE_PART):
        @pl.when(jnp.logical_and(valid_rows > live - MOE_PART, valid_rows <= live))
        def _():
            x = _unpack_pairs(x_ref[:live, :])
            hdn = _dot(x, w1p_ref[...]) + b1_ref[0]
            acts = []
            for c in range(2 * D_FF // PAIR_COLS):
                hg = jnp.minimum(hdn[:, c * PAIR_COLS:c * PAIR_COLS + LANES], SWIGLU_LIMIT)
                hl = jnp.clip(hdn[:, c * PAIR_COLS + LANES:(c + 1) * PAIR_COLS], -SWIGLU_LIMIT, SWIGLU_LIMIT)
                acts.append((hg * _sigmoid(SWIGLU_ALPHA * hg) * (hl + 1.0)).astype(BF16))
            y = _dot(jnp.concatenate(acts, axis=1), w2b_ref[...]) + b2_ref[0]
            y_ref[:live, :] = _pack_pairs(y.astype(BF16).astype(F32))
            if live < MOE_ROWS:
                y_ref[live:, :] = jnp.zeros((MOE_ROWS - live, PACKED), U32)

    @pl.when(jnp.logical_not(active))
    def _():
        y_ref[...] = jnp.zeros_like(y_ref)


def _experts(block_exp, next_exp, valid_rows, n_active, xs, w1, b1p, w2, b2):
    n_rows = xs.shape[0]
    n_blocks = n_rows // MOE_ROWS
    xmap = lambda b, be, nx, nv, na: (jnp.maximum(jnp.minimum(b, na[0] - 1), 0), 0)
    emap = lambda b, be, nx, nv, na: (be[b], 0, 0)
    perm = np.zeros((PAIR_COLS, PAIR_COLS), np.float32)
    perm[2 * np.arange(LANES), np.arange(LANES)] = 1.0
    perm[2 * np.arange(LANES) + 1, LANES + np.arange(LANES)] = 1.0
    return pl.pallas_call(
        _expert_kernel,
        name="experts",
        grid_spec=pltpu.PrefetchScalarGridSpec(
            num_scalar_prefetch=4,
            grid=(n_blocks,),
            in_specs=[
                pl.BlockSpec((MOE_ROWS, PACKED), xmap),
                pl.BlockSpec((1, 1, 2 * D_FF), emap),
                pl.BlockSpec((1, 1, D_MODEL), emap),
                pl.BlockSpec((PAIR_COLS, PAIR_COLS), lambda b, *_: (0, 0)),
                pl.BlockSpec(memory_space=pl.ANY),
                pl.BlockSpec(memory_space=pl.ANY),
            ],
            out_specs=pl.BlockSpec((MOE_ROWS, PACKED), lambda b, *_: (b, 0)),
            scratch_shapes=[
                pltpu.VMEM((D_MODEL, 2 * D_FF), F32),
                pltpu.VMEM((D_FF, D_MODEL), F32),
                pltpu.VMEM((D_MODEL, 2 * D_FF), BF16),
                pltpu.VMEM((D_FF, D_MODEL), BF16),
                pltpu.SemaphoreType.DMA((2,)),
            ],
        ),
        out_shape=jax.ShapeDtypeStruct((n_rows, PACKED), U32),
        compiler_params=_params(("arbitrary",)),
    )(block_exp, next_exp, valid_rows, n_active, xs, b1p, b2, jnp.asarray(perm).astype(BF16), w1, w2)


def _combine_kernel(tcnt_ref, lo_ref, gs_ref, h2_ref, gate_ref, lp_ref, g_ref, ys_ref, o_ref, yloc_ref, sem):
    i = pl.program_id(0)

    slot = i % 2

    def start_runs(tile):
        s = tile % 2

        def body(e, carry):
            j = tile * N_EXPERTS + e
            _run_copies(tcnt_ref[j], ys_ref, gs_ref[j], yloc_ref.at[s], lo_ref[j], sem.at[s], RUN_BITS, False)
            return carry
        lax.fori_loop(0, N_EXPERTS, body, 0)

    @pl.when(i == 0)
    def _():
        yloc_ref[...] = jnp.zeros_like(yloc_ref)
        start_runs(i)

    @pl.when(i + 1 < pl.num_programs(0))
    def _():
        start_runs(i + 1)

    gates = gate_ref[...]
    lp = lp_ref[...]
    last = i * N_EXPERTS + N_EXPERTS - 1
    _wait_rows(lo_ref[last] + tcnt_ref[last], ys_ref, yloc_ref.at[slot], sem.at[slot])

    h = h2_ref[...]
    for c in range(LOCAL_ROWS // ROW_CHUNK):
        r = lax.broadcasted_iota(I32, (ROW_CHUNK, TM_MID), 0) + c * ROW_CHUNK
        a = jnp.zeros((ROW_CHUNK, TM_MID), F32)
        for kk in range(TOP_K):
            a = jnp.where(lp[kk:kk + 1, :] == r, gates[kk:kk + 1, :], a)
        h = h + _tn(a.astype(BF16), _unpack_pairs(yloc_ref[slot, c * ROW_CHUNK:(c + 1) * ROW_CHUNK, :]))
    o_ref[...] = _rms(h, g_ref[...])


def _combine(tcnt, lo, gstart, h2, gates, lp, g_final, ys):
    n = h2.shape[0]
    return pl.pallas_call(
        _combine_kernel,
        name="combine",
        grid_spec=pltpu.PrefetchScalarGridSpec(
            num_scalar_prefetch=3,
            grid=(n // TM_MID,),
            in_specs=[
                pl.BlockSpec((TM_MID, D_MODEL), lambda i, *_: (i, 0)),
                pl.BlockSpec((TOP_K, TM_MID), lambda i, *_: (0, i)),
                pl.BlockSpec((TOP_K, TM_MID), lambda i, *_: (0, i)),
                pl.BlockSpec((1, D_MODEL), lambda i, *_: (0, 0)),
                pl.BlockSpec(memory_space=pl.ANY),
            ],
            out_specs=pl.BlockSpec((TM_MID, D_MODEL), lambda i, *_: (i, 0)),
            scratch_shapes=[
                pltpu.VMEM((2, LOCAL_ROWS, PACKED), U32),
                pltpu.SemaphoreType.DMA((2,)),
            ],
        ),
        out_shape=jax.ShapeDtypeStruct((n, D_MODEL), F32),
        compiler_params=_params(("arbitrary",)),
    )(tcnt, lo, gstart, h2, gates, lp, g_final, ys)


def _layer(h, mem2, batch, seq, mem_len, g_mix, w_in, b_qkv, conv_w, conv_b, dt_bias, a_log, d_skip,
           g_ssm_out, attn_sinks, bias, w_out, b_out, g_cross, g_mem, w_q_cross, w_kv_cross, w_o_cross,
           g_ffn, w_router, b_router, w1, b1, w2, b2, g_final):
    n = h.shape[0]
    z, xbc, dtp, qkv = _inproj(h, g_mix[None, :], w_in, b_qkv[None, :])

    pad_h = (0, DT_PAD - SSM_HEADS)
    dt_bias_p = jnp.pad(dt_bias, pad_h)[None, :]
    a_head_p = jnp.pad(-jnp.exp(a_log), pad_h)[None, :]
    d_full = jnp.repeat(d_skip, SSM_HEAD_DIM)[None, :]
    per_seq = lambda t: t.reshape(batch, seq, t.shape[-1])
    y_ssm = _ssd(per_seq(xbc), per_seq(z), per_seq(dtp), conv_w, conv_b[None, :], dt_bias_p, a_head_p,
                 d_full, g_ssm_out[None, :]).reshape(n, SSM_WIDTH)
    y_attn = _swa(per_seq(qkv), attn_sinks, bias).reshape(n, ATTN_WIDTH)

    k_mem, v_mem = _memkv(mem2, g_mem[None, :], w_kv_cross)
    h2, u3, gates, lp, tcnt, tbase = _mid(
        y_ssm, y_attn, h, w_out, b_out[None, :], g_cross[None, :],
        w_q_cross, k_mem, v_mem, w_o_cross, g_ffn[None, :],
        w_router.T.astype(BF16), b_router[:, None], seq, mem_len)

    max_rows = n * TOP_K + (n // TM_MID) * N_EXPERTS * (SUBLANES - 1)
    n_blocks = -(-max_rows // MOE_ROWS) + N_EXPERTS
    tcnt = tcnt[:, :, 0]
    tbase = tbase[:, :, 0]
    counts = tbase[-1] + tcnt[-1]
    padded = (counts + MOE_ROWS - 1) // MOE_ROWS * MOE_ROWS
    padded_end = jnp.cumsum(padded)
    start = (padded_end - padded).astype(I32)
    n_active = (padded_end[-1] // MOE_ROWS).astype(I32)
    blk = jnp.arange(n_blocks, dtype=I32)
    blk_exp = jnp.sum((blk * MOE_ROWS)[:, None] >= padded_end[None, :], axis=1)
    eid = jnp.arange(N_EXPERTS, dtype=I32)
    last_exp = jnp.max(jnp.where(padded > 0, eid, 0))
    blk_exp = jnp.where(blk < n_active, jnp.minimum(blk_exp, N_EXPERTS - 1), last_exp).astype(I32)
    later = jnp.logical_and(padded[None, :] > 0, eid[None, :] > eid[:, None])
    next_of = jnp.min(jnp.where(later, eid[None, :], N_EXPERTS), axis=1)
    next_of = jnp.where(next_of < N_EXPERTS, next_of, -1).astype(I32)
    of_blk = blk_exp[:, None] == eid[None, :]
    blk_next = jnp.sum(jnp.where(of_blk, next_of[None, :], 0), axis=1).astype(I32)
    blk_end = jnp.sum(jnp.where(of_blk, (start + counts)[None, :], 0), axis=1)
    blk_valid = jnp.clip(blk_end - blk * MOE_ROWS, 0, MOE_ROWS)
    blk_valid = jnp.where(blk < n_active, blk_valid, 0).astype(I32)

    run_n = tcnt.reshape(-1)
    run_local = (jnp.cumsum(tcnt, axis=1) - tcnt).reshape(-1)
    run_global = (start[None, :] + tbase).reshape(-1)
    xs = _dispatch(run_n, run_local, run_global, padded - counts, start + counts, n_active[None], u3, lp,
                   n_blocks * MOE_ROWS)

    b1p = b1.reshape(N_EXPERTS, 2 * D_FF // PAIR_COLS, LANES, 2).transpose(0, 1, 3, 2)
    ys = _experts(blk_exp, blk_next, blk_valid, n_active[None], xs, w1,
                  b1p.reshape(N_EXPERTS, 1, 2 * D_FF), w2, b2[:, None, :])
    return _combine(run_n, run_local, run_global, h2, gates, lp, g_final, ys)


def kernel(x, mem, g_mix, w_in, b_qkv, conv_w, conv_b, dt_bias, a_log, d_skip, g_ssm_out, attn_sinks, rel_bias, w_out, b_out, g_cross, g_mem, w_q_cross, w_kv_cross, w_o_cross, g_ffn, w_router, b_router, w1, b1, w2, b2, g_final):
    batch, seq, d = x.shape
    mem_len = mem.shape[1]
    depth = g_mix.shape[0]
    assert depth == 1 and d == D_MODEL and seq % TM_MID == 0
    h = x.reshape(batch * seq, d)
    mem2 = mem.reshape(batch * mem_len, d)
    bias = _bias_table(rel_bias)
    out = _layer(
        h, mem2, batch, seq, mem_len, g_mix[0], w_in[0], b_qkv[0], conv_w[0], conv_b[0], dt_bias[0],
        a_log[0], d_skip[0], g_ssm_out[0], attn_sinks[0], bias, w_out[0], b_out[0], g_cross[0],
        g_mem[0], w_q_cross[0], w_kv_cross[0], w_o_cross[0], g_ffn[0], w_router[0], b_router[0],
        w1[0], b1[0], w2[0], b2[0], g_final[None, :])
    return out.reshape(batch, seq, d)
```

```python
import math

import numpy as np
import jax
import jax.numpy as jnp
from jax import lax
from jax.experimental import pallas as pl
from jax.experimental.pallas import tpu as pltpu

F32 = jnp.float32
BF16 = jnp.bfloat16
I32 = jnp.int32

D_MODEL = 1024
RMS_EPS = 1e-5

SSM_HEADS = 8
SSM_HEAD_DIM = 64
SSM_WIDTH = SSM_HEADS * SSM_HEAD_DIM
SSM_STATE = 128
SSM_GROUPS = 2
HEADS_PER_GROUP = SSM_HEADS // SSM_GROUPS
GROUP_WIDTH = SSM_WIDTH // SSM_GROUPS
CONV_WIDTH = 4
CHUNK = 128
CONV_CH = SSM_WIDTH + 2 * SSM_GROUPS * SSM_STATE

ATTN_HEADS = 8
KV_HEADS = 2
Q_PER_KV = ATTN_HEADS // KV_HEADS
HEAD_DIM = 64
ATTN_WIDTH = ATTN_HEADS * HEAD_DIM
KV_WIDTH = KV_HEADS * HEAD_DIM
WINDOW = 128
BLOCK = WINDOW
QKV_COLS = ATTN_WIDTH + 2 * KV_WIDTH

REL_BUCKETS = 32
REL_MAX_DIST = 128

CROSS_HEADS = 4
CROSS_HEAD_DIM = 128
CROSS_WIDTH = CROSS_HEADS * CROSS_HEAD_DIM

N_EXPERTS = 32
TOP_K = 4
D_FF = D_MODEL
SWIGLU_ALPHA = 1.702
SWIGLU_LIMIT = 7.0

LANES = 128
SUBLANES = 8
DT_PAD = LANES
PROJ_COLS = SSM_WIDTH + CONV_CH + DT_PAD + QKV_COLS
VMEM_LIMIT = 56 * 1024 * 1024

TM_PROJ = 512
TM_MID = 512
MID_PARTS = 1
MOE_ROWS = 512
MOE_PART = 128
WEIGHT_DMA_ROWS = 256


def _nt(a, b):
    return lax.dot_general(a, b, (((1,), (1,)), ((), ())), preferred_element_type=F32)


def _tn(a, b):
    return lax.dot_general(a, b, (((0,), (0,)), ((), ())), preferred_element_type=F32)


def _dot(a, b):
    return jnp.dot(a, b, preferred_element_type=F32)


def _dot_exact(a, b):
    return jnp.dot(a, b, preferred_element_type=F32, precision=lax.Precision.HIGHEST)


def _rms(x, g):
    return x * lax.rsqrt(jnp.mean(x * x, axis=-1, keepdims=True) + RMS_EPS) * g


def _sigmoid(x):
    return 1.0 / (1.0 + jnp.exp(-x))


def _params(sem=None, **kwargs):
    return pltpu.CompilerParams(dimension_semantics=sem, vmem_limit_bytes=VMEM_LIMIT, **kwargs)


def _inproj_kernel(x_ref, g_ref, w_f32, bq_ref, z_ref, xbc_ref, dt_ref, qkv_ref, w_ref):
    @pl.when(pl.program_id(0) == 0)
    def _():
        c1 = SSM_WIDTH + CONV_CH
        w_ref[:, :c1] = w_f32[:, :c1].astype(BF16)
        lane = lax.broadcasted_iota(I32, (D_MODEL, DT_PAD), 1)
        w_ref[:, c1:c1 + DT_PAD] = jnp.where(lane < SSM_HEADS, w_f32[:, c1:c1 + DT_PAD], 0.0).astype(BF16)
        w_ref[:, c1 + DT_PAD:] = w_f32[:, c1 + SSM_HEADS:].astype(BF16)

    u = _rms(x_ref[...], g_ref[...])
    p = _dot(u.astype(BF16), w_ref[...])
    z_ref[...] = p[:, :SSM_WIDTH]
    xbc_ref[...] = p[:, SSM_WIDTH:SSM_WIDTH + CONV_CH]
    dt_ref[...] = p[:, SSM_WIDTH + CONV_CH:SSM_WIDTH + CONV_CH + DT_PAD]
    qkv_ref[...] = (p[:, SSM_WIDTH + CONV_CH + DT_PAD:] + bq_ref[...]).astype(BF16)


def _inproj(x2, g_mix, w_in, b_qkv):
    n = x2.shape[0]
    row = lambda i: (i, 0)
    fixed = lambda i: (0, 0)
    return pl.pallas_call(
        _inproj_kernel,
        name="inproj",
        grid=(n // TM_PROJ,),
        in_specs=[
            pl.BlockSpec((TM_PROJ, D_MODEL), row),
            pl.BlockSpec((1, D_MODEL), fixed),
            pl.BlockSpec(w_in.shape, fixed),
            pl.BlockSpec((1, QKV_COLS), fixed),
        ],
        out_specs=[
            pl.BlockSpec((TM_PROJ, SSM_WIDTH), row),
            pl.BlockSpec((TM_PROJ, CONV_CH), row),
            pl.BlockSpec((TM_PROJ, DT_PAD), row),
            pl.BlockSpec((TM_PROJ, QKV_COLS), row),
        ],
        out_shape=[
            jax.ShapeDtypeStruct((n, SSM_WIDTH), F32),
            jax.ShapeDtypeStruct((n, CONV_CH), F32),
            jax.ShapeDtypeStruct((n, DT_PAD), F32),
            jax.ShapeDtypeStruct((n, QKV_COLS), BF16),
        ],
        scratch_shapes=[pltpu.VMEM((D_MODEL, PROJ_COLS), BF16)],
        compiler_params=_params(("arbitrary",)),
    )(x2, g_mix, w_in, b_qkv)


def _split3(x):
    hi = x.astype(BF16)
    rest = x - hi.astype(F32)
    mid = rest.astype(BF16)
    lo = (rest - mid.astype(F32)).astype(BF16)
    return jnp.concatenate([hi, mid, lo], axis=1)


def _ssd_kernel(xbc_ref, z_ref, dt_ref, cw_ref, cb_ref, dtb_ref, ah_ref, dsk_ref, gout_ref,
                e64_ref, ecat_ref, tri_ref, y_ref, conv_ref, state_ref):
    c = pl.program_id(0)

    @pl.when(c == 0)
    def _():
        conv_ref[:, :SUBLANES, :] = jnp.zeros((conv_ref.shape[0], SUBLANES, CONV_CH), F32)
        state_ref[...] = jnp.zeros_like(state_ref)

    for b in range(xbc_ref.shape[0]):
        _ssd_chunk(b, xbc_ref, z_ref, dt_ref, cw_ref, cb_ref, dtb_ref, ah_ref, dsk_ref, gout_ref,
                   e64_ref, ecat_ref, tri_ref, y_ref, conv_ref, state_ref)


def _ssd_chunk(b, xbc_ref, z_ref, dt_ref, cw_ref, cb_ref, dtb_ref, ah_ref, dsk_ref, gout_ref,
               e64_ref, ecat_ref, tri_ref, y_ref, conv_ref, state_ref):
    u = xbc_ref[b]
    conv_ref[b, SUBLANES:, :] = u
    acc = u * cw_ref[CONV_WIDTH - 1:CONV_WIDTH, :] + cb_ref[...]
    for j in range(1, CONV_WIDTH):
        shifted = conv_ref[b, SUBLANES - j:SUBLANES - j + CHUNK, :]
        acc = acc + shifted * cw_ref[CONV_WIDTH - 1 - j:CONV_WIDTH - j, :]
    conv_ref[b, :SUBLANES, :] = u[CHUNK - SUBLANES:, :]
    xbc = acc * _sigmoid(acc)

    xs = xbc[:, :SSM_WIDTH]
    bm = xbc[:, SSM_WIDTH:SSM_WIDTH + SSM_GROUPS * SSM_STATE]
    cm = xbc[:, SSM_WIDTH + SSM_GROUPS * SSM_STATE:]

    dtr = dt_ref[b] + dtb_ref[...]
    dt = jnp.maximum(dtr, 0.0) + jnp.log1p(jnp.exp(-jnp.abs(dtr)))
    a = dt * ah_ref[...]
    a3 = _split3(a)
    a_rows = jnp.concatenate([a3[:, :DT_PAD], a3[:, DT_PAD:2 * DT_PAD], a3[:, 2 * DT_PAD:]], axis=0)
    cs = _dot(tri_ref[...], a_rows)
    cs_row = cs.T
    dt_full = _dot(_split3(dt), e64_ref[...])
    cs_cat = _dot(_split3(cs), ecat_ref[...])
    cs_wide = cs_cat[:, :SSM_HEADS * CHUNK]
    cs_full = cs_cat[:, SSM_HEADS * CHUNK:]
    cs_last = cs_full[CHUNK - 1:CHUNK, :]

    xdt = xs * dt_full
    xdt_b = xdt.astype(BF16)
    xw_b = (xdt * jnp.exp(cs_last - cs_full)).astype(BF16)
    in_decay = jnp.exp(cs_full)
    chunk_decay = jnp.exp(cs_last)

    li = lax.broadcasted_iota(I32, (CHUNK, CHUNK), 0)
    si = lax.broadcasted_iota(I32, (CHUNK, CHUNK), 1)
    causal = li >= si

    ys = []
    for g in range(SSM_GROUPS):
        bg = bm[:, g * SSM_STATE:(g + 1) * SSM_STATE].astype(BF16)
        cg = cm[:, g * SSM_STATE:(g + 1) * SSM_STATE].astype(BF16)
        cb = _nt(cg, bg)
        yd = []
        for r in range(HEADS_PER_GROUP):
            h = g * HEADS_PER_GROUP + r
            diff = cs_wide[:, h * CHUNK:(h + 1) * CHUNK] - cs_row[h:h + 1, :]
            m = cb * jnp.exp(jnp.where(causal, diff, -jnp.inf))
            yd.append(_dot(m.astype(BF16), xdt_b[:, h * SSM_HEAD_DIM:(h + 1) * SSM_HEAD_DIM]))
        y_diag = jnp.concatenate(yd, axis=1)
        gs = slice(g * GROUP_WIDTH, (g + 1) * GROUP_WIDTH)
        st = state_ref[b, g]
        y_off = _dot(cg, st.astype(BF16)) * in_decay[:, gs]
        state_ref[b, g] = st * chunk_decay[:, gs] + _tn(bg, xw_b[:, gs])
        ys.append(y_diag + y_off)
    y = jnp.concatenate(ys, axis=1) + dsk_ref[...] * xs

    zz = z_ref[b]
    y = y * (zz * _sigmoid(zz))
    outs = []
    for g in range(SSM_GROUPS):
        yg = y[:, g * GROUP_WIDTH:(g + 1) * GROUP_WIDTH]
        outs.append(yg * lax.rsqrt(jnp.mean(yg * yg, axis=-1, keepdims=True) + RMS_EPS))
    y_ref[b] = (jnp.concatenate(outs, axis=1) * gout_ref[...]).astype(y_ref.dtype)


def _ssd(xbc, z, dtp, conv_w, conv_b, dt_bias_p, a_head_p, d_full, g_out):
    batch, seq, _ = xbc.shape
    chunk = lambda c: (0, c, 0)
    fixed = lambda c: (0, 0)
    head = np.arange(SSM_WIDTH) // SSM_HEAD_DIM
    e64 = (np.arange(DT_PAD)[:, None] == head[None, :]).astype(np.float32)
    head_w = np.arange(SSM_HEADS * CHUNK) // CHUNK
    e128 = (np.arange(DT_PAD)[:, None] == head_w[None, :]).astype(np.float32)
    e64_3 = jnp.asarray(np.tile(e64, (3, 1))).astype(BF16)
    ecat_3 = jnp.asarray(np.tile(np.concatenate([e128, e64], axis=1), (3, 1))).astype(BF16)
    tri_3 = jnp.asarray(np.tile(np.tril(np.ones((CHUNK, CHUNK), np.float32)), (1, 3))).astype(BF16)
    return pl.pallas_call(
        _ssd_kernel,
        name="ssd",
        grid=(seq // CHUNK,),
        in_specs=[
            pl.BlockSpec((batch, CHUNK, CONV_CH), chunk),
            pl.BlockSpec((batch, CHUNK, SSM_WIDTH), chunk),
            pl.BlockSpec((batch, CHUNK, DT_PAD), chunk),
            pl.BlockSpec((CONV_WIDTH, CONV_CH), fixed),
            pl.BlockSpec((1, CONV_CH), fixed),
            pl.BlockSpec((1, DT_PAD), fixed),
            pl.BlockSpec((1, DT_PAD), fixed),
            pl.BlockSpec((1, SSM_WIDTH), fixed),
            pl.BlockSpec((1, SSM_WIDTH), fixed),
            pl.BlockSpec((3 * DT_PAD, SSM_WIDTH), fixed),
            pl.BlockSpec((3 * DT_PAD, SSM_HEADS * CHUNK + SSM_WIDTH), fixed),
            pl.BlockSpec((CHUNK, 3 * CHUNK), fixed),
        ],
        out_specs=pl.BlockSpec((batch, CHUNK, SSM_WIDTH), chunk),
        out_shape=jax.ShapeDtypeStruct((batch, seq, SSM_WIDTH), BF16),
        scratch_shapes=[
            pltpu.VMEM((batch, SUBLANES + CHUNK, CONV_CH), F32),
            pltpu.VMEM((batch, SSM_GROUPS, SSM_STATE, GROUP_WIDTH), F32),
        ],
        compiler_params=_params(("arbitrary",)),
    )(xbc, z, dtp, conv_w, conv_b, dt_bias_p, a_head_p, d_full, g_out, e64_3, ecat_3, tri_3)


def _t5_bucket_table():
    q_pos = np.arange(BLOCK)[:, None]
    k_pos = np.arange(2 * BLOCK)[None, :] - BLOCK
    dist = q_pos - k_pos
    d = np.maximum(dist, 0)
    max_exact = REL_BUCKETS // 2
    ratio = np.maximum(d, max_exact).astype(np.float32) / np.float32(max_exact)
    large = max_exact + (np.log(ratio) / np.float32(math.log(REL_MAX_DIST / max_exact))
                         * np.float32(REL_BUCKETS - max_exact)).astype(np.int32)
    large = np.minimum(large, REL_BUCKETS - 1)
    bucket = np.where(d < max_exact, d, large)
    in_window = (dist >= 0) & (dist < WINDOW)
    return np.where(in_window, bucket, -1).astype(np.int32)


def _bias_kernel(rb_ref, bucket_ref, out_ref):
    bucket = bucket_ref[...]
    for h in range(ATTN_HEADS):
        acc = jnp.zeros(bucket.shape, F32)
        for b in range(REL_BUCKETS):
            acc = jnp.where(bucket == b, rb_ref[b, h], acc)
        out_ref[h] = jnp.where(bucket >= 0, acc, -jnp.inf)


def _bias_table(rel_bias):
    bucket = jnp.asarray(_t5_bucket_table())
    return pl.pallas_call(
        _bias_kernel,
        in_specs=[
            pl.BlockSpec(memory_space=pltpu.SMEM),
            pl.BlockSpec(memory_space=pltpu.VMEM),
        ],
        out_specs=pl.BlockSpec(memory_space=pltpu.VMEM),
        out_shape=jax.ShapeDtypeStruct((ATTN_HEADS, BLOCK, 2 * BLOCK), F32),
    )(rel_bias, bucket)


def _swa_kernel(sink_ref, q_ref, k_ref, v_ref, kp_ref, vp_ref, bias_ref, o_ref):
    i = pl.program_id(0)
    col = lax.broadcasted_iota(I32, (BLOCK, 2 * BLOCK), 1)
    key_ok = jnp.logical_or(col >= BLOCK, i > 0)
    for b in range(q_ref.shape[0]):
        q = q_ref[b] * jnp.asarray(HEAD_DIM ** -0.5, BF16)
        outs = []
        for g in range(KV_HEADS):
            gs = slice(g * HEAD_DIM, (g + 1) * HEAD_DIM)
            k2 = jnp.concatenate([kp_ref[b, :, gs], k_ref[b, :, gs]], axis=0).astype(BF16)
            v2 = jnp.concatenate([vp_ref[b, :, gs], v_ref[b, :, gs]], axis=0).astype(BF16)
            for r in range(Q_PER_KV):
                h = g * Q_PER_KV + r
                s = _nt(q[:, h * HEAD_DIM:(h + 1) * HEAD_DIM], k2) + bias_ref[h]
                s = jnp.where(key_ok, s, -jnp.inf)
                sink = sink_ref[h]
                m = jnp.maximum(jnp.max(s, axis=-1, keepdims=True), sink)
                p = jnp.exp(s - m)
                denom = jnp.sum(p, axis=-1, keepdims=True) + jnp.exp(sink - m)
                outs.append(_dot(p.astype(BF16), v2) / denom)
        o_ref[b] = jnp.concatenate(outs, axis=1).astype(o_ref.dtype)


def _swa(qkv, sinks, bias):
    batch, seq, _ = qkv.shape
    kcol = ATTN_WIDTH // KV_WIDTH
    prev = lambda i: jnp.maximum(i - 1, 0)
    return pl.pallas_call(
        _swa_kernel,
        name="swa",
        grid_spec=pltpu.PrefetchScalarGridSpec(
            num_scalar_prefetch=1,
            grid=(seq // BLOCK,),
            in_specs=[
                pl.BlockSpec((batch, BLOCK, ATTN_WIDTH), lambda i, s: (0, i, 0)),
                pl.BlockSpec((batch, BLOCK, KV_WIDTH), lambda i, s: (0, i, kcol)),
                pl.BlockSpec((batch, BLOCK, KV_WIDTH), lambda i, s: (0, i, kcol + 1)),
                pl.BlockSpec((batch, BLOCK, KV_WIDTH), lambda i, s: (0, prev(i), kcol)),
                pl.BlockSpec((batch, BLOCK, KV_WIDTH), lambda i, s: (0, prev(i), kcol + 1)),
                pl.BlockSpec((ATTN_HEADS, BLOCK, 2 * BLOCK), lambda i, s: (0, 0, 0)),
            ],
            out_specs=pl.BlockSpec((batch, BLOCK, ATTN_WIDTH), lambda i, s: (0, i, 0)),
        ),
        out_shape=jax.ShapeDtypeStruct((batch, seq, ATTN_WIDTH), BF16),
        compiler_params=_params(("arbitrary",)),
    )(sinks, qkv, qkv, qkv, qkv, qkv, bias)


def _memkv_kernel(m_ref, g_ref, w_ref, k_ref, v_ref):
    u = _rms(m_ref[...], g_ref[...])
    kv = _dot(u.astype(BF16), w_ref[...].astype(BF16))
    k_ref[...] = kv[:, :CROSS_WIDTH].astype(BF16)
    v_ref[...] = kv[:, CROSS_WIDTH:].astype(BF16)


def _memkv(mem2, g_mem, w_kv):
    n = mem2.shape[0]
    return pl.pallas_call(
        _memkv_kernel,
        out_shape=[jax.ShapeDtypeStruct((n, CROSS_WIDTH), BF16)] * 2,
        compiler_params=_params(),
    )(mem2, g_mem, w_kv)


def _mid_kernel(ys_ref, ya_ref, x_ref, wout_f32, bout_ref, gc_ref, wq_f32, k_ref, v_ref, wo_f32,
                gf_ref, wr_ref, br_ref, upper_ref, lower_ref,
                h2_ref, u3_ref, gate_ref, lp_ref, tcnt_ref, tbase_ref, run_ref, wout_ref, wq_ref, wo_ref):
    i = pl.program_id(0)

    @pl.when(i == 0)
    def _():
        run_ref[...] = jnp.zeros_like(run_ref)
        wout_ref[...] = wout_f32[...].astype(BF16)
        wq_ref[...] = wq_f32[...].astype(BF16)
        wo_ref[...] = wo_f32[...].astype(BF16)

    u3_parts = []
    for part in range(MID_PARTS):
        rows = slice(part * (TM_MID // MID_PARTS), (part + 1) * (TM_MID // MID_PARTS))
        ycat = jnp.concatenate([ys_ref[rows, :], ya_ref[rows, :]], axis=1)
        h1 = x_ref[rows, :] + _dot(ycat, wout_ref[...]) + bout_ref[...]

        u2 = _rms(h1, gc_ref[...])
        q = _dot(u2.astype(BF16), wq_ref[...]).astype(BF16)
        outs = []
        for h in range(CROSS_HEADS):
            hs = slice(h * CROSS_HEAD_DIM, (h + 1) * CROSS_HEAD_DIM)
            s = _nt(q[:, hs], k_ref[:, hs]) * (CROSS_HEAD_DIM ** -0.5)
            m = jnp.max(s, axis=-1, keepdims=True)
            p = jnp.exp(s - m)
            outs.append(_dot(p.astype(BF16), v_ref[:, hs]) / jnp.sum(p, axis=-1, keepdims=True))
        o = jnp.concatenate(outs, axis=1).astype(BF16)
        h2 = h1 + _dot(o, wo_ref[...])
        h2_ref[rows, :] = h2
        u3_parts.append(_rms(h2, gf_ref[...]).astype(BF16))
    u3 = jnp.concatenate(u3_parts, axis=0)
    u3_ref[...] = u3

    logits = _nt(wr_ref[...], u3) + br_ref[...]
    eid = lax.broadcasted_iota(I32, logits.shape, 0)
    vals, idxs, hots = [], [], []
    for _ in range(TOP_K):
        m = jnp.max(logits, axis=0, keepdims=True)
        ix = jnp.min(jnp.where(logits == m, eid, N_EXPERTS), axis=0, keepdims=True)
        hot = eid == ix
        logits = jnp.where(hot, -jnp.inf, logits)
        vals.append(m)
        idxs.append(ix)
        hots.append(hot)
    ex = [jnp.exp(v - vals[0]) for v in vals]
    tot = ex[0] + ex[1] + ex[2] + ex[3]
    gate_ref[...] = jnp.concatenate([e / tot for e in ex], axis=0)

    hot_f = [jnp.where(hot, 1.0, 0.0) for hot in hots]
    cnt_k = [jnp.sum(hf, axis=1, keepdims=True) for hf in hot_f]
    tile_cnt = cnt_k[0] + cnt_k[1] + cnt_k[2] + cnt_k[3]
    tile_cnt = jnp.floor((tile_cnt + (SUBLANES - 1)) * (1.0 / SUBLANES)) * SUBLANES
    offs = _dot_exact(lower_ref[...], jnp.broadcast_to(tile_cnt, (N_EXPERTS, LANES)))[:, :1]
    local = []
    before_all = _dot(jnp.concatenate(hot_f, axis=0).astype(BF16), upper_ref[...])
    for kk in range(TOP_K):
        before = before_all[kk * N_EXPERTS:(kk + 1) * N_EXPERTS, :]
        local.append(jnp.sum(jnp.where(hots[kk], before + offs, 0.0), axis=0, keepdims=True))
        offs = offs + cnt_k[kk]
    lp_ref[...] = jnp.concatenate(local, axis=0).astype(I32)
    tcnt_ref[0] = jnp.broadcast_to(tile_cnt, (N_EXPERTS, LANES)).astype(I32)
    tbase_ref[0] = jnp.broadcast_to(run_ref[...], (N_EXPERTS, LANES)).astype(I32)
    run_ref[...] = run_ref[...] + tile_cnt


def _mid(y_ssm, y_attn, x2, w_out, b_out, g_cross, w_q, k_mem, v_mem, w_o, g_ffn, w_r_t, b_r, seq, mem_len):
    n = x2.shape[0]
    tiles_per_batch = seq // TM_MID
    row = lambda i: (i, 0)
    col = lambda i: (0, i)
    fixed = lambda i: (0, 0)
    memb = lambda i: (i // tiles_per_batch, 0)
    tile3 = lambda i: (i, 0, 0)
    n_tiles = n // TM_MID
    upper = jnp.asarray(np.triu(np.ones((TM_MID, TM_MID), np.float32), 1)).astype(BF16)
    lower = jnp.asarray(np.tril(np.ones((N_EXPERTS, N_EXPERTS), np.float32), -1))
    return pl.pallas_call(
        _mid_kernel,
        name="mid",
        grid=(n_tiles,),
        in_specs=[
            pl.BlockSpec((TM_MID, SSM_WIDTH), row),
            pl.BlockSpec((TM_MID, ATTN_WIDTH), row),
            pl.BlockSpec((TM_MID, D_MODEL), row),
            pl.BlockSpec((D_MODEL, D_MODEL), fixed),
            pl.BlockSpec((1, D_MODEL), fixed),
            pl.BlockSpec((1, D_MODEL), fixed),
            pl.BlockSpec((D_MODEL, CROSS_WIDTH), fixed),
            pl.BlockSpec((mem_len, CROSS_WIDTH), memb),
            pl.BlockSpec((mem_len, CROSS_WIDTH), memb),
            pl.BlockSpec((CROSS_WIDTH, D_MODEL), fixed),
            pl.BlockSpec((1, D_MODEL), fixed),
            pl.BlockSpec((N_EXPERTS, D_MODEL), fixed),
            pl.BlockSpec((N_EXPERTS, 1), fixed),
            pl.BlockSpec((TM_MID, TM_MID), fixed),
            pl.BlockSpec((N_EXPERTS, N_EXPERTS), fixed),
        ],
        out_specs=[
            pl.BlockSpec((TM_MID, D_MODEL), row),
            pl.BlockSpec((TM_MID, D_MODEL), row),
            pl.BlockSpec((TOP_K, TM_MID), col),
            pl.BlockSpec((TOP_K, TM_MID), col),
            pl.BlockSpec((1, N_EXPERTS, LANES), tile3),
            pl.BlockSpec((1, N_EXPERTS, LANES), tile3),
        ],
        out_shape=[
            jax.ShapeDtypeStruct((n, D_MODEL), F32),
            jax.ShapeDtypeStruct((n, D_MODEL), BF16),
            jax.ShapeDtypeStruct((TOP_K, n), F32),
            jax.ShapeDtypeStruct((TOP_K, n), I32),
            jax.ShapeDtypeStruct((n_tiles, N_EXPERTS, LANES), I32),
            jax.ShapeDtypeStruct((n_tiles, N_EXPERTS, LANES), I32),
        ],
        scratch_shapes=[
            pltpu.VMEM((N_EXPERTS, 1), F32),
            pltpu.VMEM((D_MODEL, D_MODEL), BF16),
            pltpu.VMEM((D_MODEL, CROSS_WIDTH), BF16),
            pltpu.VMEM((CROSS_WIDTH, D_MODEL), BF16),
        ],
        compiler_params=_params(("arbitrary",)),
    )(y_ssm, y_attn, x2, w_out, b_out, g_cross, w_q, k_mem, v_mem, w_o, g_ffn, w_r_t, b_r, upper, lower)


ROW_CHUNK = 256
LOCAL_ROWS = -(-(TM_MID * TOP_K + N_EXPERTS * (SUBLANES - 1)) // ROW_CHUNK) * ROW_CHUNK
RUN_BITS = tuple(1 << s for s in range(TM_MID.bit_length() - 1, SUBLANES.bit_length() - 2, -1))
PAD_BITS = tuple(1 << s for s in range((MOE_ROWS - 1).bit_length() - 1, SUBLANES.bit_length() - 2, -1))


U32 = jnp.uint32
PACKED = D_MODEL // 2


def _pack_pairs(v):
    bits = lax.bitcast_convert_type(v, U32)
    return bits[:, PACKED:] | (bits[:, :PACKED] >> 16)


def _unpack_pairs(w):
    lo = lax.bitcast_convert_type(w << 16, F32)
    hi = lax.bitcast_convert_type(w & jnp.uint32(0xFFFF0000), F32)
    return jnp.concatenate([lo, hi], axis=1).astype(BF16)


def _run_copies(n, src_ref, src, dst_ref, dst, sem, bits, wait, advance_src=True):
    for bit in bits:
        step = n & bit

        @pl.when(step != 0)
        def _():
            cp = pltpu.make_async_copy(src_ref.at[pl.ds(pl.multiple_of(src, SUBLANES), bit)],
                                       dst_ref.at[pl.ds(pl.multiple_of(dst, SUBLANES), bit)], sem)
            if wait:
                cp.wait()
            else:
                cp.start()

        if advance_src:
            src = src + step
        dst = dst + step


def _wait_rows(n, src_ref, dst_ref, sem):
    rows = pl.ds(0, pl.multiple_of(n, SUBLANES))
    pltpu.make_async_copy(src_ref.at[rows], dst_ref.at[rows], sem).wait()


def _dispatch_kernel(tcnt_ref, lo_ref, gs_ref, padn_ref, pads_ref, na_ref, u_ref, lp_ref, xs_ref,
                     xloc_ref, zero_ref, sem, zsem):
    i = pl.program_id(0)

    def zero_fill(wait):
        def body(e, carry):
            _run_copies(padn_ref[e], zero_ref, 0, xs_ref, pads_ref[e], zsem, PAD_BITS, wait, advance_src=False)
            return carry
        lax.fori_loop(0, N_EXPERTS, body, 0)

        def tail(b, carry):
            cp = pltpu.make_async_copy(zero_ref, xs_ref.at[pl.ds(pl.multiple_of(b * MOE_ROWS, MOE_ROWS), MOE_ROWS)], zsem)
            if wait:
                cp.wait()
            else:
                cp.start()
            return carry
        lax.fori_loop(na_ref[0], xs_ref.shape[0] // MOE_ROWS, tail, 0)

    @pl.when(i == 0)
    def _():
        zero_ref[...] = jnp.zeros_like(zero_ref)
        zero_fill(False)

    slot = i % 2
    u = u_ref[...]
    lp = lp_ref[...]
    for c in range(LOCAL_ROWS // ROW_CHUNK):
        r = lax.broadcasted_iota(I32, (ROW_CHUNK, TM_MID), 0) + c * ROW_CHUNK
        p = jnp.zeros((ROW_CHUNK, TM_MID), F32)
        for kk in range(TOP_K):
            p = jnp.where(lp[kk:kk + 1, :] == r, 1.0, p)
        xloc_ref[slot, c * ROW_CHUNK:(c + 1) * ROW_CHUNK, :] = _pack_pairs(_dot(p.astype(BF16), u))

    def start_runs(tile):
        s = tile % 2

        def body(e, carry):
            j = tile * N_EXPERTS + e
            _run_copies(tcnt_ref[j], xloc_ref.at[s], lo_ref[j], xs_ref, gs_ref[j], sem.at[s], RUN_BITS, False)
            return carry
        lax.fori_loop(0, N_EXPERTS, body, 0)

    def wait_runs(tile):
        s = tile % 2
        last = tile * N_EXPERTS + N_EXPERTS - 1
        _wait_rows(lo_ref[last] + tcnt_ref[last], xloc_ref.at[s], xs_ref, sem.at[s])

    start_runs(i)

    @pl.when(i > 0)
    def _():
        wait_runs(i - 1)

    @pl.when(i == pl.num_programs(0) - 1)
    def _():
        wait_runs(i)

    @pl.when(i == 0)
    def _():
        zero_fill(True)


def _dispatch(tcnt, lo, gstart, padn, pads, n_active, u3, lp, n_rows):
    n = u3.shape[0]
    return pl.pallas_call(
        _dispatch_kernel,
        name="dispatch",
        grid_spec=pltpu.PrefetchScalarGridSpec(
            num_scalar_prefetch=6,
            grid=(n // TM_MID,),
            in_specs=[
                pl.BlockSpec((TM_MID, D_MODEL), lambda i, *_: (i, 0)),
                pl.BlockSpec((TOP_K, TM_MID), lambda i, *_: (0, i)),
            ],
            out_specs=pl.BlockSpec(memory_space=pl.ANY),
            scratch_shapes=[
                pltpu.VMEM((2, LOCAL_ROWS, PACKED), U32),
                pltpu.VMEM((MOE_ROWS, PACKED), U32),
                pltpu.SemaphoreType.DMA((2,)),
                pltpu.SemaphoreType.DMA,
            ],
        ),
        out_shape=jax.ShapeDtypeStruct((n_rows, PACKED), U32),
        compiler_params=_params(("arbitrary",)),
    )(tcnt, lo, gstart, padn, pads, n_active, u3, lp)


PAIR_COLS = 2 * LANES


def _expert_kernel(be_ref, nx_ref, nv_ref, na_ref, x_ref, b1_ref, b2_ref, perm_ref, w1_hbm, w2_hbm, y_ref,
                   w1s_ref, w2s_ref, w1p_ref, w2b_ref, sem):
    b = pl.program_id(0)
    valid_rows = nv_ref[b]
    active = valid_rows > 0
    expert = be_ref[b]
    new_expert = jnp.logical_or(b == 0, expert != be_ref[jnp.maximum(b - 1, 0)])

    def fetch(e, wait):
        for src, dst, s in ((w1_hbm, w1s_ref, 0), (w2_hbm, w2s_ref, 1)):
            for r in range(0, src.shape[1], WEIGHT_DMA_ROWS):
                rs = pl.ds(r, WEIGHT_DMA_ROWS)
                cp = pltpu.make_async_copy(src.at[e, rs], dst.at[rs], sem.at[s])
                if wait:
                    cp.wait()
                else:
                    cp.start(priority=1)

    @pl.when(jnp.logical_and(b == 0, active))
    def _():
        fetch(expert, False)

    @pl.when(jnp.logical_and(active, new_expert))
    def _():
        fetch(expert, True)
        for c in range(2 * D_FF // PAIR_COLS):
            cs = slice(c * PAIR_COLS, (c + 1) * PAIR_COLS)
            w1p_ref[:, cs] = _dot(w1s_ref[:, cs].astype(BF16), perm_ref[...]).astype(BF16)
        w2b_ref[...] = w2s_ref[...].astype(BF16)

        @pl.when(nx_ref[b] >= 0)
        def _():
            fetch(nx_ref[b], False)

    for live in range(MOE_PART, MOE_ROWS + 1, MOE_PART):
        @pl.when(jnp.logical_and(valid_rows > live - MOE_PART, valid_rows <= live))
        def _():
            x = _unpack_pairs(x_ref[:live, :])
            hdn = _dot(x, w1p_ref[...]) + b1_ref[0]
            acts = []
            for c in range(2 * D_FF // PAIR_COLS):
                hg = jnp.minimum(hdn[:, c * PAIR_COLS:c * PAIR_COLS + LANES], SWIGLU_LIMIT)
                hl = jnp.clip(hdn[:, c * PAIR_COLS + LANES:(c + 1) * PAIR_COLS], -SWIGLU_LIMIT, SWIGLU_LIMIT)
                acts.append((hg * _sigmoid(SWIGLU_ALPHA * hg) * (hl + 1.0)).astype(BF16))
            y = _dot(jnp.concatenate(acts, axis=1), w2b_ref[...]) + b2_ref[0]
            y_ref[:live, :] = _pack_pairs(y.astype(BF16).astype(F32))
            if live < MOE_ROWS:
                y_ref[live:, :] = jnp.zeros((MOE_ROWS - live, PACKED), U32)

    @pl.when(jnp.logical_not(active))
    def _():
        y_ref[...] = jnp.zeros_like(y_ref)


def _experts(block_exp, next_exp, valid_rows, n_active, xs, w1, b1p, w2, b2):
    n_rows = xs.shape[0]
    n_blocks = n_rows // MOE_ROWS
    xmap = lambda b, be, nx, nv, na: (jnp.maximum(jnp.minimum(b, na[0] - 1), 0), 0)
    emap = lambda b, be, nx, nv, na: (be[b], 0, 0)
    perm = np.zeros((PAIR_COLS, PAIR_COLS), np.float32)
    perm[2 * np.arange(LANES), np.arange(LANES)] = 1.0
    perm[2 * np.arange(LANES) + 1, LANES + np.arange(LANES)] = 1.0
    return pl.pallas_call(
        _expert_kernel,
        name="experts",
        grid_spec=pltpu.PrefetchScalarGridSpec(
            num_scalar_prefetch=4,
            grid=(n_blocks,),
            in_specs=[
                pl.BlockSpec((MOE_ROWS, PACKED), xmap),
                pl.BlockSpec((1, 1, 2 * D_FF), emap),
                pl.BlockSpec((1, 1, D_MODEL), emap),
                pl.BlockSpec((PAIR_COLS, PAIR_COLS), lambda b, *_: (0, 0)),
                pl.BlockSpec(memory_space=pl.ANY),
                pl.BlockSpec(memory_space=pl.ANY),
            ],
            out_specs=pl.BlockSpec((MOE_ROWS, PACKED), lambda b, *_: (b, 0)),
            scratch_shapes=[
                pltpu.VMEM((D_MODEL, 2 * D_FF), F32),
                pltpu.VMEM((D_FF, D_MODEL), F32),
                pltpu.VMEM((D_MODEL, 2 * D_FF), BF16),
                pltpu.VMEM((D_FF, D_MODEL), BF16),
                pltpu.SemaphoreType.DMA((2,)),
            ],
        ),
        out_shape=jax.ShapeDtypeStruct((n_rows, PACKED), U32),
        compiler_params=_params(("arbitrary",)),
    )(block_exp, next_exp, valid_rows, n_active, xs, b1p, b2, jnp.asarray(perm).astype(BF16), w1, w2)


def _combine_kernel(tcnt_ref, lo_ref, gs_ref, h2_ref, gate_ref, lp_ref, g_ref, ys_ref, o_ref, yloc_ref, sem):
    i = pl.program_id(0)

    slot = i % 2

    def start_runs(tile):
        s = tile % 2

        def body(e, carry):
            j = tile * N_EXPERTS + e
            _run_copies(tcnt_ref[j], ys_ref, gs_ref[j], yloc_ref.at[s], lo_ref[j], sem.at[s], RUN_BITS, False)
            return carry
        lax.fori_loop(0, N_EXPERTS, body, 0)

    @pl.when(i == 0)
    def _():
        yloc_ref[...] = jnp.zeros_like(yloc_ref)
        start_runs(i)

    @pl.when(i + 1 < pl.num_programs(0))
    def _():
        start_runs(i + 1)

    gates = gate_ref[...]
    lp = lp_ref[...]
    last = i * N_EXPERTS + N_EXPERTS - 1
    _wait_rows(lo_ref[last] + tcnt_ref[last], ys_ref, yloc_ref.at[slot], sem.at[slot])

    h = h2_ref[...]
    for c in range(LOCAL_ROWS // ROW_CHUNK):
        r = lax.broadcasted_iota(I32, (ROW_CHUNK, TM_MID), 0) + c * ROW_CHUNK
        a = jnp.zeros((ROW_CHUNK, TM_MID), F32)
        for kk in range(TOP_K):
            a = jnp.where(lp[kk:kk + 1, :] == r, gates[kk:kk + 1, :], a)
        h = h + _tn(a.astype(BF16), _unpack_pairs(yloc_ref[slot, c * ROW_CHUNK:(c + 1) * ROW_CHUNK, :]))
    o_ref[...] = _rms(h, g_ref[...])


def _combine(tcnt, lo, gstart, h2, gates, lp, g_final, ys):
    n = h2.shape[0]
    return pl.pallas_call(
        _combine_kernel,
        name="combine",
        grid_spec=pltpu.PrefetchScalarGridSpec(
            num_scalar_prefetch=3,
            grid=(n // TM_MID,),
            in_specs=[
                pl.BlockSpec((TM_MID, D_MODEL), lambda i, *_: (i, 0)),
                pl.BlockSpec((TOP_K, TM_MID), lambda i, *_: (0, i)),
                pl.BlockSpec((TOP_K, TM_MID), lambda i, *_: (0, i)),
                pl.BlockSpec((1, D_MODEL), lambda i, *_: (0, 0)),
                pl.BlockSpec(memory_space=pl.ANY),
            ],
            out_specs=pl.BlockSpec((TM_MID, D_MODEL), lambda i, *_: (i, 0)),
            scratch_shapes=[
                pltpu.VMEM((2, LOCAL_ROWS, PACKED), U32),
                pltpu.SemaphoreType.DMA((2,)),
            ],
        ),
        out_shape=jax.ShapeDtypeStruct((n, D_MODEL), F32),
        compiler_params=_params(("arbitrary",)),
    )(tcnt, lo, gstart, h2, gates, lp, g_final, ys)


def _layer(h, mem2, batch, seq, mem_len, g_mix, w_in, b_qkv, conv_w, conv_b, dt_bias, a_log, d_skip,
           g_ssm_out, attn_sinks, bias, w_out, b_out, g_cross, g_mem, w_q_cross, w_kv_cross, w_o_cross,
           g_ffn, w_router, b_router, w1, b1, w2, b2, g_final):
    n = h.shape[0]
    z, xbc, dtp, qkv = _inproj(h, g_mix[None, :], w_in, b_qkv[None, :])

    pad_h = (0, DT_PAD - SSM_HEADS)
    dt_bias_p = jnp.pad(dt_bias, pad_h)[None, :]
    a_head_p = jnp.pad(-jnp.exp(a_log), pad_h)[None, :]
    d_full = jnp.repeat(d_skip, SSM_HEAD_DIM)[None, :]
    per_seq = lambda t: t.reshape(batch, seq, t.shape[-1])
    y_ssm = _ssd(per_seq(xbc), per_seq(z), per_seq(dtp), conv_w, conv_b[None, :], dt_bias_p, a_head_p,
                 d_full, g_ssm_out[None, :]).reshape(n, SSM_WIDTH)
    y_attn = _swa(per_seq(qkv), attn_sinks, bias).reshape(n, ATTN_WIDTH)

    k_mem, v_mem = _memkv(mem2, g_mem[None, :], w_kv_cross)
    h2, u3, gates, lp, tcnt, tbase = _mid(
        y_ssm, y_attn, h, w_out, b_out[None, :], g_cross[None, :],
        w_q_cross, k_mem, v_mem, w_o_cross, g_ffn[None, :],
        w_router.T.astype(BF16), b_router[:, None], seq, mem_len)

    max_rows = n * TOP_K + (n // TM_MID) * N_EXPERTS * (SUBLANES - 1)
    n_blocks = -(-max_rows // MOE_ROWS) + N_EXPERTS
    tcnt = tcnt[:, :, 0]
    tbase = tbase[:, :, 0]
    counts = tbase[-1] + tcnt[-1]
    padded = (counts + MOE_ROWS - 1) // MOE_ROWS * MOE_ROWS
    padded_end = jnp.cumsum(padded)
    start = (padded_end - padded).astype(I32)
    n_active = (padded_end[-1] // MOE_ROWS).astype(I32)
    blk = jnp.arange(n_blocks, dtype=I32)
    blk_exp = jnp.sum((blk * MOE_ROWS)[:, None] >= padded_end[None, :], axis=1)
    eid = jnp.arange(N_EXPERTS, dtype=I32)
    last_exp = jnp.max(jnp.where(padded > 0, eid, 0))
    blk_exp = jnp.where(blk < n_active, jnp.minimum(blk_exp, N_EXPERTS - 1), last_exp).astype(I32)
    later = jnp.logical_and(padded[None, :] > 0, eid[None, :] > eid[:, None])
    next_of = jnp.min(jnp.where(later, eid[None, :], N_EXPERTS), axis=1)
    next_of = jnp.where(next_of < N_EXPERTS, next_of, -1).astype(I32)
    of_blk = blk_exp[:, None] == eid[None, :]
    blk_next = jnp.sum(jnp.where(of_blk, next_of[None, :], 0), axis=1).astype(I32)
    blk_end = jnp.sum(jnp.where(of_blk, (start + counts)[None, :], 0), axis=1)
    blk_valid = jnp.clip(blk_end - blk * MOE_ROWS, 0, MOE_ROWS)
    blk_valid = jnp.where(blk < n_active, blk_valid, 0).astype(I32)

    run_n = tcnt.reshape(-1)
    run_local = (jnp.cumsum(tcnt, axis=1) - tcnt).reshape(-1)
    run_global = (start[None, :] + tbase).reshape(-1)
    xs = _dispatch(run_n, run_local, run_global, padded - counts, start + counts, n_active[None], u3, lp,
                   n_blocks * MOE_ROWS)

    b1p = b1.reshape(N_EXPERTS, 2 * D_FF // PAIR_COLS, LANES, 2).transpose(0, 1, 3, 2)
    ys = _experts(blk_exp, blk_next, blk_valid, n_active[None], xs, w1,
                  b1p.reshape(N_EXPERTS, 1, 2 * D_FF), w2, b2[:, None, :])
    return _combine(run_n, run_local, run_global, h2, gates, lp, g_final, ys)


def kernel(x, mem, g_mix, w_in, b_qkv, conv_w, conv_b, dt_bias, a_log, d_skip, g_ssm_out, attn_sinks, rel_bias, w_out, b_out, g_cross, g_mem, w_q_cross, w_kv_cross, w_o_cross, g_ffn, w_router, b_router, w1, b1, w2, b2, g_final):
    batch, seq, d = x.shape
    mem_len = mem.shape[1]
    depth = g_mix.shape[0]
    assert depth == 1 and d == D_MODEL and seq % TM_MID == 0
    h = x.reshape(batch * seq, d)
    mem2 = mem.reshape(batch * mem_len, d)
    bias = _bias_table(rel_bias)
    out = _layer(
        h, mem2, batch, seq, mem_len, g_mix[0], w_in[0], b_qkv[0], conv_w[0], conv_b[0], dt_bias[0],
        a_log[0], d_skip[0], g_ssm_out[0], attn_sinks[0], bias, w_out[0], b_out[0], g_cross[0],
        g_mem[0], w_q_cross[0], w_kv_cross[0], w_o_cross[0], g_ffn[0], w_router[0], b_router[0],
        w1[0], b1[0], w2[0], b2[0], g_final[None, :])
    return out.reshape(batch, seq, d)
```

```python
import math

import numpy as np
import jax
import jax.numpy as jnp
from jax import lax
from jax.experimental import pallas as pl
from jax.experimental.pallas import tpu as pltpu

F32 = jnp.float32
BF16 = jnp.bfloat16
I32 = jnp.int32

D_MODEL = 1024
RMS_EPS = 1e-5

SSM_HEADS = 8
SSM_HEAD_DIM = 64
SSM_WIDTH = SSM_HEADS * SSM_HEAD_DIM
SSM_STATE = 128
SSM_GROUPS = 2
HEADS_PER_GROUP = SSM_HEADS // SSM_GROUPS
GROUP_WIDTH = SSM_WIDTH // SSM_GROUPS
CONV_WIDTH = 4
CHUNK = 128
CONV_CH = SSM_WIDTH + 2 * SSM_GROUPS * SSM_STATE

ATTN_HEADS = 8
KV_HEADS = 2
Q_PER_KV = ATTN_HEADS // KV_HEADS
HEAD_DIM = 64
ATTN_WIDTH = ATTN_HEADS * HEAD_DIM
KV_WIDTH = KV_HEADS * HEAD_DIM
WINDOW = 128
BLOCK = WINDOW
QKV_COLS = ATTN_WIDTH + 2 * KV_WIDTH

REL_BUCKETS = 32
REL_MAX_DIST = 128

CROSS_HEADS = 4
CROSS_HEAD_DIM = 128
CROSS_WIDTH = CROSS_HEADS * CROSS_HEAD_DIM

N_EXPERTS = 32
TOP_K = 4
D_FF = D_MODEL
SWIGLU_ALPHA = 1.702
SWIGLU_LIMIT = 7.0

LANES = 128
SUBLANES = 8
DT_PAD = LANES
PROJ_COLS = SSM_WIDTH + CONV_CH + DT_PAD + QKV_COLS
VMEM_LIMIT = 56 * 1024 * 1024

TM_PROJ = 512
TM_MID = 512
MID_PARTS = 1
MOE_ROWS = 512
MOE_PART = 128
WEIGHT_DMA_ROWS = 256


def _nt(a, b):
    return lax.dot_general(a, b, (((1,), (1,)), ((), ())), preferred_element_type=F32)


def _tn(a, b):
    return lax.dot_general(a, b, (((0,), (0,)), ((), ())), preferred_element_type=F32)


def _dot(a, b):
    return jnp.dot(a, b, preferred_element_type=F32)


def _dot_exact(a, b):
    return jnp.dot(a, b, preferred_element_type=F32, precision=lax.Precision.HIGHEST)


def _rms(x, g):
    return x * lax.rsqrt(jnp.mean(x * x, axis=-1, keepdims=True) + RMS_EPS) * g


def _sigmoid(x):
    return 0.5 * jnp.tanh(0.5 * x) + 0.5


def _params(sem=None, **kwargs):
    return pltpu.CompilerParams(dimension_semantics=sem, vmem_limit_bytes=VMEM_LIMIT, **kwargs)


def _inproj_kernel(x_ref, g_ref, w_f32, bq_ref, z_ref, xbc_ref, dt_ref, qkv_ref, w_ref):
    @pl.when(pl.program_id(0) == 0)
    def _():
        c1 = SSM_WIDTH + CONV_CH
        w_ref[:, :c1] = w_f32[:, :c1].astype(BF16)
        lane = lax.broadcasted_iota(I32, (D_MODEL, DT_PAD), 1)
        w_ref[:, c1:c1 + DT_PAD] = jnp.where(lane < SSM_HEADS, w_f32[:, c1:c1 + DT_PAD], 0.0).astype(BF16)
        w_ref[:, c1 + DT_PAD:] = w_f32[:, c1 + SSM_HEADS:].astype(BF16)

    u = _rms(x_ref[...], g_ref[...])
    p = _dot(u.astype(BF16), w_ref[...])
    z_ref[...] = p[:, :SSM_WIDTH]
    xbc_ref[...] = p[:, SSM_WIDTH:SSM_WIDTH + CONV_CH]
    dt_ref[...] = p[:, SSM_WIDTH + CONV_CH:SSM_WIDTH + CONV_CH + DT_PAD]
    qkv_ref[...] = p[:, SSM_WIDTH + CONV_CH + DT_PAD:] + bq_ref[...]


def _inproj(x2, g_mix, w_in, b_qkv):
    n = x2.shape[0]
    row = lambda i: (i, 0)
    fixed = lambda i: (0, 0)
    return pl.pallas_call(
        _inproj_kernel,
        name="inproj",
        grid=(n // TM_PROJ,),
        in_specs=[
            pl.BlockSpec((TM_PROJ, D_MODEL), row),
            pl.BlockSpec((1, D_MODEL), fixed),
            pl.BlockSpec(w_in.shape, fixed),
            pl.BlockSpec((1, QKV_COLS), fixed),
        ],
        out_specs=[
            pl.BlockSpec((TM_PROJ, SSM_WIDTH), row),
            pl.BlockSpec((TM_PROJ, CONV_CH), row),
            pl.BlockSpec((TM_PROJ, DT_PAD), row),
            pl.BlockSpec((TM_PROJ, QKV_COLS), row),
        ],
        out_shape=[
            jax.ShapeDtypeStruct((n, SSM_WIDTH), F32),
            jax.ShapeDtypeStruct((n, CONV_CH), F32),
            jax.ShapeDtypeStruct((n, DT_PAD), F32),
            jax.ShapeDtypeStruct((n, QKV_COLS), F32),
        ],
        scratch_shapes=[pltpu.VMEM((D_MODEL, PROJ_COLS), BF16)],
        compiler_params=_params(("arbitrary",)),
    )(x2, g_mix, w_in, b_qkv)


def _split3(x):
    hi = x.astype(BF16)
    rest = x - hi.astype(F32)
    mid = rest.astype(BF16)
    lo = (rest - mid.astype(F32)).astype(BF16)
    return jnp.concatenate([hi, mid, lo], axis=1)


def _ssd_kernel(xbc_ref, z_ref, dt_ref, cw_ref, cb_ref, dtb_ref, ah_ref, dsk_ref, gout_ref,
                e64_ref, ecat_ref, tri_ref, y_ref, conv_ref, state_ref):
    c = pl.program_id(0)

    @pl.when(c == 0)
    def _():
        conv_ref[:, :SUBLANES, :] = jnp.zeros((conv_ref.shape[0], SUBLANES, CONV_CH), F32)
        state_ref[...] = jnp.zeros_like(state_ref)

    for b in range(xbc_ref.shape[0]):
        _ssd_chunk(b, xbc_ref, z_ref, dt_ref, cw_ref, cb_ref, dtb_ref, ah_ref, dsk_ref, gout_ref,
                   e64_ref, ecat_ref, tri_ref, y_ref, conv_ref, state_ref)


def _ssd_chunk(b, xbc_ref, z_ref, dt_ref, cw_ref, cb_ref, dtb_ref, ah_ref, dsk_ref, gout_ref,
               e64_ref, ecat_ref, tri_ref, y_ref, conv_ref, state_ref):
    u = xbc_ref[b]
    conv_ref[b, SUBLANES:, :] = u
    acc = u * cw_ref[CONV_WIDTH - 1:CONV_WIDTH, :] + cb_ref[...]
    for j in range(1, CONV_WIDTH):
        shifted = conv_ref[b, SUBLANES - j:SUBLANES - j + CHUNK, :]
        acc = acc + shifted * cw_ref[CONV_WIDTH - 1 - j:CONV_WIDTH - j, :]
    conv_ref[b, :SUBLANES, :] = u[CHUNK - SUBLANES:, :]
    xbc = acc * _sigmoid(acc)

    xs = xbc[:, :SSM_WIDTH]
    bm = xbc[:, SSM_WIDTH:SSM_WIDTH + SSM_GROUPS * SSM_STATE]
    cm = xbc[:, SSM_WIDTH + SSM_GROUPS * SSM_STATE:]

    dtr = dt_ref[b] + dtb_ref[...]
    dt = jnp.maximum(dtr, 0.0) + jnp.log1p(jnp.exp(-jnp.abs(dtr)))
    a = dt * ah_ref[...]
    a3 = _split3(a)
    a_rows = jnp.concatenate([a3[:, :DT_PAD], a3[:, DT_PAD:2 * DT_PAD], a3[:, 2 * DT_PAD:]], axis=0)
    cs = _dot(tri_ref[...], a_rows)
    cs_row = cs.T
    dt_full = _dot(_split3(dt), e64_ref[...])
    cs_cat = _dot(_split3(cs), ecat_ref[...])
    cs_wide = cs_cat[:, :SSM_HEADS * CHUNK]
    cs_full = cs_cat[:, SSM_HEADS * CHUNK:]
    cs_last = cs_full[CHUNK - 1:CHUNK, :]

    xdt = xs * dt_full
    xdt_b = xdt.astype(BF16)
    xw_b = (xdt * jnp.exp(cs_last - cs_full)).astype(BF16)
    in_decay = jnp.exp(cs_full)
    chunk_decay = jnp.exp(cs_last)

    li = lax.broadcasted_iota(I32, (CHUNK, CHUNK), 0)
    si = lax.broadcasted_iota(I32, (CHUNK, CHUNK), 1)
    causal = li >= si

    ys = []
    for g in range(SSM_GROUPS):
        bg = bm[:, g * SSM_STATE:(g + 1) * SSM_STATE].astype(BF16)
        cg = cm[:, g * SSM_STATE:(g + 1) * SSM_STATE].astype(BF16)
        cb = _nt(cg, bg)
        yd = []
        for r in range(HEADS_PER_GROUP):
            h = g * HEADS_PER_GROUP + r
            diff = cs_wide[:, h * CHUNK:(h + 1) * CHUNK] - cs_row[h:h + 1, :]
            m = cb * jnp.exp(jnp.where(causal, diff, -jnp.inf))
            yd.append(_dot(m.astype(BF16), xdt_b[:, h * SSM_HEAD_DIM:(h + 1) * SSM_HEAD_DIM]))
        y_diag = jnp.concatenate(yd, axis=1)
        gs = slice(g * GROUP_WIDTH, (g + 1) * GROUP_WIDTH)
        st = state_ref[b, g]
        y_off = _dot(cg, st.astype(BF16)) * in_decay[:, gs]
        state_ref[b, g] = st * chunk_decay[:, gs] + _tn(bg, xw_b[:, gs])
        ys.append(y_diag + y_off)
    y = jnp.concatenate(ys, axis=1) + dsk_ref[...] * xs

    zz = z_ref[b]
    y = y * (zz * _sigmoid(zz))
    outs = []
    for g in range(SSM_GROUPS):
        yg = y[:, g * GROUP_WIDTH:(g + 1) * GROUP_WIDTH]
        outs.append(yg * lax.rsqrt(jnp.mean(yg * yg, axis=-1, keepdims=True) + RMS_EPS))
    y_ref[b] = jnp.concatenate(outs, axis=1) * gout_ref[...]


def _ssd(xbc, z, dtp, conv_w, conv_b, dt_bias_p, a_head_p, d_full, g_out):
    batch, seq, _ = xbc.shape
    chunk = lambda c: (0, c, 0)
    fixed = lambda c: (0, 0)
    head = np.arange(SSM_WIDTH) // SSM_HEAD_DIM
    e64 = (np.arange(DT_PAD)[:, None] == head[None, :]).astype(np.float32)
    head_w = np.arange(SSM_HEADS * CHUNK) // CHUNK
    e128 = (np.arange(DT_PAD)[:, None] == head_w[None, :]).astype(np.float32)
    e64_3 = jnp.asarray(np.tile(e64, (3, 1))).astype(BF16)
    ecat_3 = jnp.asarray(np.tile(np.concatenate([e128, e64], axis=1), (3, 1))).astype(BF16)
    tri_3 = jnp.asarray(np.tile(np.tril(np.ones((CHUNK, CHUNK), np.float32)), (1, 3))).astype(BF16)
    return pl.pallas_call(
        _ssd_kernel,
        name="ssd",
        grid=(seq // CHUNK,),
        in_specs=[
            pl.BlockSpec((batch, CHUNK, CONV_CH), chunk),
            pl.BlockSpec((batch, CHUNK, SSM_WIDTH), chunk),
            pl.BlockSpec((batch, CHUNK, DT_PAD), chunk),
            pl.BlockSpec((CONV_WIDTH, CONV_CH), fixed),
            pl.BlockSpec((1, CONV_CH), fixed),
            pl.BlockSpec((1, DT_PAD), fixed),
            pl.BlockSpec((1, DT_PAD), fixed),
            pl.BlockSpec((1, SSM_WIDTH), fixed),
            pl.BlockSpec((1, SSM_WIDTH), fixed),
            pl.BlockSpec((3 * DT_PAD, SSM_WIDTH), fixed),
            pl.BlockSpec((3 * DT_PAD, SSM_HEADS * CHUNK + SSM_WIDTH), fixed),
            pl.BlockSpec((CHUNK, 3 * CHUNK), fixed),
        ],
        out_specs=pl.BlockSpec((batch, CHUNK, SSM_WIDTH), chunk),
        out_shape=jax.ShapeDtypeStruct((batch, seq, SSM_WIDTH), F32),
        scratch_shapes=[
            pltpu.VMEM((batch, SUBLANES + CHUNK, CONV_CH), F32),
            pltpu.VMEM((batch, SSM_GROUPS, SSM_STATE, GROUP_WIDTH), F32),
        ],
        compiler_params=_params(("arbitrary",)),
    )(xbc, z, dtp, conv_w, conv_b, dt_bias_p, a_head_p, d_full, g_out, e64_3, ecat_3, tri_3)


def _t5_bucket_table():
    q_pos = np.arange(BLOCK)[:, None]
    k_pos = np.arange(2 * BLOCK)[None, :] - BLOCK
    dist = q_pos - k_pos
    d = np.maximum(dist, 0)
    max_exact = REL_BUCKETS // 2
    ratio = np.maximum(d, max_exact).astype(np.float32) / np.float32(max_exact)
    large = max_exact + (np.log(ratio) / np.float32(math.log(REL_MAX_DIST / max_exact))
                         * np.float32(REL_BUCKETS - max_exact)).astype(np.int32)
    large = np.minimum(large, REL_BUCKETS - 1)
    bucket = np.where(d < max_exact, d, large)
    in_window = (dist >= 0) & (dist < WINDOW)
    return np.where(in_window, bucket, -1).astype(np.int32)


def _bias_kernel(rb_ref, bucket_ref, out_ref):
    bucket = bucket_ref[...]
    for h in range(ATTN_HEADS):
        acc = jnp.zeros(bucket.shape, F32)
        for b in range(REL_BUCKETS):
            acc = jnp.where(bucket == b, rb_ref[b, h], acc)
        out_ref[h] = jnp.where(bucket >= 0, acc, -jnp.inf)


def _bias_table(rel_bias):
    bucket = jnp.asarray(_t5_bucket_table())
    return pl.pallas_call(
        _bias_kernel,
        in_specs=[
            pl.BlockSpec(memory_space=pltpu.SMEM),
            pl.BlockSpec(memory_space=pltpu.VMEM),
        ],
        out_specs=pl.BlockSpec(memory_space=pltpu.VMEM),
        out_shape=jax.ShapeDtypeStruct((ATTN_HEADS, BLOCK, 2 * BLOCK), F32),
    )(rel_bias, bucket)


def _swa_kernel(sink_ref, q_ref, k_ref, v_ref, kp_ref, vp_ref, bias_ref, o_ref):
    i = pl.program_id(0)
    col = lax.broadcasted_iota(I32, (BLOCK, 2 * BLOCK), 1)
    key_ok = jnp.logical_or(col >= BLOCK, i > 0)
    for b in range(q_ref.shape[0]):
        q = (q_ref[b] * (HEAD_DIM ** -0.5)).astype(BF16)
        outs = []
        for g in range(KV_HEADS):
            gs = slice(g * HEAD_DIM, (g + 1) * HEAD_DIM)
            k2 = jnp.concatenate([kp_ref[b, :, gs], k_ref[b, :, gs]], axis=0).astype(BF16)
            v2 = jnp.concatenate([vp_ref[b, :, gs], v_ref[b, :, gs]], axis=0).astype(BF16)
            for r in range(Q_PER_KV):
                h = g * Q_PER_KV + r
                s = _nt(q[:, h * HEAD_DIM:(h + 1) * HEAD_DIM], k2) + bias_ref[h]
                s = jnp.where(key_ok, s, -jnp.inf)
                sink = sink_ref[h]
                m = jnp.maximum(jnp.max(s, axis=-1, keepdims=True), sink)
                p = jnp.exp(s - m)
                denom = jnp.sum(p, axis=-1, keepdims=True) + jnp.exp(sink - m)
                outs.append(_dot(p.astype(BF16), v2) / denom)
        o_ref[b] = jnp.concatenate(outs, axis=1)


def _swa(qkv, sinks, bias):
    batch, seq, _ = qkv.shape
    kcol = ATTN_WIDTH // KV_WIDTH
    prev = lambda i: jnp.maximum(i - 1, 0)
    return pl.pallas_call(
        _swa_kernel,
        name="swa",
        grid_spec=pltpu.PrefetchScalarGridSpec(
            num_scalar_prefetch=1,
            grid=(seq // BLOCK,),
            in_specs=[
                pl.BlockSpec((batch, BLOCK, ATTN_WIDTH), lambda i, s: (0, i, 0)),
                pl.BlockSpec((batch, BLOCK, KV_WIDTH), lambda i, s: (0, i, kcol)),
                pl.BlockSpec((batch, BLOCK, KV_WIDTH), lambda i, s: (0, i, kcol + 1)),
                pl.BlockSpec((batch, BLOCK, KV_WIDTH), lambda i, s: (0, prev(i), kcol)),
                pl.BlockSpec((batch, BLOCK, KV_WIDTH), lambda i, s: (0, prev(i), kcol + 1)),
                pl.BlockSpec((ATTN_HEADS, BLOCK, 2 * BLOCK), lambda i, s: (0, 0, 0)),
            ],
            out_specs=pl.BlockSpec((batch, BLOCK, ATTN_WIDTH), lambda i, s: (0, i, 0)),
        ),
        out_shape=jax.ShapeDtypeStruct((batch, seq, ATTN_WIDTH), F32),
        compiler_params=_params(("arbitrary",)),
    )(sinks, qkv, qkv, qkv, qkv, qkv, bias)


def _memkv_kernel(m_ref, g_ref, w_ref, k_ref, v_ref):
    u = _rms(m_ref[...], g_ref[...])
    kv = _dot(u.astype(BF16), w_ref[...].astype(BF16))
    k_ref[...] = kv[:, :CROSS_WIDTH].astype(BF16)
    v_ref[...] = kv[:, CROSS_WIDTH:].astype(BF16)


def _memkv(mem2, g_mem, w_kv):
    n = mem2.shape[0]
    return pl.pallas_call(
        _memkv_kernel,
        out_shape=[jax.ShapeDtypeStruct((n, CROSS_WIDTH), BF16)] * 2,
        compiler_params=_params(),
    )(mem2, g_mem, w_kv)


def _mid_kernel(ys_ref, ya_ref, x_ref, wout_f32, bout_ref, gc_ref, wq_f32, k_ref, v_ref, wo_f32,
                gf_ref, wr_ref, br_ref, upper_ref, lower_ref,
                h2_ref, u3_ref, gate_ref, lp_ref, tcnt_ref, tbase_ref, run_ref, wout_ref, wq_ref, wo_ref):
    i = pl.program_id(0)

    @pl.when(i == 0)
    def _():
        run_ref[...] = jnp.zeros_like(run_ref)
        wout_ref[...] = wout_f32[...].astype(BF16)
        wq_ref[...] = wq_f32[...].astype(BF16)
        wo_ref[...] = wo_f32[...].astype(BF16)

    u3_parts = []
    for part in range(MID_PARTS):
        rows = slice(part * (TM_MID // MID_PARTS), (part + 1) * (TM_MID // MID_PARTS))
        ycat = jnp.concatenate([ys_ref[rows, :], ya_ref[rows, :]], axis=1).astype(BF16)
        h1 = x_ref[rows, :] + _dot(ycat, wout_ref[...]) + bout_ref[...]

        u2 = _rms(h1, gc_ref[...])
        q = _dot(u2.astype(BF16), wq_ref[...]).astype(BF16)
        outs = []
        for h in range(CROSS_HEADS):
            hs = slice(h * CROSS_HEAD_DIM, (h + 1) * CROSS_HEAD_DIM)
            s = _nt(q[:, hs], k_ref[:, hs]) * (CROSS_HEAD_DIM ** -0.5)
            m = jnp.max(s, axis=-1, keepdims=True)
            p = jnp.exp(s - m)
            outs.append(_dot(p.astype(BF16), v_ref[:, hs]) / jnp.sum(p, axis=-1, keepdims=True))
        o = jnp.concatenate(outs, axis=1).astype(BF16)
        h2 = h1 + _dot(o, wo_ref[...])
        h2_ref[rows, :] = h2
        u3_parts.append(_rms(h2, gf_ref[...]).astype(BF16))
    u3 = jnp.concatenate(u3_parts, axis=0)
    u3_ref[...] = u3

    logits = _nt(wr_ref[...], u3) + br_ref[...]
    eid = lax.broadcasted_iota(I32, logits.shape, 0)
    vals, idxs, hots = [], [], []
    for _ in range(TOP_K):
        m = jnp.max(logits, axis=0, keepdims=True)
        ix = jnp.min(jnp.where(logits == m, eid, N_EXPERTS), axis=0, keepdims=True)
        hot = eid == ix
        logits = jnp.where(hot, -jnp.inf, logits)
        vals.append(m)
        idxs.append(ix)
        hots.append(hot)
    ex = [jnp.exp(v - vals[0]) for v in vals]
    tot = ex[0] + ex[1] + ex[2] + ex[3]
    gate_ref[...] = jnp.concatenate([e / tot for e in ex], axis=0)

    hot_f = [jnp.where(hot, 1.0, 0.0) for hot in hots]
    cnt_k = [jnp.sum(hf, axis=1, keepdims=True) for hf in hot_f]
    tile_cnt = cnt_k[0] + cnt_k[1] + cnt_k[2] + cnt_k[3]
    tile_cnt = jnp.floor((tile_cnt + (SUBLANES - 1)) * (1.0 / SUBLANES)) * SUBLANES
    offs = _dot_exact(lower_ref[...], jnp.broadcast_to(tile_cnt, (N_EXPERTS, LANES)))[:, :1]
    local = []
    before_all = _dot(jnp.concatenate(hot_f, axis=0).astype(BF16), upper_ref[...])
    for kk in range(TOP_K):
        before = before_all[kk * N_EXPERTS:(kk + 1) * N_EXPERTS, :]
        local.append(jnp.sum(jnp.where(hots[kk], before + offs, 0.0), axis=0, keepdims=True))
        offs = offs + cnt_k[kk]
    lp_ref[...] = jnp.concatenate(local, axis=0).astype(I32)
    tcnt_ref[0] = jnp.broadcast_to(tile_cnt, (N_EXPERTS, LANES)).astype(I32)
    tbase_ref[0] = jnp.broadcast_to(run_ref[...], (N_EXPERTS, LANES)).astype(I32)
    run_ref[...] = run_ref[...] + tile_cnt


def _mid(y_ssm, y_attn, x2, w_out, b_out, g_cross, w_q, k_mem, v_mem, w_o, g_ffn, w_r_t, b_r, seq, mem_len):
    n = x2.shape[0]
    tiles_per_batch = seq // TM_MID
    row = lambda i: (i, 0)
    col = lambda i: (0, i)
    fixed = lambda i: (0, 0)
    memb = lambda i: (i // tiles_per_batch, 0)
    tile3 = lambda i: (i, 0, 0)
    n_tiles = n // TM_MID
    upper = jnp.asarray(np.triu(np.ones((TM_MID, TM_MID), np.float32), 1)).astype(BF16)
    lower = jnp.asarray(np.tril(np.ones((N_EXPERTS, N_EXPERTS), np.float32), -1))
    return pl.pallas_call(
        _mid_kernel,
        name="mid",
        grid=(n_tiles,),
        in_specs=[
            pl.BlockSpec((TM_MID, SSM_WIDTH), row),
            pl.BlockSpec((TM_MID, ATTN_WIDTH), row),
            pl.BlockSpec((TM_MID, D_MODEL), row),
            pl.BlockSpec((D_MODEL, D_MODEL), fixed),
            pl.BlockSpec((1, D_MODEL), fixed),
            pl.BlockSpec((1, D_MODEL), fixed),
            pl.BlockSpec((D_MODEL, CROSS_WIDTH), fixed),
            pl.BlockSpec((mem_len, CROSS_WIDTH), memb),
            pl.BlockSpec((mem_len, CROSS_WIDTH), memb),
            pl.BlockSpec((CROSS_WIDTH, D_MODEL), fixed),
            pl.BlockSpec((1, D_MODEL), fixed),
            pl.BlockSpec((N_EXPERTS, D_MODEL), fixed),
            pl.BlockSpec((N_EXPERTS, 1), fixed),
            pl.BlockSpec((TM_MID, TM_MID), fixed),
            pl.BlockSpec((N_EXPERTS, N_EXPERTS), fixed),
        ],
        out_specs=[
            pl.BlockSpec((TM_MID, D_MODEL), row),
            pl.BlockSpec((TM_MID, D_MODEL), row),
            pl.BlockSpec((TOP_K, TM_MID), col),
            pl.BlockSpec((TOP_K, TM_MID), col),
            pl.BlockSpec((1, N_EXPERTS, LANES), tile3),
            pl.BlockSpec((1, N_EXPERTS, LANES), tile3),
        ],
        out_shape=[
            jax.ShapeDtypeStruct((n, D_MODEL), F32),
            jax.ShapeDtypeStruct((n, D_MODEL), BF16),
            jax.ShapeDtypeStruct((TOP_K, n), F32),
            jax.ShapeDtypeStruct((TOP_K, n), I32),
            jax.ShapeDtypeStruct((n_tiles, N_EXPERTS, LANES), I32),
            jax.ShapeDtypeStruct((n_tiles, N_EXPERTS, LANES), I32),
        ],
        scratch_shapes=[
            pltpu.VMEM((N_EXPERTS, 1), F32),
            pltpu.VMEM((D_MODEL, D_MODEL), BF16),
            pltpu.VMEM((D_MODEL, CROSS_WIDTH), BF16),
            pltpu.VMEM((CROSS_WIDTH, D_MODEL), BF16),
        ],
        compiler_params=_params(("arbitrary",)),
    )(y_ssm, y_attn, x2, w_out, b_out, g_cross, w_q, k_mem, v_mem, w_o, g_ffn, w_r_t, b_r, upper, lower)


ROW_CHUNK = 256
LOCAL_ROWS = -(-(TM_MID * TOP_K + N_EXPERTS * (SUBLANES - 1)) // ROW_CHUNK) * ROW_CHUNK
RUN_BITS = tuple(1 << s for s in range(TM_MID.bit_length() - 1, SUBLANES.bit_length() - 2, -1))
PAD_BITS = tuple(1 << s for s in range((MOE_ROWS - 1).bit_length() - 1, SUBLANES.bit_length() - 2, -1))


U32 = jnp.uint32
PACKED = D_MODEL // 2


def _pack_pairs(v):
    bits = lax.bitcast_convert_type(v, U32)
    return bits[:, PACKED:] | (bits[:, :PACKED] >> 16)


def _unpack_pairs(w):
    lo = lax.bitcast_convert_type(w << 16, F32)
    hi = lax.bitcast_convert_type(w & jnp.uint32(0xFFFF0000), F32)
    return jnp.concatenate([lo, hi], axis=1).astype(BF16)


def _run_copies(n, src_ref, src, dst_ref, dst, sem, bits, wait, advance_src=True):
    for bit in bits:
        step = n & bit

        @pl.when(step != 0)
        def _():
            cp = pltpu.make_async_copy(src_ref.at[pl.ds(pl.multiple_of(src, SUBLANES), bit)],
                                       dst_ref.at[pl.ds(pl.multiple_of(dst, SUBLANES), bit)], sem)
            if wait:
                cp.wait()
            else:
                cp.start()

        if advance_src:
            src = src + step
        dst = dst + step


def _wait_rows(n, src_ref, dst_ref, sem):
    rows = pl.ds(0, pl.multiple_of(n, SUBLANES))
    pltpu.make_async_copy(src_ref.at[rows], dst_ref.at[rows], sem).wait()


def _dispatch_kernel(tcnt_ref, lo_ref, gs_ref, padn_ref, pads_ref, na_ref, u_ref, lp_ref, xs_ref,
                     xloc_ref, zero_ref, sem, zsem):
    i = pl.program_id(0)

    def zero_fill(wait):
        def body(e, carry):
            _run_copies(padn_ref[e], zero_ref, 0, xs_ref, pads_ref[e], zsem, PAD_BITS, wait, advance_src=False)
            return carry
        lax.fori_loop(0, N_EXPERTS, body, 0)

        def tail(b, carry):
            cp = pltpu.make_async_copy(zero_ref, xs_ref.at[pl.ds(pl.multiple_of(b * MOE_ROWS, MOE_ROWS), MOE_ROWS)], zsem)
            if wait:
                cp.wait()
            else:
                cp.start()
            return carry
        lax.fori_loop(na_ref[0], xs_ref.shape[0] // MOE_ROWS, tail, 0)

    @pl.when(i == 0)
    def _():
        zero_ref[...] = jnp.zeros_like(zero_ref)
        zero_fill(False)

    slot = i % 2
    u = u_ref[...]
    lp = lp_ref[...]
    for c in range(LOCAL_ROWS // ROW_CHUNK):
        r = lax.broadcasted_iota(I32, (ROW_CHUNK, TM_MID), 0) + c * ROW_CHUNK
        p = jnp.zeros((ROW_CHUNK, TM_MID), F32)
        for kk in range(TOP_K):
            p = jnp.where(lp[kk:kk + 1, :] == r, 1.0, p)
        xloc_ref[slot, c * ROW_CHUNK:(c + 1) * ROW_CHUNK, :] = _pack_pairs(_dot(p.astype(BF16), u))

    def start_runs(tile):
        s = tile % 2

        def body(e, carry):
            j = tile * N_EXPERTS + e
            _run_copies(tcnt_ref[j], xloc_ref.at[s], lo_ref[j], xs_ref, gs_ref[j], sem.at[s], RUN_BITS, False)
            return carry
        lax.fori_loop(0, N_EXPERTS, body, 0)

    def wait_runs(tile):
        s = tile % 2
        last = tile * N_EXPERTS + N_EXPERTS - 1
        _wait_rows(lo_ref[last] + tcnt_ref[last], xloc_ref.at[s], xs_ref, sem.at[s])

    start_runs(i)

    @pl.when(i > 0)
    def _():
        wait_runs(i - 1)

    @pl.when(i == pl.num_programs(0) - 1)
    def _():
        wait_runs(i)
        zero_fill(True)


def _dispatch(tcnt, lo, gstart, padn, pads, n_active, u3, lp, n_rows):
    n = u3.shape[0]
    return pl.pallas_call(
        _dispatch_kernel,
        name="dispatch",
        grid_spec=pltpu.PrefetchScalarGridSpec(
            num_scalar_prefetch=6,
            grid=(n // TM_MID,),
            in_specs=[
                pl.BlockSpec((TM_MID, D_MODEL), lambda i, *_: (i, 0)),
                pl.BlockSpec((TOP_K, TM_MID), lambda i, *_: (0, i)),
            ],
            out_specs=pl.BlockSpec(memory_space=pl.ANY),
            scratch_shapes=[
                pltpu.VMEM((2, LOCAL_ROWS, PACKED), U32),
                pltpu.VMEM((MOE_ROWS, PACKED), U32),
                pltpu.SemaphoreType.DMA((2,)),
                pltpu.SemaphoreType.DMA,
            ],
        ),
        out_shape=jax.ShapeDtypeStruct((n_rows, PACKED), U32),
        compiler_params=_params(("arbitrary",)),
    )(tcnt, lo, gstart, padn, pads, n_active, u3, lp)


PAIR_COLS = 2 * LANES


def _expert_kernel(be_ref, nx_ref, nv_ref, na_ref, x_ref, b1_ref, b2_ref, perm_ref, w1_hbm, w2_hbm, y_ref,
                   w1s_ref, w2s_ref, w1p_ref, w2b_ref, sem):
    b = pl.program_id(0)
    valid_rows = nv_ref[b]
    active = valid_rows > 0
    expert = be_ref[b]
    new_expert = jnp.logical_or(b == 0, expert != be_ref[jnp.maximum(b - 1, 0)])

    def fetch(e, wait):
        for src, dst, s in ((w1_hbm, w1s_ref, 0), (w2_hbm, w2s_ref, 1)):
            for r in range(0, src.shape[1], WEIGHT_DMA_ROWS):
                rs = pl.ds(r, WEIGHT_DMA_ROWS)
                cp = pltpu.make_async_copy(src.at[e, rs], dst.at[rs], sem.at[s])
                if wait:
                    cp.wait()
                else:
                    cp.start(priority=1)

    @pl.when(jnp.logical_and(b == 0, active))
    def _():
        fetch(expert, False)

    @pl.when(jnp.logical_and(active, new_expert))
    def _():
        fetch(expert, True)
        for c in range(2 * D_FF // PAIR_COLS):
            cs = slice(c * PAIR_COLS, (c + 1) * PAIR_COLS)
            w1p_ref[:, cs] = _dot(w1s_ref[:, cs].astype(BF16), perm_ref[...]).astype(BF16)
        w2b_ref[...] = w2s_ref[...].astype(BF16)

        @pl.when(nx_ref[b] >= 0)
        def _():
            fetch(nx_ref[b], False)

    for live in range(MOE_PART, MOE_ROWS + 1, MOE_PART):
        @pl.when(jnp.logical_and(valid_rows > live - MOE_PART, valid_rows <= live))
        def _():
            x = _unpack_pairs(x_ref[:live, :])
            hdn = _dot(x, w1p_ref[...]) + b1_ref[0]
            acts = []
            for c in range(2 * D_FF // PAIR_COLS):
                hg = jnp.minimum(hdn[:, c * PAIR_COLS:c * PAIR_COLS + LANES], SWIGLU_LIMIT)
                hl = jnp.clip(hdn[:, c * PAIR_COLS + LANES:(c + 1) * PAIR_COLS], -SWIGLU_LIMIT, SWIGLU_LIMIT)
                acts.append((hg * _sigmoid(SWIGLU_ALPHA * hg) * (hl + 1.0)).astype(BF16))
            y = _dot(jnp.concatenate(acts, axis=1), w2b_ref[...]) + b2_ref[0]
            y_ref[:live, :] = _pack_pairs(y.astype(BF16).astype(F32))
            if live < MOE_ROWS:
                y_ref[live:, :] = jnp.zeros((MOE_ROWS - live, PACKED), U32)

    @pl.when(jnp.logical_not(active))
    def _():
        y_ref[...] = jnp.zeros_like(y_ref)


def _experts(block_exp, next_exp, valid_rows, n_active, xs, w1, b1p, w2, b2):
    n_rows = xs.shape[0]
    n_blocks = n_rows // MOE_ROWS
    xmap = lambda b, be, nx, nv, na: (jnp.maximum(jnp.minimum(b, na[0] - 1), 0), 0)
    emap = lambda b, be, nx, nv, na: (be[b], 0, 0)
    perm = np.zeros((PAIR_COLS, PAIR_COLS), np.float32)
    perm[2 * np.arange(LANES), np.arange(LANES)] = 1.0
    perm[2 * np.arange(LANES) + 1, LANES + np.arange(LANES)] = 1.0
    return pl.pallas_call(
        _expert_kernel,
        name="experts",
        grid_spec=pltpu.PrefetchScalarGridSpec(
            num_scalar_prefetch=4,
            grid=(n_blocks,),
            in_specs=[
                pl.BlockSpec((MOE_ROWS, PACKED), xmap),
                pl.BlockSpec((1, 1, 2 * D_FF), emap),
                pl.BlockSpec((1, 1, D_MODEL), emap),
                pl.BlockSpec((PAIR_COLS, PAIR_COLS), lambda b, *_: (0, 0)),
                pl.BlockSpec(memory_space=pl.ANY),
                pl.BlockSpec(memory_space=pl.ANY),
            ],
            out_specs=pl.BlockSpec((MOE_ROWS, PACKED), lambda b, *_: (b, 0)),
            scratch_shapes=[
                pltpu.VMEM((D_MODEL, 2 * D_FF), F32),
                pltpu.VMEM((D_FF, D_MODEL), F32),
                pltpu.VMEM((D_MODEL, 2 * D_FF), BF16),
                pltpu.VMEM((D_FF, D_MODEL), BF16),
                pltpu.SemaphoreType.DMA((2,)),
            ],
        ),
        out_shape=jax.ShapeDtypeStruct((n_rows, PACKED), U32),
        compiler_params=_params(("arbitrary",)),
    )(block_exp, next_exp, valid_rows, n_active, xs, b1p, b2, jnp.asarray(perm).astype(BF16), w1, w2)


def _combine_kernel(tcnt_ref, lo_ref, gs_ref, h2_ref, gate_ref, lp_ref, g_ref, ys_ref, o_ref, yloc_ref, sem):
    i = pl.program_id(0)

    slot = i % 2

    def start_runs(tile):
        s = tile % 2

        def body(e, carry):
            j = tile * N_EXPERTS + e
            _run_copies(tcnt_ref[j], ys_ref, gs_ref[j], yloc_ref.at[s], lo_ref[j], sem.at[s], RUN_BITS, False)
            return carry
        lax.fori_loop(0, N_EXPERTS, body, 0)

    @pl.when(i == 0)
    def _():
        yloc_ref[...] = jnp.zeros_like(yloc_ref)
        start_runs(i)

    @pl.when(i + 1 < pl.num_programs(0))
    def _():
        start_runs(i + 1)

    gates = gate_ref[...]
    lp = lp_ref[...]
    last = i * N_EXPERTS + N_EXPERTS - 1
    _wait_rows(lo_ref[last] + tcnt_ref[last], ys_ref, yloc_ref.at[slot], sem.at[slot])

    h = h2_ref[...]
    for c in range(LOCAL_ROWS // ROW_CHUNK):
        r = lax.broadcasted_iota(I32, (ROW_CHUNK, TM_MID), 0) + c * ROW_CHUNK
        a = jnp.zeros((ROW_CHUNK, TM_MID), F32)
        for kk in range(TOP_K):
            a = jnp.where(lp[kk:kk + 1, :] == r, gates[kk:kk + 1, :], a)
        h = h + _tn(a.astype(BF16), _unpack_pairs(yloc_ref[slot, c * ROW_CHUNK:(c + 1) * ROW_CHUNK, :]))
    o_ref[...] = _rms(h, g_ref[...])


def _combine(tcnt, lo, gstart, h2, gates, lp, g_final, ys):
    n = h2.shape[0]
    return pl.pallas_call(
        _combine_kernel,
        name="combine",
        grid_spec=pltpu.PrefetchScalarGridSpec(
            num_scalar_prefetch=3,
            grid=(n // TM_MID,),
            in_specs=[
                pl.BlockSpec((TM_MID, D_MODEL), lambda i, *_: (i, 0)),
                pl.BlockSpec((TOP_K, TM_MID), lambda i, *_: (0, i)),
                pl.BlockSpec((TOP_K, TM_MID), lambda i, *_: (0, i)),
                pl.BlockSpec((1, D_MODEL), lambda i, *_: (0, 0)),
                pl.BlockSpec(memory_space=pl.ANY),
            ],
            out_specs=pl.BlockSpec((TM_MID, D_MODEL), lambda i, *_: (i, 0)),
            scratch_shapes=[
                pltpu.VMEM((2, LOCAL_ROWS, PACKED), U32),
                pltpu.SemaphoreType.DMA((2,)),
            ],
        ),
        out_shape=jax.ShapeDtypeStruct((n, D_MODEL), F32),
        compiler_params=_params(("arbitrary",)),
    )(tcnt, lo, gstart, h2, gates, lp, g_final, ys)


def _layer(h, mem2, batch, seq, mem_len, g_mix, w_in, b_qkv, conv_w, conv_b, dt_bias, a_log, d_skip,
           g_ssm_out, attn_sinks, bias, w_out, b_out, g_cross, g_mem, w_q_cross, w_kv_cross, w_o_cross,
           g_ffn, w_router, b_router, w1, b1, w2, b2, g_final):
    n = h.shape[0]
    z, xbc, dtp, qkv = _inproj(h, g_mix[None, :], w_in, b_qkv[None, :])

    pad_h = (0, DT_PAD - SSM_HEADS)
    dt_bias_p = jnp.pad(dt_bias, pad_h)[None, :]
    a_head_p = jnp.pad(-jnp.exp(a_log), pad_h)[None, :]
    d_full = jnp.repeat(d_skip, SSM_HEAD_DIM)[None, :]
    per_seq = lambda t: t.reshape(batch, seq, t.shape[-1])
    y_ssm = _ssd(per_seq(xbc), per_seq(z), per_seq(dtp), conv_w, conv_b[None, :], dt_bias_p, a_head_p,
                 d_full, g_ssm_out[None, :]).reshape(n, SSM_WIDTH)
    y_attn = _swa(per_seq(qkv), attn_sinks, bias).reshape(n, ATTN_WIDTH)

    k_mem, v_mem = _memkv(mem2, g_mem[None, :], w_kv_cross)
    h2, u3, gates, lp, tcnt, tbase = _mid(
        y_ssm, y_attn, h, w_out, b_out[None, :], g_cross[None, :],
        w_q_cross, k_mem, v_mem, w_o_cross, g_ffn[None, :],
        w_router.T.astype(BF16), b_router[:, None], seq, mem_len)

    max_rows = n * TOP_K + (n // TM_MID) * N_EXPERTS * (SUBLANES - 1)
    n_blocks = -(-max_rows // MOE_ROWS) + N_EXPERTS
    tcnt = tcnt[:, :, 0]
    tbase = tbase[:, :, 0]
    counts = tbase[-1] + tcnt[-1]
    padded = (counts + MOE_ROWS - 1) // MOE_ROWS * MOE_ROWS
    padded_end = jnp.cumsum(padded)
    start = (padded_end - padded).astype(I32)
    n_active = (padded_end[-1] // MOE_ROWS).astype(I32)
    blk = jnp.arange(n_blocks, dtype=I32)
    blk_exp = jnp.sum((blk * MOE_ROWS)[:, None] >= padded_end[None, :], axis=1)
    eid = jnp.arange(N_EXPERTS, dtype=I32)
    last_exp = jnp.max(jnp.where(padded > 0, eid, 0))
    blk_exp = jnp.where(blk < n_active, jnp.minimum(blk_exp, N_EXPERTS - 1), last_exp).astype(I32)
    later = jnp.logical_and(padded[None, :] > 0, eid[None, :] > eid[:, None])
    next_of = jnp.min(jnp.where(later, eid[None, :], N_EXPERTS), axis=1)
    next_of = jnp.where(next_of < N_EXPERTS, next_of, -1).astype(I32)
    of_blk = blk_exp[:, None] == eid[None, :]
    blk_next = jnp.sum(jnp.where(of_blk, next_of[None, :], 0), axis=1).astype(I32)
    blk_end = jnp.sum(jnp.where(of_blk, (start + counts)[None, :], 0), axis=1)
    blk_valid = jnp.clip(blk_end - blk * MOE_ROWS, 0, MOE_ROWS)
    blk_valid = jnp.where(blk < n_active, blk_valid, 0).astype(I32)

    run_n = tcnt.reshape(-1)
    run_local = (jnp.cumsum(tcnt, axis=1) - tcnt).reshape(-1)
    run_global = (start[None, :] + tbase).reshape(-1)
    xs = _dispatch(run_n, run_local, run_global, padded - counts, start + counts, n_active[None], u3, lp,
                   n_blocks * MOE_ROWS)

    b1p = b1.reshape(N_EXPERTS, 2 * D_FF // PAIR_COLS, LANES, 2).transpose(0, 1, 3, 2)
    ys = _experts(blk_exp, blk_next, blk_valid, n_active[None], xs, w1,
                  b1p.reshape(N_EXPERTS, 1, 2 * D_FF), w2, b2[:, None, :])
    return _combine(run_n, run_local, run_global, h2, gates, lp, g_final, ys)


def kernel(x, mem, g_mix, w_in, b_qkv, conv_w, conv_b, dt_bias, a_log, d_skip, g_ssm_out, attn_sinks, rel_bias, w_out, b_out, g_cross, g_mem, w_q_cross, w_kv_cross, w_o_cross, g_ffn, w_router, b_router, w1, b1, w2, b2, g_final):
    batch, seq, d = x.shape
    mem_len = mem.shape[1]
    depth = g_mix.shape[0]
    assert depth == 1 and d == D_MODEL and seq % TM_MID == 0
    h = x.reshape(batch * seq, d)
    mem2 = mem.reshape(batch * mem_len, d)
    bias = _bias_table(rel_bias)
    out = _layer(
        h, mem2, batch, seq, mem_len, g_mix[0], w_in[0], b_qkv[0], conv_w[0], conv_b[0], dt_bias[0],
        a_log[0], d_skip[0], g_ssm_out[0], attn_sinks[0], bias, w_out[0], b_out[0], g_cross[0],
        g_mem[0], w_q_cross[0], w_kv_cross[0], w_o_cross[0], g_ffn[0], w_router[0], b_router[0],
        w1[0], b1[0], w2[0], b2[0], g_final[None, :])
    return out.reshape(batch, seq, d)
```

```python
import math

import numpy as np
import jax
import jax.numpy as jnp
from jax import lax
from jax.experimental import pallas as pl
from jax.experimental.pallas import tpu as pltpu

F32 = jnp.float32
BF16 = jnp.bfloat16
I32 = jnp.int32

D_MODEL = 1024
RMS_EPS = 1e-5

SSM_HEADS = 8
SSM_HEAD_DIM = 64
SSM_WIDTH = SSM_HEADS * SSM_HEAD_DIM
SSM_STATE = 128
SSM_GROUPS = 2
HEADS_PER_GROUP = SSM_HEADS // SSM_GROUPS
GROUP_WIDTH = SSM_WIDTH // SSM_GROUPS
CONV_WIDTH = 4
CHUNK = 128
CONV_CH = SSM_WIDTH + 2 * SSM_GROUPS * SSM_STATE

ATTN_HEADS = 8
KV_HEADS = 2
Q_PER_KV = ATTN_HEADS // KV_HEADS
HEAD_DIM = 64
ATTN_WIDTH = ATTN_HEADS * HEAD_DIM
KV_WIDTH = KV_HEADS * HEAD_DIM
WINDOW = 128
BLOCK = WINDOW
QKV_COLS = ATTN_WIDTH + 2 * KV_WIDTH

REL_BUCKETS = 32
REL_MAX_DIST = 128

CROSS_HEADS = 4
CROSS_HEAD_DIM = 128
CROSS_WIDTH = CROSS_HEADS * CROSS_HEAD_DIM

N_EXPERTS = 32
TOP_K = 4
D_FF = D_MODEL
SWIGLU_ALPHA = 1.702
SWIGLU_LIMIT = 7.0

LANES = 128
SUBLANES = 8
DT_PAD = LANES
PROJ_COLS = SSM_WIDTH + CONV_CH + DT_PAD + QKV_COLS
VMEM_LIMIT = 56 * 1024 * 1024

TM_PROJ = 512
TM_MID = 512
SSD_ROWS = 4 * CHUNK
MOE_ROWS = 512
MOE_PART = 128


def _nt(a, b):
    return lax.dot_general(a, b, (((1,), (1,)), ((), ())), preferred_element_type=F32)


def _tn(a, b):
    return lax.dot_general(a, b, (((0,), (0,)), ((), ())), preferred_element_type=F32)


def _dot(a, b):
    return jnp.dot(a, b, preferred_element_type=F32)


def _dot_exact(a, b):
    return jnp.dot(a, b, preferred_element_type=F32, precision=lax.Precision.HIGHEST)


def _rms(x, g):
    return x * lax.rsqrt(jnp.mean(x * x, axis=-1, keepdims=True) + RMS_EPS) * g


def _sigmoid(x):
    return 0.5 * jnp.tanh(0.5 * x) + 0.5


def _params(sem=None, **kwargs):
    return pltpu.CompilerParams(dimension_semantics=sem, vmem_limit_bytes=VMEM_LIMIT, **kwargs)


def _inproj_kernel(x_ref, g_ref, w_f32, bq_ref, z_ref, xbc_ref, dt_ref, qkv_ref, w_ref):
    @pl.when(pl.program_id(0) == 0)
    def _():
        c1 = SSM_WIDTH + CONV_CH
        w_ref[:, :c1] = w_f32[:, :c1].astype(BF16)
        lane = lax.broadcasted_iota(I32, (D_MODEL, DT_PAD), 1)
        w_ref[:, c1:c1 + DT_PAD] = jnp.where(lane < SSM_HEADS, w_f32[:, c1:c1 + DT_PAD], 0.0).astype(BF16)
        w_ref[:, c1 + DT_PAD:] = w_f32[:, c1 + SSM_HEADS:].astype(BF16)

    u = _rms(x_ref[...], g_ref[...])
    p = _dot(u.astype(BF16), w_ref[...])
    z_ref[...] = p[:, :SSM_WIDTH]
    xbc_ref[...] = p[:, SSM_WIDTH:SSM_WIDTH + CONV_CH]
    dt_ref[...] = p[:, SSM_WIDTH + CONV_CH:SSM_WIDTH + CONV_CH + DT_PAD]
    qkv_ref[...] = p[:, SSM_WIDTH + CONV_CH + DT_PAD:] + bq_ref[...]


def _inproj(x2, g_mix, w_in, b_qkv):
    n = x2.shape[0]
    row = lambda i: (i, 0)
    fixed = lambda i: (0, 0)
    return pl.pallas_call(
        _inproj_kernel,
        name="inproj",
        grid=(n // TM_PROJ,),
        in_specs=[
            pl.BlockSpec((TM_PROJ, D_MODEL), row),
            pl.BlockSpec((1, D_MODEL), fixed),
            pl.BlockSpec(w_in.shape, fixed),
            pl.BlockSpec((1, QKV_COLS), fixed),
        ],
        out_specs=[
            pl.BlockSpec((TM_PROJ, SSM_WIDTH), row),
            pl.BlockSpec((TM_PROJ, CONV_CH), row),
            pl.BlockSpec((TM_PROJ, DT_PAD), row),
            pl.BlockSpec((TM_PROJ, QKV_COLS), row),
        ],
        out_shape=[
            jax.ShapeDtypeStruct((n, SSM_WIDTH), F32),
            jax.ShapeDtypeStruct((n, CONV_CH), F32),
            jax.ShapeDtypeStruct((n, DT_PAD), F32),
            jax.ShapeDtypeStruct((n, QKV_COLS), F32),
        ],
        scratch_shapes=[pltpu.VMEM((D_MODEL, PROJ_COLS), BF16)],
        compiler_params=_params(("arbitrary",)),
    )(x2, g_mix, w_in, b_qkv)


def _split3(x):
    hi = x.astype(BF16)
    rest = x - hi.astype(F32)
    mid = rest.astype(BF16)
    lo = (rest - mid.astype(F32)).astype(BF16)
    return jnp.concatenate([hi, mid, lo], axis=1)


def _ssd_kernel(xbc_ref, z_ref, dt_ref, cw_ref, cb_ref, dtb_ref, ah_ref, dsk_ref, gout_ref,
                e64_ref, ecat_ref, tri_ref, y_ref, conv_ref, state_ref):
    c = pl.program_id(0)

    @pl.when(c == 0)
    def _():
        conv_ref[:, :SUBLANES, :] = jnp.zeros((conv_ref.shape[0], SUBLANES, CONV_CH), F32)
        state_ref[...] = jnp.zeros_like(state_ref)

    for cc in range(xbc_ref.shape[1] // CHUNK):
        for b in range(xbc_ref.shape[0]):
            _ssd_chunk(b, slice(cc * CHUNK, (cc + 1) * CHUNK), xbc_ref, z_ref, dt_ref, cw_ref, cb_ref, dtb_ref,
                       ah_ref, dsk_ref, gout_ref, e64_ref, ecat_ref, tri_ref, y_ref, conv_ref, state_ref)


def _ssd_chunk(b, rows, xbc_ref, z_ref, dt_ref, cw_ref, cb_ref, dtb_ref, ah_ref, dsk_ref, gout_ref,
               e64_ref, ecat_ref, tri_ref, y_ref, conv_ref, state_ref):
    u = xbc_ref[b, rows, :]
    conv_ref[b, SUBLANES:, :] = u
    acc = u * cw_ref[CONV_WIDTH - 1:CONV_WIDTH, :] + cb_ref[...]
    for j in range(1, CONV_WIDTH):
        shifted = conv_ref[b, SUBLANES - j:SUBLANES - j + CHUNK, :]
        acc = acc + shifted * cw_ref[CONV_WIDTH - 1 - j:CONV_WIDTH - j, :]
    conv_ref[b, :SUBLANES, :] = u[CHUNK - SUBLANES:, :]
    xbc = acc * _sigmoid(acc)

    xs = xbc[:, :SSM_WIDTH]
    bm = xbc[:, SSM_WIDTH:SSM_WIDTH + SSM_GROUPS * SSM_STATE]
    cm = xbc[:, SSM_WIDTH + SSM_GROUPS * SSM_STATE:]

    dtr = dt_ref[b, rows, :] + dtb_ref[...]
    dt = jnp.maximum(dtr, 0.0) + jnp.log1p(jnp.exp(-jnp.abs(dtr)))
    a = dt * ah_ref[...]
    a3 = _split3(a)
    a_rows = jnp.concatenate([a3[:, :DT_PAD], a3[:, DT_PAD:2 * DT_PAD], a3[:, 2 * DT_PAD:]], axis=0)
    cs = _dot(tri_ref[...], a_rows)
    cs_row = cs.T
    dt_full = _dot(_split3(dt), e64_ref[...])
    cs_cat = _dot(_split3(cs), ecat_ref[...])
    cs_wide = cs_cat[:, :SSM_HEADS * CHUNK]
    cs_full = cs_cat[:, SSM_HEADS * CHUNK:]
    cs_last = cs_full[CHUNK - 1:CHUNK, :]

    xdt = xs * dt_full
    xdt_b = xdt.astype(BF16)
    xw_b = (xdt * jnp.exp(cs_last - cs_full)).astype(BF16)
    in_decay = jnp.exp(cs_full)
    chunk_decay = jnp.exp(cs_last)

    li = lax.broadcasted_iota(I32, (CHUNK, CHUNK), 0)
    si = lax.broadcasted_iota(I32, (CHUNK, CHUNK), 1)
    causal = li >= si

    ys = []
    for g in range(SSM_GROUPS):
        bg = bm[:, g * SSM_STATE:(g + 1) * SSM_STATE].astype(BF16)
        cg = cm[:, g * SSM_STATE:(g + 1) * SSM_STATE].astype(BF16)
        cb = _nt(cg, bg)
        yd = []
        for r in range(HEADS_PER_GROUP):
            h = g * HEADS_PER_GROUP + r
            diff = cs_wide[:, h * CHUNK:(h + 1) * CHUNK] - cs_row[h:h + 1, :]
            m = cb * jnp.exp(jnp.where(causal, diff, -jnp.inf))
            yd.append(_dot(m.astype(BF16), xdt_b[:, h * SSM_HEAD_DIM:(h + 1) * SSM_HEAD_DIM]))
        y_diag = jnp.concatenate(yd, axis=1)
        gs = slice(g * GROUP_WIDTH, (g + 1) * GROUP_WIDTH)
        st = state_ref[b, g]
        y_off = _dot(cg, st.astype(BF16)) * in_decay[:, gs]
        state_ref[b, g] = st * chunk_decay[:, gs] + _tn(bg, xw_b[:, gs])
        ys.append(y_diag + y_off)
    y = jnp.concatenate(ys, axis=1) + dsk_ref[...] * xs

    zz = z_ref[b, rows, :]
    y = y * (zz * _sigmoid(zz))
    outs = []
    for g in range(SSM_GROUPS):
        yg = y[:, g * GROUP_WIDTH:(g + 1) * GROUP_WIDTH]
        outs.append(yg * lax.rsqrt(jnp.mean(yg * yg, axis=-1, keepdims=True) + RMS_EPS))
    y_ref[b, rows, :] = jnp.concatenate(outs, axis=1) * gout_ref[...]


def _ssd(xbc, z, dtp, conv_w, conv_b, dt_bias_p, a_head_p, d_full, g_out):
    batch, seq, _ = xbc.shape
    chunk = lambda c: (0, c, 0)
    fixed = lambda c: (0, 0)
    head = np.arange(SSM_WIDTH) // SSM_HEAD_DIM
    e64 = (np.arange(DT_PAD)[:, None] == head[None, :]).astype(np.float32)
    head_w = np.arange(SSM_HEADS * CHUNK) // CHUNK
    e128 = (np.arange(DT_PAD)[:, None] == head_w[None, :]).astype(np.float32)
    e64_3 = jnp.asarray(np.tile(e64, (3, 1))).astype(BF16)
    ecat_3 = jnp.asarray(np.tile(np.concatenate([e128, e64], axis=1), (3, 1))).astype(BF16)
    tri_3 = jnp.asarray(np.tile(np.tril(np.ones((CHUNK, CHUNK), np.float32)), (1, 3))).astype(BF16)
    return pl.pallas_call(
        _ssd_kernel,
        name="ssd",
        grid=(seq // SSD_ROWS,),
        in_specs=[
            pl.BlockSpec((batch, SSD_ROWS, CONV_CH), chunk),
            pl.BlockSpec((batch, SSD_ROWS, SSM_WIDTH), chunk),
            pl.BlockSpec((batch, SSD_ROWS, DT_PAD), chunk),
            pl.BlockSpec((CONV_WIDTH, CONV_CH), fixed),
            pl.BlockSpec((1, CONV_CH), fixed),
            pl.BlockSpec((1, DT_PAD), fixed),
            pl.BlockSpec((1, DT_PAD), fixed),
            pl.BlockSpec((1, SSM_WIDTH), fixed),
            pl.BlockSpec((1, SSM_WIDTH), fixed),
            pl.BlockSpec((3 * DT_PAD, SSM_WIDTH), fixed),
            pl.BlockSpec((3 * DT_PAD, SSM_HEADS * CHUNK + SSM_WIDTH), fixed),
            pl.BlockSpec((CHUNK, 3 * CHUNK), fixed),
        ],
        out_specs=pl.BlockSpec((batch, SSD_ROWS, SSM_WIDTH), chunk),
        out_shape=jax.ShapeDtypeStruct((batch, seq, SSM_WIDTH), F32),
        scratch_shapes=[
            pltpu.VMEM((batch, SUBLANES + CHUNK, CONV_CH), F32),
            pltpu.VMEM((batch, SSM_GROUPS, SSM_STATE, GROUP_WIDTH), F32),
        ],
        compiler_params=_params(("arbitrary",)),
    )(xbc, z, dtp, conv_w, conv_b, dt_bias_p, a_head_p, d_full, g_out, e64_3, ecat_3, tri_3)


def _t5_bucket_table():
    q_pos = np.arange(BLOCK)[:, None]
    k_pos = np.arange(2 * BLOCK)[None, :] - BLOCK
    dist = q_pos - k_pos
    d = np.maximum(dist, 0)
    max_exact = REL_BUCKETS // 2
    ratio = np.maximum(d, max_exact).astype(np.float32) / np.float32(max_exact)
    large = max_exact + (np.log(ratio) / np.float32(math.log(REL_MAX_DIST / max_exact))
                         * np.float32(REL_BUCKETS - max_exact)).astype(np.int32)
    large = np.minimum(large, REL_BUCKETS - 1)
    bucket = np.where(d < max_exact, d, large)
    in_window = (dist >= 0) & (dist < WINDOW)
    return np.where(in_window, bucket, -1).astype(np.int32)


def _bias_kernel(rb_ref, bucket_ref, out_ref):
    bucket = bucket_ref[...]
    for h in range(ATTN_HEADS):
        acc = jnp.zeros(bucket.shape, F32)
        for b in range(REL_BUCKETS):
            acc = jnp.where(bucket == b, rb_ref[b, h], acc)
        out_ref[h] = jnp.where(bucket >= 0, acc, -jnp.inf)


def _bias_table(rel_bias):
    bucket = jnp.asarray(_t5_bucket_table())
    return pl.pallas_call(
        _bias_kernel,
        name="bias_table",
        in_specs=[
            pl.BlockSpec(memory_space=pltpu.SMEM),
            pl.BlockSpec(memory_space=pltpu.VMEM),
        ],
        out_specs=pl.BlockSpec(memory_space=pltpu.VMEM),
        out_shape=jax.ShapeDtypeStruct((ATTN_HEADS, BLOCK, 2 * BLOCK), F32),
    )(rel_bias, bucket)


def _swa_kernel(sink_ref, q_ref, k_ref, v_ref, kp_ref, vp_ref, bias_ref, o_ref):
    i = pl.program_id(0)
    col = lax.broadcasted_iota(I32, (BLOCK, 2 * BLOCK), 1)
    key_ok = jnp.logical_or(col >= BLOCK, i > 0)
    for b in range(q_ref.shape[0]):
        q = (q_ref[b] * (HEAD_DIM ** -0.5)).astype(BF16)
        outs = []
        for g in range(KV_HEADS):
            gs = slice(g * HEAD_DIM, (g + 1) * HEAD_DIM)
            k2 = jnp.concatenate([kp_ref[b, :, gs], k_ref[b, :, gs]], axis=0).astype(BF16)
            v2 = jnp.concatenate([vp_ref[b, :, gs], v_ref[b, :, gs]], axis=0).astype(BF16)
            for r in range(Q_PER_KV):
                h = g * Q_PER_KV + r
                s = _nt(q[:, h * HEAD_DIM:(h + 1) * HEAD_DIM], k2) + bias_ref[h]
                s = jnp.where(key_ok, s, -jnp.inf)
                sink = sink_ref[h]
                m = jnp.maximum(jnp.max(s, axis=-1, keepdims=True), sink)
                p = jnp.exp(s - m)
                denom = jnp.sum(p, axis=-1, keepdims=True) + jnp.exp(sink - m)
                outs.append(_dot(p.astype(BF16), v2) / denom)
        o_ref[b] = jnp.concatenate(outs, axis=1)


def _swa(qkv, sinks, bias):
    batch, seq, _ = qkv.shape
    kcol = ATTN_WIDTH // KV_WIDTH
    prev = lambda i: jnp.maximum(i - 1, 0)
    return pl.pallas_call(
        _swa_kernel,
        name="swa",
        grid_spec=pltpu.PrefetchScalarGridSpec(
            num_scalar_prefetch=1,
            grid=(seq // BLOCK,),
            in_specs=[
                pl.BlockSpec((batch, BLOCK, ATTN_WIDTH), lambda i, s: (0, i, 0)),
                pl.BlockSpec((batch, BLOCK, KV_WIDTH), lambda i, s: (0, i, kcol)),
                pl.BlockSpec((batch, BLOCK, KV_WIDTH), lambda i, s: (0, i, kcol + 1)),
                pl.BlockSpec((batch, BLOCK, KV_WIDTH), lambda i, s: (0, prev(i), kcol)),
                pl.BlockSpec((batch, BLOCK, KV_WIDTH), lambda i, s: (0, prev(i), kcol + 1)),
                pl.BlockSpec((ATTN_HEADS, BLOCK, 2 * BLOCK), lambda i, s: (0, 0, 0)),
            ],
            out_specs=pl.BlockSpec((batch, BLOCK, ATTN_WIDTH), lambda i, s: (0, i, 0)),
        ),
        out_shape=jax.ShapeDtypeStruct((batch, seq, ATTN_WIDTH), F32),
        compiler_params=_params(("arbitrary",)),
    )(sinks, qkv, qkv, qkv, qkv, qkv, bias)


def _memkv_kernel(m_ref, g_ref, w_ref, k_ref, v_ref):
    u = _rms(m_ref[...], g_ref[...])
    kv = _dot(u.astype(BF16), w_ref[...].astype(BF16))
    k_ref[...] = kv[:, :CROSS_WIDTH].astype(BF16)
    v_ref[...] = kv[:, CROSS_WIDTH:].astype(BF16)


def _memkv(mem2, g_mem, w_kv):
    n = mem2.shape[0]
    return pl.pallas_call(
        _memkv_kernel,
        name="memkv",
        out_shape=[jax.ShapeDtypeStruct((n, CROSS_WIDTH), BF16)] * 2,
        compiler_params=_params(),
    )(mem2, g_mem, w_kv)


def _mid_kernel(ys_ref, ya_ref, x_ref, wout_f32, bout_ref, gc_ref, wq_f32, k_ref, v_ref, wo_f32,
                gf_ref, wr_ref, br_ref, upper_ref, lower_ref,
                h2_ref, u3_ref, gate_ref, lp_ref, tcnt_ref, tbase_ref, run_ref, wout_ref, wq_ref, wo_ref):
    i = pl.program_id(0)

    @pl.when(i == 0)
    def _():
        run_ref[...] = jnp.zeros_like(run_ref)
        wout_ref[...] = wout_f32[...].astype(BF16)
        wq_ref[...] = wq_f32[...].astype(BF16)
        wo_ref[...] = wo_f32[...].astype(BF16)

    ycat = jnp.concatenate([ys_ref[...], ya_ref[...]], axis=1).astype(BF16)
    h1 = x_ref[...] + _dot(ycat, wout_ref[...]) + bout_ref[...]

    u2 = _rms(h1, gc_ref[...])
    q = _dot(u2.astype(BF16), wq_ref[...]).astype(BF16)
    outs = []
    for h in range(CROSS_HEADS):
        hs = slice(h * CROSS_HEAD_DIM, (h + 1) * CROSS_HEAD_DIM)
        s = _nt(q[:, hs], k_ref[:, hs]) * (CROSS_HEAD_DIM ** -0.5)
        m = jnp.max(s, axis=-1, keepdims=True)
        p = jnp.exp(s - m)
        outs.append(_dot(p.astype(BF16), v_ref[:, hs]) / jnp.sum(p, axis=-1, keepdims=True))
    o = jnp.concatenate(outs, axis=1).astype(BF16)
    h2 = h1 + _dot(o, wo_ref[...])
    h2_ref[...] = h2
    u3 = _rms(h2, gf_ref[...]).astype(BF16)
    u3_ref[...] = u3

    logits = _nt(wr_ref[...], u3) + br_ref[...]
    eid = lax.broadcasted_iota(I32, logits.shape, 0)
    vals, idxs, hots = [], [], []
    for _ in range(TOP_K):
        m = jnp.max(logits, axis=0, keepdims=True)
        ix = jnp.min(jnp.where(logits == m, eid, N_EXPERTS), axis=0, keepdims=True)
        hot = eid == ix
        logits = jnp.where(hot, -jnp.inf, logits)
        vals.append(m)
        idxs.append(ix)
        hots.append(hot)
    ex = [jnp.exp(v - vals[0]) for v in vals]
    tot = ex[0] + ex[1] + ex[2] + ex[3]
    gate_ref[...] = jnp.concatenate([e / tot for e in ex], axis=0)

    hot_f = [jnp.where(hot, 1.0, 0.0) for hot in hots]
    cnt_k = [jnp.sum(hf, axis=1, keepdims=True) for hf in hot_f]
    tile_cnt = cnt_k[0] + cnt_k[1] + cnt_k[2] + cnt_k[3]
    tile_cnt = jnp.floor((tile_cnt + (SUBLANES - 1)) * (1.0 / SUBLANES)) * SUBLANES
    offs = _dot_exact(lower_ref[...], jnp.broadcast_to(tile_cnt, (N_EXPERTS, LANES)))[:, :1]
    local = []
    before_all = _dot(jnp.concatenate(hot_f, axis=0).astype(BF16), upper_ref[...])
    for kk in range(TOP_K):
        before = before_all[kk * N_EXPERTS:(kk + 1) * N_EXPERTS, :]
        local.append(jnp.sum(jnp.where(hots[kk], before + offs, 0.0), axis=0, keepdims=True))
        offs = offs + cnt_k[kk]
    lp_ref[...] = jnp.concatenate(local, axis=0).astype(I32)
    tcnt_ref[0] = jnp.broadcast_to(tile_cnt, (N_EXPERTS, LANES)).astype(I32)
    tbase_ref[0] = jnp.broadcast_to(run_ref[...], (N_EXPERTS, LANES)).astype(I32)
    run_ref[...] = run_ref[...] + tile_cnt


def _mid(y_ssm, y_attn, x2, w_out, b_out, g_cross, w_q, k_mem, v_mem, w_o, g_ffn, w_r_t, b_r, seq, mem_len):
    n = x2.shape[0]
    tiles_per_batch = seq // TM_MID
    row = lambda i: (i, 0)
    col = lambda i: (0, i)
    fixed = lambda i: (0, 0)
    memb = lambda i: (i // tiles_per_batch, 0)
    tile3 = lambda i: (i, 0, 0)
    n_tiles = n // TM_MID
    upper = jnp.asarray(np.triu(np.ones((TM_MID, TM_MID), np.float32), 1)).astype(BF16)
    lower = jnp.asarray(np.tril(np.ones((N_EXPERTS, N_EXPERTS), np.float32), -1))
    return pl.pallas_call(
        _mid_kernel,
        name="mid",
        grid=(n_tiles,),
        in_specs=[
            pl.BlockSpec((TM_MID, SSM_WIDTH), row),
            pl.BlockSpec((TM_MID, ATTN_WIDTH), row),
            pl.BlockSpec((TM_MID, D_MODEL), row),
            pl.BlockSpec((D_MODEL, D_MODEL), fixed),
            pl.BlockSpec((1, D_MODEL), fixed),
            pl.BlockSpec((1, D_MODEL), fixed),
            pl.BlockSpec((D_MODEL, CROSS_WIDTH), fixed),
            pl.BlockSpec((mem_len, CROSS_WIDTH), memb),
            pl.BlockSpec((mem_len, CROSS_WIDTH), memb),
            pl.BlockSpec((CROSS_WIDTH, D_MODEL), fixed),
            pl.BlockSpec((1, D_MODEL), fixed),
            pl.BlockSpec((N_EXPERTS, D_MODEL), fixed),
            pl.BlockSpec((N_EXPERTS, 1), fixed),
            pl.BlockSpec((TM_MID, TM_MID), fixed),
            pl.BlockSpec((N_EXPERTS, N_EXPERTS), fixed),
        ],
        out_specs=[
            pl.BlockSpec((TM_MID, D_MODEL), row),
            pl.BlockSpec((TM_MID, D_MODEL), row),
            pl.BlockSpec((TOP_K, TM_MID), col),
            pl.BlockSpec((TOP_K, TM_MID), col),
            pl.BlockSpec((1, N_EXPERTS, LANES), tile3),
            pl.BlockSpec((1, N_EXPERTS, LANES), tile3),
        ],
        out_shape=[
            jax.ShapeDtypeStruct((n, D_MODEL), F32),
            jax.ShapeDtypeStruct((n, D_MODEL), BF16),
            jax.ShapeDtypeStruct((TOP_K, n), F32),
            jax.ShapeDtypeStruct((TOP_K, n), I32),
            jax.ShapeDtypeStruct((n_tiles, N_EXPERTS, LANES), I32),
            jax.ShapeDtypeStruct((n_tiles, N_EXPERTS, LANES), I32),
        ],
        scratch_shapes=[
            pltpu.VMEM((N_EXPERTS, 1), F32),
            pltpu.VMEM((D_MODEL, D_MODEL), BF16),
            pltpu.VMEM((D_MODEL, CROSS_WIDTH), BF16),
            pltpu.VMEM((CROSS_WIDTH, D_MODEL), BF16),
        ],
        compiler_params=_params(("arbitrary",)),
    )(y_ssm, y_attn, x2, w_out, b_out, g_cross, w_q, k_mem, v_mem, w_o, g_ffn, w_r_t, b_r, upper, lower)


ROW_CHUNK = 256
LOCAL_ROWS = -(-(TM_MID * TOP_K + N_EXPERTS * (SUBLANES - 1)) // ROW_CHUNK) * ROW_CHUNK
RUN_BITS = tuple(1 << s for s in range(TM_MID.bit_length() - 1, SUBLANES.bit_length() - 2, -1))
PAD_BITS = tuple(1 << s for s in range((MOE_ROWS - 1).bit_length() - 1, SUBLANES.bit_length() - 2, -1))


U32 = jnp.uint32
PACKED = D_MODEL // 2


def _pack_pairs(v):
    bits = lax.bitcast_convert_type(v, U32)
    return bits[:, PACKED:] | (bits[:, :PACKED] >> 16)


def _unpack_pairs(w):
    lo = lax.bitcast_convert_type(w << 16, F32)
    hi = lax.bitcast_convert_type(w & jnp.uint32(0xFFFF0000), F32)
    return jnp.concatenate([lo, hi], axis=1).astype(BF16)


def _run_copies(n, src_ref, src, dst_ref, dst, sem, bits, wait, advance_src=True):
    for bit in bits:
        step = n & bit

        @pl.when(step != 0)
        def _():
            cp = pltpu.make_async_copy(src_ref.at[pl.ds(pl.multiple_of(src, SUBLANES), bit)],
                                       dst_ref.at[pl.ds(pl.multiple_of(dst, SUBLANES), bit)], sem)
            if wait:
                cp.wait()
            else:
                cp.start()

        if advance_src:
            src = src + step
        dst = dst + step


def _wait_rows(n, src_ref, dst_ref, sem):
    rows = pl.ds(0, pl.multiple_of(n, SUBLANES))
    pltpu.make_async_copy(src_ref.at[rows], dst_ref.at[rows], sem).wait()


def _dispatch_kernel(tcnt_ref, lo_ref, gs_ref, padn_ref, pads_ref, na_ref, u_ref, lp_ref, xs_ref,
                     xloc_ref, zero_ref, sem, zsem):
    i = pl.program_id(0)

    def zero_fill(wait):
        def body(e, carry):
            _run_copies(padn_ref[e], zero_ref, 0, xs_ref, pads_ref[e], zsem, PAD_BITS, wait, advance_src=False)
            return carry
        lax.fori_loop(0, N_EXPERTS, body, 0)

        def tail(b, carry):
            cp = pltpu.make_async_copy(zero_ref, xs_ref.at[pl.ds(pl.multiple_of(b * MOE_ROWS, MOE_ROWS), MOE_ROWS)], zsem)
            if wait:
                cp.wait()
            else:
                cp.start()
            return carry
        lax.fori_loop(na_ref[0], xs_ref.shape[0] // MOE_ROWS, tail, 0)

    @pl.when(i == 0)
    def _():
        zero_ref[...] = jnp.zeros_like(zero_ref)
        zero_fill(False)

    slot = i % 2
    u = u_ref[...]
    lp = lp_ref[...]
    for c in range(LOCAL_ROWS // ROW_CHUNK):
        r = lax.broadcasted_iota(I32, (ROW_CHUNK, TM_MID), 0) + c * ROW_CHUNK
        p = jnp.zeros((ROW_CHUNK, TM_MID), F32)
        for kk in range(TOP_K):
            p = jnp.where(lp[kk:kk + 1, :] == r, 1.0, p)
        xloc_ref[slot, c * ROW_CHUNK:(c + 1) * ROW_CHUNK, :] = _pack_pairs(_dot(p.astype(BF16), u))

    def start_runs(tile):
        s = tile % 2

        def body(e, carry):
            j = tile * N_EXPERTS + e
            _run_copies(tcnt_ref[j], xloc_ref.at[s], lo_ref[j], xs_ref, gs_ref[j], sem.at[s], RUN_BITS, False)
            return carry
        lax.fori_loop(0, N_EXPERTS, body, 0)

    def wait_runs(tile):
        s = tile % 2
        last = tile * N_EXPERTS + N_EXPERTS - 1
        _wait_rows(lo_ref[last] + tcnt_ref[last], xloc_ref.at[s], xs_ref, sem.at[s])

    start_runs(i)

    @pl.when(i > 0)
    def _():
        wait_runs(i - 1)

    @pl.when(i == pl.num_programs(0) - 1)
    def _():
        wait_runs(i)
        zero_fill(True)


def _dispatch(tcnt, lo, gstart, padn, pads, n_active, u3, lp, n_rows):
    n = u3.shape[0]
    return pl.pallas_call(
        _dispatch_kernel,
        name="dispatch",
        grid_spec=pltpu.PrefetchScalarGridSpec(
            num_scalar_prefetch=6,
            grid=(n // TM_MID,),
            in_specs=[
                pl.BlockSpec((TM_MID, D_MODEL), lambda i, *_: (i, 0)),
                pl.BlockSpec((TOP_K, TM_MID), lambda i, *_: (0, i)),
            ],
            out_specs=pl.BlockSpec(memory_space=pl.ANY),
            scratch_shapes=[
                pltpu.VMEM((2, LOCAL_ROWS, PACKED), U32),
                pltpu.VMEM((MOE_ROWS, PACKED), U32),
                pltpu.SemaphoreType.DMA((2,)),
                pltpu.SemaphoreType.DMA,
            ],
        ),
        out_shape=jax.ShapeDtypeStruct((n_rows, PACKED), U32),
        compiler_params=_params(("arbitrary",)),
    )(tcnt, lo, gstart, padn, pads, n_active, u3, lp)


PAIR_COLS = 2 * LANES


def _expert_kernel(be_ref, nx_ref, nv_ref, na_ref, x_ref, b1_ref, b2_ref, perm_ref, w1_hbm, w2_hbm, y_ref,
                   w1s_ref, w2s_ref, w1p_ref, w2b_ref, sem):
    b = pl.program_id(0)
    valid_rows = nv_ref[b]
    active = valid_rows > 0
    expert = be_ref[b]
    new_expert = jnp.logical_or(b == 0, expert != be_ref[jnp.maximum(b - 1, 0)])

    def fetch(e, wait):
        for src, dst, s in ((w1_hbm, w1s_ref, 0), (w2_hbm, w2s_ref, 1)):
            cp = pltpu.make_async_copy(src.at[e], dst, sem.at[s])
            if wait:
                cp.wait()
            else:
                cp.start(priority=1)

    @pl.when(jnp.logical_and(b == 0, active))
    def _():
        fetch(expert, False)

    @pl.when(jnp.logical_and(active, new_expert))
    def _():
        fetch(expert, True)
        for c in range(2 * D_FF // PAIR_COLS):
            cs = slice(c * PAIR_COLS, (c + 1) * PAIR_COLS)
            w1p_ref[:, cs] = _dot(w1s_ref[:, cs].astype(BF16), perm_ref[...]).astype(BF16)
        w2b_ref[...] = w2s_ref[...].astype(BF16)

        @pl.when(nx_ref[b] >= 0)
        def _():
            fetch(nx_ref[b], False)

    for live in range(MOE_PART, MOE_ROWS + 1, MOE_PART):
        @pl.when(jnp.logical_and(valid_rows > live - MOE_PART, valid_rows <= live))
        def _():
            x = _unpack_pairs(x_ref[:live, :])
            hdn = _dot(x, w1p_ref[...]) + b1_ref[0]
            acts = []
            for c in range(2 * D_FF // PAIR_COLS):
                hg = jnp.minimum(hdn[:, c * PAIR_COLS:c * PAIR_COLS + LANES], SWIGLU_LIMIT)
                hl = jnp.clip(hdn[:, c * PAIR_COLS + LANES:(c + 1) * PAIR_COLS], -SWIGLU_LIMIT, SWIGLU_LIMIT)
                acts.append((hg * _sigmoid(SWIGLU_ALPHA * hg) * (hl + 1.0)).astype(BF16))
            y = _dot(jnp.concatenate(acts, axis=1), w2b_ref[...]) + b2_ref[0]
            y_ref[:live, :] = _pack_pairs(y.astype(BF16).astype(F32))
            if live < MOE_ROWS:
                y_ref[live:, :] = jnp.zeros((MOE_ROWS - live, PACKED), U32)

    @pl.when(jnp.logical_not(active))
    def _():
        y_ref[...] = jnp.zeros_like(y_ref)


def _experts(block_exp, next_exp, valid_rows, n_active, xs, w1, b1p, w2, b2):
    n_rows = xs.shape[0]
    n_blocks = n_rows // MOE_ROWS
    xmap = lambda b, be, nx, nv, na: (jnp.maximum(jnp.minimum(b, na[0] - 1), 0), 0)
    emap = lambda b, be, nx, nv, na: (be[b], 0, 0)
    perm = np.zeros((PAIR_COLS, PAIR_COLS), np.float32)
    perm[2 * np.arange(LANES), np.arange(LANES)] = 1.0
    perm[2 * np.arange(LANES) + 1, LANES + np.arange(LANES)] = 1.0
    return pl.pallas_call(
        _expert_kernel,
        name="experts",
        grid_spec=pltpu.PrefetchScalarGridSpec(
            num_scalar_prefetch=4,
            grid=(n_blocks,),
            in_specs=[
                pl.BlockSpec((MOE_ROWS, PACKED), xmap),
                pl.BlockSpec((1, 1, 2 * D_FF), emap),
                pl.BlockSpec((1, 1, D_MODEL), emap),
                pl.BlockSpec((PAIR_COLS, PAIR_COLS), lambda b, *_: (0, 0)),
                pl.BlockSpec(memory_space=pl.ANY),
                pl.BlockSpec(memory_space=pl.ANY),
            ],
            out_specs=pl.BlockSpec((MOE_ROWS, PACKED), lambda b, *_: (b, 0)),
            scratch_shapes=[
                pltpu.VMEM((D_MODEL, 2 * D_FF), F32),
                pltpu.VMEM((D_FF, D_MODEL), F32),
                pltpu.VMEM((D_MODEL, 2 * D_FF), BF16),
                pltpu.VMEM((D_FF, D_MODEL), BF16),
                pltpu.SemaphoreType.DMA((2,)),
            ],
        ),
        out_shape=jax.ShapeDtypeStruct((n_rows, PACKED), U32),
        compiler_params=_params(("arbitrary",)),
    )(block_exp, next_exp, valid_rows, n_active, xs, b1p, b2, jnp.asarray(perm).astype(BF16), w1, w2)


def _combine_kernel(tcnt_ref, lo_ref, gs_ref, h2_ref, gate_ref, lp_ref, g_ref, ys_ref, o_ref, yloc_ref, sem):
    i = pl.program_id(0)

    slot = i % 2

    def start_runs(tile):
        s = tile % 2

        def body(e, carry):
            j = tile * N_EXPERTS + e
            _run_copies(tcnt_ref[j], ys_ref, gs_ref[j], yloc_ref.at[s], lo_ref[j], sem.at[s], RUN_BITS, False)
            return carry
        lax.fori_loop(0, N_EXPERTS, body, 0)

    @pl.when(i == 0)
    def _():
        yloc_ref[...] = jnp.zeros_like(yloc_ref)
        start_runs(i)

    @pl.when(i + 1 < pl.num_programs(0))
    def _():
        start_runs(i + 1)

    gates = gate_ref[...]
    lp = lp_ref[...]
    last = i * N_EXPERTS + N_EXPERTS - 1
    _wait_rows(lo_ref[last] + tcnt_ref[last], ys_ref, yloc_ref.at[slot], sem.at[slot])

    h = h2_ref[...]
    for c in range(LOCAL_ROWS // ROW_CHUNK):
        r = lax.broadcasted_iota(I32, (ROW_CHUNK, TM_MID), 0) + c * ROW_CHUNK
        a = jnp.zeros((ROW_CHUNK, TM_MID), F32)
        for kk in range(TOP_K):
            a = jnp.where(lp[kk:kk + 1, :] == r, gates[kk:kk + 1, :], a)
        h = h + _tn(a.astype(BF16), _unpack_pairs(yloc_ref[slot, c * ROW_CHUNK:(c + 1) * ROW_CHUNK, :]))
    o_ref[...] = _rms(h, g_ref[...])


def _combine(tcnt, lo, gstart, h2, gates, lp, g_final, ys):
    n = h2.shape[0]
    return pl.pallas_call(
        _combine_kernel,
        name="combine",
        grid_spec=pltpu.PrefetchScalarGridSpec(
            num_scalar_prefetch=3,
            grid=(n // TM_MID,),
            in_specs=[
                pl.BlockSpec((TM_MID, D_MODEL), lambda i, *_: (i, 0)),
                pl.BlockSpec((TOP_K, TM_MID), lambda i, *_: (0, i)),
                pl.BlockSpec((TOP_K, TM_MID), lambda i, *_: (0, i)),
                pl.BlockSpec((1, D_MODEL), lambda i, *_: (0, 0)),
                pl.BlockSpec(memory_space=pl.ANY),
            ],
            out_specs=pl.BlockSpec((TM_MID, D_MODEL), lambda i, *_: (i, 0)),
            scratch_shapes=[
                pltpu.VMEM((2, LOCAL_ROWS, PACKED), U32),
                pltpu.SemaphoreType.DMA((2,)),
            ],
        ),
        out_shape=jax.ShapeDtypeStruct((n, D_MODEL), F32),
        compiler_params=_params(("arbitrary",)),
    )(tcnt, lo, gstart, h2, gates, lp, g_final, ys)


def _layer(h, mem2, batch, seq, mem_len, g_mix, w_in, b_qkv, conv_w, conv_b, dt_bias, a_log, d_skip,
           g_ssm_out, attn_sinks, bias, w_out, b_out, g_cross, g_mem, w_q_cross, w_kv_cross, w_o_cross,
           g_ffn, w_router, b_router, w1, b1, w2, b2, g_final):
    n = h.shape[0]
    z, xbc, dtp, qkv = _inproj(h, g_mix[None, :], w_in, b_qkv[None, :])

    pad_h = (0, DT_PAD - SSM_HEADS)
    dt_bias_p = jnp.pad(dt_bias, pad_h)[None, :]
    a_head_p = jnp.pad(-jnp.exp(a_log), pad_h)[None, :]
    d_full = jnp.repeat(d_skip, SSM_HEAD_DIM)[None, :]
    per_seq = lambda t: t.reshape(batch, seq, t.shape[-1])
    y_ssm = _ssd(per_seq(xbc), per_seq(z), per_seq(dtp), conv_w, conv_b[None, :], dt_bias_p, a_head_p,
                 d_full, g_ssm_out[None, :]).reshape(n, SSM_WIDTH)
    y_attn = _swa(per_seq(qkv), attn_sinks, bias).reshape(n, ATTN_WIDTH)

    k_mem, v_mem = _memkv(mem2, g_mem[None, :], w_kv_cross)
    h2, u3, gates, lp, tcnt, tbase = _mid(
        y_ssm, y_attn, h, w_out, b_out[None, :], g_cross[None, :],
        w_q_cross, k_mem, v_mem, w_o_cross, g_ffn[None, :],
        w_router.T.astype(BF16), b_router[:, None], seq, mem_len)

    max_rows = n * TOP_K + (n // TM_MID) * N_EXPERTS * (SUBLANES - 1)
    n_blocks = -(-max_rows // MOE_ROWS) + N_EXPERTS
    tcnt = tcnt[:, :, 0]
    tbase = tbase[:, :, 0]
    counts = tbase[-1] + tcnt[-1]
    padded = (counts + MOE_ROWS - 1) // MOE_ROWS * MOE_ROWS
    padded_end = jnp.cumsum(padded)
    start = (padded_end - padded).astype(I32)
    n_active = (padded_end[-1] // MOE_ROWS).astype(I32)
    blk = jnp.arange(n_blocks, dtype=I32)
    blk_exp = jnp.sum((blk * MOE_ROWS)[:, None] >= padded_end[None, :], axis=1)
    eid = jnp.arange(N_EXPERTS, dtype=I32)
    last_exp = jnp.max(jnp.where(padded > 0, eid, 0))
    blk_exp = jnp.where(blk < n_active, jnp.minimum(blk_exp, N_EXPERTS - 1), last_exp).astype(I32)
    later = jnp.logical_and(padded[None, :] > 0, eid[None, :] > eid[:, None])
    next_of = jnp.min(jnp.where(later, eid[None, :], N_EXPERTS), axis=1)
    next_of = jnp.where(next_of < N_EXPERTS, next_of, -1).astype(I32)
    of_blk = blk_exp[:, None] == eid[None, :]
    blk_next = jnp.sum(jnp.where(of_blk, next_of[None, :], 0), axis=1).astype(I32)
    blk_end = jnp.sum(jnp.where(of_blk, (start + counts)[None, :], 0), axis=1)
    blk_valid = jnp.clip(blk_end - blk * MOE_ROWS, 0, MOE_ROWS)
    blk_valid = jnp.where(blk < n_active, blk_valid, 0).astype(I32)

    run_n = tcnt.reshape(-1)
    run_local = (jnp.cumsum(tcnt, axis=1) - tcnt).reshape(-1)
    run_global = (start[None, :] + tbase).reshape(-1)
    xs = _dispatch(run_n, run_local, run_global, padded - counts, start + counts, n_active[None], u3, lp,
                   n_blocks * MOE_ROWS)

    b1p = b1.reshape(N_EXPERTS, 2 * D_FF // PAIR_COLS, LANES, 2).transpose(0, 1, 3, 2)
    ys = _experts(blk_exp, blk_next, blk_valid, n_active[None], xs, w1,
                  b1p.reshape(N_EXPERTS, 1, 2 * D_FF), w2, b2[:, None, :])
    return _combine(run_n, run_local, run_global, h2, gates, lp, g_final, ys)


def kernel(x, mem, g_mix, w_in, b_qkv, conv_w, conv_b, dt_bias, a_log, d_skip, g_ssm_out, attn_sinks, rel_bias, w_out, b_out, g_cross, g_mem, w_q_cross, w_kv_cross, w_o_cross, g_ffn, w_router, b_router, w1, b1, w2, b2, g_final):
    batch, seq, d = x.shape
    mem_len = mem.shape[1]
    depth = g_mix.shape[0]
    assert depth == 1 and d == D_MODEL and seq % TM_MID == 0
    h = x.reshape(batch * seq, d)
    mem2 = mem.reshape(batch * mem_len, d)
    bias = _bias_table(rel_bias)
    out = _layer(
        h, mem2, batch, seq, mem_len, g_mix[0], w_in[0], b_qkv[0], conv_w[0], conv_b[0], dt_bias[0],
        a_log[0], d_skip[0], g_ssm_out[0], attn_sinks[0], bias, w_out[0], b_out[0], g_cross[0],
        g_mem[0], w_q_cross[0], w_kv_cross[0], w_o_cross[0], g_ffn[0], w_router[0], b_router[0],
        w1[0], b1[0], w2[0], b2[0], g_final[None, :])
    return out.reshape(batch, seq, d)
```

```python
import math

import numpy as np
import jax
import jax.numpy as jnp
from jax import lax
from jax.experimental import pallas as pl
from jax.experimental.pallas import tpu as pltpu

F32 = jnp.float32
BF16 = jnp.bfloat16
I32 = jnp.int32

D_MODEL = 1024
RMS_EPS = 1e-5

SSM_HEADS = 8
SSM_HEAD_DIM = 64
SSM_WIDTH = SSM_HEADS * SSM_HEAD_DIM
SSM_STATE = 128
SSM_GROUPS = 2
HEADS_PER_GROUP = SSM_HEADS // SSM_GROUPS
GROUP_WIDTH = SSM_WIDTH // SSM_GROUPS
CONV_WIDTH = 4
CHUNK = 128
CONV_CH = SSM_WIDTH + 2 * SSM_GROUPS * SSM_STATE

ATTN_HEADS = 8
KV_HEADS = 2
Q_PER_KV = ATTN_HEADS // KV_HEADS
HEAD_DIM = 64
ATTN_WIDTH = ATTN_HEADS * HEAD_DIM
KV_WIDTH = KV_HEADS * HEAD_DIM
WINDOW = 128
BLOCK = WINDOW
QKV_COLS = ATTN_WIDTH + 2 * KV_WIDTH

REL_BUCKETS = 32
REL_MAX_DIST = 128

CROSS_HEADS = 4
CROSS_HEAD_DIM = 128
CROSS_WIDTH = CROSS_HEADS * CROSS_HEAD_DIM

N_EXPERTS = 32
TOP_K = 4
D_FF = D_MODEL
SWIGLU_ALPHA = 1.702
SWIGLU_LIMIT = 7.0

LANES = 128
SUBLANES = 8
DT_PAD = LANES
PROJ_COLS = SSM_WIDTH + CONV_CH + DT_PAD + QKV_COLS
VMEM_LIMIT = 56 * 1024 * 1024

TM_PROJ = 512
TM_MID = 512
SSD_ROWS = 4 * CHUNK
MOE_ROWS = 512
MOE_PART = 128


def _nt(a, b):
    return lax.dot_general(a, b, (((1,), (1,)), ((), ())), preferred_element_type=F32)


def _tn(a, b):
    return lax.dot_general(a, b, (((0,), (0,)), ((), ())), preferred_element_type=F32)


def _dot(a, b):
    return jnp.dot(a, b, preferred_element_type=F32)


def _rms(x, g):
    return x * lax.rsqrt(jnp.mean(x * x, axis=-1, keepdims=True) + RMS_EPS) * g


def _sigmoid(x):
    return 0.5 * jnp.tanh(0.5 * x) + 0.5


def _params(sem=None, **kwargs):
    return pltpu.CompilerParams(dimension_semantics=sem, vmem_limit_bytes=VMEM_LIMIT, **kwargs)


def _inproj_kernel(x_ref, g_ref, w_f32, bq_ref, z_ref, xbc_ref, dt_ref, qkv_ref, w_ref):
    @pl.when(pl.program_id(0) == 0)
    def _():
        c1 = SSM_WIDTH + CONV_CH
        w_ref[:, :c1] = w_f32[:, :c1].astype(BF16)
        lane = lax.broadcasted_iota(I32, (D_MODEL, DT_PAD), 1)
        w_ref[:, c1:c1 + DT_PAD] = jnp.where(lane < SSM_HEADS, w_f32[:, c1:c1 + DT_PAD], 0.0).astype(BF16)
        w_ref[:, c1 + DT_PAD:] = w_f32[:, c1 + SSM_HEADS:].astype(BF16)

    u = _rms(x_ref[...], g_ref[...])
    p = _dot(u.astype(BF16), w_ref[...])
    z_ref[...] = p[:, :SSM_WIDTH]
    xbc_ref[...] = p[:, SSM_WIDTH:SSM_WIDTH + CONV_CH]
    dt_ref[...] = p[:, SSM_WIDTH + CONV_CH:SSM_WIDTH + CONV_CH + DT_PAD]
    qkv_ref[...] = p[:, SSM_WIDTH + CONV_CH + DT_PAD:] + bq_ref[...]


def _inproj(x2, g_mix, w_in, b_qkv):
    n = x2.shape[0]
    row = lambda i: (i, 0)
    fixed = lambda i: (0, 0)
    return pl.pallas_call(
        _inproj_kernel,
        name="inproj",
        grid=(n // TM_PROJ,),
        in_specs=[
            pl.BlockSpec((TM_PROJ, D_MODEL), row),
            pl.BlockSpec((1, D_MODEL), fixed),
            pl.BlockSpec(w_in.shape, fixed),
            pl.BlockSpec((1, QKV_COLS), fixed),
        ],
        out_specs=[
            pl.BlockSpec((TM_PROJ, SSM_WIDTH), row),
            pl.BlockSpec((TM_PROJ, CONV_CH), row),
            pl.BlockSpec((TM_PROJ, DT_PAD), row),
            pl.BlockSpec((TM_PROJ, QKV_COLS), row),
        ],
        out_shape=[
            jax.ShapeDtypeStruct((n, SSM_WIDTH), F32),
            jax.ShapeDtypeStruct((n, CONV_CH), F32),
            jax.ShapeDtypeStruct((n, DT_PAD), F32),
            jax.ShapeDtypeStruct((n, QKV_COLS), F32),
        ],
        scratch_shapes=[pltpu.VMEM((D_MODEL, PROJ_COLS), BF16)],
        compiler_params=_params(("arbitrary",)),
    )(x2, g_mix, w_in, b_qkv)


def _split3(x):
    hi = x.astype(BF16)
    rest = x - hi.astype(F32)
    mid = rest.astype(BF16)
    lo = (rest - mid.astype(F32)).astype(BF16)
    return jnp.concatenate([hi, mid, lo], axis=1)


def _ssd_kernel(xbc_ref, z_ref, dt_ref, cw_ref, cb_ref, dtb_ref, ah_ref, dsk_ref, gout_ref,
                e64_ref, ecat_ref, tri_ref, y_ref, conv_ref, state_ref):
    c = pl.program_id(0)

    @pl.when(c == 0)
    def _():
        conv_ref[:, :SUBLANES, :] = jnp.zeros((conv_ref.shape[0], SUBLANES, CONV_CH), F32)
        state_ref[...] = jnp.zeros_like(state_ref)

    for cc in range(xbc_ref.shape[1] // CHUNK):
        for b in range(xbc_ref.shape[0]):
            _ssd_chunk(b, slice(cc * CHUNK, (cc + 1) * CHUNK), xbc_ref, z_ref, dt_ref, cw_ref, cb_ref, dtb_ref,
                       ah_ref, dsk_ref, gout_ref, e64_ref, ecat_ref, tri_ref, y_ref, conv_ref, state_ref)


def _ssd_chunk(b, rows, xbc_ref, z_ref, dt_ref, cw_ref, cb_ref, dtb_ref, ah_ref, dsk_ref, gout_ref,
               e64_ref, ecat_ref, tri_ref, y_ref, conv_ref, state_ref):
    u = xbc_ref[b, rows, :]
    conv_ref[b, SUBLANES:, :] = u
    acc = u * cw_ref[CONV_WIDTH - 1:CONV_WIDTH, :] + cb_ref[...]
    for j in range(1, CONV_WIDTH):
        shifted = conv_ref[b, SUBLANES - j:SUBLANES - j + CHUNK, :]
        acc = acc + shifted * cw_ref[CONV_WIDTH - 1 - j:CONV_WIDTH - j, :]
    conv_ref[b, :SUBLANES, :] = u[CHUNK - SUBLANES:, :]
    xbc = acc * _sigmoid(acc)

    xs = xbc[:, :SSM_WIDTH]
    bm = xbc[:, SSM_WIDTH:SSM_WIDTH + SSM_GROUPS * SSM_STATE]
    cm = xbc[:, SSM_WIDTH + SSM_GROUPS * SSM_STATE:]

    dtr = dt_ref[b, rows, :] + dtb_ref[...]
    dt = jnp.maximum(dtr, 0.0) + jnp.log1p(jnp.exp(-jnp.abs(dtr)))
    a = dt * ah_ref[...]
    a3 = _split3(a)
    a_rows = jnp.concatenate([a3[:, :DT_PAD], a3[:, DT_PAD:2 * DT_PAD], a3[:, 2 * DT_PAD:]], axis=0)
    cs = _dot(tri_ref[...], a_rows)
    cs_row = cs.T
    dt_full = _dot(_split3(dt), e64_ref[...])
    cs_cat = _dot(_split3(cs), ecat_ref[...])
    cs_wide = cs_cat[:, :SSM_HEADS * CHUNK]
    cs_full = cs_cat[:, SSM_HEADS * CHUNK:]
    cs_last = cs_full[CHUNK - 1:CHUNK, :]

    xdt = xs * dt_full
    xdt_b = xdt.astype(BF16)
    xw_b = (xdt * jnp.exp(cs_last - cs_full)).astype(BF16)
    in_decay = jnp.exp(cs_full)
    chunk_decay = jnp.exp(cs_last)

    li = lax.broadcasted_iota(I32, (CHUNK, CHUNK), 0)
    si = lax.broadcasted_iota(I32, (CHUNK, CHUNK), 1)
    causal = li >= si

    ys = []
    for g in range(SSM_GROUPS):
        bg = bm[:, g * SSM_STATE:(g + 1) * SSM_STATE].astype(BF16)
        cg = cm[:, g * SSM_STATE:(g + 1) * SSM_STATE].astype(BF16)
        cb = _nt(cg, bg)
        yd = []
        for r in range(HEADS_PER_GROUP):
            h = g * HEADS_PER_GROUP + r
            diff = cs_wide[:, h * CHUNK:(h + 1) * CHUNK] - cs_row[h:h + 1, :]
            m = cb * jnp.exp(jnp.where(causal, diff, -jnp.inf))
            yd.append(_dot(m.astype(BF16), xdt_b[:, h * SSM_HEAD_DIM:(h + 1) * SSM_HEAD_DIM]))
        y_diag = jnp.concatenate(yd, axis=1)
        gs = slice(g * GROUP_WIDTH, (g + 1) * GROUP_WIDTH)
        st = state_ref[b, g]
        y_off = _dot(cg, st.astype(BF16)) * in_decay[:, gs]
        state_ref[b, g] = st * chunk_decay[:, gs] + _tn(bg, xw_b[:, gs])
        ys.append(y_diag + y_off)
    y = jnp.concatenate(ys, axis=1) + dsk_ref[...] * xs

    zz = z_ref[b, rows, :]
    y = y * (zz * _sigmoid(zz))
    outs = []
    for g in range(SSM_GROUPS):
        yg = y[:, g * GROUP_WIDTH:(g + 1) * GROUP_WIDTH]
        outs.append(yg * lax.rsqrt(jnp.mean(yg * yg, axis=-1, keepdims=True) + RMS_EPS))
    y_ref[b, rows, :] = jnp.concatenate(outs, axis=1) * gout_ref[...]


def _ssd(xbc, z, dtp, conv_w, conv_b, dt_bias_p, a_head_p, d_full, g_out):
    batch, seq, _ = xbc.shape
    chunk = lambda c: (0, c, 0)
    fixed = lambda c: (0, 0)
    head = np.arange(SSM_WIDTH) // SSM_HEAD_DIM
    e64 = (np.arange(DT_PAD)[:, None] == head[None, :]).astype(np.float32)
    head_w = np.arange(SSM_HEADS * CHUNK) // CHUNK
    e128 = (np.arange(DT_PAD)[:, None] == head_w[None, :]).astype(np.float32)
    e64_3 = jnp.asarray(np.tile(e64, (3, 1))).astype(BF16)
    ecat_3 = jnp.asarray(np.tile(np.concatenate([e128, e64], axis=1), (3, 1))).astype(BF16)
    tri_3 = jnp.asarray(np.tile(np.tril(np.ones((CHUNK, CHUNK), np.float32)), (1, 3))).astype(BF16)
    return pl.pallas_call(
        _ssd_kernel,
        name="ssd",
        grid=(seq // SSD_ROWS,),
        in_specs=[
            pl.BlockSpec((batch, SSD_ROWS, CONV_CH), chunk),
            pl.BlockSpec((batch, SSD_ROWS, SSM_WIDTH), chunk),
            pl.BlockSpec((batch, SSD_ROWS, DT_PAD), chunk),
            pl.BlockSpec((CONV_WIDTH, CONV_CH), fixed),
            pl.BlockSpec((1, CONV_CH), fixed),
            pl.BlockSpec((1, DT_PAD), fixed),
            pl.BlockSpec((1, DT_PAD), fixed),
            pl.BlockSpec((1, SSM_WIDTH), fixed),
            pl.BlockSpec((1, SSM_WIDTH), fixed),
            pl.BlockSpec((3 * DT_PAD, SSM_WIDTH), fixed),
            pl.BlockSpec((3 * DT_PAD, SSM_HEADS * CHUNK + SSM_WIDTH), fixed),
            pl.BlockSpec((CHUNK, 3 * CHUNK), fixed),
        ],
        out_specs=pl.BlockSpec((batch, SSD_ROWS, SSM_WIDTH), chunk),
        out_shape=jax.ShapeDtypeStruct((batch, seq, SSM_WIDTH), F32),
        scratch_shapes=[
            pltpu.VMEM((batch, SUBLANES + CHUNK, CONV_CH), F32),
            pltpu.VMEM((batch, SSM_GROUPS, SSM_STATE, GROUP_WIDTH), F32),
        ],
        compiler_params=_params(("arbitrary",)),
    )(xbc, z, dtp, conv_w, conv_b, dt_bias_p, a_head_p, d_full, g_out, e64_3, ecat_3, tri_3)


def _t5_bucket_table():
    q_pos = np.arange(BLOCK)[:, None]
    k_pos = np.arange(2 * BLOCK)[None, :] - BLOCK
    dist = q_pos - k_pos
    d = np.maximum(dist, 0)
    max_exact = REL_BUCKETS // 2
    ratio = np.maximum(d, max_exact).astype(np.float32) / np.float32(max_exact)
    large = max_exact + (np.log(ratio) / np.float32(math.log(REL_MAX_DIST / max_exact))
                         * np.float32(REL_BUCKETS - max_exact)).astype(np.int32)
    large = np.minimum(large, REL_BUCKETS - 1)
    bucket = np.where(d < max_exact, d, large)
    in_window = (dist >= 0) & (dist < WINDOW)
    return np.where(in_window, bucket, -1).astype(np.int32)


def _bias_kernel(rb_ref, bucket_ref, out_ref):
    bucket = bucket_ref[...]
    for h in range(ATTN_HEADS):
        acc = jnp.zeros(bucket.shape, F32)
        for b in range(REL_BUCKETS):
            acc = jnp.where(bucket == b, rb_ref[b, h], acc)
        out_ref[h] = jnp.where(bucket >= 0, acc, -jnp.inf)


def _bias_table(rel_bias):
    bucket = jnp.asarray(_t5_bucket_table())
    return pl.pallas_call(
        _bias_kernel,
        name="bias_table",
        in_specs=[
            pl.BlockSpec(memory_space=pltpu.SMEM),
            pl.BlockSpec(memory_space=pltpu.VMEM),
        ],
        out_specs=pl.BlockSpec(memory_space=pltpu.VMEM),
        out_shape=jax.ShapeDtypeStruct((ATTN_HEADS, BLOCK, 2 * BLOCK), F32),
    )(rel_bias, bucket)


def _swa_kernel(sink_ref, q_ref, k_ref, v_ref, kp_ref, vp_ref, bias_ref, o_ref):
    i = pl.program_id(0)
    col = lax.broadcasted_iota(I32, (BLOCK, 2 * BLOCK), 1)
    key_ok = jnp.logical_or(col >= BLOCK, i > 0)
    for b in range(q_ref.shape[0]):
        q = (q_ref[b] * (HEAD_DIM ** -0.5)).astype(BF16)
        outs = []
        for g in range(KV_HEADS):
            gs = slice(g * HEAD_DIM, (g + 1) * HEAD_DIM)
            k2 = jnp.concatenate([kp_ref[b, :, gs], k_ref[b, :, gs]], axis=0).astype(BF16)
            v2 = jnp.concatenate([vp_ref[b, :, gs], v_ref[b, :, gs]], axis=0).astype(BF16)
            for r in range(Q_PER_KV):
                h = g * Q_PER_KV + r
                s = _nt(q[:, h * HEAD_DIM:(h + 1) * HEAD_DIM], k2) + bias_ref[h]
                s = jnp.where(key_ok, s, -jnp.inf)
                sink = sink_ref[h]
                m = jnp.maximum(jnp.max(s, axis=-1, keepdims=True), sink)
                p = jnp.exp(s - m)
                denom = jnp.sum(p, axis=-1, keepdims=True) + jnp.exp(sink - m)
                outs.append(_dot(p.astype(BF16), v2) / denom)
        o_ref[b] = jnp.concatenate(outs, axis=1)


def _swa(qkv, sinks, bias):
    batch, seq, _ = qkv.shape
    kcol = ATTN_WIDTH // KV_WIDTH
    prev = lambda i: jnp.maximum(i - 1, 0)
    return pl.pallas_call(
        _swa_kernel,
        name="swa",
        grid_spec=pltpu.PrefetchScalarGridSpec(
            num_scalar_prefetch=1,
            grid=(seq // BLOCK,),
            in_specs=[
                pl.BlockSpec((batch, BLOCK, ATTN_WIDTH), lambda i, s: (0, i, 0)),
                pl.BlockSpec((batch, BLOCK, KV_WIDTH), lambda i, s: (0, i, kcol)),
                pl.BlockSpec((batch, BLOCK, KV_WIDTH), lambda i, s: (0, i, kcol + 1)),
                pl.BlockSpec((batch, BLOCK, KV_WIDTH), lambda i, s: (0, prev(i), kcol)),
                pl.BlockSpec((batch, BLOCK, KV_WIDTH), lambda i, s: (0, prev(i), kcol + 1)),
                pl.BlockSpec((ATTN_HEADS, BLOCK, 2 * BLOCK), lambda i, s: (0, 0, 0)),
            ],
            out_specs=pl.BlockSpec((batch, BLOCK, ATTN_WIDTH), lambda i, s: (0, i, 0)),
        ),
        out_shape=jax.ShapeDtypeStruct((batch, seq, ATTN_WIDTH), F32),
        compiler_params=_params(("arbitrary",)),
    )(sinks, qkv, qkv, qkv, qkv, qkv, bias)


def _memkv_kernel(m_ref, g_ref, w_ref, k_ref, v_ref):
    u = _rms(m_ref[...], g_ref[...])
    kv = _dot(u.astype(BF16), w_ref[...].astype(BF16))
    k_ref[...] = kv[:, :CROSS_WIDTH].astype(BF16)
    v_ref[...] = kv[:, CROSS_WIDTH:].astype(BF16)


def _memkv(mem2, g_mem, w_kv):
    n = mem2.shape[0]
    return pl.pallas_call(
        _memkv_kernel,
        name="memkv",
        out_shape=[jax.ShapeDtypeStruct((n, CROSS_WIDTH), BF16)] * 2,
        compiler_params=_params(),
    )(mem2, g_mem, w_kv)


def _mid_kernel(ys_ref, ya_ref, x_ref, wout_f32, bout_ref, gc_ref, wq_f32, k_ref, v_ref, wo_f32,
                gf_ref, wr_ref, br_ref, upper_ref, lower_ref,
                h2_ref, u3_ref, gate_ref, lp_ref, tcnt_ref, tbase_ref, run_ref, wout_ref, wq_ref, wo_ref):
    i = pl.program_id(0)

    @pl.when(i == 0)
    def _():
        run_ref[...] = jnp.zeros_like(run_ref)
        wout_ref[...] = wout_f32[...].astype(BF16)
        wq_ref[...] = wq_f32[...].astype(BF16)
        wo_ref[...] = wo_f32[...].astype(BF16)

    ycat = jnp.concatenate([ys_ref[...], ya_ref[...]], axis=1).astype(BF16)
    h1 = x_ref[...] + _dot(ycat, wout_ref[...]) + bout_ref[...]

    u2 = _rms(h1, gc_ref[...])
    q = _dot(u2.astype(BF16), wq_ref[...]).astype(BF16)
    outs = []
    for h in range(CROSS_HEADS):
        hs = slice(h * CROSS_HEAD_DIM, (h + 1) * CROSS_HEAD_DIM)
        s = _nt(q[:, hs], k_ref[:, hs]) * (CROSS_HEAD_DIM ** -0.5)
        m = jnp.max(s, axis=-1, keepdims=True)
        p = jnp.exp(s - m)
        outs.append(_dot(p.astype(BF16), v_ref[:, hs]) / jnp.sum(p, axis=-1, keepdims=True))
    o = jnp.concatenate(outs, axis=1).astype(BF16)
    h2 = h1 + _dot(o, wo_ref[...])
    h2_ref[...] = h2
    u3 = _rms(h2, gf_ref[...]).astype(BF16)
    u3_ref[...] = u3

    logits = _nt(wr_ref[...], u3) + br_ref[...]
    eid = lax.broadcasted_iota(I32, logits.shape, 0)
    vals, idxs, hots = [], [], []
    for _ in range(TOP_K):
        m = jnp.max(logits, axis=0, keepdims=True)
        ix = jnp.min(jnp.where(logits == m, eid, N_EXPERTS), axis=0, keepdims=True)
        hot = eid == ix
        logits = jnp.where(hot, -jnp.inf, logits)
        vals.append(m)
        idxs.append(ix)
        hots.append(hot)
    ex = [jnp.exp(v - vals[0]) for v in vals]
    tot = ex[0] + ex[1] + ex[2] + ex[3]
    gate_ref[...] = jnp.concatenate([e / tot for e in ex], axis=0)

    hot_f = [jnp.where(hot, 1.0, 0.0) for hot in hots]
    cnt_k = [jnp.sum(hf, axis=1, keepdims=True) for hf in hot_f]
    tile_cnt = cnt_k[0] + cnt_k[1] + cnt_k[2] + cnt_k[3]
    tile_cnt = jnp.floor((tile_cnt + (SUBLANES - 1)) * (1.0 / SUBLANES)) * SUBLANES
    units = jnp.broadcast_to(tile_cnt * (1.0 / SUBLANES), (N_EXPERTS, LANES)).astype(BF16)
    offs = _dot(lower_ref[...], units)[:, :1] * SUBLANES
    local = []
    before_all = _dot(jnp.concatenate(hot_f, axis=0).astype(BF16), upper_ref[...])
    for kk in range(TOP_K):
        before = before_all[kk * N_EXPERTS:(kk + 1) * N_EXPERTS, :]
        local.append(jnp.sum(jnp.where(hots[kk], before + offs, 0.0), axis=0, keepdims=True))
        offs = offs + cnt_k[kk]
    lp_ref[...] = jnp.concatenate(local, axis=0).astype(I32)
    tcnt_ref[0] = jnp.broadcast_to(tile_cnt, (N_EXPERTS, LANES)).astype(I32)
    tbase_ref[0] = jnp.broadcast_to(run_ref[...], (N_EXPERTS, LANES)).astype(I32)
    run_ref[...] = run_ref[...] + tile_cnt


def _mid(y_ssm, y_attn, x2, w_out, b_out, g_cross, w_q, k_mem, v_mem, w_o, g_ffn, w_r_t, b_r, seq, mem_len):
    n = x2.shape[0]
    tiles_per_batch = seq // TM_MID
    row = lambda i: (i, 0)
    col = lambda i: (0, i)
    fixed = lambda i: (0, 0)
    memb = lambda i: (i // tiles_per_batch, 0)
    tile3 = lambda i: (i, 0, 0)
    n_tiles = n // TM_MID
    upper = jnp.asarray(np.triu(np.ones((TM_MID, TM_MID), np.float32), 1)).astype(BF16)
    lower = jnp.asarray(np.tril(np.ones((N_EXPERTS, N_EXPERTS), np.float32), -1)).astype(BF16)
    return pl.pallas_call(
        _mid_kernel,
        name="mid",
        grid=(n_tiles,),
        in_specs=[
            pl.BlockSpec((TM_MID, SSM_WIDTH), row),
            pl.BlockSpec((TM_MID, ATTN_WIDTH), row),
            pl.BlockSpec((TM_MID, D_MODEL), row),
            pl.BlockSpec((D_MODEL, D_MODEL), fixed),
            pl.BlockSpec((1, D_MODEL), fixed),
            pl.BlockSpec((1, D_MODEL), fixed),
            pl.BlockSpec((D_MODEL, CROSS_WIDTH), fixed),
            pl.BlockSpec((mem_len, CROSS_WIDTH), memb),
            pl.BlockSpec((mem_len, CROSS_WIDTH), memb),
            pl.BlockSpec((CROSS_WIDTH, D_MODEL), fixed),
            pl.BlockSpec((1, D_MODEL), fixed),
            pl.BlockSpec((N_EXPERTS, D_MODEL), fixed),
            pl.BlockSpec((N_EXPERTS, 1), fixed),
            pl.BlockSpec((TM_MID, TM_MID), fixed),
            pl.BlockSpec((N_EXPERTS, N_EXPERTS), fixed),
        ],
        out_specs=[
            pl.BlockSpec((TM_MID, D_MODEL), row),
            pl.BlockSpec((TM_MID, D_MODEL), row),
            pl.BlockSpec((TOP_K, TM_MID), col),
            pl.BlockSpec((TOP_K, TM_MID), col),
            pl.BlockSpec((1, N_EXPERTS, LANES), tile3),
            pl.BlockSpec((1, N_EXPERTS, LANES), tile3),
        ],
        out_shape=[
            jax.ShapeDtypeStruct((n, D_MODEL), F32),
            jax.ShapeDtypeStruct((n, D_MODEL), BF16),
            jax.ShapeDtypeStruct((TOP_K, n), F32),
            jax.ShapeDtypeStruct((TOP_K, n), I32),
            jax.ShapeDtypeStruct((n_tiles, N_EXPERTS, LANES), I32),
            jax.ShapeDtypeStruct((n_tiles, N_EXPERTS, LANES), I32),
        ],
        scratch_shapes=[
            pltpu.VMEM((N_EXPERTS, 1), F32),
            pltpu.VMEM((D_MODEL, D_MODEL), BF16),
            pltpu.VMEM((D_MODEL, CROSS_WIDTH), BF16),
            pltpu.VMEM((CROSS_WIDTH, D_MODEL), BF16),
        ],
        compiler_params=_params(("arbitrary",)),
    )(y_ssm, y_attn, x2, w_out, b_out, g_cross, w_q, k_mem, v_mem, w_o, g_ffn, w_r_t, b_r, upper, lower)


ROW_CHUNK = 256
LOCAL_ROWS = -(-(TM_MID * TOP_K + N_EXPERTS * (SUBLANES - 1)) // ROW_CHUNK) * ROW_CHUNK
RUN_BITS = tuple(1 << s for s in range(TM_MID.bit_length() - 1, SUBLANES.bit_length() - 2, -1))
PAD_BITS = tuple(1 << s for s in range((MOE_ROWS - 1).bit_length() - 1, SUBLANES.bit_length() - 2, -1))


U32 = jnp.uint32
PACKED = D_MODEL // 2


def _pack_pairs(v):
    bits = lax.bitcast_convert_type(v, U32)
    return bits[:, PACKED:] | (bits[:, :PACKED] >> 16)


def _unpack_pairs(w):
    lo = lax.bitcast_convert_type(w << 16, F32)
    hi = lax.bitcast_convert_type(w & jnp.uint32(0xFFFF0000), F32)
    return jnp.concatenate([lo, hi], axis=1).astype(BF16)


def _run_copies(n, src_ref, src, dst_ref, dst, sem, bits, wait, advance_src=True):
    for bit in bits:
        step = n & bit

        @pl.when(step != 0)
        def _():
            cp = pltpu.make_async_copy(src_ref.at[pl.ds(pl.multiple_of(src, SUBLANES), bit)],
                                       dst_ref.at[pl.ds(pl.multiple_of(dst, SUBLANES), bit)], sem)
            if wait:
                cp.wait()
            else:
                cp.start()

        if advance_src:
            src = src + step
        dst = dst + step


def _wait_rows(n, src_ref, dst_ref, sem):
    rows = pl.ds(0, pl.multiple_of(n, SUBLANES))
    pltpu.make_async_copy(src_ref.at[rows], dst_ref.at[rows], sem).wait()


def _dispatch_kernel(tcnt_ref, lo_ref, gs_ref, padn_ref, pads_ref, na_ref, u_ref, lp_ref, xs_ref,
                     xloc_ref, zero_ref, sem, zsem):
    i = pl.program_id(0)

    def zero_fill(wait):
        def body(e, carry):
            _run_copies(padn_ref[e], zero_ref, 0, xs_ref, pads_ref[e], zsem, PAD_BITS, wait, advance_src=False)
            return carry
        lax.fori_loop(0, N_EXPERTS, body, 0)

        def tail(b, carry):
            cp = pltpu.make_async_copy(zero_ref, xs_ref.at[pl.ds(pl.multiple_of(b * MOE_ROWS, MOE_ROWS), MOE_ROWS)], zsem)
            if wait:
                cp.wait()
            else:
                cp.start()
            return carry
        lax.fori_loop(na_ref[0], xs_ref.shape[0] // MOE_ROWS, tail, 0)

    @pl.when(i == 0)
    def _():
        zero_ref[...] = jnp.zeros_like(zero_ref)
        zero_fill(False)

    slot = i % 2
    u = u_ref[...]
    lp = lp_ref[...]
    for c in range(LOCAL_ROWS // ROW_CHUNK):
        r = lax.broadcasted_iota(I32, (ROW_CHUNK, TM_MID), 0) + c * ROW_CHUNK
        p = jnp.zeros((ROW_CHUNK, TM_MID), F32)
        for kk in range(TOP_K):
            p = jnp.where(lp[kk:kk + 1, :] == r, 1.0, p)
        xloc_ref[slot, c * ROW_CHUNK:(c + 1) * ROW_CHUNK, :] = _pack_pairs(_dot(p.astype(BF16), u))

    def start_runs(tile):
        s = tile % 2

        def body(e, carry):
            j = tile * N_EXPERTS + e
            _run_copies(tcnt_ref[j], xloc_ref.at[s], lo_ref[j], xs_ref, gs_ref[j], sem.at[s], RUN_BITS, False)
            return carry
        lax.fori_loop(0, N_EXPERTS, body, 0)

    def wait_runs(tile):
        s = tile % 2
        last = tile * N_EXPERTS + N_EXPERTS - 1
        _wait_rows(lo_ref[last] + tcnt_ref[last], xloc_ref.at[s], xs_ref, sem.at[s])

    start_runs(i)

    @pl.when(i > 0)
    def _():
        wait_runs(i - 1)

    @pl.when(i == pl.num_programs(0) - 1)
    def _():
        wait_runs(i)
        zero_fill(True)


def _dispatch(tcnt, lo, gstart, padn, pads, n_active, u3, lp, n_rows):
    n = u3.shape[0]
    return pl.pallas_call(
        _dispatch_kernel,
        name="dispatch",
        grid_spec=pltpu.PrefetchScalarGridSpec(
            num_scalar_prefetch=6,
            grid=(n // TM_MID,),
            in_specs=[
                pl.BlockSpec((TM_MID, D_MODEL), lambda i, *_: (i, 0)),
                pl.BlockSpec((TOP_K, TM_MID), lambda i, *_: (0, i)),
            ],
            out_specs=pl.BlockSpec(memory_space=pl.ANY),
            scratch_shapes=[
                pltpu.VMEM((2, LOCAL_ROWS, PACKED), U32),
                pltpu.VMEM((MOE_ROWS, PACKED), U32),
                pltpu.SemaphoreType.DMA((2,)),
                pltpu.SemaphoreType.DMA,
            ],
        ),
        out_shape=jax.ShapeDtypeStruct((n_rows, PACKED), U32),
        compiler_params=_params(("arbitrary",)),
    )(tcnt, lo, gstart, padn, pads, n_active, u3, lp)


PAIR_COLS = 2 * LANES


def _expert_kernel(be_ref, nx_ref, nv_ref, na_ref, x_ref, b1_ref, b2_ref, perm_ref, w1_hbm, w2_hbm, y_ref,
                   w1s_ref, w2s_ref, w1p_ref, w2b_ref, sem):
    b = pl.program_id(0)
    valid_rows = nv_ref[b]
    active = valid_rows > 0
    expert = be_ref[b]
    new_expert = jnp.logical_or(b == 0, expert != be_ref[jnp.maximum(b - 1, 0)])

    def fetch(e, wait):
        for src, dst, s in ((w1_hbm, w1s_ref, 0), (w2_hbm, w2s_ref, 1)):
            cp = pltpu.make_async_copy(src.at[e], dst, sem.at[s])
            if wait:
                cp.wait()
            else:
                cp.start(priority=1)

    @pl.when(jnp.logical_and(b == 0, active))
    def _():
        fetch(expert, False)

    @pl.when(jnp.logical_and(active, new_expert))
    def _():
        fetch(expert, True)
        for c in range(2 * D_FF // PAIR_COLS):
            cs = slice(c * PAIR_COLS, (c + 1) * PAIR_COLS)
            w1p_ref[:, cs] = _dot(w1s_ref[:, cs].astype(BF16), perm_ref[...]).astype(BF16)
        w2b_ref[...] = w2s_ref[...].astype(BF16)

        @pl.when(nx_ref[b] >= 0)
        def _():
            fetch(nx_ref[b], False)

    for live in range(MOE_PART, MOE_ROWS + 1, MOE_PART):
        @pl.when(jnp.logical_and(valid_rows > live - MOE_PART, valid_rows <= live))
        def _():
            x = _unpack_pairs(x_ref[:live, :])
            hdn = _dot(x, w1p_ref[...]) + b1_ref[0]
            acts = []
            for c in range(2 * D_FF // PAIR_COLS):
                hg = jnp.minimum(hdn[:, c * PAIR_COLS:c * PAIR_COLS + LANES], SWIGLU_LIMIT)
                hl = jnp.clip(hdn[:, c * PAIR_COLS + LANES:(c + 1) * PAIR_COLS], -SWIGLU_LIMIT, SWIGLU_LIMIT)
                acts.append((hg * _sigmoid(SWIGLU_ALPHA * hg) * (hl + 1.0)).astype(BF16))
            y = _dot(jnp.concatenate(acts, axis=1), w2b_ref[...]) + b2_ref[0]
            y_ref[:live, :] = _pack_pairs(y.astype(BF16).astype(F32))
            if live < MOE_ROWS:
                y_ref[live:, :] = jnp.zeros((MOE_ROWS - live, PACKED), U32)

    @pl.when(jnp.logical_not(active))
    def _():
        y_ref[...] = jnp.zeros_like(y_ref)


def _experts(block_exp, next_exp, valid_rows, n_active, xs, w1, b1p, w2, b2):
    n_rows = xs.shape[0]
    n_blocks = n_rows // MOE_ROWS
    xmap = lambda b, be, nx, nv, na: (jnp.maximum(jnp.minimum(b, na[0] - 1), 0), 0)
    emap = lambda b, be, nx, nv, na: (be[b], 0, 0)
    perm = np.zeros((PAIR_COLS, PAIR_COLS), np.float32)
    perm[2 * np.arange(LANES), np.arange(LANES)] = 1.0
    perm[2 * np.arange(LANES) + 1, LANES + np.arange(LANES)] = 1.0
    return pl.pallas_call(
        _expert_kernel,
        name="experts",
        grid_spec=pltpu.PrefetchScalarGridSpec(
            num_scalar_prefetch=4,
            grid=(n_blocks,),
            in_specs=[
                pl.BlockSpec((MOE_ROWS, PACKED), xmap),
                pl.BlockSpec((1, 1, 2 * D_FF), emap),
                pl.BlockSpec((1, 1, D_MODEL), emap),
                pl.BlockSpec((PAIR_COLS, PAIR_COLS), lambda b, *_: (0, 0)),
                pl.BlockSpec(memory_space=pl.ANY),
                pl.BlockSpec(memory_space=pl.ANY),
            ],
            out_specs=pl.BlockSpec((MOE_ROWS, PACKED), lambda b, *_: (b, 0)),
            scratch_shapes=[
                pltpu.VMEM((D_MODEL, 2 * D_FF), F32),
                pltpu.VMEM((D_FF, D_MODEL), F32),
                pltpu.VMEM((D_MODEL, 2 * D_FF), BF16),
                pltpu.VMEM((D_FF, D_MODEL), BF16),
                pltpu.SemaphoreType.DMA((2,)),
            ],
        ),
        out_shape=jax.ShapeDtypeStruct((n_rows, PACKED), U32),
        compiler_params=_params(("arbitrary",)),
    )(block_exp, next_exp, valid_rows, n_active, xs, b1p, b2, jnp.asarray(perm).astype(BF16), w1, w2)


def _combine_kernel(tcnt_ref, lo_ref, gs_ref, h2_ref, gate_ref, lp_ref, g_ref, ys_ref, o_ref, yloc_ref, sem):
    i = pl.program_id(0)

    slot = i % 2

    def start_runs(tile):
        s = tile % 2

        def body(e, carry):
            j = tile * N_EXPERTS + e
            _run_copies(tcnt_ref[j], ys_ref, gs_ref[j], yloc_ref.at[s], lo_ref[j], sem.at[s], RUN_BITS, False)
            return carry
        lax.fori_loop(0, N_EXPERTS, body, 0)

    @pl.when(i == 0)
    def _():
        yloc_ref[...] = jnp.zeros_like(yloc_ref)
        start_runs(i)

    @pl.when(i + 1 < pl.num_programs(0))
    def _():
        start_runs(i + 1)

    gates = gate_ref[...]
    lp = lp_ref[...]
    last = i * N_EXPERTS + N_EXPERTS - 1
    _wait_rows(lo_ref[last] + tcnt_ref[last], ys_ref, yloc_ref.at[slot], sem.at[slot])

    h = h2_ref[...]
    for c in range(LOCAL_ROWS // ROW_CHUNK):
        r = lax.broadcasted_iota(I32, (ROW_CHUNK, TM_MID), 0) + c * ROW_CHUNK
        a = jnp.zeros((ROW_CHUNK, TM_MID), F32)
        for kk in range(TOP_K):
            a = jnp.where(lp[kk:kk + 1, :] == r, gates[kk:kk + 1, :], a)
        h = h + _tn(a.astype(BF16), _unpack_pairs(yloc_ref[slot, c * ROW_CHUNK:(c + 1) * ROW_CHUNK, :]))
    o_ref[...] = _rms(h, g_ref[...])


def _combine(tcnt, lo, gstart, h2, gates, lp, g_final, ys):
    n = h2.shape[0]
    return pl.pallas_call(
        _combine_kernel,
        name="combine",
        grid_spec=pltpu.PrefetchScalarGridSpec(
            num_scalar_prefetch=3,
            grid=(n // TM_MID,),
            in_specs=[
                pl.BlockSpec((TM_MID, D_MODEL), lambda i, *_: (i, 0)),
                pl.BlockSpec((TOP_K, TM_MID), lambda i, *_: (0, i)),
                pl.BlockSpec((TOP_K, TM_MID), lambda i, *_: (0, i)),
                pl.BlockSpec((1, D_MODEL), lambda i, *_: (0, 0)),
                pl.BlockSpec(memory_space=pl.ANY),
            ],
            out_specs=pl.BlockSpec((TM_MID, D_MODEL), lambda i, *_: (i, 0)),
            scratch_shapes=[
                pltpu.VMEM((2, LOCAL_ROWS, PACKED), U32),
                pltpu.SemaphoreType.DMA((2,)),
            ],
        ),
        out_shape=jax.ShapeDtypeStruct((n, D_MODEL), F32),
        compiler_params=_params(("arbitrary",)),
    )(tcnt, lo, gstart, h2, gates, lp, g_final, ys)


def _layer(h, mem2, batch, seq, mem_len, g_mix, w_in, b_qkv, conv_w, conv_b, dt_bias, a_log, d_skip,
           g_ssm_out, attn_sinks, bias, w_out, b_out, g_cross, g_mem, w_q_cross, w_kv_cross, w_o_cross,
           g_ffn, w_router, b_router, w1, b1, w2, b2, g_final):
    n = h.shape[0]
    z, xbc, dtp, qkv = _inproj(h, g_mix[None, :], w_in, b_qkv[None, :])

    pad_h = (0, DT_PAD - SSM_HEADS)
    dt_bias_p = jnp.pad(dt_bias, pad_h)[None, :]
    a_head_p = jnp.pad(-jnp.exp(a_log), pad_h)[None, :]
    d_full = jnp.repeat(d_skip, SSM_HEAD_DIM)[None, :]
    per_seq = lambda t: t.reshape(batch, seq, t.shape[-1])
    y_ssm = _ssd(per_seq(xbc), per_seq(z), per_seq(dtp), conv_w, conv_b[None, :], dt_bias_p, a_head_p,
                 d_full, g_ssm_out[None, :]).reshape(n, SSM_WIDTH)
    y_attn = _swa(per_seq(qkv), attn_sinks, bias).reshape(n, ATTN_WIDTH)

    k_mem, v_mem = _memkv(mem2, g_mem[None, :], w_kv_cross)
    h2, u3, gates, lp, tcnt, tbase = _mid(
        y_ssm, y_attn, h, w_out, b_out[None, :], g_cross[None, :],
        w_q_cross, k_mem, v_mem, w_o_cross, g_ffn[None, :],
        w_router.T.astype(BF16), b_router[:, None], seq, mem_len)

    max_rows = n * TOP_K + (n // TM_MID) * N_EXPERTS * (SUBLANES - 1)
    n_blocks = -(-max_rows // MOE_ROWS) + N_EXPERTS
    tcnt = tcnt[:, :, 0]
    tbase = tbase[:, :, 0]
    counts = tbase[-1] + tcnt[-1]
    padded = (counts + MOE_ROWS - 1) // MOE_ROWS * MOE_ROWS
    padded_end = jnp.cumsum(padded)
    start = (padded_end - padded).astype(I32)
    n_active = (padded_end[-1] // MOE_ROWS).astype(I32)
    blk = jnp.arange(n_blocks, dtype=I32)
    blk_exp = jnp.sum((blk * MOE_ROWS)[:, None] >= padded_end[None, :], axis=1)
    eid = jnp.arange(N_EXPERTS, dtype=I32)
    last_exp = jnp.max(jnp.where(padded > 0, eid, 0))
    blk_exp = jnp.where(blk < n_active, jnp.minimum(blk_exp, N_EXPERTS - 1), last_exp).astype(I32)
    later = jnp.logical_and(padded[None, :] > 0, eid[None, :] > eid[:, None])
    next_of = jnp.min(jnp.where(later, eid[None, :], N_EXPERTS), axis=1)
    next_of = jnp.where(next_of < N_EXPERTS, next_of, -1).astype(I32)
    of_blk = blk_exp[:, None] == eid[None, :]
    blk_next = jnp.sum(jnp.where(of_blk, next_of[None, :], 0), axis=1).astype(I32)
    blk_end = jnp.sum(jnp.where(of_blk, (start + counts)[None, :], 0), axis=1)
    blk_valid = jnp.clip(blk_end - blk * MOE_ROWS, 0, MOE_ROWS)
    blk_valid = jnp.where(blk < n_active, blk_valid, 0).astype(I32)

    run_n = tcnt.reshape(-1)
    run_local = (jnp.cumsum(tcnt, axis=1) - tcnt).reshape(-1)
    run_global = (start[None, :] + tbase).reshape(-1)
    xs = _dispatch(run_n, run_local, run_global, padded - counts, start + counts, n_active[None], u3, lp,
                   n_blocks * MOE_ROWS)

    b1p = b1.reshape(N_EXPERTS, 2 * D_FF // PAIR_COLS, LANES, 2).transpose(0, 1, 3, 2)
    ys = _experts(blk_exp, blk_next, blk_valid, n_active[None], xs, w1,
                  b1p.reshape(N_EXPERTS, 1, 2 * D_FF), w2, b2[:, None, :])
    return _combine(run_n, run_local, run_global, h2, gates, lp, g_final, ys)


def kernel(x, mem, g_mix, w_in, b_qkv, conv_w, conv_b, dt_bias, a_log, d_skip, g_ssm_out, attn_sinks, rel_bias, w_out, b_out, g_cross, g_mem, w_q_cross, w_kv_cross, w_o_cross, g_ffn, w_router, b_router, w1, b1, w2, b2, g_final):
    batch, seq, d = x.shape
    mem_len = mem.shape[1]
    depth = g_mix.shape[0]
    assert depth == 1 and d == D_MODEL and seq % TM_MID == 0
    h = x.reshape(batch * seq, d)
    mem2 = mem.reshape(batch * mem_len, d)
    bias = _bias_table(rel_bias)
    out = _layer(
        h, mem2, batch, seq, mem_len, g_mix[0], w_in[0], b_qkv[0], conv_w[0], conv_b[0], dt_bias[0],
        a_log[0], d_skip[0], g_ssm_out[0], attn_sinks[0], bias, w_out[0], b_out[0], g_cross[0],
        g_mem[0], w_q_cross[0], w_kv_cross[0], w_o_cross[0], g_ffn[0], w_router[0], b_router[0],
        w1[0], b1[0], w2[0], b2[0], g_final[None, :])
    return out.reshape(batch, seq, d)
```

```python
import math

import numpy as np
import jax
import jax.numpy as jnp
from jax import lax
from jax.experimental import pallas as pl
from jax.experimental.pallas import tpu as pltpu

F32 = jnp.float32
BF16 = jnp.bfloat16
I32 = jnp.int32

D_MODEL = 1024
RMS_EPS = 1e-5

SSM_HEADS = 8
SSM_HEAD_DIM = 64
SSM_WIDTH = SSM_HEADS * SSM_HEAD_DIM
SSM_STATE = 128
SSM_GROUPS = 2
HEADS_PER_GROUP = SSM_HEADS // SSM_GROUPS
GROUP_WIDTH = SSM_WIDTH // SSM_GROUPS
CONV_WIDTH = 4
CHUNK = 128
CONV_CH = SSM_WIDTH + 2 * SSM_GROUPS * SSM_STATE

ATTN_HEADS = 8
KV_HEADS = 2
Q_PER_KV = ATTN_HEADS // KV_HEADS
HEAD_DIM = 64
ATTN_WIDTH = ATTN_HEADS * HEAD_DIM
KV_WIDTH = KV_HEADS * HEAD_DIM
WINDOW = 128
BLOCK = WINDOW
QKV_COLS = ATTN_WIDTH + 2 * KV_WIDTH

REL_BUCKETS = 32
REL_MAX_DIST = 128

CROSS_HEADS = 4
CROSS_HEAD_DIM = 128
CROSS_WIDTH = CROSS_HEADS * CROSS_HEAD_DIM

N_EXPERTS = 32
TOP_K = 4
D_FF = D_MODEL
SWIGLU_ALPHA = 1.702
SWIGLU_LIMIT = 7.0

LANES = 128
SUBLANES = 8
DT_PAD = LANES
PROJ_COLS = SSM_WIDTH + CONV_CH + DT_PAD + QKV_COLS
VMEM_LIMIT = 56 * 1024 * 1024

TM_PROJ = 512
TM_MID = 512
SSD_ROWS = 4 * CHUNK
MOE_ROWS = 512
MOE_PART = 128


def _nt(a, b):
    return lax.dot_general(a, b, (((1,), (1,)), ((), ())), preferred_element_type=F32)


def _tn(a, b):
    return lax.dot_general(a, b, (((0,), (0,)), ((), ())), preferred_element_type=F32)


def _dot(a, b):
    return jnp.dot(a, b, preferred_element_type=F32)


def _rms(x, g):
    return x * lax.rsqrt(jnp.mean(x * x, axis=-1, keepdims=True) + RMS_EPS) * g


def _sigmoid(x):
    return 0.5 * jnp.tanh(0.5 * x) + 0.5


def _params(sem=None, **kwargs):
    return pltpu.CompilerParams(dimension_semantics=sem, vmem_limit_bytes=VMEM_LIMIT, **kwargs)


def _inproj_kernel(x_ref, g_ref, w_f32, bq_ref, z_ref, xbc_ref, dt_ref, qkv_ref, w_ref):
    @pl.when(pl.program_id(0) == 0)
    def _():
        c1 = SSM_WIDTH + CONV_CH
        w_ref[:, :c1] = w_f32[:, :c1].astype(BF16)
        lane = lax.broadcasted_iota(I32, (D_MODEL, DT_PAD), 1)
        w_ref[:, c1:c1 + DT_PAD] = jnp.where(lane < SSM_HEADS, w_f32[:, c1:c1 + DT_PAD], 0.0).astype(BF16)
        w_ref[:, c1 + DT_PAD:] = w_f32[:, c1 + SSM_HEADS:].astype(BF16)

    u = _rms(x_ref[...], g_ref[...])
    p = _dot(u.astype(BF16), w_ref[...])
    z_ref[...] = p[:, :SSM_WIDTH]
    xbc_ref[...] = p[:, SSM_WIDTH:SSM_WIDTH + CONV_CH]
    dt_ref[...] = p[:, SSM_WIDTH + CONV_CH:SSM_WIDTH + CONV_CH + DT_PAD]
    qkv_ref[...] = p[:, SSM_WIDTH + CONV_CH + DT_PAD:] + bq_ref[...]


def _inproj(x2, g_mix, w_in, b_qkv):
    n = x2.shape[0]
    row = lambda i: (i, 0)
    fixed = lambda i: (0, 0)
    return pl.pallas_call(
        _inproj_kernel,
        name="inproj",
        grid=(n // TM_PROJ,),
        in_specs=[
            pl.BlockSpec((TM_PROJ, D_MODEL), row),
            pl.BlockSpec((1, D_MODEL), fixed),
            pl.BlockSpec(w_in.shape, fixed),
            pl.BlockSpec((1, QKV_COLS), fixed),
        ],
        out_specs=[
            pl.BlockSpec((TM_PROJ, SSM_WIDTH), row),
            pl.BlockSpec((TM_PROJ, CONV_CH), row),
            pl.BlockSpec((TM_PROJ, DT_PAD), row),
            pl.BlockSpec((TM_PROJ, QKV_COLS), row),
        ],
        out_shape=[
            jax.ShapeDtypeStruct((n, SSM_WIDTH), F32),
            jax.ShapeDtypeStruct((n, CONV_CH), F32),
            jax.ShapeDtypeStruct((n, DT_PAD), F32),
            jax.ShapeDtypeStruct((n, QKV_COLS), F32),
        ],
        scratch_shapes=[pltpu.VMEM((D_MODEL, PROJ_COLS), BF16)],
        compiler_params=_params(("arbitrary",)),
    )(x2, g_mix, w_in, b_qkv)


def _split3(x):
    hi = x.astype(BF16)
    rest = x - hi.astype(F32)
    mid = rest.astype(BF16)
    lo = (rest - mid.astype(F32)).astype(BF16)
    return jnp.concatenate([hi, mid, lo], axis=1)


def _ssd_kernel(xbc_ref, z_ref, dt_ref, cw_ref, cb_ref, dtb_ref, ah_ref, dsk_ref, gout_ref,
                e64_ref, ecat_ref, tri_ref, y_ref, conv_ref, state_ref):
    c = pl.program_id(0)

    @pl.when(c == 0)
    def _():
        conv_ref[:, :SUBLANES, :] = jnp.zeros((conv_ref.shape[0], SUBLANES, CONV_CH), F32)
        state_ref[...] = jnp.zeros_like(state_ref)

    for cc in range(xbc_ref.shape[1] // CHUNK):
        for b in range(xbc_ref.shape[0]):
            _ssd_chunk(b, slice(cc * CHUNK, (cc + 1) * CHUNK), xbc_ref, z_ref, dt_ref, cw_ref, cb_ref, dtb_ref,
                       ah_ref, dsk_ref, gout_ref, e64_ref, ecat_ref, tri_ref, y_ref, conv_ref, state_ref)


def _ssd_chunk(b, rows, xbc_ref, z_ref, dt_ref, cw_ref, cb_ref, dtb_ref, ah_ref, dsk_ref, gout_ref,
               e64_ref, ecat_ref, tri_ref, y_ref, conv_ref, state_ref):
    u = xbc_ref[b, rows, :]
    conv_ref[b, SUBLANES:, :] = u
    acc = u * cw_ref[CONV_WIDTH - 1:CONV_WIDTH, :] + cb_ref[...]
    for j in range(1, CONV_WIDTH):
        shifted = conv_ref[b, SUBLANES - j:SUBLANES - j + CHUNK, :]
        acc = acc + shifted * cw_ref[CONV_WIDTH - 1 - j:CONV_WIDTH - j, :]
    conv_ref[b, :SUBLANES, :] = u[CHUNK - SUBLANES:, :]
    xbc = acc * _sigmoid(acc)

    xs = xbc[:, :SSM_WIDTH]
    bm = xbc[:, SSM_WIDTH:SSM_WIDTH + SSM_GROUPS * SSM_STATE]
    cm = xbc[:, SSM_WIDTH + SSM_GROUPS * SSM_STATE:]

    dtr = dt_ref[b, rows, :] + dtb_ref[...]
    dt = jnp.maximum(dtr, 0.0) + jnp.log1p(jnp.exp(-jnp.abs(dtr)))
    a = dt * ah_ref[...]
    a3 = _split3(a)
    a_rows = jnp.concatenate([a3[:, :DT_PAD], a3[:, DT_PAD:2 * DT_PAD], a3[:, 2 * DT_PAD:]], axis=0)
    cs = _dot(tri_ref[...], a_rows)
    cs_row = cs.T
    dt_full = _dot(_split3(dt), e64_ref[...])
    cs_cat = _dot(_split3(cs), ecat_ref[...])
    cs_wide = cs_cat[:, :SSM_HEADS * CHUNK]
    cs_full = cs_cat[:, SSM_HEADS * CHUNK:]
    cs_last = cs_full[CHUNK - 1:CHUNK, :]

    xdt = xs * dt_full
    xdt_b = xdt.astype(BF16)
    xw_b = (xdt * jnp.exp(cs_last - cs_full)).astype(BF16)
    in_decay = jnp.exp(cs_full)
    chunk_decay = jnp.exp(cs_last)

    li = lax.broadcasted_iota(I32, (CHUNK, CHUNK), 0)
    si = lax.broadcasted_iota(I32, (CHUNK, CHUNK), 1)
    causal = li >= si

    ys = []
    for g in range(SSM_GROUPS):
        bg = bm[:, g * SSM_STATE:(g + 1) * SSM_STATE].astype(BF16)
        cg = cm[:, g * SSM_STATE:(g + 1) * SSM_STATE].astype(BF16)
        cb = _nt(cg, bg)
        yd = []
        for r in range(HEADS_PER_GROUP):
            h = g * HEADS_PER_GROUP + r
            diff = cs_wide[:, h * CHUNK:(h + 1) * CHUNK] - cs_row[h:h + 1, :]
            m = cb * jnp.exp(jnp.where(causal, diff, -jnp.inf))
            yd.append(_dot(m.astype(BF16), xdt_b[:, h * SSM_HEAD_DIM:(h + 1) * SSM_HEAD_DIM]))
        y_diag = jnp.concatenate(yd, axis=1)
        gs = slice(g * GROUP_WIDTH, (g + 1) * GROUP_WIDTH)
        st = state_ref[b, g]
        y_off = _dot(cg, st.astype(BF16)) * in_decay[:, gs]
        state_ref[b, g] = st * chunk_decay[:, gs] + _tn(bg, xw_b[:, gs])
        ys.append(y_diag + y_off)
    y = jnp.concatenate(ys, axis=1) + dsk_ref[...] * xs

    zz = z_ref[b, rows, :]
    y = y * (zz * _sigmoid(zz))
    outs = []
    for g in range(SSM_GROUPS):
        yg = y[:, g * GROUP_WIDTH:(g + 1) * GROUP_WIDTH]
        outs.append(yg * lax.rsqrt(jnp.mean(yg * yg, axis=-1, keepdims=True) + RMS_EPS))
    y_ref[b, rows, :] = jnp.concatenate(outs, axis=1) * gout_ref[...]


def _ssd(xbc, z, dtp, conv_w, conv_b, dt_bias_p, a_head_p, d_full, g_out):
    batch, seq, _ = xbc.shape
    chunk = lambda c: (0, c, 0)
    fixed = lambda c: (0, 0)
    head = np.arange(SSM_WIDTH) // SSM_HEAD_DIM
    e64 = (np.arange(DT_PAD)[:, None] == head[None, :]).astype(np.float32)
    head_w = np.arange(SSM_HEADS * CHUNK) // CHUNK
    e128 = (np.arange(DT_PAD)[:, None] == head_w[None, :]).astype(np.float32)
    e64_3 = jnp.asarray(np.tile(e64, (3, 1))).astype(BF16)
    ecat_3 = jnp.asarray(np.tile(np.concatenate([e128, e64], axis=1), (3, 1))).astype(BF16)
    tri_3 = jnp.asarray(np.tile(np.tril(np.ones((CHUNK, CHUNK), np.float32)), (1, 3))).astype(BF16)
    return pl.pallas_call(
        _ssd_kernel,
        name="ssd",
        grid=(seq // SSD_ROWS,),
        in_specs=[
            pl.BlockSpec((batch, SSD_ROWS, CONV_CH), chunk),
            pl.BlockSpec((batch, SSD_ROWS, SSM_WIDTH), chunk),
            pl.BlockSpec((batch, SSD_ROWS, DT_PAD), chunk),
            pl.BlockSpec((CONV_WIDTH, CONV_CH), fixed),
            pl.BlockSpec((1, CONV_CH), fixed),
            pl.BlockSpec((1, DT_PAD), fixed),
            pl.BlockSpec((1, DT_PAD), fixed),
            pl.BlockSpec((1, SSM_WIDTH), fixed),
            pl.BlockSpec((1, SSM_WIDTH), fixed),
            pl.BlockSpec((3 * DT_PAD, SSM_WIDTH), fixed),
            pl.BlockSpec((3 * DT_PAD, SSM_HEADS * CHUNK + SSM_WIDTH), fixed),
            pl.BlockSpec((CHUNK, 3 * CHUNK), fixed),
        ],
        out_specs=pl.BlockSpec((batch, SSD_ROWS, SSM_WIDTH), chunk),
        out_shape=jax.ShapeDtypeStruct((batch, seq, SSM_WIDTH), F32),
        scratch_shapes=[
            pltpu.VMEM((batch, SUBLANES + CHUNK, CONV_CH), F32),
            pltpu.VMEM((batch, SSM_GROUPS, SSM_STATE, GROUP_WIDTH), F32),
        ],
        compiler_params=_params(("arbitrary",)),
    )(xbc, z, dtp, conv_w, conv_b, dt_bias_p, a_head_p, d_full, g_out, e64_3, ecat_3, tri_3)


def _t5_bucket_table():
    q_pos = np.arange(BLOCK)[:, None]
    k_pos = np.arange(2 * BLOCK)[None, :] - BLOCK
    dist = q_pos - k_pos
    d = np.maximum(dist, 0)
    max_exact = REL_BUCKETS // 2
    ratio = np.maximum(d, max_exact).astype(np.float32) / np.float32(max_exact)
    large = max_exact + (np.log(ratio) / np.float32(math.log(REL_MAX_DIST / max_exact))
                         * np.float32(REL_BUCKETS - max_exact)).astype(np.int32)
    large = np.minimum(large, REL_BUCKETS - 1)
    bucket = np.where(d < max_exact, d, large)
    in_window = (dist >= 0) & (dist < WINDOW)
    return np.where(in_window, bucket, -1).astype(np.int32)


def _bias_kernel(rb_ref, bucket_ref, out_ref):
    bucket = bucket_ref[...]
    for h in range(ATTN_HEADS):
        acc = jnp.zeros(bucket.shape, F32)
        for b in range(REL_BUCKETS):
            acc = jnp.where(bucket == b, rb_ref[b, h], acc)
        out_ref[h] = jnp.where(bucket >= 0, acc, -jnp.inf)


def _bias_table(rel_bias):
    bucket = jnp.asarray(_t5_bucket_table())
    return pl.pallas_call(
        _bias_kernel,
        name="bias_table",
        in_specs=[
            pl.BlockSpec(memory_space=pltpu.SMEM),
            pl.BlockSpec(memory_space=pltpu.VMEM),
        ],
        out_specs=pl.BlockSpec(memory_space=pltpu.VMEM),
        out_shape=jax.ShapeDtypeStruct((ATTN_HEADS, BLOCK, 2 * BLOCK), F32),
    )(rel_bias, bucket)


def _swa_kernel(sink_ref, q_ref, k_ref, v_ref, kp_ref, vp_ref, bias_ref, o_ref):
    i = pl.program_id(0)
    col = lax.broadcasted_iota(I32, (BLOCK, 2 * BLOCK), 1)
    key_ok = jnp.logical_or(col >= BLOCK, i > 0)
    for b in range(q_ref.shape[0]):
        q = (q_ref[b] * (HEAD_DIM ** -0.5)).astype(BF16)
        outs = []
        for g in range(KV_HEADS):
            gs = slice(g * HEAD_DIM, (g + 1) * HEAD_DIM)
            k2 = jnp.concatenate([kp_ref[b, :, gs], k_ref[b, :, gs]], axis=0).astype(BF16)
            v2 = jnp.concatenate([vp_ref[b, :, gs], v_ref[b, :, gs]], axis=0).astype(BF16)
            for r in range(Q_PER_KV):
                h = g * Q_PER_KV + r
                s = _nt(q[:, h * HEAD_DIM:(h + 1) * HEAD_DIM], k2) + bias_ref[h]
                s = jnp.where(key_ok, s, -jnp.inf)
                sink = sink_ref[h]
                m = jnp.maximum(jnp.max(s, axis=-1, keepdims=True), sink)
                p = jnp.exp(s - m)
                denom = jnp.sum(p, axis=-1, keepdims=True) + jnp.exp(sink - m)
                outs.append(_dot(p.astype(BF16), v2) / denom)
        o_ref[b] = jnp.concatenate(outs, axis=1)


def _swa(qkv, sinks, bias):
    batch, seq, _ = qkv.shape
    kcol = ATTN_WIDTH // KV_WIDTH
    prev = lambda i: jnp.maximum(i - 1, 0)
    return pl.pallas_call(
        _swa_kernel,
        name="swa",
        grid_spec=pltpu.PrefetchScalarGridSpec(
            num_scalar_prefetch=1,
            grid=(seq // BLOCK,),
            in_specs=[
                pl.BlockSpec((batch, BLOCK, ATTN_WIDTH), lambda i, s: (0, i, 0)),
                pl.BlockSpec((batch, BLOCK, KV_WIDTH), lambda i, s: (0, i, kcol)),
                pl.BlockSpec((batch, BLOCK, KV_WIDTH), lambda i, s: (0, i, kcol + 1)),
                pl.BlockSpec((batch, BLOCK, KV_WIDTH), lambda i, s: (0, prev(i), kcol)),
                pl.BlockSpec((batch, BLOCK, KV_WIDTH), lambda i, s: (0, prev(i), kcol + 1)),
                pl.BlockSpec((ATTN_HEADS, BLOCK, 2 * BLOCK), lambda i, s: (0, 0, 0)),
            ],
            out_specs=pl.BlockSpec((batch, BLOCK, ATTN_WIDTH), lambda i, s: (0, i, 0)),
        ),
        out_shape=jax.ShapeDtypeStruct((batch, seq, ATTN_WIDTH), F32),
        compiler_params=_params(("arbitrary",)),
    )(sinks, qkv, qkv, qkv, qkv, qkv, bias)


def _memkv_kernel(m_ref, g_ref, w_ref, k_ref, v_ref):
    u = _rms(m_ref[...], g_ref[...])
    kv = _dot(u.astype(BF16), w_ref[...].astype(BF16))
    k_ref[...] = kv[:, :CROSS_WIDTH].astype(BF16)
    v_ref[...] = kv[:, CROSS_WIDTH:].astype(BF16)


def _memkv(mem2, g_mem, w_kv):
    n = mem2.shape[0]
    return pl.pallas_call(
        _memkv_kernel,
        name="memkv",
        out_shape=[jax.ShapeDtypeStruct((n, CROSS_WIDTH), BF16)] * 2,
        compiler_params=_params(),
    )(mem2, g_mem, w_kv)


def _mid_kernel(ys_ref, ya_ref, x_ref, wout_f32, bout_ref, gc_ref, wq_f32, k_ref, v_ref, wo_f32,
                gf_ref, wr_ref, br_ref, upper_ref, lower_ref,
                h2_ref, u3_ref, gate_ref, lp_ref, tcnt_ref, tbase_ref, run_ref, wout_ref, wq_ref, wo_ref):
    i = pl.program_id(0)

    @pl.when(i == 0)
    def _():
        run_ref[...] = jnp.zeros_like(run_ref)
        wout_ref[...] = wout_f32[...].astype(BF16)
        wq_ref[...] = wq_f32[...].astype(BF16)
        wo_ref[...] = wo_f32[...].astype(BF16)

    ycat = jnp.concatenate([ys_ref[...], ya_ref[...]], axis=1).astype(BF16)
    h1 = x_ref[...] + _dot(ycat, wout_ref[...]) + bout_ref[...]

    u2 = _rms(h1, gc_ref[...])
    q = _dot(u2.astype(BF16), wq_ref[...]).astype(BF16)
    outs = []
    for h in range(CROSS_HEADS):
        hs = slice(h * CROSS_HEAD_DIM, (h + 1) * CROSS_HEAD_DIM)
        s = _nt(q[:, hs], k_ref[:, hs]) * (CROSS_HEAD_DIM ** -0.5)
        m = jnp.max(s, axis=-1, keepdims=True)
        p = jnp.exp(s - m)
        outs.append(_dot(p.astype(BF16), v_ref[:, hs]) / jnp.sum(p, axis=-1, keepdims=True))
    o = jnp.concatenate(outs, axis=1).astype(BF16)
    h2 = h1 + _dot(o, wo_ref[...])
    h2_ref[...] = h2
    u3 = _rms(h2, gf_ref[...]).astype(BF16)
    u3_ref[...] = u3

    logits = _nt(wr_ref[...], u3) + br_ref[...]
    eid = lax.broadcasted_iota(I32, logits.shape, 0)
    vals, idxs, hots = [], [], []
    for _ in range(TOP_K):
        m = jnp.max(logits, axis=0, keepdims=True)
        ix = jnp.min(jnp.where(logits == m, eid, N_EXPERTS), axis=0, keepdims=True)
        hot = eid == ix
        logits = jnp.where(hot, -jnp.inf, logits)
        vals.append(m)
        idxs.append(ix)
        hots.append(hot)
    ex = [jnp.exp(v - vals[0]) for v in vals]
    tot = ex[0] + ex[1] + ex[2] + ex[3]
    gate_ref[...] = jnp.concatenate([e / tot for e in ex], axis=0)

    hot_f = [jnp.where(hot, 1.0, 0.0) for hot in hots]
    cnt_k = [jnp.sum(hf, axis=1, keepdims=True) for hf in hot_f]
    tile_cnt = cnt_k[0] + cnt_k[1] + cnt_k[2] + cnt_k[3]
    tile_cnt = jnp.floor((tile_cnt + (SUBLANES - 1)) * (1.0 / SUBLANES)) * SUBLANES
    units = jnp.broadcast_to(tile_cnt * (1.0 / SUBLANES), (N_EXPERTS, LANES)).astype(BF16)
    offs = _dot(lower_ref[...], units)[:, :1] * SUBLANES
    local = []
    before_all = _dot(jnp.concatenate(hot_f, axis=0).astype(BF16), upper_ref[...])
    for kk in range(TOP_K):
        before = before_all[kk * N_EXPERTS:(kk + 1) * N_EXPERTS, :]
        local.append(jnp.sum(jnp.where(hots[kk], before + offs, 0.0), axis=0, keepdims=True))
        offs = offs + cnt_k[kk]
    lp_ref[...] = jnp.concatenate(local, axis=0).astype(I32)
    tcnt_ref[0] = jnp.broadcast_to(tile_cnt, (N_EXPERTS, LANES)).astype(I32)
    tbase_ref[0] = jnp.broadcast_to(run_ref[...], (N_EXPERTS, LANES)).astype(I32)
    run_ref[...] = run_ref[...] + tile_cnt


def _mid(y_ssm, y_attn, x2, w_out, b_out, g_cross, w_q, k_mem, v_mem, w_o, g_ffn, w_r_t, b_r, seq, mem_len):
    n = x2.shape[0]
    tiles_per_batch = seq // TM_MID
    row = lambda i: (i, 0)
    col = lambda i: (0, i)
    fixed = lambda i: (0, 0)
    memb = lambda i: (i // tiles_per_batch, 0)
    tile3 = lambda i: (i, 0, 0)
    n_tiles = n // TM_MID
    upper = jnp.asarray(np.triu(np.ones((TM_MID, TM_MID), np.float32), 1)).astype(BF16)
    lower = jnp.asarray(np.tril(np.ones((N_EXPERTS, N_EXPERTS), np.float32), -1)).astype(BF16)
    return pl.pallas_call(
        _mid_kernel,
        name="mid",
        grid=(n_tiles,),
        in_specs=[
            pl.BlockSpec((TM_MID, SSM_WIDTH), row),
            pl.BlockSpec((TM_MID, ATTN_WIDTH), row),
            pl.BlockSpec((TM_MID, D_MODEL), row),
            pl.BlockSpec((D_MODEL, D_MODEL), fixed),
            pl.BlockSpec((1, D_MODEL), fixed),
            pl.BlockSpec((1, D_MODEL), fixed),
            pl.BlockSpec((D_MODEL, CROSS_WIDTH), fixed),
            pl.BlockSpec((mem_len, CROSS_WIDTH), memb),
            pl.BlockSpec((mem_len, CROSS_WIDTH), memb),
            pl.BlockSpec((CROSS_WIDTH, D_MODEL), fixed),
            pl.BlockSpec((1, D_MODEL), fixed),
            pl.BlockSpec((N_EXPERTS, D_MODEL), fixed),
            pl.BlockSpec((N_EXPERTS, 1), fixed),
            pl.BlockSpec((TM_MID, TM_MID), fixed),
            pl.BlockSpec((N_EXPERTS, N_EXPERTS), fixed),
        ],
        out_specs=[
            pl.BlockSpec((TM_MID, D_MODEL), row),
            pl.BlockSpec((TM_MID, D_MODEL), row),
            pl.BlockSpec((TOP_K, TM_MID), col),
            pl.BlockSpec((TOP_K, TM_MID), col),
            pl.BlockSpec((1, N_EXPERTS, LANES), tile3),
            pl.BlockSpec((1, N_EXPERTS, LANES), tile3),
        ],
        out_shape=[
            jax.ShapeDtypeStruct((n, D_MODEL), F32),
            jax.ShapeDtypeStruct((n, D_MODEL), BF16),
            jax.ShapeDtypeStruct((TOP_K, n), F32),
            jax.ShapeDtypeStruct((TOP_K, n), I32),
            jax.ShapeDtypeStruct((n_tiles, N_EXPERTS, LANES), I32),
            jax.ShapeDtypeStruct((n_tiles, N_EXPERTS, LANES), I32),
        ],
        scratch_shapes=[
            pltpu.VMEM((N_EXPERTS, 1), F32),
            pltpu.VMEM((D_MODEL, D_MODEL), BF16),
            pltpu.VMEM((D_MODEL, CROSS_WIDTH), BF16),
            pltpu.VMEM((CROSS_WIDTH, D_MODEL), BF16),
        ],
        compiler_params=_params(("arbitrary",)),
    )(y_ssm, y_attn, x2, w_out, b_out, g_cross, w_q, k_mem, v_mem, w_o, g_ffn, w_r_t, b_r, upper, lower)


ROW_CHUNK = 256
LOCAL_ROWS = -(-(TM_MID * TOP_K + N_EXPERTS * (SUBLANES - 1)) // ROW_CHUNK) * ROW_CHUNK


U32 = jnp.uint32
PACKED = D_MODEL // 2


def _pack_pairs(v):
    bits = lax.bitcast_convert_type(v, U32)
    return bits[:, PACKED:] | (bits[:, :PACKED] >> 16)


def _unpack_pairs(w):
    lo = lax.bitcast_convert_type(w << 16, F32)
    hi = lax.bitcast_convert_type(w & jnp.uint32(0xFFFF0000), F32)
    return jnp.concatenate([lo, hi], axis=1).astype(BF16)


def _row_copy(n, src_ref, src, dst_ref, dst, sem):
    aligned = lambda v: v if isinstance(v, int) else pl.multiple_of(v, SUBLANES)
    return pltpu.make_async_copy(src_ref.at[pl.ds(aligned(src), aligned(n))],
                                 dst_ref.at[pl.ds(aligned(dst), aligned(n))], sem)


def _start_run(n, src_ref, src, dst_ref, dst, sem):
    @pl.when(n > 0)
    def _():
        _row_copy(n, src_ref, src, dst_ref, dst, sem).start()


def _wait_rows(n, src_ref, dst_ref, sem):
    _row_copy(n, src_ref, 0, dst_ref, 0, sem).wait()


def _dispatch_kernel(tcnt_ref, lo_ref, gs_ref, padn_ref, pads_ref, na_ref, u_ref, lp_ref, xs_ref,
                     xloc_ref, zero_ref, sem, zsem):
    i = pl.program_id(0)

    def zero_fill(wait):
        def fill(n, dst):
            cp = _row_copy(n, zero_ref, 0, xs_ref, dst, zsem)
            if wait:
                cp.wait()
            else:
                cp.start()

        def body(e, carry):
            pl.when(padn_ref[e] > 0)(lambda: fill(padn_ref[e], pads_ref[e]))
            return carry
        lax.fori_loop(0, N_EXPERTS, body, 0)

        def tail(b, carry):
            fill(MOE_ROWS, b * MOE_ROWS)
            return carry
        lax.fori_loop(na_ref[0], xs_ref.shape[0] // MOE_ROWS, tail, 0)

    @pl.when(i == 0)
    def _():
        zero_ref[...] = jnp.zeros_like(zero_ref)
        zero_fill(False)

    slot = i % 2
    u = u_ref[...]
    lp = lp_ref[...]
    for c in range(LOCAL_ROWS // ROW_CHUNK):
        r = lax.broadcasted_iota(I32, (ROW_CHUNK, TM_MID), 0) + c * ROW_CHUNK
        p = jnp.zeros((ROW_CHUNK, TM_MID), F32)
        for kk in range(TOP_K):
            p = jnp.where(lp[kk:kk + 1, :] == r, 1.0, p)
        xloc_ref[slot, c * ROW_CHUNK:(c + 1) * ROW_CHUNK, :] = _pack_pairs(_dot(p.astype(BF16), u))

    def start_runs(tile):
        s = tile % 2

        def body(e, carry):
            j = tile * N_EXPERTS + e
            _start_run(tcnt_ref[j], xloc_ref.at[s], lo_ref[j], xs_ref, gs_ref[j], sem.at[s])
            return carry
        lax.fori_loop(0, N_EXPERTS, body, 0)

    def wait_runs(tile):
        s = tile % 2
        last = tile * N_EXPERTS + N_EXPERTS - 1
        _wait_rows(lo_ref[last] + tcnt_ref[last], xloc_ref.at[s], xs_ref, sem.at[s])

    start_runs(i)

    @pl.when(i > 0)
    def _():
        wait_runs(i - 1)

    @pl.when(i == pl.num_programs(0) - 1)
    def _():
        wait_runs(i)
        zero_fill(True)


def _dispatch(tcnt, lo, gstart, padn, pads, n_active, u3, lp, n_rows):
    n = u3.shape[0]
    return pl.pallas_call(
        _dispatch_kernel,
        name="dispatch",
        grid_spec=pltpu.PrefetchScalarGridSpec(
            num_scalar_prefetch=6,
            grid=(n // TM_MID,),
            in_specs=[
                pl.BlockSpec((TM_MID, D_MODEL), lambda i, *_: (i, 0)),
                pl.BlockSpec((TOP_K, TM_MID), lambda i, *_: (0, i)),
            ],
            out_specs=pl.BlockSpec(memory_space=pl.ANY),
            scratch_shapes=[
                pltpu.VMEM((2, LOCAL_ROWS, PACKED), U32),
                pltpu.VMEM((MOE_ROWS, PACKED), U32),
                pltpu.SemaphoreType.DMA((2,)),
                pltpu.SemaphoreType.DMA,
            ],
        ),
        out_shape=jax.ShapeDtypeStruct((n_rows, PACKED), U32),
        compiler_params=_params(("arbitrary",)),
    )(tcnt, lo, gstart, padn, pads, n_active, u3, lp)


PAIR_COLS = 2 * LANES


def _expert_kernel(be_ref, nx_ref, nv_ref, na_ref, x_ref, b1_ref, b2_ref, perm_ref, w1_hbm, w2_hbm, y_ref,
                   w1s_ref, w2s_ref, w1p_ref, w2b_ref, sem):
    b = pl.program_id(0)
    valid_rows = nv_ref[b]
    active = valid_rows > 0
    expert = be_ref[b]
    new_expert = jnp.logical_or(b == 0, expert != be_ref[jnp.maximum(b - 1, 0)])

    def fetch(e, wait):
        for src, dst, s in ((w1_hbm, w1s_ref, 0), (w2_hbm, w2s_ref, 1)):
            cp = pltpu.make_async_copy(src.at[e], dst, sem.at[s])
            if wait:
                cp.wait()
            else:
                cp.start(priority=1)

    @pl.when(jnp.logical_and(b == 0, active))
    def _():
        fetch(expert, False)

    @pl.when(jnp.logical_and(active, new_expert))
    def _():
        fetch(expert, True)
        for c in range(2 * D_FF // PAIR_COLS):
            cs = slice(c * PAIR_COLS, (c + 1) * PAIR_COLS)
            w1p_ref[:, cs] = _dot(w1s_ref[:, cs].astype(BF16), perm_ref[...]).astype(BF16)
        w2b_ref[...] = w2s_ref[...].astype(BF16)

        @pl.when(nx_ref[b] >= 0)
        def _():
            fetch(nx_ref[b], False)

    for live in range(MOE_PART, MOE_ROWS + 1, MOE_PART):
        @pl.when(jnp.logical_and(valid_rows > live - MOE_PART, valid_rows <= live))
        def _():
            x = _unpack_pairs(x_ref[:live, :])
            hdn = _dot(x, w1p_ref[...]) + b1_ref[0]
            acts = []
            for c in range(2 * D_FF // PAIR_COLS):
                hg = jnp.minimum(hdn[:, c * PAIR_COLS:c * PAIR_COLS + LANES], SWIGLU_LIMIT)
                hl = jnp.clip(hdn[:, c * PAIR_COLS + LANES:(c + 1) * PAIR_COLS], -SWIGLU_LIMIT, SWIGLU_LIMIT)
                acts.append((hg * _sigmoid(SWIGLU_ALPHA * hg) * (hl + 1.0)).astype(BF16))
            y = _dot(jnp.concatenate(acts, axis=1), w2b_ref[...]) + b2_ref[0]
            y_ref[:live, :] = _pack_pairs(y.astype(BF16).astype(F32))
            if live < MOE_ROWS:
                y_ref[live:, :] = jnp.zeros((MOE_ROWS - live, PACKED), U32)

    @pl.when(jnp.logical_not(active))
    def _():
        y_ref[...] = jnp.zeros_like(y_ref)


def _experts(block_exp, next_exp, valid_rows, n_active, xs, w1, b1p, w2, b2):
    n_rows = xs.shape[0]
    n_blocks = n_rows // MOE_ROWS
    xmap = lambda b, be, nx, nv, na: (jnp.maximum(jnp.minimum(b, na[0] - 1), 0), 0)
    emap = lambda b, be, nx, nv, na: (be[b], 0, 0)
    perm = np.zeros((PAIR_COLS, PAIR_COLS), np.float32)
    perm[2 * np.arange(LANES), np.arange(LANES)] = 1.0
    perm[2 * np.arange(LANES) + 1, LANES + np.arange(LANES)] = 1.0
    return pl.pallas_call(
        _expert_kernel,
        name="experts",
        grid_spec=pltpu.PrefetchScalarGridSpec(
            num_scalar_prefetch=4,
            grid=(n_blocks,),
            in_specs=[
                pl.BlockSpec((MOE_ROWS, PACKED), xmap),
                pl.BlockSpec((1, 1, 2 * D_FF), emap),
                pl.BlockSpec((1, 1, D_MODEL), emap),
                pl.BlockSpec((PAIR_COLS, PAIR_COLS), lambda b, *_: (0, 0)),
                pl.BlockSpec(memory_space=pl.ANY),
                pl.BlockSpec(memory_space=pl.ANY),
            ],
            out_specs=pl.BlockSpec((MOE_ROWS, PACKED), lambda b, *_: (b, 0)),
            scratch_shapes=[
                pltpu.VMEM((D_MODEL, 2 * D_FF), F32),
                pltpu.VMEM((D_FF, D_MODEL), F32),
                pltpu.VMEM((D_MODEL, 2 * D_FF), BF16),
                pltpu.VMEM((D_FF, D_MODEL), BF16),
                pltpu.SemaphoreType.DMA((2,)),
            ],
        ),
        out_shape=jax.ShapeDtypeStruct((n_rows, PACKED), U32),
        compiler_params=_params(("arbitrary",)),
    )(block_exp, next_exp, valid_rows, n_active, xs, b1p, b2, jnp.asarray(perm).astype(BF16), w1, w2)


def _combine_kernel(tcnt_ref, lo_ref, gs_ref, h2_ref, gate_ref, lp_ref, g_ref, ys_ref, o_ref, yloc_ref, sem):
    i = pl.program_id(0)

    slot = i % 2

    def start_runs(tile):
        s = tile % 2

        def body(e, carry):
            j = tile * N_EXPERTS + e
            _start_run(tcnt_ref[j], ys_ref, gs_ref[j], yloc_ref.at[s], lo_ref[j], sem.at[s])
            return carry
        lax.fori_loop(0, N_EXPERTS, body, 0)

    @pl.when(i == 0)
    def _():
        yloc_ref[...] = jnp.zeros_like(yloc_ref)
        start_runs(i)

    @pl.when(i + 1 < pl.num_programs(0))
    def _():
        start_runs(i + 1)

    gates = gate_ref[...]
    lp = lp_ref[...]
    last = i * N_EXPERTS + N_EXPERTS - 1
    _wait_rows(lo_ref[last] + tcnt_ref[last], ys_ref, yloc_ref.at[slot], sem.at[slot])

    h = h2_ref[...]
    for c in range(LOCAL_ROWS // ROW_CHUNK):
        r = lax.broadcasted_iota(I32, (ROW_CHUNK, TM_MID), 0) + c * ROW_CHUNK
        a = jnp.zeros((ROW_CHUNK, TM_MID), F32)
        for kk in range(TOP_K):
            a = jnp.where(lp[kk:kk + 1, :] == r, gates[kk:kk + 1, :], a)
        h = h + _tn(a.astype(BF16), _unpack_pairs(yloc_ref[slot, c * ROW_CHUNK:(c + 1) * ROW_CHUNK, :]))
    o_ref[...] = _rms(h, g_ref[...])


def _combine(tcnt, lo, gstart, h2, gates, lp, g_final, ys):
    n = h2.shape[0]
    return pl.pallas_call(
        _combine_kernel,
        name="combine",
        grid_spec=pltpu.PrefetchScalarGridSpec(
            num_scalar_prefetch=3,
            grid=(n // TM_MID,),
            in_specs=[
                pl.BlockSpec((TM_MID, D_MODEL), lambda i, *_: (i, 0)),
                pl.BlockSpec((TOP_K, TM_MID), lambda i, *_: (0, i)),
                pl.BlockSpec((TOP_K, TM_MID), lambda i, *_: (0, i)),
                pl.BlockSpec((1, D_MODEL), lambda i, *_: (0, 0)),
                pl.BlockSpec(memory_space=pl.ANY),
            ],
            out_specs=pl.BlockSpec((TM_MID, D_MODEL), lambda i, *_: (i, 0)),
            scratch_shapes=[
                pltpu.VMEM((2, LOCAL_ROWS, PACKED), U32),
                pltpu.SemaphoreType.DMA((2,)),
            ],
        ),
        out_shape=jax.ShapeDtypeStruct((n, D_MODEL), F32),
        compiler_params=_params(("arbitrary",)),
    )(tcnt, lo, gstart, h2, gates, lp, g_final, ys)


def _layer(h, mem2, batch, seq, mem_len, g_mix, w_in, b_qkv, conv_w, conv_b, dt_bias, a_log, d_skip,
           g_ssm_out, attn_sinks, bias, w_out, b_out, g_cross, g_mem, w_q_cross, w_kv_cross, w_o_cross,
           g_ffn, w_router, b_router, w1, b1, w2, b2, g_final):
    n = h.shape[0]
    z, xbc, dtp, qkv = _inproj(h, g_mix[None, :], w_in, b_qkv[None, :])

    pad_h = (0, DT_PAD - SSM_HEADS)
    dt_bias_p = jnp.pad(dt_bias, pad_h)[None, :]
    a_head_p = jnp.pad(-jnp.exp(a_log), pad_h)[None, :]
    d_full = jnp.repeat(d_skip, SSM_HEAD_DIM)[None, :]
    per_seq = lambda t: t.reshape(batch, seq, t.shape[-1])
    y_ssm = _ssd(per_seq(xbc), per_seq(z), per_seq(dtp), conv_w, conv_b[None, :], dt_bias_p, a_head_p,
                 d_full, g_ssm_out[None, :]).reshape(n, SSM_WIDTH)
    y_attn = _swa(per_seq(qkv), attn_sinks, bias).reshape(n, ATTN_WIDTH)

    k_mem, v_mem = _memkv(mem2, g_mem[None, :], w_kv_cross)
    h2, u3, gates, lp, tcnt, tbase = _mid(
        y_ssm, y_attn, h, w_out, b_out[None, :], g_cross[None, :],
        w_q_cross, k_mem, v_mem, w_o_cross, g_ffn[None, :],
        w_router.T.astype(BF16), b_router[:, None], seq, mem_len)

    max_rows = n * TOP_K + (n // TM_MID) * N_EXPERTS * (SUBLANES - 1)
    n_blocks = -(-max_rows // MOE_ROWS) + N_EXPERTS
    tcnt = tcnt[:, :, 0]
    tbase = tbase[:, :, 0]
    counts = tbase[-1] + tcnt[-1]
    padded = (counts + MOE_ROWS - 1) // MOE_ROWS * MOE_ROWS
    padded_end = jnp.cumsum(padded)
    start = (padded_end - padded).astype(I32)
    n_active = (padded_end[-1] // MOE_ROWS).astype(I32)
    blk = jnp.arange(n_blocks, dtype=I32)
    blk_exp = jnp.sum((blk * MOE_ROWS)[:, None] >= padded_end[None, :], axis=1)
    eid = jnp.arange(N_EXPERTS, dtype=I32)
    last_exp = jnp.max(jnp.where(padded > 0, eid, 0))
    blk_exp = jnp.where(blk < n_active, jnp.minimum(blk_exp, N_EXPERTS - 1), last_exp).astype(I32)
    later = jnp.logical_and(padded[None, :] > 0, eid[None, :] > eid[:, None])
    next_of = jnp.min(jnp.where(later, eid[None, :], N_EXPERTS), axis=1)
    next_of = jnp.where(next_of < N_EXPERTS, next_of, -1).astype(I32)
    of_blk = blk_exp[:, None] == eid[None, :]
    blk_next = jnp.sum(jnp.where(of_blk, next_of[None, :], 0), axis=1).astype(I32)
    blk_end = jnp.sum(jnp.where(of_blk, (start + counts)[None, :], 0), axis=1)
    blk_valid = jnp.clip(blk_end - blk * MOE_ROWS, 0, MOE_ROWS)
    blk_valid = jnp.where(blk < n_active, blk_valid, 0).astype(I32)

    run_n = tcnt.reshape(-1)
    run_local = (jnp.cumsum(tcnt, axis=1) - tcnt).reshape(-1)
    run_global = (start[None, :] + tbase).reshape(-1)
    xs = _dispatch(run_n, run_local, run_global, padded - counts, start + counts, n_active[None], u3, lp,
                   n_blocks * MOE_ROWS)

    b1p = b1.reshape(N_EXPERTS, 2 * D_FF // PAIR_COLS, LANES, 2).transpose(0, 1, 3, 2)
    ys = _experts(blk_exp, blk_next, blk_valid, n_active[None], xs, w1,
                  b1p.reshape(N_EXPERTS, 1, 2 * D_FF), w2, b2[:, None, :])
    return _combine(run_n, run_local, run_global, h2, gates, lp, g_final, ys)


def kernel(x, mem, g_mix, w_in, b_qkv, conv_w, conv_b, dt_bias, a_log, d_skip, g_ssm_out, attn_sinks, rel_bias, w_out, b_out, g_cross, g_mem, w_q_cross, w_kv_cross, w_o_cross, g_ffn, w_router, b_router, w1, b1, w2, b2, g_final):
    batch, seq, d = x.shape
    mem_len = mem.shape[1]
    depth = g_mix.shape[0]
    assert depth == 1 and d == D_MODEL and seq % TM_MID == 0
    h = x.reshape(batch * seq, d)
    mem2 = mem.reshape(batch * mem_len, d)
    bias = _bias_table(rel_bias)
    out = _layer(
        h, mem2, batch, seq, mem_len, g_mix[0], w_in[0], b_qkv[0], conv_w[0], conv_b[0], dt_bias[0],
        a_log[0], d_skip[0], g_ssm_out[0], attn_sinks[0], bias, w_out[0], b_out[0], g_cross[0],
        g_mem[0], w_q_cross[0], w_kv_cross[0], w_o_cross[0], g_ffn[0], w_router[0], b_router[0],
        w1[0], b1[0], w2[0], b2[0], g_final[None, :])
    return out.reshape(batch, seq, d)
```

```python
import math

import numpy as np
import jax
import jax.numpy as jnp
from jax import lax
from jax.experimental import pallas as pl
from jax.experimental.pallas import tpu as pltpu

F32 = jnp.float32
BF16 = jnp.bfloat16
I32 = jnp.int32

D_MODEL = 1024
RMS_EPS = 1e-5

SSM_HEADS = 8
SSM_HEAD_DIM = 64
SSM_WIDTH = SSM_HEADS * SSM_HEAD_DIM
SSM_STATE = 128
SSM_GROUPS = 2
HEADS_PER_GROUP = SSM_HEADS // SSM_GROUPS
GROUP_WIDTH = SSM_WIDTH // SSM_GROUPS
CONV_WIDTH = 4
CHUNK = 128
CONV_CH = SSM_WIDTH + 2 * SSM_GROUPS * SSM_STATE

ATTN_HEADS = 8
KV_HEADS = 2
Q_PER_KV = ATTN_HEADS // KV_HEADS
HEAD_DIM = 64
ATTN_WIDTH = ATTN_HEADS * HEAD_DIM
KV_WIDTH = KV_HEADS * HEAD_DIM
WINDOW = 128
BLOCK = WINDOW
QKV_COLS = ATTN_WIDTH + 2 * KV_WIDTH

REL_BUCKETS = 32
REL_MAX_DIST = 128

CROSS_HEADS = 4
CROSS_HEAD_DIM = 128
CROSS_WIDTH = CROSS_HEADS * CROSS_HEAD_DIM

N_EXPERTS = 32
TOP_K = 4
D_FF = D_MODEL
SWIGLU_ALPHA = 1.702
SWIGLU_LIMIT = 7.0

LANES = 128
SUBLANES = 8
DT_PAD = LANES
PROJ_COLS = SSM_WIDTH + CONV_CH + DT_PAD + QKV_COLS
VMEM_LIMIT = 56 * 1024 * 1024

TM_PROJ = 1024
TM_MID = 512
SSD_ROWS = 4 * CHUNK
MOE_ROWS = 512
MOE_PART = 128


def _nt(a, b):
    return lax.dot_general(a, b, (((1,), (1,)), ((), ())), preferred_element_type=F32)


def _tn(a, b):
    return lax.dot_general(a, b, (((0,), (0,)), ((), ())), preferred_element_type=F32)


def _dot(a, b):
    return jnp.dot(a, b, preferred_element_type=F32)


def _rms(x, g):
    return x * lax.rsqrt(jnp.mean(x * x, axis=-1, keepdims=True) + RMS_EPS) * g


def _sigmoid(x):
    return 0.5 * jnp.tanh(0.5 * x) + 0.5


def _params(sem=None, **kwargs):
    return pltpu.CompilerParams(dimension_semantics=sem, vmem_limit_bytes=VMEM_LIMIT, **kwargs)


def _inproj_kernel(x_ref, g_ref, w_f32, bq_ref, z_ref, xbc_ref, dt_ref, qkv_ref, w_ref):
    @pl.when(pl.program_id(0) == 0)
    def _():
        c1 = SSM_WIDTH + CONV_CH
        w_ref[:, :c1] = w_f32[:, :c1].astype(BF16)
        lane = lax.broadcasted_iota(I32, (D_MODEL, DT_PAD), 1)
        w_ref[:, c1:c1 + DT_PAD] = jnp.where(lane < SSM_HEADS, w_f32[:, c1:c1 + DT_PAD], 0.0).astype(BF16)
        w_ref[:, c1 + DT_PAD:] = w_f32[:, c1 + SSM_HEADS:].astype(BF16)

    u = _rms(x_ref[...], g_ref[...])
    p = _dot(u.astype(BF16), w_ref[...])
    z_ref[...] = p[:, :SSM_WIDTH]
    xbc_ref[...] = p[:, SSM_WIDTH:SSM_WIDTH + CONV_CH]
    dt_ref[...] = p[:, SSM_WIDTH + CONV_CH:SSM_WIDTH + CONV_CH + DT_PAD]
    qkv_ref[...] = p[:, SSM_WIDTH + CONV_CH + DT_PAD:] + bq_ref[...]


def _inproj(x2, g_mix, w_in, b_qkv):
    n = x2.shape[0]
    row = lambda i: (i, 0)
    fixed = lambda i: (0, 0)
    return pl.pallas_call(
        _inproj_kernel,
        name="inproj",
        grid=(n // TM_PROJ,),
        in_specs=[
            pl.BlockSpec((TM_PROJ, D_MODEL), row),
            pl.BlockSpec((1, D_MODEL), fixed),
            pl.BlockSpec(w_in.shape, fixed),
            pl.BlockSpec((1, QKV_COLS), fixed),
        ],
        out_specs=[
            pl.BlockSpec((TM_PROJ, SSM_WIDTH), row),
            pl.BlockSpec((TM_PROJ, CONV_CH), row),
            pl.BlockSpec((TM_PROJ, DT_PAD), row),
            pl.BlockSpec((TM_PROJ, QKV_COLS), row),
        ],
        out_shape=[
            jax.ShapeDtypeStruct((n, SSM_WIDTH), F32),
            jax.ShapeDtypeStruct((n, CONV_CH), F32),
            jax.ShapeDtypeStruct((n, DT_PAD), F32),
            jax.ShapeDtypeStruct((n, QKV_COLS), F32),
        ],
        scratch_shapes=[pltpu.VMEM((D_MODEL, PROJ_COLS), BF16)],
        compiler_params=_params(("arbitrary",)),
    )(x2, g_mix, w_in, b_qkv)


def _split3(x):
    hi = x.astype(BF16)
    rest = x - hi.astype(F32)
    mid = rest.astype(BF16)
    lo = (rest - mid.astype(F32)).astype(BF16)
    return jnp.concatenate([hi, mid, lo], axis=1)


def _ssd_kernel(xbc_ref, z_ref, dt_ref, cw_ref, cb_ref, dtb_ref, ah_ref, dsk_ref, gout_ref,
                e64_ref, ecat_ref, tri_ref, y_ref, conv_ref, state_ref):
    c = pl.program_id(0)

    @pl.when(c == 0)
    def _():
        conv_ref[:, :SUBLANES, :] = jnp.zeros((conv_ref.shape[0], SUBLANES, CONV_CH), F32)
        state_ref[...] = jnp.zeros_like(state_ref)

    for cc in range(xbc_ref.shape[1] // CHUNK):
        for b in range(xbc_ref.shape[0]):
            _ssd_chunk(b, slice(cc * CHUNK, (cc + 1) * CHUNK), xbc_ref, z_ref, dt_ref, cw_ref, cb_ref, dtb_ref,
                       ah_ref, dsk_ref, gout_ref, e64_ref, ecat_ref, tri_ref, y_ref, conv_ref, state_ref)


def _ssd_chunk(b, rows, xbc_ref, z_ref, dt_ref, cw_ref, cb_ref, dtb_ref, ah_ref, dsk_ref, gout_ref,
               e64_ref, ecat_ref, tri_ref, y_ref, conv_ref, state_ref):
    u = xbc_ref[b, rows, :]
    conv_ref[b, SUBLANES:, :] = u
    acc = u * cw_ref[CONV_WIDTH - 1:CONV_WIDTH, :] + cb_ref[...]
    for j in range(1, CONV_WIDTH):
        shifted = conv_ref[b, SUBLANES - j:SUBLANES - j + CHUNK, :]
        acc = acc + shifted * cw_ref[CONV_WIDTH - 1 - j:CONV_WIDTH - j, :]
    conv_ref[b, :SUBLANES, :] = u[CHUNK - SUBLANES:, :]
    xbc = acc * _sigmoid(acc)

    xs = xbc[:, :SSM_WIDTH]
    bm = xbc[:, SSM_WIDTH:SSM_WIDTH + SSM_GROUPS * SSM_STATE]
    cm = xbc[:, SSM_WIDTH + SSM_GROUPS * SSM_STATE:]

    dtr = dt_ref[b, rows, :] + dtb_ref[...]
    dt = jnp.maximum(dtr, 0.0) + jnp.log1p(jnp.exp(-jnp.abs(dtr)))
    a = dt * ah_ref[...]
    a3 = _split3(a)
    a_rows = jnp.concatenate([a3[:, :DT_PAD], a3[:, DT_PAD:2 * DT_PAD], a3[:, 2 * DT_PAD:]], axis=0)
    cs = _dot(tri_ref[...], a_rows)
    cs_row = cs.T
    dt_full = _dot(_split3(dt), e64_ref[...])
    cs_cat = _dot(_split3(cs), ecat_ref[...])
    cs_wide = cs_cat[:, :SSM_HEADS * CHUNK]
    cs_full = cs_cat[:, SSM_HEADS * CHUNK:]
    cs_last = cs_full[CHUNK - 1:CHUNK, :]

    xdt = xs * dt_full
    xdt_b = xdt.astype(BF16)
    xw_b = (xdt * jnp.exp(cs_last - cs_full)).astype(BF16)
    in_decay = jnp.exp(cs_full)
    chunk_decay = jnp.exp(cs_last)

    li = lax.broadcasted_iota(I32, (CHUNK, CHUNK), 0)
    si = lax.broadcasted_iota(I32, (CHUNK, CHUNK), 1)
    causal = li >= si

    ys = []
    for g in range(SSM_GROUPS):
        bg = bm[:, g * SSM_STATE:(g + 1) * SSM_STATE].astype(BF16)
        cg = cm[:, g * SSM_STATE:(g + 1) * SSM_STATE].astype(BF16)
        cb = _nt(cg, bg)
        yd = []
        for r in range(HEADS_PER_GROUP):
            h = g * HEADS_PER_GROUP + r
            diff = cs_wide[:, h * CHUNK:(h + 1) * CHUNK] - cs_row[h:h + 1, :]
            m = cb * jnp.exp(jnp.where(causal, diff, -jnp.inf))
            yd.append(_dot(m.astype(BF16), xdt_b[:, h * SSM_HEAD_DIM:(h + 1) * SSM_HEAD_DIM]))
        y_diag = jnp.concatenate(yd, axis=1)
        gs = slice(g * GROUP_WIDTH, (g + 1) * GROUP_WIDTH)
        st = state_ref[b, g]
        y_off = _dot(cg, st.astype(BF16)) * in_decay[:, gs]
        state_ref[b, g] = st * chunk_decay[:, gs] + _tn(bg, xw_b[:, gs])
        ys.append(y_diag + y_off)
    y = jnp.concatenate(ys, axis=1) + dsk_ref[...] * xs

    zz = z_ref[b, rows, :]
    y = y * (zz * _sigmoid(zz))
    outs = []
    for g in range(SSM_GROUPS):
        yg = y[:, g * GROUP_WIDTH:(g + 1) * GROUP_WIDTH]
        outs.append(yg * lax.rsqrt(jnp.mean(yg * yg, axis=-1, keepdims=True) + RMS_EPS))
    y_ref[b, rows, :] = jnp.concatenate(outs, axis=1) * gout_ref[...]


def _ssd(xbc, z, dtp, conv_w, conv_b, dt_bias_p, a_head_p, d_full, g_out):
    batch, seq, _ = xbc.shape
    chunk = lambda c: (0, c, 0)
    fixed = lambda c: (0, 0)
    head = np.arange(SSM_WIDTH) // SSM_HEAD_DIM
    e64 = (np.arange(DT_PAD)[:, None] == head[None, :]).astype(np.float32)
    head_w = np.arange(SSM_HEADS * CHUNK) // CHUNK
    e128 = (np.arange(DT_PAD)[:, None] == head_w[None, :]).astype(np.float32)
    e64_3 = jnp.asarray(np.tile(e64, (3, 1))).astype(BF16)
    ecat_3 = jnp.asarray(np.tile(np.concatenate([e128, e64], axis=1), (3, 1))).astype(BF16)
    tri_3 = jnp.asarray(np.tile(np.tril(np.ones((CHUNK, CHUNK), np.float32)), (1, 3))).astype(BF16)
    return pl.pallas_call(
        _ssd_kernel,
        name="ssd",
        grid=(seq // SSD_ROWS,),
        in_specs=[
            pl.BlockSpec((batch, SSD_ROWS, CONV_CH), chunk),
            pl.BlockSpec((batch, SSD_ROWS, SSM_WIDTH), chunk),
            pl.BlockSpec((batch, SSD_ROWS, DT_PAD), chunk),
            pl.BlockSpec((CONV_WIDTH, CONV_CH), fixed),
            pl.BlockSpec((1, CONV_CH), fixed),
            pl.BlockSpec((1, DT_PAD), fixed),
            pl.BlockSpec((1, DT_PAD), fixed),
            pl.BlockSpec((1, SSM_WIDTH), fixed),
            pl.BlockSpec((1, SSM_WIDTH), fixed),
            pl.BlockSpec((3 * DT_PAD, SSM_WIDTH), fixed),
            pl.BlockSpec((3 * DT_PAD, SSM_HEADS * CHUNK + SSM_WIDTH), fixed),
            pl.BlockSpec((CHUNK, 3 * CHUNK), fixed),
        ],
        out_specs=pl.BlockSpec((batch, SSD_ROWS, SSM_WIDTH), chunk),
        out_shape=jax.ShapeDtypeStruct((batch, seq, SSM_WIDTH), F32),
        scratch_shapes=[
            pltpu.VMEM((batch, SUBLANES + CHUNK, CONV_CH), F32),
            pltpu.VMEM((batch, SSM_GROUPS, SSM_STATE, GROUP_WIDTH), F32),
        ],
        compiler_params=_params(("arbitrary",)),
    )(xbc, z, dtp, conv_w, conv_b, dt_bias_p, a_head_p, d_full, g_out, e64_3, ecat_3, tri_3)


def _t5_bucket_table():
    q_pos = np.arange(BLOCK)[:, None]
    k_pos = np.arange(2 * BLOCK)[None, :] - BLOCK
    dist = q_pos - k_pos
    d = np.maximum(dist, 0)
    max_exact = REL_BUCKETS // 2
    ratio = np.maximum(d, max_exact).astype(np.float32) / np.float32(max_exact)
    large = max_exact + (np.log(ratio) / np.float32(math.log(REL_MAX_DIST / max_exact))
                         * np.float32(REL_BUCKETS - max_exact)).astype(np.int32)
    large = np.minimum(large, REL_BUCKETS - 1)
    bucket = np.where(d < max_exact, d, large)
    in_window = (dist >= 0) & (dist < WINDOW)
    return np.where(in_window, bucket, -1).astype(np.int32)


def _bias_kernel(rb_ref, bucket_ref, out_ref):
    bucket = bucket_ref[...]
    for h in range(ATTN_HEADS):
        acc = jnp.zeros(bucket.shape, F32)
        for b in range(REL_BUCKETS):
            acc = jnp.where(bucket == b, rb_ref[b, h], acc)
        out_ref[h] = jnp.where(bucket >= 0, acc, -jnp.inf)


def _bias_table(rel_bias):
    bucket = jnp.asarray(_t5_bucket_table())
    return pl.pallas_call(
        _bias_kernel,
        name="bias_table",
        in_specs=[
            pl.BlockSpec(memory_space=pltpu.SMEM),
            pl.BlockSpec(memory_space=pltpu.VMEM),
        ],
        out_specs=pl.BlockSpec(memory_space=pltpu.VMEM),
        out_shape=jax.ShapeDtypeStruct((ATTN_HEADS, BLOCK, 2 * BLOCK), F32),
    )(rel_bias, bucket)


def _swa_kernel(sink_ref, q_ref, k_ref, v_ref, kp_ref, vp_ref, bias_ref, o_ref):
    i = pl.program_id(0)
    col = lax.broadcasted_iota(I32, (BLOCK, 2 * BLOCK), 1)
    key_ok = jnp.logical_or(col >= BLOCK, i > 0)
    for b in range(q_ref.shape[0]):
        q = (q_ref[b] * (HEAD_DIM ** -0.5)).astype(BF16)
        outs = []
        for g in range(KV_HEADS):
            gs = slice(g * HEAD_DIM, (g + 1) * HEAD_DIM)
            k2 = jnp.concatenate([kp_ref[b, :, gs], k_ref[b, :, gs]], axis=0).astype(BF16)
            v2 = jnp.concatenate([vp_ref[b, :, gs], v_ref[b, :, gs]], axis=0).astype(BF16)
            for r in range(Q_PER_KV):
                h = g * Q_PER_KV + r
                s = _nt(q[:, h * HEAD_DIM:(h + 1) * HEAD_DIM], k2) + bias_ref[h]
                s = jnp.where(key_ok, s, -jnp.inf)
                sink = sink_ref[h]
                m = jnp.maximum(jnp.max(s, axis=-1, keepdims=True), sink)
                p = jnp.exp(s - m)
                denom = jnp.sum(p, axis=-1, keepdims=True) + jnp.exp(sink - m)
                outs.append(_dot(p.astype(BF16), v2) / denom)
        o_ref[b] = jnp.concatenate(outs, axis=1)


def _swa(qkv, sinks, bias):
    batch, seq, _ = qkv.shape
    kcol = ATTN_WIDTH // KV_WIDTH
    prev = lambda i: jnp.maximum(i - 1, 0)
    return pl.pallas_call(
        _swa_kernel,
        name="swa",
        grid_spec=pltpu.PrefetchScalarGridSpec(
            num_scalar_prefetch=1,
            grid=(seq // BLOCK,),
            in_specs=[
                pl.BlockSpec((batch, BLOCK, ATTN_WIDTH), lambda i, s: (0, i, 0)),
                pl.BlockSpec((batch, BLOCK, KV_WIDTH), lambda i, s: (0, i, kcol)),
                pl.BlockSpec((batch, BLOCK, KV_WIDTH), lambda i, s: (0, i, kcol + 1)),
                pl.BlockSpec((batch, BLOCK, KV_WIDTH), lambda i, s: (0, prev(i), kcol)),
                pl.BlockSpec((batch, BLOCK, KV_WIDTH), lambda i, s: (0, prev(i), kcol + 1)),
                pl.BlockSpec((ATTN_HEADS, BLOCK, 2 * BLOCK), lambda i, s: (0, 0, 0)),
            ],
            out_specs=pl.BlockSpec((batch, BLOCK, ATTN_WIDTH), lambda i, s: (0, i, 0)),
        ),
        out_shape=jax.ShapeDtypeStruct((batch, seq, ATTN_WIDTH), F32),
        compiler_params=_params(("arbitrary",)),
    )(sinks, qkv, qkv, qkv, qkv, qkv, bias)


def _memkv_kernel(m_ref, g_ref, w_ref, k_ref, v_ref):
    u = _rms(m_ref[...], g_ref[...])
    kv = _dot(u.astype(BF16), w_ref[...].astype(BF16))
    k_ref[...] = kv[:, :CROSS_WIDTH].astype(BF16)
    v_ref[...] = kv[:, CROSS_WIDTH:].astype(BF16)


def _memkv(mem2, g_mem, w_kv):
    n = mem2.shape[0]
    return pl.pallas_call(
        _memkv_kernel,
        name="memkv",
        out_shape=[jax.ShapeDtypeStruct((n, CROSS_WIDTH), BF16)] * 2,
        compiler_params=_params(),
    )(mem2, g_mem, w_kv)


def _mid_kernel(ys_ref, ya_ref, x_ref, wout_f32, bout_ref, gc_ref, wq_f32, k_ref, v_ref, wo_f32,
                gf_ref, wr_ref, br_ref, upper_ref, lower_ref,
                h2_ref, u3_ref, gate_ref, lp_ref, tcnt_ref, tbase_ref, run_ref, wout_ref, wq_ref, wo_ref):
    i = pl.program_id(0)

    @pl.when(i == 0)
    def _():
        run_ref[...] = jnp.zeros_like(run_ref)
        wout_ref[...] = wout_f32[...].astype(BF16)
        wq_ref[...] = wq_f32[...].astype(BF16)
        wo_ref[...] = wo_f32[...].astype(BF16)

    ycat = jnp.concatenate([ys_ref[...], ya_ref[...]], axis=1).astype(BF16)
    h1 = x_ref[...] + _dot(ycat, wout_ref[...]) + bout_ref[...]

    u2 = _rms(h1, gc_ref[...])
    q = _dot(u2.astype(BF16), wq_ref[...]).astype(BF16)
    outs = []
    for h in range(CROSS_HEADS):
        hs = slice(h * CROSS_HEAD_DIM, (h + 1) * CROSS_HEAD_DIM)
        s = _nt(q[:, hs], k_ref[:, hs]) * (CROSS_HEAD_DIM ** -0.5)
        m = jnp.max(s, axis=-1, keepdims=True)
        p = jnp.exp(s - m)
        outs.append(_dot(p.astype(BF16), v_ref[:, hs]) / jnp.sum(p, axis=-1, keepdims=True))
    o = jnp.concatenate(outs, axis=1).astype(BF16)
    h2 = h1 + _dot(o, wo_ref[...])
    h2_ref[...] = h2
    u3 = _rms(h2, gf_ref[...]).astype(BF16)
    u3_ref[...] = u3

    logits = _nt(wr_ref[...], u3) + br_ref[...]
    eid = lax.broadcasted_iota(I32, logits.shape, 0)
    vals, idxs, hots = [], [], []
    for _ in range(TOP_K):
        m = jnp.max(logits, axis=0, keepdims=True)
        ix = jnp.min(jnp.where(logits == m, eid, N_EXPERTS), axis=0, keepdims=True)
        hot = eid == ix
        logits = jnp.where(hot, -jnp.inf, logits)
        vals.append(m)
        idxs.append(ix)
        hots.append(hot)
    ex = [jnp.exp(v - vals[0]) for v in vals]
    tot = ex[0] + ex[1] + ex[2] + ex[3]
    gate_ref[...] = jnp.concatenate([e / tot for e in ex], axis=0)

    hot_f = [jnp.where(hot, 1.0, 0.0) for hot in hots]
    cnt_k = [jnp.sum(hf, axis=1, keepdims=True) for hf in hot_f]
    tile_cnt = cnt_k[0] + cnt_k[1] + cnt_k[2] + cnt_k[3]
    tile_cnt = jnp.floor((tile_cnt + (SUBLANES - 1)) * (1.0 / SUBLANES)) * SUBLANES
    units = jnp.broadcast_to(tile_cnt * (1.0 / SUBLANES), (N_EXPERTS, LANES)).astype(BF16)
    offs = _dot(lower_ref[...], units)[:, :1] * SUBLANES
    local = []
    before_all = _dot(jnp.concatenate(hot_f, axis=0).astype(BF16), upper_ref[...])
    for kk in range(TOP_K):
        before = before_all[kk * N_EXPERTS:(kk + 1) * N_EXPERTS, :]
        local.append(jnp.sum(jnp.where(hots[kk], before + offs, 0.0), axis=0, keepdims=True))
        offs = offs + cnt_k[kk]
    lp_ref[...] = jnp.concatenate(local, axis=0).astype(I32)
    tcnt_ref[0] = jnp.broadcast_to(tile_cnt, (N_EXPERTS, LANES)).astype(I32)
    tbase_ref[0] = jnp.broadcast_to(run_ref[...], (N_EXPERTS, LANES)).astype(I32)
    run_ref[...] = run_ref[...] + tile_cnt


def _mid(y_ssm, y_attn, x2, w_out, b_out, g_cross, w_q, k_mem, v_mem, w_o, g_ffn, w_r_t, b_r, seq, mem_len):
    n = x2.shape[0]
    tiles_per_batch = seq // TM_MID
    row = lambda i: (i, 0)
    col = lambda i: (0, i)
    fixed = lambda i: (0, 0)
    memb = lambda i: (i // tiles_per_batch, 0)
    tile3 = lambda i: (i, 0, 0)
    n_tiles = n // TM_MID
    upper = jnp.asarray(np.triu(np.ones((TM_MID, TM_MID), np.float32), 1)).astype(BF16)
    lower = jnp.asarray(np.tril(np.ones((N_EXPERTS, N_EXPERTS), np.float32), -1)).astype(BF16)
    return pl.pallas_call(
        _mid_kernel,
        name="mid",
        grid=(n_tiles,),
        in_specs=[
            pl.BlockSpec((TM_MID, SSM_WIDTH), row),
            pl.BlockSpec((TM_MID, ATTN_WIDTH), row),
            pl.BlockSpec((TM_MID, D_MODEL), row),
            pl.BlockSpec((D_MODEL, D_MODEL), fixed),
            pl.BlockSpec((1, D_MODEL), fixed),
            pl.BlockSpec((1, D_MODEL), fixed),
            pl.BlockSpec((D_MODEL, CROSS_WIDTH), fixed),
            pl.BlockSpec((mem_len, CROSS_WIDTH), memb),
            pl.BlockSpec((mem_len, CROSS_WIDTH), memb),
            pl.BlockSpec((CROSS_WIDTH, D_MODEL), fixed),
            pl.BlockSpec((1, D_MODEL), fixed),
            pl.BlockSpec((N_EXPERTS, D_MODEL), fixed),
            pl.BlockSpec((N_EXPERTS, 1), fixed),
            pl.BlockSpec((TM_MID, TM_MID), fixed),
            pl.BlockSpec((N_EXPERTS, N_EXPERTS), fixed),
        ],
        out_specs=[
            pl.BlockSpec((TM_MID, D_MODEL), row),
            pl.BlockSpec((TM_MID, D_MODEL), row),
            pl.BlockSpec((TOP_K, TM_MID), col),
            pl.BlockSpec((TOP_K, TM_MID), col),
            pl.BlockSpec((1, N_EXPERTS, LANES), tile3),
            pl.BlockSpec((1, N_EXPERTS, LANES), tile3),
        ],
        out_shape=[
            jax.ShapeDtypeStruct((n, D_MODEL), F32),
            jax.ShapeDtypeStruct((n, D_MODEL), BF16),
            jax.ShapeDtypeStruct((TOP_K, n), F32),
            jax.ShapeDtypeStruct((TOP_K, n), I32),
            jax.ShapeDtypeStruct((n_tiles, N_EXPERTS, LANES), I32),
            jax.ShapeDtypeStruct((n_tiles, N_EXPERTS, LANES), I32),
        ],
        scratch_shapes=[
            pltpu.VMEM((N_EXPERTS, 1), F32),
            pltpu.VMEM((D_MODEL, D_MODEL), BF16),
            pltpu.VMEM((D_MODEL, CROSS_WIDTH), BF16),
            pltpu.VMEM((CROSS_WIDTH, D_MODEL), BF16),
        ],
        compiler_params=_params(("arbitrary",)),
    )(y_ssm, y_attn, x2, w_out, b_out, g_cross, w_q, k_mem, v_mem, w_o, g_ffn, w_r_t, b_r, upper, lower)


ROW_CHUNK = 256
LOCAL_ROWS = -(-(TM_MID * TOP_K + N_EXPERTS * (SUBLANES - 1)) // ROW_CHUNK) * ROW_CHUNK


U32 = jnp.uint32
PACKED = D_MODEL // 2


def _pack_pairs(v):
    bits = lax.bitcast_convert_type(v, U32)
    return bits[:, PACKED:] | (bits[:, :PACKED] >> 16)


def _unpack_pairs(w):
    lo = lax.bitcast_convert_type(w << 16, F32)
    hi = lax.bitcast_convert_type(w & jnp.uint32(0xFFFF0000), F32)
    return jnp.concatenate([lo, hi], axis=1).astype(BF16)


def _row_copy(n, src_ref, src, dst_ref, dst, sem):
    aligned = lambda v: v if isinstance(v, int) else pl.multiple_of(v, SUBLANES)
    return pltpu.make_async_copy(src_ref.at[pl.ds(aligned(src), aligned(n))],
                                 dst_ref.at[pl.ds(aligned(dst), aligned(n))], sem)


def _start_run(n, src_ref, src, dst_ref, dst, sem):
    @pl.when(n > 0)
    def _():
        _row_copy(n, src_ref, src, dst_ref, dst, sem).start()


def _wait_rows(n, src_ref, dst_ref, sem):
    _row_copy(n, src_ref, 0, dst_ref, 0, sem).wait()


def _dispatch_kernel(tcnt_ref, lo_ref, gs_ref, padn_ref, pads_ref, na_ref, u_ref, lp_ref, xs_ref,
                     xloc_ref, zero_ref, sem, zsem):
    i = pl.program_id(0)

    def zero_fill(wait):
        def fill(n, dst):
            cp = _row_copy(n, zero_ref, 0, xs_ref, dst, zsem)
            if wait:
                cp.wait()
            else:
                cp.start()

        def body(e, carry):
            pl.when(padn_ref[e] > 0)(lambda: fill(padn_ref[e], pads_ref[e]))
            return carry
        lax.fori_loop(0, N_EXPERTS, body, 0)

        def tail(b, carry):
            fill(MOE_ROWS, b * MOE_ROWS)
            return carry
        lax.fori_loop(na_ref[0], xs_ref.shape[0] // MOE_ROWS, tail, 0)

    @pl.when(i == 0)
    def _():
        zero_ref[...] = jnp.zeros_like(zero_ref)
        zero_fill(False)

    slot = i % 2
    u = u_ref[...]
    lp = lp_ref[...]
    for c in range(LOCAL_ROWS // ROW_CHUNK):
        r = lax.broadcasted_iota(I32, (ROW_CHUNK, TM_MID), 0) + c * ROW_CHUNK
        p = jnp.zeros((ROW_CHUNK, TM_MID), F32)
        for kk in range(TOP_K):
            p = jnp.where(lp[kk:kk + 1, :] == r, 1.0, p)
        xloc_ref[slot, c * ROW_CHUNK:(c + 1) * ROW_CHUNK, :] = _pack_pairs(_dot(p.astype(BF16), u))

    def start_runs(tile):
        s = tile % 2

        def body(e, carry):
            j = tile * N_EXPERTS + e
            _start_run(tcnt_ref[j], xloc_ref.at[s], lo_ref[j], xs_ref, gs_ref[j], sem.at[s])
            return carry
        lax.fori_loop(0, N_EXPERTS, body, 0)

    def wait_runs(tile):
        s = tile % 2
        last = tile * N_EXPERTS + N_EXPERTS - 1
        _wait_rows(lo_ref[last] + tcnt_ref[last], xloc_ref.at[s], xs_ref, sem.at[s])

    start_runs(i)

    @pl.when(i > 0)
    def _():
        wait_runs(i - 1)

    @pl.when(i == pl.num_programs(0) - 1)
    def _():
        wait_runs(i)
        zero_fill(True)


def _dispatch(tcnt, lo, gstart, padn, pads, n_active, u3, lp, n_rows):
    n = u3.shape[0]
    return pl.pallas_call(
        _dispatch_kernel,
        name="dispatch",
        grid_spec=pltpu.PrefetchScalarGridSpec(
            num_scalar_prefetch=6,
            grid=(n // TM_MID,),
            in_specs=[
                pl.BlockSpec((TM_MID, D_MODEL), lambda i, *_: (i, 0)),
                pl.BlockSpec((TOP_K, TM_MID), lambda i, *_: (0, i)),
            ],
            out_specs=pl.BlockSpec(memory_space=pl.ANY),
            scratch_shapes=[
                pltpu.VMEM((2, LOCAL_ROWS, PACKED), U32),
                pltpu.VMEM((MOE_ROWS, PACKED), U32),
                pltpu.SemaphoreType.DMA((2,)),
                pltpu.SemaphoreType.DMA,
            ],
        ),
        out_shape=jax.ShapeDtypeStruct((n_rows, PACKED), U32),
        compiler_params=_params(("arbitrary",)),
    )(tcnt, lo, gstart, padn, pads, n_active, u3, lp)


PAIR_COLS = 2 * LANES


def _expert_kernel(be_ref, nx_ref, nv_ref, na_ref, x_ref, b1_ref, b2_ref, perm_ref, w1_hbm, w2_hbm, y_ref,
                   w1s_ref, w2s_ref, w1p_ref, w2b_ref, sem):
    b = pl.program_id(0)
    valid_rows = nv_ref[b]
    active = valid_rows > 0
    expert = be_ref[b]
    new_expert = jnp.logical_or(b == 0, expert != be_ref[jnp.maximum(b - 1, 0)])

    def fetch(e, wait):
        for src, dst, s in ((w1_hbm, w1s_ref, 0), (w2_hbm, w2s_ref, 1)):
            cp = pltpu.make_async_copy(src.at[e], dst, sem.at[s])
            if wait:
                cp.wait()
            else:
                cp.start(priority=1)

    @pl.when(jnp.logical_and(b == 0, active))
    def _():
        fetch(expert, False)

    @pl.when(jnp.logical_and(active, new_expert))
    def _():
        fetch(expert, True)
        for c in range(2 * D_FF // PAIR_COLS):
            cs = slice(c * PAIR_COLS, (c + 1) * PAIR_COLS)
            w1p_ref[:, cs] = _dot(w1s_ref[:, cs].astype(BF16), perm_ref[...]).astype(BF16)
        w2b_ref[...] = w2s_ref[...].astype(BF16)

        @pl.when(nx_ref[b] >= 0)
        def _():
            fetch(nx_ref[b], False)

    for live in range(MOE_PART, MOE_ROWS + 1, MOE_PART):
        @pl.when(jnp.logical_and(valid_rows > live - MOE_PART, valid_rows <= live))
        def _():
            x = _unpack_pairs(x_ref[:live, :])
            hdn = _dot(x, w1p_ref[...]) + b1_ref[0]
            acts = []
            for c in range(2 * D_FF // PAIR_COLS):
                hg = jnp.minimum(hdn[:, c * PAIR_COLS:c * PAIR_COLS + LANES], SWIGLU_LIMIT)
                hl = jnp.clip(hdn[:, c * PAIR_COLS + LANES:(c + 1) * PAIR_COLS], -SWIGLU_LIMIT, SWIGLU_LIMIT)
                acts.append((hg * _sigmoid(SWIGLU_ALPHA * hg) * (hl + 1.0)).astype(BF16))
            y = _dot(jnp.concatenate(acts, axis=1), w2b_ref[...]) + b2_ref[0]
            y_ref[:live, :] = _pack_pairs(y.astype(BF16).astype(F32))
            if live < MOE_ROWS:
                y_ref[live:, :] = jnp.zeros((MOE_ROWS - live, PACKED), U32)

    @pl.when(jnp.logical_not(active))
    def _():
        y_ref[...] = jnp.zeros_like(y_ref)


def _experts(block_exp, next_exp, valid_rows, n_active, xs, w1, b1p, w2, b2):
    n_rows = xs.shape[0]
    n_blocks = n_rows // MOE_ROWS
    xmap = lambda b, be, nx, nv, na: (jnp.maximum(jnp.minimum(b, na[0] - 1), 0), 0)
    emap = lambda b, be, nx, nv, na: (be[b], 0, 0)
    perm = np.zeros((PAIR_COLS, PAIR_COLS), np.float32)
    perm[2 * np.arange(LANES), np.arange(LANES)] = 1.0
    perm[2 * np.arange(LANES) + 1, LANES + np.arange(LANES)] = 1.0
    return pl.pallas_call(
        _expert_kernel,
        name="experts",
        grid_spec=pltpu.PrefetchScalarGridSpec(
            num_scalar_prefetch=4,
            grid=(n_blocks,),
            in_specs=[
                pl.BlockSpec((MOE_ROWS, PACKED), xmap),
                pl.BlockSpec((1, 1, 2 * D_FF), emap),
                pl.BlockSpec((1, 1, D_MODEL), emap),
                pl.BlockSpec((PAIR_COLS, PAIR_COLS), lambda b, *_: (0, 0)),
                pl.BlockSpec(memory_space=pl.ANY),
                pl.BlockSpec(memory_space=pl.ANY),
            ],
            out_specs=pl.BlockSpec((MOE_ROWS, PACKED), lambda b, *_: (b, 0)),
            scratch_shapes=[
                pltpu.VMEM((D_MODEL, 2 * D_FF), F32),
                pltpu.VMEM((D_FF, D_MODEL), F32),
                pltpu.VMEM((D_MODEL, 2 * D_FF), BF16),
                pltpu.VMEM((D_FF, D_MODEL), BF16),
                pltpu.SemaphoreType.DMA((2,)),
            ],
        ),
        out_shape=jax.ShapeDtypeStruct((n_rows, PACKED), U32),
        compiler_params=_params(("arbitrary",)),
    )(block_exp, next_exp, valid_rows, n_active, xs, b1p, b2, jnp.asarray(perm).astype(BF16), w1, w2)


def _combine_kernel(tcnt_ref, lo_ref, gs_ref, h2_ref, gate_ref, lp_ref, g_ref, ys_ref, o_ref, yloc_ref, sem):
    i = pl.program_id(0)

    slot = i % 2

    def start_runs(tile):
        s = tile % 2

        def body(e, carry):
            j = tile * N_EXPERTS + e
            _start_run(tcnt_ref[j], ys_ref, gs_ref[j], yloc_ref.at[s], lo_ref[j], sem.at[s])
            return carry
        lax.fori_loop(0, N_EXPERTS, body, 0)

    @pl.when(i == 0)
    def _():
        yloc_ref[...] = jnp.zeros_like(yloc_ref)
        start_runs(i)

    @pl.when(i + 1 < pl.num_programs(0))
    def _():
        start_runs(i + 1)

    gates = gate_ref[...]
    lp = lp_ref[...]
    last = i * N_EXPERTS + N_EXPERTS - 1
    _wait_rows(lo_ref[last] + tcnt_ref[last], ys_ref, yloc_ref.at[slot], sem.at[slot])

    h = h2_ref[...]
    for c in range(LOCAL_ROWS // ROW_CHUNK):
        r = lax.broadcasted_iota(I32, (ROW_CHUNK, TM_MID), 0) + c * ROW_CHUNK
        a = jnp.zeros((ROW_CHUNK, TM_MID), F32)
        for kk in range(TOP_K):
            a = jnp.where(lp[kk:kk + 1, :] == r, gates[kk:kk + 1, :], a)
        h = h + _tn(a.astype(BF16), _unpack_pairs(yloc_ref[slot, c * ROW_CHUNK:(c + 1) * ROW_CHUNK, :]))
    o_ref[...] = _rms(h, g_ref[...])


def _combine(tcnt, lo, gstart, h2, gates, lp, g_final, ys):
    n = h2.shape[0]
    return pl.pallas_call(
        _combine_kernel,
        name="combine",
        grid_spec=pltpu.PrefetchScalarGridSpec(
            num_scalar_prefetch=3,
            grid=(n // TM_MID,),
            in_specs=[
                pl.BlockSpec((TM_MID, D_MODEL), lambda i, *_: (i, 0)),
                pl.BlockSpec((TOP_K, TM_MID), lambda i, *_: (0, i)),
                pl.BlockSpec((TOP_K, TM_MID), lambda i, *_: (0, i)),
                pl.BlockSpec((1, D_MODEL), lambda i, *_: (0, 0)),
                pl.BlockSpec(memory_space=pl.ANY),
            ],
            out_specs=pl.BlockSpec((TM_MID, D_MODEL), lambda i, *_: (i, 0)),
            scratch_shapes=[
                pltpu.VMEM((2, LOCAL_ROWS, PACKED), U32),
                pltpu.SemaphoreType.DMA((2,)),
            ],
        ),
        out_shape=jax.ShapeDtypeStruct((n, D_MODEL), F32),
        compiler_params=_params(("arbitrary",)),
    )(tcnt, lo, gstart, h2, gates, lp, g_final, ys)


def _layer(h, mem2, batch, seq, mem_len, g_mix, w_in, b_qkv, conv_w, conv_b, dt_bias, a_log, d_skip,
           g_ssm_out, attn_sinks, bias, w_out, b_out, g_cross, g_mem, w_q_cross, w_kv_cross, w_o_cross,
           g_ffn, w_router, b_router, w1, b1, w2, b2, g_final):
    n = h.shape[0]
    z, xbc, dtp, qkv = _inproj(h, g_mix[None, :], w_in, b_qkv[None, :])

    pad_h = (0, DT_PAD - SSM_HEADS)
    dt_bias_p = jnp.pad(dt_bias, pad_h)[None, :]
    a_head_p = jnp.pad(-jnp.exp(a_log), pad_h)[None, :]
    d_full = jnp.repeat(d_skip, SSM_HEAD_DIM)[None, :]
    per_seq = lambda t: t.reshape(batch, seq, t.shape[-1])
    y_ssm = _ssd(per_seq(xbc), per_seq(z), per_seq(dtp), conv_w, conv_b[None, :], dt_bias_p, a_head_p,
                 d_full, g_ssm_out[None, :]).reshape(n, SSM_WIDTH)
    y_attn = _swa(per_seq(qkv), attn_sinks, bias).reshape(n, ATTN_WIDTH)

    k_mem, v_mem = _memkv(mem2, g_mem[None, :], w_kv_cross)
    h2, u3, gates, lp, tcnt, tbase = _mid(
        y_ssm, y_attn, h, w_out, b_out[None, :], g_cross[None, :],
        w_q_cross, k_mem, v_mem, w_o_cross, g_ffn[None, :],
        w_router.T.astype(BF16), b_router[:, None], seq, mem_len)

    max_rows = n * TOP_K + (n // TM_MID) * N_EXPERTS * (SUBLANES - 1)
    n_blocks = -(-max_rows // MOE_ROWS) + N_EXPERTS
    tcnt = tcnt[:, :, 0]
    tbase = tbase[:, :, 0]
    counts = tbase[-1] + tcnt[-1]
    padded = (counts + MOE_ROWS - 1) // MOE_ROWS * MOE_ROWS
    padded_end = jnp.cumsum(padded)
    start = (padded_end - padded).astype(I32)
    n_active = (padded_end[-1] // MOE_ROWS).astype(I32)
    blk = jnp.arange(n_blocks, dtype=I32)
    blk_exp = jnp.sum((blk * MOE_ROWS)[:, None] >= padded_end[None, :], axis=1)
    eid = jnp.arange(N_EXPERTS, dtype=I32)
    last_exp = jnp.max(jnp.where(padded > 0, eid, 0))
    blk_exp = jnp.where(blk < n_active, jnp.minimum(blk_exp, N_EXPERTS - 1), last_exp).astype(I32)
    later = jnp.logical_and(padded[None, :] > 0, eid[None, :] > eid[:, None])
    next_of = jnp.min(jnp.where(later, eid[None, :], N_EXPERTS), axis=1)
    next_of = jnp.where(next_of < N_EXPERTS, next_of, -1).astype(I32)
    of_blk = blk_exp[:, None] == eid[None, :]
    blk_next = jnp.sum(jnp.where(of_blk, next_of[None, :], 0), axis=1).astype(I32)
    blk_end = jnp.sum(jnp.where(of_blk, (start + counts)[None, :], 0), axis=1)
    blk_valid = jnp.clip(blk_end - blk * MOE_ROWS, 0, MOE_ROWS)
    blk_valid = jnp.where(blk < n_active, blk_valid, 0).astype(I32)

    run_n = tcnt.reshape(-1)
    run_local = (jnp.cumsum(tcnt, axis=1) - tcnt).reshape(-1)
    run_global = (start[None, :] + tbase).reshape(-1)
    xs = _dispatch(run_n, run_local, run_global, padded - counts, start + counts, n_active[None], u3, lp,
                   n_blocks * MOE_ROWS)

    b1p = b1.reshape(N_EXPERTS, 2 * D_FF // PAIR_COLS, LANES, 2).transpose(0, 1, 3, 2)
    ys = _experts(blk_exp, blk_next, blk_valid, n_active[None], xs, w1,
                  b1p.reshape(N_EXPERTS, 1, 2 * D_FF), w2, b2[:, None, :])
    return _combine(run_n, run_local, run_global, h2, gates, lp, g_final, ys)


def kernel(x, mem, g_mix, w_in, b_qkv, conv_w, conv_b, dt_bias, a_log, d_skip, g_ssm_out, attn_sinks, rel_bias, w_out, b_out, g_cross, g_mem, w_q_cross, w_kv_cross, w_o_cross, g_ffn, w_router, b_router, w1, b1, w2, b2, g_final):
    batch, seq, d = x.shape
    mem_len = mem.shape[1]
    depth = g_mix.shape[0]
    assert depth == 1 and d == D_MODEL and seq % TM_MID == 0
    h = x.reshape(batch * seq, d)
    mem2 = mem.reshape(batch * mem_len, d)
    bias = _bias_table(rel_bias)
    out = _layer(
        h, mem2, batch, seq, mem_len, g_mix[0], w_in[0], b_qkv[0], conv_w[0], conv_b[0], dt_bias[0],
        a_log[0], d_skip[0], g_ssm_out[0], attn_sinks[0], bias, w_out[0], b_out[0], g_cross[0],
        g_mem[0], w_q_cross[0], w_kv_cross[0], w_o_cross[0], g_ffn[0], w_router[0], b_router[0],
        w1[0], b1[0], w2[0], b2[0], g_final[None, :])
    return out.reshape(batch, seq, d)
```

```python
import math

import numpy as np
import jax
import jax.numpy as jnp
from jax import lax
from jax.experimental import pallas as pl
from jax.experimental.pallas import tpu as pltpu

F32 = jnp.float32
BF16 = jnp.bfloat16
I32 = jnp.int32

D_MODEL = 1024
RMS_EPS = 1e-5

SSM_HEADS = 8
SSM_HEAD_DIM = 64
SSM_WIDTH = SSM_HEADS * SSM_HEAD_DIM
SSM_STATE = 128
SSM_GROUPS = 2
HEADS_PER_GROUP = SSM_HEADS // SSM_GROUPS
GROUP_WIDTH = SSM_WIDTH // SSM_GROUPS
CONV_WIDTH = 4
CHUNK = 128
CONV_CH = SSM_WIDTH + 2 * SSM_GROUPS * SSM_STATE

ATTN_HEADS = 8
KV_HEADS = 2
Q_PER_KV = ATTN_HEADS // KV_HEADS
HEAD_DIM = 64
ATTN_WIDTH = ATTN_HEADS * HEAD_DIM
KV_WIDTH = KV_HEADS * HEAD_DIM
WINDOW = 128
BLOCK = WINDOW
QKV_COLS = ATTN_WIDTH + 2 * KV_WIDTH

REL_BUCKETS = 32
REL_MAX_DIST = 128

CROSS_HEADS = 4
CROSS_HEAD_DIM = 128
CROSS_WIDTH = CROSS_HEADS * CROSS_HEAD_DIM

N_EXPERTS = 32
TOP_K = 4
D_FF = D_MODEL
SWIGLU_ALPHA = 1.702
SWIGLU_LIMIT = 7.0

LANES = 128
SUBLANES = 8
DT_PAD = LANES
PROJ_COLS = SSM_WIDTH + CONV_CH + DT_PAD + QKV_COLS
VMEM_LIMIT = 56 * 1024 * 1024

TM_PROJ = 1024
TM_MID = 512
SSD_ROWS = 4 * CHUNK
MOE_ROWS = 1024
MOE_PART = 128


def _nt(a, b):
    return lax.dot_general(a, b, (((1,), (1,)), ((), ())), preferred_element_type=F32)


def _tn(a, b):
    return lax.dot_general(a, b, (((0,), (0,)), ((), ())), preferred_element_type=F32)


def _dot(a, b):
    return jnp.dot(a, b, preferred_element_type=F32)


def _rms(x, g):
    return x * lax.rsqrt(jnp.mean(x * x, axis=-1, keepdims=True) + RMS_EPS) * g


def _sigmoid(x):
    return 0.5 * jnp.tanh(0.5 * x) + 0.5


def _params(sem=None, **kwargs):
    return pltpu.CompilerParams(dimension_semantics=sem, vmem_limit_bytes=VMEM_LIMIT, **kwargs)


def _inproj_kernel(x_ref, g_ref, w_f32, bq_ref, z_ref, xbc_ref, dt_ref, qkv_ref, w_ref):
    @pl.when(pl.program_id(0) == 0)
    def _():
        c1 = SSM_WIDTH + CONV_CH
        w_ref[:, :c1] = w_f32[:, :c1].astype(BF16)
        lane = lax.broadcasted_iota(I32, (D_MODEL, DT_PAD), 1)
        w_ref[:, c1:c1 + DT_PAD] = jnp.where(lane < SSM_HEADS, w_f32[:, c1:c1 + DT_PAD], 0.0).astype(BF16)
        w_ref[:, c1 + DT_PAD:] = w_f32[:, c1 + SSM_HEADS:].astype(BF16)

    u = _rms(x_ref[...], g_ref[...])
    p = _dot(u.astype(BF16), w_ref[...])
    z_ref[...] = p[:, :SSM_WIDTH]
    xbc_ref[...] = p[:, SSM_WIDTH:SSM_WIDTH + CONV_CH]
    dt_ref[...] = p[:, SSM_WIDTH + CONV_CH:SSM_WIDTH + CONV_CH + DT_PAD]
    qkv_ref[...] = p[:, SSM_WIDTH + CONV_CH + DT_PAD:] + bq_ref[...]


def _inproj(x2, g_mix, w_in, b_qkv):
    n = x2.shape[0]
    row = lambda i: (i, 0)
    fixed = lambda i: (0, 0)
    return pl.pallas_call(
        _inproj_kernel,
        name="inproj",
        grid=(n // TM_PROJ,),
        in_specs=[
            pl.BlockSpec((TM_PROJ, D_MODEL), row),
            pl.BlockSpec((1, D_MODEL), fixed),
            pl.BlockSpec(w_in.shape, fixed),
            pl.BlockSpec((1, QKV_COLS), fixed),
        ],
        out_specs=[
            pl.BlockSpec((TM_PROJ, SSM_WIDTH), row),
            pl.BlockSpec((TM_PROJ, CONV_CH), row),
            pl.BlockSpec((TM_PROJ, DT_PAD), row),
            pl.BlockSpec((TM_PROJ, QKV_COLS), row),
        ],
        out_shape=[
            jax.ShapeDtypeStruct((n, SSM_WIDTH), F32),
            jax.ShapeDtypeStruct((n, CONV_CH), F32),
            jax.ShapeDtypeStruct((n, DT_PAD), F32),
            jax.ShapeDtypeStruct((n, QKV_COLS), F32),
        ],
        scratch_shapes=[pltpu.VMEM((D_MODEL, PROJ_COLS), BF16)],
        compiler_params=_params(("arbitrary",)),
    )(x2, g_mix, w_in, b_qkv)


def _split3(x):
    hi = x.astype(BF16)
    rest = x - hi.astype(F32)
    mid = rest.astype(BF16)
    lo = (rest - mid.astype(F32)).astype(BF16)
    return jnp.concatenate([hi, mid, lo], axis=1)


def _ssd_kernel(xbc_ref, z_ref, dt_ref, cw_ref, cb_ref, dtb_ref, ah_ref, dsk_ref, gout_ref,
                e64_ref, ecat_ref, tri_ref, y_ref, conv_ref, state_ref):
    c = pl.program_id(0)

    @pl.when(c == 0)
    def _():
        conv_ref[:, :SUBLANES, :] = jnp.zeros((conv_ref.shape[0], SUBLANES, CONV_CH), F32)
        state_ref[...] = jnp.zeros_like(state_ref)

    for cc in range(xbc_ref.shape[1] // CHUNK):
        for b in range(xbc_ref.shape[0]):
            _ssd_chunk(b, slice(cc * CHUNK, (cc + 1) * CHUNK), xbc_ref, z_ref, dt_ref, cw_ref, cb_ref, dtb_ref,
                       ah_ref, dsk_ref, gout_ref, e64_ref, ecat_ref, tri_ref, y_ref, conv_ref, state_ref)


def _ssd_chunk(b, rows, xbc_ref, z_ref, dt_ref, cw_ref, cb_ref, dtb_ref, ah_ref, dsk_ref, gout_ref,
               e64_ref, ecat_ref, tri_ref, y_ref, conv_ref, state_ref):
    u = xbc_ref[b, rows, :]
    conv_ref[b, SUBLANES:, :] = u
    acc = u * cw_ref[CONV_WIDTH - 1:CONV_WIDTH, :] + cb_ref[...]
    for j in range(1, CONV_WIDTH):
        shifted = conv_ref[b, SUBLANES - j:SUBLANES - j + CHUNK, :]
        acc = acc + shifted * cw_ref[CONV_WIDTH - 1 - j:CONV_WIDTH - j, :]
    conv_ref[b, :SUBLANES, :] = u[CHUNK - SUBLANES:, :]
    xbc = acc * _sigmoid(acc)

    xs = xbc[:, :SSM_WIDTH]
    bm = xbc[:, SSM_WIDTH:SSM_WIDTH + SSM_GROUPS * SSM_STATE]
    cm = xbc[:, SSM_WIDTH + SSM_GROUPS * SSM_STATE:]

    dtr = dt_ref[b, rows, :] + dtb_ref[...]
    dt = jnp.maximum(dtr, 0.0) + jnp.log1p(jnp.exp(-jnp.abs(dtr)))
    a = dt * ah_ref[...]
    a3 = _split3(a)
    a_rows = jnp.concatenate([a3[:, :DT_PAD], a3[:, DT_PAD:2 * DT_PAD], a3[:, 2 * DT_PAD:]], axis=0)
    cs = _dot(tri_ref[...], a_rows)
    cs_row = cs.T
    dt_full = _dot(_split3(dt), e64_ref[...])
    cs_cat = _dot(_split3(cs), ecat_ref[...])
    cs_wide = cs_cat[:, :SSM_HEADS * CHUNK]
    cs_full = cs_cat[:, SSM_HEADS * CHUNK:]
    cs_last = cs_full[CHUNK - 1:CHUNK, :]

    xdt = xs * dt_full
    xdt_b = xdt.astype(BF16)
    xw_b = (xdt * jnp.exp(cs_last - cs_full)).astype(BF16)
    in_decay = jnp.exp(cs_full)
    chunk_decay = jnp.exp(cs_last)

    li = lax.broadcasted_iota(I32, (CHUNK, CHUNK), 0)
    si = lax.broadcasted_iota(I32, (CHUNK, CHUNK), 1)
    causal = li >= si

    ys = []
    for g in range(SSM_GROUPS):
        bg = bm[:, g * SSM_STATE:(g + 1) * SSM_STATE].astype(BF16)
        cg = cm[:, g * SSM_STATE:(g + 1) * SSM_STATE].astype(BF16)
        cb = _nt(cg, bg)
        yd = []
        for r in range(HEADS_PER_GROUP):
            h = g * HEADS_PER_GROUP + r
            diff = cs_wide[:, h * CHUNK:(h + 1) * CHUNK] - cs_row[h:h + 1, :]
            m = cb * jnp.exp(jnp.where(causal, diff, -jnp.inf))
            yd.append(_dot(m.astype(BF16), xdt_b[:, h * SSM_HEAD_DIM:(h + 1) * SSM_HEAD_DIM]))
        y_diag = jnp.concatenate(yd, axis=1)
        gs = slice(g * GROUP_WIDTH, (g + 1) * GROUP_WIDTH)
        st = state_ref[b, g]
        y_off = _dot(cg, st.astype(BF16)) * in_decay[:, gs]
        state_ref[b, g] = st * chunk_decay[:, gs] + _tn(bg, xw_b[:, gs])
        ys.append(y_diag + y_off)
    y = jnp.concatenate(ys, axis=1) + dsk_ref[...] * xs

    zz = z_ref[b, rows, :]
    y = y * (zz * _sigmoid(zz))
    outs = []
    for g in range(SSM_GROUPS):
        yg = y[:, g * GROUP_WIDTH:(g + 1) * GROUP_WIDTH]
        outs.append(yg * lax.rsqrt(jnp.mean(yg * yg, axis=-1, keepdims=True) + RMS_EPS))
    y_ref[b, rows, :] = jnp.concatenate(outs, axis=1) * gout_ref[...]


def _ssd(xbc, z, dtp, conv_w, conv_b, dt_bias_p, a_head_p, d_full, g_out):
    batch, seq, _ = xbc.shape
    chunk = lambda c: (0, c, 0)
    fixed = lambda c: (0, 0)
    head = np.arange(SSM_WIDTH) // SSM_HEAD_DIM
    e64 = (np.arange(DT_PAD)[:, None] == head[None, :]).astype(np.float32)
    head_w = np.arange(SSM_HEADS * CHUNK) // CHUNK
    e128 = (np.arange(DT_PAD)[:, None] == head_w[None, :]).astype(np.float32)
    e64_3 = jnp.asarray(np.tile(e64, (3, 1))).astype(BF16)
    ecat_3 = jnp.asarray(np.tile(np.concatenate([e128, e64], axis=1), (3, 1))).astype(BF16)
    tri_3 = jnp.asarray(np.tile(np.tril(np.ones((CHUNK, CHUNK), np.float32)), (1, 3))).astype(BF16)
    return pl.pallas_call(
        _ssd_kernel,
        name="ssd",
        grid=(seq // SSD_ROWS,),
        in_specs=[
            pl.BlockSpec((batch, SSD_ROWS, CONV_CH), chunk),
            pl.BlockSpec((batch, SSD_ROWS, SSM_WIDTH), chunk),
            pl.BlockSpec((batch, SSD_ROWS, DT_PAD), chunk),
            pl.BlockSpec((CONV_WIDTH, CONV_CH), fixed),
            pl.BlockSpec((1, CONV_CH), fixed),
            pl.BlockSpec((1, DT_PAD), fixed),
            pl.BlockSpec((1, DT_PAD), fixed),
            pl.BlockSpec((1, SSM_WIDTH), fixed),
            pl.BlockSpec((1, SSM_WIDTH), fixed),
            pl.BlockSpec((3 * DT_PAD, SSM_WIDTH), fixed),
            pl.BlockSpec((3 * DT_PAD, SSM_HEADS * CHUNK + SSM_WIDTH), fixed),
            pl.BlockSpec((CHUNK, 3 * CHUNK), fixed),
        ],
        out_specs=pl.BlockSpec((batch, SSD_ROWS, SSM_WIDTH), chunk),
        out_shape=jax.ShapeDtypeStruct((batch, seq, SSM_WIDTH), F32),
        scratch_shapes=[
            pltpu.VMEM((batch, SUBLANES + CHUNK, CONV_CH), F32),
            pltpu.VMEM((batch, SSM_GROUPS, SSM_STATE, GROUP_WIDTH), F32),
        ],
        compiler_params=_params(("arbitrary",)),
    )(xbc, z, dtp, conv_w, conv_b, dt_bias_p, a_head_p, d_full, g_out, e64_3, ecat_3, tri_3)


def _t5_bucket_table():
    q_pos = np.arange(BLOCK)[:, None]
    k_pos = np.arange(2 * BLOCK)[None, :] - BLOCK
    dist = q_pos - k_pos
    d = np.maximum(dist, 0)
    max_exact = REL_BUCKETS // 2
    ratio = np.maximum(d, max_exact).astype(np.float32) / np.float32(max_exact)
    large = max_exact + (np.log(ratio) / np.float32(math.log(REL_MAX_DIST / max_exact))
                         * np.float32(REL_BUCKETS - max_exact)).astype(np.int32)
    large = np.minimum(large, REL_BUCKETS - 1)
    bucket = np.where(d < max_exact, d, large)
    in_window = (dist >= 0) & (dist < WINDOW)
    return np.where(in_window, bucket, -1).astype(np.int32)


def _bias_kernel(rb_ref, bucket_ref, out_ref):
    bucket = bucket_ref[...]
    for h in range(ATTN_HEADS):
        acc = jnp.zeros(bucket.shape, F32)
        for b in range(REL_BUCKETS):
            acc = jnp.where(bucket == b, rb_ref[b, h], acc)
        out_ref[h] = jnp.where(bucket >= 0, acc, -jnp.inf)


def _bias_table(rel_bias):
    bucket = jnp.asarray(_t5_bucket_table())
    return pl.pallas_call(
        _bias_kernel,
        name="bias_table",
        in_specs=[
            pl.BlockSpec(memory_space=pltpu.SMEM),
            pl.BlockSpec(memory_space=pltpu.VMEM),
        ],
        out_specs=pl.BlockSpec(memory_space=pltpu.VMEM),
        out_shape=jax.ShapeDtypeStruct((ATTN_HEADS, BLOCK, 2 * BLOCK), F32),
    )(rel_bias, bucket)


def _swa_kernel(sink_ref, q_ref, k_ref, v_ref, kp_ref, vp_ref, bias_ref, o_ref):
    i = pl.program_id(0)
    col = lax.broadcasted_iota(I32, (BLOCK, 2 * BLOCK), 1)
    key_ok = jnp.logical_or(col >= BLOCK, i > 0)
    for b in range(q_ref.shape[0]):
        q = (q_ref[b] * (HEAD_DIM ** -0.5)).astype(BF16)
        outs = []
        for g in range(KV_HEADS):
            gs = slice(g * HEAD_DIM, (g + 1) * HEAD_DIM)
            k2 = jnp.concatenate([kp_ref[b, :, gs], k_ref[b, :, gs]], axis=0).astype(BF16)
            v2 = jnp.concatenate([vp_ref[b, :, gs], v_ref[b, :, gs]], axis=0).astype(BF16)
            for r in range(Q_PER_KV):
                h = g * Q_PER_KV + r
                s = _nt(q[:, h * HEAD_DIM:(h + 1) * HEAD_DIM], k2) + bias_ref[h]
                s = jnp.where(key_ok, s, -jnp.inf)
                sink = sink_ref[h]
                m = jnp.maximum(jnp.max(s, axis=-1, keepdims=True), sink)
                p = jnp.exp(s - m)
                denom = jnp.sum(p, axis=-1, keepdims=True) + jnp.exp(sink - m)
                outs.append(_dot(p.astype(BF16), v2) / denom)
        o_ref[b] = jnp.concatenate(outs, axis=1)


def _swa(qkv, sinks, bias):
    batch, seq, _ = qkv.shape
    kcol = ATTN_WIDTH // KV_WIDTH
    prev = lambda i: jnp.maximum(i - 1, 0)
    return pl.pallas_call(
        _swa_kernel,
        name="swa",
        grid_spec=pltpu.PrefetchScalarGridSpec(
            num_scalar_prefetch=1,
            grid=(seq // BLOCK,),
            in_specs=[
                pl.BlockSpec((batch, BLOCK, ATTN_WIDTH), lambda i, s: (0, i, 0)),
                pl.BlockSpec((batch, BLOCK, KV_WIDTH), lambda i, s: (0, i, kcol)),
                pl.BlockSpec((batch, BLOCK, KV_WIDTH), lambda i, s: (0, i, kcol + 1)),
                pl.BlockSpec((batch, BLOCK, KV_WIDTH), lambda i, s: (0, prev(i), kcol)),
                pl.BlockSpec((batch, BLOCK, KV_WIDTH), lambda i, s: (0, prev(i), kcol + 1)),
                pl.BlockSpec((ATTN_HEADS, BLOCK, 2 * BLOCK), lambda i, s: (0, 0, 0)),
            ],
            out_specs=pl.BlockSpec((batch, BLOCK, ATTN_WIDTH), lambda i, s: (0, i, 0)),
        ),
        out_shape=jax.ShapeDtypeStruct((batch, seq, ATTN_WIDTH), F32),
        compiler_params=_params(("arbitrary",)),
    )(sinks, qkv, qkv, qkv, qkv, qkv, bias)


def _memkv_kernel(m_ref, g_ref, w_ref, k_ref, v_ref):
    u = _rms(m_ref[...], g_ref[...])
    kv = _dot(u.astype(BF16), w_ref[...].astype(BF16))
    k_ref[...] = kv[:, :CROSS_WIDTH].astype(BF16)
    v_ref[...] = kv[:, CROSS_WIDTH:].astype(BF16)


def _memkv(mem2, g_mem, w_kv):
    n = mem2.shape[0]
    return pl.pallas_call(
        _memkv_kernel,
        name="memkv",
        out_shape=[jax.ShapeDtypeStruct((n, CROSS_WIDTH), BF16)] * 2,
        compiler_params=_params(),
    )(mem2, g_mem, w_kv)


def _mid_kernel(ys_ref, ya_ref, x_ref, wout_f32, bout_ref, gc_ref, wq_f32, k_ref, v_ref, wo_f32,
                gf_ref, wr_ref, br_ref, upper_ref, lower_ref,
                h2_ref, u3_ref, gate_ref, lp_ref, tcnt_ref, tbase_ref, run_ref, wout_ref, wq_ref, wo_ref):
    i = pl.program_id(0)

    @pl.when(i == 0)
    def _():
        run_ref[...] = jnp.zeros_like(run_ref)
        wout_ref[...] = wout_f32[...].astype(BF16)
        wq_ref[...] = wq_f32[...].astype(BF16)
        wo_ref[...] = wo_f32[...].astype(BF16)

    ycat = jnp.concatenate([ys_ref[...], ya_ref[...]], axis=1).astype(BF16)
    h1 = x_ref[...] + _dot(ycat, wout_ref[...]) + bout_ref[...]

    u2 = _rms(h1, gc_ref[...])
    q = _dot(u2.astype(BF16), wq_ref[...]).astype(BF16)
    outs = []
    for h in range(CROSS_HEADS):
        hs = slice(h * CROSS_HEAD_DIM, (h + 1) * CROSS_HEAD_DIM)
        s = _nt(q[:, hs], k_ref[:, hs]) * (CROSS_HEAD_DIM ** -0.5)
        m = jnp.max(s, axis=-1, keepdims=True)
        p = jnp.exp(s - m)
        outs.append(_dot(p.astype(BF16), v_ref[:, hs]) / jnp.sum(p, axis=-1, keepdims=True))
    o = jnp.concatenate(outs, axis=1).astype(BF16)
    h2 = h1 + _dot(o, wo_ref[...])
    h2_ref[...] = h2
    u3 = _rms(h2, gf_ref[...]).astype(BF16)
    u3_ref[...] = u3

    logits = _nt(wr_ref[...], u3) + br_ref[...]
    eid = lax.broadcasted_iota(I32, logits.shape, 0)
    vals, idxs, hots = [], [], []
    for _ in range(TOP_K):
        m = jnp.max(logits, axis=0, keepdims=True)
        ix = jnp.min(jnp.where(logits == m, eid, N_EXPERTS), axis=0, keepdims=True)
        hot = eid == ix
        logits = jnp.where(hot, -jnp.inf, logits)
        vals.append(m)
        idxs.append(ix)
        hots.append(hot)
    ex = [jnp.exp(v - vals[0]) for v in vals]
    tot = ex[0] + ex[1] + ex[2] + ex[3]
    gate_ref[...] = jnp.concatenate([e / tot for e in ex], axis=0)

    hot_f = [jnp.where(hot, 1.0, 0.0) for hot in hots]
    cnt_k = [jnp.sum(hf, axis=1, keepdims=True) for hf in hot_f]
    tile_cnt = cnt_k[0] + cnt_k[1] + cnt_k[2] + cnt_k[3]
    tile_cnt = jnp.floor((tile_cnt + (SUBLANES - 1)) * (1.0 / SUBLANES)) * SUBLANES
    units = jnp.broadcast_to(tile_cnt * (1.0 / SUBLANES), (N_EXPERTS, LANES)).astype(BF16)
    offs = _dot(lower_ref[...], units)[:, :1] * SUBLANES
    local = []
    before_all = _dot(jnp.concatenate(hot_f, axis=0).astype(BF16), upper_ref[...])
    for kk in range(TOP_K):
        before = before_all[kk * N_EXPERTS:(kk + 1) * N_EXPERTS, :]
        local.append(jnp.sum(jnp.where(hots[kk], before + offs, 0.0), axis=0, keepdims=True))
        offs = offs + cnt_k[kk]
    lp_ref[...] = jnp.concatenate(local, axis=0).astype(I32)
    tcnt_ref[0] = jnp.broadcast_to(tile_cnt, (N_EXPERTS, LANES)).astype(I32)
    tbase_ref[0] = jnp.broadcast_to(run_ref[...], (N_EXPERTS, LANES)).astype(I32)
    run_ref[...] = run_ref[...] + tile_cnt


def _mid(y_ssm, y_attn, x2, w_out, b_out, g_cross, w_q, k_mem, v_mem, w_o, g_ffn, w_r_t, b_r, seq, mem_len):
    n = x2.shape[0]
    tiles_per_batch = seq // TM_MID
    row = lambda i: (i, 0)
    col = lambda i: (0, i)
    fixed = lambda i: (0, 0)
    memb = lambda i: (i // tiles_per_batch, 0)
    tile3 = lambda i: (i, 0, 0)
    n_tiles = n // TM_MID
    upper = jnp.asarray(np.triu(np.ones((TM_MID, TM_MID), np.float32), 1)).astype(BF16)
    lower = jnp.asarray(np.tril(np.ones((N_EXPERTS, N_EXPERTS), np.float32), -1)).astype(BF16)
    return pl.pallas_call(
        _mid_kernel,
        name="mid",
        grid=(n_tiles,),
        in_specs=[
            pl.BlockSpec((TM_MID, SSM_WIDTH), row),
            pl.BlockSpec((TM_MID, ATTN_WIDTH), row),
            pl.BlockSpec((TM_MID, D_MODEL), row),
            pl.BlockSpec((D_MODEL, D_MODEL), fixed),
            pl.BlockSpec((1, D_MODEL), fixed),
            pl.BlockSpec((1, D_MODEL), fixed),
            pl.BlockSpec((D_MODEL, CROSS_WIDTH), fixed),
            pl.BlockSpec((mem_len, CROSS_WIDTH), memb),
            pl.BlockSpec((mem_len, CROSS_WIDTH), memb),
            pl.BlockSpec((CROSS_WIDTH, D_MODEL), fixed),
            pl.BlockSpec((1, D_MODEL), fixed),
            pl.BlockSpec((N_EXPERTS, D_MODEL), fixed),
            pl.BlockSpec((N_EXPERTS, 1), fixed),
            pl.BlockSpec((TM_MID, TM_MID), fixed),
            pl.BlockSpec((N_EXPERTS, N_EXPERTS), fixed),
        ],
        out_specs=[
            pl.BlockSpec((TM_MID, D_MODEL), row),
            pl.BlockSpec((TM_MID, D_MODEL), row),
            pl.BlockSpec((TOP_K, TM_MID), col),
            pl.BlockSpec((TOP_K, TM_MID), col),
            pl.BlockSpec((1, N_EXPERTS, LANES), tile3),
            pl.BlockSpec((1, N_EXPERTS, LANES), tile3),
        ],
        out_shape=[
            jax.ShapeDtypeStruct((n, D_MODEL), F32),
            jax.ShapeDtypeStruct((n, D_MODEL), BF16),
            jax.ShapeDtypeStruct((TOP_K, n), F32),
            jax.ShapeDtypeStruct((TOP_K, n), I32),
            jax.ShapeDtypeStruct((n_tiles, N_EXPERTS, LANES), I32),
            jax.ShapeDtypeStruct((n_tiles, N_EXPERTS, LANES), I32),
        ],
        scratch_shapes=[
            pltpu.VMEM((N_EXPERTS, 1), F32),
            pltpu.VMEM((D_MODEL, D_MODEL), BF16),
            pltpu.VMEM((D_MODEL, CROSS_WIDTH), BF16),
            pltpu.VMEM((CROSS_WIDTH, D_MODEL), BF16),
        ],
        compiler_params=_params(("arbitrary",)),
    )(y_ssm, y_attn, x2, w_out, b_out, g_cross, w_q, k_mem, v_mem, w_o, g_ffn, w_r_t, b_r, upper, lower)


ROW_CHUNK = 256
LOCAL_ROWS = -(-(TM_MID * TOP_K + N_EXPERTS * (SUBLANES - 1)) // ROW_CHUNK) * ROW_CHUNK


U32 = jnp.uint32
PACKED = D_MODEL // 2


def _pack_pairs(v):
    bits = lax.bitcast_convert_type(v, U32)
    return bits[:, PACKED:] | (bits[:, :PACKED] >> 16)


def _unpack_pairs(w):
    lo = lax.bitcast_convert_type(w << 16, F32)
    hi = lax.bitcast_convert_type(w & jnp.uint32(0xFFFF0000), F32)
    return jnp.concatenate([lo, hi], axis=1).astype(BF16)


def _row_copy(n, src_ref, src, dst_ref, dst, sem):
    aligned = lambda v: v if isinstance(v, int) else pl.multiple_of(v, SUBLANES)
    return pltpu.make_async_copy(src_ref.at[pl.ds(aligned(src), aligned(n))],
                                 dst_ref.at[pl.ds(aligned(dst), aligned(n))], sem)


def _start_run(n, src_ref, src, dst_ref, dst, sem):
    @pl.when(n > 0)
    def _():
        _row_copy(n, src_ref, src, dst_ref, dst, sem).start()


def _wait_rows(n, src_ref, dst_ref, sem):
    _row_copy(n, src_ref, 0, dst_ref, 0, sem).wait()


def _dispatch_kernel(tcnt_ref, lo_ref, gs_ref, padn_ref, pads_ref, na_ref, u_ref, lp_ref, xs_ref,
                     xloc_ref, zero_ref, sem, zsem):
    i = pl.program_id(0)

    def zero_fill(wait):
        def fill(n, dst):
            cp = _row_copy(n, zero_ref, 0, xs_ref, dst, zsem)
            if wait:
                cp.wait()
            else:
                cp.start()

        def body(e, carry):
            pl.when(padn_ref[e] > 0)(lambda: fill(padn_ref[e], pads_ref[e]))
            return carry
        lax.fori_loop(0, N_EXPERTS, body, 0)

        def tail(b, carry):
            fill(MOE_ROWS, b * MOE_ROWS)
            return carry
        lax.fori_loop(na_ref[0], xs_ref.shape[0] // MOE_ROWS, tail, 0)

    @pl.when(i == 0)
    def _():
        zero_ref[...] = jnp.zeros_like(zero_ref)
        zero_fill(False)

    slot = i % 2
    u = u_ref[...]
    lp = lp_ref[...]
    for c in range(LOCAL_ROWS // ROW_CHUNK):
        r = lax.broadcasted_iota(I32, (ROW_CHUNK, TM_MID), 0) + c * ROW_CHUNK
        p = jnp.zeros((ROW_CHUNK, TM_MID), F32)
        for kk in range(TOP_K):
            p = jnp.where(lp[kk:kk + 1, :] == r, 1.0, p)
        xloc_ref[slot, c * ROW_CHUNK:(c + 1) * ROW_CHUNK, :] = _pack_pairs(_dot(p.astype(BF16), u))

    def start_runs(tile):
        s = tile % 2

        def body(e, carry):
            j = tile * N_EXPERTS + e
            _start_run(tcnt_ref[j], xloc_ref.at[s], lo_ref[j], xs_ref, gs_ref[j], sem.at[s])
            return carry
        lax.fori_loop(0, N_EXPERTS, body, 0)

    def wait_runs(tile):
        s = tile % 2
        last = tile * N_EXPERTS + N_EXPERTS - 1
        _wait_rows(lo_ref[last] + tcnt_ref[last], xloc_ref.at[s], xs_ref, sem.at[s])

    start_runs(i)

    @pl.when(i > 0)
    def _():
        wait_runs(i - 1)

    @pl.when(i == pl.num_programs(0) - 1)
    def _():
        wait_runs(i)
        zero_fill(True)


def _dispatch(tcnt, lo, gstart, padn, pads, n_active, u3, lp, n_rows):
    n = u3.shape[0]
    return pl.pallas_call(
        _dispatch_kernel,
        name="dispatch",
        grid_spec=pltpu.PrefetchScalarGridSpec(
            num_scalar_prefetch=6,
            grid=(n // TM_MID,),
            in_specs=[
                pl.BlockSpec((TM_MID, D_MODEL), lambda i, *_: (i, 0)),
                pl.BlockSpec((TOP_K, TM_MID), lambda i, *_: (0, i)),
            ],
            out_specs=pl.BlockSpec(memory_space=pl.ANY),
            scratch_shapes=[
                pltpu.VMEM((2, LOCAL_ROWS, PACKED), U32),
                pltpu.VMEM((MOE_ROWS, PACKED), U32),
                pltpu.SemaphoreType.DMA((2,)),
                pltpu.SemaphoreType.DMA,
            ],
        ),
        out_shape=jax.ShapeDtypeStruct((n_rows, PACKED), U32),
        compiler_params=_params(("arbitrary",)),
    )(tcnt, lo, gstart, padn, pads, n_active, u3, lp)


PAIR_COLS = 2 * LANES


def _expert_kernel(be_ref, nx_ref, nv_ref, na_ref, x_ref, b1_ref, b2_ref, perm_ref, w1_hbm, w2_hbm, y_ref,
                   w1s_ref, w2s_ref, w1p_ref, w2b_ref, sem):
    b = pl.program_id(0)
    valid_rows = nv_ref[b]
    active = valid_rows > 0
    expert = be_ref[b]
    new_expert = jnp.logical_or(b == 0, expert != be_ref[jnp.maximum(b - 1, 0)])

    def fetch(e, wait):
        for src, dst, s in ((w1_hbm, w1s_ref, 0), (w2_hbm, w2s_ref, 1)):
            cp = pltpu.make_async_copy(src.at[e], dst, sem.at[s])
            if wait:
                cp.wait()
            else:
                cp.start(priority=1)

    @pl.when(jnp.logical_and(b == 0, active))
    def _():
        fetch(expert, False)

    @pl.when(jnp.logical_and(active, new_expert))
    def _():
        fetch(expert, True)
        for c in range(2 * D_FF // PAIR_COLS):
            cs = slice(c * PAIR_COLS, (c + 1) * PAIR_COLS)
            w1p_ref[:, cs] = _dot(w1s_ref[:, cs].astype(BF16), perm_ref[...]).astype(BF16)
        w2b_ref[...] = w2s_ref[...].astype(BF16)

        @pl.when(nx_ref[b] >= 0)
        def _():
            fetch(nx_ref[b], False)

    for live in range(MOE_PART, MOE_ROWS + 1, MOE_PART):
        @pl.when(jnp.logical_and(valid_rows > live - MOE_PART, valid_rows <= live))
        def _():
            x = _unpack_pairs(x_ref[:live, :])
            hdn = _dot(x, w1p_ref[...]) + b1_ref[0]
            acts = []
            for c in range(2 * D_FF // PAIR_COLS):
                hg = jnp.minimum(hdn[:, c * PAIR_COLS:c * PAIR_COLS + LANES], SWIGLU_LIMIT)
                hl = jnp.clip(hdn[:, c * PAIR_COLS + LANES:(c + 1) * PAIR_COLS], -SWIGLU_LIMIT, SWIGLU_LIMIT)
                acts.append((hg * _sigmoid(SWIGLU_ALPHA * hg) * (hl + 1.0)).astype(BF16))
            y = _dot(jnp.concatenate(acts, axis=1), w2b_ref[...]) + b2_ref[0]
            y_ref[:live, :] = _pack_pairs(y.astype(BF16).astype(F32))
            if live < MOE_ROWS:
                y_ref[live:, :] = jnp.zeros((MOE_ROWS - live, PACKED), U32)

    @pl.when(jnp.logical_not(active))
    def _():
        y_ref[...] = jnp.zeros_like(y_ref)


def _experts(block_exp, next_exp, valid_rows, n_active, xs, w1, b1p, w2, b2):
    n_rows = xs.shape[0]
    n_blocks = n_rows // MOE_ROWS
    xmap = lambda b, be, nx, nv, na: (jnp.maximum(jnp.minimum(b, na[0] - 1), 0), 0)
    emap = lambda b, be, nx, nv, na: (be[b], 0, 0)
    perm = np.zeros((PAIR_COLS, PAIR_COLS), np.float32)
    perm[2 * np.arange(LANES), np.arange(LANES)] = 1.0
    perm[2 * np.arange(LANES) + 1, LANES + np.arange(LANES)] = 1.0
    return pl.pallas_call(
        _expert_kernel,
        name="experts",
        grid_spec=pltpu.PrefetchScalarGridSpec(
            num_scalar_prefetch=4,
            grid=(n_blocks,),
            in_specs=[
                pl.BlockSpec((MOE_ROWS, PACKED), xmap),
                pl.BlockSpec((1, 1, 2 * D_FF), emap),
                pl.BlockSpec((1, 1, D_MODEL), emap),
                pl.BlockSpec((PAIR_COLS, PAIR_COLS), lambda b, *_: (0, 0)),
                pl.BlockSpec(memory_space=pl.ANY),
                pl.BlockSpec(memory_space=pl.ANY),
            ],
            out_specs=pl.BlockSpec((MOE_ROWS, PACKED), lambda b, *_: (b, 0)),
            scratch_shapes=[
                pltpu.VMEM((D_MODEL, 2 * D_FF), F32),
                pltpu.VMEM((D_FF, D_MODEL), F32),
                pltpu.VMEM((D_MODEL, 2 * D_FF), BF16),
                pltpu.VMEM((D_FF, D_MODEL), BF16),
                pltpu.SemaphoreType.DMA((2,)),
            ],
        ),
        out_shape=jax.ShapeDtypeStruct((n_rows, PACKED), U32),
        compiler_params=_params(("arbitrary",)),
    )(block_exp, next_exp, valid_rows, n_active, xs, b1p, b2, jnp.asarray(perm).astype(BF16), w1, w2)


def _combine_kernel(tcnt_ref, lo_ref, gs_ref, h2_ref, gate_ref, lp_ref, g_ref, ys_ref, o_ref, yloc_ref, sem):
    i = pl.program_id(0)

    slot = i % 2

    def start_runs(tile):
        s = tile % 2

        def body(e, carry):
            j = tile * N_EXPERTS + e
            _start_run(tcnt_ref[j], ys_ref, gs_ref[j], yloc_ref.at[s], lo_ref[j], sem.at[s])
            return carry
        lax.fori_loop(0, N_EXPERTS, body, 0)

    @pl.when(i == 0)
    def _():
        yloc_ref[...] = jnp.zeros_like(yloc_ref)
        start_runs(i)

    @pl.when(i + 1 < pl.num_programs(0))
    def _():
        start_runs(i + 1)

    gates = gate_ref[...]
    lp = lp_ref[...]
    last = i * N_EXPERTS + N_EXPERTS - 1
    _wait_rows(lo_ref[last] + tcnt_ref[last], ys_ref, yloc_ref.at[slot], sem.at[slot])

    h = h2_ref[...]
    for c in range(LOCAL_ROWS // ROW_CHUNK):
        r = lax.broadcasted_iota(I32, (ROW_CHUNK, TM_MID), 0) + c * ROW_CHUNK
        a = jnp.zeros((ROW_CHUNK, TM_MID), F32)
        for kk in range(TOP_K):
            a = jnp.where(lp[kk:kk + 1, :] == r, gates[kk:kk + 1, :], a)
        h = h + _tn(a.astype(BF16), _unpack_pairs(yloc_ref[slot, c * ROW_CHUNK:(c + 1) * ROW_CHUNK, :]))
    o_ref[...] = _rms(h, g_ref[...])


def _combine(tcnt, lo, gstart, h2, gates, lp, g_final, ys):
    n = h2.shape[0]
    return pl.pallas_call(
        _combine_kernel,
        name="combine",
        grid_spec=pltpu.PrefetchScalarGridSpec(
            num_scalar_prefetch=3,
            grid=(n // TM_MID,),
            in_specs=[
                pl.BlockSpec((TM_MID, D_MODEL), lambda i, *_: (i, 0)),
                pl.BlockSpec((TOP_K, TM_MID), lambda i, *_: (0, i)),
                pl.BlockSpec((TOP_K, TM_MID), lambda i, *_: (0, i)),
                pl.BlockSpec((1, D_MODEL), lambda i, *_: (0, 0)),
                pl.BlockSpec(memory_space=pl.ANY),
            ],
            out_specs=pl.BlockSpec((TM_MID, D_MODEL), lambda i, *_: (i, 0)),
            scratch_shapes=[
                pltpu.VMEM((2, LOCAL_ROWS, PACKED), U32),
                pltpu.SemaphoreType.DMA((2,)),
            ],
        ),
        out_shape=jax.ShapeDtypeStruct((n, D_MODEL), F32),
        compiler_params=_params(("arbitrary",)),
    )(tcnt, lo, gstart, h2, gates, lp, g_final, ys)


def _layer(h, mem2, batch, seq, mem_len, g_mix, w_in, b_qkv, conv_w, conv_b, dt_bias, a_log, d_skip,
           g_ssm_out, attn_sinks, bias, w_out, b_out, g_cross, g_mem, w_q_cross, w_kv_cross, w_o_cross,
           g_ffn, w_router, b_router, w1, b1, w2, b2, g_final):
    n = h.shape[0]
    z, xbc, dtp, qkv = _inproj(h, g_mix[None, :], w_in, b_qkv[None, :])

    pad_h = (0, DT_PAD - SSM_HEADS)
    dt_bias_p = jnp.pad(dt_bias, pad_h)[None, :]
    a_head_p = jnp.pad(-jnp.exp(a_log), pad_h)[None, :]
    d_full = jnp.repeat(d_skip, SSM_HEAD_DIM)[None, :]
    per_seq = lambda t: t.reshape(batch, seq, t.shape[-1])
    y_ssm = _ssd(per_seq(xbc), per_seq(z), per_seq(dtp), conv_w, conv_b[None, :], dt_bias_p, a_head_p,
                 d_full, g_ssm_out[None, :]).reshape(n, SSM_WIDTH)
    y_attn = _swa(per_seq(qkv), attn_sinks, bias).reshape(n, ATTN_WIDTH)

    k_mem, v_mem = _memkv(mem2, g_mem[None, :], w_kv_cross)
    h2, u3, gates, lp, tcnt, tbase = _mid(
        y_ssm, y_attn, h, w_out, b_out[None, :], g_cross[None, :],
        w_q_cross, k_mem, v_mem, w_o_cross, g_ffn[None, :],
        w_router.T.astype(BF16), b_router[:, None], seq, mem_len)

    max_rows = n * TOP_K + (n // TM_MID) * N_EXPERTS * (SUBLANES - 1)
    n_blocks = -(-max_rows // MOE_ROWS) + N_EXPERTS
    tcnt = tcnt[:, :, 0]
    tbase = tbase[:, :, 0]
    counts = tbase[-1] + tcnt[-1]
    padded = (counts + MOE_ROWS - 1) // MOE_ROWS * MOE_ROWS
    padded_end = jnp.cumsum(padded)
    start = (padded_end - padded).astype(I32)
    n_active = (padded_end[-1] // MOE_ROWS).astype(I32)
    blk = jnp.arange(n_blocks, dtype=I32)
    blk_exp = jnp.sum((blk * MOE_ROWS)[:, None] >= padded_end[None, :], axis=1)
    eid = jnp.arange(N_EXPERTS, dtype=I32)
    last_exp = jnp.max(jnp.where(padded > 0, eid, 0))
    blk_exp = jnp.where(blk < n_active, jnp.minimum(blk_exp, N_EXPERTS - 1), last_exp).astype(I32)
    later = jnp.logical_and(padded[None, :] > 0, eid[None, :] > eid[:, None])
    next_of = jnp.min(jnp.where(later, eid[None, :], N_EXPERTS), axis=1)
    next_of = jnp.where(next_of < N_EXPERTS, next_of, -1).astype(I32)
    of_blk = blk_exp[:, None] == eid[None, :]
    blk_next = jnp.sum(jnp.where(of_blk, next_of[None, :], 0), axis=1).astype(I32)
    blk_end = jnp.sum(jnp.where(of_blk, (start + counts)[None, :], 0), axis=1)
    blk_valid = jnp.clip(blk_end - blk * MOE_ROWS, 0, MOE_ROWS)
    blk_valid = jnp.where(blk < n_active, blk_valid, 0).astype(I32)

    run_n = tcnt.reshape(-1)
    run_local = (jnp.cumsum(tcnt, axis=1) - tcnt).reshape(-1)
    run_global = (start[None, :] + tbase).reshape(-1)
    xs = _dispatch(run_n, run_local, run_global, padded - counts, start + counts, n_active[None], u3, lp,
                   n_blocks * MOE_ROWS)

    b1p = b1.reshape(N_EXPERTS, 2 * D_FF // PAIR_COLS, LANES, 2).transpose(0, 1, 3, 2)
    ys = _experts(blk_exp, blk_next, blk_valid, n_active[None], xs, w1,
                  b1p.reshape(N_EXPERTS, 1, 2 * D_FF), w2, b2[:, None, :])
    return _combine(run_n, run_local, run_global, h2, gates, lp, g_final, ys)


def kernel(x, mem, g_mix, w_in, b_qkv, conv_w, conv_b, dt_bias, a_log, d_skip, g_ssm_out, attn_sinks, rel_bias, w_out, b_out, g_cross, g_mem, w_q_cross, w_kv_cross, w_o_cross, g_ffn, w_router, b_router, w1, b1, w2, b2, g_final):
    batch, seq, d = x.shape
    mem_len = mem.shape[1]
    depth = g_mix.shape[0]
    assert depth == 1 and d == D_MODEL and seq % TM_MID == 0
    h = x.reshape(batch * seq, d)
    mem2 = mem.reshape(batch * mem_len, d)
    bias = _bias_table(rel_bias)
    out = _layer(
        h, mem2, batch, seq, mem_len, g_mix[0], w_in[0], b_qkv[0], conv_w[0], conv_b[0], dt_bias[0],
        a_log[0], d_skip[0], g_ssm_out[0], attn_sinks[0], bias, w_out[0], b_out[0], g_cross[0],
        g_mem[0], w_q_cross[0], w_kv_cross[0], w_o_cross[0], g_ffn[0], w_router[0], b_router[0],
        w1[0], b1[0], w2[0], b2[0], g_final[None, :])
    return out.reshape(batch, seq, d)
```

```python
import math

import numpy as np
import jax
import jax.numpy as jnp
from jax import lax
from jax.experimental import pallas as pl
from jax.experimental.pallas import tpu as pltpu

F32 = jnp.float32
BF16 = jnp.bfloat16
I32 = jnp.int32

D_MODEL = 1024
RMS_EPS = 1e-5

SSM_HEADS = 8
SSM_HEAD_DIM = 64
SSM_WIDTH = SSM_HEADS * SSM_HEAD_DIM
SSM_STATE = 128
SSM_GROUPS = 2
HEADS_PER_GROUP = SSM_HEADS // SSM_GROUPS
GROUP_WIDTH = SSM_WIDTH // SSM_GROUPS
CONV_WIDTH = 4
CHUNK = 128
CONV_CH = SSM_WIDTH + 2 * SSM_GROUPS * SSM_STATE

ATTN_HEADS = 8
KV_HEADS = 2
Q_PER_KV = ATTN_HEADS // KV_HEADS
HEAD_DIM = 64
ATTN_WIDTH = ATTN_HEADS * HEAD_DIM
KV_WIDTH = KV_HEADS * HEAD_DIM
WINDOW = 128
BLOCK = WINDOW
QKV_COLS = ATTN_WIDTH + 2 * KV_WIDTH

REL_BUCKETS = 32
REL_MAX_DIST = 128

CROSS_HEADS = 4
CROSS_HEAD_DIM = 128
CROSS_WIDTH = CROSS_HEADS * CROSS_HEAD_DIM

N_EXPERTS = 32
TOP_K = 4
D_FF = D_MODEL
SWIGLU_ALPHA = 1.702
SWIGLU_LIMIT = 7.0

LANES = 128
SUBLANES = 8
DT_PAD = LANES
PROJ_COLS = SSM_WIDTH + CONV_CH + DT_PAD + QKV_COLS
VMEM_LIMIT = 56 * 1024 * 1024

TM_PROJ = 1024
TM_MID = 512
SSD_ROWS = 4 * CHUNK
MOE_ROWS = 1024
MOE_PART = 128


def _nt(a, b):
    return lax.dot_general(a, b, (((1,), (1,)), ((), ())), preferred_element_type=F32)


def _tn(a, b):
    return lax.dot_general(a, b, (((0,), (0,)), ((), ())), preferred_element_type=F32)


def _dot(a, b):
    return jnp.dot(a, b, preferred_element_type=F32)


def _rms(x, g):
    return x * lax.rsqrt(jnp.mean(x * x, axis=-1, keepdims=True) + RMS_EPS) * g


def _sigmoid(x):
    return 0.5 * jnp.tanh(0.5 * x) + 0.5


def _params(sem=None, **kwargs):
    return pltpu.CompilerParams(dimension_semantics=sem, vmem_limit_bytes=VMEM_LIMIT, **kwargs)


def _inproj_kernel(x_ref, g_ref, w_f32, bq_ref, z_ref, xbc_ref, dt_ref, qkv_ref, w_ref):
    @pl.when(pl.program_id(0) == 0)
    def _():
        c1 = SSM_WIDTH + CONV_CH
        w_ref[:, :c1] = w_f32[:, :c1].astype(BF16)
        lane = lax.broadcasted_iota(I32, (D_MODEL, DT_PAD), 1)
        w_ref[:, c1:c1 + DT_PAD] = jnp.where(lane < SSM_HEADS, w_f32[:, c1:c1 + DT_PAD], 0.0).astype(BF16)
        w_ref[:, c1 + DT_PAD:] = w_f32[:, c1 + SSM_HEADS:].astype(BF16)

    u = _rms(x_ref[...], g_ref[...])
    p = _dot(u.astype(BF16), w_ref[...])
    z_ref[...] = p[:, :SSM_WIDTH]
    xbc_ref[...] = p[:, SSM_WIDTH:SSM_WIDTH + CONV_CH]
    dt_ref[...] = p[:, SSM_WIDTH + CONV_CH:SSM_WIDTH + CONV_CH + DT_PAD]
    qkv_ref[...] = p[:, SSM_WIDTH + CONV_CH + DT_PAD:] + bq_ref[...]


def _inproj(x2, g_mix, w_in, b_qkv):
    n = x2.shape[0]
    row = lambda i: (i, 0)
    fixed = lambda i: (0, 0)
    return pl.pallas_call(
        _inproj_kernel,
        name="inproj",
        grid=(n // TM_PROJ,),
        in_specs=[
            pl.BlockSpec((TM_PROJ, D_MODEL), row),
            pl.BlockSpec((1, D_MODEL), fixed),
            pl.BlockSpec(w_in.shape, fixed),
            pl.BlockSpec((1, QKV_COLS), fixed),
        ],
        out_specs=[
            pl.BlockSpec((TM_PROJ, SSM_WIDTH), row),
            pl.BlockSpec((TM_PROJ, CONV_CH), row),
            pl.BlockSpec((TM_PROJ, DT_PAD), row),
            pl.BlockSpec((TM_PROJ, QKV_COLS), row),
        ],
        out_shape=[
            jax.ShapeDtypeStruct((n, SSM_WIDTH), F32),
            jax.ShapeDtypeStruct((n, CONV_CH), F32),
            jax.ShapeDtypeStruct((n, DT_PAD), F32),
            jax.ShapeDtypeStruct((n, QKV_COLS), F32),
        ],
        scratch_shapes=[pltpu.VMEM((D_MODEL, PROJ_COLS), BF16)],
        compiler_params=_params(("arbitrary",)),
    )(x2, g_mix, w_in, b_qkv)


def _split3(x):
    hi = x.astype(BF16)
    rest = x - hi.astype(F32)
    mid = rest.astype(BF16)
    lo = (rest - mid.astype(F32)).astype(BF16)
    return jnp.concatenate([hi, mid, lo], axis=1)


def _ssd_kernel(xbc_ref, z_ref, dt_ref, cw_ref, cb_ref, dtb_ref, ah_ref, dsk_ref, gout_ref,
                e64_ref, ecat_ref, tri_ref, y_ref, conv_ref, state_ref):
    c = pl.program_id(0)

    @pl.when(c == 0)
    def _():
        conv_ref[:, :SUBLANES, :] = jnp.zeros((conv_ref.shape[0], SUBLANES, CONV_CH), F32)
        state_ref[...] = jnp.zeros_like(state_ref)

    for cc in range(xbc_ref.shape[1] // CHUNK):
        for b in range(xbc_ref.shape[0]):
            _ssd_chunk(b, slice(cc * CHUNK, (cc + 1) * CHUNK), xbc_ref, z_ref, dt_ref, cw_ref, cb_ref, dtb_ref,
                       ah_ref, dsk_ref, gout_ref, e64_ref, ecat_ref, tri_ref, y_ref, conv_ref, state_ref)


def _ssd_chunk(b, rows, xbc_ref, z_ref, dt_ref, cw_ref, cb_ref, dtb_ref, ah_ref, dsk_ref, gout_ref,
               e64_ref, ecat_ref, tri_ref, y_ref, conv_ref, state_ref):
    u = xbc_ref[b, rows, :]
    conv_ref[b, SUBLANES:, :] = u
    acc = u * cw_ref[CONV_WIDTH - 1:CONV_WIDTH, :] + cb_ref[...]
    for j in range(1, CONV_WIDTH):
        shifted = conv_ref[b, SUBLANES - j:SUBLANES - j + CHUNK, :]
        acc = acc + shifted * cw_ref[CONV_WIDTH - 1 - j:CONV_WIDTH - j, :]
    conv_ref[b, :SUBLANES, :] = u[CHUNK - SUBLANES:, :]
    xbc = acc * _sigmoid(acc)

    xs = xbc[:, :SSM_WIDTH]
    bm = xbc[:, SSM_WIDTH:SSM_WIDTH + SSM_GROUPS * SSM_STATE]
    cm = xbc[:, SSM_WIDTH + SSM_GROUPS * SSM_STATE:]

    dtr = dt_ref[b, rows, :] + dtb_ref[...]
    dt = jnp.maximum(dtr, 0.0) + jnp.log1p(jnp.exp(-jnp.abs(dtr)))
    a = dt * ah_ref[...]
    a3 = _split3(a)
    a_rows = jnp.concatenate([a3[:, :DT_PAD], a3[:, DT_PAD:2 * DT_PAD], a3[:, 2 * DT_PAD:]], axis=0)
    cs = _dot(tri_ref[...], a_rows)
    cs_row = cs.T
    dt_full = _dot(_split3(dt), e64_ref[...])
    cs_cat = _dot(_split3(cs), ecat_ref[...])
    cs_wide = cs_cat[:, :SSM_HEADS * CHUNK]
    cs_full = cs_cat[:, SSM_HEADS * CHUNK:]
    cs_last = cs_full[CHUNK - 1:CHUNK, :]

    xdt = xs * dt_full
    xdt_b = xdt.astype(BF16)
    xw_b = (xdt * jnp.exp(cs_last - cs_full)).astype(BF16)
    in_decay = jnp.exp(cs_full)
    chunk_decay = jnp.exp(cs_last)

    li = lax.broadcasted_iota(I32, (CHUNK, CHUNK), 0)
    si = lax.broadcasted_iota(I32, (CHUNK, CHUNK), 1)
    causal = li >= si

    ys = []
    for g in range(SSM_GROUPS):
        bg = bm[:, g * SSM_STATE:(g + 1) * SSM_STATE].astype(BF16)
        cg = cm[:, g * SSM_STATE:(g + 1) * SSM_STATE].astype(BF16)
        cb = _nt(cg, bg)
        yd = []
        for r in range(HEADS_PER_GROUP):
            h = g * HEADS_PER_GROUP + r
            diff = cs_wide[:, h * CHUNK:(h + 1) * CHUNK] - cs_row[h:h + 1, :]
            m = cb * jnp.exp(jnp.where(causal, diff, -jnp.inf))
            yd.append(_dot(m.astype(BF16), xdt_b[:, h * SSM_HEAD_DIM:(h + 1) * SSM_HEAD_DIM]))
        y_diag = jnp.concatenate(yd, axis=1)
        gs = slice(g * GROUP_WIDTH, (g + 1) * GROUP_WIDTH)
        st = state_ref[b, g]
        y_off = _dot(cg, st.astype(BF16)) * in_decay[:, gs]
        state_ref[b, g] = st * chunk_decay[:, gs] + _tn(bg, xw_b[:, gs])
        ys.append(y_diag + y_off)
    y = jnp.concatenate(ys, axis=1) + dsk_ref[...] * xs

    zz = z_ref[b, rows, :]
    y = y * (zz * _sigmoid(zz))
    outs = []
    for g in range(SSM_GROUPS):
        yg = y[:, g * GROUP_WIDTH:(g + 1) * GROUP_WIDTH]
        outs.append(yg * lax.rsqrt(jnp.mean(yg * yg, axis=-1, keepdims=True) + RMS_EPS))
    y_ref[b, rows, :] = jnp.concatenate(outs, axis=1) * gout_ref[...]


def _ssd(xbc, z, dtp, conv_w, conv_b, dt_bias_p, a_head_p, d_full, g_out):
    batch, seq, _ = xbc.shape
    chunk = lambda c: (0, c, 0)
    fixed = lambda c: (0, 0)
    head = np.arange(SSM_WIDTH) // SSM_HEAD_DIM
    e64 = (np.arange(DT_PAD)[:, None] == head[None, :]).astype(np.float32)
    head_w = np.arange(SSM_HEADS * CHUNK) // CHUNK
    e128 = (np.arange(DT_PAD)[:, None] == head_w[None, :]).astype(np.float32)
    e64_3 = jnp.asarray(np.tile(e64, (3, 1))).astype(BF16)
    ecat_3 = jnp.asarray(np.tile(np.concatenate([e128, e64], axis=1), (3, 1))).astype(BF16)
    tri_3 = jnp.asarray(np.tile(np.tril(np.ones((CHUNK, CHUNK), np.float32)), (1, 3))).astype(BF16)
    return pl.pallas_call(
        _ssd_kernel,
        name="ssd",
        grid=(seq // SSD_ROWS,),
        in_specs=[
            pl.BlockSpec((batch, SSD_ROWS, CONV_CH), chunk),
            pl.BlockSpec((batch, SSD_ROWS, SSM_WIDTH), chunk),
            pl.BlockSpec((batch, SSD_ROWS, DT_PAD), chunk),
            pl.BlockSpec((CONV_WIDTH, CONV_CH), fixed),
            pl.BlockSpec((1, CONV_CH), fixed),
            pl.BlockSpec((1, DT_PAD), fixed),
            pl.BlockSpec((1, DT_PAD), fixed),
            pl.BlockSpec((1, SSM_WIDTH), fixed),
            pl.BlockSpec((1, SSM_WIDTH), fixed),
            pl.BlockSpec((3 * DT_PAD, SSM_WIDTH), fixed),
            pl.BlockSpec((3 * DT_PAD, SSM_HEADS * CHUNK + SSM_WIDTH), fixed),
            pl.BlockSpec((CHUNK, 3 * CHUNK), fixed),
        ],
        out_specs=pl.BlockSpec((batch, SSD_ROWS, SSM_WIDTH), chunk),
        out_shape=jax.ShapeDtypeStruct((batch, seq, SSM_WIDTH), F32),
        scratch_shapes=[
            pltpu.VMEM((batch, SUBLANES + CHUNK, CONV_CH), F32),
            pltpu.VMEM((batch, SSM_GROUPS, SSM_STATE, GROUP_WIDTH), F32),
        ],
        compiler_params=_params(("arbitrary",)),
    )(xbc, z, dtp, conv_w, conv_b, dt_bias_p, a_head_p, d_full, g_out, e64_3, ecat_3, tri_3)


def _t5_bucket_table():
    q_pos = np.arange(BLOCK)[:, None]
    k_pos = np.arange(2 * BLOCK)[None, :] - BLOCK
    dist = q_pos - k_pos
    d = np.maximum(dist, 0)
    max_exact = REL_BUCKETS // 2
    ratio = np.maximum(d, max_exact).astype(np.float32) / np.float32(max_exact)
    large = max_exact + (np.log(ratio) / np.float32(math.log(REL_MAX_DIST / max_exact))
                         * np.float32(REL_BUCKETS - max_exact)).astype(np.int32)
    large = np.minimum(large, REL_BUCKETS - 1)
    bucket = np.where(d < max_exact, d, large)
    in_window = (dist >= 0) & (dist < WINDOW)
    return np.where(in_window, bucket, -1).astype(np.int32)


def _bias_kernel(rb_ref, bucket_ref, out_ref):
    bucket = bucket_ref[...]
    for h in range(ATTN_HEADS):
        acc = jnp.zeros(bucket.shape, F32)
        for b in range(REL_BUCKETS):
            acc = jnp.where(bucket == b, rb_ref[b, h], acc)
        out_ref[h] = jnp.where(bucket >= 0, acc, -jnp.inf)


def _bias_table(rel_bias):
    bucket = jnp.asarray(_t5_bucket_table())
    return pl.pallas_call(
        _bias_kernel,
        name="bias_table",
        in_specs=[
            pl.BlockSpec(memory_space=pltpu.SMEM),
            pl.BlockSpec(memory_space=pltpu.VMEM),
        ],
        out_specs=pl.BlockSpec(memory_space=pltpu.VMEM),
        out_shape=jax.ShapeDtypeStruct((ATTN_HEADS, BLOCK, 2 * BLOCK), F32),
    )(rel_bias, bucket)


def _swa_kernel(sink_ref, q_ref, k_ref, v_ref, kp_ref, vp_ref, bias_ref, o_ref):
    i = pl.program_id(0)
    col = lax.broadcasted_iota(I32, (BLOCK, 2 * BLOCK), 1)
    key_ok = jnp.logical_or(col >= BLOCK, i > 0)
    for b in range(q_ref.shape[0]):
        q = (q_ref[b] * (HEAD_DIM ** -0.5)).astype(BF16)
        outs = []
        for g in range(KV_HEADS):
            gs = slice(g * HEAD_DIM, (g + 1) * HEAD_DIM)
            k2 = jnp.concatenate([kp_ref[b, :, gs], k_ref[b, :, gs]], axis=0).astype(BF16)
            v2 = jnp.concatenate([vp_ref[b, :, gs], v_ref[b, :, gs]], axis=0).astype(BF16)
            for r in range(Q_PER_KV):
                h = g * Q_PER_KV + r
                s = _nt(q[:, h * HEAD_DIM:(h + 1) * HEAD_DIM], k2) + bias_ref[h]
                s = jnp.where(key_ok, s, -jnp.inf)
                sink = sink_ref[h]
                m = jnp.maximum(jnp.max(s, axis=-1, keepdims=True), sink)
                p = jnp.exp(s - m)
                denom = jnp.sum(p, axis=-1, keepdims=True) + jnp.exp(sink - m)
                outs.append(_dot(p.astype(BF16), v2) / denom)
        o_ref[b] = jnp.concatenate(outs, axis=1)


def _swa(qkv, sinks, bias):
    batch, seq, _ = qkv.shape
    kcol = ATTN_WIDTH // KV_WIDTH
    prev = lambda i: jnp.maximum(i - 1, 0)
    return pl.pallas_call(
        _swa_kernel,
        name="swa",
        grid_spec=pltpu.PrefetchScalarGridSpec(
            num_scalar_prefetch=1,
            grid=(seq // BLOCK,),
            in_specs=[
                pl.BlockSpec((batch, BLOCK, ATTN_WIDTH), lambda i, s: (0, i, 0)),
                pl.BlockSpec((batch, BLOCK, KV_WIDTH), lambda i, s: (0, i, kcol)),
                pl.BlockSpec((batch, BLOCK, KV_WIDTH), lambda i, s: (0, i, kcol + 1)),
                pl.BlockSpec((batch, BLOCK, KV_WIDTH), lambda i, s: (0, prev(i), kcol)),
                pl.BlockSpec((batch, BLOCK, KV_WIDTH), lambda i, s: (0, prev(i), kcol + 1)),
                pl.BlockSpec((ATTN_HEADS, BLOCK, 2 * BLOCK), lambda i, s: (0, 0, 0)),
            ],
            out_specs=pl.BlockSpec((batch, BLOCK, ATTN_WIDTH), lambda i, s: (0, i, 0)),
        ),
        out_shape=jax.ShapeDtypeStruct((batch, seq, ATTN_WIDTH), F32),
        compiler_params=_params(("arbitrary",)),
    )(sinks, qkv, qkv, qkv, qkv, qkv, bias)


def _memkv_kernel(m_ref, g_ref, w_ref, k_ref, v_ref):
    u = _rms(m_ref[...], g_ref[...])
    kv = _dot(u.astype(BF16), w_ref[...].astype(BF16))
    k_ref[...] = kv[:, :CROSS_WIDTH].astype(BF16)
    v_ref[...] = kv[:, CROSS_WIDTH:].astype(BF16)


def _memkv(mem2, g_mem, w_kv):
    n = mem2.shape[0]
    return pl.pallas_call(
        _memkv_kernel,
        name="memkv",
        out_shape=[jax.ShapeDtypeStruct((n, CROSS_WIDTH), BF16)] * 2,
        compiler_params=_params(),
    )(mem2, g_mem, w_kv)


def _mid_kernel(ys_ref, ya_ref, x_ref, wout_f32, bout_ref, gc_ref, wq_f32, k_ref, v_ref, wo_f32,
                gf_ref, wr_ref, br_ref, upper_ref, lower_ref,
                h2_ref, u3_ref, gate_ref, lp_ref, tcnt_ref, tbase_ref, run_ref, wout_ref, wq_ref, wo_ref):
    i = pl.program_id(0)

    @pl.when(i == 0)
    def _():
        run_ref[...] = jnp.zeros_like(run_ref)
        wout_ref[...] = wout_f32[...].astype(BF16)
        wq_ref[...] = wq_f32[...].astype(BF16)
        wo_ref[...] = wo_f32[...].astype(BF16)

    ycat = jnp.concatenate([ys_ref[...], ya_ref[...]], axis=1).astype(BF16)
    h1 = x_ref[...] + _dot(ycat, wout_ref[...]) + bout_ref[...]

    u2 = _rms(h1, gc_ref[...])
    q = _dot(u2.astype(BF16), wq_ref[...]).astype(BF16)
    outs = []
    for h in range(CROSS_HEADS):
        hs = slice(h * CROSS_HEAD_DIM, (h + 1) * CROSS_HEAD_DIM)
        s = _nt(q[:, hs], k_ref[:, hs]) * (CROSS_HEAD_DIM ** -0.5)
        m = jnp.max(s, axis=-1, keepdims=True)
        p = jnp.exp(s - m)
        outs.append(_dot(p.astype(BF16), v_ref[:, hs]) / jnp.sum(p, axis=-1, keepdims=True))
    o = jnp.concatenate(outs, axis=1).astype(BF16)
    h2 = h1 + _dot(o, wo_ref[...])
    h2_ref[...] = h2
    u3 = _rms(h2, gf_ref[...]).astype(BF16)
    u3_ref[...] = u3

    logits = _nt(wr_ref[...], u3) + br_ref[...]
    eid = lax.broadcasted_iota(I32, logits.shape, 0)
    vals, idxs, hots = [], [], []
    for _ in range(TOP_K):
        m = jnp.max(logits, axis=0, keepdims=True)
        ix = jnp.min(jnp.where(logits == m, eid, N_EXPERTS), axis=0, keepdims=True)
        hot = eid == ix
        logits = jnp.where(hot, -jnp.inf, logits)
        vals.append(m)
        idxs.append(ix)
        hots.append(hot)
    ex = [jnp.exp(v - vals[0]) for v in vals]
    tot = ex[0] + ex[1] + ex[2] + ex[3]
    gate_ref[...] = jnp.concatenate([e / tot for e in ex], axis=0)

    hot_f = [jnp.where(hot, 1.0, 0.0) for hot in hots]
    cnt_k = [jnp.sum(hf, axis=1, keepdims=True) for hf in hot_f]
    tile_cnt = cnt_k[0] + cnt_k[1] + cnt_k[2] + cnt_k[3]
    tile_cnt = jnp.floor((tile_cnt + (SUBLANES - 1)) * (1.0 / SUBLANES)) * SUBLANES
    units = jnp.broadcast_to(tile_cnt * (1.0 / SUBLANES), (N_EXPERTS, LANES)).astype(BF16)
    offs = _dot(lower_ref[...], units)[:, :1] * SUBLANES
    local = []
    before_all = _dot(jnp.concatenate(hot_f, axis=0).astype(BF16), upper_ref[...])
    for kk in range(TOP_K):
        before = before_all[kk * N_EXPERTS:(kk + 1) * N_EXPERTS, :]
        local.append(jnp.sum(jnp.where(hots[kk], before + offs, 0.0), axis=0, keepdims=True))
        offs = offs + cnt_k[kk]
    lp_ref[...] = jnp.concatenate(local, axis=0).astype(I32)
    tcnt_ref[0] = jnp.broadcast_to(tile_cnt, (N_EXPERTS, LANES)).astype(I32)
    tbase_ref[0] = jnp.broadcast_to(run_ref[...], (N_EXPERTS, LANES)).astype(I32)
    run_ref[...] = run_ref[...] + tile_cnt


def _mid(y_ssm, y_attn, x2, w_out, b_out, g_cross, w_q, k_mem, v_mem, w_o, g_ffn, w_r_t, b_r, seq, mem_len):
    n = x2.shape[0]
    tiles_per_batch = seq // TM_MID
    row = lambda i: (i, 0)
    col = lambda i: (0, i)
    fixed = lambda i: (0, 0)
    memb = lambda i: (i // tiles_per_batch, 0)
    tile3 = lambda i: (i, 0, 0)
    n_tiles = n // TM_MID
    upper = jnp.asarray(np.triu(np.ones((TM_MID, TM_MID), np.float32), 1)).astype(BF16)
    lower = jnp.asarray(np.tril(np.ones((N_EXPERTS, N_EXPERTS), np.float32), -1)).astype(BF16)
    return pl.pallas_call(
        _mid_kernel,
        name="mid",
        grid=(n_tiles,),
        in_specs=[
            pl.BlockSpec((TM_MID, SSM_WIDTH), row),
            pl.BlockSpec((TM_MID, ATTN_WIDTH), row),
            pl.BlockSpec((TM_MID, D_MODEL), row),
            pl.BlockSpec((D_MODEL, D_MODEL), fixed),
            pl.BlockSpec((1, D_MODEL), fixed),
            pl.BlockSpec((1, D_MODEL), fixed),
            pl.BlockSpec((D_MODEL, CROSS_WIDTH), fixed),
            pl.BlockSpec((mem_len, CROSS_WIDTH), memb),
            pl.BlockSpec((mem_len, CROSS_WIDTH), memb),
            pl.BlockSpec((CROSS_WIDTH, D_MODEL), fixed),
            pl.BlockSpec((1, D_MODEL), fixed),
            pl.BlockSpec((N_EXPERTS, D_MODEL), fixed),
            pl.BlockSpec((N_EXPERTS, 1), fixed),
            pl.BlockSpec((TM_MID, TM_MID), fixed),
            pl.BlockSpec((N_EXPERTS, N_EXPERTS), fixed),
        ],
        out_specs=[
            pl.BlockSpec((TM_MID, D_MODEL), row),
            pl.BlockSpec((TM_MID, D_MODEL), row),
            pl.BlockSpec((TOP_K, TM_MID), col),
            pl.BlockSpec((TOP_K, TM_MID), col),
            pl.BlockSpec((1, N_EXPERTS, LANES), tile3),
            pl.BlockSpec((1, N_EXPERTS, LANES), tile3),
        ],
        out_shape=[
            jax.ShapeDtypeStruct((n, D_MODEL), F32),
            jax.ShapeDtypeStruct((n, D_MODEL), BF16),
            jax.ShapeDtypeStruct((TOP_K, n), F32),
            jax.ShapeDtypeStruct((TOP_K, n), I32),
            jax.ShapeDtypeStruct((n_tiles, N_EXPERTS, LANES), I32),
            jax.ShapeDtypeStruct((n_tiles, N_EXPERTS, LANES), I32),
        ],
        scratch_shapes=[
            pltpu.VMEM((N_EXPERTS, 1), F32),
            pltpu.VMEM((D_MODEL, D_MODEL), BF16),
            pltpu.VMEM((D_MODEL, CROSS_WIDTH), BF16),
            pltpu.VMEM((CROSS_WIDTH, D_MODEL), BF16),
        ],
        compiler_params=_params(("arbitrary",)),
    )(y_ssm, y_attn, x2, w_out, b_out, g_cross, w_q, k_mem, v_mem, w_o, g_ffn, w_r_t, b_r, upper, lower)


ROW_CHUNK = 256
LOCAL_ROWS = -(-(TM_MID * TOP_K + N_EXPERTS * (SUBLANES - 1)) // ROW_CHUNK) * ROW_CHUNK


U32 = jnp.uint32
PACKED = D_MODEL // 2


def _pack_pairs(v):
    bits = lax.bitcast_convert_type(v, U32)
    return bits[:, PACKED:] | (bits[:, :PACKED] >> 16)


def _unpack_pairs(w):
    lo = lax.bitcast_convert_type(w << 16, F32)
    hi = lax.bitcast_convert_type(w & jnp.uint32(0xFFFF0000), F32)
    return jnp.concatenate([lo, hi], axis=1).astype(BF16)


def _row_copy(n, src_ref, src, dst_ref, dst, sem):
    aligned = lambda v: v if isinstance(v, int) else pl.multiple_of(v, SUBLANES)
    return pltpu.make_async_copy(src_ref.at[pl.ds(aligned(src), aligned(n))],
                                 dst_ref.at[pl.ds(aligned(dst), aligned(n))], sem)


def _start_run(n, src_ref, src, dst_ref, dst, sem):
    @pl.when(n > 0)
    def _():
        _row_copy(n, src_ref, src, dst_ref, dst, sem).start()


def _wait_rows(n, src_ref, dst_ref, sem):
    _row_copy(n, src_ref, 0, dst_ref, 0, sem).wait()


def _dispatch_kernel(tcnt_ref, lo_ref, gs_ref, padn_ref, pads_ref, na_ref, u_ref, lp_ref, xs_ref,
                     xloc_ref, zero_ref, sem, zsem):
    i = pl.program_id(0)

    def zero_fill(wait):
        def fill(n, dst):
            cp = _row_copy(n, zero_ref, 0, xs_ref, dst, zsem)
            if wait:
                cp.wait()
            else:
                cp.start(priority=1)

        def body(e, carry):
            pl.when(padn_ref[e] > 0)(lambda: fill(padn_ref[e], pads_ref[e]))
            return carry
        lax.fori_loop(0, N_EXPERTS, body, 0)

        def tail(b, carry):
            fill(MOE_ROWS, b * MOE_ROWS)
            return carry
        lax.fori_loop(na_ref[0], xs_ref.shape[0] // MOE_ROWS, tail, 0)

    @pl.when(i == 0)
    def _():
        zero_ref[...] = jnp.zeros_like(zero_ref)
        zero_fill(False)

    slot = i % 2
    u = u_ref[...]
    lp = lp_ref[...]
    for c in range(LOCAL_ROWS // ROW_CHUNK):
        r = lax.broadcasted_iota(I32, (ROW_CHUNK, TM_MID), 0) + c * ROW_CHUNK
        p = jnp.zeros((ROW_CHUNK, TM_MID), F32)
        for kk in range(TOP_K):
            p = jnp.where(lp[kk:kk + 1, :] == r, 1.0, p)
        xloc_ref[slot, c * ROW_CHUNK:(c + 1) * ROW_CHUNK, :] = _pack_pairs(_dot(p.astype(BF16), u))

    def start_runs(tile):
        s = tile % 2

        def body(e, carry):
            j = tile * N_EXPERTS + e
            _start_run(tcnt_ref[j], xloc_ref.at[s], lo_ref[j], xs_ref, gs_ref[j], sem.at[s])
            return carry
        lax.fori_loop(0, N_EXPERTS, body, 0)

    def wait_runs(tile):
        s = tile % 2
        last = tile * N_EXPERTS + N_EXPERTS - 1
        _wait_rows(lo_ref[last] + tcnt_ref[last], xloc_ref.at[s], xs_ref, sem.at[s])

    start_runs(i)

    @pl.when(i > 0)
    def _():
        wait_runs(i - 1)

    @pl.when(i == pl.num_programs(0) - 1)
    def _():
        wait_runs(i)
        zero_fill(True)


def _dispatch(tcnt, lo, gstart, padn, pads, n_active, u3, lp, n_rows):
    n = u3.shape[0]
    return pl.pallas_call(
        _dispatch_kernel,
        name="dispatch",
        grid_spec=pltpu.PrefetchScalarGridSpec(
            num_scalar_prefetch=6,
            grid=(n // TM_MID,),
            in_specs=[
                pl.BlockSpec((TM_MID, D_MODEL), lambda i, *_: (i, 0)),
                pl.BlockSpec((TOP_K, TM_MID), lambda i, *_: (0, i)),
            ],
            out_specs=pl.BlockSpec(memory_space=pl.ANY),
            scratch_shapes=[
                pltpu.VMEM((2, LOCAL_ROWS, PACKED), U32),
                pltpu.VMEM((MOE_ROWS, PACKED), U32),
                pltpu.SemaphoreType.DMA((2,)),
                pltpu.SemaphoreType.DMA,
            ],
        ),
        out_shape=jax.ShapeDtypeStruct((n_rows, PACKED), U32),
        compiler_params=_params(("arbitrary",)),
    )(tcnt, lo, gstart, padn, pads, n_active, u3, lp)


PAIR_COLS = 2 * LANES


def _expert_kernel(be_ref, nx_ref, nv_ref, na_ref, x_ref, b1_ref, b2_ref, perm_ref, w1_hbm, w2_hbm, y_ref,
                   w1s_ref, w2s_ref, w1p_ref, w2b_ref, sem):
    b = pl.program_id(0)
    valid_rows = nv_ref[b]
    active = valid_rows > 0
    expert = be_ref[b]
    new_expert = jnp.logical_or(b == 0, expert != be_ref[jnp.maximum(b - 1, 0)])

    def fetch(e, wait):
        for src, dst, s in ((w1_hbm, w1s_ref, 0), (w2_hbm, w2s_ref, 1)):
            cp = pltpu.make_async_copy(src.at[e], dst, sem.at[s])
            if wait:
                cp.wait()
            else:
                cp.start(priority=1)

    @pl.when(jnp.logical_and(b == 0, active))
    def _():
        fetch(expert, False)

    @pl.when(jnp.logical_and(active, new_expert))
    def _():
        fetch(expert, True)
        for c in range(2 * D_FF // PAIR_COLS):
            cs = slice(c * PAIR_COLS, (c + 1) * PAIR_COLS)
            w1p_ref[:, cs] = _dot(w1s_ref[:, cs].astype(BF16), perm_ref[...]).astype(BF16)
        w2b_ref[...] = w2s_ref[...].astype(BF16)

        @pl.when(nx_ref[b] >= 0)
        def _():
            fetch(nx_ref[b], False)

    for live in range(MOE_PART, MOE_ROWS + 1, MOE_PART):
        @pl.when(jnp.logical_and(valid_rows > live - MOE_PART, valid_rows <= live))
        def _():
            x = _unpack_pairs(x_ref[:live, :])
            hdn = _dot(x, w1p_ref[...]) + b1_ref[0]
            acts = []
            for c in range(2 * D_FF // PAIR_COLS):
                hg = jnp.minimum(hdn[:, c * PAIR_COLS:c * PAIR_COLS + LANES], SWIGLU_LIMIT)
                hl = jnp.clip(hdn[:, c * PAIR_COLS + LANES:(c + 1) * PAIR_COLS], -SWIGLU_LIMIT, SWIGLU_LIMIT)
                acts.append((hg * _sigmoid(SWIGLU_ALPHA * hg) * (hl + 1.0)).astype(BF16))
            y = _dot(jnp.concatenate(acts, axis=1), w2b_ref[...]) + b2_ref[0]
            y_ref[:live, :] = _pack_pairs(y.astype(BF16).astype(F32))
            if live < MOE_ROWS:
                y_ref[live:, :] = jnp.zeros((MOE_ROWS - live, PACKED), U32)

    @pl.when(jnp.logical_not(active))
    def _():
        y_ref[...] = jnp.zeros_like(y_ref)


def _experts(block_exp, next_exp, valid_rows, n_active, xs, w1, b1p, w2, b2):
    n_rows = xs.shape[0]
    n_blocks = n_rows // MOE_ROWS
    xmap = lambda b, be, nx, nv, na: (jnp.maximum(jnp.minimum(b, na[0] - 1), 0), 0)
    emap = lambda b, be, nx, nv, na: (be[b], 0, 0)
    perm = np.zeros((PAIR_COLS, PAIR_COLS), np.float32)
    perm[2 * np.arange(LANES), np.arange(LANES)] = 1.0
    perm[2 * np.arange(LANES) + 1, LANES + np.arange(LANES)] = 1.0
    return pl.pallas_call(
        _expert_kernel,
        name="experts",
        grid_spec=pltpu.PrefetchScalarGridSpec(
            num_scalar_prefetch=4,
            grid=(n_blocks,),
            in_specs=[
                pl.BlockSpec((MOE_ROWS, PACKED), xmap),
                pl.BlockSpec((1, 1, 2 * D_FF), emap),
                pl.BlockSpec((1, 1, D_MODEL), emap),
                pl.BlockSpec((PAIR_COLS, PAIR_COLS), lambda b, *_: (0, 0)),
                pl.BlockSpec(memory_space=pl.ANY),
                pl.BlockSpec(memory_space=pl.ANY),
            ],
            out_specs=pl.BlockSpec((MOE_ROWS, PACKED), lambda b, *_: (b, 0)),
            scratch_shapes=[
                pltpu.VMEM((D_MODEL, 2 * D_FF), F32),
                pltpu.VMEM((D_FF, D_MODEL), F32),
                pltpu.VMEM((D_MODEL, 2 * D_FF), BF16),
                pltpu.VMEM((D_FF, D_MODEL), BF16),
                pltpu.SemaphoreType.DMA((2,)),
            ],
        ),
        out_shape=jax.ShapeDtypeStruct((n_rows, PACKED), U32),
        compiler_params=_params(("arbitrary",)),
    )(block_exp, next_exp, valid_rows, n_active, xs, b1p, b2, jnp.asarray(perm).astype(BF16), w1, w2)


def _combine_kernel(tcnt_ref, lo_ref, gs_ref, h2_ref, gate_ref, lp_ref, g_ref, ys_ref, o_ref, yloc_ref, sem):
    i = pl.program_id(0)

    slot = i % 2

    def start_runs(tile):
        s = tile % 2

        def body(e, carry):
            j = tile * N_EXPERTS + e
            _start_run(tcnt_ref[j], ys_ref, gs_ref[j], yloc_ref.at[s], lo_ref[j], sem.at[s])
            return carry
        lax.fori_loop(0, N_EXPERTS, body, 0)

    @pl.when(i == 0)
    def _():
        yloc_ref[...] = jnp.zeros_like(yloc_ref)
        start_runs(i)

    @pl.when(i + 1 < pl.num_programs(0))
    def _():
        start_runs(i + 1)

    gates = gate_ref[...]
    lp = lp_ref[...]
    last = i * N_EXPERTS + N_EXPERTS - 1
    _wait_rows(lo_ref[last] + tcnt_ref[last], ys_ref, yloc_ref.at[slot], sem.at[slot])

    h = h2_ref[...]
    for c in range(LOCAL_ROWS // ROW_CHUNK):
        r = lax.broadcasted_iota(I32, (ROW_CHUNK, TM_MID), 0) + c * ROW_CHUNK
        a = jnp.zeros((ROW_CHUNK, TM_MID), F32)
        for kk in range(TOP_K):
            a = jnp.where(lp[kk:kk + 1, :] == r, gates[kk:kk + 1, :], a)
        h = h + _tn(a.astype(BF16), _unpack_pairs(yloc_ref[slot, c * ROW_CHUNK:(c + 1) * ROW_CHUNK, :]))
    o_ref[...] = _rms(h, g_ref[...])


def _combine(tcnt, lo, gstart, h2, gates, lp, g_final, ys):
    n = h2.shape[0]
    return pl.pallas_call(
        _combine_kernel,
        name="combine",
        grid_spec=pltpu.PrefetchScalarGridSpec(
            num_scalar_prefetch=3,
            grid=(n // TM_MID,),
            in_specs=[
                pl.BlockSpec((TM_MID, D_MODEL), lambda i, *_: (i, 0)),
                pl.BlockSpec((TOP_K, TM_MID), lambda i, *_: (0, i)),
                pl.BlockSpec((TOP_K, TM_MID), lambda i, *_: (0, i)),
                pl.BlockSpec((1, D_MODEL), lambda i, *_: (0, 0)),
                pl.BlockSpec(memory_space=pl.ANY),
            ],
            out_specs=pl.BlockSpec((TM_MID, D_MODEL), lambda i, *_: (i, 0)),
            scratch_shapes=[
                pltpu.VMEM((2, LOCAL_ROWS, PACKED), U32),
                pltpu.SemaphoreType.DMA((2,)),
            ],
        ),
        out_shape=jax.ShapeDtypeStruct((n, D_MODEL), F32),
        compiler_params=_params(("arbitrary",)),
    )(tcnt, lo, gstart, h2, gates, lp, g_final, ys)


def _layer(h, mem2, batch, seq, mem_len, g_mix, w_in, b_qkv, conv_w, conv_b, dt_bias, a_log, d_skip,
           g_ssm_out, attn_sinks, bias, w_out, b_out, g_cross, g_mem, w_q_cross, w_kv_cross, w_o_cross,
           g_ffn, w_router, b_router, w1, b1, w2, b2, g_final):
    n = h.shape[0]
    z, xbc, dtp, qkv = _inproj(h, g_mix[None, :], w_in, b_qkv[None, :])

    pad_h = (0, DT_PAD - SSM_HEADS)
    dt_bias_p = jnp.pad(dt_bias, pad_h)[None, :]
    a_head_p = jnp.pad(-jnp.exp(a_log), pad_h)[None, :]
    d_full = jnp.repeat(d_skip, SSM_HEAD_DIM)[None, :]
    per_seq = lambda t: t.reshape(batch, seq, t.shape[-1])
    y_ssm = _ssd(per_seq(xbc), per_seq(z), per_seq(dtp), conv_w, conv_b[None, :], dt_bias_p, a_head_p,
                 d_full, g_ssm_out[None, :]).reshape(n, SSM_WIDTH)
    y_attn = _swa(per_seq(qkv), attn_sinks, bias).reshape(n, ATTN_WIDTH)

    k_mem, v_mem = _memkv(mem2, g_mem[None, :], w_kv_cross)
    h2, u3, gates, lp, tcnt, tbase = _mid(
        y_ssm, y_attn, h, w_out, b_out[None, :], g_cross[None, :],
        w_q_cross, k_mem, v_mem, w_o_cross, g_ffn[None, :],
        w_router.T.astype(BF16), b_router[:, None], seq, mem_len)

    max_rows = n * TOP_K + (n // TM_MID) * N_EXPERTS * (SUBLANES - 1)
    n_blocks = -(-max_rows // MOE_ROWS) + N_EXPERTS
    tcnt = tcnt[:, :, 0]
    tbase = tbase[:, :, 0]
    counts = tbase[-1] + tcnt[-1]
    padded = (counts + MOE_ROWS - 1) // MOE_ROWS * MOE_ROWS
    padded_end = jnp.cumsum(padded)
    start = (padded_end - padded).astype(I32)
    n_active = (padded_end[-1] // MOE_ROWS).astype(I32)
    blk = jnp.arange(n_blocks, dtype=I32)
    blk_exp = jnp.sum((blk * MOE_ROWS)[:, None] >= padded_end[None, :], axis=1)
    eid = jnp.arange(N_EXPERTS, dtype=I32)
    last_exp = jnp.max(jnp.where(padded > 0, eid, 0))
    blk_exp = jnp.where(blk < n_active, jnp.minimum(blk_exp, N_EXPERTS - 1), last_exp).astype(I32)
    later = jnp.logical_and(padded[None, :] > 0, eid[None, :] > eid[:, None])
    next_of = jnp.min(jnp.where(later, eid[None, :], N_EXPERTS), axis=1)
    next_of = jnp.where(next_of < N_EXPERTS, next_of, -1).astype(I32)
    of_blk = blk_exp[:, None] == eid[None, :]
    blk_next = jnp.sum(jnp.where(of_blk, next_of[None, :], 0), axis=1).astype(I32)
    blk_end = jnp.sum(jnp.where(of_blk, (start + counts)[None, :], 0), axis=1)
    blk_valid = jnp.clip(blk_end - blk * MOE_ROWS, 0, MOE_ROWS)
    blk_valid = jnp.where(blk < n_active, blk_valid, 0).astype(I32)

    run_n = tcnt.reshape(-1)
    run_local = (jnp.cumsum(tcnt, axis=1) - tcnt).reshape(-1)
    run_global = (start[None, :] + tbase).reshape(-1)
    xs = _dispatch(run_n, run_local, run_global, padded - counts, start + counts, n_active[None], u3, lp,
                   n_blocks * MOE_ROWS)

    b1p = b1.reshape(N_EXPERTS, 2 * D_FF // PAIR_COLS, LANES, 2).transpose(0, 1, 3, 2)
    ys = _experts(blk_exp, blk_next, blk_valid, n_active[None], xs, w1,
                  b1p.reshape(N_EXPERTS, 1, 2 * D_FF), w2, b2[:, None, :])
    return _combine(run_n, run_local, run_global, h2, gates, lp, g_final, ys)


def kernel(x, mem, g_mix, w_in, b_qkv, conv_w, conv_b, dt_bias, a_log, d_skip, g_ssm_out, attn_sinks, rel_bias, w_out, b_out, g_cross, g_mem, w_q_cross, w_kv_cross, w_o_cross, g_ffn, w_router, b_router, w1, b1, w2, b2, g_final):
    batch, seq, d = x.shape
    mem_len = mem.shape[1]
    depth = g_mix.shape[0]
    assert depth == 1 and d == D_MODEL and seq % TM_MID == 0
    h = x.reshape(batch * seq, d)
    mem2 = mem.reshape(batch * mem_len, d)
    bias = _bias_table(rel_bias)
    out = _layer(
        h, mem2, batch, seq, mem_len, g_mix[0], w_in[0], b_qkv[0], conv_w[0], conv_b[0], dt_bias[0],
        a_log[0], d_skip[0], g_ssm_out[0], attn_sinks[0], bias, w_out[0], b_out[0], g_cross[0],
        g_mem[0], w_q_cross[0], w_kv_cross[0], w_o_cross[0], g_ffn[0], w_router[0], b_router[0],
        w1[0], b1[0], w2[0], b2[0], g_final[None, :])
    return out.reshape(batch, seq, d)
```

```python
import math

import numpy as np
import jax
import jax.numpy as jnp
from jax import lax
from jax.experimental import pallas as pl
from jax.experimental.pallas import tpu as pltpu

F32 = jnp.float32
BF16 = jnp.bfloat16
I32 = jnp.int32

D_MODEL = 1024
RMS_EPS = 1e-5

SSM_HEADS = 8
SSM_HEAD_DIM = 64
SSM_WIDTH = SSM_HEADS * SSM_HEAD_DIM
SSM_STATE = 128
SSM_GROUPS = 2
HEADS_PER_GROUP = SSM_HEADS // SSM_GROUPS
GROUP_WIDTH = SSM_WIDTH // SSM_GROUPS
CONV_WIDTH = 4
CHUNK = 128
CONV_CH = SSM_WIDTH + 2 * SSM_GROUPS * SSM_STATE

ATTN_HEADS = 8
KV_HEADS = 2
Q_PER_KV = ATTN_HEADS // KV_HEADS
HEAD_DIM = 64
ATTN_WIDTH = ATTN_HEADS * HEAD_DIM
KV_WIDTH = KV_HEADS * HEAD_DIM
WINDOW = 128
BLOCK = WINDOW
QKV_COLS = ATTN_WIDTH + 2 * KV_WIDTH

REL_BUCKETS = 32
REL_MAX_DIST = 128

CROSS_HEADS = 4
CROSS_HEAD_DIM = 128
CROSS_WIDTH = CROSS_HEADS * CROSS_HEAD_DIM

N_EXPERTS = 32
TOP_K = 4
D_FF = D_MODEL
SWIGLU_ALPHA = 1.702
SWIGLU_LIMIT = 7.0

LANES = 128
SUBLANES = 8
DT_PAD = LANES
PROJ_COLS = SSM_WIDTH + CONV_CH + DT_PAD + QKV_COLS
VMEM_LIMIT = 56 * 1024 * 1024

TM_PROJ = 1024
TM_MID = 512
SSD_ROWS = 4 * CHUNK
MOE_ROWS = 1024
MOE_PART = 128


def _nt(a, b):
    return lax.dot_general(a, b, (((1,), (1,)), ((), ())), preferred_element_type=F32)


def _tn(a, b):
    return lax.dot_general(a, b, (((0,), (0,)), ((), ())), preferred_element_type=F32)


def _dot(a, b):
    return jnp.dot(a, b, preferred_element_type=F32)


def _rms(x, g):
    return x * lax.rsqrt(jnp.mean(x * x, axis=-1, keepdims=True) + RMS_EPS) * g


def _sigmoid(x):
    return 0.5 * jnp.tanh(0.5 * x) + 0.5


def _params(sem=None, **kwargs):
    return pltpu.CompilerParams(dimension_semantics=sem, vmem_limit_bytes=VMEM_LIMIT, **kwargs)


def _inproj_kernel(x_ref, g_ref, w_f32, bq_ref, z_ref, xbc_ref, dt_ref, qkv_ref, w_ref):
    @pl.when(pl.program_id(0) == 0)
    def _():
        c1 = SSM_WIDTH + CONV_CH
        w_ref[:, :c1] = w_f32[:, :c1].astype(BF16)
        lane = lax.broadcasted_iota(I32, (D_MODEL, DT_PAD), 1)
        w_ref[:, c1:c1 + DT_PAD] = jnp.where(lane < SSM_HEADS, w_f32[:, c1:c1 + DT_PAD], 0.0).astype(BF16)
        w_ref[:, c1 + DT_PAD:] = w_f32[:, c1 + SSM_HEADS:].astype(BF16)

    u = _rms(x_ref[...], g_ref[...])
    p = _dot(u.astype(BF16), w_ref[...])
    z_ref[...] = p[:, :SSM_WIDTH]
    xbc_ref[...] = p[:, SSM_WIDTH:SSM_WIDTH + CONV_CH]
    dt_ref[...] = p[:, SSM_WIDTH + CONV_CH:SSM_WIDTH + CONV_CH + DT_PAD]
    qkv_ref[...] = p[:, SSM_WIDTH + CONV_CH + DT_PAD:] + bq_ref[...]


def _inproj(x2, g_mix, w_in, b_qkv):
    n = x2.shape[0]
    row = lambda i: (i, 0)
    fixed = lambda i: (0, 0)
    return pl.pallas_call(
        _inproj_kernel,
        name="inproj",
        grid=(n // TM_PROJ,),
        in_specs=[
            pl.BlockSpec((TM_PROJ, D_MODEL), row),
            pl.BlockSpec((1, D_MODEL), fixed),
            pl.BlockSpec(w_in.shape, fixed),
            pl.BlockSpec((1, QKV_COLS), fixed),
        ],
        out_specs=[
            pl.BlockSpec((TM_PROJ, SSM_WIDTH), row),
            pl.BlockSpec((TM_PROJ, CONV_CH), row),
            pl.BlockSpec((TM_PROJ, DT_PAD), row),
            pl.BlockSpec((TM_PROJ, QKV_COLS), row),
        ],
        out_shape=[
            jax.ShapeDtypeStruct((n, SSM_WIDTH), F32),
            jax.ShapeDtypeStruct((n, CONV_CH), F32),
            jax.ShapeDtypeStruct((n, DT_PAD), F32),
            jax.ShapeDtypeStruct((n, QKV_COLS), F32),
        ],
        scratch_shapes=[pltpu.VMEM((D_MODEL, PROJ_COLS), BF16)],
        compiler_params=_params(("arbitrary",)),
    )(x2, g_mix, w_in, b_qkv)


def _split3(x):
    hi = x.astype(BF16)
    rest = x - hi.astype(F32)
    mid = rest.astype(BF16)
    lo = (rest - mid.astype(F32)).astype(BF16)
    return jnp.concatenate([hi, mid, lo], axis=1)


def _ssd_kernel(xbc_ref, z_ref, dt_ref, cw_ref, cb_ref, dtb_ref, ah_ref, dsk_ref, gout_ref,
                e64_ref, ecat_ref, tri_ref, y_ref, conv_ref, state_ref):
    c = pl.program_id(0)

    @pl.when(c == 0)
    def _():
        conv_ref[:, :SUBLANES, :] = jnp.zeros((conv_ref.shape[0], SUBLANES, CONV_CH), F32)
        state_ref[...] = jnp.zeros_like(state_ref)

    for cc in range(xbc_ref.shape[1] // CHUNK):
        for b in range(xbc_ref.shape[0]):
            _ssd_chunk(b, slice(cc * CHUNK, (cc + 1) * CHUNK), xbc_ref, z_ref, dt_ref, cw_ref, cb_ref, dtb_ref,
                       ah_ref, dsk_ref, gout_ref, e64_ref, ecat_ref, tri_ref, y_ref, conv_ref, state_ref)


def _ssd_chunk(b, rows, xbc_ref, z_ref, dt_ref, cw_ref, cb_ref, dtb_ref, ah_ref, dsk_ref, gout_ref,
               e64_ref, ecat_ref, tri_ref, y_ref, conv_ref, state_ref):
    u = xbc_ref[b, rows, :]
    conv_ref[b, SUBLANES:, :] = u
    acc = u * cw_ref[CONV_WIDTH - 1:CONV_WIDTH, :] + cb_ref[...]
    for j in range(1, CONV_WIDTH):
        shifted = conv_ref[b, SUBLANES - j:SUBLANES - j + CHUNK, :]
        acc = acc + shifted * cw_ref[CONV_WIDTH - 1 - j:CONV_WIDTH - j, :]
    conv_ref[b, :SUBLANES, :] = u[CHUNK - SUBLANES:, :]
    xbc = acc * _sigmoid(acc)

    xs = xbc[:, :SSM_WIDTH]
    bm = xbc[:, SSM_WIDTH:SSM_WIDTH + SSM_GROUPS * SSM_STATE]
    cm = xbc[:, SSM_WIDTH + SSM_GROUPS * SSM_STATE:]

    dtr = dt_ref[b, rows, :] + dtb_ref[...]
    dt = jnp.maximum(dtr, 0.0) + jnp.log1p(jnp.exp(-jnp.abs(dtr)))
    a = dt * ah_ref[...]
    a3 = _split3(a)
    a_rows = jnp.concatenate([a3[:, :DT_PAD], a3[:, DT_PAD:2 * DT_PAD], a3[:, 2 * DT_PAD:]], axis=0)
    cs = _dot(tri_ref[...], a_rows)
    cs_row = cs.T
    dt_full = _dot(_split3(dt), e64_ref[...])
    cs_cat = _dot(_split3(cs), ecat_ref[...])
    cs_wide = cs_cat[:, :SSM_HEADS * CHUNK]
    cs_full = cs_cat[:, SSM_HEADS * CHUNK:]
    cs_last = cs_full[CHUNK - 1:CHUNK, :]

    xdt = xs * dt_full
    xdt_b = xdt.astype(BF16)
    xw_b = (xdt * jnp.exp(cs_last - cs_full)).astype(BF16)
    in_decay = jnp.exp(cs_full)
    chunk_decay = jnp.exp(cs_last)

    li = lax.broadcasted_iota(I32, (CHUNK, CHUNK), 0)
    si = lax.broadcasted_iota(I32, (CHUNK, CHUNK), 1)
    causal = li >= si

    ys = []
    for g in range(SSM_GROUPS):
        bg = bm[:, g * SSM_STATE:(g + 1) * SSM_STATE].astype(BF16)
        cg = cm[:, g * SSM_STATE:(g + 1) * SSM_STATE].astype(BF16)
        cb = _nt(cg, bg)
        yd = []
        for r in range(HEADS_PER_GROUP):
            h = g * HEADS_PER_GROUP + r
            diff = cs_wide[:, h * CHUNK:(h + 1) * CHUNK] - cs_row[h:h + 1, :]
            m = cb * jnp.exp(jnp.where(causal, diff, -jnp.inf))
            yd.append(_dot(m.astype(BF16), xdt_b[:, h * SSM_HEAD_DIM:(h + 1) * SSM_HEAD_DIM]))
        y_diag = jnp.concatenate(yd, axis=1)
        gs = slice(g * GROUP_WIDTH, (g + 1) * GROUP_WIDTH)
        st = state_ref[b, g]
        y_off = _dot(cg, st.astype(BF16)) * in_decay[:, gs]
        state_ref[b, g] = st * chunk_decay[:, gs] + _tn(bg, xw_b[:, gs])
        ys.append(y_diag + y_off)
    y = jnp.concatenate(ys, axis=1) + dsk_ref[...] * xs

    zz = z_ref[b, rows, :]
    y = y * (zz * _sigmoid(zz))
    outs = []
    for g in range(SSM_GROUPS):
        yg = y[:, g * GROUP_WIDTH:(g + 1) * GROUP_WIDTH]
        outs.append(yg * lax.rsqrt(jnp.mean(yg * yg, axis=-1, keepdims=True) + RMS_EPS))
    y_ref[b, rows, :] = jnp.concatenate(outs, axis=1) * gout_ref[...]


def _ssd(xbc, z, dtp, conv_w, conv_b, dt_bias_p, a_head_p, d_full, g_out):
    batch, seq, _ = xbc.shape
    chunk = lambda c: (0, c, 0)
    fixed = lambda c: (0, 0)
    head = np.arange(SSM_WIDTH) // SSM_HEAD_DIM
    e64 = (np.arange(DT_PAD)[:, None] == head[None, :]).astype(np.float32)
    head_w = np.arange(SSM_HEADS * CHUNK) // CHUNK
    e128 = (np.arange(DT_PAD)[:, None] == head_w[None, :]).astype(np.float32)
    e64_3 = jnp.asarray(np.tile(e64, (3, 1))).astype(BF16)
    ecat_3 = jnp.asarray(np.tile(np.concatenate([e128, e64], axis=1), (3, 1))).astype(BF16)
    tri_3 = jnp.asarray(np.tile(np.tril(np.ones((CHUNK, CHUNK), np.float32)), (1, 3))).astype(BF16)
    return pl.pallas_call(
        _ssd_kernel,
        name="ssd",
        grid=(seq // SSD_ROWS,),
        in_specs=[
            pl.BlockSpec((batch, SSD_ROWS, CONV_CH), chunk),
            pl.BlockSpec((batch, SSD_ROWS, SSM_WIDTH), chunk),
            pl.BlockSpec((batch, SSD_ROWS, DT_PAD), chunk),
            pl.BlockSpec((CONV_WIDTH, CONV_CH), fixed),
            pl.BlockSpec((1, CONV_CH), fixed),
            pl.BlockSpec((1, DT_PAD), fixed),
            pl.BlockSpec((1, DT_PAD), fixed),
            pl.BlockSpec((1, SSM_WIDTH), fixed),
            pl.BlockSpec((1, SSM_WIDTH), fixed),
            pl.BlockSpec((3 * DT_PAD, SSM_WIDTH), fixed),
            pl.BlockSpec((3 * DT_PAD, SSM_HEADS * CHUNK + SSM_WIDTH), fixed),
            pl.BlockSpec((CHUNK, 3 * CHUNK), fixed),
        ],
        out_specs=pl.BlockSpec((batch, SSD_ROWS, SSM_WIDTH), chunk),
        out_shape=jax.ShapeDtypeStruct((batch, seq, SSM_WIDTH), F32),
        scratch_shapes=[
            pltpu.VMEM((batch, SUBLANES + CHUNK, CONV_CH), F32),
            pltpu.VMEM((batch, SSM_GROUPS, SSM_STATE, GROUP_WIDTH), F32),
        ],
        compiler_params=_params(("arbitrary",)),
    )(xbc, z, dtp, conv_w, conv_b, dt_bias_p, a_head_p, d_full, g_out, e64_3, ecat_3, tri_3)


def _t5_bucket_table():
    q_pos = np.arange(BLOCK)[:, None]
    k_pos = np.arange(2 * BLOCK)[None, :] - BLOCK
    dist = q_pos - k_pos
    d = np.maximum(dist, 0)
    max_exact = REL_BUCKETS // 2
    ratio = np.maximum(d, max_exact).astype(np.float32) / np.float32(max_exact)
    large = max_exact + (np.log(ratio) / np.float32(math.log(REL_MAX_DIST / max_exact))
                         * np.float32(REL_BUCKETS - max_exact)).astype(np.int32)
    large = np.minimum(large, REL_BUCKETS - 1)
    bucket = np.where(d < max_exact, d, large)
    in_window = (dist >= 0) & (dist < WINDOW)
    return np.where(in_window, bucket, -1).astype(np.int32)


def _bias_kernel(rb_ref, bucket_ref, out_ref):
    bucket = bucket_ref[...]
    for h in range(ATTN_HEADS):
        acc = jnp.zeros(bucket.shape, F32)
        for b in range(REL_BUCKETS):
            acc = jnp.where(bucket == b, rb_ref[b, h], acc)
        out_ref[h] = jnp.where(bucket >= 0, acc, -jnp.inf)


def _bias_table(rel_bias):
    bucket = jnp.asarray(_t5_bucket_table())
    return pl.pallas_call(
        _bias_kernel,
        name="bias_table",
        in_specs=[
            pl.BlockSpec(memory_space=pltpu.SMEM),
            pl.BlockSpec(memory_space=pltpu.VMEM),
        ],
        out_specs=pl.BlockSpec(memory_space=pltpu.VMEM),
        out_shape=jax.ShapeDtypeStruct((ATTN_HEADS, BLOCK, 2 * BLOCK), F32),
    )(rel_bias, bucket)


def _swa_kernel(sink_ref, q_ref, k_ref, v_ref, kp_ref, vp_ref, bias_ref, o_ref):
    i = pl.program_id(0)
    col = lax.broadcasted_iota(I32, (BLOCK, 2 * BLOCK), 1)
    key_ok = jnp.logical_or(col >= BLOCK, i > 0)
    for b in range(q_ref.shape[0]):
        q = (q_ref[b] * (HEAD_DIM ** -0.5)).astype(BF16)
        outs = []
        for g in range(KV_HEADS):
            gs = slice(g * HEAD_DIM, (g + 1) * HEAD_DIM)
            k2 = jnp.concatenate([kp_ref[b, :, gs], k_ref[b, :, gs]], axis=0).astype(BF16)
            v2 = jnp.concatenate([vp_ref[b, :, gs], v_ref[b, :, gs]], axis=0).astype(BF16)
            for r in range(Q_PER_KV):
                h = g * Q_PER_KV + r
                s = _nt(q[:, h * HEAD_DIM:(h + 1) * HEAD_DIM], k2) + bias_ref[h]
                s = jnp.where(key_ok, s, -jnp.inf)
                sink = sink_ref[h]
                m = jnp.maximum(jnp.max(s, axis=-1, keepdims=True), sink)
                p = jnp.exp(s - m)
                denom = jnp.sum(p, axis=-1, keepdims=True) + jnp.exp(sink - m)
                outs.append(_dot(p.astype(BF16), v2) / denom)
        o_ref[b] = jnp.concatenate(outs, axis=1)


def _swa(qkv, sinks, bias):
    batch, seq, _ = qkv.shape
    kcol = ATTN_WIDTH // KV_WIDTH
    prev = lambda i: jnp.maximum(i - 1, 0)
    return pl.pallas_call(
        _swa_kernel,
        name="swa",
        grid_spec=pltpu.PrefetchScalarGridSpec(
            num_scalar_prefetch=1,
            grid=(seq // BLOCK,),
            in_specs=[
                pl.BlockSpec((batch, BLOCK, ATTN_WIDTH), lambda i, s: (0, i, 0)),
                pl.BlockSpec((batch, BLOCK, KV_WIDTH), lambda i, s: (0, i, kcol)),
                pl.BlockSpec((batch, BLOCK, KV_WIDTH), lambda i, s: (0, i, kcol + 1)),
                pl.BlockSpec((batch, BLOCK, KV_WIDTH), lambda i, s: (0, prev(i), kcol)),
                pl.BlockSpec((batch, BLOCK, KV_WIDTH), lambda i, s: (0, prev(i), kcol + 1)),
                pl.BlockSpec((ATTN_HEADS, BLOCK, 2 * BLOCK), lambda i, s: (0, 0, 0)),
            ],
            out_specs=pl.BlockSpec((batch, BLOCK, ATTN_WIDTH), lambda i, s: (0, i, 0)),
        ),
        out_shape=jax.ShapeDtypeStruct((batch, seq, ATTN_WIDTH), F32),
        compiler_params=_params(("arbitrary",)),
    )(sinks, qkv, qkv, qkv, qkv, qkv, bias)


def _memkv_kernel(m_ref, g_ref, w_ref, k_ref, v_ref):
    u = _rms(m_ref[...], g_ref[...])
    kv = _dot(u.astype(BF16), w_ref[...].astype(BF16))
    k_ref[...] = kv[:, :CROSS_WIDTH].astype(BF16)
    v_ref[...] = kv[:, CROSS_WIDTH:].astype(BF16)


def _memkv(mem2, g_mem, w_kv):
    n = mem2.shape[0]
    return pl.pallas_call(
        _memkv_kernel,
        name="memkv",
        out_shape=[jax.ShapeDtypeStruct((n, CROSS_WIDTH), BF16)] * 2,
        compiler_params=_params(),
    )(mem2, g_mem, w_kv)


def _mid_kernel(ys_ref, ya_ref, x_ref, wout_f32, bout_ref, gc_ref, wq_f32, k_ref, v_ref, wo_f32,
                gf_ref, wr_ref, br_ref, upper_ref, lower_ref,
                h2_ref, u3_ref, gate_ref, lp_ref, tcnt_ref, tbase_ref, run_ref, wout_ref, wq_ref, wo_ref):
    i = pl.program_id(0)

    @pl.when(i == 0)
    def _():
        run_ref[...] = jnp.zeros_like(run_ref)
        wout_ref[...] = wout_f32[...].astype(BF16)
        wq_ref[...] = wq_f32[...].astype(BF16)
        wo_ref[...] = wo_f32[...].astype(BF16)

    ycat = jnp.concatenate([ys_ref[...], ya_ref[...]], axis=1).astype(BF16)
    h1 = x_ref[...] + _dot(ycat, wout_ref[...]) + bout_ref[...]

    u2 = _rms(h1, gc_ref[...])
    q = _dot(u2.astype(BF16), wq_ref[...]).astype(BF16)
    outs = []
    for h in range(CROSS_HEADS):
        hs = slice(h * CROSS_HEAD_DIM, (h + 1) * CROSS_HEAD_DIM)
        s = _nt(q[:, hs], k_ref[:, hs]) * (CROSS_HEAD_DIM ** -0.5)
        m = jnp.max(s, axis=-1, keepdims=True)
        p = jnp.exp(s - m)
        outs.append(_dot(p.astype(BF16), v_ref[:, hs]) / jnp.sum(p, axis=-1, keepdims=True))
    o = jnp.concatenate(outs, axis=1).astype(BF16)
    h2 = h1 + _dot(o, wo_ref[...])
    h2_ref[...] = h2
    u3 = _rms(h2, gf_ref[...]).astype(BF16)
    u3_ref[...] = u3

    logits = _nt(wr_ref[...], u3) + br_ref[...]
    eid = lax.broadcasted_iota(I32, logits.shape, 0)
    vals, idxs, hots = [], [], []
    for _ in range(TOP_K):
        m = jnp.max(logits, axis=0, keepdims=True)
        ix = jnp.min(jnp.where(logits == m, eid, N_EXPERTS), axis=0, keepdims=True)
        hot = eid == ix
        logits = jnp.where(hot, -jnp.inf, logits)
        vals.append(m)
        idxs.append(ix)
        hots.append(hot)
    ex = [jnp.exp(v - vals[0]) for v in vals]
    tot = ex[0] + ex[1] + ex[2] + ex[3]
    gate_ref[...] = jnp.concatenate([e / tot for e in ex], axis=0)

    hot_f = [jnp.where(hot, 1.0, 0.0) for hot in hots]
    cnt_k = [jnp.sum(hf, axis=1, keepdims=True) for hf in hot_f]
    tile_cnt = cnt_k[0] + cnt_k[1] + cnt_k[2] + cnt_k[3]
    tile_cnt = jnp.floor((tile_cnt + (SUBLANES - 1)) * (1.0 / SUBLANES)) * SUBLANES
    units = jnp.broadcast_to(tile_cnt * (1.0 / SUBLANES), (N_EXPERTS, LANES)).astype(BF16)
    offs = _dot(lower_ref[...], units)[:, :1] * SUBLANES
    local = []
    before_all = _dot(jnp.concatenate(hot_f, axis=0).astype(BF16), upper_ref[...])
    for kk in range(TOP_K):
        before = before_all[kk * N_EXPERTS:(kk + 1) * N_EXPERTS, :]
        local.append(jnp.sum(jnp.where(hots[kk], before + offs, 0.0), axis=0, keepdims=True))
        offs = offs + cnt_k[kk]
    lp_ref[...] = jnp.concatenate(local, axis=0).astype(I32)
    tcnt_ref[0] = jnp.broadcast_to(tile_cnt, (N_EXPERTS, LANES)).astype(I32)
    tbase_ref[0] = jnp.broadcast_to(run_ref[...], (N_EXPERTS, LANES)).astype(I32)
    run_ref[...] = run_ref[...] + tile_cnt


def _mid(y_ssm, y_attn, x2, w_out, b_out, g_cross, w_q, k_mem, v_mem, w_o, g_ffn, w_r_t, b_r, seq, mem_len):
    n = x2.shape[0]
    tiles_per_batch = seq // TM_MID
    row = lambda i: (i, 0)
    col = lambda i: (0, i)
    fixed = lambda i: (0, 0)
    memb = lambda i: (i // tiles_per_batch, 0)
    tile3 = lambda i: (i, 0, 0)
    n_tiles = n // TM_MID
    upper = jnp.asarray(np.triu(np.ones((TM_MID, TM_MID), np.float32), 1)).astype(BF16)
    lower = jnp.asarray(np.tril(np.ones((N_EXPERTS, N_EXPERTS), np.float32), -1)).astype(BF16)
    return pl.pallas_call(
        _mid_kernel,
        name="mid",
        grid=(n_tiles,),
        in_specs=[
            pl.BlockSpec((TM_MID, SSM_WIDTH), row),
            pl.BlockSpec((TM_MID, ATTN_WIDTH), row),
            pl.BlockSpec((TM_MID, D_MODEL), row),
            pl.BlockSpec((D_MODEL, D_MODEL), fixed),
            pl.BlockSpec((1, D_MODEL), fixed),
            pl.BlockSpec((1, D_MODEL), fixed),
            pl.BlockSpec((D_MODEL, CROSS_WIDTH), fixed),
            pl.BlockSpec((mem_len, CROSS_WIDTH), memb),
            pl.BlockSpec((mem_len, CROSS_WIDTH), memb),
            pl.BlockSpec((CROSS_WIDTH, D_MODEL), fixed),
            pl.BlockSpec((1, D_MODEL), fixed),
            pl.BlockSpec((N_EXPERTS, D_MODEL), fixed),
            pl.BlockSpec((N_EXPERTS, 1), fixed),
            pl.BlockSpec((TM_MID, TM_MID), fixed),
            pl.BlockSpec((N_EXPERTS, N_EXPERTS), fixed),
        ],
        out_specs=[
            pl.BlockSpec((TM_MID, D_MODEL), row),
            pl.BlockSpec((TM_MID, D_MODEL), row),
            pl.BlockSpec((TOP_K, TM_MID), col),
            pl.BlockSpec((TOP_K, TM_MID), col),
            pl.BlockSpec((1, N_EXPERTS, LANES), tile3),
            pl.BlockSpec((1, N_EXPERTS, LANES), tile3),
        ],
        out_shape=[
            jax.ShapeDtypeStruct((n, D_MODEL), F32),
            jax.ShapeDtypeStruct((n, D_MODEL), BF16),
            jax.ShapeDtypeStruct((TOP_K, n), F32),
            jax.ShapeDtypeStruct((TOP_K, n), I32),
            jax.ShapeDtypeStruct((n_tiles, N_EXPERTS, LANES), I32),
            jax.ShapeDtypeStruct((n_tiles, N_EXPERTS, LANES), I32),
        ],
        scratch_shapes=[
            pltpu.VMEM((N_EXPERTS, 1), F32),
            pltpu.VMEM((D_MODEL, D_MODEL), BF16),
            pltpu.VMEM((D_MODEL, CROSS_WIDTH), BF16),
            pltpu.VMEM((CROSS_WIDTH, D_MODEL), BF16),
        ],
        compiler_params=_params(("arbitrary",)),
    )(y_ssm, y_attn, x2, w_out, b_out, g_cross, w_q, k_mem, v_mem, w_o, g_ffn, w_r_t, b_r, upper, lower)


ROW_CHUNK = 256
LOCAL_ROWS = -(-(TM_MID * TOP_K + N_EXPERTS * (SUBLANES - 1)) // ROW_CHUNK) * ROW_CHUNK


U32 = jnp.uint32
PACKED = D_MODEL // 2


def _pack_pairs(v):
    bits = lax.bitcast_convert_type(v, U32)
    return bits[:, PACKED:] | (bits[:, :PACKED] >> 16)


def _unpack_pairs(w):
    lo = lax.bitcast_convert_type(w << 16, F32)
    hi = lax.bitcast_convert_type(w & jnp.uint32(0xFFFF0000), F32)
    return jnp.concatenate([lo, hi], axis=1).astype(BF16)


def _row_copy(n, src_ref, src, dst_ref, dst, sem):
    aligned = lambda v: v if isinstance(v, int) else pl.multiple_of(v, SUBLANES)
    return pltpu.make_async_copy(src_ref.at[pl.ds(aligned(src), aligned(n))],
                                 dst_ref.at[pl.ds(aligned(dst), aligned(n))], sem)


def _start_run(n, src_ref, src, dst_ref, dst, sem):
    @pl.when(n > 0)
    def _():
        _row_copy(n, src_ref, src, dst_ref, dst, sem).start()


def _wait_rows(n, src_ref, dst_ref, sem):
    _row_copy(n, src_ref, 0, dst_ref, 0, sem).wait()


def _dispatch_kernel(tcnt_ref, lo_ref, gs_ref, padn_ref, pads_ref, na_ref, u_ref, lp_ref, xs_ref,
                     xloc_ref, zero_ref, sem, zsem):
    i = pl.program_id(0)

    def zero_fill(wait):
        def fill(n, dst):
            cp = _row_copy(n, zero_ref, 0, xs_ref, dst, zsem)
            if wait:
                cp.wait()
            else:
                cp.start(priority=1)

        def body(e, carry):
            pl.when(padn_ref[e] > 0)(lambda: fill(padn_ref[e], pads_ref[e]))
            return carry
        lax.fori_loop(0, N_EXPERTS, body, 0)

        def tail(b, carry):
            fill(MOE_ROWS, b * MOE_ROWS)
            return carry
        lax.fori_loop(na_ref[0], xs_ref.shape[0] // MOE_ROWS, tail, 0)

    @pl.when(i == 0)
    def _():
        zero_ref[...] = jnp.zeros_like(zero_ref)
        zero_fill(False)

    slot = i % 2
    u = u_ref[...]
    lp = lp_ref[...]
    for c in range(LOCAL_ROWS // ROW_CHUNK):
        r = lax.broadcasted_iota(I32, (ROW_CHUNK, TM_MID), 0) + c * ROW_CHUNK
        p = jnp.zeros((ROW_CHUNK, TM_MID), F32)
        for kk in range(TOP_K):
            p = jnp.where(lp[kk:kk + 1, :] == r, 1.0, p)
        xloc_ref[slot, c * ROW_CHUNK:(c + 1) * ROW_CHUNK, :] = _pack_pairs(_dot(p.astype(BF16), u))

    def start_runs(tile):
        s = tile % 2

        def body(e, carry):
            j = tile * N_EXPERTS + e
            _start_run(tcnt_ref[j], xloc_ref.at[s], lo_ref[j], xs_ref, gs_ref[j], sem.at[s])
            return carry
        lax.fori_loop(0, N_EXPERTS, body, 0)

    def wait_runs(tile):
        s = tile % 2
        last = tile * N_EXPERTS + N_EXPERTS - 1
        _wait_rows(lo_ref[last] + tcnt_ref[last], xloc_ref.at[s], xs_ref, sem.at[s])

    start_runs(i)

    @pl.when(i > 0)
    def _():
        wait_runs(i - 1)

    @pl.when(i == pl.num_programs(0) - 1)
    def _():
        wait_runs(i)
        zero_fill(True)


def _dispatch(tcnt, lo, gstart, padn, pads, n_active, u3, lp, n_rows):
    n = u3.shape[0]
    return pl.pallas_call(
        _dispatch_kernel,
        name="dispatch",
        grid_spec=pltpu.PrefetchScalarGridSpec(
            num_scalar_prefetch=6,
            grid=(n // TM_MID,),
            in_specs=[
                pl.BlockSpec((TM_MID, D_MODEL), lambda i, *_: (i, 0)),
                pl.BlockSpec((TOP_K, TM_MID), lambda i, *_: (0, i)),
            ],
            out_specs=pl.BlockSpec(memory_space=pl.ANY),
            scratch_shapes=[
                pltpu.VMEM((2, LOCAL_ROWS, PACKED), U32),
                pltpu.VMEM((MOE_ROWS, PACKED), U32),
                pltpu.SemaphoreType.DMA((2,)),
                pltpu.SemaphoreType.DMA,
            ],
        ),
        out_shape=jax.ShapeDtypeStruct((n_rows, PACKED), U32),
        compiler_params=_params(("arbitrary",)),
    )(tcnt, lo, gstart, padn, pads, n_active, u3, lp)


PAIR_COLS = 2 * LANES


def _expert_kernel(be_ref, nx_ref, nv_ref, na_ref, x_ref, b1_ref, b2_ref, perm_ref, w1_hbm, w2_hbm, y_ref,
                   w1s_ref, w2s_ref, w1p_ref, w2b_ref, sem):
    b = pl.program_id(0)
    valid_rows = nv_ref[b]
    active = valid_rows > 0
    expert = be_ref[b]
    new_expert = jnp.logical_or(b == 0, expert != be_ref[jnp.maximum(b - 1, 0)])

    def fetch(e, wait):
        for src, dst, s in ((w1_hbm, w1s_ref, 0), (w2_hbm, w2s_ref, 1)):
            cp = pltpu.make_async_copy(src.at[e], dst, sem.at[s])
            if wait:
                cp.wait()
            else:
                cp.start()

    @pl.when(jnp.logical_and(b == 0, active))
    def _():
        fetch(expert, False)

    @pl.when(jnp.logical_and(active, new_expert))
    def _():
        fetch(expert, True)
        for c in range(2 * D_FF // PAIR_COLS):
            cs = slice(c * PAIR_COLS, (c + 1) * PAIR_COLS)
            w1p_ref[:, cs] = _dot(w1s_ref[:, cs].astype(BF16), perm_ref[...]).astype(BF16)
        w2b_ref[...] = w2s_ref[...].astype(BF16)

        @pl.when(nx_ref[b] >= 0)
        def _():
            fetch(nx_ref[b], False)

    for live in range(MOE_PART, MOE_ROWS + 1, MOE_PART):
        @pl.when(jnp.logical_and(valid_rows > live - MOE_PART, valid_rows <= live))
        def _():
            x = _unpack_pairs(x_ref[:live, :])
            hdn = _dot(x, w1p_ref[...]) + b1_ref[0]
            acts = []
            for c in range(2 * D_FF // PAIR_COLS):
                hg = jnp.minimum(hdn[:, c * PAIR_COLS:c * PAIR_COLS + LANES], SWIGLU_LIMIT)
                hl = jnp.clip(hdn[:, c * PAIR_COLS + LANES:(c + 1) * PAIR_COLS], -SWIGLU_LIMIT, SWIGLU_LIMIT)
                acts.append((hg * _sigmoid(SWIGLU_ALPHA * hg) * (hl + 1.0)).astype(BF16))
            y = _dot(jnp.concatenate(acts, axis=1), w2b_ref[...]) + b2_ref[0]
            y_ref[:live, :] = _pack_pairs(y.astype(BF16).astype(F32))
            if live < MOE_ROWS:
                y_ref[live:, :] = jnp.zeros((MOE_ROWS - live, PACKED), U32)

    @pl.when(jnp.logical_not(active))
    def _():
        y_ref[...] = jnp.zeros_like(y_ref)


def _experts(block_exp, next_exp, valid_rows, n_active, xs, w1, b1p, w2, b2):
    n_rows = xs.shape[0]
    n_blocks = n_rows // MOE_ROWS
    xmap = lambda b, be, nx, nv, na: (jnp.maximum(jnp.minimum(b, na[0] - 1), 0), 0)
    emap = lambda b, be, nx, nv, na: (be[b], 0, 0)
    perm = np.zeros((PAIR_COLS, PAIR_COLS), np.float32)
    perm[2 * np.arange(LANES), np.arange(LANES)] = 1.0
    perm[2 * np.arange(LANES) + 1, LANES + np.arange(LANES)] = 1.0
    return pl.pallas_call(
        _expert_kernel,
        name="experts",
        grid_spec=pltpu.PrefetchScalarGridSpec(
            num_scalar_prefetch=4,
            grid=(n_blocks,),
            in_specs=[
                pl.BlockSpec((MOE_ROWS, PACKED), xmap),
                pl.BlockSpec((1, 1, 2 * D_FF), emap),
                pl.BlockSpec((1, 1, D_MODEL), emap),
                pl.BlockSpec((PAIR_COLS, PAIR_COLS), lambda b, *_: (0, 0)),
                pl.BlockSpec(memory_space=pl.ANY),
                pl.BlockSpec(memory_space=pl.ANY),
            ],
            out_specs=pl.BlockSpec((MOE_ROWS, PACKED), lambda b, *_: (b, 0)),
            scratch_shapes=[
                pltpu.VMEM((D_MODEL, 2 * D_FF), F32),
                pltpu.VMEM((D_FF, D_MODEL), F32),
                pltpu.VMEM((D_MODEL, 2 * D_FF), BF16),
                pltpu.VMEM((D_FF, D_MODEL), BF16),
                pltpu.SemaphoreType.DMA((2,)),
            ],
        ),
        out_shape=jax.ShapeDtypeStruct((n_rows, PACKED), U32),
        compiler_params=_params(("arbitrary",)),
    )(block_exp, next_exp, valid_rows, n_active, xs, b1p, b2, jnp.asarray(perm).astype(BF16), w1, w2)


def _combine_kernel(tcnt_ref, lo_ref, gs_ref, h2_ref, gate_ref, lp_ref, g_ref, ys_ref, o_ref, yloc_ref, sem):
    i = pl.program_id(0)

    slot = i % 2

    def start_runs(tile):
        s = tile % 2

        def body(e, carry):
            j = tile * N_EXPERTS + e
            _start_run(tcnt_ref[j], ys_ref, gs_ref[j], yloc_ref.at[s], lo_ref[j], sem.at[s])
            return carry
        lax.fori_loop(0, N_EXPERTS, body, 0)

    @pl.when(i == 0)
    def _():
        yloc_ref[...] = jnp.zeros_like(yloc_ref)
        start_runs(i)

    @pl.when(i + 1 < pl.num_programs(0))
    def _():
        start_runs(i + 1)

    gates = gate_ref[...]
    lp = lp_ref[...]
    last = i * N_EXPERTS + N_EXPERTS - 1
    _wait_rows(lo_ref[last] + tcnt_ref[last], ys_ref, yloc_ref.at[slot], sem.at[slot])

    h = h2_ref[...]
    for c in range(LOCAL_ROWS // ROW_CHUNK):
        r = lax.broadcasted_iota(I32, (ROW_CHUNK, TM_MID), 0) + c * ROW_CHUNK
        a = jnp.zeros((ROW_CHUNK, TM_MID), F32)
        for kk in range(TOP_K):
            a = jnp.where(lp[kk:kk + 1, :] == r, gates[kk:kk + 1, :], a)
        h = h + _tn(a.astype(BF16), _unpack_pairs(yloc_ref[slot, c * ROW_CHUNK:(c + 1) * ROW_CHUNK, :]))
    o_ref[...] = _rms(h, g_ref[...])


def _combine(tcnt, lo, gstart, h2, gates, lp, g_final, ys):
    n = h2.shape[0]
    return pl.pallas_call(
        _combine_kernel,
        name="combine",
        grid_spec=pltpu.PrefetchScalarGridSpec(
            num_scalar_prefetch=3,
            grid=(n // TM_MID,),
            in_specs=[
                pl.BlockSpec((TM_MID, D_MODEL), lambda i, *_: (i, 0)),
                pl.BlockSpec((TOP_K, TM_MID), lambda i, *_: (0, i)),
                pl.BlockSpec((TOP_K, TM_MID), lambda i, *_: (0, i)),
                pl.BlockSpec((1, D_MODEL), lambda i, *_: (0, 0)),
                pl.BlockSpec(memory_space=pl.ANY),
            ],
            out_specs=pl.BlockSpec((TM_MID, D_MODEL), lambda i, *_: (i, 0)),
            scratch_shapes=[
                pltpu.VMEM((2, LOCAL_ROWS, PACKED), U32),
                pltpu.SemaphoreType.DMA((2,)),
            ],
        ),
        out_shape=jax.ShapeDtypeStruct((n, D_MODEL), F32),
        compiler_params=_params(("arbitrary",)),
    )(tcnt, lo, gstart, h2, gates, lp, g_final, ys)


def _layer(h, mem2, batch, seq, mem_len, g_mix, w_in, b_qkv, conv_w, conv_b, dt_bias, a_log, d_skip,
           g_ssm_out, attn_sinks, bias, w_out, b_out, g_cross, g_mem, w_q_cross, w_kv_cross, w_o_cross,
           g_ffn, w_router, b_router, w1, b1, w2, b2, g_final):
    n = h.shape[0]
    z, xbc, dtp, qkv = _inproj(h, g_mix[None, :], w_in, b_qkv[None, :])

    pad_h = (0, DT_PAD - SSM_HEADS)
    dt_bias_p = jnp.pad(dt_bias, pad_h)[None, :]
    a_head_p = jnp.pad(-jnp.exp(a_log), pad_h)[None, :]
    d_full = jnp.repeat(d_skip, SSM_HEAD_DIM)[None, :]
    per_seq = lambda t: t.reshape(batch, seq, t.shape[-1])
    y_ssm = _ssd(per_seq(xbc), per_seq(z), per_seq(dtp), conv_w, conv_b[None, :], dt_bias_p, a_head_p,
                 d_full, g_ssm_out[None, :]).reshape(n, SSM_WIDTH)
    y_attn = _swa(per_seq(qkv), attn_sinks, bias).reshape(n, ATTN_WIDTH)

    k_mem, v_mem = _memkv(mem2, g_mem[None, :], w_kv_cross)
    h2, u3, gates, lp, tcnt, tbase = _mid(
        y_ssm, y_attn, h, w_out, b_out[None, :], g_cross[None, :],
        w_q_cross, k_mem, v_mem, w_o_cross, g_ffn[None, :],
        w_router.T.astype(BF16), b_router[:, None], seq, mem_len)

    max_rows = n * TOP_K + (n // TM_MID) * N_EXPERTS * (SUBLANES - 1)
    n_blocks = -(-max_rows // MOE_ROWS) + N_EXPERTS
    tcnt = tcnt[:, :, 0]
    tbase = tbase[:, :, 0]
    counts = tbase[-1] + tcnt[-1]
    padded = (counts + MOE_ROWS - 1) // MOE_ROWS * MOE_ROWS
    padded_end = jnp.cumsum(padded)
    start = (padded_end - padded).astype(I32)
    n_active = (padded_end[-1] // MOE_ROWS).astype(I32)
    blk = jnp.arange(n_blocks, dtype=I32)
    blk_exp = jnp.sum((blk * MOE_ROWS)[:, None] >= padded_end[None, :], axis=1)
    eid = jnp.arange(N_EXPERTS, dtype=I32)
    last_exp = jnp.max(jnp.where(padded > 0, eid, 0))
    blk_exp = jnp.where(blk < n_active, jnp.minimum(blk_exp, N_EXPERTS - 1), last_exp).astype(I32)
    later = jnp.logical_and(padded[None, :] > 0, eid[None, :] > eid[:, None])
    next_of = jnp.min(jnp.where(later, eid[None, :], N_EXPERTS), axis=1)
    next_of = jnp.where(next_of < N_EXPERTS, next_of, -1).astype(I32)
    of_blk = blk_exp[:, None] == eid[None, :]
    blk_next = jnp.sum(jnp.where(of_blk, next_of[None, :], 0), axis=1).astype(I32)
    blk_end = jnp.sum(jnp.where(of_blk, (start + counts)[None, :], 0), axis=1)
    blk_valid = jnp.clip(blk_end - blk * MOE_ROWS, 0, MOE_ROWS)
    blk_valid = jnp.where(blk < n_active, blk_valid, 0).astype(I32)

    run_n = tcnt.reshape(-1)
    run_local = (jnp.cumsum(tcnt, axis=1) - tcnt).reshape(-1)
    run_global = (start[None, :] + tbase).reshape(-1)
    xs = _dispatch(run_n, run_local, run_global, padded - counts, start + counts, n_active[None], u3, lp,
                   n_blocks * MOE_ROWS)

    b1p = b1.reshape(N_EXPERTS, 2 * D_FF // PAIR_COLS, LANES, 2).transpose(0, 1, 3, 2)
    ys = _experts(blk_exp, blk_next, blk_valid, n_active[None], xs, w1,
                  b1p.reshape(N_EXPERTS, 1, 2 * D_FF), w2, b2[:, None, :])
    return _combine(run_n, run_local, run_global, h2, gates, lp, g_final, ys)


def kernel(x, mem, g_mix, w_in, b_qkv, conv_w, conv_b, dt_bias, a_log, d_skip, g_ssm_out, attn_sinks, rel_bias, w_out, b_out, g_cross, g_mem, w_q_cross, w_kv_cross, w_o_cross, g_ffn, w_router, b_router, w1, b1, w2, b2, g_final):
    batch, seq, d = x.shape
    mem_len = mem.shape[1]
    depth = g_mix.shape[0]
    assert depth == 1 and d == D_MODEL and seq % TM_MID == 0
    h = x.reshape(batch * seq, d)
    mem2 = mem.reshape(batch * mem_len, d)
    bias = _bias_table(rel_bias)
    out = _layer(
        h, mem2, batch, seq, mem_len, g_mix[0], w_in[0], b_qkv[0], conv_w[0], conv_b[0], dt_bias[0],
        a_log[0], d_skip[0], g_ssm_out[0], attn_sinks[0], bias, w_out[0], b_out[0], g_cross[0],
        g_mem[0], w_q_cross[0], w_kv_cross[0], w_o_cross[0], g_ffn[0], w_router[0], b_router[0],
        w1[0], b1[0], w2[0], b2[0], g_final[None, :])
    return out.reshape(batch, seq, d)
```

```python
import math

import numpy as np
import jax
import jax.numpy as jnp
from jax import lax
from jax.experimental import pallas as pl
from jax.experimental.pallas import tpu as pltpu

F32 = jnp.float32
BF16 = jnp.bfloat16
I32 = jnp.int32

D_MODEL = 1024
RMS_EPS = 1e-5

SSM_HEADS = 8
SSM_HEAD_DIM = 64
SSM_WIDTH = SSM_HEADS * SSM_HEAD_DIM
SSM_STATE = 128
SSM_GROUPS = 2
HEADS_PER_GROUP = SSM_HEADS // SSM_GROUPS
GROUP_WIDTH = SSM_WIDTH // SSM_GROUPS
CONV_WIDTH = 4
CHUNK = 128
CONV_CH = SSM_WIDTH + 2 * SSM_GROUPS * SSM_STATE

ATTN_HEADS = 8
KV_HEADS = 2
Q_PER_KV = ATTN_HEADS // KV_HEADS
HEAD_DIM = 64
ATTN_WIDTH = ATTN_HEADS * HEAD_DIM
KV_WIDTH = KV_HEADS * HEAD_DIM
WINDOW = 128
BLOCK = WINDOW
QKV_COLS = ATTN_WIDTH + 2 * KV_WIDTH

REL_BUCKETS = 32
REL_MAX_DIST = 128

CROSS_HEADS = 4
CROSS_HEAD_DIM = 128
CROSS_WIDTH = CROSS_HEADS * CROSS_HEAD_DIM

N_EXPERTS = 32
TOP_K = 4
D_FF = D_MODEL
SWIGLU_ALPHA = 1.702
SWIGLU_LIMIT = 7.0

LANES = 128
SUBLANES = 8
DT_PAD = LANES
PROJ_COLS = SSM_WIDTH + CONV_CH + DT_PAD + QKV_COLS
VMEM_LIMIT = 56 * 1024 * 1024

TM_PROJ = 1024
TM_MID = 512
SSD_ROWS = 4 * CHUNK
MOE_ROWS = 1024
MOE_PART = 128


def _nt(a, b):
    return lax.dot_general(a, b, (((1,), (1,)), ((), ())), preferred_element_type=F32)


def _tn(a, b):
    return lax.dot_general(a, b, (((0,), (0,)), ((), ())), preferred_element_type=F32)


def _dot(a, b):
    return jnp.dot(a, b, preferred_element_type=F32)


def _rms(x, g):
    return x * lax.rsqrt(jnp.mean(x * x, axis=-1, keepdims=True) + RMS_EPS) * g


def _sigmoid(x):
    return 0.5 * jnp.tanh(0.5 * x) + 0.5


def _params(sem=None, **kwargs):
    return pltpu.CompilerParams(dimension_semantics=sem, vmem_limit_bytes=VMEM_LIMIT, **kwargs)


def _inproj_kernel(x_ref, g_ref, w_f32, bq_ref, z_ref, xbc_ref, dt_ref, qkv_ref, w_ref):
    @pl.when(pl.program_id(0) == 0)
    def _():
        c1 = SSM_WIDTH + CONV_CH
        w_ref[:, :c1] = w_f32[:, :c1].astype(BF16)
        lane = lax.broadcasted_iota(I32, (D_MODEL, DT_PAD), 1)
        w_ref[:, c1:c1 + DT_PAD] = jnp.where(lane < SSM_HEADS, w_f32[:, c1:c1 + DT_PAD], 0.0).astype(BF16)
        w_ref[:, c1 + DT_PAD:] = w_f32[:, c1 + SSM_HEADS:].astype(BF16)

    u = _rms(x_ref[...], g_ref[...])
    p = _dot(u.astype(BF16), w_ref[...])
    z_ref[...] = p[:, :SSM_WIDTH]
    xbc_ref[...] = p[:, SSM_WIDTH:SSM_WIDTH + CONV_CH]
    dt_ref[...] = p[:, SSM_WIDTH + CONV_CH:SSM_WIDTH + CONV_CH + DT_PAD]
    qkv_ref[...] = p[:, SSM_WIDTH + CONV_CH + DT_PAD:] + bq_ref[...]


def _inproj(x2, g_mix, w_in, b_qkv):
    n = x2.shape[0]
    row = lambda i: (i, 0)
    fixed = lambda i: (0, 0)
    return pl.pallas_call(
        _inproj_kernel,
        name="inproj",
        grid=(n // TM_PROJ,),
        in_specs=[
            pl.BlockSpec((TM_PROJ, D_MODEL), row),
            pl.BlockSpec((1, D_MODEL), fixed),
            pl.BlockSpec(w_in.shape, fixed),
            pl.BlockSpec((1, QKV_COLS), fixed),
        ],
        out_specs=[
            pl.BlockSpec((TM_PROJ, SSM_WIDTH), row),
            pl.BlockSpec((TM_PROJ, CONV_CH), row),
            pl.BlockSpec((TM_PROJ, DT_PAD), row),
            pl.BlockSpec((TM_PROJ, QKV_COLS), row),
        ],
        out_shape=[
            jax.ShapeDtypeStruct((n, SSM_WIDTH), F32),
            jax.ShapeDtypeStruct((n, CONV_CH), F32),
            jax.ShapeDtypeStruct((n, DT_PAD), F32),
            jax.ShapeDtypeStruct((n, QKV_COLS), F32),
        ],
        scratch_shapes=[pltpu.VMEM((D_MODEL, PROJ_COLS), BF16)],
        compiler_params=_params(("arbitrary",)),
    )(x2, g_mix, w_in, b_qkv)


def _split3(x):
    hi = x.astype(BF16)
    rest = x - hi.astype(F32)
    mid = rest.astype(BF16)
    lo = (rest - mid.astype(F32)).astype(BF16)
    return jnp.concatenate([hi, mid, lo], axis=1)


def _ssd_kernel(xbc_ref, z_ref, dt_ref, cw_ref, cb_ref, dtb_ref, ah_ref, dsk_ref, gout_ref,
                e64_ref, ecat_ref, tri_ref, y_ref, conv_ref, state_ref):
    c = pl.program_id(0)

    @pl.when(c == 0)
    def _():
        conv_ref[:, :SUBLANES, :] = jnp.zeros((conv_ref.shape[0], SUBLANES, CONV_CH), F32)
        state_ref[...] = jnp.zeros_like(state_ref)

    for cc in range(xbc_ref.shape[1] // CHUNK):
        for b in range(xbc_ref.shape[0]):
            _ssd_chunk(b, slice(cc * CHUNK, (cc + 1) * CHUNK), xbc_ref, z_ref, dt_ref, cw_ref, cb_ref, dtb_ref,
                       ah_ref, dsk_ref, gout_ref, e64_ref, ecat_ref, tri_ref, y_ref, conv_ref, state_ref)


def _ssd_chunk(b, rows, xbc_ref, z_ref, dt_ref, cw_ref, cb_ref, dtb_ref, ah_ref, dsk_ref, gout_ref,
               e64_ref, ecat_ref, tri_ref, y_ref, conv_ref, state_ref):
    u = xbc_ref[b, rows, :]
    conv_ref[b, SUBLANES:, :] = u
    acc = u * cw_ref[CONV_WIDTH - 1:CONV_WIDTH, :] + cb_ref[...]
    for j in range(1, CONV_WIDTH):
        shifted = conv_ref[b, SUBLANES - j:SUBLANES - j + CHUNK, :]
        acc = acc + shifted * cw_ref[CONV_WIDTH - 1 - j:CONV_WIDTH - j, :]
    conv_ref[b, :SUBLANES, :] = u[CHUNK - SUBLANES:, :]
    xbc = acc * _sigmoid(acc)

    xs = xbc[:, :SSM_WIDTH]
    bm = xbc[:, SSM_WIDTH:SSM_WIDTH + SSM_GROUPS * SSM_STATE]
    cm = xbc[:, SSM_WIDTH + SSM_GROUPS * SSM_STATE:]

    dtr = dt_ref[b, rows, :] + dtb_ref[...]
    dt = jnp.maximum(dtr, 0.0) + jnp.log1p(jnp.exp(-jnp.abs(dtr)))
    a = dt * ah_ref[...]
    a3 = _split3(a)
    a_rows = jnp.concatenate([a3[:, :DT_PAD], a3[:, DT_PAD:2 * DT_PAD], a3[:, 2 * DT_PAD:]], axis=0)
    cs = _dot(tri_ref[...], a_rows)
    cs_row = cs.T
    dt_full = _dot(_split3(dt), e64_ref[...])
    cs_cat = _dot(_split3(cs), ecat_ref[...])
    cs_wide = cs_cat[:, :SSM_HEADS * CHUNK]
    cs_full = cs_cat[:, SSM_HEADS * CHUNK:]
    cs_last = cs_full[CHUNK - 1:CHUNK, :]

    xdt = xs * dt_full
    xdt_b = xdt.astype(BF16)
    xw_b = (xdt * jnp.exp(cs_last - cs_full)).astype(BF16)
    in_decay = jnp.exp(cs_full)
    chunk_decay = jnp.exp(cs_last)

    li = lax.broadcasted_iota(I32, (CHUNK, CHUNK), 0)
    si = lax.broadcasted_iota(I32, (CHUNK, CHUNK), 1)
    causal = li >= si

    ys = []
    for g in range(SSM_GROUPS):
        bg = bm[:, g * SSM_STATE:(g + 1) * SSM_STATE].astype(BF16)
        cg = cm[:, g * SSM_STATE:(g + 1) * SSM_STATE].astype(BF16)
        cb = _nt(cg, bg)
        yd = []
        for r in range(HEADS_PER_GROUP):
            h = g * HEADS_PER_GROUP + r
            diff = cs_wide[:, h * CHUNK:(h + 1) * CHUNK] - cs_row[h:h + 1, :]
            m = cb * jnp.exp(jnp.where(causal, diff, -jnp.inf))
            yd.append(_dot(m.astype(BF16), xdt_b[:, h * SSM_HEAD_DIM:(h + 1) * SSM_HEAD_DIM]))
        y_diag = jnp.concatenate(yd, axis=1)
        gs = slice(g * GROUP_WIDTH, (g + 1) * GROUP_WIDTH)
        st = state_ref[b, g]
        y_off = _dot(cg, st.astype(BF16)) * in_decay[:, gs]
        state_ref[b, g] = st * chunk_decay[:, gs] + _tn(bg, xw_b[:, gs])
        ys.append(y_diag + y_off)
    y = jnp.concatenate(ys, axis=1) + dsk_ref[...] * xs

    zz = z_ref[b, rows, :]
    y = y * (zz * _sigmoid(zz))
    outs = []
    for g in range(SSM_GROUPS):
        yg = y[:, g * GROUP_WIDTH:(g + 1) * GROUP_WIDTH]
        outs.append(yg * lax.rsqrt(jnp.mean(yg * yg, axis=-1, keepdims=True) + RMS_EPS))
    y_ref[b, rows, :] = jnp.concatenate(outs, axis=1) * gout_ref[...]


def _ssd(xbc, z, dtp, conv_w, conv_b, dt_bias_p, a_head_p, d_full, g_out):
    batch, seq, _ = xbc.shape
    chunk = lambda c: (0, c, 0)
    fixed = lambda c: (0, 0)
    head = np.arange(SSM_WIDTH) // SSM_HEAD_DIM
    e64 = (np.arange(DT_PAD)[:, None] == head[None, :]).astype(np.float32)
    head_w = np.arange(SSM_HEADS * CHUNK) // CHUNK
    e128 = (np.arange(DT_PAD)[:, None] == head_w[None, :]).astype(np.float32)
    e64_3 = jnp.asarray(np.tile(e64, (3, 1))).astype(BF16)
    ecat_3 = jnp.asarray(np.tile(np.concatenate([e128, e64], axis=1), (3, 1))).astype(BF16)
    tri_3 = jnp.asarray(np.tile(np.tril(np.ones((CHUNK, CHUNK), np.float32)), (1, 3))).astype(BF16)
    return pl.pallas_call(
        _ssd_kernel,
        name="ssd",
        grid=(seq // SSD_ROWS,),
        in_specs=[
            pl.BlockSpec((batch, SSD_ROWS, CONV_CH), chunk),
            pl.BlockSpec((batch, SSD_ROWS, SSM_WIDTH), chunk),
            pl.BlockSpec((batch, SSD_ROWS, DT_PAD), chunk),
            pl.BlockSpec((CONV_WIDTH, CONV_CH), fixed),
            pl.BlockSpec((1, CONV_CH), fixed),
            pl.BlockSpec((1, DT_PAD), fixed),
            pl.BlockSpec((1, DT_PAD), fixed),
            pl.BlockSpec((1, SSM_WIDTH), fixed),
            pl.BlockSpec((1, SSM_WIDTH), fixed),
            pl.BlockSpec((3 * DT_PAD, SSM_WIDTH), fixed),
            pl.BlockSpec((3 * DT_PAD, SSM_HEADS * CHUNK + SSM_WIDTH), fixed),
            pl.BlockSpec((CHUNK, 3 * CHUNK), fixed),
        ],
        out_specs=pl.BlockSpec((batch, SSD_ROWS, SSM_WIDTH), chunk),
        out_shape=jax.ShapeDtypeStruct((batch, seq, SSM_WIDTH), F32),
        scratch_shapes=[
            pltpu.VMEM((batch, SUBLANES + CHUNK, CONV_CH), F32),
            pltpu.VMEM((batch, SSM_GROUPS, SSM_STATE, GROUP_WIDTH), F32),
        ],
        compiler_params=_params(("arbitrary",)),
    )(xbc, z, dtp, conv_w, conv_b, dt_bias_p, a_head_p, d_full, g_out, e64_3, ecat_3, tri_3)


def _t5_bucket_table():
    q_pos = np.arange(BLOCK)[:, None]
    k_pos = np.arange(2 * BLOCK)[None, :] - BLOCK
    dist = q_pos - k_pos
    d = np.maximum(dist, 0)
    max_exact = REL_BUCKETS // 2
    ratio = np.maximum(d, max_exact).astype(np.float32) / np.float32(max_exact)
    large = max_exact + (np.log(ratio) / np.float32(math.log(REL_MAX_DIST / max_exact))
                         * np.float32(REL_BUCKETS - max_exact)).astype(np.int32)
    large = np.minimum(large, REL_BUCKETS - 1)
    bucket = np.where(d < max_exact, d, large)
    in_window = (dist >= 0) & (dist < WINDOW)
    return np.where(in_window, bucket, -1).astype(np.int32)


def _bias_kernel(rb_ref, bucket_ref, out_ref):
    bucket = bucket_ref[...]
    for h in range(ATTN_HEADS):
        acc = jnp.zeros(bucket.shape, F32)
        for b in range(REL_BUCKETS):
            acc = jnp.where(bucket == b, rb_ref[b, h], acc)
        out_ref[h] = jnp.where(bucket >= 0, acc, -jnp.inf)


def _bias_table(rel_bias):
    bucket = jnp.asarray(_t5_bucket_table())
    return pl.pallas_call(
        _bias_kernel,
        name="bias_table",
        in_specs=[
            pl.BlockSpec(memory_space=pltpu.SMEM),
            pl.BlockSpec(memory_space=pltpu.VMEM),
        ],
        out_specs=pl.BlockSpec(memory_space=pltpu.VMEM),
        out_shape=jax.ShapeDtypeStruct((ATTN_HEADS, BLOCK, 2 * BLOCK), F32),
    )(rel_bias, bucket)


def _swa_kernel(sink_ref, q_ref, k_ref, v_ref, kp_ref, vp_ref, bias_ref, o_ref):
    i = pl.program_id(0)
    col = lax.broadcasted_iota(I32, (BLOCK, 2 * BLOCK), 1)
    key_ok = jnp.logical_or(col >= BLOCK, i > 0)
    for b in range(q_ref.shape[0]):
        q = (q_ref[b] * (HEAD_DIM ** -0.5)).astype(BF16)
        outs = []
        for g in range(KV_HEADS):
            gs = slice(g * HEAD_DIM, (g + 1) * HEAD_DIM)
            k2 = jnp.concatenate([kp_ref[b, :, gs], k_ref[b, :, gs]], axis=0).astype(BF16)
            v2 = jnp.concatenate([vp_ref[b, :, gs], v_ref[b, :, gs]], axis=0).astype(BF16)
            for r in range(Q_PER_KV):
                h = g * Q_PER_KV + r
                s = _nt(q[:, h * HEAD_DIM:(h + 1) * HEAD_DIM], k2) + bias_ref[h]
                s = jnp.where(key_ok, s, -jnp.inf)
                sink = sink_ref[h]
                m = jnp.maximum(jnp.max(s, axis=-1, keepdims=True), sink)
                p = jnp.exp(s - m)
                denom = jnp.sum(p, axis=-1, keepdims=True) + jnp.exp(sink - m)
                outs.append(_dot(p.astype(BF16), v2) / denom)
        o_ref[b] = jnp.concatenate(outs, axis=1)


def _swa(qkv, sinks, bias):
    batch, seq, _ = qkv.shape
    kcol = ATTN_WIDTH // KV_WIDTH
    prev = lambda i: jnp.maximum(i - 1, 0)
    return pl.pallas_call(
        _swa_kernel,
        name="swa",
        grid_spec=pltpu.PrefetchScalarGridSpec(
            num_scalar_prefetch=1,
            grid=(seq // BLOCK,),
            in_specs=[
                pl.BlockSpec((batch, BLOCK, ATTN_WIDTH), lambda i, s: (0, i, 0)),
                pl.BlockSpec((batch, BLOCK, KV_WIDTH), lambda i, s: (0, i, kcol)),
                pl.BlockSpec((batch, BLOCK, KV_WIDTH), lambda i, s: (0, i, kcol + 1)),
                pl.BlockSpec((batch, BLOCK, KV_WIDTH), lambda i, s: (0, prev(i), kcol)),
                pl.BlockSpec((batch, BLOCK, KV_WIDTH), lambda i, s: (0, prev(i), kcol + 1)),
                pl.BlockSpec((ATTN_HEADS, BLOCK, 2 * BLOCK), lambda i, s: (0, 0, 0)),
            ],
            out_specs=pl.BlockSpec((batch, BLOCK, ATTN_WIDTH), lambda i, s: (0, i, 0)),
        ),
        out_shape=jax.ShapeDtypeStruct((batch, seq, ATTN_WIDTH), F32),
        compiler_params=_params(("arbitrary",)),
    )(sinks, qkv, qkv, qkv, qkv, qkv, bias)


def _memkv_kernel(m_ref, g_ref, w_ref, k_ref, v_ref):
    u = _rms(m_ref[...], g_ref[...])
    kv = _dot(u.astype(BF16), w_ref[...].astype(BF16))
    k_ref[...] = kv[:, :CROSS_WIDTH].astype(BF16)
    v_ref[...] = kv[:, CROSS_WIDTH:].astype(BF16)


def _memkv(mem2, g_mem, w_kv):
    n = mem2.shape[0]
    return pl.pallas_call(
        _memkv_kernel,
        name="memkv",
        out_shape=[jax.ShapeDtypeStruct((n, CROSS_WIDTH), BF16)] * 2,
        compiler_params=_params(),
    )(mem2, g_mem, w_kv)


def _mid_kernel(ys_ref, ya_ref, x_ref, wout_f32, bout_ref, gc_ref, wq_f32, k_ref, v_ref, wo_f32,
                gf_ref, wr_ref, br_ref, upper_ref, lower_ref,
                h2_ref, u3_ref, gate_ref, lp_ref, tcnt_ref, tbase_ref, run_ref, wout_ref, wq_ref, wo_ref):
    i = pl.program_id(0)

    @pl.when(i == 0)
    def _():
        run_ref[...] = jnp.zeros_like(run_ref)
        wout_ref[...] = wout_f32[...].astype(BF16)
        wq_ref[...] = wq_f32[...].astype(BF16)
        wo_ref[...] = wo_f32[...].astype(BF16)

    ycat = jnp.concatenate([ys_ref[...], ya_ref[...]], axis=1).astype(BF16)
    h1 = x_ref[...] + _dot(ycat, wout_ref[...]) + bout_ref[...]

    u2 = _rms(h1, gc_ref[...])
    q = _dot(u2.astype(BF16), wq_ref[...]).astype(BF16)
    outs = []
    for h in range(CROSS_HEADS):
        hs = slice(h * CROSS_HEAD_DIM, (h + 1) * CROSS_HEAD_DIM)
        s = _nt(q[:, hs], k_ref[:, hs]) * (CROSS_HEAD_DIM ** -0.5)
        m = jnp.max(s, axis=-1, keepdims=True)
        p = jnp.exp(s - m)
        outs.append(_dot(p.astype(BF16), v_ref[:, hs]) / jnp.sum(p, axis=-1, keepdims=True))
    o = jnp.concatenate(outs, axis=1).astype(BF16)
    h2 = h1 + _dot(o, wo_ref[...])
    h2_ref[...] = h2
    u3 = _rms(h2, gf_ref[...]).astype(BF16)
    u3_ref[...] = u3

    logits = _nt(wr_ref[...], u3) + br_ref[...]
    eid = lax.broadcasted_iota(I32, logits.shape, 0)
    vals, idxs, hots = [], [], []
    for _ in range(TOP_K):
        m = jnp.max(logits, axis=0, keepdims=True)
        ix = jnp.min(jnp.where(logits == m, eid, N_EXPERTS), axis=0, keepdims=True)
        hot = eid == ix
        logits = jnp.where(hot, -jnp.inf, logits)
        vals.append(m)
        idxs.append(ix)
        hots.append(hot)
    ex = [jnp.exp(v - vals[0]) for v in vals]
    tot = ex[0] + ex[1] + ex[2] + ex[3]
    gate_ref[...] = jnp.concatenate([e / tot for e in ex], axis=0)

    hot_f = [jnp.where(hot, 1.0, 0.0) for hot in hots]
    cnt_k = [jnp.sum(hf, axis=1, keepdims=True) for hf in hot_f]
    tile_cnt = cnt_k[0] + cnt_k[1] + cnt_k[2] + cnt_k[3]
    tile_cnt = jnp.floor((tile_cnt + (SUBLANES - 1)) * (1.0 / SUBLANES)) * SUBLANES
    units = jnp.broadcast_to(tile_cnt * (1.0 / SUBLANES), (N_EXPERTS, LANES)).astype(BF16)
    offs = _dot(lower_ref[...], units)[:, :1] * SUBLANES
    local = []
    before_all = _dot(jnp.concatenate(hot_f, axis=0).astype(BF16), upper_ref[...])
    for kk in range(TOP_K):
        before = before_all[kk * N_EXPERTS:(kk + 1) * N_EXPERTS, :]
        local.append(jnp.sum(jnp.where(hots[kk], before + offs, 0.0), axis=0, keepdims=True))
        offs = offs + cnt_k[kk]
    lp_ref[...] = jnp.concatenate(local, axis=0).astype(I32)
    tcnt_ref[0] = jnp.broadcast_to(tile_cnt, (N_EXPERTS, LANES)).astype(I32)
    tbase_ref[0] = jnp.broadcast_to(run_ref[...], (N_EXPERTS, LANES)).astype(I32)
    run_ref[...] = run_ref[...] + tile_cnt


def _mid(y_ssm, y_attn, x2, w_out, b_out, g_cross, w_q, k_mem, v_mem, w_o, g_ffn, w_r_t, b_r, seq, mem_len):
    n = x2.shape[0]
    tiles_per_batch = seq // TM_MID
    row = lambda i: (i, 0)
    col = lambda i: (0, i)
    fixed = lambda i: (0, 0)
    memb = lambda i: (i // tiles_per_batch, 0)
    tile3 = lambda i: (i, 0, 0)
    n_tiles = n // TM_MID
    upper = jnp.asarray(np.triu(np.ones((TM_MID, TM_MID), np.float32), 1)).astype(BF16)
    lower = jnp.asarray(np.tril(np.ones((N_EXPERTS, N_EXPERTS), np.float32), -1)).astype(BF16)
    return pl.pallas_call(
        _mid_kernel,
        name="mid",
        grid=(n_tiles,),
        in_specs=[
            pl.BlockSpec((TM_MID, SSM_WIDTH), row),
            pl.BlockSpec((TM_MID, ATTN_WIDTH), row),
            pl.BlockSpec((TM_MID, D_MODEL), row),
            pl.BlockSpec((D_MODEL, D_MODEL), fixed),
            pl.BlockSpec((1, D_MODEL), fixed),
            pl.BlockSpec((1, D_MODEL), fixed),
            pl.BlockSpec((D_MODEL, CROSS_WIDTH), fixed),
            pl.BlockSpec((mem_len, CROSS_WIDTH), memb),
            pl.BlockSpec((mem_len, CROSS_WIDTH), memb),
            pl.BlockSpec((CROSS_WIDTH, D_MODEL), fixed),
            pl.BlockSpec((1, D_MODEL), fixed),
            pl.BlockSpec((N_EXPERTS, D_MODEL), fixed),
            pl.BlockSpec((N_EXPERTS, 1), fixed),
            pl.BlockSpec((TM_MID, TM_MID), fixed),
            pl.BlockSpec((N_EXPERTS, N_EXPERTS), fixed),
        ],
        out_specs=[
            pl.BlockSpec((TM_MID, D_MODEL), row),
            pl.BlockSpec((TM_MID, D_MODEL), row),
            pl.BlockSpec((TOP_K, TM_MID), col),
            pl.BlockSpec((TOP_K, TM_MID), col),
            pl.BlockSpec((1, N_EXPERTS, LANES), tile3),
            pl.BlockSpec((1, N_EXPERTS, LANES), tile3),
        ],
        out_shape=[
            jax.ShapeDtypeStruct((n, D_MODEL), F32),
            jax.ShapeDtypeStruct((n, D_MODEL), BF16),
            jax.ShapeDtypeStruct((TOP_K, n), F32),
            jax.ShapeDtypeStruct((TOP_K, n), I32),
            jax.ShapeDtypeStruct((n_tiles, N_EXPERTS, LANES), I32),
            jax.ShapeDtypeStruct((n_tiles, N_EXPERTS, LANES), I32),
        ],
        scratch_shapes=[
            pltpu.VMEM((N_EXPERTS, 1), F32),
            pltpu.VMEM((D_MODEL, D_MODEL), BF16),
            pltpu.VMEM((D_MODEL, CROSS_WIDTH), BF16),
            pltpu.VMEM((CROSS_WIDTH, D_MODEL), BF16),
        ],
        compiler_params=_params(("arbitrary",)),
    )(y_ssm, y_attn, x2, w_out, b_out, g_cross, w_q, k_mem, v_mem, w_o, g_ffn, w_r_t, b_r, upper, lower)


ROW_CHUNK = 256
LOCAL_ROWS = -(-(TM_MID * TOP_K + N_EXPERTS * (SUBLANES - 1)) // ROW_CHUNK) * ROW_CHUNK


U32 = jnp.uint32
PACKED = D_MODEL // 2


def _pack_pairs(v):
    bits = lax.bitcast_convert_type(v, U32)
    return bits[:, PACKED:] | (bits[:, :PACKED] >> 16)


def _unpack_pairs(w):
    lo = lax.bitcast_convert_type(w << 16, F32)
    hi = lax.bitcast_convert_type(w & jnp.uint32(0xFFFF0000), F32)
    return jnp.concatenate([lo, hi], axis=1).astype(BF16)


def _row_copy(n, src_ref, src, dst_ref, dst, sem):
    aligned = lambda v: v if isinstance(v, int) else pl.multiple_of(v, SUBLANES)
    return pltpu.make_async_copy(src_ref.at[pl.ds(aligned(src), aligned(n))],
                                 dst_ref.at[pl.ds(aligned(dst), aligned(n))], sem)


def _start_run(n, src_ref, src, dst_ref, dst, sem, priority):
    @pl.when(n > 0)
    def _():
        _row_copy(n, src_ref, src, dst_ref, dst, sem).start(priority=priority)


def _wait_rows(n, src_ref, dst_ref, sem):
    _row_copy(n, src_ref, 0, dst_ref, 0, sem).wait()


def _dispatch_kernel(tcnt_ref, lo_ref, gs_ref, padn_ref, pads_ref, na_ref, u_ref, lp_ref, xs_ref,
                     xloc_ref, zero_ref, sem, zsem):
    i = pl.program_id(0)

    def zero_fill(wait):
        def fill(n, dst):
            cp = _row_copy(n, zero_ref, 0, xs_ref, dst, zsem)
            if wait:
                cp.wait()
            else:
                cp.start(priority=1)

        def body(e, carry):
            pl.when(padn_ref[e] > 0)(lambda: fill(padn_ref[e], pads_ref[e]))
            return carry
        lax.fori_loop(0, N_EXPERTS, body, 0)

        def tail(b, carry):
            fill(MOE_ROWS, b * MOE_ROWS)
            return carry
        lax.fori_loop(na_ref[0], xs_ref.shape[0] // MOE_ROWS, tail, 0)

    @pl.when(i == 0)
    def _():
        zero_ref[...] = jnp.zeros_like(zero_ref)
        zero_fill(False)

    slot = i % 2
    u = u_ref[...]
    lp = lp_ref[...]
    for c in range(LOCAL_ROWS // ROW_CHUNK):
        r = lax.broadcasted_iota(I32, (ROW_CHUNK, TM_MID), 0) + c * ROW_CHUNK
        p = jnp.zeros((ROW_CHUNK, TM_MID), F32)
        for kk in range(TOP_K):
            p = jnp.where(lp[kk:kk + 1, :] == r, 1.0, p)
        xloc_ref[slot, c * ROW_CHUNK:(c + 1) * ROW_CHUNK, :] = _pack_pairs(_dot(p.astype(BF16), u))

    def start_runs(tile):
        s = tile % 2

        def body(pair, carry):
            for priority in range(2):
                j = tile * N_EXPERTS + 2 * pair + priority
                _start_run(tcnt_ref[j], xloc_ref.at[s], lo_ref[j], xs_ref, gs_ref[j], sem.at[s], priority)
            return carry
        lax.fori_loop(0, N_EXPERTS // 2, body, 0)

    def wait_runs(tile):
        s = tile % 2
        last = tile * N_EXPERTS + N_EXPERTS - 1
        _wait_rows(lo_ref[last] + tcnt_ref[last], xloc_ref.at[s], xs_ref, sem.at[s])

    start_runs(i)

    @pl.when(i > 0)
    def _():
        wait_runs(i - 1)

    @pl.when(i == pl.num_programs(0) - 1)
    def _():
        wait_runs(i)
        zero_fill(True)


def _dispatch(tcnt, lo, gstart, padn, pads, n_active, u3, lp, n_rows):
    n = u3.shape[0]
    return pl.pallas_call(
        _dispatch_kernel,
        name="dispatch",
        grid_spec=pltpu.PrefetchScalarGridSpec(
            num_scalar_prefetch=6,
            grid=(n // TM_MID,),
            in_specs=[
                pl.BlockSpec((TM_MID, D_MODEL), lambda i, *_: (i, 0)),
                pl.BlockSpec((TOP_K, TM_MID), lambda i, *_: (0, i)),
            ],
            out_specs=pl.BlockSpec(memory_space=pl.ANY),
            scratch_shapes=[
                pltpu.VMEM((2, LOCAL_ROWS, PACKED), U32),
                pltpu.VMEM((MOE_ROWS, PACKED), U32),
                pltpu.SemaphoreType.DMA((2,)),
                pltpu.SemaphoreType.DMA,
            ],
        ),
        out_shape=jax.ShapeDtypeStruct((n_rows, PACKED), U32),
        compiler_params=_params(("arbitrary",)),
    )(tcnt, lo, gstart, padn, pads, n_active, u3, lp)


PAIR_COLS = 2 * LANES


def _expert_kernel(be_ref, nx_ref, nv_ref, na_ref, x_ref, b1_ref, b2_ref, perm_ref, w1_hbm, w2_hbm, y_ref,
                   w1s_ref, w2s_ref, w1p_ref, w2b_ref, sem):
    b = pl.program_id(0)
    valid_rows = nv_ref[b]
    active = valid_rows > 0
    expert = be_ref[b]
    new_expert = jnp.logical_or(b == 0, expert != be_ref[jnp.maximum(b - 1, 0)])

    def fetch(e, wait):
        for src, dst, s in ((w1_hbm, w1s_ref, 0), (w2_hbm, w2s_ref, 1)):
            cp = pltpu.make_async_copy(src.at[e], dst, sem.at[s])
            if wait:
                cp.wait()
            else:
                cp.start(priority=1)

    @pl.when(jnp.logical_and(b == 0, active))
    def _():
        fetch(expert, False)

    @pl.when(jnp.logical_and(active, new_expert))
    def _():
        fetch(expert, True)
        for c in range(2 * D_FF // PAIR_COLS):
            cs = slice(c * PAIR_COLS, (c + 1) * PAIR_COLS)
            w1p_ref[:, cs] = _dot(w1s_ref[:, cs].astype(BF16), perm_ref[...]).astype(BF16)
        w2b_ref[...] = w2s_ref[...].astype(BF16)

        @pl.when(nx_ref[b] >= 0)
        def _():
            fetch(nx_ref[b], False)

    for live in range(MOE_PART, MOE_ROWS + 1, MOE_PART):
        @pl.when(jnp.logical_and(valid_rows > live - MOE_PART, valid_rows <= live))
        def _():
            x = _unpack_pairs(x_ref[:live, :])
            hdn = _dot(x, w1p_ref[...]) + b1_ref[0]
            acts = []
            for c in range(2 * D_FF // PAIR_COLS):
                hg = jnp.minimum(hdn[:, c * PAIR_COLS:c * PAIR_COLS + LANES], SWIGLU_LIMIT)
                hl = jnp.clip(hdn[:, c * PAIR_COLS + LANES:(c + 1) * PAIR_COLS], -SWIGLU_LIMIT, SWIGLU_LIMIT)
                acts.append((hg * _sigmoid(SWIGLU_ALPHA * hg) * (hl + 1.0)).astype(BF16))
            y = _dot(jnp.concatenate(acts, axis=1), w2b_ref[...]) + b2_ref[0]
            y_ref[:live, :] = _pack_pairs(y.astype(BF16).astype(F32))
            if live < MOE_ROWS:
                y_ref[live:, :] = jnp.zeros((MOE_ROWS - live, PACKED), U32)

    @pl.when(jnp.logical_not(active))
    def _():
        y_ref[...] = jnp.zeros_like(y_ref)


def _experts(block_exp, next_exp, valid_rows, n_active, xs, w1, b1p, w2, b2):
    n_rows = xs.shape[0]
    n_blocks = n_rows // MOE_ROWS
    xmap = lambda b, be, nx, nv, na: (jnp.maximum(jnp.minimum(b, na[0] - 1), 0), 0)
    emap = lambda b, be, nx, nv, na: (be[b], 0, 0)
    perm = np.zeros((PAIR_COLS, PAIR_COLS), np.float32)
    perm[2 * np.arange(LANES), np.arange(LANES)] = 1.0
    perm[2 * np.arange(LANES) + 1, LANES + np.arange(LANES)] = 1.0
    return pl.pallas_call(
        _expert_kernel,
        name="experts",
        grid_spec=pltpu.PrefetchScalarGridSpec(
            num_scalar_prefetch=4,
            grid=(n_blocks,),
            in_specs=[
                pl.BlockSpec((MOE_ROWS, PACKED), xmap),
                pl.BlockSpec((1, 1, 2 * D_FF), emap),
                pl.BlockSpec((1, 1, D_MODEL), emap),
                pl.BlockSpec((PAIR_COLS, PAIR_COLS), lambda b, *_: (0, 0)),
                pl.BlockSpec(memory_space=pl.ANY),
                pl.BlockSpec(memory_space=pl.ANY),
            ],
            out_specs=pl.BlockSpec((MOE_ROWS, PACKED), lambda b, *_: (b, 0)),
            scratch_shapes=[
                pltpu.VMEM((D_MODEL, 2 * D_FF), F32),
                pltpu.VMEM((D_FF, D_MODEL), F32),
                pltpu.VMEM((D_MODEL, 2 * D_FF), BF16),
                pltpu.VMEM((D_FF, D_MODEL), BF16),
                pltpu.SemaphoreType.DMA((2,)),
            ],
        ),
        out_shape=jax.ShapeDtypeStruct((n_rows, PACKED), U32),
        compiler_params=_params(("arbitrary",)),
    )(block_exp, next_exp, valid_rows, n_active, xs, b1p, b2, jnp.asarray(perm).astype(BF16), w1, w2)


def _combine_kernel(tcnt_ref, lo_ref, gs_ref, h2_ref, gate_ref, lp_ref, g_ref, ys_ref, o_ref, yloc_ref, sem):
    i = pl.program_id(0)

    slot = i % 2

    def start_runs(tile):
        s = tile % 2

        def body(pair, carry):
            for priority in range(2):
                j = tile * N_EXPERTS + 2 * pair + priority
                _start_run(tcnt_ref[j], ys_ref, gs_ref[j], yloc_ref.at[s], lo_ref[j], sem.at[s], priority)
            return carry
        lax.fori_loop(0, N_EXPERTS // 2, body, 0)

    @pl.when(i == 0)
    def _():
        yloc_ref[...] = jnp.zeros_like(yloc_ref)
        start_runs(i)

    @pl.when(i + 1 < pl.num_programs(0))
    def _():
        start_runs(i + 1)

    gates = gate_ref[...]
    lp = lp_ref[...]
    last = i * N_EXPERTS + N_EXPERTS - 1
    _wait_rows(lo_ref[last] + tcnt_ref[last], ys_ref, yloc_ref.at[slot], sem.at[slot])

    h = h2_ref[...]
    for c in range(LOCAL_ROWS // ROW_CHUNK):
        r = lax.broadcasted_iota(I32, (ROW_CHUNK, TM_MID), 0) + c * ROW_CHUNK
        a = jnp.zeros((ROW_CHUNK, TM_MID), F32)
        for kk in range(TOP_K):
            a = jnp.where(lp[kk:kk + 1, :] == r, gates[kk:kk + 1, :], a)
        h = h + _tn(a.astype(BF16), _unpack_pairs(yloc_ref[slot, c * ROW_CHUNK:(c + 1) * ROW_CHUNK, :]))
    o_ref[...] = _rms(h, g_ref[...])


def _combine(tcnt, lo, gstart, h2, gates, lp, g_final, ys):
    n = h2.shape[0]
    return pl.pallas_call(
        _combine_kernel,
        name="combine",
        grid_spec=pltpu.PrefetchScalarGridSpec(
            num_scalar_prefetch=3,
            grid=(n // TM_MID,),
            in_specs=[
                pl.BlockSpec((TM_MID, D_MODEL), lambda i, *_: (i, 0)),
                pl.BlockSpec((TOP_K, TM_MID), lambda i, *_: (0, i)),
                pl.BlockSpec((TOP_K, TM_MID), lambda i, *_: (0, i)),
                pl.BlockSpec((1, D_MODEL), lambda i, *_: (0, 0)),
                pl.BlockSpec(memory_space=pl.ANY),
            ],
            out_specs=pl.BlockSpec((TM_MID, D_MODEL), lambda i, *_: (i, 0)),
            scratch_shapes=[
                pltpu.VMEM((2, LOCAL_ROWS, PACKED), U32),
                pltpu.SemaphoreType.DMA((2,)),
            ],
        ),
        out_shape=jax.ShapeDtypeStruct((n, D_MODEL), F32),
        compiler_params=_params(("arbitrary",)),
    )(tcnt, lo, gstart, h2, gates, lp, g_final, ys)


def _layer(h, mem2, batch, seq, mem_len, g_mix, w_in, b_qkv, conv_w, conv_b, dt_bias, a_log, d_skip,
           g_ssm_out, attn_sinks, bias, w_out, b_out, g_cross, g_mem, w_q_cross, w_kv_cross, w_o_cross,
           g_ffn, w_router, b_router, w1, b1, w2, b2, g_final):
    n = h.shape[0]
    z, xbc, dtp, qkv = _inproj(h, g_mix[None, :], w_in, b_qkv[None, :])

    pad_h = (0, DT_PAD - SSM_HEADS)
    dt_bias_p = jnp.pad(dt_bias, pad_h)[None, :]
    a_head_p = jnp.pad(-jnp.exp(a_log), pad_h)[None, :]
    d_full = jnp.repeat(d_skip, SSM_HEAD_DIM)[None, :]
    per_seq = lambda t: t.reshape(batch, seq, t.shape[-1])
    y_ssm = _ssd(per_seq(xbc), per_seq(z), per_seq(dtp), conv_w, conv_b[None, :], dt_bias_p, a_head_p,
                 d_full, g_ssm_out[None, :]).reshape(n, SSM_WIDTH)
    y_attn = _swa(per_seq(qkv), attn_sinks, bias).reshape(n, ATTN_WIDTH)

    k_mem, v_mem = _memkv(mem2, g_mem[None, :], w_kv_cross)
    h2, u3, gates, lp, tcnt, tbase = _mid(
        y_ssm, y_attn, h, w_out, b_out[None, :], g_cross[None, :],
        w_q_cross, k_mem, v_mem, w_o_cross, g_ffn[None, :],
        w_router.T.astype(BF16), b_router[:, None], seq, mem_len)

    max_rows = n * TOP_K + (n // TM_MID) * N_EXPERTS * (SUBLANES - 1)
    n_blocks = -(-max_rows // MOE_ROWS) + N_EXPERTS
    tcnt = tcnt[:, :, 0]
    tbase = tbase[:, :, 0]
    counts = tbase[-1] + tcnt[-1]
    padded = (counts + MOE_ROWS - 1) // MOE_ROWS * MOE_ROWS
    padded_end = jnp.cumsum(padded)
    start = (padded_end - padded).astype(I32)
    n_active = (padded_end[-1] // MOE_ROWS).astype(I32)
    blk = jnp.arange(n_blocks, dtype=I32)
    blk_exp = jnp.sum((blk * MOE_ROWS)[:, None] >= padded_end[None, :], axis=1)
    eid = jnp.arange(N_EXPERTS, dtype=I32)
    last_exp = jnp.max(jnp.where(padded > 0, eid, 0))
    blk_exp = jnp.where(blk < n_active, jnp.minimum(blk_exp, N_EXPERTS - 1), last_exp).astype(I32)
    later = jnp.logical_and(padded[None, :] > 0, eid[None, :] > eid[:, None])
    next_of = jnp.min(jnp.where(later, eid[None, :], N_EXPERTS), axis=1)
    next_of = jnp.where(next_of < N_EXPERTS, next_of, -1).astype(I32)
    of_blk = blk_exp[:, None] == eid[None, :]
    blk_next = jnp.sum(jnp.where(of_blk, next_of[None, :], 0), axis=1).astype(I32)
    blk_end = jnp.sum(jnp.where(of_blk, (start + counts)[None, :], 0), axis=1)
    blk_valid = jnp.clip(blk_end - blk * MOE_ROWS, 0, MOE_ROWS)
    blk_valid = jnp.where(blk < n_active, blk_valid, 0).astype(I32)

    run_n = tcnt.reshape(-1)
    run_local = (jnp.cumsum(tcnt, axis=1) - tcnt).reshape(-1)
    run_global = (start[None, :] + tbase).reshape(-1)
    xs = _dispatch(run_n, run_local, run_global, padded - counts, start + counts, n_active[None], u3, lp,
                   n_blocks * MOE_ROWS)

    b1p = b1.reshape(N_EXPERTS, 2 * D_FF // PAIR_COLS, LANES, 2).transpose(0, 1, 3, 2)
    ys = _experts(blk_exp, blk_next, blk_valid, n_active[None], xs, w1,
                  b1p.reshape(N_EXPERTS, 1, 2 * D_FF), w2, b2[:, None, :])
    return _combine(run_n, run_local, run_global, h2, gates, lp, g_final, ys)


def kernel(x, mem, g_mix, w_in, b_qkv, conv_w, conv_b, dt_bias, a_log, d_skip, g_ssm_out, attn_sinks, rel_bias, w_out, b_out, g_cross, g_mem, w_q_cross, w_kv_cross, w_o_cross, g_ffn, w_router, b_router, w1, b1, w2, b2, g_final):
    batch, seq, d = x.shape
    mem_len = mem.shape[1]
    depth = g_mix.shape[0]
    assert depth == 1 and d == D_MODEL and seq % TM_MID == 0
    h = x.reshape(batch * seq, d)
    mem2 = mem.reshape(batch * mem_len, d)
    bias = _bias_table(rel_bias)
    out = _layer(
        h, mem2, batch, seq, mem_len, g_mix[0], w_in[0], b_qkv[0], conv_w[0], conv_b[0], dt_bias[0],
        a_log[0], d_skip[0], g_ssm_out[0], attn_sinks[0], bias, w_out[0], b_out[0], g_cross[0],
        g_mem[0], w_q_cross[0], w_kv_cross[0], w_o_cross[0], g_ffn[0], w_router[0], b_router[0],
        w1[0], b1[0], w2[0], b2[0], g_final[None, :])
    return out.reshape(batch, seq, d)
```
